```python
import jax
import jax.numpy as jnp
from jax import lax
import numpy as np

D_MODEL = 1024
BATCH = 16
SEQ = 2048
DEPTH = 1

GRID_W = 64
CTX_LEN = 256
EPS = 1e-6

MLA_HEADS = 8
QK_NOPE = 64
QK_ROPE = 32
QK_HEAD = QK_NOPE + QK_ROPE
V_HEAD = 64
Q_LORA = 256
KV_LORA = 128
ROPE_BASE = 10000.0
ATTN_SCALE = QK_HEAD ** -0.5
Q_BLOCK = 128
MLA_IN = Q_LORA + KV_LORA + QK_ROPE

RW_HEADS = 8
RW_HEAD = 64
RW_DIM = RW_HEADS * RW_HEAD
DECAY_LORA = 64
AICL_LORA = 64
GATE_LORA = 160
GN_EPS = 64e-5
SHIFT_WIDTH = 3
RW_SPLITS = (RW_DIM, RW_DIM, RW_DIM, DECAY_LORA, DECAY_LORA, AICL_LORA, AICL_LORA, GATE_LORA)
RW_IN = sum(RW_SPLITS)

N_BRANCH = 2
IN_WIDTH = MLA_IN + RW_IN + N_BRANCH * D_MODEL

N_EXPERTS = 256
TOP_K = 8
N_GROUPS = 8
TOPK_GROUPS = 4
EXPERT_FF = 256
ROUTED_SCALE = 2.5
DISPATCH_BLOCK = 128

kernel_name = 'hybrid_mla_rwkv7_moe_dit_layer'


def _split(x, sizes):
    return jnp.split(x, np.cumsum(sizes)[:-1].tolist(), axis=-1)


def rms_norm(x, gain):
    xf = x.astype(jnp.float32)
    y = xf * lax.rsqrt(jnp.mean(xf * xf, axis=-1, keepdims=True) + EPS)
    return (y * gain.astype(jnp.float32)).astype(x.dtype)


def modulate(x, shift, scale):
    return x * (1 + scale) + shift


def centred_conv(x, w):
    T = x.shape[1]
    pad = SHIFT_WIDTH // 2
    xp = jnp.pad(x, ((0, 0), (pad, pad), (0, 0)))
    return sum(xp[:, j:j + T] * w[j] for j in range(SHIFT_WIDTH))


def axial_rope(n_tokens, dtype):
    rows = n_tokens // GRID_W
    row = jnp.repeat(jnp.arange(rows, dtype=jnp.float32), GRID_W)
    col = jnp.tile(jnp.arange(GRID_W, dtype=jnp.float32), rows)
    n_freq = QK_ROPE // 4
    inv_freq = ROPE_BASE ** (-jnp.arange(n_freq, dtype=jnp.float32) / n_freq)
    ang = jnp.concatenate([row[:, None] * inv_freq, col[:, None] * inv_freq], axis=-1)
    return jnp.cos(ang).astype(dtype), jnp.sin(ang).astype(dtype)


def apply_rope(x, cos, sin):
    half = QK_ROPE // 2
    x_nope, x1, x2 = x[..., :QK_NOPE], x[..., QK_NOPE:QK_NOPE + half], x[..., QK_NOPE + half:]
    c, s = cos[:, None, :], sin[:, None, :]
    return jnp.concatenate([x_nope, x1 * c - x2 * s, x1 * s + x2 * c], axis=-1)


def mla_queries(q_lat, p, rope):
    B, T, _ = q_lat.shape
    q = (rms_norm(q_lat, p['q_lat_norm']) @ p['w_uq']).reshape(B, T, MLA_HEADS, QK_HEAD)
    q = rms_norm(q, p['q_norm'])
    return q if rope is None else apply_rope(q, *rope)


def mla_keys_values(kv_lat, k_rope, p, rope):
    B, T, _ = kv_lat.shape
    kv = (rms_norm(kv_lat, p['kv_lat_norm']) @ p['w_ukv']).reshape(B, T, MLA_HEADS, QK_NOPE + V_HEAD)
    k_nope, v = kv[..., :QK_NOPE], kv[..., QK_NOPE:]
    k_shared = jnp.broadcast_to(k_rope[:, :, None, :], (B, T, MLA_HEADS, QK_ROPE))
    k = rms_norm(jnp.concatenate([k_nope, k_shared], axis=-1), p['k_norm'])
    return (k if rope is None else apply_rope(k, *rope)), v


def attend(q, k, v):
    s = jnp.einsum('bqhd,bkhd->bhqk', q, k).astype(jnp.float32) * ATTN_SCALE
    w = jax.nn.softmax(s, axis=-1).astype(v.dtype)
    return jnp.einsum('bhqk,bkhd->bqhd', w, v)


def block_attention(q, k, v):
    B, T, H, Dh = q.shape
    nb = T // Q_BLOCK
    qb = q.reshape(B, nb, Q_BLOCK, H, Dh).swapaxes(0, 1)
    o = lax.map(lambda qblk: attend(qblk, k, v), qb)
    return o.swapaxes(0, 1).reshape(B, T, H * V_HEAD)


def rwkv_operands(rw, p):
    B, T, _ = rw.shape
    heads = lambda z: z.reshape(B, T, RW_HEADS, RW_HEAD)
    r, k, v, lw_f, lw_b, la_f, la_b, lg = _split(rw, RW_SPLITS)
    kk = heads(k * p['k_k']).astype(jnp.float32)
    kk = kk * lax.rsqrt(jnp.sum(kk * kk, axis=-1, keepdims=True) + 1e-12)
    dirs = []
    for d, (lw, la) in enumerate(((lw_f, la_f), (lw_b, la_b))):
        w_log = -jax.nn.softplus(-(p['decay_w0'][d] + jnp.tanh(lw) @ p['decay_w2'][d])) - 0.5
        decay = jnp.exp(-jnp.exp(w_log.astype(jnp.float32)))
        a = jax.nn.sigmoid(p['aicl_a0'][d] + la @ p['aicl_a2'][d])
        k_d = k * (1 + (a - 1) * p['k_a'])
        dirs.append((heads(decay), heads(k_d), heads(a)))
    return heads(r), heads(v), kk, dirs, lg


def wkv7_scan(state0, r, decay, k, v, kk, a, reverse, emit):
    f32 = jnp.float32
    seqs = (decay, k, v, kk, a) + ((r,) if emit else ())
    xs = tuple(jnp.moveaxis(z.astype(f32), 1, 0) for z in seqs)

    def step(S, inp):
        w_t, k_t, v_t, kk_t, a_t = inp[:5]
        sa = jnp.einsum('bhvk,bhk->bhv', S, -kk_t)
        S = (S * w_t[:, :, None, :] + sa[..., None] * (kk_t * a_t)[:, :, None, :]
             + v_t[..., None] * k_t[:, :, None, :])
        return S, (jnp.einsum('bhvk,bhk->bhv', S, inp[5]) if emit else None)

    S, ys = lax.scan(step, state0, xs, reverse=reverse)
    return S, (jnp.moveaxis(ys, 0, 1) if emit else None)


def rwkv_bidirectional(r, v, kk, dirs, init_states, emit):
    ys, states = [], []
    for d, (decay, k_d, a) in enumerate(dirs):
        S, y = wkv7_scan(init_states[d], r, decay, k_d, v, kk, a, d == 1, emit)
        states.append(S)
        ys.append(y)
    return (ys[0] + ys[1] if emit else None), (states[0], states[1])


def rwkv_readout(y, r, v, dirs, lg, p):
    B, T = y.shape[:2]
    mu = jnp.mean(y, axis=-1, keepdims=True)
    var = jnp.mean(jnp.square(y - mu), axis=-1, keepdims=True)
    y_n = ((y - mu) * lax.rsqrt(var + GN_EPS)).reshape(B, T, RW_DIM)
    y_n = y_n * p['gn_w'].astype(jnp.float32) + p['gn_b'].astype(jnp.float32)
    k_sum = dirs[0][1] + dirs[1][1]
    bonus = jnp.sum(r * k_sum * p['r_k'], axis=-1, keepdims=True) * v
    g = jax.nn.sigmoid(lg) @ p['gate_g2']
    return (y_n.astype(v.dtype) + bonus.reshape(B, T, RW_DIM)) * g


def merge_branches(att, rw, gate_logits, p):
    g_att, g_rw = jnp.split(jax.nn.sigmoid(gate_logits), N_BRANCH, axis=-1)
    return (g_att * (att @ p['w_o_mla']) + g_rw * (rw @ p['w_o_rwkv'])) @ p['w_out']


def token_mixers(h_lat, h_ctx, p, update_ctx):
    B, T, _ = h_lat.shape
    C = h_ctx.shape[1]
    in_sizes = (MLA_IN, RW_IN, N_BRANCH * D_MODEL)
    mla_l, rw_l, gate_l = _split(h_lat @ p['w_in'], in_sizes)
    mla_c, rw_c, gate_c = _split(h_ctx @ p['w_in'], in_sizes)
    mla_sizes = (Q_LORA, KV_LORA, QK_ROPE)
    q_lat_l, kv_lat_l, kr_l = _split(mla_l, mla_sizes)
    q_lat_c, kv_lat_c, kr_c = _split(mla_c, mla_sizes)

    rope = axial_rope(T, h_lat.dtype)
    q_l = mla_queries(q_lat_l, p, rope)
    k_l, v_l = mla_keys_values(kv_lat_l, kr_l, p, rope)
    k_c, v_c = mla_keys_values(kv_lat_c, kr_c, p, None)
    att_l = block_attention(q_l, jnp.concatenate([k_l, k_c], axis=1), jnp.concatenate([v_l, v_c], axis=1))

    r_c, v_rc, kk_c, dirs_c, lg_c = rwkv_operands(centred_conv(rw_c, p['shift_conv']), p)
    r_l, v_rl, kk_l, dirs_l, lg_l = rwkv_operands(centred_conv(rw_l, p['shift_conv']), p)
    zero_state = jnp.zeros((B, RW_HEADS, RW_HEAD, RW_HEAD), jnp.float32)
    y_c, ctx_states = rwkv_bidirectional(r_c, v_rc, kk_c, dirs_c, (zero_state, zero_state), update_ctx)
    y_l, _ = rwkv_bidirectional(r_l, v_rl, kk_l, dirs_l, ctx_states, True)
    rw_out_l = rwkv_readout(y_l, r_l, v_rl, dirs_l, lg_l, p)
    mix_l = merge_branches(att_l, rw_out_l, gate_l, p)
    if not update_ctx:
        return mix_l, None
    att_c = attend(mla_queries(q_lat_c, p, None), k_c, v_c).reshape(B, C, MLA_HEADS * V_HEAD)
    rw_out_c = rwkv_readout(y_c, r_c, v_rc, dirs_c, lg_c, p)
    return mix_l, merge_branches(att_c, rw_out_c, gate_c, p)


def moe_ffn(u, p):
    B, T, D = u.shape
    n_tok = B * T
    t = u.reshape(n_tok, D)
    scores = jax.nn.sigmoid(t.astype(jnp.float32) @ p['router_w'].astype(jnp.float32))
    sel = scores + p['router_bias'].astype(jnp.float32)
    per_group = N_EXPERTS // N_GROUPS
    group_score = lax.top_k(sel.reshape(n_tok, N_GROUPS, per_group), 2)[0].sum(-1)
    _, top_groups = lax.top_k(group_score, TOPK_GROUPS)
    group_mask = jnp.any(top_groups[:, :, None] == jnp.arange(N_GROUPS)[None, None, :], axis=1)
    sel = jnp.where(jnp.repeat(group_mask, per_group, axis=1), sel, -jnp.inf)
    _, expert_idx = lax.top_k(sel, TOP_K)
    gate = jnp.take_along_axis(scores, expert_idx, axis=1)
    gate = (gate / jnp.sum(gate, axis=-1, keepdims=True) * ROUTED_SCALE).astype(u.dtype)

    n_assign = n_tok * TOP_K
    flat_e = expert_idx.reshape(-1)
    order = jnp.argsort(flat_e)
    sorted_e = flat_e[order]
    counts = jnp.bincount(flat_e, length=N_EXPERTS)
    padded = (counts + DISPATCH_BLOCK - 1) // DISPATCH_BLOCK * DISPATCH_BLOCK
    padded_end = jnp.cumsum(padded)
    rank = jnp.arange(n_assign) - (jnp.cumsum(counts) - counts)[sorted_e]
    dest = (padded_end - padded)[sorted_e] + rank
    n_blocks = -(-n_assign // DISPATCH_BLOCK) + N_EXPERTS
    slots = n_blocks * DISPATCH_BLOCK
    slot_tok = jnp.full((slots,), n_tok, jnp.int32).at[dest].set((order // TOP_K).astype(jnp.int32))
    slot_w = jnp.zeros((slots,), u.dtype).at[dest].set(gate.reshape(-1)[order])
    block_e = jnp.minimum(jnp.searchsorted(padded_end, jnp.arange(n_blocks) * DISPATCH_BLOCK, side='right'),
                          N_EXPERTS - 1)
    t_pad = jnp.concatenate([t, jnp.zeros((1, D), t.dtype)], axis=0)

    def expert_block(acc, blk):
        tok, wgt, e = blk
        xb = t_pad[tok]
        hb = jax.nn.silu(xb @ p['expert_w1'][e]) * (xb @ p['expert_w3'][e])
        return acc.at[tok].add((hb @ p['expert_w2'][e]) * wgt[:, None]), None

    acc, _ = lax.scan(expert_block, jnp.zeros((n_tok + 1, D), u.dtype),
                      (slot_tok.reshape(n_blocks, DISPATCH_BLOCK), slot_w.reshape(n_blocks, DISPATCH_BLOCK), block_e))
    shared = (jax.nn.silu(t @ p['shared_w1']) * (t @ p['shared_w3'])) @ p['shared_w2']
    return (acc[:n_tok] + shared).reshape(B, T, D)


def hybrid_layer(x, ctx, mod_lat, mod_ctx, p, update_ctx):
    sh_a, sc_a, g_a, sh_m, sc_m, g_m = jnp.split(mod_lat, 6, axis=-1)
    csh_a, csc_a, cg_a, csh_m, csc_m, cg_m = jnp.split(mod_ctx, 6, axis=-1)
    h_lat = modulate(rms_norm(x, p['norm_mix']), sh_a, sc_a)
    h_ctx = modulate(rms_norm(ctx, p['norm_mix']), csh_a, csc_a)
    mix_l, mix_c = token_mixers(h_lat, h_ctx, p, update_ctx)
    x = x + g_a * mix_l
    x = x + g_m * moe_ffn(modulate(rms_norm(x, p['norm_ffn']), sh_m, sc_m), p)
    if update_ctx:
        ctx = ctx + cg_a * mix_c
        ctx = ctx + cg_m * moe_ffn(modulate(rms_norm(ctx, p['norm_ffn']), csh_m, csc_m), p)
    return x, ctx


def setup_inputs(seed: int = 0) -> dict:
    key = jax.random.key(seed)
    ks = iter(jax.random.split(key, 40))
    nrm = lambda shape, scale: jax.random.normal(next(ks), shape, jnp.float32) * scale
    L, D, E, F = DEPTH, D_MODEL, N_EXPERTS, EXPERT_FF
    return {
        'x': nrm((BATCH, SEQ, D), 1.0),
        'c': nrm((BATCH, D), 1.0),
        'ctx': nrm((BATCH, CTX_LEN, D), 1.0),
        'c_ctx': nrm((D,), 1.0),
        'ada_w': nrm((L, D, 6 * D), 0.5 * D ** -0.5),
        'ada_b': nrm((L, 6 * D), 0.01),
        'norm_mix': 1.0 + nrm((L, D), 0.05),
        'norm_ffn': 1.0 + nrm((L, D), 0.05),
        'w_in': nrm((L, D, IN_WIDTH), D ** -0.5),
        'shift_conv': jnp.asarray([0.2, 0.6, 0.2], jnp.float32)[None, :, None] + nrm((L, SHIFT_WIDTH, RW_IN), 0.05),
        'q_lat_norm': 1.0 + nrm((L, Q_LORA), 0.05),
        'w_uq': nrm((L, Q_LORA, MLA_HEADS * QK_HEAD), Q_LORA ** -0.5),
        'kv_lat_norm': 1.0 + nrm((L, KV_LORA), 0.05),
        'w_ukv': nrm((L, KV_LORA, MLA_HEADS * (QK_NOPE + V_HEAD)), KV_LORA ** -0.5),
        'q_norm': 1.0 + nrm((L, QK_HEAD), 0.05),
        'k_norm': 1.0 + nrm((L, QK_HEAD), 0.05),
        'w_o_mla': nrm((L, MLA_HEADS * V_HEAD, D), (MLA_HEADS * V_HEAD) ** -0.5),
        'decay_w0': nrm((L, 2, RW_DIM), 0.5),
        'decay_w2': nrm((L, 2, DECAY_LORA, RW_DIM), 0.5 * DECAY_LORA ** -0.5),
        'aicl_a0': nrm((L, 2, RW_DIM), 0.5),
        'aicl_a2': nrm((L, 2, AICL_LORA, RW_DIM), 0.5 * AICL_LORA ** -0.5),
        'k_k': 0.85 + nrm((L, RW_DIM), 0.05),
        'k_a': 1.0 + nrm((L, RW_DIM), 0.05),
        'r_k': nrm((L, RW_HEADS, RW_HEAD), 0.1),
        'gn_w': 1.0 + nrm((L, RW_DIM), 0.05),
        'gn_b': nrm((L, RW_DIM), 0.01),
        'gate_g2': nrm((L, GATE_LORA, RW_DIM), GATE_LORA ** -0.5),
        'w_o_rwkv': nrm((L, RW_DIM, D), RW_DIM ** -0.5),
        'w_out': nrm((L, D, D), D ** -0.5),
        'router_w': nrm((L, D, E), D ** -0.5),
        'router_bias': nrm((L, E), 0.01),
        'expert_w1': nrm((L, E, D, F), D ** -0.5),
        'expert_w3': nrm((L, E, D, F), D ** -0.5),
        'expert_w2': nrm((L, E, F, D), F ** -0.5),
        'shared_w1': nrm((L, D, F), D ** -0.5),
        'shared_w3': nrm((L, D, F), D ** -0.5),
        'shared_w2': nrm((L, F, D), F ** -0.5),
    }


def reference(x, c, ctx, c_ctx, ada_w, ada_b, norm_mix, norm_ffn, w_in, shift_conv,
              q_lat_norm, w_uq, kv_lat_norm, w_ukv, q_norm, k_norm, w_o_mla,
              decay_w0, decay_w2, aicl_a0, aicl_a2, k_k, k_a, r_k, gn_w, gn_b, gate_g2, w_o_rwkv,
              w_out, router_w, router_bias, expert_w1, expert_w3, expert_w2,
              shared_w1, shared_w3, shared_w2):
    for i in range(DEPTH):
        p = dict(norm_mix=norm_mix[i], norm_ffn=norm_ffn[i], w_in=w_in[i], shift_conv=shift_conv[i],
                 q_lat_norm=q_lat_norm[i], w_uq=w_uq[i], kv_lat_norm=kv_lat_norm[i], w_ukv=w_ukv[i],
                 q_norm=q_norm[i], k_norm=k_norm[i], w_o_mla=w_o_mla[i],
                 decay_w0=decay_w0[i], decay_w2=decay_w2[i], aicl_a0=aicl_a0[i], aicl_a2=aicl_a2[i],
                 k_k=k_k[i], k_a=k_a[i], r_k=r_k[i], gn_w=gn_w[i], gn_b=gn_b[i], gate_g2=gate_g2[i],
                 w_o_rwkv=w_o_rwkv[i], w_out=w_out[i], router_w=router_w[i], router_bias=router_bias[i],
                 expert_w1=expert_w1[i], expert_w3=expert_w3[i], expert_w2=expert_w2[i],
                 shared_w1=shared_w1[i], shared_w3=shared_w3[i], shared_w2=shared_w2[i])
        mod_lat = (jax.nn.silu(c) @ ada_w[i] + ada_b[i])[:, None, :]
        mod_ctx = (jax.nn.silu(c_ctx) @ ada_w[i] + ada_b[i])[None, None, :]
        x, ctx = hybrid_layer(x, ctx, mod_lat, mod_ctx, p, i < DEPTH - 1)
    return x
```

```python
import functools

import jax
import jax.numpy as jnp
import numpy as np
from jax import lax
from jax.experimental import pallas as pl
from jax.experimental.pallas import tpu as pltpu

F32 = jnp.float32
BF16 = jnp.bfloat16

GRID_W = 64
EPS = 1e-6
MLA_HEADS = 8
QK_NOPE = 64
QK_ROPE = 32
QK_HEAD = QK_NOPE + QK_ROPE
V_HEAD = 64
Q_LORA = 256
KV_LORA = 128
ROPE_BASE = 10000.0
ATTN_SCALE = QK_HEAD ** -0.5
RW_HEADS = 8
RW_HEAD = 64
RW_DIM = RW_HEADS * RW_HEAD
DECAY_LORA = 64
AICL_LORA = 64
GATE_LORA = 160
GN_EPS = 64e-5
N_EXPERTS = 256
TOP_K = 8
N_GROUPS = 8
TOPK_GROUPS = 4
ROUTED_SCALE = 2.5

LANES = 128
MXU_DIM = 256
VMEM_LIMIT = 56 * 1024 * 1024

WKV_CHUNK = 64
HEADS_PER_GROUP = MXU_DIM // RW_HEAD
HEAD_PAD = LANES


def _dot(a, b):
    return jnp.dot(a.astype(BF16), b.astype(BF16), preferred_element_type=F32)


def _dot_t(a, b):
    return lax.dot_general(a.astype(BF16), b.astype(BF16), (((1,), (1,)), ((), ())),
                           preferred_element_type=F32)


def _split3(x):
    h = x.astype(BF16)
    r1 = x - h.astype(F32)
    m = r1.astype(BF16)
    lo = (r1 - m.astype(F32)).astype(BF16)
    return h, m, lo


def _dot_hi(a_bf16_exact, x):
    h, m, lo = _split3(x)
    d = lambda y: jnp.dot(a_bf16_exact, y, preferred_element_type=F32)
    return d(h) + d(m) + d(lo)


def _wkv_kernel(r_ref, v_ref, kk_ref, lw_ref, bk_ref, kd_ref, y_ref, s_ref, *, n_ctx_chunks, n_groups):
    d = pl.program_id(0)
    s = pl.program_id(2)
    L = WKV_CHUNK
    G = MXU_DIM

    @pl.when(s == 0)
    def _():
        s_ref[...] = jnp.zeros_like(s_ref)

    sgn = 1 - 2 * d
    row = lax.broadcasted_iota(jnp.int32, (L, G), 0)
    lane = lax.broadcasted_iota(jnp.int32, (L, G), 1)
    col = lane % L
    diff = (row - col) * sgn
    strict = diff > 0
    incl = diff >= 0
    eye = row == col
    r2 = lax.broadcasted_iota(jnp.int32, (L, L), 0)
    c2 = lax.broadcasted_iota(jnp.int32, (L, L), 1)
    tri = jnp.where((r2 - c2) * sgn >= 0, 1.0, 0.0).astype(BF16)
    ones = jnp.ones((L, L), BF16)
    lane_head = lane // RW_HEAD
    rowb = lax.broadcasted_iota(jnp.int32, (G, G), 0) // RW_HEAD
    colb = lax.broadcasted_iota(jnp.int32, (G, G), 1) // RW_HEAD
    same_head = rowb == colb
    eye_g = lax.broadcasted_iota(jnp.int32, (G, G), 0) == lax.broadcasted_iota(jnp.int32, (G, G), 1)

    def stack(x):
        return jnp.concatenate(
            [jnp.where(lane_head == h, x, 0.0) for h in range(HEADS_PER_GROUP)], axis=0).astype(BF16)

    for g in range(n_groups):
        sl = slice(g * G, (g + 1) * G)
        lw = lw_ref[0, 0, :, sl]
        cum_in = _dot_hi(tri, lw)
        tot = _dot_hi(ones, lw)
        g_in = jnp.exp(cum_in)
        g_inv = jnp.exp(-cum_in)
        g_ex = jnp.exp(cum_in - lw)
        g_tot = jnp.exp(tot)
        vv = v_ref[0, :, sl]
        a_h = -kk_ref[0, :, sl] * g_ex
        b_h = bk_ref[0, 0, :, sl] * g_inv
        k_h = kd_ref[0, 0, :, sl] * g_inv
        r_h = r_ref[0, :, sl] * g_in

        gram = _dot_t(jnp.concatenate([a_h, r_h], axis=0),
                      jnp.concatenate([stack(b_h), stack(k_h)], axis=0))
        ab = jnp.where(strict, gram[:L, :G], 0.0)
        ak = jnp.where(strict, gram[:L, G:], 0.0)
        rb = jnp.where(incl, gram[L:, :G], 0.0)
        rk = jnp.where(incl, gram[L:, G:], 0.0)

        tm = jnp.where(eye, 1.0, 0.0) + ab
        p = ab
        n_sq = int(np.log2(L)) - 1
        for _ in range(n_sq):
            p = _dot(p, stack(p))
            tm = tm + _dot(tm, stack(p))

        v_st = stack(vv)
        akv = _dot(ak, v_st)
        wu = _dot(tm, jnp.concatenate([stack(a_h), stack(akv)], axis=1))
        w_t, u_t = wu[:, :G], wu[:, G:]
        rbwu = _dot(rb, jnp.concatenate([stack(w_t), stack(u_t)], axis=1))
        r_t = r_h + rbwu[:, :G]
        y_t = rbwu[:, G:] + _dot(rk, v_st)

        s0 = s_ref[g]
        s0_st = stack(s0)
        y = _dot_t(r_t, s0_st) + y_t
        y_ref[0, 0, :, sl] = y

        u = _dot_t(w_t, s0_st) + u_t
        uv = jnp.concatenate([u, vv], axis=0)
        bkc = jnp.concatenate([b_h, k_h], axis=0)
        upd = lax.dot_general(uv.astype(BF16), bkc.astype(BF16), (((0,), (0,)), ((), ())),
                              preferred_element_type=F32)
        upd = jnp.where(same_head, upd, 0.0)
        upd_d = upd[0:L]
        for h in range(1, HEADS_PER_GROUP):
            upd_d = upd_d + upd[h * L:(h + 1) * L]
        s_ref[g] = (s0 + upd_d) * g_tot


def _wkv_scan(r, v, kk, lw, bk, kd, n_ctx):
    B, Ttot, C = r.shape
    L = WKV_CHUNK
    nc = Ttot // L
    ncc = n_ctx // L
    nl = nc - ncc
    n_groups = C // MXU_DIM

    def cid(d, s):
        fwd = s
        bwd = jnp.where(s < ncc, ncc - 1 - s, nc + ncc - 1 - s)
        return jnp.where(d == 0, fwd, bwd)

    def shared_map(d, b, s):
        return (b, cid(d, s), 0)

    def dir_map(d, b, s):
        return (d, b, cid(d, s), 0)

    def out_map(d, b, s):
        c = jnp.clip(cid(d, s) - ncc, 0, nl - 1)
        c = jnp.where(s < ncc, jnp.where(d == 0, 0, nl - 1), c)
        return (d, b, c, 0)

    shared = pl.BlockSpec((1, L, C), shared_map)
    per_dir = pl.BlockSpec((1, 1, L, C), dir_map)
    return pl.pallas_call(
        functools.partial(_wkv_kernel, n_ctx_chunks=ncc, n_groups=n_groups),
        grid=(2, B, nc),
        in_specs=[shared, shared, shared, per_dir, per_dir, per_dir],
        out_specs=pl.BlockSpec((1, 1, L, C), out_map),
        out_shape=jax.ShapeDtypeStruct((2, B, nl * L, C), F32),
        scratch_shapes=[pltpu.VMEM((n_groups, L, MXU_DIM), F32)],
        compiler_params=pltpu.CompilerParams(
            dimension_semantics=("arbitrary", "arbitrary", "arbitrary"),
            vmem_limit_bytes=VMEM_LIMIT),
        name="wkv_scan",
    )(r, v, kk, lw, bk, kd)


def kernel(x, c, ctx, c_ctx, ada_w, ada_b, norm_mix, norm_ffn, w_in, shift_conv, q_lat_norm, w_uq, kv_lat_norm, w_ukv, q_norm, k_norm, w_o_mla, decay_w0, decay_w2, aicl_a0, aicl_a2, k_k, k_a, r_k, gn_w, gn_b, gate_g2, w_o_rwkv, w_out, router_w, router_bias, expert_w1, expert_w3, expert_w2, shared_w1, shared_w3, shared_w2):
    B, T, D = x.shape
    n_ctx = ctx.shape[1]
    i0 = 0
    p = dict(norm_mix=norm_mix[i0], norm_ffn=norm_ffn[i0], w_in=w_in[i0], shift_conv=shift_conv[i0],
             q_lat_norm=q_lat_norm[i0], w_uq=w_uq[i0], kv_lat_norm=kv_lat_norm[i0], w_ukv=w_ukv[i0],
             q_norm=q_norm[i0], k_norm=k_norm[i0], w_o_mla=w_o_mla[i0],
             decay_w0=decay_w0[i0], decay_w2=decay_w2[i0], aicl_a0=aicl_a0[i0], aicl_a2=aicl_a2[i0],
             k_k=k_k[i0], k_a=k_a[i0], r_k=r_k[i0], gn_w=gn_w[i0], gn_b=gn_b[i0], gate_g2=gate_g2[i0],
             w_o_rwkv=w_o_rwkv[i0], w_out=w_out[i0], router_w=router_w[i0], router_bias=router_bias[i0],
             expert_w1=expert_w1[i0], expert_w3=expert_w3[i0], expert_w2=expert_w2[i0],
             shared_w1=shared_w1[i0], shared_w3=shared_w3[i0], shared_w2=shared_w2[i0])

    mod_lat = jax.nn.silu(c) @ ada_w[i0] + ada_b[i0]
    mod_ctx = jax.nn.silu(c_ctx) @ ada_w[i0] + ada_b[i0]
    sh_a, sc_a, g_a, sh_m, sc_m, g_m = jnp.split(mod_lat, 6, axis=-1)
    csh_a, csc_a = jnp.split(mod_ctx, 6, axis=-1)[:2]
    sh2 = jnp.stack([jnp.broadcast_to(csh_a, (B, D)), sh_a], axis=1)
    sc2 = jnp.stack([jnp.broadcast_to(csc_a, (B, D)), sc_a], axis=1)

    proj = _in_proj(ctx, x, sh2, sc2, p['norm_mix'], _pack_w_in(p['w_in']))

    q, k, v = _mla_prep(proj, p, n_ctx, T)
    att = _attention(q, k, v)

    r, vv, kk, lw, bk, kd, lg = _rwkv_prep(proj, p, n_ctx)
    y2 = _wkv_scan(r, vv, kk, lw, bk, kd, n_ctx)
    rw_out = _rwkv_readout(y2[0] + y2[1], r[:, n_ctx:], vv[:, n_ctx:], kd[:, :, n_ctx:], lg[:, n_ctx:], p)

    x1 = _merge(att, rw_out, proj, x, g_a, p, n_ctx)
    return _moe(x1, sh_m, sc_m, g_m, p)


ROW_TILE = 256
PROJ_RW = 2048
PROJ_GATE = 2048
PROJ_MLA = 512
PROJ_W = PROJ_RW + PROJ_GATE + PROJ_MLA
MLA_IN = Q_LORA + KV_LORA + QK_ROPE
RW_SPLITS = (RW_DIM, RW_DIM, RW_DIM, DECAY_LORA, DECAY_LORA, AICL_LORA, AICL_LORA, GATE_LORA)
RW_IN = sum(RW_SPLITS)


def _pack_w_in(w_in):
    w_mla = w_in[:, :MLA_IN]
    w_rw = w_in[:, MLA_IN:MLA_IN + RW_IN]
    w_gate = w_in[:, MLA_IN + RW_IN:]
    pad = lambda w, n: jnp.pad(w, ((0, 0), (0, n - w.shape[1])))
    return jnp.concatenate([pad(w_rw, PROJ_RW), w_gate, pad(w_mla, PROJ_MLA)], axis=1).astype(BF16)


def _in_proj_kernel(ctx_ref, x_ref, sh_ref, sc_ref, gain_ref, w_ref, o_ref):
    is_ctx = pl.program_id(1) == 0
    xt = jnp.where(is_ctx, ctx_ref[0], x_ref[0])
    sh = jnp.where(is_ctx, sh_ref[0, 0:1], sh_ref[0, 1:2])
    sc = jnp.where(is_ctx, sc_ref[0, 0:1], sc_ref[0, 1:2])
    y = xt * lax.rsqrt(jnp.mean(xt * xt, axis=-1, keepdims=True) + EPS) * gain_ref[...]
    h = y * (1.0 + sc) + sh
    o_ref[0] = _dot(h, w_ref[...])


def _in_proj(ctx, x, sh2, sc2, gain, w):
    B, T, D = x.shape
    n_ctx = ctx.shape[1]
    assert n_ctx == ROW_TILE and T % ROW_TILE == 0
    nt = 1 + T // ROW_TILE
    return pl.pallas_call(
        _in_proj_kernel,
        grid=(B, nt),
        in_specs=[
            pl.BlockSpec((1, ROW_TILE, D), lambda b, i: (b, 0, 0)),
            pl.BlockSpec((1, ROW_TILE, D), lambda b, i: (b, jnp.maximum(i - 1, 0), 0)),
            pl.BlockSpec((1, 2, D), lambda b, i: (b, 0, 0)),
            pl.BlockSpec((1, 2, D), lambda b, i: (b, 0, 0)),
            pl.BlockSpec((1, D), lambda b, i: (0, 0)),
            pl.BlockSpec((D, PROJ_W), lambda b, i: (0, 0)),
        ],
        out_specs=pl.BlockSpec((1, ROW_TILE, PROJ_W), lambda b, i: (b, i, 0)),
        out_shape=jax.ShapeDtypeStruct((B, n_ctx + T, PROJ_W), F32),
        compiler_params=pltpu.CompilerParams(
            dimension_semantics=("arbitrary", "arbitrary"), vmem_limit_bytes=VMEM_LIMIT),
        name="in_proj",
    )(ctx, x, sh2, sc2, gain.reshape(1, D), w)


def _rms(x, gain):
    return x * lax.rsqrt(jnp.mean(x * x, axis=-1, keepdims=True) + EPS) * gain


def _rope_tables(n_tokens):
    rows = n_tokens // GRID_W
    row = jnp.repeat(jnp.arange(rows, dtype=F32), GRID_W)
    col = jnp.tile(jnp.arange(GRID_W, dtype=F32), rows)
    n_freq = QK_ROPE // 4
    inv_freq = ROPE_BASE ** (-jnp.arange(n_freq, dtype=F32) / n_freq)
    ang = jnp.concatenate([row[:, None] * inv_freq, col[:, None] * inv_freq], axis=-1)
    return jnp.cos(ang), jnp.sin(ang)


def _rope(x, cos, sin):
    half = QK_ROPE // 2
    x_nope, x1, x2 = x[..., :QK_NOPE], x[..., QK_NOPE:QK_NOPE + half], x[..., QK_NOPE + half:]
    c, s = cos[:, None, :], sin[:, None, :]
    return jnp.concatenate([x_nope, x1 * c - x2 * s, x1 * s + x2 * c], axis=-1)


def _mla_prep(proj, p, n_ctx, T):
    B = proj.shape[0]
    H = MLA_HEADS
    mla = proj[:, :, PROJ_RW + PROJ_GATE:PROJ_RW + PROJ_GATE + MLA_IN]
    q_lat = mla[:, n_ctx:, :Q_LORA]
    kv_lat = mla[:, :, Q_LORA:Q_LORA + KV_LORA]
    kr = mla[:, :, Q_LORA + KV_LORA:]
    cos, sin = _rope_tables(T)
    q = (_rms(q_lat, p['q_lat_norm']) @ p['w_uq']).reshape(B, T, H, QK_HEAD)
    q = _rope(_rms(q, p['q_norm']), cos, sin) * ATTN_SCALE
    kv = (_rms(kv_lat, p['kv_lat_norm']) @ p['w_ukv']).reshape(B, n_ctx + T, H, QK_NOPE + V_HEAD)
    k_nope, v = kv[..., :QK_NOPE], kv[..., QK_NOPE:]
    k_sh = jnp.broadcast_to(kr[:, :, None, :], (B, n_ctx + T, H, QK_ROPE))
    k = _rms(jnp.concatenate([k_nope, k_sh], axis=-1), p['k_norm'])
    k = jnp.concatenate([k[:, :n_ctx], _rope(k[:, n_ctx:], cos, sin)], axis=1)
    pad = lambda z: jnp.pad(z, ((0, 0), (0, 0), (0, 0), (0, HEAD_PAD - QK_HEAD)))
    q = pad(q).reshape(B, T, H * HEAD_PAD).astype(BF16)
    k = pad(k).reshape(B, n_ctx + T, H * HEAD_PAD).astype(BF16)
    v = v.reshape(B, n_ctx + T, H * V_HEAD).astype(BF16)
    return q, k, v


ATTN_Q_TILE = 512
HEADS_PER_STEP = LANES // V_HEAD


def _attn_kernel(q_ref, k_ref, v_ref, o_ref):
    v2 = v_ref[0]
    outs = []
    for hh in range(HEADS_PER_STEP):
        sl = slice(hh * HEAD_PAD, (hh + 1) * HEAD_PAD)
        s = _dot_t(q_ref[0, :, sl], k_ref[0, :, sl])
        m = jnp.max(s, axis=-1, keepdims=True)
        e = jnp.exp(s - m)
        l = jnp.sum(e, axis=-1, keepdims=True)
        outs.append(_dot(e, v2) / l)
    lane = lax.broadcasted_iota(jnp.int32, outs[0].shape, 1)
    o_ref[0] = jnp.where(lane < V_HEAD, outs[0], outs[1])


def _attention(q, k, v):
    B, T, _ = q.shape
    Kt = k.shape[1]
    hp = MLA_HEADS // HEADS_PER_STEP
    qw = HEADS_PER_STEP * HEAD_PAD
    return pl.pallas_call(
        _attn_kernel,
        grid=(B, hp, T // ATTN_Q_TILE),
        in_specs=[
            pl.BlockSpec((1, ATTN_Q_TILE, qw), lambda b, h, i: (b, i, h)),
            pl.BlockSpec((1, Kt, qw), lambda b, h, i: (b, 0, h)),
            pl.BlockSpec((1, Kt, LANES), lambda b, h, i: (b, 0, h)),
        ],
        out_specs=pl.BlockSpec((1, ATTN_Q_TILE, LANES), lambda b, h, i: (b, i, h)),
        out_shape=jax.ShapeDtypeStruct((B, T, MLA_HEADS * V_HEAD), F32),
        compiler_params=pltpu.CompilerParams(
            dimension_semantics=("arbitrary", "arbitrary", "arbitrary"), vmem_limit_bytes=VMEM_LIMIT),
        name="attention",
    )(q, k, v)


def _conv3(x, w):
    xp = jnp.pad(x, ((0, 0), (1, 1), (0, 0)))
    T = x.shape[1]
    return xp[:, 0:T] * w[0] + xp[:, 1:T + 1] * w[1] + xp[:, 2:T + 2] * w[2]


def _rwkv_prep(proj, p, n_ctx):
    B, Tt, _ = proj.shape
    rw = proj[:, :, :RW_IN]
    rw = jnp.concatenate([_conv3(rw[:, :n_ctx], p['shift_conv']), _conv3(rw[:, n_ctx:], p['shift_conv'])], axis=1)
    r, k, v, lw_f, lw_b, la_f, la_b, lg = jnp.split(rw, np.cumsum(RW_SPLITS)[:-1].tolist(), axis=-1)
    heads = lambda z: z.reshape(B, Tt, RW_HEADS, RW_HEAD)
    kk = heads(k * p['k_k'])
    kk = (kk * lax.rsqrt(jnp.sum(kk * kk, axis=-1, keepdims=True) + 1e-12)).reshape(B, Tt, RW_DIM)
    lws, bks, kds = [], [], []
    for d, (lw, la) in enumerate(((lw_f, la_f), (lw_b, la_b))):
        w_log = -jax.nn.softplus(-(p['decay_w0'][d] + jnp.tanh(lw) @ p['decay_w2'][d])) - 0.5
        a = jax.nn.sigmoid(p['aicl_a0'][d] + la @ p['aicl_a2'][d])
        lws.append(-jnp.exp(w_log))
        bks.append(kk * a)
        kds.append(k * (1 + (a - 1) * p['k_a']))
    return r, v, kk, jnp.stack(lws), jnp.stack(bks), jnp.stack(kds), lg


def _rwkv_readout(y, r, v, kd, lg, p):
    B, T, _ = y.shape
    heads = lambda z: z.reshape(B, T, RW_HEADS, RW_HEAD)
    yh = heads(y)
    mu = jnp.mean(yh, axis=-1, keepdims=True)
    var = jnp.mean(jnp.square(yh - mu), axis=-1, keepdims=True)
    y_n = ((yh - mu) * lax.rsqrt(var + GN_EPS)).reshape(B, T, RW_DIM) * p['gn_w'] + p['gn_b']
    k_sum = heads(kd[0] + kd[1])
    bonus = jnp.sum(heads(r) * k_sum * p['r_k'], axis=-1, keepdims=True) * heads(v)
    g = jax.nn.sigmoid(lg) @ p['gate_g2']
    return (y_n + bonus.reshape(B, T, RW_DIM)) * g


def _merge_kernel(att_ref, rw_ref, gate_ref, x_ref, ga_ref, wm_ref, wr_ref, wo_ref, o_ref):
    D = x_ref.shape[-1]
    g = jax.nn.sigmoid(gate_ref[0])
    mix = g[:, :D] * _dot(att_ref[0], wm_ref[...]) + g[:, D:] * _dot(rw_ref[0], wr_ref[...])
    o_ref[0] = x_ref[0] + ga_ref[0] * _dot(mix, wo_ref[...])


def _merge(att, rw_out, proj, x, g_a, p, n_ctx):
    B, T, D = x.shape
    ctx_tiles = n_ctx // ROW_TILE
    gate_blk = PROJ_RW // PROJ_GATE
    row = lambda b, i: (b, i, 0)
    const = lambda b, i: (0, 0)
    return pl.pallas_call(
        _merge_kernel,
        grid=(B, T // ROW_TILE),
        in_specs=[
            pl.BlockSpec((1, ROW_TILE, att.shape[-1]), row),
            pl.BlockSpec((1, ROW_TILE, rw_out.shape[-1]), row),
            pl.BlockSpec((1, ROW_TILE, PROJ_GATE), lambda b, i: (b, i + ctx_tiles, gate_blk)),
            pl.BlockSpec((1, ROW_TILE, D), row),
            pl.BlockSpec((1, 1, D), lambda b, i: (b, 0, 0)),
            pl.BlockSpec(p['w_o_mla'].shape, const),
            pl.BlockSpec(p['w_o_rwkv'].shape, const),
            pl.BlockSpec(p['w_out'].shape, const),
        ],
        out_specs=pl.BlockSpec((1, ROW_TILE, D), row),
        out_shape=jax.ShapeDtypeStruct((B, T, D), F32),
        compiler_params=pltpu.CompilerParams(
            dimension_semantics=("arbitrary", "arbitrary"), vmem_limit_bytes=VMEM_LIMIT),
        name="merge",
    )(att, rw_out, proj, x, g_a[:, None, :], p['w_o_mla'].astype(BF16), p['w_o_rwkv'].astype(BF16),
      p['w_out'].astype(BF16))


MOE_BLOCK = 256


def _moe_ffn_kernel(be_ref, nu_ref, x_ref, w1_ref, w3_ref, w2_ref, o_ref):
    j = pl.program_id(0)

    @pl.when(j < nu_ref[0])
    def _():
        xb = x_ref[...]
        h1 = _dot(xb, w1_ref[0])
        h3 = _dot(xb, w3_ref[0])
        o_ref[...] = _dot(h1 * jax.nn.sigmoid(h1) * h3, w2_ref[0]).astype(o_ref.dtype)

    @pl.when(j >= nu_ref[0])
    def _():
        o_ref[...] = jnp.zeros_like(o_ref)


def _moe_ffn(block_e, n_used, x_slots, w1, w3, w2):
    slots, D = x_slots.shape
    E, _, F = w1.shape
    nblk = slots // MOE_BLOCK
    grid_spec = pltpu.PrefetchScalarGridSpec(
        num_scalar_prefetch=2,
        grid=(nblk,),
        in_specs=[
            pl.BlockSpec((MOE_BLOCK, D), lambda j, be, nu: (jnp.minimum(j, nu[0] - 1), 0)),
            pl.BlockSpec((1, D, F), lambda j, be, nu: (be[j], 0, 0)),
            pl.BlockSpec((1, D, F), lambda j, be, nu: (be[j], 0, 0)),
            pl.BlockSpec((1, F, D), lambda j, be, nu: (be[j], 0, 0)),
        ],
        out_specs=pl.BlockSpec((MOE_BLOCK, D), lambda j, be, nu: (j, 0)),
    )
    return pl.pallas_call(
        _moe_ffn_kernel,
        grid_spec=grid_spec,
        out_shape=jax.ShapeDtypeStruct((slots, D), F32),
        compiler_params=pltpu.CompilerParams(
            dimension_semantics=("arbitrary",), vmem_limit_bytes=VMEM_LIMIT),
        name="moe_ffn",
    )(block_e, n_used, x_slots, w1, w3, w2)


def _shared_kernel(u_ref, routed_ref, x_ref, gm_ref, w1_ref, w3_ref, w2_ref, o_ref):
    u = u_ref[0]
    h1 = _dot(u, w1_ref[...])
    h3 = _dot(u, w3_ref[...])
    shared = _dot(h1 * jax.nn.sigmoid(h1) * h3, w2_ref[...])
    o_ref[0] = x_ref[0] + gm_ref[0] * (routed_ref[0] + shared)


def _shared_combine(u, routed, x1, g_m, p):
    B, T, D = x1.shape
    row = lambda b, i: (b, i, 0)
    const = lambda b, i: (0, 0)
    return pl.pallas_call(
        _shared_kernel,
        grid=(B, T // ROW_TILE),
        in_specs=[
            pl.BlockSpec((1, ROW_TILE, D), row),
            pl.BlockSpec((1, ROW_TILE, D), row),
            pl.BlockSpec((1, ROW_TILE, D), row),
            pl.BlockSpec((1, 1, D), lambda b, i: (b, 0, 0)),
            pl.BlockSpec(p['shared_w1'].shape, const),
            pl.BlockSpec(p['shared_w3'].shape, const),
            pl.BlockSpec(p['shared_w2'].shape, const),
        ],
        out_specs=pl.BlockSpec((1, ROW_TILE, D), row),
        out_shape=jax.ShapeDtypeStruct((B, T, D), F32),
        compiler_params=pltpu.CompilerParams(
            dimension_semantics=("arbitrary", "arbitrary"), vmem_limit_bytes=VMEM_LIMIT),
        name="shared_combine",
    )(u, routed, x1, g_m[:, None, :], p['shared_w1'].astype(BF16), p['shared_w3'].astype(BF16),
      p['shared_w2'].astype(BF16))


def _moe(x1, sh_m, sc_m, g_m, p):
    B, T, D = x1.shape
    n_tok = B * T
    u = _rms(x1, p['norm_ffn']) * (1 + sc_m[:, None, :]) + sh_m[:, None, :]
    t = u.reshape(n_tok, D)
    scores = jax.nn.sigmoid(jnp.dot(t, p['router_w'], precision=lax.Precision.HIGHEST))
    sel = scores + p['router_bias']
    per_group = N_EXPERTS // N_GROUPS
    group_score = lax.top_k(sel.reshape(n_tok, N_GROUPS, per_group), 2)[0].sum(-1)
    _, top_groups = lax.top_k(group_score, TOPK_GROUPS)
    group_mask = jnp.any(top_groups[:, :, None] == jnp.arange(N_GROUPS)[None, None, :], axis=1)
    sel = jnp.where(jnp.repeat(group_mask, per_group, axis=1), sel, -jnp.inf)
    _, expert_idx = lax.top_k(sel, TOP_K)
    gate = jnp.take_along_axis(scores, expert_idx, axis=1)
    gate = gate / jnp.sum(gate, axis=-1, keepdims=True) * ROUTED_SCALE

    n_assign = n_tok * TOP_K
    flat_e = expert_idx.reshape(-1)
    order = jnp.argsort(flat_e)
    sorted_e = flat_e[order]
    counts = jnp.bincount(flat_e, length=N_EXPERTS)
    padded = (counts + MOE_BLOCK - 1) // MOE_BLOCK * MOE_BLOCK
    padded_end = jnp.cumsum(padded)
    rank = jnp.arange(n_assign) - (jnp.cumsum(counts) - counts)[sorted_e]
    dest = ((padded_end - padded)[sorted_e] + rank).astype(jnp.int32)
    n_blocks = n_assign // MOE_BLOCK + N_EXPERTS
    slots = n_blocks * MOE_BLOCK
    slot_tok = jnp.zeros((slots,), jnp.int32).at[dest].set((order // TOP_K).astype(jnp.int32))
    n_used = (padded_end[-1] // MOE_BLOCK).astype(jnp.int32)
    blk = jnp.minimum(jnp.arange(n_blocks), n_used - 1) * MOE_BLOCK
    block_e = jnp.minimum(jnp.searchsorted(padded_end, blk, side='right'), N_EXPERTS - 1).astype(jnp.int32)
    x_slots = t.astype(BF16)[slot_tok]

    y_slots = _moe_ffn(block_e, n_used.reshape(1), x_slots, p['expert_w1'].astype(BF16),
                       p['expert_w3'].astype(BF16), p['expert_w2'].astype(BF16))
    dest_pair = jnp.zeros((n_assign,), jnp.int32).at[order].set(dest).reshape(n_tok, TOP_K)
    routed = jnp.sum(y_slots[dest_pair] * gate[:, :, None], axis=1)
    return _shared_combine(u, routed.reshape(B, T, D), x1, g_m, p)
```

```python
import functools

import jax
import jax.numpy as jnp
import numpy as np
from jax import lax
from jax.experimental import pallas as pl
from jax.experimental.pallas import tpu as pltpu

F32 = jnp.float32
BF16 = jnp.bfloat16

GRID_W = 64
EPS = 1e-6
MLA_HEADS = 8
QK_NOPE = 64
QK_ROPE = 32
QK_HEAD = QK_NOPE + QK_ROPE
V_HEAD = 64
Q_LORA = 256
KV_LORA = 128
ROPE_BASE = 10000.0
ATTN_SCALE = QK_HEAD ** -0.5
RW_HEADS = 8
RW_HEAD = 64
RW_DIM = RW_HEADS * RW_HEAD
DECAY_LORA = 64
AICL_LORA = 64
GATE_LORA = 160
GN_EPS = 64e-5
N_EXPERTS = 256
TOP_K = 8
N_GROUPS = 8
TOPK_GROUPS = 4
ROUTED_SCALE = 2.5

LANES = 128
MXU_DIM = 256
VMEM_LIMIT = 56 * 1024 * 1024

WKV_CHUNK = 64
HEADS_PER_GROUP = MXU_DIM // RW_HEAD
HEAD_PAD = LANES


def _dot(a, b):
    return jnp.dot(a.astype(BF16), b.astype(BF16), preferred_element_type=F32)


def _dot_t(a, b):
    return lax.dot_general(a.astype(BF16), b.astype(BF16), (((1,), (1,)), ((), ())),
                           preferred_element_type=F32)


def _split3(x):
    h = x.astype(BF16)
    r1 = x - h.astype(F32)
    m = r1.astype(BF16)
    lo = (r1 - m.astype(F32)).astype(BF16)
    return h, m, lo


def _dot_hi(a_bf16_exact, x):
    h, m, lo = _split3(x)
    d = lambda y: jnp.dot(a_bf16_exact, y, preferred_element_type=F32)
    return d(h) + d(m) + d(lo)


def _wkv_kernel(r_ref, v_ref, kk_ref, lw_ref, bk_ref, kd_ref, y_ref, s_ref, *, n_ctx_chunks, n_groups):
    d = pl.program_id(0)
    s = pl.program_id(2)
    L = WKV_CHUNK
    G = MXU_DIM

    @pl.when(s == 0)
    def _():
        s_ref[...] = jnp.zeros_like(s_ref)

    sgn = 1 - 2 * d
    row = lax.broadcasted_iota(jnp.int32, (L, G), 0)
    lane = lax.broadcasted_iota(jnp.int32, (L, G), 1)
    col = lane % L
    diff = (row - col) * sgn
    strict = diff > 0
    incl = diff >= 0
    eye = row == col
    r2 = lax.broadcasted_iota(jnp.int32, (L, L), 0)
    c2 = lax.broadcasted_iota(jnp.int32, (L, L), 1)
    tri = jnp.where((r2 - c2) * sgn >= 0, 1.0, 0.0).astype(BF16)
    ones = jnp.ones((L, L), BF16)
    lane_head = lane // RW_HEAD
    rowb = lax.broadcasted_iota(jnp.int32, (G, G), 0) // RW_HEAD
    colb = lax.broadcasted_iota(jnp.int32, (G, G), 1) // RW_HEAD
    same_head = rowb == colb
    eye_g = lax.broadcasted_iota(jnp.int32, (G, G), 0) == lax.broadcasted_iota(jnp.int32, (G, G), 1)

    def stack(x):
        return jnp.concatenate(
            [jnp.where(lane_head == h, x, 0.0) for h in range(HEADS_PER_GROUP)], axis=0).astype(BF16)

    for g in range(n_groups):
        sl = slice(g * G, (g + 1) * G)
        lw = lw_ref[0, 0, :, sl]
        cum_in = _dot_hi(tri, lw)
        tot = _dot_hi(ones, lw)
        g_in = jnp.exp(cum_in)
        g_inv = jnp.exp(-cum_in)
        g_ex = jnp.exp(cum_in - lw)
        g_tot = jnp.exp(tot)
        vv = v_ref[0, :, sl]
        a_h = -kk_ref[0, :, sl] * g_ex
        b_h = bk_ref[0, 0, :, sl] * g_inv
        k_h = kd_ref[0, 0, :, sl] * g_inv
        r_h = r_ref[0, :, sl] * g_in

        gram = _dot_t(jnp.concatenate([a_h, r_h], axis=0),
                      jnp.concatenate([stack(b_h), stack(k_h)], axis=0))
        ab = jnp.where(strict, gram[:L, :G], 0.0)
        ak = jnp.where(strict, gram[:L, G:], 0.0)
        rb = jnp.where(incl, gram[L:, :G], 0.0)
        rk = jnp.where(incl, gram[L:, G:], 0.0)

        tm = jnp.where(eye, 1.0, 0.0) + ab
        p = ab
        n_sq = int(np.log2(L)) - 1
        for _ in range(n_sq):
            p = _dot(p, stack(p))
            tm = tm + _dot(tm, stack(p))

        v_st = stack(vv)
        akv = _dot(ak, v_st)
        wu = _dot(tm, jnp.concatenate([stack(a_h), stack(akv)], axis=1))
        w_t, u_t = wu[:, :G], wu[:, G:]
        rbwu = _dot(rb, jnp.concatenate([stack(w_t), stack(u_t)], axis=1))
        r_t = r_h + rbwu[:, :G]
        y_t = rbwu[:, G:] + _dot(rk, v_st)

        s0 = s_ref[g]
        s0_st = stack(s0)
        y = _dot_t(r_t, s0_st) + y_t
        y_ref[0, 0, :, sl] = y

        u = _dot_t(w_t, s0_st) + u_t
        uv = jnp.concatenate([u, vv], axis=0)
        bkc = jnp.concatenate([b_h, k_h], axis=0)
        upd = lax.dot_general(uv.astype(BF16), bkc.astype(BF16), (((0,), (0,)), ((), ())),
                              preferred_element_type=F32)
        upd = jnp.where(same_head, upd, 0.0)
        upd_d = upd[0:L]
        for h in range(1, HEADS_PER_GROUP):
            upd_d = upd_d + upd[h * L:(h + 1) * L]
        s_ref[g] = (s0 + upd_d) * g_tot


def _wkv_scan(r, v, kk, lw, bk, kd, n_ctx):
    B, Ttot, C = r.shape
    L = WKV_CHUNK
    nc = Ttot // L
    ncc = n_ctx // L
    nl = nc - ncc
    n_groups = C // MXU_DIM

    def cid(d, s):
        fwd = s
        bwd = jnp.where(s < ncc, ncc - 1 - s, nc + ncc - 1 - s)
        return jnp.where(d == 0, fwd, bwd)

    def shared_map(d, b, s):
        return (b, cid(d, s), 0)

    def dir_map(d, b, s):
        return (d, b, cid(d, s), 0)

    def out_map(d, b, s):
        c = jnp.clip(cid(d, s) - ncc, 0, nl - 1)
        c = jnp.where(s < ncc, jnp.where(d == 0, 0, nl - 1), c)
        return (d, b, c, 0)

    shared = pl.BlockSpec((1, L, C), shared_map)
    per_dir = pl.BlockSpec((1, 1, L, C), dir_map)
    return pl.pallas_call(
        functools.partial(_wkv_kernel, n_ctx_chunks=ncc, n_groups=n_groups),
        grid=(2, B, nc),
        in_specs=[shared, shared, shared, per_dir, per_dir, per_dir],
        out_specs=pl.BlockSpec((1, 1, L, C), out_map),
        out_shape=jax.ShapeDtypeStruct((2, B, nl * L, C), F32),
        scratch_shapes=[pltpu.VMEM((n_groups, L, MXU_DIM), F32)],
        compiler_params=pltpu.CompilerParams(
            dimension_semantics=("arbitrary", "arbitrary", "arbitrary"),
            vmem_limit_bytes=VMEM_LIMIT),
        name="wkv_scan",
    )(r, v, kk, lw, bk, kd)


def kernel(x, c, ctx, c_ctx, ada_w, ada_b, norm_mix, norm_ffn, w_in, shift_conv, q_lat_norm, w_uq, kv_lat_norm, w_ukv, q_norm, k_norm, w_o_mla, decay_w0, decay_w2, aicl_a0, aicl_a2, k_k, k_a, r_k, gn_w, gn_b, gate_g2, w_o_rwkv, w_out, router_w, router_bias, expert_w1, expert_w3, expert_w2, shared_w1, shared_w3, shared_w2):
    B, T, D = x.shape
    n_ctx = ctx.shape[1]
    i0 = 0
    p = dict(norm_mix=norm_mix[i0], norm_ffn=norm_ffn[i0], w_in=w_in[i0], shift_conv=shift_conv[i0],
             q_lat_norm=q_lat_norm[i0], w_uq=w_uq[i0], kv_lat_norm=kv_lat_norm[i0], w_ukv=w_ukv[i0],
             q_norm=q_norm[i0], k_norm=k_norm[i0], w_o_mla=w_o_mla[i0],
             decay_w0=decay_w0[i0], decay_w2=decay_w2[i0], aicl_a0=aicl_a0[i0], aicl_a2=aicl_a2[i0],
             k_k=k_k[i0], k_a=k_a[i0], r_k=r_k[i0], gn_w=gn_w[i0], gn_b=gn_b[i0], gate_g2=gate_g2[i0],
             w_o_rwkv=w_o_rwkv[i0], w_out=w_out[i0], router_w=router_w[i0], router_bias=router_bias[i0],
             expert_w1=expert_w1[i0], expert_w3=expert_w3[i0], expert_w2=expert_w2[i0],
             shared_w1=shared_w1[i0], shared_w3=shared_w3[i0], shared_w2=shared_w2[i0])

    mod_lat = jax.nn.silu(c) @ ada_w[i0] + ada_b[i0]
    mod_ctx = jax.nn.silu(c_ctx) @ ada_w[i0] + ada_b[i0]
    sh_a, sc_a, g_a, sh_m, sc_m, g_m = jnp.split(mod_lat, 6, axis=-1)
    csh_a, csc_a = jnp.split(mod_ctx, 6, axis=-1)[:2]
    sh2 = jnp.stack([jnp.broadcast_to(csh_a, (B, D)), sh_a], axis=1)
    sc2 = jnp.stack([jnp.broadcast_to(csc_a, (B, D)), sc_a], axis=1)

    proj = _in_proj(ctx, x, sh2, sc2, p['norm_mix'], _pack_w_in(p['w_in']))

    q, k, v = _mla_prep(proj, p, n_ctx, T)
    att = _attention(q, k, v)

    r, vv, kk, lw, bk, kd, lg = _rwkv_prep(proj, p, n_ctx)
    y2 = _wkv_scan(r, vv, kk, lw, bk, kd, n_ctx)
    rw_out = _rwkv_readout(y2[0] + y2[1], r[:, n_ctx:], vv[:, n_ctx:], kd[:, :, n_ctx:], lg[:, n_ctx:], p)

    x1 = _merge(att, rw_out, proj, x, g_a, p, n_ctx)
    return _moe(x1, sh_m, sc_m, g_m, p)


ROW_TILE = 256
PROJ_RW = 2048
PROJ_GATE = 2048
PROJ_MLA = 512
PROJ_W = PROJ_RW + PROJ_GATE + PROJ_MLA
MLA_IN = Q_LORA + KV_LORA + QK_ROPE
RW_SPLITS = (RW_DIM, RW_DIM, RW_DIM, DECAY_LORA, DECAY_LORA, AICL_LORA, AICL_LORA, GATE_LORA)
RW_IN = sum(RW_SPLITS)


def _pack_w_in(w_in):
    w_mla = w_in[:, :MLA_IN]
    w_rw = w_in[:, MLA_IN:MLA_IN + RW_IN]
    w_gate = w_in[:, MLA_IN + RW_IN:]
    pad = lambda w, n: jnp.pad(w, ((0, 0), (0, n - w.shape[1])))
    return jnp.concatenate([pad(w_rw, PROJ_RW), w_gate, pad(w_mla, PROJ_MLA)], axis=1).astype(BF16)


def _in_proj_kernel(ctx_ref, x_ref, sh_ref, sc_ref, gain_ref, w_ref, o_ref):
    is_ctx = pl.program_id(1) == 0
    xt = jnp.where(is_ctx, ctx_ref[0], x_ref[0])
    sh = jnp.where(is_ctx, sh_ref[0, 0:1], sh_ref[0, 1:2])
    sc = jnp.where(is_ctx, sc_ref[0, 0:1], sc_ref[0, 1:2])
    y = xt * lax.rsqrt(jnp.mean(xt * xt, axis=-1, keepdims=True) + EPS) * gain_ref[...]
    h = y * (1.0 + sc) + sh
    o_ref[0] = _dot(h, w_ref[...])


def _in_proj(ctx, x, sh2, sc2, gain, w):
    B, T, D = x.shape
    n_ctx = ctx.shape[1]
    assert n_ctx == ROW_TILE and T % ROW_TILE == 0
    nt = 1 + T // ROW_TILE
    return pl.pallas_call(
        _in_proj_kernel,
        grid=(B, nt),
        in_specs=[
            pl.BlockSpec((1, ROW_TILE, D), lambda b, i: (b, 0, 0)),
            pl.BlockSpec((1, ROW_TILE, D), lambda b, i: (b, jnp.maximum(i - 1, 0), 0)),
            pl.BlockSpec((1, 2, D), lambda b, i: (b, 0, 0)),
            pl.BlockSpec((1, 2, D), lambda b, i: (b, 0, 0)),
            pl.BlockSpec((1, D), lambda b, i: (0, 0)),
            pl.BlockSpec((D, PROJ_W), lambda b, i: (0, 0)),
        ],
        out_specs=pl.BlockSpec((1, ROW_TILE, PROJ_W), lambda b, i: (b, i, 0)),
        out_shape=jax.ShapeDtypeStruct((B, n_ctx + T, PROJ_W), F32),
        compiler_params=pltpu.CompilerParams(
            dimension_semantics=("arbitrary", "arbitrary"), vmem_limit_bytes=VMEM_LIMIT),
        name="in_proj",
    )(ctx, x, sh2, sc2, gain.reshape(1, D), w)


def _rms(x, gain):
    return x * lax.rsqrt(jnp.mean(x * x, axis=-1, keepdims=True) + EPS) * gain


def _rope_tables(n_tokens):
    rows = n_tokens // GRID_W
    row = jnp.repeat(jnp.arange(rows, dtype=F32), GRID_W)
    col = jnp.tile(jnp.arange(GRID_W, dtype=F32), rows)
    n_freq = QK_ROPE // 4
    inv_freq = ROPE_BASE ** (-jnp.arange(n_freq, dtype=F32) / n_freq)
    ang = jnp.concatenate([row[:, None] * inv_freq, col[:, None] * inv_freq], axis=-1)
    return jnp.cos(ang), jnp.sin(ang)


def _rope(x, cos, sin):
    half = QK_ROPE // 2
    x_nope, x1, x2 = x[..., :QK_NOPE], x[..., QK_NOPE:QK_NOPE + half], x[..., QK_NOPE + half:]
    c, s = cos[:, None, :], sin[:, None, :]
    return jnp.concatenate([x_nope, x1 * c - x2 * s, x1 * s + x2 * c], axis=-1)


def _mla_prep(proj, p, n_ctx, T):
    B = proj.shape[0]
    H = MLA_HEADS
    mla = proj[:, :, PROJ_RW + PROJ_GATE:PROJ_RW + PROJ_GATE + MLA_IN]
    q_lat = mla[:, n_ctx:, :Q_LORA]
    kv_lat = mla[:, :, Q_LORA:Q_LORA + KV_LORA]
    kr = mla[:, :, Q_LORA + KV_LORA:]
    cos, sin = _rope_tables(T)
    q = (_rms(q_lat, p['q_lat_norm']) @ p['w_uq']).reshape(B, T, H, QK_HEAD)
    q = _rope(_rms(q, p['q_norm']), cos, sin) * ATTN_SCALE
    kv = (_rms(kv_lat, p['kv_lat_norm']) @ p['w_ukv']).reshape(B, n_ctx + T, H, QK_NOPE + V_HEAD)
    k_nope, v = kv[..., :QK_NOPE], kv[..., QK_NOPE:]
    k_sh = jnp.broadcast_to(kr[:, :, None, :], (B, n_ctx + T, H, QK_ROPE))
    k = _rms(jnp.concatenate([k_nope, k_sh], axis=-1), p['k_norm'])
    k = jnp.concatenate([k[:, :n_ctx], _rope(k[:, n_ctx:], cos, sin)], axis=1)
    pad = lambda z: jnp.pad(z, ((0, 0), (0, 0), (0, 0), (0, HEAD_PAD - QK_HEAD)))
    q = pad(q).reshape(B, T, H * HEAD_PAD).astype(BF16)
    k = pad(k).reshape(B, n_ctx + T, H * HEAD_PAD).astype(BF16)
    v = v.reshape(B, n_ctx + T, H * V_HEAD).astype(BF16)
    return q, k, v


ATTN_Q_TILE = 512
HEADS_PER_STEP = LANES // V_HEAD


def _attn_kernel(q_ref, k_ref, v_ref, o_ref):
    v2 = v_ref[0]
    outs = []
    for hh in range(HEADS_PER_STEP):
        sl = slice(hh * HEAD_PAD, (hh + 1) * HEAD_PAD)
        s = _dot_t(q_ref[0, :, sl], k_ref[0, :, sl])
        m = jnp.max(s, axis=-1, keepdims=True)
        e = jnp.exp(s - m)
        l = jnp.sum(e, axis=-1, keepdims=True)
        outs.append(_dot(e, v2) / l)
    lane = lax.broadcasted_iota(jnp.int32, outs[0].shape, 1)
    o_ref[0] = jnp.where(lane < V_HEAD, outs[0], outs[1])


def _attention(q, k, v):
    B, T, _ = q.shape
    Kt = k.shape[1]
    hp = MLA_HEADS // HEADS_PER_STEP
    qw = HEADS_PER_STEP * HEAD_PAD
    return pl.pallas_call(
        _attn_kernel,
        grid=(B, hp, T // ATTN_Q_TILE),
        in_specs=[
            pl.BlockSpec((1, ATTN_Q_TILE, qw), lambda b, h, i: (b, i, h)),
            pl.BlockSpec((1, Kt, qw), lambda b, h, i: (b, 0, h)),
            pl.BlockSpec((1, Kt, LANES), lambda b, h, i: (b, 0, h)),
        ],
        out_specs=pl.BlockSpec((1, ATTN_Q_TILE, LANES), lambda b, h, i: (b, i, h)),
        out_shape=jax.ShapeDtypeStruct((B, T, MLA_HEADS * V_HEAD), F32),
        compiler_params=pltpu.CompilerParams(
            dimension_semantics=("arbitrary", "arbitrary", "arbitrary"), vmem_limit_bytes=VMEM_LIMIT),
        name="attention",
    )(q, k, v)


def _conv3(x, w):
    xp = jnp.pad(x, ((0, 0), (1, 1), (0, 0)))
    T = x.shape[1]
    return xp[:, 0:T] * w[0] + xp[:, 1:T + 1] * w[1] + xp[:, 2:T + 2] * w[2]


def _rwkv_prep(proj, p, n_ctx):
    B, Tt, _ = proj.shape
    rw = proj[:, :, :RW_IN]
    rw = jnp.concatenate([_conv3(rw[:, :n_ctx], p['shift_conv']), _conv3(rw[:, n_ctx:], p['shift_conv'])], axis=1)
    r, k, v, lw_f, lw_b, la_f, la_b, lg = jnp.split(rw, np.cumsum(RW_SPLITS)[:-1].tolist(), axis=-1)
    heads = lambda z: z.reshape(B, Tt, RW_HEADS, RW_HEAD)
    kk = heads(k * p['k_k'])
    kk = (kk * lax.rsqrt(jnp.sum(kk * kk, axis=-1, keepdims=True) + 1e-12)).reshape(B, Tt, RW_DIM)
    lws, bks, kds = [], [], []
    for d, (lw, la) in enumerate(((lw_f, la_f), (lw_b, la_b))):
        w_log = -jax.nn.softplus(-(p['decay_w0'][d] + jnp.tanh(lw) @ p['decay_w2'][d])) - 0.5
        a = jax.nn.sigmoid(p['aicl_a0'][d] + la @ p['aicl_a2'][d])
        lws.append(-jnp.exp(w_log))
        bks.append(kk * a)
        kds.append(k * (1 + (a - 1) * p['k_a']))
    return r, v, kk, jnp.stack(lws), jnp.stack(bks), jnp.stack(kds), lg


def _rwkv_readout(y, r, v, kd, lg, p):
    B, T, _ = y.shape
    heads = lambda z: z.reshape(B, T, RW_HEADS, RW_HEAD)
    yh = heads(y)
    mu = jnp.mean(yh, axis=-1, keepdims=True)
    var = jnp.mean(jnp.square(yh - mu), axis=-1, keepdims=True)
    y_n = ((yh - mu) * lax.rsqrt(var + GN_EPS)).reshape(B, T, RW_DIM) * p['gn_w'] + p['gn_b']
    k_sum = heads(kd[0] + kd[1])
    bonus = jnp.sum(heads(r) * k_sum * p['r_k'], axis=-1, keepdims=True) * heads(v)
    g = jax.nn.sigmoid(lg) @ p['gate_g2']
    return (y_n + bonus.reshape(B, T, RW_DIM)) * g


def _merge_kernel(att_ref, rw_ref, gate_ref, x_ref, ga_ref, wm_ref, wr_ref, wo_ref, o_ref):
    D = x_ref.shape[-1]
    g = jax.nn.sigmoid(gate_ref[0])
    mix = g[:, :D] * _dot(att_ref[0], wm_ref[...]) + g[:, D:] * _dot(rw_ref[0], wr_ref[...])
    o_ref[0] = x_ref[0] + ga_ref[0] * _dot(mix, wo_ref[...])


def _merge(att, rw_out, proj, x, g_a, p, n_ctx):
    B, T, D = x.shape
    ctx_tiles = n_ctx // ROW_TILE
    gate_blk = PROJ_RW // PROJ_GATE
    row = lambda b, i: (b, i, 0)
    const = lambda b, i: (0, 0)
    return pl.pallas_call(
        _merge_kernel,
        grid=(B, T // ROW_TILE),
        in_specs=[
            pl.BlockSpec((1, ROW_TILE, att.shape[-1]), row),
            pl.BlockSpec((1, ROW_TILE, rw_out.shape[-1]), row),
            pl.BlockSpec((1, ROW_TILE, PROJ_GATE), lambda b, i: (b, i + ctx_tiles, gate_blk)),
            pl.BlockSpec((1, ROW_TILE, D), row),
            pl.BlockSpec((1, 1, D), lambda b, i: (b, 0, 0)),
            pl.BlockSpec(p['w_o_mla'].shape, const),
            pl.BlockSpec(p['w_o_rwkv'].shape, const),
            pl.BlockSpec(p['w_out'].shape, const),
        ],
        out_specs=pl.BlockSpec((1, ROW_TILE, D), row),
        out_shape=jax.ShapeDtypeStruct((B, T, D), F32),
        compiler_params=pltpu.CompilerParams(
            dimension_semantics=("arbitrary", "arbitrary"), vmem_limit_bytes=VMEM_LIMIT),
        name="merge",
    )(att, rw_out, proj, x, g_a[:, None, :], p['w_o_mla'].astype(BF16), p['w_o_rwkv'].astype(BF16),
      p['w_out'].astype(BF16))


MOE_BLOCK = 256


NEG_INF = float("-inf")
HI_MASK = 0xFFFF0000


def _pack_pair(lo, hi):
    lo_b = lax.bitcast_convert_type(lo.astype(BF16).astype(F32), jnp.uint32)
    hi_b = lax.bitcast_convert_type(hi.astype(BF16).astype(F32), jnp.uint32)
    return (lo_b >> 16) | (hi_b & jnp.uint32(HI_MASK))


def _unpack_pair(w):
    lo = lax.bitcast_convert_type(w << 16, F32)
    hi = lax.bitcast_convert_type(w & jnp.uint32(HI_MASK), F32)
    return lo.astype(BF16), hi.astype(BF16)


def _row_max(x):
    return jnp.max(x, axis=-1, keepdims=True)


def _first_index_of(x, value, lane_f):
    return jnp.min(jnp.where(x == value, lane_f, float(x.shape[-1])), axis=-1, keepdims=True)


def _router_kernel(x_ref, gain_ref, sh_ref, sc_ref, wh_ref, wm_ref, wl_ref, bias_ref,
                   u_ref, ids_ref, gates_ref, ranks_ref, counts_ref, carry_ref):
    @pl.when((pl.program_id(0) == 0) & (pl.program_id(1) == 0))
    def _():
        carry_ref[...] = jnp.zeros_like(carry_ref)

    x = x_ref[0]
    tm, D = x.shape
    E = bias_ref.shape[-1]
    u = x * lax.rsqrt(jnp.mean(x * x, axis=-1, keepdims=True) + EPS) * gain_ref[...]
    u = u * (1.0 + sc_ref[0]) + sh_ref[0]
    u_ref[0] = _pack_pair(u[:, :D // 2], u[:, D // 2:])

    uh, um, ul = _split3(u)
    wh, wm, wl = wh_ref[...], wm_ref[...], wl_ref[...]
    d = lambda a, b: jnp.dot(a, b, preferred_element_type=F32)
    logits = d(uh, wh) + (d(uh, wm) + d(um, wh)) + (d(uh, wl) + d(um, wm) + d(ul, wh))
    scores = jax.nn.sigmoid(logits)
    sel = scores + bias_ref[...]

    lane_i = lax.broadcasted_iota(jnp.int32, (tm, E), 1)
    lane_f = lane_i.astype(F32)
    out_f = lax.broadcasted_iota(jnp.int32, (tm, LANES), 1).astype(F32)
    per_group = E // N_GROUPS
    grp_f = jnp.floor(lane_f * (1.0 / per_group))

    gs = jnp.full((tm, LANES), NEG_INF, F32)
    for g in range(N_GROUPS):
        sg = jnp.where(lane_i >= g * per_group, jnp.where(lane_i < (g + 1) * per_group, sel, NEG_INF), NEG_INF)
        m1 = _row_max(sg)
        i1 = _first_index_of(sg, m1, lane_f)
        m2 = _row_max(jnp.where(lane_f == i1, NEG_INF, sg))
        gs = jnp.where(out_f == g, m1 + m2, gs)

    allow = jnp.zeros((tm, E), F32)
    for _ in range(TOPK_GROUPS):
        m = _row_max(gs)
        i = _first_index_of(gs, m, out_f)
        gs = jnp.where(out_f == i, NEG_INF, gs)
        allow = jnp.where(grp_f == i, 1.0, allow)
    selm = jnp.where(allow > 0.0, sel, NEG_INF)

    ids = jnp.zeros((tm, LANES), F32)
    gts = jnp.zeros((tm, LANES), F32)
    member = jnp.zeros((tm, E), F32)
    idx_cols = []
    gsum = jnp.zeros((tm, 1), F32)
    for k in range(TOP_K):
        m = _row_max(selm)
        i = _first_index_of(selm, m, lane_f)
        hit = lane_f == i
        gk = jnp.sum(jnp.where(hit, scores, 0.0), axis=-1, keepdims=True)
        selm = jnp.where(hit, NEG_INF, selm)
        member = jnp.where(hit, 1.0, member)
        ids = jnp.where(out_f == k, i, ids)
        gts = jnp.where(out_f == k, gk, gts)
        idx_cols.append(i)
        gsum = gsum + gk
    gts = gts / gsum * ROUTED_SCALE

    r2 = lax.broadcasted_iota(jnp.int32, (tm, tm), 0)
    c2 = lax.broadcasted_iota(jnp.int32, (tm, tm), 1)
    before = jnp.where(r2 > c2, 1.0, 0.0).astype(BF16)
    mem_b = member.astype(BF16)
    carry = carry_ref[...]
    pos = carry + jnp.dot(before, mem_b, preferred_element_type=F32)
    rk = jnp.zeros((tm, LANES), F32)
    for k in range(TOP_K):
        rk = jnp.where(out_f == k, jnp.sum(jnp.where(lane_f == idx_cols[k], pos, 0.0), axis=-1, keepdims=True), rk)
    colsum = jnp.dot(jnp.ones((8, tm), BF16), mem_b, preferred_element_type=F32)[0:1]
    carry_ref[...] = carry + colsum
    counts_ref[...] = carry + colsum
    ids_ref[0] = ids.astype(jnp.int32)
    ranks_ref[0] = rk.astype(jnp.int32)
    gates_ref[0] = gts


def _router(x1, sh_m, sc_m, p):
    B, T, D = x1.shape
    E = N_EXPERTS
    wh, wm, wl = _split3(p['router_w'])
    row = lambda b, i: (b, i, 0)
    const = lambda b, i: (0, 0)
    vec = lambda b, i: (b, 0, 0)
    lane_out = lambda dt: jax.ShapeDtypeStruct((B, T, LANES), dt)
    return pl.pallas_call(
        _router_kernel,
        grid=(B, T // ROW_TILE),
        in_specs=[
            pl.BlockSpec((1, ROW_TILE, D), row),
            pl.BlockSpec((1, D), const),
            pl.BlockSpec((1, 1, D), vec),
            pl.BlockSpec((1, 1, D), vec),
            pl.BlockSpec((D, E), const),
            pl.BlockSpec((D, E), const),
            pl.BlockSpec((D, E), const),
            pl.BlockSpec((1, E), const),
        ],
        out_specs=[
            pl.BlockSpec((1, ROW_TILE, D // 2), row),
            pl.BlockSpec((1, ROW_TILE, LANES), row),
            pl.BlockSpec((1, ROW_TILE, LANES), row),
            pl.BlockSpec((1, ROW_TILE, LANES), row),
            pl.BlockSpec((1, E), const),
        ],
        out_shape=[
            jax.ShapeDtypeStruct((B, T, D // 2), jnp.uint32),
            lane_out(jnp.int32), lane_out(F32), lane_out(jnp.int32),
            jax.ShapeDtypeStruct((1, E), F32),
        ],
        scratch_shapes=[pltpu.VMEM((1, E), F32)],
        compiler_params=pltpu.CompilerParams(
            dimension_semantics=("arbitrary", "arbitrary"), vmem_limit_bytes=VMEM_LIMIT),
        name="router",
    )(x1, p['norm_ffn'].reshape(1, D), sh_m[:, None, :], sc_m[:, None, :], wh, wm, wl,
      p['router_bias'].reshape(1, E))


def _row_copy(src, src_row, dst, dst_row, sem):
    return pltpu.make_async_copy(src.at[pl.ds(src_row, 1)], dst.at[pl.ds(dst_row, 1)], sem)


def _dispatch_kernel(dest_ref, u_ref, slots_in_ref, slots_ref, sem):
    del slots_in_ref
    tm = u_ref.shape[0]

    def issue(r, carry):
        for k in range(TOP_K):
            _row_copy(u_ref, r, slots_ref, dest_ref[0, 0, r * TOP_K + k], sem).start()
        return carry

    lax.fori_loop(0, tm, issue, 0)

    def drain(r, carry):
        for k in range(TOP_K):
            _row_copy(u_ref, r, slots_ref, 0, sem).wait()
        return carry

    lax.fori_loop(0, tm, drain, 0)


def _dispatch(dest, u_rows, n_slots):
    n_tok, W = u_rows.shape
    nt = n_tok // ROW_TILE
    return pl.pallas_call(
        _dispatch_kernel,
        grid=(nt,),
        in_specs=[
            pl.BlockSpec((1, 1, ROW_TILE * TOP_K), lambda i: (i, 0, 0), memory_space=pltpu.SMEM),
            pl.BlockSpec((ROW_TILE, W), lambda i: (i, 0)),
            pl.BlockSpec(memory_space=pl.ANY),
        ],
        out_specs=pl.BlockSpec(memory_space=pl.ANY),
        out_shape=jax.ShapeDtypeStruct((n_slots, W), jnp.uint32),
        scratch_shapes=[pltpu.SemaphoreType.DMA(())],
        input_output_aliases={2: 0},
        compiler_params=pltpu.CompilerParams(
            dimension_semantics=("arbitrary",), vmem_limit_bytes=VMEM_LIMIT),
        name="dispatch",
    )(dest, u_rows, jnp.zeros((n_slots, W), jnp.uint32))


def _moe_ffn_kernel(be_ref, nu_ref, x_ref, w1_ref, w3_ref, w2_ref, o_ref, w13_ref, w2b_ref):
    j = pl.program_id(0)
    F = w1_ref.shape[-1]
    half = x_ref.shape[-1]

    @pl.when((j == 0) | (be_ref[j] != be_ref[jnp.maximum(j - 1, 0)]))
    def _():
        w13_ref[:, :F] = w1_ref[0].astype(BF16)
        w13_ref[:, F:] = w3_ref[0].astype(BF16)
        w2b_ref[...] = w2_ref[0].astype(BF16)

    @pl.when(j < nu_ref[0])
    def _():
        lo, hi = _unpack_pair(x_ref[...])
        h = (jnp.dot(lo, w13_ref[:half], preferred_element_type=F32)
             + jnp.dot(hi, w13_ref[half:], preferred_element_type=F32))
        h1, h3 = h[:, :F], h[:, F:]
        y = _dot(h1 * jax.nn.sigmoid(h1) * h3, w2b_ref[...])
        o_ref[...] = _pack_pair(y[:, :half], y[:, half:])

    @pl.when(j >= nu_ref[0])
    def _():
        o_ref[...] = jnp.zeros_like(o_ref)


def _moe_ffn(block_e, n_used, x_slots, w1, w3, w2):
    slots, W = x_slots.shape
    E, D, F = w1.shape
    nblk = slots // MOE_BLOCK
    grid_spec = pltpu.PrefetchScalarGridSpec(
        num_scalar_prefetch=2,
        grid=(nblk,),
        in_specs=[
            pl.BlockSpec((MOE_BLOCK, W), lambda j, be, nu: (jnp.minimum(j, nu[0] - 1), 0)),
            pl.BlockSpec((1, D, F), lambda j, be, nu: (be[j], 0, 0)),
            pl.BlockSpec((1, D, F), lambda j, be, nu: (be[j], 0, 0)),
            pl.BlockSpec((1, F, D), lambda j, be, nu: (be[j], 0, 0)),
        ],
        out_specs=pl.BlockSpec((MOE_BLOCK, W), lambda j, be, nu: (j, 0)),
        scratch_shapes=[pltpu.VMEM((D, 2 * F), BF16), pltpu.VMEM((F, D), BF16)],
    )
    return pl.pallas_call(
        _moe_ffn_kernel,
        grid_spec=grid_spec,
        out_shape=jax.ShapeDtypeStruct((slots, W), jnp.uint32),
        compiler_params=pltpu.CompilerParams(
            dimension_semantics=("arbitrary",), vmem_limit_bytes=VMEM_LIMIT),
        name="moe_ffn",
    )(block_e, n_used, x_slots, w1, w3, w2)


def _combine_kernel(dest_ref, gates_ref, u_ref, x_ref, gm_ref, w1_ref, w3_ref, w2_ref, ys_ref, o_ref, buf_ref, sem):
    tm, half = u_ref.shape

    def issue(r, carry):
        for k in range(TOP_K):
            _row_copy(ys_ref, dest_ref[0, 0, r * TOP_K + k], buf_ref.at[k], r, sem).start()
        return carry

    lax.fori_loop(0, tm, issue, 0)

    ulo, uhi = _unpack_pair(u_ref[...])
    both = lambda w_ref: (jnp.dot(ulo, w_ref[:half], preferred_element_type=F32)
                          + jnp.dot(uhi, w_ref[half:], preferred_element_type=F32))
    h1, h3 = both(w1_ref), both(w3_ref)
    shared = _dot(h1 * jax.nn.sigmoid(h1) * h3, w2_ref[...])

    def drain(r, carry):
        for k in range(TOP_K):
            _row_copy(ys_ref, 0, buf_ref.at[k], r, sem).wait()
        return carry

    lax.fori_loop(0, tm, drain, 0)

    gates = gates_ref[...]
    lo_acc = jnp.zeros((tm, half), F32)
    hi_acc = jnp.zeros((tm, half), F32)
    for k in range(TOP_K):
        lo, hi = _unpack_pair(buf_ref[k])
        gk = gates[:, k:k + 1]
        lo_acc = lo_acc + gk * lo.astype(F32)
        hi_acc = hi_acc + gk * hi.astype(F32)
    routed = jnp.concatenate([lo_acc, hi_acc], axis=1)
    o_ref[...] = x_ref[...] + gm_ref[0] * (routed + shared)


def _combine(dest, gates, u_rows, x1, g_m, y_slots, p):
    B, T, D = x1.shape
    n_tok = B * T
    W = u_rows.shape[1]
    tiles_per_batch = T // ROW_TILE
    row = lambda i: (i, 0)
    const = lambda i: (0, 0)
    out = pl.pallas_call(
        _combine_kernel,
        grid=(n_tok // ROW_TILE,),
        in_specs=[
            pl.BlockSpec((1, 1, ROW_TILE * TOP_K), lambda i: (i, 0, 0), memory_space=pltpu.SMEM),
            pl.BlockSpec((ROW_TILE, LANES), row),
            pl.BlockSpec((ROW_TILE, W), row),
            pl.BlockSpec((ROW_TILE, D), row),
            pl.BlockSpec((1, 1, D), lambda i: (i // tiles_per_batch, 0, 0)),
            pl.BlockSpec(p['shared_w1'].shape, const),
            pl.BlockSpec(p['shared_w3'].shape, const),
            pl.BlockSpec(p['shared_w2'].shape, const),
            pl.BlockSpec(memory_space=pl.ANY),
        ],
        out_specs=pl.BlockSpec((ROW_TILE, D), row),
        out_shape=jax.ShapeDtypeStruct((n_tok, D), F32),
        scratch_shapes=[pltpu.VMEM((TOP_K, ROW_TILE, W), jnp.uint32), pltpu.SemaphoreType.DMA(())],
        compiler_params=pltpu.CompilerParams(
            dimension_semantics=("arbitrary",), vmem_limit_bytes=VMEM_LIMIT),
        name="combine",
    )(dest, gates.reshape(n_tok, LANES), u_rows, x1.reshape(n_tok, D), g_m[:, None, :],
      p['shared_w1'].astype(BF16), p['shared_w3'].astype(BF16), p['shared_w2'].astype(BF16), y_slots)
    return out.reshape(B, T, D)


def _moe(x1, sh_m, sc_m, g_m, p):
    B, T, D = x1.shape
    n_tok = B * T
    u_packed, ids, gates, ranks, counts = _router(x1, sh_m, sc_m, p)

    counts = counts[0].astype(jnp.int32)
    padded = (counts + MOE_BLOCK - 1) // MOE_BLOCK * MOE_BLOCK
    padded_end = jnp.cumsum(padded)
    base = padded_end - padded
    n_blocks = n_tok * TOP_K // MOE_BLOCK + N_EXPERTS
    n_used = (padded_end[-1] // MOE_BLOCK).astype(jnp.int32)
    blk = jnp.minimum(jnp.arange(n_blocks), n_used - 1) * MOE_BLOCK
    block_e = jnp.minimum(jnp.searchsorted(padded_end, blk, side='right'), N_EXPERTS - 1).astype(jnp.int32)
    ids8 = ids[:, :, :TOP_K].reshape(-1)
    dest = (base[ids8] + ranks[:, :, :TOP_K].reshape(-1)).reshape(n_tok // ROW_TILE, 1, ROW_TILE * TOP_K)

    u_rows = u_packed.reshape(n_tok, D // 2)
    x_slots = _dispatch(dest, u_rows, n_blocks * MOE_BLOCK)
    y_slots = _moe_ffn(block_e, n_used.reshape(1), x_slots, p['expert_w1'], p['expert_w3'], p['expert_w2'])
    return _combine(dest, gates, u_rows, x1, g_m, y_slots, p)
```

```python
import functools

import jax
import jax.numpy as jnp
import numpy as np
from jax import lax
from jax.experimental import pallas as pl
from jax.experimental.pallas import tpu as pltpu

F32 = jnp.float32
BF16 = jnp.bfloat16

GRID_W = 64
EPS = 1e-6
MLA_HEADS = 8
QK_NOPE = 64
QK_ROPE = 32
QK_HEAD = QK_NOPE + QK_ROPE
V_HEAD = 64
Q_LORA = 256
KV_LORA = 128
ROPE_BASE = 10000.0
ATTN_SCALE = QK_HEAD ** -0.5
RW_HEADS = 8
RW_HEAD = 64
RW_DIM = RW_HEADS * RW_HEAD
DECAY_LORA = 64
AICL_LORA = 64
GATE_LORA = 160
GN_EPS = 64e-5
N_EXPERTS = 256
TOP_K = 8
N_GROUPS = 8
TOPK_GROUPS = 4
ROUTED_SCALE = 2.5

LANES = 128
MXU_DIM = 256
VMEM_LIMIT = 56 * 1024 * 1024

WKV_CHUNK = 64
HEADS_PER_GROUP = MXU_DIM // RW_HEAD
HEAD_PAD = LANES


def _dot(a, b):
    return jnp.dot(a.astype(BF16), b.astype(BF16), preferred_element_type=F32)


def _dot_t(a, b):
    return lax.dot_general(a.astype(BF16), b.astype(BF16), (((1,), (1,)), ((), ())),
                           preferred_element_type=F32)


def _split3(x):
    h = x.astype(BF16)
    r1 = x - h.astype(F32)
    m = r1.astype(BF16)
    lo = (r1 - m.astype(F32)).astype(BF16)
    return h, m, lo


def _dot_hi(a_bf16_exact, x):
    h, m, lo = _split3(x)
    d = lambda y: jnp.dot(a_bf16_exact, y, preferred_element_type=F32)
    return d(h) + d(m) + d(lo)


def _wkv_chunk(r, v, kk, lw, bk, kd, s0, reverse, emit):
    L, G = r.shape
    row = lax.broadcasted_iota(jnp.int32, (L, G), 0)
    lane = lax.broadcasted_iota(jnp.int32, (L, G), 1)
    diff = (lane % L - row) if reverse else (row - lane % L)
    strict = diff > 0
    incl = diff >= 0
    r2 = lax.broadcasted_iota(jnp.int32, (L, L), 0)
    c2 = lax.broadcasted_iota(jnp.int32, (L, L), 1)
    tri = jnp.where(((c2 - r2) if reverse else (r2 - c2)) >= 0, 1.0, 0.0).astype(BF16)
    ones = jnp.ones((L, L), BF16)
    lane_head = lane // RW_HEAD
    rowb = lax.broadcasted_iota(jnp.int32, (G, G), 0) // RW_HEAD
    colb = lax.broadcasted_iota(jnp.int32, (G, G), 1) // RW_HEAD

    def stack(x):
        return jnp.concatenate(
            [jnp.where(lane_head == h, x, 0.0) for h in range(HEADS_PER_GROUP)], axis=0).astype(BF16)

    cum_in = _dot_hi(tri, lw)
    g_in = jnp.exp(cum_in)
    g_inv = jnp.exp(-cum_in)
    g_ex = jnp.exp(cum_in - lw)
    g_tot = jnp.exp(_dot_hi(ones, lw))
    a_h = -kk * g_ex
    b_h = bk * g_inv
    k_h = kd * g_inv
    r_h = r * g_in
    yield

    gram = _dot_t(jnp.concatenate([a_h, r_h], axis=0),
                  jnp.concatenate([stack(b_h), stack(k_h)], axis=0))
    ab = jnp.where(strict, gram[:L, :G], 0.0)
    ak = jnp.where(strict, gram[:L, G:], 0.0)
    rb = jnp.where(incl, gram[L:, :G], 0.0)
    rk = jnp.where(incl, gram[L:, G:], 0.0)
    yield

    tm = jnp.where(diff == 0, 1.0, 0.0) + ab
    p = ab
    v_st = stack(v)
    akv = _dot(ak, v_st)
    for _ in range(int(np.log2(L)) - 1):
        p = _dot(p, stack(p))
        yield
        tm = tm + _dot(tm, stack(p))
        yield

    wu = _dot(tm, jnp.concatenate([stack(a_h), stack(akv)], axis=1))
    w_t, u_t = wu[:, :G], wu[:, G:]
    yield
    rbwu = _dot(rb, jnp.concatenate([stack(w_t), stack(u_t)], axis=1))
    r_t = r_h + rbwu[:, :G]
    y_t = rbwu[:, G:] + _dot(rk, v_st)
    yield

    s0_st = stack(s0)
    y = _dot_t(r_t, s0_st) + y_t
    u = _dot_t(w_t, s0_st) + u_t
    yield
    uv = jnp.concatenate([u, v], axis=0)
    bkc = jnp.concatenate([b_h, k_h], axis=0)
    upd = lax.dot_general(uv.astype(BF16), bkc.astype(BF16), (((0,), (0,)), ((), ())),
                          preferred_element_type=F32)
    upd = jnp.where(rowb == colb, upd, 0.0)
    upd_d = upd[0:L]
    for h in range(1, HEADS_PER_GROUP):
        upd_d = upd_d + upd[h * L:(h + 1) * L]
    emit(y, (s0 + upd_d) * g_tot)


def _wkv_kernel(*refs, n_groups, n_batch):
    ins, (yf_ref, yb_ref, s_ref) = refs[:12], refs[12:]
    G = MXU_DIM

    @pl.when(pl.program_id(1) == 0)
    def _():
        s_ref[...] = jnp.zeros_like(s_ref)

    chains = []
    for d, y_ref in enumerate((yf_ref, yb_ref)):
        r_ref, v_ref, kk_ref, lw_ref, bk_ref, kd_ref = ins[6 * d:6 * d + 6]
        for bi in range(n_batch):
            for g in range(n_groups):
                sl = slice(g * G, (g + 1) * G)

                def emit(y, s_new, y_ref=y_ref, d=d, bi=bi, g=g, sl=sl):
                    y_ref[bi, :, sl] = y
                    s_ref[d, bi, g] = s_new

                chains.append(_wkv_chunk(
                    r_ref[bi, :, sl], v_ref[bi, :, sl], kk_ref[bi, :, sl], lw_ref[0, bi, :, sl],
                    bk_ref[0, bi, :, sl], kd_ref[0, bi, :, sl], s_ref[d, bi, g], d == 1, emit))
    while chains:
        chains = [c for c in chains if next(c, StopIteration) is not StopIteration]


WKV_BATCH = 2


def _wkv_scan(r, v, kk, lw, bk, kd, n_ctx):
    B, Ttot, C = r.shape
    L = WKV_CHUNK
    nc = Ttot // L
    ncc = n_ctx // L
    nl = nc - ncc
    n_groups = C // MXU_DIM
    nb = WKV_BATCH

    cid = (lambda s: s, lambda s: jnp.where(s < ncc, ncc - 1 - s, nc + ncc - 1 - s))
    first_out = (0, nl - 1)
    in_specs, out_specs = [], []
    for d in range(2):
        shared = pl.BlockSpec((nb, L, C), lambda b, s, d=d: (b, cid[d](s), 0))
        per_dir = pl.BlockSpec((1, nb, L, C), lambda b, s, d=d: (d, b, cid[d](s), 0))
        in_specs += [shared, shared, shared, per_dir, per_dir, per_dir]
        out_specs.append(pl.BlockSpec(
            (nb, L, C), lambda b, s, d=d: (b, jnp.where(s < ncc, first_out[d], cid[d](s) - ncc), 0)))
    out = jax.ShapeDtypeStruct((B, nl * L, C), F32)
    return pl.pallas_call(
        functools.partial(_wkv_kernel, n_groups=n_groups, n_batch=nb),
        grid=(B // nb, nc),
        in_specs=in_specs,
        out_specs=out_specs,
        out_shape=[out, out],
        scratch_shapes=[pltpu.VMEM((2, nb, n_groups, L, MXU_DIM), F32)],
        compiler_params=pltpu.CompilerParams(
            dimension_semantics=("arbitrary", "arbitrary"), vmem_limit_bytes=VMEM_LIMIT),
        name="wkv_scan",
    )(r, v, kk, lw, bk, kd, r, v, kk, lw, bk, kd)


def kernel(x, c, ctx, c_ctx, ada_w, ada_b, norm_mix, norm_ffn, w_in, shift_conv, q_lat_norm, w_uq, kv_lat_norm, w_ukv, q_norm, k_norm, w_o_mla, decay_w0, decay_w2, aicl_a0, aicl_a2, k_k, k_a, r_k, gn_w, gn_b, gate_g2, w_o_rwkv, w_out, router_w, router_bias, expert_w1, expert_w3, expert_w2, shared_w1, shared_w3, shared_w2):
    B, T, D = x.shape
    n_ctx = ctx.shape[1]
    i0 = 0
    p = dict(norm_mix=norm_mix[i0], norm_ffn=norm_ffn[i0], w_in=w_in[i0], shift_conv=shift_conv[i0],
             q_lat_norm=q_lat_norm[i0], w_uq=w_uq[i0], kv_lat_norm=kv_lat_norm[i0], w_ukv=w_ukv[i0],
             q_norm=q_norm[i0], k_norm=k_norm[i0], w_o_mla=w_o_mla[i0],
             decay_w0=decay_w0[i0], decay_w2=decay_w2[i0], aicl_a0=aicl_a0[i0], aicl_a2=aicl_a2[i0],
             k_k=k_k[i0], k_a=k_a[i0], r_k=r_k[i0], gn_w=gn_w[i0], gn_b=gn_b[i0], gate_g2=gate_g2[i0],
             w_o_rwkv=w_o_rwkv[i0], w_out=w_out[i0], router_w=router_w[i0], router_bias=router_bias[i0],
             expert_w1=expert_w1[i0], expert_w3=expert_w3[i0], expert_w2=expert_w2[i0],
             shared_w1=shared_w1[i0], shared_w3=shared_w3[i0], shared_w2=shared_w2[i0])

    mod = _ada_modulation(c, c_ctx, ada_w[i0], ada_b[i0])
    sh_a, sc_a, g_a, sh_m, sc_m, g_m = jnp.split(mod[:B], 6, axis=-1)
    csh_a, csc_a = jnp.split(mod[B], 6, axis=-1)[:2]
    sh2 = jnp.stack([jnp.broadcast_to(csh_a, (B, D)), sh_a], axis=1)
    sc2 = jnp.stack([jnp.broadcast_to(csc_a, (B, D)), sc_a], axis=1)

    proj = _in_proj(ctx, x, sh2, sc2, p['norm_mix'], _pack_w_in(p['w_in']))

    q, k, v = _mla_prep(proj, p, n_ctx, T)
    att = _attention(q, k, v)

    r, vv, kk, lw, bk, kd, bonus, gg = _rwkv_prep(proj, p, n_ctx)
    y_f, y_b = _wkv_scan(r, vv, kk, lw, bk, kd, n_ctx)

    x1 = _merge(att, y_f, y_b, bonus, gg, proj, x, g_a, p, n_ctx)
    return _moe(x1, sh_m, sc_m, g_m, p)


SUBLANES = 8


def _ada_kernel(c_ref, w_ref, b_ref, o_ref):
    cc = c_ref[...]
    w = w_ref[...]
    w_hi = w.astype(BF16)
    w_lo = (w - w_hi.astype(F32)).astype(BF16)
    o_ref[...] = _dot3(cc * jax.nn.sigmoid(cc), w_hi, w_lo) + b_ref[...]


def _ada_modulation(c, c_ctx, ada_w, ada_b):
    B, D = c.shape
    n_out = ada_w.shape[1]
    rows = -(-(B + 1) // SUBLANES) * SUBLANES
    c_all = jnp.concatenate([c, c_ctx[None, :], jnp.zeros((rows - B - 1, D), F32)], axis=0)
    out = pl.pallas_call(
        _ada_kernel,
        grid=(n_out // D,),
        in_specs=[pl.BlockSpec((rows, D), lambda j: (0, 0)),
                  pl.BlockSpec((D, D), lambda j: (0, j)),
                  pl.BlockSpec((1, D), lambda j: (0, j))],
        out_specs=pl.BlockSpec((rows, D), lambda j: (0, j)),
        out_shape=jax.ShapeDtypeStruct((rows, n_out), F32),
        compiler_params=pltpu.CompilerParams(dimension_semantics=("arbitrary",), vmem_limit_bytes=VMEM_LIMIT),
        name="ada_modulation",
    )(c_all, ada_w, ada_b.reshape(1, n_out))
    return out[:B + 1]


ROW_TILE = 256
PROJ_RW = 2048
PROJ_GATE = 2048
PROJ_MLA = 512
PROJ_W = PROJ_RW + PROJ_GATE + PROJ_MLA
MLA_IN = Q_LORA + KV_LORA + QK_ROPE
RW_SPLITS = (RW_DIM, RW_DIM, RW_DIM, DECAY_LORA, DECAY_LORA, AICL_LORA, AICL_LORA, GATE_LORA)
RW_IN = sum(RW_SPLITS)


def _pack_w_in(w_in):
    w_mla = w_in[:, :MLA_IN]
    w_rw = w_in[:, MLA_IN:MLA_IN + RW_IN]
    w_gate = w_in[:, MLA_IN + RW_IN:]
    pad = lambda w, n: jnp.pad(w, ((0, 0), (0, n - w.shape[1])))
    return jnp.concatenate([pad(w_rw, PROJ_RW), w_gate, pad(w_mla, PROJ_MLA)], axis=1).astype(BF16)


def _in_proj_kernel(ctx_ref, x_ref, sh_ref, sc_ref, gain_ref, w_ref, o_ref):
    is_ctx = pl.program_id(1) == 0
    xt = jnp.where(is_ctx, ctx_ref[0], x_ref[0])
    sh = jnp.where(is_ctx, sh_ref[0, 0:1], sh_ref[0, 1:2])
    sc = jnp.where(is_ctx, sc_ref[0, 0:1], sc_ref[0, 1:2])
    y = xt * lax.rsqrt(jnp.mean(xt * xt, axis=-1, keepdims=True) + EPS) * gain_ref[...]
    h = y * (1.0 + sc) + sh
    o_ref[0] = _dot(h, w_ref[...])


def _in_proj(ctx, x, sh2, sc2, gain, w):
    B, T, D = x.shape
    n_ctx = ctx.shape[1]
    assert n_ctx == ROW_TILE and T % ROW_TILE == 0
    nt = 1 + T // ROW_TILE
    return pl.pallas_call(
        _in_proj_kernel,
        grid=(B, nt),
        in_specs=[
            pl.BlockSpec((1, ROW_TILE, D), lambda b, i: (b, 0, 0)),
            pl.BlockSpec((1, ROW_TILE, D), lambda b, i: (b, jnp.maximum(i - 1, 0), 0)),
            pl.BlockSpec((1, 2, D), lambda b, i: (b, 0, 0)),
            pl.BlockSpec((1, 2, D), lambda b, i: (b, 0, 0)),
            pl.BlockSpec((1, D), lambda b, i: (0, 0)),
            pl.BlockSpec((D, PROJ_W), lambda b, i: (0, 0)),
        ],
        out_specs=pl.BlockSpec((1, ROW_TILE, PROJ_W), lambda b, i: (b, i, 0)),
        out_shape=jax.ShapeDtypeStruct((B, n_ctx + T, PROJ_W), F32),
        compiler_params=pltpu.CompilerParams(
            dimension_semantics=("arbitrary", "arbitrary"), vmem_limit_bytes=VMEM_LIMIT),
        name="in_proj",
    )(ctx, x, sh2, sc2, gain.reshape(1, D), w)


def _rms(x, gain):
    return x * lax.rsqrt(jnp.mean(x * x, axis=-1, keepdims=True) + EPS) * gain


def _rope_tables(n_tokens):
    rows = n_tokens // GRID_W
    row = jnp.repeat(jnp.arange(rows, dtype=F32), GRID_W)
    col = jnp.tile(jnp.arange(GRID_W, dtype=F32), rows)
    n_freq = QK_ROPE // 4
    inv_freq = ROPE_BASE ** (-jnp.arange(n_freq, dtype=F32) / n_freq)
    ang = jnp.concatenate([row[:, None] * inv_freq, col[:, None] * inv_freq], axis=-1)
    return jnp.cos(ang), jnp.sin(ang)


ROPE_HALF = QK_ROPE // 2
X1 = slice(QK_NOPE, QK_NOPE + ROPE_HALF)
X2 = slice(QK_NOPE + ROPE_HALF, QK_HEAD)


def _rot_cols(w):
    w3 = w.reshape(w.shape[0], MLA_HEADS, HEAD_PAD)
    rot = jnp.zeros_like(w3).at[:, :, X1].set(-w3[:, :, X2]).at[:, :, X2].set(w3[:, :, X1])
    return rot.reshape(w.shape)


def _swap_halves(g):
    return jnp.zeros_like(g).at[:, X1].set(g[:, X2]).at[:, X2].set(g[:, X1])


def _mla_weights(p):
    H = MLA_HEADS
    wq = jnp.pad(p['w_uq'].reshape(Q_LORA, H, QK_HEAD), ((0, 0), (0, 0), (0, HEAD_PAD - QK_HEAD)))
    wq = wq.reshape(Q_LORA, H * HEAD_PAD)
    wkv = p['w_ukv'].reshape(KV_LORA, H, QK_NOPE + V_HEAD)
    wk_lat = jnp.pad(wkv[:, :, :QK_NOPE], ((0, 0), (0, 0), (0, HEAD_PAD - QK_NOPE)))
    place = jnp.zeros((LANES, H, HEAD_PAD), F32).at[:QK_ROPE, :, QK_NOPE:QK_HEAD].set(
        jnp.broadcast_to(jnp.eye(QK_ROPE, dtype=F32)[:, None, :], (QK_ROPE, H, QK_ROPE)))
    wk = jnp.concatenate([wk_lat, place], axis=0).reshape(KV_LORA + LANES, H * HEAD_PAD)
    wv = wkv[:, :, QK_NOPE:].reshape(KV_LORA, H * V_HEAD)
    gq = jnp.pad(p['q_norm'], (0, HEAD_PAD - QK_HEAD)).reshape(1, HEAD_PAD)
    gk = jnp.pad(p['k_norm'], (0, HEAD_PAD - QK_HEAD)).reshape(1, HEAD_PAD)
    bf = lambda w: w.astype(BF16)
    return (bf(wq), bf(_rot_cols(wq)), bf(wk), bf(_rot_cols(wk)), bf(wv), gq, _swap_halves(gq), gk, _swap_halves(gk))


def _mla_tables(n_ctx, T):
    cos, sin = _rope_tables(T)
    c = jnp.ones((n_ctx + T, HEAD_PAD), F32).at[n_ctx:, X1].set(cos).at[n_ctx:, X2].set(cos)
    s = jnp.zeros((n_ctx + T, HEAD_PAD), F32).at[n_ctx:, X1].set(sin).at[n_ctx:, X2].set(sin)
    return c, s


def _mla_prep_kernel(x_ref, c_ref, s_ref, qlg_ref, kvg_ref, wq_ref, wqr_ref, wk_ref, wkr_ref, wv_ref,
                     gq_ref, gqp_ref, gk_ref, gkp_ref, q_ref, k_ref, v_ref):
    x = x_ref[0]
    ql = _rms(x[:, :Q_LORA], qlg_ref[...])
    kvl = _rms(x[:, Q_LORA:Q_LORA + KV_LORA], kvg_ref[...])
    k_in = jnp.concatenate([kvl, x[:, Q_LORA + KV_LORA:]], axis=1)
    cos, sin = c_ref[...], s_ref[...]

    def finish(raw, partner, g, g_swapped, scale, o_ref):
        gc, gs = g * cos, g_swapped * sin
        for h in range(MLA_HEADS):
            sl = slice(h * HEAD_PAD, (h + 1) * HEAD_PAD)
            rh = raw[:, sl]
            inv = lax.rsqrt(jnp.sum(rh * rh, axis=-1, keepdims=True) * (1.0 / QK_HEAD) + EPS) * scale
            o_ref[0, :, sl] = ((rh * gc + partner[:, sl] * gs) * inv).astype(o_ref.dtype)

    finish(_dot(ql, wq_ref[...]), _dot(ql, wqr_ref[...]), gq_ref[...], gqp_ref[...], ATTN_SCALE, q_ref)
    finish(_dot(k_in, wk_ref[...]), _dot(k_in, wkr_ref[...]), gk_ref[...], gkp_ref[...], 1.0, k_ref)
    v_ref[0] = _dot(kvl, wv_ref[...]).astype(v_ref.dtype)


def _mla_prep(proj, p, n_ctx, T):
    B, Tt, _ = proj.shape
    H = MLA_HEADS
    ctx_tiles = n_ctx // ROW_TILE
    mla_blk = (PROJ_RW + PROJ_GATE) // PROJ_MLA
    weights = _mla_weights(p)
    cos, sin = _mla_tables(n_ctx, T)
    const = lambda b, i: (0, 0)
    row = lambda b, i: (b, i, 0)
    tab = pl.BlockSpec((ROW_TILE, HEAD_PAD), lambda b, i: (i, 0))
    full = lambda a: pl.BlockSpec(a.shape, const)
    return pl.pallas_call(
        _mla_prep_kernel,
        grid=(B, Tt // ROW_TILE),
        in_specs=[pl.BlockSpec((1, ROW_TILE, PROJ_MLA), lambda b, i: (b, i, mla_blk)), tab, tab,
                  pl.BlockSpec((1, Q_LORA), const), pl.BlockSpec((1, KV_LORA), const)]
                 + [full(w) for w in weights],
        out_specs=[
            pl.BlockSpec((1, ROW_TILE, H * HEAD_PAD), lambda b, i: (b, jnp.maximum(i - ctx_tiles, 0), 0)),
            pl.BlockSpec((1, ROW_TILE, H * HEAD_PAD), row),
            pl.BlockSpec((1, ROW_TILE, H * V_HEAD), row),
        ],
        out_shape=[
            jax.ShapeDtypeStruct((B, T, H * HEAD_PAD), BF16),
            jax.ShapeDtypeStruct((B, Tt, H * HEAD_PAD), BF16),
            jax.ShapeDtypeStruct((B, Tt, H * V_HEAD), BF16),
        ],
        compiler_params=pltpu.CompilerParams(
            dimension_semantics=("arbitrary", "arbitrary"), vmem_limit_bytes=VMEM_LIMIT),
        name="mla_prep",
    )(proj, cos, sin, p['q_lat_norm'].reshape(1, Q_LORA), p['kv_lat_norm'].reshape(1, KV_LORA), *weights)


ATTN_Q_TILE = 512
HEADS_PER_STEP = LANES // V_HEAD


def _attn_kernel(q_ref, k_ref, v_ref, o_ref):
    v2 = v_ref[0]
    outs = []
    for hh in range(HEADS_PER_STEP):
        sl = slice(hh * HEAD_PAD, (hh + 1) * HEAD_PAD)
        s = _dot_t(q_ref[0, :, sl], k_ref[0, :, sl])
        m = jnp.max(s, axis=-1, keepdims=True)
        e = jnp.exp(s - m)
        l = jnp.sum(e, axis=-1, keepdims=True)
        outs.append(_dot(e, v2) / l)
    lane = lax.broadcasted_iota(jnp.int32, outs[0].shape, 1)
    o_ref[0] = jnp.where(lane < V_HEAD, outs[0], outs[1])


def _attention(q, k, v):
    B, T, _ = q.shape
    Kt = k.shape[1]
    hp = MLA_HEADS // HEADS_PER_STEP
    qw = HEADS_PER_STEP * HEAD_PAD
    return pl.pallas_call(
        _attn_kernel,
        grid=(B, hp, T // ATTN_Q_TILE),
        in_specs=[
            pl.BlockSpec((1, ATTN_Q_TILE, qw), lambda b, h, i: (b, i, h)),
            pl.BlockSpec((1, Kt, qw), lambda b, h, i: (b, 0, h)),
            pl.BlockSpec((1, Kt, LANES), lambda b, h, i: (b, 0, h)),
        ],
        out_specs=pl.BlockSpec((1, ATTN_Q_TILE, LANES), lambda b, h, i: (b, i, h)),
        out_shape=jax.ShapeDtypeStruct((B, T, MLA_HEADS * V_HEAD), F32),
        compiler_params=pltpu.CompilerParams(
            dimension_semantics=("arbitrary", "arbitrary", "arbitrary"), vmem_limit_bytes=VMEM_LIMIT),
        name="attention",
    )(q, k, v)


HALO = 8
LORA_W = LANES
GATE_W = PROJ_RW - 3 * RW_DIM - 2 * LORA_W


def _head_ones(width, head):
    i = np.arange(width) // head
    return jnp.asarray(i[:, None] == i[None, :], BF16)


def _head_sum(x, ones_bd):
    hi = x.astype(BF16)
    lo = (x - hi.astype(F32)).astype(BF16)
    return (jnp.dot(hi, ones_bd, preferred_element_type=F32)
            + jnp.dot(lo, ones_bd, preferred_element_type=F32))


def _dot3(a, b_hi, b_lo):
    hi = a.astype(BF16)
    lo = (a - hi.astype(F32)).astype(BF16)
    d = lambda u, w: jnp.dot(u, w, preferred_element_type=F32)
    return d(hi, b_hi) + (d(hi, b_lo) + d(lo, b_hi))


def _rwkv_prep_kernel(x_ref, prev_ref, next_ref, conv_ref, kkg_ref, ka_ref, rk_ref, w0_ref, a0_ref,
                      w2h_ref, w2l_ref, a2_ref, g2_ref, ones_ref,
                      r_ref, v_ref, kk_ref, lw_ref, bk_ref, kd_ref, bonus_ref, gg_ref, *, ctx_tiles, n_tiles):
    i = pl.program_id(1)
    x = x_ref[0]
    tm, W = x.shape
    C = RW_DIM
    first = (i == 0) | (i == ctx_tiles)
    last = (i == ctx_tiles - 1) | (i == n_tiles - 1)
    prev_row = jnp.where(first, 0.0, prev_ref[0, HALO - 1:HALO])
    next_row = jnp.where(last, 0.0, next_ref[0, 0:1])
    row = lax.broadcasted_iota(jnp.int32, (tm, W), 0)
    x_dn = jnp.where(row == 0, prev_row, pltpu.roll(x, 1, 0))
    x_up = jnp.where(row == tm - 1, next_row, pltpu.roll(x, tm - 1, 0))
    xc = x_dn * conv_ref[0:1] + x * conv_ref[1:2] + x_up * conv_ref[2:3]

    r, k, v = xc[:, :C], xc[:, C:2 * C], xc[:, 2 * C:3 * C]
    lora_w = jnp.tanh(xc[:, 3 * C:3 * C + LORA_W])
    lora_a = xc[:, 3 * C + LORA_W:3 * C + 2 * LORA_W]
    lg = xc[:, 3 * C + 2 * LORA_W:]
    ones_bd = ones_ref[...]
    kq = k * kkg_ref[...]
    kk = kq * lax.rsqrt(_head_sum(kq * kq, ones_bd) + 1e-12)
    r_ref[0], v_ref[0], kk_ref[0] = r, v, kk

    k_sum = jnp.zeros_like(k)
    for d in range(2):
        z = w0_ref[d:d + 1] + _dot3(lora_w, w2h_ref[d], w2l_ref[d])
        softplus_neg = jnp.maximum(-z, 0.0) + jnp.log(1.0 + jnp.exp(-jnp.abs(z)))
        lw_ref[d, 0] = -jnp.exp(-softplus_neg - 0.5)
        a = jax.nn.sigmoid(a0_ref[d:d + 1] + _dot(lora_a, a2_ref[d]))
        kd = k * (1.0 + (a - 1.0) * ka_ref[...])
        bk_ref[d, 0] = kk * a
        kd_ref[d, 0] = kd
        k_sum = k_sum + kd
    bonus_ref[0] = _head_sum(r * k_sum * rk_ref[...], ones_bd) * v
    gg_ref[0] = _dot(jax.nn.sigmoid(lg), g2_ref[...])


def _rwkv_prep(proj, p, n_ctx):
    B, Tt, _ = proj.shape
    C = RW_DIM
    nt = Tt // ROW_TILE
    hb = ROW_TILE // HALO
    pad_cols = lambda w: jnp.pad(w, ((0, 0), (0, PROJ_RW - w.shape[1])))
    w2 = jnp.stack([jnp.pad(p['decay_w2'][0], ((0, LORA_W - DECAY_LORA), (0, 0))),
                    jnp.pad(p['decay_w2'][1], ((DECAY_LORA, 0), (0, 0)))])
    w2h = w2.astype(BF16)
    w2l = (w2 - w2h.astype(F32)).astype(BF16)
    a2 = jnp.stack([jnp.pad(p['aicl_a2'][0], ((0, LORA_W - AICL_LORA), (0, 0))),
                    jnp.pad(p['aicl_a2'][1], ((AICL_LORA, 0), (0, 0)))]).astype(BF16)
    g2 = jnp.pad(p['gate_g2'], ((0, GATE_W - GATE_LORA), (0, 0))).astype(BF16)
    row = lambda b, i: (b, i, 0)
    drow = lambda b, i: (0, b, i, 0)
    const2 = lambda b, i: (0, 0)
    const3 = lambda b, i: (0, 0, 0)
    vec = pl.BlockSpec((1, C), const2)
    out_row = pl.BlockSpec((1, ROW_TILE, C), row)
    out_dir = pl.BlockSpec((2, 1, ROW_TILE, C), drow)
    sds = jax.ShapeDtypeStruct((B, Tt, C), F32)
    sds2 = jax.ShapeDtypeStruct((2, B, Tt, C), F32)
    return pl.pallas_call(
        functools.partial(_rwkv_prep_kernel, ctx_tiles=n_ctx // ROW_TILE, n_tiles=nt),
        grid=(B, nt),
        in_specs=[
            pl.BlockSpec((1, ROW_TILE, PROJ_RW), row),
            pl.BlockSpec((1, HALO, PROJ_RW), lambda b, i: (b, jnp.maximum(i * hb - 1, 0), 0)),
            pl.BlockSpec((1, HALO, PROJ_RW), lambda b, i: (b, jnp.minimum((i + 1) * hb, nt * hb - 1), 0)),
            pl.BlockSpec((3, PROJ_RW), const2),
            vec, vec, vec,
            pl.BlockSpec((2, C), const2),
            pl.BlockSpec((2, C), const2),
            pl.BlockSpec((2, LORA_W, C), const3),
            pl.BlockSpec((2, LORA_W, C), const3),
            pl.BlockSpec((2, LORA_W, C), const3),
            pl.BlockSpec((GATE_W, C), const2),
            pl.BlockSpec((C, C), const2),
        ],
        out_specs=[out_row, out_row, out_row, out_dir, out_dir, out_dir, out_row, out_row],
        out_shape=[sds, sds, sds, sds2, sds2, sds2, sds, sds],
        compiler_params=pltpu.CompilerParams(
            dimension_semantics=("arbitrary", "arbitrary"), vmem_limit_bytes=VMEM_LIMIT),
        name="rwkv_prep",
    )(proj, proj, proj, pad_cols(p['shift_conv']), p['k_k'].reshape(1, C), p['k_a'].reshape(1, C),
      p['r_k'].reshape(1, C), p['decay_w0'], p['aicl_a0'], w2h, w2l, a2, g2, _head_ones(C, RW_HEAD))


def _merge_kernel(att_ref, yf_ref, yb_ref, bonus_ref, gg_ref, gate_ref, x_ref, ga_ref, gnw_ref, gnb_ref, ones_ref,
                  wm_ref, wr_ref, wo_ref, o_ref):
    D = x_ref.shape[-1]
    ones_bd = ones_ref[...]
    y = yf_ref[0] + yb_ref[0]
    yc = y - _head_sum(y, ones_bd) * (1.0 / RW_HEAD)
    var = _head_sum(yc * yc, ones_bd) * (1.0 / RW_HEAD)
    y_n = yc * lax.rsqrt(var + GN_EPS) * gnw_ref[...] + gnb_ref[...]
    rw = (y_n + bonus_ref[0]) * gg_ref[0]
    g = jax.nn.sigmoid(gate_ref[0])
    mix = g[:, :D] * _dot(att_ref[0], wm_ref[...]) + g[:, D:] * _dot(rw, wr_ref[...])
    o_ref[0] = x_ref[0] + ga_ref[0] * _dot(mix, wo_ref[...])


def _merge(att, y_f, y_b, bonus, gg, proj, x, g_a, p, n_ctx):
    B, T, D = x.shape
    C = RW_DIM
    ctx_tiles = n_ctx // ROW_TILE
    gate_blk = PROJ_RW // PROJ_GATE
    row = lambda b, i: (b, i, 0)
    lat_row = lambda b, i: (b, i + ctx_tiles, 0)
    const = lambda b, i: (0, 0)
    return pl.pallas_call(
        _merge_kernel,
        grid=(B, T // ROW_TILE),
        in_specs=[
            pl.BlockSpec((1, ROW_TILE, att.shape[-1]), row),
            pl.BlockSpec((1, ROW_TILE, C), row),
            pl.BlockSpec((1, ROW_TILE, C), row),
            pl.BlockSpec((1, ROW_TILE, C), lat_row),
            pl.BlockSpec((1, ROW_TILE, C), lat_row),
            pl.BlockSpec((1, ROW_TILE, PROJ_GATE), lambda b, i: (b, i + ctx_tiles, gate_blk)),
            pl.BlockSpec((1, ROW_TILE, D), row),
            pl.BlockSpec((1, 1, D), lambda b, i: (b, 0, 0)),
            pl.BlockSpec((1, C), const),
            pl.BlockSpec((1, C), const),
            pl.BlockSpec((C, C), const),
            pl.BlockSpec(p['w_o_mla'].shape, const),
            pl.BlockSpec(p['w_o_rwkv'].shape, const),
            pl.BlockSpec(p['w_out'].shape, const),
        ],
        out_specs=pl.BlockSpec((1, ROW_TILE, D), row),
        out_shape=jax.ShapeDtypeStruct((B, T, D), F32),
        compiler_params=pltpu.CompilerParams(
            dimension_semantics=("arbitrary", "arbitrary"), vmem_limit_bytes=VMEM_LIMIT),
        name="merge",
    )(att, y_f, y_b, bonus, gg, proj, x, g_a[:, None, :], p['gn_w'].reshape(1, C), p['gn_b'].reshape(1, C),
      _head_ones(C, RW_HEAD), p['w_o_mla'].astype(BF16), p['w_o_rwkv'].astype(BF16), p['w_out'].astype(BF16))


MOE_BLOCK = 256


NEG_INF = float("-inf")
HI_MASK = 0xFFFF0000


def _pack_pair(lo, hi):
    lo_b = lax.bitcast_convert_type(lo.astype(BF16).astype(F32), jnp.uint32)
    hi_b = lax.bitcast_convert_type(hi.astype(BF16).astype(F32), jnp.uint32)
    return (lo_b >> 16) | (hi_b & jnp.uint32(HI_MASK))


def _unpack_pair(w):
    lo = lax.bitcast_convert_type(w << 16, F32)
    hi = lax.bitcast_convert_type(w & jnp.uint32(HI_MASK), F32)
    return lo.astype(BF16), hi.astype(BF16)


def _row_max(x):
    return jnp.max(x, axis=-1, keepdims=True)


def _first_index_of(x, value, lane_f):
    return jnp.min(jnp.where(x == value, lane_f, float(x.shape[-1])), axis=-1, keepdims=True)


def _router_kernel(x_ref, gain_ref, sh_ref, sc_ref, wh_ref, wm_ref, wl_ref, bias_ref,
                   u_ref, ids_ref, gates_ref, ranks_ref, counts_ref, carry_ref):
    @pl.when((pl.program_id(0) == 0) & (pl.program_id(1) == 0))
    def _():
        carry_ref[...] = jnp.zeros_like(carry_ref)

    x = x_ref[0]
    tm, D = x.shape
    E = bias_ref.shape[-1]
    u = x * lax.rsqrt(jnp.mean(x * x, axis=-1, keepdims=True) + EPS) * gain_ref[...]
    u = u * (1.0 + sc_ref[0]) + sh_ref[0]
    u_ref[0] = _pack_pair(u[:, :D // 2], u[:, D // 2:])

    uh, um, ul = _split3(u)
    wh, wm, wl = wh_ref[...], wm_ref[...], wl_ref[...]
    d = lambda a, b: jnp.dot(a, b, preferred_element_type=F32)
    logits = d(uh, wh) + (d(uh, wm) + d(um, wh)) + (d(uh, wl) + d(um, wm) + d(ul, wh))
    scores = jax.nn.sigmoid(logits)
    sel = scores + bias_ref[...]

    lane_i = lax.broadcasted_iota(jnp.int32, (tm, E), 1)
    lane_f = lane_i.astype(F32)
    out_f = lax.broadcasted_iota(jnp.int32, (tm, LANES), 1).astype(F32)
    per_group = E // N_GROUPS
    grp_f = jnp.floor(lane_f * (1.0 / per_group))

    gs = jnp.full((tm, LANES), NEG_INF, F32)
    for g in range(N_GROUPS):
        sg = jnp.where(lane_i >= g * per_group, jnp.where(lane_i < (g + 1) * per_group, sel, NEG_INF), NEG_INF)
        m1 = _row_max(sg)
        i1 = _first_index_of(sg, m1, lane_f)
        m2 = _row_max(jnp.where(lane_f == i1, NEG_INF, sg))
        gs = jnp.where(out_f == g, m1 + m2, gs)

    allow = jnp.zeros((tm, E), F32)
    for _ in range(TOPK_GROUPS):
        m = _row_max(gs)
        i = _first_index_of(gs, m, out_f)
        gs = jnp.where(out_f == i, NEG_INF, gs)
        allow = jnp.where(grp_f == i, 1.0, allow)
    selm = jnp.where(allow > 0.0, sel, NEG_INF)

    ids = jnp.zeros((tm, LANES), F32)
    gts = jnp.zeros((tm, LANES), F32)
    member = jnp.zeros((tm, E), F32)
    idx_cols = []
    gsum = jnp.zeros((tm, 1), F32)
    for k in range(TOP_K):
        m = _row_max(selm)
        i = _first_index_of(selm, m, lane_f)
        hit = lane_f == i
        gk = jnp.sum(jnp.where(hit, scores, 0.0), axis=-1, keepdims=True)
        selm = jnp.where(hit, NEG_INF, selm)
        member = jnp.where(hit, 1.0, member)
        ids = jnp.where(out_f == k, i, ids)
        gts = jnp.where(out_f == k, gk, gts)
        idx_cols.append(i)
        gsum = gsum + gk
    gts = gts / gsum * ROUTED_SCALE

    r2 = lax.broadcasted_iota(jnp.int32, (tm, tm), 0)
    c2 = lax.broadcasted_iota(jnp.int32, (tm, tm), 1)
    before = jnp.where(r2 > c2, 1.0, 0.0).astype(BF16)
    mem_b = member.astype(BF16)
    carry = carry_ref[...]
    pos = carry + jnp.dot(before, mem_b, preferred_element_type=F32)
    rk = jnp.zeros((tm, LANES), F32)
    for k in range(TOP_K):
        rk = jnp.where(out_f == k, jnp.sum(jnp.where(lane_f == idx_cols[k], pos, 0.0), axis=-1, keepdims=True), rk)
    colsum = jnp.dot(jnp.ones((8, tm), BF16), mem_b, preferred_element_type=F32)[0:1]
    carry_ref[...] = carry + colsum
    counts_ref[...] = carry + colsum
    ids_ref[0] = ids.astype(jnp.int32)
    ranks_ref[0] = rk.astype(jnp.int32)
    gates_ref[0] = gts


def _router(x1, sh_m, sc_m, p):
    B, T, D = x1.shape
    E = N_EXPERTS
    wh, wm, wl = _split3(p['router_w'])
    row = lambda b, i: (b, i, 0)
    const = lambda b, i: (0, 0)
    vec = lambda b, i: (b, 0, 0)
    lane_out = lambda dt: jax.ShapeDtypeStruct((B, T, LANES), dt)
    return pl.pallas_call(
        _router_kernel,
        grid=(B, T // ROW_TILE),
        in_specs=[
            pl.BlockSpec((1, ROW_TILE, D), row),
            pl.BlockSpec((1, D), const),
            pl.BlockSpec((1, 1, D), vec),
            pl.BlockSpec((1, 1, D), vec),
            pl.BlockSpec((D, E), const),
            pl.BlockSpec((D, E), const),
            pl.BlockSpec((D, E), const),
            pl.BlockSpec((1, E), const),
        ],
        out_specs=[
            pl.BlockSpec((1, ROW_TILE, D // 2), row),
            pl.BlockSpec((1, ROW_TILE, LANES), row),
            pl.BlockSpec((1, ROW_TILE, LANES), row),
            pl.BlockSpec((1, ROW_TILE, LANES), row),
            pl.BlockSpec((1, E), const),
        ],
        out_shape=[
            jax.ShapeDtypeStruct((B, T, D // 2), jnp.uint32),
            lane_out(jnp.int32), lane_out(F32), lane_out(jnp.int32),
            jax.ShapeDtypeStruct((1, E), F32),
        ],
        scratch_shapes=[pltpu.VMEM((1, E), F32)],
        compiler_params=pltpu.CompilerParams(
            dimension_semantics=("arbitrary", "arbitrary"), vmem_limit_bytes=VMEM_LIMIT),
        name="router",
    )(x1, p['norm_ffn'].reshape(1, D), sh_m[:, None, :], sc_m[:, None, :], wh, wm, wl,
      p['router_bias'].reshape(1, E))


def _slot_kernel(ids_ref, ranks_ref, base_ref, o_ref):
    ids = ids_ref[...].astype(F32)
    tm = ids.shape[0]
    E = base_ref.shape[-1]
    lane_e = lax.broadcasted_iota(jnp.int32, (tm, E), 1).astype(F32)
    out_lane = lax.broadcasted_iota(jnp.int32, (tm, LANES), 1)
    first = jnp.zeros((tm, LANES), F32)
    for k in range(TOP_K):
        fk = jnp.sum(jnp.where(lane_e == ids[:, k:k + 1], base_ref[...], 0.0), axis=-1, keepdims=True)
        first = jnp.where(out_lane == k, fk, first)
    o_ref[...] = first.astype(jnp.int32) + ranks_ref[...]


def _slots(ids, ranks, base):
    n_tok = ids.shape[0]
    E = base.shape[-1]
    row = pl.BlockSpec((ROW_TILE, LANES), lambda i: (i, 0))
    return pl.pallas_call(
        _slot_kernel,
        grid=(n_tok // ROW_TILE,),
        in_specs=[row, row, pl.BlockSpec((1, E), lambda i: (0, 0))],
        out_specs=row,
        out_shape=jax.ShapeDtypeStruct((n_tok, LANES), jnp.int32),
        compiler_params=pltpu.CompilerParams(dimension_semantics=("arbitrary",)),
        name="slots",
    )(ids, ranks, base)


def _row_copy(src, src_row, dst, dst_row, sem):
    return pltpu.make_async_copy(src.at[pl.ds(src_row, 1)], dst.at[pl.ds(dst_row, 1)], sem)


def _dispatch_kernel(dest_ref, u_ref, slots_in_ref, slots_ref, sem):
    del slots_in_ref
    tm = u_ref.shape[0]

    def issue(r, carry):
        for k in range(TOP_K):
            _row_copy(u_ref, r, slots_ref, dest_ref[0, 0, r * TOP_K + k], sem).start()
        return carry

    lax.fori_loop(0, tm, issue, 0)

    def drain(r, carry):
        for k in range(TOP_K):
            _row_copy(u_ref, r, slots_ref, 0, sem).wait()
        return carry

    lax.fori_loop(0, tm, drain, 0)


def _dispatch(dest, u_rows, n_slots):
    n_tok, W = u_rows.shape
    nt = n_tok // ROW_TILE
    return pl.pallas_call(
        _dispatch_kernel,
        grid=(nt,),
        in_specs=[
            pl.BlockSpec((1, 1, ROW_TILE * TOP_K), lambda i: (i, 0, 0), memory_space=pltpu.SMEM),
            pl.BlockSpec((ROW_TILE, W), lambda i: (i, 0)),
            pl.BlockSpec(memory_space=pl.ANY),
        ],
        out_specs=pl.BlockSpec(memory_space=pl.ANY),
        out_shape=jax.ShapeDtypeStruct((n_slots, W), jnp.uint32),
        scratch_shapes=[pltpu.SemaphoreType.DMA(())],
        input_output_aliases={2: 0},
        compiler_params=pltpu.CompilerParams(
            dimension_semantics=("arbitrary",), vmem_limit_bytes=VMEM_LIMIT),
        name="dispatch",
    )(dest, u_rows, jnp.zeros((n_slots, W), jnp.uint32))


def _moe_ffn_kernel(be_ref, nu_ref, x_ref, w1_ref, w3_ref, w2_ref, o_ref, w13_ref, w2b_ref):
    j = pl.program_id(0)
    F = w1_ref.shape[-1]
    half = x_ref.shape[-1]

    @pl.when((j == 0) | (be_ref[j] != be_ref[jnp.maximum(j - 1, 0)]))
    def _():
        w13_ref[:, :F] = w1_ref[0].astype(BF16)
        w13_ref[:, F:] = w3_ref[0].astype(BF16)
        w2b_ref[...] = w2_ref[0].astype(BF16)

    @pl.when(j < nu_ref[0])
    def _():
        lo, hi = _unpack_pair(x_ref[...])
        h = (jnp.dot(lo, w13_ref[:half], preferred_element_type=F32)
             + jnp.dot(hi, w13_ref[half:], preferred_element_type=F32))
        h1, h3 = h[:, :F], h[:, F:]
        y = _dot(h1 * jax.nn.sigmoid(h1) * h3, w2b_ref[...])
        o_ref[...] = _pack_pair(y[:, :half], y[:, half:])

    @pl.when(j >= nu_ref[0])
    def _():
        o_ref[...] = jnp.zeros_like(o_ref)


def _moe_ffn(block_e, n_used, x_slots, w1, w3, w2):
    slots, W = x_slots.shape
    E, D, F = w1.shape
    nblk = slots // MOE_BLOCK
    grid_spec = pltpu.PrefetchScalarGridSpec(
        num_scalar_prefetch=2,
        grid=(nblk,),
        in_specs=[
            pl.BlockSpec((MOE_BLOCK, W), lambda j, be, nu: (jnp.minimum(j, nu[0] - 1), 0)),
            pl.BlockSpec((1, D, F), lambda j, be, nu: (be[j], 0, 0)),
            pl.BlockSpec((1, D, F), lambda j, be, nu: (be[j], 0, 0)),
            pl.BlockSpec((1, F, D), lambda j, be, nu: (be[j], 0, 0)),
        ],
        out_specs=pl.BlockSpec((MOE_BLOCK, W), lambda j, be, nu: (j, 0)),
        scratch_shapes=[pltpu.VMEM((D, 2 * F), BF16), pltpu.VMEM((F, D), BF16)],
    )
    return pl.pallas_call(
        _moe_ffn_kernel,
        grid_spec=grid_spec,
        out_shape=jax.ShapeDtypeStruct((slots, W), jnp.uint32),
        compiler_params=pltpu.CompilerParams(
            dimension_semantics=("arbitrary",), vmem_limit_bytes=VMEM_LIMIT),
        name="moe_ffn",
    )(block_e, n_used, x_slots, w1, w3, w2)


def _combine_kernel(dest_ref, gates_ref, u_ref, x_ref, gm_ref, w1_ref, w3_ref, w2_ref, ys_ref, o_ref, buf_ref, sem):
    tm, half = u_ref.shape

    def issue(r, carry):
        for k in range(TOP_K):
            _row_copy(ys_ref, dest_ref[0, 0, r * TOP_K + k], buf_ref.at[k], r, sem).start()
        return carry

    lax.fori_loop(0, tm, issue, 0)

    ulo, uhi = _unpack_pair(u_ref[...])
    both = lambda w_ref: (jnp.dot(ulo, w_ref[:half], preferred_element_type=F32)
                          + jnp.dot(uhi, w_ref[half:], preferred_element_type=F32))
    h1, h3 = both(w1_ref), both(w3_ref)
    shared = _dot(h1 * jax.nn.sigmoid(h1) * h3, w2_ref[...])

    def drain(r, carry):
        for k in range(TOP_K):
            _row_copy(ys_ref, 0, buf_ref.at[k], r, sem).wait()
        return carry

    lax.fori_loop(0, tm, drain, 0)

    gates = gates_ref[...]
    lo_acc = jnp.zeros((tm, half), F32)
    hi_acc = jnp.zeros((tm, half), F32)
    for k in range(TOP_K):
        lo, hi = _unpack_pair(buf_ref[k])
        gk = gates[:, k:k + 1]
        lo_acc = lo_acc + gk * lo.astype(F32)
        hi_acc = hi_acc + gk * hi.astype(F32)
    routed = jnp.concatenate([lo_acc, hi_acc], axis=1)
    o_ref[...] = x_ref[...] + gm_ref[0] * (routed + shared)


def _combine(dest, gates, u_rows, x1, g_m, y_slots, p):
    B, T, D = x1.shape
    n_tok = B * T
    W = u_rows.shape[1]
    tiles_per_batch = T // ROW_TILE
    row = lambda i: (i, 0)
    const = lambda i: (0, 0)
    out = pl.pallas_call(
        _combine_kernel,
        grid=(n_tok // ROW_TILE,),
        in_specs=[
            pl.BlockSpec((1, 1, ROW_TILE * TOP_K), lambda i: (i, 0, 0), memory_space=pltpu.SMEM),
            pl.BlockSpec((ROW_TILE, LANES), row),
            pl.BlockSpec((ROW_TILE, W), row),
            pl.BlockSpec((ROW_TILE, D), row),
            pl.BlockSpec((1, 1, D), lambda i: (i // tiles_per_batch, 0, 0)),
            pl.BlockSpec(p['shared_w1'].shape, const),
            pl.BlockSpec(p['shared_w3'].shape, const),
            pl.BlockSpec(p['shared_w2'].shape, const),
            pl.BlockSpec(memory_space=pl.ANY),
        ],
        out_specs=pl.BlockSpec((ROW_TILE, D), row),
        out_shape=jax.ShapeDtypeStruct((n_tok, D), F32),
        scratch_shapes=[pltpu.VMEM((TOP_K, ROW_TILE, W), jnp.uint32), pltpu.SemaphoreType.DMA(())],
        compiler_params=pltpu.CompilerParams(
            dimension_semantics=("arbitrary",), vmem_limit_bytes=VMEM_LIMIT),
        name="combine",
    )(dest, gates.reshape(n_tok, LANES), u_rows, x1.reshape(n_tok, D), g_m[:, None, :],
      p['shared_w1'].astype(BF16), p['shared_w3'].astype(BF16), p['shared_w2'].astype(BF16), y_slots)
    return out.reshape(B, T, D)


def _moe(x1, sh_m, sc_m, g_m, p):
    B, T, D = x1.shape
    n_tok = B * T
    u_packed, ids, gates, ranks, counts = _router(x1, sh_m, sc_m, p)

    counts = counts[0].astype(jnp.int32)
    padded = (counts + MOE_BLOCK - 1) // MOE_BLOCK * MOE_BLOCK
    padded_end = jnp.cumsum(padded)
    base = padded_end - padded
    n_blocks = n_tok * TOP_K // MOE_BLOCK + N_EXPERTS
    n_used = (padded_end[-1] // MOE_BLOCK).astype(jnp.int32)
    blk = jnp.minimum(jnp.arange(n_blocks), n_used - 1) * MOE_BLOCK
    block_e = jnp.minimum(jnp.searchsorted(padded_end, blk, side='right'), N_EXPERTS - 1).astype(jnp.int32)
    dest = _slots(ids.reshape(n_tok, LANES), ranks.reshape(n_tok, LANES), base.astype(F32).reshape(1, N_EXPERTS))
    dest = dest[:, :TOP_K].reshape(n_tok // ROW_TILE, 1, ROW_TILE * TOP_K)

    u_rows = u_packed.reshape(n_tok, D // 2)
    x_slots = _dispatch(dest, u_rows, n_blocks * MOE_BLOCK)
    y_slots = _moe_ffn(block_e, n_used.reshape(1), x_slots, p['expert_w1'], p['expert_w3'], p['expert_w2'])
    return _combine(dest, gates, u_rows, x1, g_m, y_slots, p)
```

```python
import functools

import jax
import jax.numpy as jnp
import numpy as np
from jax import lax
from jax.experimental import pallas as pl
from jax.experimental.pallas import tpu as pltpu

F32 = jnp.float32
BF16 = jnp.bfloat16

GRID_W = 64
EPS = 1e-6
MLA_HEADS = 8
QK_NOPE = 64
QK_ROPE = 32
QK_HEAD = QK_NOPE + QK_ROPE
V_HEAD = 64
Q_LORA = 256
KV_LORA = 128
ROPE_BASE = 10000.0
ATTN_SCALE = QK_HEAD ** -0.5
RW_HEADS = 8
RW_HEAD = 64
RW_DIM = RW_HEADS * RW_HEAD
DECAY_LORA = 64
AICL_LORA = 64
GATE_LORA = 160
GN_EPS = 64e-5
N_EXPERTS = 256
TOP_K = 8
N_GROUPS = 8
TOPK_GROUPS = 4
ROUTED_SCALE = 2.5

LANES = 128
MXU_DIM = 256
VMEM_LIMIT = 56 * 1024 * 1024

WKV_CHUNK = 64
HEADS_PER_GROUP = MXU_DIM // RW_HEAD
HEAD_PAD = LANES


def _dot(a, b):
    return jnp.dot(a.astype(BF16), b.astype(BF16), preferred_element_type=F32)


def _dot_t(a, b):
    return lax.dot_general(a.astype(BF16), b.astype(BF16), (((1,), (1,)), ((), ())),
                           preferred_element_type=F32)


def _split3(x):
    h = x.astype(BF16)
    r1 = x - h.astype(F32)
    m = r1.astype(BF16)
    lo = (r1 - m.astype(F32)).astype(BF16)
    return h, m, lo


def _dot_hi(a_bf16_exact, x):
    h, m, lo = _split3(x)
    d = lambda y: jnp.dot(a_bf16_exact, y, preferred_element_type=F32)
    return d(h) + d(m) + d(lo)


def _wkv_chunk(r, v, kk, lw, bk, kd, s0, reverse, emit):
    L, G = r.shape
    row = lax.broadcasted_iota(jnp.int32, (L, G), 0)
    lane = lax.broadcasted_iota(jnp.int32, (L, G), 1)
    diff = (lane % L - row) if reverse else (row - lane % L)
    strict = diff > 0
    incl = diff >= 0
    r2 = lax.broadcasted_iota(jnp.int32, (L, L), 0)
    c2 = lax.broadcasted_iota(jnp.int32, (L, L), 1)
    tri = jnp.where(((c2 - r2) if reverse else (r2 - c2)) >= 0, 1.0, 0.0).astype(BF16)
    ones = jnp.ones((L, L), BF16)
    lane_head = lane // RW_HEAD
    rowb = lax.broadcasted_iota(jnp.int32, (G, G), 0) // RW_HEAD
    colb = lax.broadcasted_iota(jnp.int32, (G, G), 1) // RW_HEAD

    def stack(x):
        return jnp.concatenate(
            [jnp.where(lane_head == h, x, 0.0) for h in range(HEADS_PER_GROUP)], axis=0).astype(BF16)

    cum_in = _dot_hi(tri, lw)
    g_in = jnp.exp(cum_in)
    g_inv = jnp.exp(-cum_in)
    g_ex = jnp.exp(cum_in - lw)
    g_tot = jnp.exp(_dot_hi(ones, lw))
    a_h = -kk * g_ex
    b_h = bk * g_inv
    k_h = kd * g_inv
    r_h = r * g_in
    yield

    gram = _dot_t(jnp.concatenate([a_h, r_h], axis=0),
                  jnp.concatenate([stack(b_h), stack(k_h)], axis=0))
    ab = jnp.where(strict, gram[:L, :G], 0.0)
    ak = jnp.where(strict, gram[:L, G:], 0.0)
    rb = jnp.where(incl, gram[L:, :G], 0.0)
    rk = jnp.where(incl, gram[L:, G:], 0.0)
    yield

    tm = jnp.where(diff == 0, 1.0, 0.0) + ab
    p = ab
    v_st = stack(v)
    akv = _dot(ak, v_st)
    for _ in range(int(np.log2(L)) - 1):
        p = _dot(p, stack(p))
        yield
        tm = tm + _dot(tm, stack(p))
        yield

    wu = _dot(tm, jnp.concatenate([stack(a_h), stack(akv)], axis=1))
    w_t, u_t = wu[:, :G], wu[:, G:]
    yield
    rbwu = _dot(rb, jnp.concatenate([stack(w_t), stack(u_t)], axis=1))
    r_t = r_h + rbwu[:, :G]
    y_t = rbwu[:, G:] + _dot(rk, v_st)
    yield

    s0_st = stack(s0)
    y = _dot_t(r_t, s0_st) + y_t
    u = _dot_t(w_t, s0_st) + u_t
    yield
    uv = jnp.concatenate([u, v], axis=0)
    bkc = jnp.concatenate([b_h, k_h], axis=0)
    upd = lax.dot_general(uv.astype(BF16), bkc.astype(BF16), (((0,), (0,)), ((), ())),
                          preferred_element_type=F32)
    upd = jnp.where(rowb == colb, upd, 0.0)
    upd_d = upd[0:L]
    for h in range(1, HEADS_PER_GROUP):
        upd_d = upd_d + upd[h * L:(h + 1) * L]
    emit(y, (s0 + upd_d) * g_tot)


def _wkv_kernel(*refs, n_groups, n_batch):
    ins, (yf_ref, yb_ref, s_ref) = refs[:12], refs[12:]
    G = MXU_DIM

    @pl.when(pl.program_id(1) == 0)
    def _():
        s_ref[...] = jnp.zeros_like(s_ref)

    chains = []
    for d, y_ref in enumerate((yf_ref, yb_ref)):
        r_ref, v_ref, kk_ref, lw_ref, bk_ref, kd_ref = ins[6 * d:6 * d + 6]
        for bi in range(n_batch):
            for g in range(n_groups):
                sl = slice(g * G, (g + 1) * G)

                def emit(y, s_new, y_ref=y_ref, d=d, bi=bi, g=g, sl=sl):
                    y_ref[bi, :, sl] = y
                    s_ref[d, bi, g] = s_new

                chains.append(_wkv_chunk(
                    r_ref[bi, :, sl], v_ref[bi, :, sl], kk_ref[bi, :, sl], lw_ref[0, bi, :, sl],
                    bk_ref[0, bi, :, sl], kd_ref[0, bi, :, sl], s_ref[d, bi, g], d == 1, emit))
    while chains:
        chains = [c for c in chains if next(c, StopIteration) is not StopIteration]


WKV_BATCH = 2


def _wkv_scan(r, v, kk, lw, bk, kd, n_ctx):
    B, Ttot, C = r.shape
    L = WKV_CHUNK
    nc = Ttot // L
    ncc = n_ctx // L
    nl = nc - ncc
    n_groups = C // MXU_DIM
    nb = WKV_BATCH

    cid = (lambda s: s, lambda s: jnp.where(s < ncc, ncc - 1 - s, nc + ncc - 1 - s))
    first_out = (0, nl - 1)
    in_specs, out_specs = [], []
    for d in range(2):
        shared = pl.BlockSpec((nb, L, C), lambda b, s, d=d: (b, cid[d](s), 0))
        per_dir = pl.BlockSpec((1, nb, L, C), lambda b, s, d=d: (d, b, cid[d](s), 0))
        in_specs += [shared, shared, shared, per_dir, per_dir, per_dir]
        out_specs.append(pl.BlockSpec(
            (nb, L, C), lambda b, s, d=d: (b, jnp.where(s < ncc, first_out[d], cid[d](s) - ncc), 0)))
    out = jax.ShapeDtypeStruct((B, nl * L, C), F32)
    return pl.pallas_call(
        functools.partial(_wkv_kernel, n_groups=n_groups, n_batch=nb),
        grid=(B // nb, nc),
        in_specs=in_specs,
        out_specs=out_specs,
        out_shape=[out, out],
        scratch_shapes=[pltpu.VMEM((2, nb, n_groups, L, MXU_DIM), F32)],
        compiler_params=pltpu.CompilerParams(
            dimension_semantics=("arbitrary", "arbitrary"), vmem_limit_bytes=VMEM_LIMIT),
        name="wkv_scan",
    )(r, v, kk, lw, bk, kd, r, v, kk, lw, bk, kd)


def kernel(x, c, ctx, c_ctx, ada_w, ada_b, norm_mix, norm_ffn, w_in, shift_conv, q_lat_norm, w_uq, kv_lat_norm, w_ukv, q_norm, k_norm, w_o_mla, decay_w0, decay_w2, aicl_a0, aicl_a2, k_k, k_a, r_k, gn_w, gn_b, gate_g2, w_o_rwkv, w_out, router_w, router_bias, expert_w1, expert_w3, expert_w2, shared_w1, shared_w3, shared_w2):
    B, T, D = x.shape
    n_ctx = ctx.shape[1]
    i0 = 0
    p = dict(norm_mix=norm_mix[i0], norm_ffn=norm_ffn[i0], w_in=w_in[i0], shift_conv=shift_conv[i0],
             q_lat_norm=q_lat_norm[i0], w_uq=w_uq[i0], kv_lat_norm=kv_lat_norm[i0], w_ukv=w_ukv[i0],
             q_norm=q_norm[i0], k_norm=k_norm[i0], w_o_mla=w_o_mla[i0],
             decay_w0=decay_w0[i0], decay_w2=decay_w2[i0], aicl_a0=aicl_a0[i0], aicl_a2=aicl_a2[i0],
             k_k=k_k[i0], k_a=k_a[i0], r_k=r_k[i0], gn_w=gn_w[i0], gn_b=gn_b[i0], gate_g2=gate_g2[i0],
             w_o_rwkv=w_o_rwkv[i0], w_out=w_out[i0], router_w=router_w[i0], router_bias=router_bias[i0],
             expert_w1=expert_w1[i0], expert_w3=expert_w3[i0], expert_w2=expert_w2[i0],
             shared_w1=shared_w1[i0], shared_w3=shared_w3[i0], shared_w2=shared_w2[i0])

    mod = _ada_modulation(c, c_ctx, ada_w[i0], ada_b[i0])
    sh_a, sc_a, g_a, sh_m, sc_m, g_m = jnp.split(mod[:B], 6, axis=-1)
    csh_a, csc_a = jnp.split(mod[B], 6, axis=-1)[:2]
    sh2 = jnp.stack([jnp.broadcast_to(csh_a, (B, D)), sh_a], axis=1)
    sc2 = jnp.stack([jnp.broadcast_to(csc_a, (B, D)), sc_a], axis=1)

    proj = _in_proj(ctx, x, sh2, sc2, p['norm_mix'], _pack_w_in(p['w_in']))

    q, k, v = _mla_prep(proj, p, n_ctx, T)
    att = _attention(q, k, v)

    r, vv, kk, lw, bk, kd, bonus, gg = _rwkv_prep(proj, p, n_ctx)
    y_f, y_b = _wkv_scan(r, vv, kk, lw, bk, kd, n_ctx)

    x1 = _merge(att, y_f, y_b, bonus, gg, proj, x, g_a, p, n_ctx)
    return _moe(x1, sh_m, sc_m, g_m, p)


SUBLANES = 8


def _ada_kernel(c_ref, w_ref, b_ref, o_ref):
    cc = c_ref[...]
    w = w_ref[...]
    w_hi = w.astype(BF16)
    w_lo = (w - w_hi.astype(F32)).astype(BF16)
    o_ref[...] = _dot3(cc * jax.nn.sigmoid(cc), w_hi, w_lo) + b_ref[...]


def _ada_modulation(c, c_ctx, ada_w, ada_b):
    B, D = c.shape
    n_out = ada_w.shape[1]
    rows = -(-(B + 1) // SUBLANES) * SUBLANES
    c_all = jnp.concatenate([c, c_ctx[None, :], jnp.zeros((rows - B - 1, D), F32)], axis=0)
    out = pl.pallas_call(
        _ada_kernel,
        grid=(n_out // D,),
        in_specs=[pl.BlockSpec((rows, D), lambda j: (0, 0)),
                  pl.BlockSpec((D, D), lambda j: (0, j)),
                  pl.BlockSpec((1, D), lambda j: (0, j))],
        out_specs=pl.BlockSpec((rows, D), lambda j: (0, j)),
        out_shape=jax.ShapeDtypeStruct((rows, n_out), F32),
        compiler_params=pltpu.CompilerParams(dimension_semantics=("arbitrary",), vmem_limit_bytes=VMEM_LIMIT),
        name="ada_modulation",
    )(c_all, ada_w, ada_b.reshape(1, n_out))
    return out[:B + 1]


ROW_TILE = 256
PROJ_RW = 2048
PROJ_GATE = 2048
PROJ_MLA = 512
PROJ_W = PROJ_RW + PROJ_GATE + PROJ_MLA
MLA_IN = Q_LORA + KV_LORA + QK_ROPE
RW_SPLITS = (RW_DIM, RW_DIM, RW_DIM, DECAY_LORA, DECAY_LORA, AICL_LORA, AICL_LORA, GATE_LORA)
RW_IN = sum(RW_SPLITS)


def _pack_w_in(w_in):
    w_mla = w_in[:, :MLA_IN]
    w_rw = w_in[:, MLA_IN:MLA_IN + RW_IN]
    w_gate = w_in[:, MLA_IN + RW_IN:]
    pad = lambda w, n: jnp.pad(w, ((0, 0), (0, n - w.shape[1])))
    return jnp.concatenate([pad(w_rw, PROJ_RW), w_gate, pad(w_mla, PROJ_MLA)], axis=1).astype(BF16)


def _in_proj_kernel(ctx_ref, x_ref, sh_ref, sc_ref, gain_ref, w_ref, o_ref):
    is_ctx = pl.program_id(1) == 0
    xt = jnp.where(is_ctx, ctx_ref[0], x_ref[0])
    sh = jnp.where(is_ctx, sh_ref[0, 0:1], sh_ref[0, 1:2])
    sc = jnp.where(is_ctx, sc_ref[0, 0:1], sc_ref[0, 1:2])
    y = xt * lax.rsqrt(jnp.mean(xt * xt, axis=-1, keepdims=True) + EPS) * gain_ref[...]
    h = y * (1.0 + sc) + sh
    o_ref[0] = _dot(h, w_ref[...])


def _in_proj(ctx, x, sh2, sc2, gain, w):
    B, T, D = x.shape
    n_ctx = ctx.shape[1]
    assert n_ctx == ROW_TILE and T % ROW_TILE == 0
    nt = 1 + T // ROW_TILE
    return pl.pallas_call(
        _in_proj_kernel,
        grid=(B, nt),
        in_specs=[
            pl.BlockSpec((1, ROW_TILE, D), lambda b, i: (b, 0, 0)),
            pl.BlockSpec((1, ROW_TILE, D), lambda b, i: (b, jnp.maximum(i - 1, 0), 0)),
            pl.BlockSpec((1, 2, D), lambda b, i: (b, 0, 0)),
            pl.BlockSpec((1, 2, D), lambda b, i: (b, 0, 0)),
            pl.BlockSpec((1, D), lambda b, i: (0, 0)),
            pl.BlockSpec((D, PROJ_W), lambda b, i: (0, 0)),
        ],
        out_specs=pl.BlockSpec((1, ROW_TILE, PROJ_W), lambda b, i: (b, i, 0)),
        out_shape=jax.ShapeDtypeStruct((B, n_ctx + T, PROJ_W), F32),
        compiler_params=pltpu.CompilerParams(
            dimension_semantics=("arbitrary", "arbitrary"), vmem_limit_bytes=VMEM_LIMIT),
        name="in_proj",
    )(ctx, x, sh2, sc2, gain.reshape(1, D), w)


def _rms(x, gain):
    return x * lax.rsqrt(jnp.mean(x * x, axis=-1, keepdims=True) + EPS) * gain


def _rope_tables(n_tokens):
    rows = n_tokens // GRID_W
    row = jnp.repeat(jnp.arange(rows, dtype=F32), GRID_W)
    col = jnp.tile(jnp.arange(GRID_W, dtype=F32), rows)
    n_freq = QK_ROPE // 4
    inv_freq = ROPE_BASE ** (-jnp.arange(n_freq, dtype=F32) / n_freq)
    ang = jnp.concatenate([row[:, None] * inv_freq, col[:, None] * inv_freq], axis=-1)
    return jnp.cos(ang), jnp.sin(ang)


ROPE_HALF = QK_ROPE // 2
X1 = slice(QK_NOPE, QK_NOPE + ROPE_HALF)
X2 = slice(QK_NOPE + ROPE_HALF, QK_HEAD)


def _rot_cols(w):
    w3 = w.reshape(w.shape[0], MLA_HEADS, HEAD_PAD)
    rot = jnp.zeros_like(w3).at[:, :, X1].set(-w3[:, :, X2]).at[:, :, X2].set(w3[:, :, X1])
    return rot.reshape(w.shape)


def _swap_halves(g):
    return jnp.zeros_like(g).at[:, X1].set(g[:, X2]).at[:, X2].set(g[:, X1])


def _mla_weights(p):
    H = MLA_HEADS
    wq = jnp.pad(p['w_uq'].reshape(Q_LORA, H, QK_HEAD), ((0, 0), (0, 0), (0, HEAD_PAD - QK_HEAD)))
    wq = wq.reshape(Q_LORA, H * HEAD_PAD)
    wkv = p['w_ukv'].reshape(KV_LORA, H, QK_NOPE + V_HEAD)
    wk_lat = jnp.pad(wkv[:, :, :QK_NOPE], ((0, 0), (0, 0), (0, HEAD_PAD - QK_NOPE)))
    place = jnp.zeros((LANES, H, HEAD_PAD), F32).at[:QK_ROPE, :, QK_NOPE:QK_HEAD].set(
        jnp.broadcast_to(jnp.eye(QK_ROPE, dtype=F32)[:, None, :], (QK_ROPE, H, QK_ROPE)))
    wk = jnp.concatenate([wk_lat, place], axis=0).reshape(KV_LORA + LANES, H * HEAD_PAD)
    wv = wkv[:, :, QK_NOPE:].reshape(KV_LORA, H * V_HEAD)
    gq = jnp.pad(p['q_norm'], (0, HEAD_PAD - QK_HEAD)).reshape(1, HEAD_PAD)
    gk = jnp.pad(p['k_norm'], (0, HEAD_PAD - QK_HEAD)).reshape(1, HEAD_PAD)
    bf = lambda w: w.astype(BF16)
    return (bf(wq), bf(_rot_cols(wq)), bf(wk), bf(_rot_cols(wk)), bf(wv), gq, _swap_halves(gq), gk, _swap_halves(gk))


def _mla_tables(n_ctx, T):
    cos, sin = _rope_tables(T)
    c = jnp.ones((n_ctx + T, HEAD_PAD), F32).at[n_ctx:, X1].set(cos).at[n_ctx:, X2].set(cos)
    s = jnp.zeros((n_ctx + T, HEAD_PAD), F32).at[n_ctx:, X1].set(sin).at[n_ctx:, X2].set(sin)
    return c, s


def _mla_prep_kernel(x_ref, c_ref, s_ref, qlg_ref, kvg_ref, wq_ref, wqr_ref, wk_ref, wkr_ref, wv_ref,
                     gq_ref, gqp_ref, gk_ref, gkp_ref, q_ref, k_ref, v_ref):
    x = x_ref[0]
    ql = _rms(x[:, :Q_LORA], qlg_ref[...])
    kvl = _rms(x[:, Q_LORA:Q_LORA + KV_LORA], kvg_ref[...])
    k_in = jnp.concatenate([kvl, x[:, Q_LORA + KV_LORA:]], axis=1)
    cos, sin = c_ref[...], s_ref[...]

    def finish(raw, partner, g, g_swapped, scale, o_ref):
        gc, gs = g * cos, g_swapped * sin
        for h in range(MLA_HEADS):
            sl = slice(h * HEAD_PAD, (h + 1) * HEAD_PAD)
            rh = raw[:, sl]
            inv = lax.rsqrt(jnp.sum(rh * rh, axis=-1, keepdims=True) * (1.0 / QK_HEAD) + EPS) * scale
            o_ref[0, :, sl] = ((rh * gc + partner[:, sl] * gs) * inv).astype(o_ref.dtype)

    finish(_dot(ql, wq_ref[...]), _dot(ql, wqr_ref[...]), gq_ref[...], gqp_ref[...], ATTN_SCALE, q_ref)
    finish(_dot(k_in, wk_ref[...]), _dot(k_in, wkr_ref[...]), gk_ref[...], gkp_ref[...], 1.0, k_ref)
    v_ref[0] = _dot(kvl, wv_ref[...]).astype(v_ref.dtype)


def _mla_prep(proj, p, n_ctx, T):
    B, Tt, _ = proj.shape
    H = MLA_HEADS
    ctx_tiles = n_ctx // ROW_TILE
    mla_blk = (PROJ_RW + PROJ_GATE) // PROJ_MLA
    weights = _mla_weights(p)
    cos, sin = _mla_tables(n_ctx, T)
    const = lambda b, i: (0, 0)
    row = lambda b, i: (b, i, 0)
    tab = pl.BlockSpec((ROW_TILE, HEAD_PAD), lambda b, i: (i, 0))
    full = lambda a: pl.BlockSpec(a.shape, const)
    return pl.pallas_call(
        _mla_prep_kernel,
        grid=(B, Tt // ROW_TILE),
        in_specs=[pl.BlockSpec((1, ROW_TILE, PROJ_MLA), lambda b, i: (b, i, mla_blk)), tab, tab,
                  pl.BlockSpec((1, Q_LORA), const), pl.BlockSpec((1, KV_LORA), const)]
                 + [full(w) for w in weights],
        out_specs=[
            pl.BlockSpec((1, ROW_TILE, H * HEAD_PAD), lambda b, i: (b, jnp.maximum(i - ctx_tiles, 0), 0)),
            pl.BlockSpec((1, ROW_TILE, H * HEAD_PAD), row),
            pl.BlockSpec((1, ROW_TILE, H * V_HEAD), row),
        ],
        out_shape=[
            jax.ShapeDtypeStruct((B, T, H * HEAD_PAD), BF16),
            jax.ShapeDtypeStruct((B, Tt, H * HEAD_PAD), BF16),
            jax.ShapeDtypeStruct((B, Tt, H * V_HEAD), BF16),
        ],
        compiler_params=pltpu.CompilerParams(
            dimension_semantics=("arbitrary", "arbitrary"), vmem_limit_bytes=VMEM_LIMIT),
        name="mla_prep",
    )(proj, cos, sin, p['q_lat_norm'].reshape(1, Q_LORA), p['kv_lat_norm'].reshape(1, KV_LORA), *weights)


ATTN_Q_TILE = 512
HEADS_PER_STEP = LANES // V_HEAD


def _attn_kernel(q_ref, k_ref, v_ref, o_ref):
    v2 = v_ref[0]
    outs = []
    for hh in range(HEADS_PER_STEP):
        sl = slice(hh * HEAD_PAD, (hh + 1) * HEAD_PAD)
        s = _dot_t(q_ref[0, :, sl], k_ref[0, :, sl])
        m = jnp.max(s, axis=-1, keepdims=True)
        e = jnp.exp(s - m)
        l = jnp.sum(e, axis=-1, keepdims=True)
        outs.append(_dot(e, v2) / l)
    lane = lax.broadcasted_iota(jnp.int32, outs[0].shape, 1)
    o_ref[0] = jnp.where(lane < V_HEAD, outs[0], outs[1])


def _attention(q, k, v):
    B, T, _ = q.shape
    Kt = k.shape[1]
    hp = MLA_HEADS // HEADS_PER_STEP
    qw = HEADS_PER_STEP * HEAD_PAD
    return pl.pallas_call(
        _attn_kernel,
        grid=(B, hp, T // ATTN_Q_TILE),
        in_specs=[
            pl.BlockSpec((1, ATTN_Q_TILE, qw), lambda b, h, i: (b, i, h)),
            pl.BlockSpec((1, Kt, qw), lambda b, h, i: (b, 0, h)),
            pl.BlockSpec((1, Kt, LANES), lambda b, h, i: (b, 0, h)),
        ],
        out_specs=pl.BlockSpec((1, ATTN_Q_TILE, LANES), lambda b, h, i: (b, i, h)),
        out_shape=jax.ShapeDtypeStruct((B, T, MLA_HEADS * V_HEAD), F32),
        compiler_params=pltpu.CompilerParams(
            dimension_semantics=("arbitrary", "arbitrary", "arbitrary"), vmem_limit_bytes=VMEM_LIMIT),
        name="attention",
    )(q, k, v)


HALO = 8
LORA_W = LANES
GATE_W = PROJ_RW - 3 * RW_DIM - 2 * LORA_W


def _head_ones(width, head):
    i = np.arange(width) // head
    return jnp.asarray(i[:, None] == i[None, :], BF16)


def _head_sum(x, ones_bd):
    hi = x.astype(BF16)
    lo = (x - hi.astype(F32)).astype(BF16)
    return (jnp.dot(hi, ones_bd, preferred_element_type=F32)
            + jnp.dot(lo, ones_bd, preferred_element_type=F32))


def _dot3(a, b_hi, b_lo):
    hi = a.astype(BF16)
    lo = (a - hi.astype(F32)).astype(BF16)
    d = lambda u, w: jnp.dot(u, w, preferred_element_type=F32)
    return d(hi, b_hi) + (d(hi, b_lo) + d(lo, b_hi))


def _rwkv_prep_kernel(x_ref, prev_ref, next_ref, conv_ref, kkg_ref, ka_ref, rk_ref, w0_ref, a0_ref,
                      w2h_ref, w2l_ref, a2_ref, g2_ref, ones_ref,
                      r_ref, v_ref, kk_ref, lw_ref, bk_ref, kd_ref, bonus_ref, gg_ref, *, ctx_tiles, n_tiles):
    i = pl.program_id(1)
    x = x_ref[0]
    tm, W = x.shape
    C = RW_DIM
    first = (i == 0) | (i == ctx_tiles)
    last = (i == ctx_tiles - 1) | (i == n_tiles - 1)
    prev_row = jnp.where(first, 0.0, prev_ref[0, HALO - 1:HALO])
    next_row = jnp.where(last, 0.0, next_ref[0, 0:1])
    row = lax.broadcasted_iota(jnp.int32, (tm, W), 0)
    x_dn = jnp.where(row == 0, prev_row, pltpu.roll(x, 1, 0))
    x_up = jnp.where(row == tm - 1, next_row, pltpu.roll(x, tm - 1, 0))
    xc = x_dn * conv_ref[0:1] + x * conv_ref[1:2] + x_up * conv_ref[2:3]

    r, k, v = xc[:, :C], xc[:, C:2 * C], xc[:, 2 * C:3 * C]
    lora_w = jnp.tanh(xc[:, 3 * C:3 * C + LORA_W])
    lora_a = xc[:, 3 * C + LORA_W:3 * C + 2 * LORA_W]
    lg = xc[:, 3 * C + 2 * LORA_W:]
    ones_bd = ones_ref[...]
    kq = k * kkg_ref[...]
    kk = kq * lax.rsqrt(_head_sum(kq * kq, ones_bd) + 1e-12)
    r_ref[0], v_ref[0], kk_ref[0] = r, v, kk

    k_sum = jnp.zeros_like(k)
    for d in range(2):
        z = w0_ref[d:d + 1] + _dot3(lora_w, w2h_ref[d], w2l_ref[d])
        softplus_neg = jnp.maximum(-z, 0.0) + jnp.log(1.0 + jnp.exp(-jnp.abs(z)))
        lw_ref[d, 0] = -jnp.exp(-softplus_neg - 0.5)
        a = jax.nn.sigmoid(a0_ref[d:d + 1] + _dot(lora_a, a2_ref[d]))
        kd = k * (1.0 + (a - 1.0) * ka_ref[...])
        bk_ref[d, 0] = kk * a
        kd_ref[d, 0] = kd
        k_sum = k_sum + kd
    bonus_ref[0] = _head_sum(r * k_sum * rk_ref[...], ones_bd) * v
    gg_ref[0] = _dot(jax.nn.sigmoid(lg), g2_ref[...])


def _rwkv_prep(proj, p, n_ctx):
    B, Tt, _ = proj.shape
    C = RW_DIM
    nt = Tt // ROW_TILE
    hb = ROW_TILE // HALO
    pad_cols = lambda w: jnp.pad(w, ((0, 0), (0, PROJ_RW - w.shape[1])))
    w2 = jnp.stack([jnp.pad(p['decay_w2'][0], ((0, LORA_W - DECAY_LORA), (0, 0))),
                    jnp.pad(p['decay_w2'][1], ((DECAY_LORA, 0), (0, 0)))])
    w2h = w2.astype(BF16)
    w2l = (w2 - w2h.astype(F32)).astype(BF16)
    a2 = jnp.stack([jnp.pad(p['aicl_a2'][0], ((0, LORA_W - AICL_LORA), (0, 0))),
                    jnp.pad(p['aicl_a2'][1], ((AICL_LORA, 0), (0, 0)))]).astype(BF16)
    g2 = jnp.pad(p['gate_g2'], ((0, GATE_W - GATE_LORA), (0, 0))).astype(BF16)
    row = lambda b, i: (b, i, 0)
    drow = lambda b, i: (0, b, i, 0)
    const2 = lambda b, i: (0, 0)
    const3 = lambda b, i: (0, 0, 0)
    vec = pl.BlockSpec((1, C), const2)
    out_row = pl.BlockSpec((1, ROW_TILE, C), row)
    out_dir = pl.BlockSpec((2, 1, ROW_TILE, C), drow)
    sds = jax.ShapeDtypeStruct((B, Tt, C), F32)
    sds2 = jax.ShapeDtypeStruct((2, B, Tt, C), F32)
    return pl.pallas_call(
        functools.partial(_rwkv_prep_kernel, ctx_tiles=n_ctx // ROW_TILE, n_tiles=nt),
        grid=(B, nt),
        in_specs=[
            pl.BlockSpec((1, ROW_TILE, PROJ_RW), row),
            pl.BlockSpec((1, HALO, PROJ_RW), lambda b, i: (b, jnp.maximum(i * hb - 1, 0), 0)),
            pl.BlockSpec((1, HALO, PROJ_RW), lambda b, i: (b, jnp.minimum((i + 1) * hb, nt * hb - 1), 0)),
            pl.BlockSpec((3, PROJ_RW), const2),
            vec, vec, vec,
            pl.BlockSpec((2, C), const2),
            pl.BlockSpec((2, C), const2),
            pl.BlockSpec((2, LORA_W, C), const3),
            pl.BlockSpec((2, LORA_W, C), const3),
            pl.BlockSpec((2, LORA_W, C), const3),
            pl.BlockSpec((GATE_W, C), const2),
            pl.BlockSpec((C, C), const2),
        ],
        out_specs=[out_row, out_row, out_row, out_dir, out_dir, out_dir, out_row, out_row],
        out_shape=[sds, sds, sds, sds2, sds2, sds2, sds, sds],
        compiler_params=pltpu.CompilerParams(
            dimension_semantics=("arbitrary", "arbitrary"), vmem_limit_bytes=VMEM_LIMIT),
        name="rwkv_prep",
    )(proj, proj, proj, pad_cols(p['shift_conv']), p['k_k'].reshape(1, C), p['k_a'].reshape(1, C),
      p['r_k'].reshape(1, C), p['decay_w0'], p['aicl_a0'], w2h, w2l, a2, g2, _head_ones(C, RW_HEAD))


def _merge_kernel(att_ref, yf_ref, yb_ref, bonus_ref, gg_ref, gate_ref, x_ref, ga_ref, gnw_ref, gnb_ref, ones_ref,
                  wm_ref, wr_ref, wo_ref, o_ref):
    D = x_ref.shape[-1]
    ones_bd = ones_ref[...]
    y = yf_ref[0] + yb_ref[0]
    yc = y - _head_sum(y, ones_bd) * (1.0 / RW_HEAD)
    var = _head_sum(yc * yc, ones_bd) * (1.0 / RW_HEAD)
    y_n = yc * lax.rsqrt(var + GN_EPS) * gnw_ref[...] + gnb_ref[...]
    rw = (y_n + bonus_ref[0]) * gg_ref[0]
    g = jax.nn.sigmoid(gate_ref[0])
    mix = g[:, :D] * _dot(att_ref[0], wm_ref[...]) + g[:, D:] * _dot(rw, wr_ref[...])
    o_ref[0] = x_ref[0] + ga_ref[0] * _dot(mix, wo_ref[...])


def _merge(att, y_f, y_b, bonus, gg, proj, x, g_a, p, n_ctx):
    B, T, D = x.shape
    C = RW_DIM
    ctx_tiles = n_ctx // ROW_TILE
    gate_blk = PROJ_RW // PROJ_GATE
    row = lambda b, i: (b, i, 0)
    lat_row = lambda b, i: (b, i + ctx_tiles, 0)
    const = lambda b, i: (0, 0)
    return pl.pallas_call(
        _merge_kernel,
        grid=(B, T // ROW_TILE),
        in_specs=[
            pl.BlockSpec((1, ROW_TILE, att.shape[-1]), row),
            pl.BlockSpec((1, ROW_TILE, C), row),
            pl.BlockSpec((1, ROW_TILE, C), row),
            pl.BlockSpec((1, ROW_TILE, C), lat_row),
            pl.BlockSpec((1, ROW_TILE, C), lat_row),
            pl.BlockSpec((1, ROW_TILE, PROJ_GATE), lambda b, i: (b, i + ctx_tiles, gate_blk)),
            pl.BlockSpec((1, ROW_TILE, D), row),
            pl.BlockSpec((1, 1, D), lambda b, i: (b, 0, 0)),
            pl.BlockSpec((1, C), const),
            pl.BlockSpec((1, C), const),
            pl.BlockSpec((C, C), const),
            pl.BlockSpec(p['w_o_mla'].shape, const),
            pl.BlockSpec(p['w_o_rwkv'].shape, const),
            pl.BlockSpec(p['w_out'].shape, const),
        ],
        out_specs=pl.BlockSpec((1, ROW_TILE, D), row),
        out_shape=jax.ShapeDtypeStruct((B, T, D), F32),
        compiler_params=pltpu.CompilerParams(
            dimension_semantics=("arbitrary", "arbitrary"), vmem_limit_bytes=VMEM_LIMIT),
        name="merge",
    )(att, y_f, y_b, bonus, gg, proj, x, g_a[:, None, :], p['gn_w'].reshape(1, C), p['gn_b'].reshape(1, C),
      _head_ones(C, RW_HEAD), p['w_o_mla'].astype(BF16), p['w_o_rwkv'].astype(BF16), p['w_out'].astype(BF16))


MOE_BLOCK = 512


NEG_INF = float("-inf")
HI_MASK = 0xFFFF0000


def _pack_pair(lo, hi):
    lo_b = lax.bitcast_convert_type(lo.astype(BF16).astype(F32), jnp.uint32)
    hi_b = lax.bitcast_convert_type(hi.astype(BF16).astype(F32), jnp.uint32)
    return (lo_b >> 16) | (hi_b & jnp.uint32(HI_MASK))


def _unpack_pair(w):
    lo = lax.bitcast_convert_type(w << 16, F32)
    hi = lax.bitcast_convert_type(w & jnp.uint32(HI_MASK), F32)
    return lo.astype(BF16), hi.astype(BF16)


def _row_max(x):
    return jnp.max(x, axis=-1, keepdims=True)


def _first_index_of(x, value, lane_f):
    return jnp.min(jnp.where(x == value, lane_f, float(x.shape[-1])), axis=-1, keepdims=True)


def _router_kernel(x_ref, gain_ref, sh_ref, sc_ref, wh_ref, wm_ref, wl_ref, bias_ref,
                   u_ref, ids_ref, gates_ref, ranks_ref, counts_ref, carry_ref):
    @pl.when((pl.program_id(0) == 0) & (pl.program_id(1) == 0))
    def _():
        carry_ref[...] = jnp.zeros_like(carry_ref)

    x = x_ref[0]
    tm, D = x.shape
    E = bias_ref.shape[-1]
    u = x * lax.rsqrt(jnp.mean(x * x, axis=-1, keepdims=True) + EPS) * gain_ref[...]
    u = u * (1.0 + sc_ref[0]) + sh_ref[0]
    u_ref[0] = _pack_pair(u[:, :D // 2], u[:, D // 2:])

    uh, um, ul = _split3(u)
    wh, wm, wl = wh_ref[...], wm_ref[...], wl_ref[...]
    d = lambda a, b: jnp.dot(a, b, preferred_element_type=F32)
    logits = d(uh, wh) + (d(uh, wm) + d(um, wh)) + (d(uh, wl) + d(um, wm) + d(ul, wh))
    scores = jax.nn.sigmoid(logits)
    sel = scores + bias_ref[...]

    lane_i = lax.broadcasted_iota(jnp.int32, (tm, E), 1)
    lane_f = lane_i.astype(F32)
    out_f = lax.broadcasted_iota(jnp.int32, (tm, LANES), 1).astype(F32)
    per_group = E // N_GROUPS
    grp_f = jnp.floor(lane_f * (1.0 / per_group))

    gs = jnp.full((tm, LANES), NEG_INF, F32)
    for g in range(N_GROUPS):
        sg = jnp.where(lane_i >= g * per_group, jnp.where(lane_i < (g + 1) * per_group, sel, NEG_INF), NEG_INF)
        m1 = _row_max(sg)
        i1 = _first_index_of(sg, m1, lane_f)
        m2 = _row_max(jnp.where(lane_f == i1, NEG_INF, sg))
        gs = jnp.where(out_f == g, m1 + m2, gs)

    allow = jnp.zeros((tm, E), F32)
    for _ in range(TOPK_GROUPS):
        m = _row_max(gs)
        i = _first_index_of(gs, m, out_f)
        gs = jnp.where(out_f == i, NEG_INF, gs)
        allow = jnp.where(grp_f == i, 1.0, allow)
    selm = jnp.where(allow > 0.0, sel, NEG_INF)

    ids = jnp.zeros((tm, LANES), F32)
    gts = jnp.zeros((tm, LANES), F32)
    member = jnp.zeros((tm, E), F32)
    idx_cols = []
    gsum = jnp.zeros((tm, 1), F32)
    for k in range(TOP_K):
        m = _row_max(selm)
        i = _first_index_of(selm, m, lane_f)
        hit = lane_f == i
        gk = jnp.sum(jnp.where(hit, scores, 0.0), axis=-1, keepdims=True)
        selm = jnp.where(hit, NEG_INF, selm)
        member = jnp.where(hit, 1.0, member)
        ids = jnp.where(out_f == k, i, ids)
        gts = jnp.where(out_f == k, gk, gts)
        idx_cols.append(i)
        gsum = gsum + gk
    gts = gts / gsum * ROUTED_SCALE

    r2 = lax.broadcasted_iota(jnp.int32, (tm, tm), 0)
    c2 = lax.broadcasted_iota(jnp.int32, (tm, tm), 1)
    before = jnp.where(r2 > c2, 1.0, 0.0).astype(BF16)
    mem_b = member.astype(BF16)
    carry = carry_ref[...]
    pos = carry + jnp.dot(before, mem_b, preferred_element_type=F32)
    rk = jnp.zeros((tm, LANES), F32)
    for k in range(TOP_K):
        rk = jnp.where(out_f == k, jnp.sum(jnp.where(lane_f == idx_cols[k], pos, 0.0), axis=-1, keepdims=True), rk)
    colsum = jnp.dot(jnp.ones((8, tm), BF16), mem_b, preferred_element_type=F32)[0:1]
    carry_ref[...] = carry + colsum
    counts_ref[...] = carry + colsum
    ids_ref[0] = ids.astype(jnp.int32)
    ranks_ref[0] = rk.astype(jnp.int32)
    gates_ref[0] = gts


def _router(x1, sh_m, sc_m, p):
    B, T, D = x1.shape
    E = N_EXPERTS
    wh, wm, wl = _split3(p['router_w'])
    row = lambda b, i: (b, i, 0)
    const = lambda b, i: (0, 0)
    vec = lambda b, i: (b, 0, 0)
    lane_out = lambda dt: jax.ShapeDtypeStruct((B, T, LANES), dt)
    return pl.pallas_call(
        _router_kernel,
        grid=(B, T // ROW_TILE),
        in_specs=[
            pl.BlockSpec((1, ROW_TILE, D), row),
            pl.BlockSpec((1, D), const),
            pl.BlockSpec((1, 1, D), vec),
            pl.BlockSpec((1, 1, D), vec),
            pl.BlockSpec((D, E), const),
            pl.BlockSpec((D, E), const),
            pl.BlockSpec((D, E), const),
            pl.BlockSpec((1, E), const),
        ],
        out_specs=[
            pl.BlockSpec((1, ROW_TILE, D // 2), row),
            pl.BlockSpec((1, ROW_TILE, LANES), row),
            pl.BlockSpec((1, ROW_TILE, LANES), row),
            pl.BlockSpec((1, ROW_TILE, LANES), row),
            pl.BlockSpec((1, E), const),
        ],
        out_shape=[
            jax.ShapeDtypeStruct((B, T, D // 2), jnp.uint32),
            lane_out(jnp.int32), lane_out(F32), lane_out(jnp.int32),
            jax.ShapeDtypeStruct((1, E), F32),
        ],
        scratch_shapes=[pltpu.VMEM((1, E), F32)],
        compiler_params=pltpu.CompilerParams(
            dimension_semantics=("arbitrary", "arbitrary"), vmem_limit_bytes=VMEM_LIMIT),
        name="router",
    )(x1, p['norm_ffn'].reshape(1, D), sh_m[:, None, :], sc_m[:, None, :], wh, wm, wl,
      p['router_bias'].reshape(1, E))


def _slot_kernel(ids_ref, ranks_ref, base_ref, o_ref):
    ids = ids_ref[...].astype(F32)
    tm = ids.shape[0]
    E = base_ref.shape[-1]
    lane_e = lax.broadcasted_iota(jnp.int32, (tm, E), 1).astype(F32)
    out_lane = lax.broadcasted_iota(jnp.int32, (tm, LANES), 1)
    first = jnp.zeros((tm, LANES), F32)
    for k in range(TOP_K):
        fk = jnp.sum(jnp.where(lane_e == ids[:, k:k + 1], base_ref[...], 0.0), axis=-1, keepdims=True)
        first = jnp.where(out_lane == k, fk, first)
    o_ref[...] = first.astype(jnp.int32) + ranks_ref[...]


def _slots(ids, ranks, base):
    n_tok = ids.shape[0]
    E = base.shape[-1]
    row = pl.BlockSpec((ROW_TILE, LANES), lambda i: (i, 0))
    return pl.pallas_call(
        _slot_kernel,
        grid=(n_tok // ROW_TILE,),
        in_specs=[row, row, pl.BlockSpec((1, E), lambda i: (0, 0))],
        out_specs=row,
        out_shape=jax.ShapeDtypeStruct((n_tok, LANES), jnp.int32),
        compiler_params=pltpu.CompilerParams(dimension_semantics=("arbitrary",)),
        name="slots",
    )(ids, ranks, base)


DMA_PRIORITIES = 2


def _row_copy(src, src_row, dst, dst_row, sem):
    return pltpu.make_async_copy(src.at[pl.ds(src_row, 1)], dst.at[pl.ds(dst_row, 1)], sem)


def _dispatch_kernel(dest_ref, u_ref, slots_in_ref, slots_ref, sem):
    del slots_in_ref
    tm = u_ref.shape[0]

    def issue(r, carry):
        for k in range(TOP_K):
            _row_copy(u_ref, r, slots_ref, dest_ref[0, 0, r * TOP_K + k], sem).start(priority=k % DMA_PRIORITIES)
        return carry

    lax.fori_loop(0, tm, issue, 0)

    def drain(r, carry):
        for k in range(TOP_K):
            _row_copy(u_ref, r, slots_ref, 0, sem).wait()
        return carry

    lax.fori_loop(0, tm, drain, 0)


def _dispatch(dest, u_rows, n_slots):
    n_tok, W = u_rows.shape
    nt = n_tok // ROW_TILE
    return pl.pallas_call(
        _dispatch_kernel,
        grid=(nt,),
        in_specs=[
            pl.BlockSpec((1, 1, ROW_TILE * TOP_K), lambda i: (i, 0, 0), memory_space=pltpu.SMEM),
            pl.BlockSpec((ROW_TILE, W), lambda i: (i, 0)),
            pl.BlockSpec(memory_space=pl.ANY),
        ],
        out_specs=pl.BlockSpec(memory_space=pl.ANY),
        out_shape=jax.ShapeDtypeStruct((n_slots, W), jnp.uint32),
        scratch_shapes=[pltpu.SemaphoreType.DMA(())],
        input_output_aliases={2: 0},
        compiler_params=pltpu.CompilerParams(
            dimension_semantics=("arbitrary",), vmem_limit_bytes=VMEM_LIMIT),
        name="dispatch",
    )(dest, u_rows, jnp.zeros((n_slots, W), jnp.uint32))


def _moe_ffn_kernel(be_ref, nu_ref, x_ref, w1_ref, w3_ref, w2_ref, o_ref, w13_ref, w2b_ref):
    j = pl.program_id(0)
    F = w1_ref.shape[-1]
    half = x_ref.shape[-1]

    @pl.when((j == 0) | (be_ref[j] != be_ref[jnp.maximum(j - 1, 0)]))
    def _():
        w13_ref[:, :F] = w1_ref[0].astype(BF16)
        w13_ref[:, F:] = w3_ref[0].astype(BF16)
        w2b_ref[...] = w2_ref[0].astype(BF16)

    @pl.when(j < nu_ref[0])
    def _():
        lo, hi = _unpack_pair(x_ref[...])
        h = (jnp.dot(lo, w13_ref[:half], preferred_element_type=F32)
             + jnp.dot(hi, w13_ref[half:], preferred_element_type=F32))
        h1, h3 = h[:, :F], h[:, F:]
        y = _dot(h1 * jax.nn.sigmoid(h1) * h3, w2b_ref[...])
        o_ref[...] = _pack_pair(y[:, :half], y[:, half:])

    @pl.when(j >= nu_ref[0])
    def _():
        o_ref[...] = jnp.zeros_like(o_ref)


def _moe_ffn(block_e, n_used, x_slots, w1, w3, w2):
    slots, W = x_slots.shape
    E, D, F = w1.shape
    nblk = slots // MOE_BLOCK
    grid_spec = pltpu.PrefetchScalarGridSpec(
        num_scalar_prefetch=2,
        grid=(nblk,),
        in_specs=[
            pl.BlockSpec((MOE_BLOCK, W), lambda j, be, nu: (jnp.minimum(j, nu[0] - 1), 0)),
            pl.BlockSpec((1, D, F), lambda j, be, nu: (be[j], 0, 0)),
            pl.BlockSpec((1, D, F), lambda j, be, nu: (be[j], 0, 0)),
            pl.BlockSpec((1, F, D), lambda j, be, nu: (be[j], 0, 0)),
        ],
        out_specs=pl.BlockSpec((MOE_BLOCK, W), lambda j, be, nu: (j, 0)),
        scratch_shapes=[pltpu.VMEM((D, 2 * F), BF16), pltpu.VMEM((F, D), BF16)],
    )
    return pl.pallas_call(
        _moe_ffn_kernel,
        grid_spec=grid_spec,
        out_shape=jax.ShapeDtypeStruct((slots, W), jnp.uint32),
        compiler_params=pltpu.CompilerParams(
            dimension_semantics=("arbitrary",), vmem_limit_bytes=VMEM_LIMIT),
        name="moe_ffn",
    )(block_e, n_used, x_slots, w1, w3, w2)


def _combine_kernel(dest_ref, gates_ref, u_ref, x_ref, gm_ref, w1_ref, w3_ref, w2_ref, ys_ref, o_ref, buf_ref, sem):
    tm, half = u_ref.shape

    def issue(r, carry):
        for k in range(TOP_K):
            _row_copy(ys_ref, dest_ref[0, 0, r * TOP_K + k], buf_ref.at[k], r, sem).start(priority=k % DMA_PRIORITIES)
        return carry

    lax.fori_loop(0, tm, issue, 0)

    ulo, uhi = _unpack_pair(u_ref[...])
    both = lambda w_ref: (jnp.dot(ulo, w_ref[:half], preferred_element_type=F32)
                          + jnp.dot(uhi, w_ref[half:], preferred_element_type=F32))
    h1, h3 = both(w1_ref), both(w3_ref)
    shared = _dot(h1 * jax.nn.sigmoid(h1) * h3, w2_ref[...])

    def drain(r, carry):
        for k in range(TOP_K):
            _row_copy(ys_ref, 0, buf_ref.at[k], r, sem).wait()
        return carry

    lax.fori_loop(0, tm, drain, 0)

    gates = gates_ref[...]
    lo_acc = jnp.zeros((tm, half), F32)
    hi_acc = jnp.zeros((tm, half), F32)
    for k in range(TOP_K):
        lo, hi = _unpack_pair(buf_ref[k])
        gk = gates[:, k:k + 1]
        lo_acc = lo_acc + gk * lo.astype(F32)
        hi_acc = hi_acc + gk * hi.astype(F32)
    routed = jnp.concatenate([lo_acc, hi_acc], axis=1)
    o_ref[...] = x_ref[...] + gm_ref[0] * (routed + shared)


def _combine(dest, gates, u_rows, x1, g_m, y_slots, p):
    B, T, D = x1.shape
    n_tok = B * T
    W = u_rows.shape[1]
    tiles_per_batch = T // ROW_TILE
    row = lambda i: (i, 0)
    const = lambda i: (0, 0)
    out = pl.pallas_call(
        _combine_kernel,
        grid=(n_tok // ROW_TILE,),
        in_specs=[
            pl.BlockSpec((1, 1, ROW_TILE * TOP_K), lambda i: (i, 0, 0), memory_space=pltpu.SMEM),
            pl.BlockSpec((ROW_TILE, LANES), row),
            pl.BlockSpec((ROW_TILE, W), row),
            pl.BlockSpec((ROW_TILE, D), row),
            pl.BlockSpec((1, 1, D), lambda i: (i // tiles_per_batch, 0, 0)),
            pl.BlockSpec(p['shared_w1'].shape, const),
            pl.BlockSpec(p['shared_w3'].shape, const),
            pl.BlockSpec(p['shared_w2'].shape, const),
            pl.BlockSpec(memory_space=pl.ANY),
        ],
        out_specs=pl.BlockSpec((ROW_TILE, D), row),
        out_shape=jax.ShapeDtypeStruct((n_tok, D), F32),
        scratch_shapes=[pltpu.VMEM((TOP_K, ROW_TILE, W), jnp.uint32), pltpu.SemaphoreType.DMA(())],
        compiler_params=pltpu.CompilerParams(
            dimension_semantics=("arbitrary",), vmem_limit_bytes=VMEM_LIMIT),
        name="combine",
    )(dest, gates.reshape(n_tok, LANES), u_rows, x1.reshape(n_tok, D), g_m[:, None, :],
      p['shared_w1'].astype(BF16), p['shared_w3'].astype(BF16), p['shared_w2'].astype(BF16), y_slots)
    return out.reshape(B, T, D)


def _moe(x1, sh_m, sc_m, g_m, p):
    B, T, D = x1.shape
    n_tok = B * T
    u_packed, ids, gates, ranks, counts = _router(x1, sh_m, sc_m, p)

    counts = counts[0].astype(jnp.int32)
    padded = (counts + MOE_BLOCK - 1) // MOE_BLOCK * MOE_BLOCK
    padded_end = jnp.cumsum(padded)
    base = padded_end - padded
    n_blocks = n_tok * TOP_K // MOE_BLOCK + N_EXPERTS
    n_used = (padded_end[-1] // MOE_BLOCK).astype(jnp.int32)
    blk = jnp.minimum(jnp.arange(n_blocks), n_used - 1) * MOE_BLOCK
    block_e = jnp.minimum(jnp.searchsorted(padded_end, blk, side='right'), N_EXPERTS - 1).astype(jnp.int32)
    dest = _slots(ids.reshape(n_tok, LANES), ranks.reshape(n_tok, LANES), base.astype(F32).reshape(1, N_EXPERTS))
    dest = dest[:, :TOP_K].reshape(n_tok // ROW_TILE, 1, ROW_TILE * TOP_K)

    u_rows = u_packed.reshape(n_tok, D // 2)
    x_slots = _dispatch(dest, u_rows, n_blocks * MOE_BLOCK)
    y_slots = _moe_ffn(block_e, n_used.reshape(1), x_slots, p['expert_w1'], p['expert_w3'], p['expert_w2'])
    return _combine(dest, gates, u_rows, x1, g_m, y_slots, p)
```

```python
import functools

import jax
import jax.numpy as jnp
import numpy as np
from jax import lax
from jax.experimental import pallas as pl
from jax.experimental.pallas import tpu as pltpu

F32 = jnp.float32
BF16 = jnp.bfloat16

GRID_W = 64
EPS = 1e-6
MLA_HEADS = 8
QK_NOPE = 64
QK_ROPE = 32
QK_HEAD = QK_NOPE + QK_ROPE
V_HEAD = 64
Q_LORA = 256
KV_LORA = 128
ROPE_BASE = 10000.0
ATTN_SCALE = QK_HEAD ** -0.5
RW_HEADS = 8
RW_HEAD = 64
RW_DIM = RW_HEADS * RW_HEAD
DECAY_LORA = 64
AICL_LORA = 64
GATE_LORA = 160
GN_EPS = 64e-5
N_EXPERTS = 256
TOP_K = 8
N_GROUPS = 8
TOPK_GROUPS = 4
ROUTED_SCALE = 2.5

LANES = 128
MXU_DIM = 256
VMEM_LIMIT = 56 * 1024 * 1024

WKV_CHUNK = 64
HEADS_PER_GROUP = MXU_DIM // RW_HEAD
HEAD_PAD = LANES


def _dot(a, b):
    return jnp.dot(a.astype(BF16), b.astype(BF16), preferred_element_type=F32)


def _dot_t(a, b):
    return lax.dot_general(a.astype(BF16), b.astype(BF16), (((1,), (1,)), ((), ())),
                           preferred_element_type=F32)


def _split3(x):
    h = x.astype(BF16)
    r1 = x - h.astype(F32)
    m = r1.astype(BF16)
    lo = (r1 - m.astype(F32)).astype(BF16)
    return h, m, lo


def _dot_hi(a_bf16_exact, x):
    h, m, lo = _split3(x)
    d = lambda y: jnp.dot(a_bf16_exact, y, preferred_element_type=F32)
    return d(h) + d(m) + d(lo)


def _wkv_chunk(r, v, kk, lw, bk, kd, s0, reverse, emit):
    L, G = r.shape
    row = lax.broadcasted_iota(jnp.int32, (L, G), 0)
    lane = lax.broadcasted_iota(jnp.int32, (L, G), 1)
    diff = (lane % L - row) if reverse else (row - lane % L)
    strict = diff > 0
    incl = diff >= 0
    r2 = lax.broadcasted_iota(jnp.int32, (L, L), 0)
    c2 = lax.broadcasted_iota(jnp.int32, (L, L), 1)
    tri = jnp.where(((c2 - r2) if reverse else (r2 - c2)) >= 0, 1.0, 0.0).astype(BF16)
    tri_ones = jnp.concatenate([tri, jnp.ones((L, L), BF16)], axis=0)
    lane_head = lane // RW_HEAD
    rowb = lax.broadcasted_iota(jnp.int32, (G, G), 0) // RW_HEAD
    colb = lax.broadcasted_iota(jnp.int32, (G, G), 1) // RW_HEAD

    def stack(x):
        return jnp.concatenate(
            [jnp.where(lane_head == h, x, 0.0) for h in range(HEADS_PER_GROUP)], axis=0).astype(BF16)

    sums = _dot_hi(tri_ones, lw)
    cum_in = sums[:L]
    g_in = jnp.exp(cum_in)
    g_inv = jnp.exp(-cum_in)
    g_ex = jnp.exp(cum_in - lw)
    g_tot = jnp.exp(sums[L:])
    a_h = -kk * g_ex
    b_h = bk * g_inv
    k_h = kd * g_inv
    r_h = r * g_in
    yield

    gram = _dot_t(jnp.concatenate([a_h, r_h], axis=0),
                  jnp.concatenate([stack(b_h), stack(k_h)], axis=0))
    ab = jnp.where(strict, gram[:L, :G], 0.0)
    ak = jnp.where(strict, gram[:L, G:], 0.0)
    rb = jnp.where(incl, gram[L:, :G], 0.0)
    rk = jnp.where(incl, gram[L:, G:], 0.0)
    yield

    tm = jnp.where(diff == 0, 1.0, 0.0) + ab
    p = ab
    v_st = stack(v)
    akv = _dot(ak, v_st)
    for _ in range(int(np.log2(L)) - 1):
        p = _dot(p, stack(p))
        yield
        tm = tm + _dot(tm, stack(p))
        yield

    wu = _dot(tm, jnp.concatenate([stack(a_h), stack(akv)], axis=1))
    w_t, u_t = wu[:, :G], wu[:, G:]
    yield
    rbwu = _dot(rb, jnp.concatenate([stack(w_t), stack(u_t)], axis=1))
    r_t = r_h + rbwu[:, :G]
    y_t = rbwu[:, G:] + _dot(rk, v_st)
    yield

    s0_st = stack(s0)
    y = _dot_t(r_t, s0_st) + y_t
    u = _dot_t(w_t, s0_st) + u_t
    yield
    uv = jnp.concatenate([u, v], axis=0)
    bkc = jnp.concatenate([b_h, k_h], axis=0)
    upd = lax.dot_general(uv.astype(BF16), bkc.astype(BF16), (((0,), (0,)), ((), ())),
                          preferred_element_type=F32)
    upd = jnp.where(rowb == colb, upd, 0.0)
    upd_d = upd[0:L]
    for h in range(1, HEADS_PER_GROUP):
        upd_d = upd_d + upd[h * L:(h + 1) * L]
    emit(y, (s0 + upd_d) * g_tot)


def _wkv_kernel(*refs, n_groups, n_batch):
    ins, (yf_ref, yb_ref, s_ref) = refs[:12], refs[12:]
    G = MXU_DIM

    @pl.when(pl.program_id(1) == 0)
    def _():
        s_ref[...] = jnp.zeros_like(s_ref)

    chains = []
    for d, y_ref in enumerate((yf_ref, yb_ref)):
        r_ref, v_ref, kk_ref, lw_ref, bk_ref, kd_ref = ins[6 * d:6 * d + 6]
        for bi in range(n_batch):
            for g in range(n_groups):
                sl = slice(g * G, (g + 1) * G)

                def emit(y, s_new, y_ref=y_ref, d=d, bi=bi, g=g, sl=sl):
                    y_ref[bi, :, sl] = y
                    s_ref[d, bi, g] = s_new

                chains.append(_wkv_chunk(
                    r_ref[bi, :, sl], v_ref[bi, :, sl], kk_ref[bi, :, sl], lw_ref[0, bi, :, sl],
                    bk_ref[0, bi, :, sl], kd_ref[0, bi, :, sl], s_ref[d, bi, g], d == 1, emit))
    while chains:
        chains = [c for c in chains if next(c, StopIteration) is not StopIteration]


WKV_BATCH = 2


def _wkv_scan(r, v, kk, lw, bk, kd, n_ctx):
    B, Ttot, C = r.shape
    L = WKV_CHUNK
    nc = Ttot // L
    ncc = n_ctx // L
    nl = nc - ncc
    n_groups = C // MXU_DIM
    nb = WKV_BATCH

    cid = (lambda s: s, lambda s: jnp.where(s < ncc, ncc - 1 - s, nc + ncc - 1 - s))
    first_out = (0, nl - 1)
    in_specs, out_specs = [], []
    for d in range(2):
        shared = pl.BlockSpec((nb, L, C), lambda b, s, d=d: (b, cid[d](s), 0))
        per_dir = pl.BlockSpec((1, nb, L, C), lambda b, s, d=d: (d, b, cid[d](s), 0))
        in_specs += [shared, shared, shared, per_dir, per_dir, per_dir]
        out_specs.append(pl.BlockSpec(
            (nb, L, C), lambda b, s, d=d: (b, jnp.where(s < ncc, first_out[d], cid[d](s) - ncc), 0)))
    out = jax.ShapeDtypeStruct((B, nl * L, C), F32)
    return pl.pallas_call(
        functools.partial(_wkv_kernel, n_groups=n_groups, n_batch=nb),
        grid=(B // nb, nc),
        in_specs=in_specs,
        out_specs=out_specs,
        out_shape=[out, out],
        scratch_shapes=[pltpu.VMEM((2, nb, n_groups, L, MXU_DIM), F32)],
        compiler_params=pltpu.CompilerParams(
            dimension_semantics=("arbitrary", "arbitrary"), vmem_limit_bytes=VMEM_LIMIT),
        name="wkv_scan",
    )(r, v, kk, lw, bk, kd, r, v, kk, lw, bk, kd)


def kernel(x, c, ctx, c_ctx, ada_w, ada_b, norm_mix, norm_ffn, w_in, shift_conv, q_lat_norm, w_uq, kv_lat_norm, w_ukv, q_norm, k_norm, w_o_mla, decay_w0, decay_w2, aicl_a0, aicl_a2, k_k, k_a, r_k, gn_w, gn_b, gate_g2, w_o_rwkv, w_out, router_w, router_bias, expert_w1, expert_w3, expert_w2, shared_w1, shared_w3, shared_w2):
    B, T, D = x.shape
    n_ctx = ctx.shape[1]
    i0 = 0
    p = dict(norm_mix=norm_mix[i0], norm_ffn=norm_ffn[i0], w_in=w_in[i0], shift_conv=shift_conv[i0],
             q_lat_norm=q_lat_norm[i0], w_uq=w_uq[i0], kv_lat_norm=kv_lat_norm[i0], w_ukv=w_ukv[i0],
             q_norm=q_norm[i0], k_norm=k_norm[i0], w_o_mla=w_o_mla[i0],
             decay_w0=decay_w0[i0], decay_w2=decay_w2[i0], aicl_a0=aicl_a0[i0], aicl_a2=aicl_a2[i0],
             k_k=k_k[i0], k_a=k_a[i0], r_k=r_k[i0], gn_w=gn_w[i0], gn_b=gn_b[i0], gate_g2=gate_g2[i0],
             w_o_rwkv=w_o_rwkv[i0], w_out=w_out[i0], router_w=router_w[i0], router_bias=router_bias[i0],
             expert_w1=expert_w1[i0], expert_w3=expert_w3[i0], expert_w2=expert_w2[i0],
             shared_w1=shared_w1[i0], shared_w3=shared_w3[i0], shared_w2=shared_w2[i0])

    mod = _ada_modulation(c, c_ctx, ada_w[i0], ada_b[i0])
    sh_a, sc_a, g_a, sh_m, sc_m, g_m = jnp.split(mod[:B], 6, axis=-1)
    csh_a, csc_a = jnp.split(mod[B], 6, axis=-1)[:2]
    sh2 = jnp.stack([jnp.broadcast_to(csh_a, (B, D)), sh_a], axis=1)
    sc2 = jnp.stack([jnp.broadcast_to(csc_a, (B, D)), sc_a], axis=1)

    proj = _in_proj(ctx, x, sh2, sc2, p['norm_mix'], _pack_w_in(p['w_in']))

    q, k, v = _mla_prep(proj, p, n_ctx, T)
    att = _attention(q, k, v)

    r, vv, kk, lw, bk, kd, bonus, gg = _rwkv_prep(proj, p, n_ctx)
    y_f, y_b = _wkv_scan(r, vv, kk, lw, bk, kd, n_ctx)

    x1 = _merge(att, y_f, y_b, bonus, gg, proj, x, g_a, p, n_ctx)
    return _moe(x1, sh_m, sc_m, g_m, p)


SUBLANES = 8


def _ada_kernel(c_ref, w_ref, b_ref, o_ref):
    cc = c_ref[...]
    w = w_ref[...]
    w_hi = w.astype(BF16)
    w_lo = (w - w_hi.astype(F32)).astype(BF16)
    o_ref[...] = _dot3(cc * jax.nn.sigmoid(cc), w_hi, w_lo) + b_ref[...]


def _ada_modulation(c, c_ctx, ada_w, ada_b):
    B, D = c.shape
    n_out = ada_w.shape[1]
    rows = -(-(B + 1) // SUBLANES) * SUBLANES
    c_all = jnp.concatenate([c, c_ctx[None, :], jnp.zeros((rows - B - 1, D), F32)], axis=0)
    out = pl.pallas_call(
        _ada_kernel,
        grid=(n_out // D,),
        in_specs=[pl.BlockSpec((rows, D), lambda j: (0, 0)),
                  pl.BlockSpec((D, D), lambda j: (0, j)),
                  pl.BlockSpec((1, D), lambda j: (0, j))],
        out_specs=pl.BlockSpec((rows, D), lambda j: (0, j)),
        out_shape=jax.ShapeDtypeStruct((rows, n_out), F32),
        compiler_params=pltpu.CompilerParams(dimension_semantics=("arbitrary",), vmem_limit_bytes=VMEM_LIMIT),
        name="ada_modulation",
    )(c_all, ada_w, ada_b.reshape(1, n_out))
    return out[:B + 1]


ROW_TILE = 256
PROJ_RW = 2048
PROJ_GATE = 2048
PROJ_MLA = 512
PROJ_W = PROJ_RW + PROJ_GATE + PROJ_MLA
MLA_IN = Q_LORA + KV_LORA + QK_ROPE
RW_SPLITS = (RW_DIM, RW_DIM, RW_DIM, DECAY_LORA, DECAY_LORA, AICL_LORA, AICL_LORA, GATE_LORA)
RW_IN = sum(RW_SPLITS)


def _pack_w_in(w_in):
    w_mla = w_in[:, :MLA_IN]
    w_rw = w_in[:, MLA_IN:MLA_IN + RW_IN]
    w_gate = w_in[:, MLA_IN + RW_IN:]
    pad = lambda w, n: jnp.pad(w, ((0, 0), (0, n - w.shape[1])))
    return jnp.concatenate([pad(w_rw, PROJ_RW), w_gate, pad(w_mla, PROJ_MLA)], axis=1).astype(BF16)


def _in_proj_kernel(ctx_ref, x_ref, sh_ref, sc_ref, gain_ref, w_ref, o_ref):
    is_ctx = pl.program_id(1) == 0
    xt = jnp.where(is_ctx, ctx_ref[0], x_ref[0])
    sh = jnp.where(is_ctx, sh_ref[0, 0:1], sh_ref[0, 1:2])
    sc = jnp.where(is_ctx, sc_ref[0, 0:1], sc_ref[0, 1:2])
    y = xt * lax.rsqrt(jnp.mean(xt * xt, axis=-1, keepdims=True) + EPS) * gain_ref[...]
    h = y * (1.0 + sc) + sh
    o_ref[0] = _dot(h, w_ref[...])


def _in_proj(ctx, x, sh2, sc2, gain, w):
    B, T, D = x.shape
    n_ctx = ctx.shape[1]
    assert n_ctx == ROW_TILE and T % ROW_TILE == 0
    nt = 1 + T // ROW_TILE
    return pl.pallas_call(
        _in_proj_kernel,
        grid=(B, nt),
        in_specs=[
            pl.BlockSpec((1, ROW_TILE, D), lambda b, i: (b, 0, 0)),
            pl.BlockSpec((1, ROW_TILE, D), lambda b, i: (b, jnp.maximum(i - 1, 0), 0)),
            pl.BlockSpec((1, 2, D), lambda b, i: (b, 0, 0)),
            pl.BlockSpec((1, 2, D), lambda b, i: (b, 0, 0)),
            pl.BlockSpec((1, D), lambda b, i: (0, 0)),
            pl.BlockSpec((D, PROJ_W), lambda b, i: (0, 0)),
        ],
        out_specs=pl.BlockSpec((1, ROW_TILE, PROJ_W), lambda b, i: (b, i, 0)),
        out_shape=jax.ShapeDtypeStruct((B, n_ctx + T, PROJ_W), F32),
        compiler_params=pltpu.CompilerParams(
            dimension_semantics=("arbitrary", "arbitrary"), vmem_limit_bytes=VMEM_LIMIT),
        name="in_proj",
    )(ctx, x, sh2, sc2, gain.reshape(1, D), w)


def _rms(x, gain):
    return x * lax.rsqrt(jnp.mean(x * x, axis=-1, keepdims=True) + EPS) * gain


def _rope_tables(n_tokens):
    rows = n_tokens // GRID_W
    row = jnp.repeat(jnp.arange(rows, dtype=F32), GRID_W)
    col = jnp.tile(jnp.arange(GRID_W, dtype=F32), rows)
    n_freq = QK_ROPE // 4
    inv_freq = ROPE_BASE ** (-jnp.arange(n_freq, dtype=F32) / n_freq)
    ang = jnp.concatenate([row[:, None] * inv_freq, col[:, None] * inv_freq], axis=-1)
    return jnp.cos(ang), jnp.sin(ang)


ROPE_HALF = QK_ROPE // 2
X1 = slice(QK_NOPE, QK_NOPE + ROPE_HALF)
X2 = slice(QK_NOPE + ROPE_HALF, QK_HEAD)


def _rot_cols(w):
    w3 = w.reshape(w.shape[0], MLA_HEADS, HEAD_PAD)
    rot = jnp.zeros_like(w3).at[:, :, X1].set(-w3[:, :, X2]).at[:, :, X2].set(w3[:, :, X1])
    return rot.reshape(w.shape)


def _swap_halves(g):
    return jnp.zeros_like(g).at[:, X1].set(g[:, X2]).at[:, X2].set(g[:, X1])


def _mla_weights(p):
    H = MLA_HEADS
    wq = jnp.pad(p['w_uq'].reshape(Q_LORA, H, QK_HEAD), ((0, 0), (0, 0), (0, HEAD_PAD - QK_HEAD)))
    wq = wq.reshape(Q_LORA, H * HEAD_PAD)
    wkv = p['w_ukv'].reshape(KV_LORA, H, QK_NOPE + V_HEAD)
    wk_lat = jnp.pad(wkv[:, :, :QK_NOPE], ((0, 0), (0, 0), (0, HEAD_PAD - QK_NOPE)))
    place = jnp.zeros((LANES, H, HEAD_PAD), F32).at[:QK_ROPE, :, QK_NOPE:QK_HEAD].set(
        jnp.broadcast_to(jnp.eye(QK_ROPE, dtype=F32)[:, None, :], (QK_ROPE, H, QK_ROPE)))
    wk = jnp.concatenate([wk_lat, place], axis=0).reshape(KV_LORA + LANES, H * HEAD_PAD)
    wv = wkv[:, :, QK_NOPE:].reshape(KV_LORA, H * V_HEAD)
    gq = jnp.pad(p['q_norm'], (0, HEAD_PAD - QK_HEAD)).reshape(1, HEAD_PAD)
    gk = jnp.pad(p['k_norm'], (0, HEAD_PAD - QK_HEAD)).reshape(1, HEAD_PAD)
    bf = lambda w: w.astype(BF16)
    return (bf(wq), bf(_rot_cols(wq)), bf(wk), bf(_rot_cols(wk)), bf(wv), gq, _swap_halves(gq), gk, _swap_halves(gk))


def _mla_tables(n_ctx, T):
    cos, sin = _rope_tables(T)
    c = jnp.ones((n_ctx + T, HEAD_PAD), F32).at[n_ctx:, X1].set(cos).at[n_ctx:, X2].set(cos)
    s = jnp.zeros((n_ctx + T, HEAD_PAD), F32).at[n_ctx:, X1].set(sin).at[n_ctx:, X2].set(sin)
    return c, s


def _mla_prep_kernel(x_ref, c_ref, s_ref, qlg_ref, kvg_ref, wq_ref, wqr_ref, wk_ref, wkr_ref, wv_ref,
                     gq_ref, gqp_ref, gk_ref, gkp_ref, q_ref, k_ref, v_ref):
    x = x_ref[0]
    ql = _rms(x[:, :Q_LORA], qlg_ref[...])
    kvl = _rms(x[:, Q_LORA:Q_LORA + KV_LORA], kvg_ref[...])
    k_in = jnp.concatenate([kvl, x[:, Q_LORA + KV_LORA:]], axis=1)
    cos, sin = c_ref[...], s_ref[...]

    def finish(raw, partner, g, g_swapped, scale, o_ref):
        gc, gs = g * cos, g_swapped * sin
        for h in range(MLA_HEADS):
            sl = slice(h * HEAD_PAD, (h + 1) * HEAD_PAD)
            rh = raw[:, sl]
            inv = lax.rsqrt(jnp.sum(rh * rh, axis=-1, keepdims=True) * (1.0 / QK_HEAD) + EPS) * scale
            o_ref[0, :, sl] = ((rh * gc + partner[:, sl] * gs) * inv).astype(o_ref.dtype)

    finish(_dot(ql, wq_ref[...]), _dot(ql, wqr_ref[...]), gq_ref[...], gqp_ref[...], ATTN_SCALE, q_ref)
    finish(_dot(k_in, wk_ref[...]), _dot(k_in, wkr_ref[...]), gk_ref[...], gkp_ref[...], 1.0, k_ref)
    v_ref[0] = _dot(kvl, wv_ref[...]).astype(v_ref.dtype)


def _mla_prep(proj, p, n_ctx, T):
    B, Tt, _ = proj.shape
    H = MLA_HEADS
    ctx_tiles = n_ctx // ROW_TILE
    mla_blk = (PROJ_RW + PROJ_GATE) // PROJ_MLA
    weights = _mla_weights(p)
    cos, sin = _mla_tables(n_ctx, T)
    const = lambda b, i: (0, 0)
    row = lambda b, i: (b, i, 0)
    tab = pl.BlockSpec((ROW_TILE, HEAD_PAD), lambda b, i: (i, 0))
    full = lambda a: pl.BlockSpec(a.shape, const)
    return pl.pallas_call(
        _mla_prep_kernel,
        grid=(B, Tt // ROW_TILE),
        in_specs=[pl.BlockSpec((1, ROW_TILE, PROJ_MLA), lambda b, i: (b, i, mla_blk)), tab, tab,
                  pl.BlockSpec((1, Q_LORA), const), pl.BlockSpec((1, KV_LORA), const)]
                 + [full(w) for w in weights],
        out_specs=[
            pl.BlockSpec((1, ROW_TILE, H * HEAD_PAD), lambda b, i: (b, jnp.maximum(i - ctx_tiles, 0), 0)),
            pl.BlockSpec((1, ROW_TILE, H * HEAD_PAD), row),
            pl.BlockSpec((1, ROW_TILE, H * V_HEAD), row),
        ],
        out_shape=[
            jax.ShapeDtypeStruct((B, T, H * HEAD_PAD), BF16),
            jax.ShapeDtypeStruct((B, Tt, H * HEAD_PAD), BF16),
            jax.ShapeDtypeStruct((B, Tt, H * V_HEAD), BF16),
        ],
        compiler_params=pltpu.CompilerParams(
            dimension_semantics=("arbitrary", "arbitrary"), vmem_limit_bytes=VMEM_LIMIT),
        name="mla_prep",
    )(proj, cos, sin, p['q_lat_norm'].reshape(1, Q_LORA), p['kv_lat_norm'].reshape(1, KV_LORA), *weights)


ATTN_Q_TILE = 1024
HEADS_PER_STEP = LANES // V_HEAD


ATTN_ROW_SPLIT = 8


def _attn_kernel(q_ref, k_ref, v_ref, o_ref):
    v2 = v_ref[0]
    rows = q_ref.shape[1] // ATTN_ROW_SPLIT
    work = [(hh, rs) for rs in range(ATTN_ROW_SPLIT) for hh in range(HEADS_PER_STEP)]

    def scores(hh, rs):
        sl = slice(hh * HEAD_PAD, (hh + 1) * HEAD_PAD)
        return _dot_t(q_ref[0, rs * rows:(rs + 1) * rows, sl], k_ref[0, :, sl])

    outs = {}
    s_next = scores(*work[0])
    for i, (hh, rs) in enumerate(work):
        s = s_next
        if i + 1 < len(work):
            s_next = scores(*work[i + 1])
        e = jnp.exp(s - jnp.max(s, axis=-1, keepdims=True))
        outs[hh, rs] = _dot(e, v2) / jnp.sum(e, axis=-1, keepdims=True)
    lane = lax.broadcasted_iota(jnp.int32, (rows, LANES), 1)
    for rs in range(ATTN_ROW_SPLIT):
        o_ref[0, rs * rows:(rs + 1) * rows] = jnp.where(lane < V_HEAD, outs[0, rs], outs[1, rs])


def _attention(q, k, v):
    B, T, _ = q.shape
    Kt = k.shape[1]
    assert T % ATTN_Q_TILE == 0 and ATTN_Q_TILE % (ATTN_ROW_SPLIT * SUBLANES) == 0
    hp = MLA_HEADS // HEADS_PER_STEP
    qw = HEADS_PER_STEP * HEAD_PAD
    return pl.pallas_call(
        _attn_kernel,
        grid=(B, hp, T // ATTN_Q_TILE),
        in_specs=[
            pl.BlockSpec((1, ATTN_Q_TILE, qw), lambda b, h, i: (b, i, h)),
            pl.BlockSpec((1, Kt, qw), lambda b, h, i: (b, 0, h)),
            pl.BlockSpec((1, Kt, LANES), lambda b, h, i: (b, 0, h)),
        ],
        out_specs=pl.BlockSpec((1, ATTN_Q_TILE, LANES), lambda b, h, i: (b, i, h)),
        out_shape=jax.ShapeDtypeStruct((B, T, MLA_HEADS * V_HEAD), F32),
        compiler_params=pltpu.CompilerParams(
            dimension_semantics=("arbitrary", "arbitrary", "arbitrary"), vmem_limit_bytes=VMEM_LIMIT),
        name="attention",
    )(q, k, v)


HALO = 8
LORA_W = LANES
GATE_W = PROJ_RW - 3 * RW_DIM - 2 * LORA_W


def _head_ones(width, head):
    i = np.arange(width) // head
    return jnp.asarray(i[:, None] == i[None, :], BF16)


def _head_sum(x, ones_bd):
    hi = x.astype(BF16)
    lo = (x - hi.astype(F32)).astype(BF16)
    return (jnp.dot(hi, ones_bd, preferred_element_type=F32)
            + jnp.dot(lo, ones_bd, preferred_element_type=F32))


def _dot3(a, b_hi, b_lo):
    hi = a.astype(BF16)
    lo = (a - hi.astype(F32)).astype(BF16)
    d = lambda u, w: jnp.dot(u, w, preferred_element_type=F32)
    return d(hi, b_hi) + (d(hi, b_lo) + d(lo, b_hi))


def _rwkv_prep_kernel(x_ref, prev_ref, next_ref, conv_ref, kkg_ref, ka_ref, rk_ref, w0_ref, a0_ref,
                      w2h_ref, w2l_ref, a2_ref, g2_ref, ones_ref,
                      r_ref, v_ref, kk_ref, lw_ref, bk_ref, kd_ref, bonus_ref, gg_ref, *, ctx_tiles, n_tiles):
    i = pl.program_id(1)
    x = x_ref[0]
    tm, W = x.shape
    C = RW_DIM
    first = (i == 0) | (i == ctx_tiles)
    last = (i == ctx_tiles - 1) | (i == n_tiles - 1)
    prev_row = jnp.where(first, 0.0, prev_ref[0, HALO - 1:HALO])
    next_row = jnp.where(last, 0.0, next_ref[0, 0:1])
    row = lax.broadcasted_iota(jnp.int32, (tm, W), 0)
    x_dn = jnp.where(row == 0, prev_row, pltpu.roll(x, 1, 0))
    x_up = jnp.where(row == tm - 1, next_row, pltpu.roll(x, tm - 1, 0))
    xc = x_dn * conv_ref[0:1] + x * conv_ref[1:2] + x_up * conv_ref[2:3]

    r, k, v = xc[:, :C], xc[:, C:2 * C], xc[:, 2 * C:3 * C]
    lora_w = jnp.tanh(xc[:, 3 * C:3 * C + LORA_W])
    lora_a = xc[:, 3 * C + LORA_W:3 * C + 2 * LORA_W]
    lg = xc[:, 3 * C + 2 * LORA_W:]
    ones_bd = ones_ref[...]
    kq = k * kkg_ref[...]
    kk = kq * lax.rsqrt(_head_sum(kq * kq, ones_bd) + 1e-12)
    r_ref[0], v_ref[0], kk_ref[0] = r, v, kk

    k_sum = jnp.zeros_like(k)
    for d in range(2):
        z = w0_ref[d:d + 1] + _dot3(lora_w, w2h_ref[d], w2l_ref[d])
        softplus_neg = jnp.maximum(-z, 0.0) + jnp.log(1.0 + jnp.exp(-jnp.abs(z)))
        lw_ref[d, 0] = -jnp.exp(-softplus_neg - 0.5)
        a = jax.nn.sigmoid(a0_ref[d:d + 1] + _dot(lora_a, a2_ref[d]))
        kd = k * (1.0 + (a - 1.0) * ka_ref[...])
        bk_ref[d, 0] = kk * a
        kd_ref[d, 0] = kd
        k_sum = k_sum + kd
    bonus_ref[0] = _head_sum(r * k_sum * rk_ref[...], ones_bd) * v
    gg_ref[0] = _dot(jax.nn.sigmoid(lg), g2_ref[...])


def _rwkv_prep(proj, p, n_ctx):
    B, Tt, _ = proj.shape
    C = RW_DIM
    nt = Tt // ROW_TILE
    hb = ROW_TILE // HALO
    pad_cols = lambda w: jnp.pad(w, ((0, 0), (0, PROJ_RW - w.shape[1])))
    w2 = jnp.stack([jnp.pad(p['decay_w2'][0], ((0, LORA_W - DECAY_LORA), (0, 0))),
                    jnp.pad(p['decay_w2'][1], ((DECAY_LORA, 0), (0, 0)))])
    w2h = w2.astype(BF16)
    w2l = (w2 - w2h.astype(F32)).astype(BF16)
    a2 = jnp.stack([jnp.pad(p['aicl_a2'][0], ((0, LORA_W - AICL_LORA), (0, 0))),
                    jnp.pad(p['aicl_a2'][1], ((AICL_LORA, 0), (0, 0)))]).astype(BF16)
    g2 = jnp.pad(p['gate_g2'], ((0, GATE_W - GATE_LORA), (0, 0))).astype(BF16)
    row = lambda b, i: (b, i, 0)
    drow = lambda b, i: (0, b, i, 0)
    const2 = lambda b, i: (0, 0)
    const3 = lambda b, i: (0, 0, 0)
    vec = pl.BlockSpec((1, C), const2)
    out_row = pl.BlockSpec((1, ROW_TILE, C), row)
    out_dir = pl.BlockSpec((2, 1, ROW_TILE, C), drow)
    sds = jax.ShapeDtypeStruct((B, Tt, C), F32)
    sds2 = jax.ShapeDtypeStruct((2, B, Tt, C), F32)
    return pl.pallas_call(
        functools.partial(_rwkv_prep_kernel, ctx_tiles=n_ctx // ROW_TILE, n_tiles=nt),
        grid=(B, nt),
        in_specs=[
            pl.BlockSpec((1, ROW_TILE, PROJ_RW), row),
            pl.BlockSpec((1, HALO, PROJ_RW), lambda b, i: (b, jnp.maximum(i * hb - 1, 0), 0)),
            pl.BlockSpec((1, HALO, PROJ_RW), lambda b, i: (b, jnp.minimum((i + 1) * hb, nt * hb - 1), 0)),
            pl.BlockSpec((3, PROJ_RW), const2),
            vec, vec, vec,
            pl.BlockSpec((2, C), const2),
            pl.BlockSpec((2, C), const2),
            pl.BlockSpec((2, LORA_W, C), const3),
            pl.BlockSpec((2, LORA_W, C), const3),
            pl.BlockSpec((2, LORA_W, C), const3),
            pl.BlockSpec((GATE_W, C), const2),
            pl.BlockSpec((C, C), const2),
        ],
        out_specs=[out_row, out_row, out_row, out_dir, out_dir, out_dir, out_row, out_row],
        out_shape=[sds, sds, sds, sds2, sds2, sds2, sds, sds],
        compiler_params=pltpu.CompilerParams(
            dimension_semantics=("arbitrary", "arbitrary"), vmem_limit_bytes=VMEM_LIMIT),
        name="rwkv_prep",
    )(proj, proj, proj, pad_cols(p['shift_conv']), p['k_k'].reshape(1, C), p['k_a'].reshape(1, C),
      p['r_k'].reshape(1, C), p['decay_w0'], p['aicl_a0'], w2h, w2l, a2, g2, _head_ones(C, RW_HEAD))


def _merge_kernel(att_ref, yf_ref, yb_ref, bonus_ref, gg_ref, gate_ref, x_ref, ga_ref, gnw_ref, gnb_ref, ones_ref,
                  wm_ref, wr_ref, wo_ref, o_ref):
    D = x_ref.shape[-1]
    ones_bd = ones_ref[...]
    y = yf_ref[0] + yb_ref[0]
    yc = y - _head_sum(y, ones_bd) * (1.0 / RW_HEAD)
    var = _head_sum(yc * yc, ones_bd) * (1.0 / RW_HEAD)
    y_n = yc * lax.rsqrt(var + GN_EPS) * gnw_ref[...] + gnb_ref[...]
    rw = (y_n + bonus_ref[0]) * gg_ref[0]
    g = jax.nn.sigmoid(gate_ref[0])
    mix = g[:, :D] * _dot(att_ref[0], wm_ref[...]) + g[:, D:] * _dot(rw, wr_ref[...])
    o_ref[0] = x_ref[0] + ga_ref[0] * _dot(mix, wo_ref[...])


def _merge(att, y_f, y_b, bonus, gg, proj, x, g_a, p, n_ctx):
    B, T, D = x.shape
    C = RW_DIM
    ctx_tiles = n_ctx // ROW_TILE
    gate_blk = PROJ_RW // PROJ_GATE
    row = lambda b, i: (b, i, 0)
    lat_row = lambda b, i: (b, i + ctx_tiles, 0)
    const = lambda b, i: (0, 0)
    return pl.pallas_call(
        _merge_kernel,
        grid=(B, T // ROW_TILE),
        in_specs=[
            pl.BlockSpec((1, ROW_TILE, att.shape[-1]), row),
            pl.BlockSpec((1, ROW_TILE, C), row),
            pl.BlockSpec((1, ROW_TILE, C), row),
            pl.BlockSpec((1, ROW_TILE, C), lat_row),
            pl.BlockSpec((1, ROW_TILE, C), lat_row),
            pl.BlockSpec((1, ROW_TILE, PROJ_GATE), lambda b, i: (b, i + ctx_tiles, gate_blk)),
            pl.BlockSpec((1, ROW_TILE, D), row),
            pl.BlockSpec((1, 1, D), lambda b, i: (b, 0, 0)),
            pl.BlockSpec((1, C), const),
            pl.BlockSpec((1, C), const),
            pl.BlockSpec((C, C), const),
            pl.BlockSpec(p['w_o_mla'].shape, const),
            pl.BlockSpec(p['w_o_rwkv'].shape, const),
            pl.BlockSpec(p['w_out'].shape, const),
        ],
        out_specs=pl.BlockSpec((1, ROW_TILE, D), row),
        out_shape=jax.ShapeDtypeStruct((B, T, D), F32),
        compiler_params=pltpu.CompilerParams(
            dimension_semantics=("arbitrary", "arbitrary"), vmem_limit_bytes=VMEM_LIMIT),
        name="merge",
    )(att, y_f, y_b, bonus, gg, proj, x, g_a[:, None, :], p['gn_w'].reshape(1, C), p['gn_b'].reshape(1, C),
      _head_ones(C, RW_HEAD), p['w_o_mla'].astype(BF16), p['w_o_rwkv'].astype(BF16), p['w_out'].astype(BF16))


MOE_BLOCK = 512


NEG_INF = float("-inf")
HI_MASK = 0xFFFF0000


def _pack_pair(lo, hi):
    lo_b = lax.bitcast_convert_type(lo.astype(BF16).astype(F32), jnp.uint32)
    hi_b = lax.bitcast_convert_type(hi.astype(BF16).astype(F32), jnp.uint32)
    return (lo_b >> 16) | (hi_b & jnp.uint32(HI_MASK))


def _unpack_pair(w):
    lo = lax.bitcast_convert_type(w << 16, F32)
    hi = lax.bitcast_convert_type(w & jnp.uint32(HI_MASK), F32)
    return lo.astype(BF16), hi.astype(BF16)


def _row_max(x):
    return jnp.max(x, axis=-1, keepdims=True)


def _first_index_of(x, value, lane_f):
    return jnp.min(jnp.where(x == value, lane_f, float(x.shape[-1])), axis=-1, keepdims=True)


def _router_kernel(x_ref, gain_ref, sh_ref, sc_ref, wh_ref, wm_ref, wl_ref, bias_ref,
                   u_ref, ids_ref, gates_ref, ranks_ref, counts_ref, carry_ref):
    @pl.when((pl.program_id(0) == 0) & (pl.program_id(1) == 0))
    def _():
        carry_ref[...] = jnp.zeros_like(carry_ref)

    x = x_ref[0]
    tm, D = x.shape
    E = bias_ref.shape[-1]
    u = x * lax.rsqrt(jnp.mean(x * x, axis=-1, keepdims=True) + EPS) * gain_ref[...]
    u = u * (1.0 + sc_ref[0]) + sh_ref[0]
    u_ref[0] = _pack_pair(u[:, :D // 2], u[:, D // 2:])

    uh, um, ul = _split3(u)
    wh, wm, wl = wh_ref[...], wm_ref[...], wl_ref[...]
    d = lambda a, b: jnp.dot(a, b, preferred_element_type=F32)
    logits = d(uh, wh) + (d(uh, wm) + d(um, wh)) + (d(uh, wl) + d(um, wm) + d(ul, wh))
    scores = jax.nn.sigmoid(logits)
    sel = scores + bias_ref[...]

    lane_i = lax.broadcasted_iota(jnp.int32, (tm, E), 1)
    lane_f = lane_i.astype(F32)
    out_f = lax.broadcasted_iota(jnp.int32, (tm, LANES), 1).astype(F32)
    per_group = E // N_GROUPS
    grp_f = jnp.floor(lane_f * (1.0 / per_group))

    gs = jnp.full((tm, LANES), NEG_INF, F32)
    for g in range(N_GROUPS):
        sg = jnp.where(lane_i >= g * per_group, jnp.where(lane_i < (g + 1) * per_group, sel, NEG_INF), NEG_INF)
        m1 = _row_max(sg)
        i1 = _first_index_of(sg, m1, lane_f)
        m2 = _row_max(jnp.where(lane_f == i1, NEG_INF, sg))
        gs = jnp.where(out_f == g, m1 + m2, gs)

    allow = jnp.zeros((tm, E), F32)
    for _ in range(TOPK_GROUPS):
        m = _row_max(gs)
        i = _first_index_of(gs, m, out_f)
        gs = jnp.where(out_f == i, NEG_INF, gs)
        allow = jnp.where(grp_f == i, 1.0, allow)
    selm = jnp.where(allow > 0.0, sel, NEG_INF)

    ids = jnp.zeros((tm, LANES), F32)
    gts = jnp.zeros((tm, LANES), F32)
    member = jnp.zeros((tm, E), F32)
    idx_cols = []
    gsum = jnp.zeros((tm, 1), F32)
    for k in range(TOP_K):
        m = _row_max(selm)
        i = _first_index_of(selm, m, lane_f)
        hit = lane_f == i
        gk = jnp.sum(jnp.where(hit, scores, 0.0), axis=-1, keepdims=True)
        selm = jnp.where(hit, NEG_INF, selm)
        member = jnp.where(hit, 1.0, member)
        ids = jnp.where(out_f == k, i, ids)
        gts = jnp.where(out_f == k, gk, gts)
        idx_cols.append(i)
        gsum = gsum + gk
    gts = gts / gsum * ROUTED_SCALE

    r2 = lax.broadcasted_iota(jnp.int32, (tm, tm), 0)
    c2 = lax.broadcasted_iota(jnp.int32, (tm, tm), 1)
    before = jnp.where(r2 > c2, 1.0, 0.0).astype(BF16)
    mem_b = member.astype(BF16)
    carry = carry_ref[...]
    pos = carry + jnp.dot(before, mem_b, preferred_element_type=F32)
    rk = jnp.zeros((tm, LANES), F32)
    for k in range(TOP_K):
        rk = jnp.where(out_f == k, jnp.sum(jnp.where(lane_f == idx_cols[k], pos, 0.0), axis=-1, keepdims=True), rk)
    colsum = jnp.dot(jnp.ones((8, tm), BF16), mem_b, preferred_element_type=F32)[0:1]
    carry_ref[...] = carry + colsum
    counts_ref[...] = carry + colsum
    ids_ref[0] = ids.astype(jnp.int32)
    ranks_ref[0] = rk.astype(jnp.int32)
    gates_ref[0] = gts


def _router(x1, sh_m, sc_m, p):
    B, T, D = x1.shape
    E = N_EXPERTS
    wh, wm, wl = _split3(p['router_w'])
    row = lambda b, i: (b, i, 0)
    const = lambda b, i: (0, 0)
    vec = lambda b, i: (b, 0, 0)
    lane_out = lambda dt: jax.ShapeDtypeStruct((B, T, LANES), dt)
    return pl.pallas_call(
        _router_kernel,
        grid=(B, T // ROW_TILE),
        in_specs=[
            pl.BlockSpec((1, ROW_TILE, D), row),
            pl.BlockSpec((1, D), const),
            pl.BlockSpec((1, 1, D), vec),
            pl.BlockSpec((1, 1, D), vec),
            pl.BlockSpec((D, E), const),
            pl.BlockSpec((D, E), const),
            pl.BlockSpec((D, E), const),
            pl.BlockSpec((1, E), const),
        ],
        out_specs=[
            pl.BlockSpec((1, ROW_TILE, D // 2), row),
            pl.BlockSpec((1, ROW_TILE, LANES), row),
            pl.BlockSpec((1, ROW_TILE, LANES), row),
            pl.BlockSpec((1, ROW_TILE, LANES), row),
            pl.BlockSpec((1, E), const),
        ],
        out_shape=[
            jax.ShapeDtypeStruct((B, T, D // 2), jnp.uint32),
            lane_out(jnp.int32), lane_out(F32), lane_out(jnp.int32),
            jax.ShapeDtypeStruct((1, E), F32),
        ],
        scratch_shapes=[pltpu.VMEM((1, E), F32)],
        compiler_params=pltpu.CompilerParams(
            dimension_semantics=("arbitrary", "arbitrary"), vmem_limit_bytes=VMEM_LIMIT),
        name="router",
    )(x1, p['norm_ffn'].reshape(1, D), sh_m[:, None, :], sc_m[:, None, :], wh, wm, wl,
      p['router_bias'].reshape(1, E))


def _slot_kernel(ids_ref, ranks_ref, base_ref, o_ref):
    ids = ids_ref[...].astype(F32)
    tm = ids.shape[0]
    E = base_ref.shape[-1]
    lane_e = lax.broadcasted_iota(jnp.int32, (tm, E), 1).astype(F32)
    out_lane = lax.broadcasted_iota(jnp.int32, (tm, LANES), 1)
    first = jnp.zeros((tm, LANES), F32)
    for k in range(TOP_K):
        fk = jnp.sum(jnp.where(lane_e == ids[:, k:k + 1], base_ref[...], 0.0), axis=-1, keepdims=True)
        first = jnp.where(out_lane == k, fk, first)
    o_ref[...] = first.astype(jnp.int32) + ranks_ref[...]


def _slots(ids, ranks, base):
    n_tok = ids.shape[0]
    E = base.shape[-1]
    row = pl.BlockSpec((ROW_TILE, LANES), lambda i: (i, 0))
    return pl.pallas_call(
        _slot_kernel,
        grid=(n_tok // ROW_TILE,),
        in_specs=[row, row, pl.BlockSpec((1, E), lambda i: (0, 0))],
        out_specs=row,
        out_shape=jax.ShapeDtypeStruct((n_tok, LANES), jnp.int32),
        compiler_params=pltpu.CompilerParams(dimension_semantics=("arbitrary",)),
        name="slots",
    )(ids, ranks, base)


DMA_PRIORITIES = 2


def _row_copy(src, src_row, dst, dst_row, sem):
    return pltpu.make_async_copy(src.at[pl.ds(src_row, 1)], dst.at[pl.ds(dst_row, 1)], sem)


def _dispatch_kernel(dest_ref, u_ref, slots_in_ref, slots_ref, sem):
    del slots_in_ref
    tm = u_ref.shape[0]

    def issue(r, carry):
        for k in range(TOP_K):
            _row_copy(u_ref, r, slots_ref, dest_ref[0, 0, r * TOP_K + k], sem).start(priority=k % DMA_PRIORITIES)
        return carry

    lax.fori_loop(0, tm, issue, 0)
    for k in range(TOP_K):
        pltpu.make_async_copy(u_ref, slots_ref.at[pl.ds(0, tm)], sem).wait()


def _dispatch(dest, u_rows, n_slots):
    n_tok, W = u_rows.shape
    nt = n_tok // ROW_TILE
    return pl.pallas_call(
        _dispatch_kernel,
        grid=(nt,),
        in_specs=[
            pl.BlockSpec((1, 1, ROW_TILE * TOP_K), lambda i: (i, 0, 0), memory_space=pltpu.SMEM),
            pl.BlockSpec((ROW_TILE, W), lambda i: (i, 0)),
            pl.BlockSpec(memory_space=pl.ANY),
        ],
        out_specs=pl.BlockSpec(memory_space=pl.ANY),
        out_shape=jax.ShapeDtypeStruct((n_slots, W), jnp.uint32),
        scratch_shapes=[pltpu.SemaphoreType.DMA(())],
        input_output_aliases={2: 0},
        compiler_params=pltpu.CompilerParams(
            dimension_semantics=("arbitrary",), vmem_limit_bytes=VMEM_LIMIT),
        name="dispatch",
    )(dest, u_rows, jnp.zeros((n_slots, W), jnp.uint32))


def _moe_ffn_kernel(be_ref, nu_ref, x_ref, w1_ref, w3_ref, w2_ref, o_ref, w13_ref, w2b_ref):
    j = pl.program_id(0)
    F = w1_ref.shape[-1]
    half = x_ref.shape[-1]

    @pl.when((j == 0) | (be_ref[j] != be_ref[jnp.maximum(j - 1, 0)]))
    def _():
        w13_ref[:, :F] = w1_ref[0].astype(BF16)
        w13_ref[:, F:] = w3_ref[0].astype(BF16)
        w2b_ref[...] = w2_ref[0].astype(BF16)

    @pl.when(j < nu_ref[0])
    def _():
        lo, hi = _unpack_pair(x_ref[...])
        h = (jnp.dot(lo, w13_ref[:half], preferred_element_type=F32)
             + jnp.dot(hi, w13_ref[half:], preferred_element_type=F32))
        h1, h3 = h[:, :F], h[:, F:]
        y = _dot(h1 * jax.nn.sigmoid(h1) * h3, w2b_ref[...])
        o_ref[...] = _pack_pair(y[:, :half], y[:, half:])

    @pl.when(j >= nu_ref[0])
    def _():
        o_ref[...] = jnp.zeros_like(o_ref)


def _moe_ffn(block_e, n_used, x_slots, w1, w3, w2):
    slots, W = x_slots.shape
    E, D, F = w1.shape
    nblk = slots // MOE_BLOCK
    grid_spec = pltpu.PrefetchScalarGridSpec(
        num_scalar_prefetch=2,
        grid=(nblk,),
        in_specs=[
            pl.BlockSpec((MOE_BLOCK, W), lambda j, be, nu: (jnp.minimum(j, nu[0] - 1), 0)),
            pl.BlockSpec((1, D, F), lambda j, be, nu: (be[j], 0, 0)),
            pl.BlockSpec((1, D, F), lambda j, be, nu: (be[j], 0, 0)),
            pl.BlockSpec((1, F, D), lambda j, be, nu: (be[j], 0, 0)),
        ],
        out_specs=pl.BlockSpec((MOE_BLOCK, W), lambda j, be, nu: (j, 0)),
        scratch_shapes=[pltpu.VMEM((D, 2 * F), BF16), pltpu.VMEM((F, D), BF16)],
    )
    return pl.pallas_call(
        _moe_ffn_kernel,
        grid_spec=grid_spec,
        out_shape=jax.ShapeDtypeStruct((slots, W), jnp.uint32),
        compiler_params=pltpu.CompilerParams(
            dimension_semantics=("arbitrary",), vmem_limit_bytes=VMEM_LIMIT),
        name="moe_ffn",
    )(block_e, n_used, x_slots, w1, w3, w2)


COMBINE_ROWS = 32


def _combine_kernel(dest_ref, gates_ref, u_ref, x_ref, gm_ref, w1_ref, w3_ref, w2_ref, ys_ref, o_ref, buf_ref, sem):
    tm, half = u_ref.shape

    def issue(r, carry):
        for k in range(TOP_K):
            _row_copy(ys_ref, dest_ref[0, 0, r * TOP_K + k], buf_ref.at[k], r, sem).start(
                priority=k % DMA_PRIORITIES)
        return carry

    lax.fori_loop(0, tm, issue, 0)

    ulo, uhi = _unpack_pair(u_ref[...])
    both = lambda w_ref: (jnp.dot(ulo, w_ref[:half], preferred_element_type=F32)
                          + jnp.dot(uhi, w_ref[half:], preferred_element_type=F32))
    h1, h3 = both(w1_ref), both(w3_ref)
    o_ref[...] = x_ref[...] + gm_ref[0] * _dot(h1 * jax.nn.sigmoid(h1) * h3, w2_ref[...])

    for k in range(TOP_K):
        pltpu.make_async_copy(ys_ref.at[pl.ds(0, tm)], buf_ref.at[k], sem).wait()

    for rb in range(tm // COMBINE_ROWS):
        rows = slice(rb * COMBINE_ROWS, (rb + 1) * COMBINE_ROWS)
        gates = gates_ref[rows, :]
        for c in range(half // LANES):
            lo_acc = jnp.zeros((COMBINE_ROWS, LANES), F32)
            hi_acc = jnp.zeros((COMBINE_ROWS, LANES), F32)
            for k in range(TOP_K):
                w = buf_ref[k, rows, c * LANES:(c + 1) * LANES]
                lo, hi = _unpack_pair(w)
                gk = gates[:, k:k + 1]
                lo_acc = lo_acc + gk * lo.astype(F32)
                hi_acc = hi_acc + gk * hi.astype(F32)
            gm = gm_ref[0]
            lo_cols = slice(c * LANES, (c + 1) * LANES)
            hi_cols = slice(half + c * LANES, half + (c + 1) * LANES)
            o_ref[rows, lo_cols] = o_ref[rows, lo_cols] + gm[:, lo_cols] * lo_acc
            o_ref[rows, hi_cols] = o_ref[rows, hi_cols] + gm[:, hi_cols] * hi_acc


def _combine(dest, gates, u_rows, x1, g_m, y_slots, p):
    B, T, D = x1.shape
    n_tok = B * T
    W = u_rows.shape[1]
    tiles_per_batch = T // ROW_TILE
    row = lambda i: (i, 0)
    const = lambda i: (0, 0)
    out = pl.pallas_call(
        _combine_kernel,
        grid=(n_tok // ROW_TILE,),
        in_specs=[
            pl.BlockSpec((1, 1, ROW_TILE * TOP_K), lambda i: (i, 0, 0), memory_space=pltpu.SMEM),
            pl.BlockSpec((ROW_TILE, LANES), row),
            pl.BlockSpec((ROW_TILE, W), row),
            pl.BlockSpec((ROW_TILE, D), row),
            pl.BlockSpec((1, 1, D), lambda i: (i // tiles_per_batch, 0, 0)),
            pl.BlockSpec(p['shared_w1'].shape, const),
            pl.BlockSpec(p['shared_w3'].shape, const),
            pl.BlockSpec(p['shared_w2'].shape, const),
            pl.BlockSpec(memory_space=pl.ANY),
        ],
        out_specs=pl.BlockSpec((ROW_TILE, D), row),
        out_shape=jax.ShapeDtypeStruct((n_tok, D), F32),
        scratch_shapes=[pltpu.VMEM((TOP_K, ROW_TILE, W), jnp.uint32), pltpu.SemaphoreType.DMA(())],
        compiler_params=pltpu.CompilerParams(
            dimension_semantics=("arbitrary",), vmem_limit_bytes=VMEM_LIMIT),
        name="combine",
    )(dest, gates.reshape(n_tok, LANES), u_rows, x1.reshape(n_tok, D), g_m[:, None, :],
      p['shared_w1'].astype(BF16), p['shared_w3'].astype(BF16), p['shared_w2'].astype(BF16), y_slots)
    return out.reshape(B, T, D)


def _moe(x1, sh_m, sc_m, g_m, p):
    B, T, D = x1.shape
    n_tok = B * T
    u_packed, ids, gates, ranks, counts = _router(x1, sh_m, sc_m, p)

    counts = counts[0].astype(jnp.int32)
    padded = (counts + MOE_BLOCK - 1) // MOE_BLOCK * MOE_BLOCK
    padded_end = jnp.cumsum(padded)
    base = padded_end - padded
    n_blocks = n_tok * TOP_K // MOE_BLOCK + N_EXPERTS
    n_used = (padded_end[-1] // MOE_BLOCK).astype(jnp.int32)
    blk = jnp.minimum(jnp.arange(n_blocks), n_used - 1) * MOE_BLOCK
    block_e = jnp.minimum(jnp.searchsorted(padded_end, blk, side='right'), N_EXPERTS - 1).astype(jnp.int32)
    dest = _slots(ids.reshape(n_tok, LANES), ranks.reshape(n_tok, LANES), base.astype(F32).reshape(1, N_EXPERTS))
    dest = dest[:, :TOP_K].reshape(n_tok // ROW_TILE, 1, ROW_TILE * TOP_K)

    u_rows = u_packed.reshape(n_tok, D // 2)
    x_slots = _dispatch(dest, u_rows, n_blocks * MOE_BLOCK)
    y_slots = _moe_ffn(block_e, n_used.reshape(1), x_slots, p['expert_w1'], p['expert_w3'], p['expert_w2'])
    return _combine(dest, gates, u_rows, x1, g_m, y_slots, p)
```

```python
import functools

import jax
import jax.numpy as jnp
import numpy as np
from jax import lax
from jax.experimental import pallas as pl
from jax.experimental.pallas import tpu as pltpu

F32 = jnp.float32
BF16 = jnp.bfloat16

GRID_W = 64
EPS = 1e-6
MLA_HEADS = 8
QK_NOPE = 64
QK_ROPE = 32
QK_HEAD = QK_NOPE + QK_ROPE
V_HEAD = 64
Q_LORA = 256
KV_LORA = 128
ROPE_BASE = 10000.0
ATTN_SCALE = QK_HEAD ** -0.5
RW_HEADS = 8
RW_HEAD = 64
RW_DIM = RW_HEADS * RW_HEAD
DECAY_LORA = 64
AICL_LORA = 64
GATE_LORA = 160
GN_EPS = 64e-5
N_EXPERTS = 256
TOP_K = 8
N_GROUPS = 8
TOPK_GROUPS = 4
ROUTED_SCALE = 2.5

LANES = 128
MXU_DIM = 256
VMEM_LIMIT = 56 * 1024 * 1024

WKV_CHUNK = 64
HEADS_PER_GROUP = MXU_DIM // RW_HEAD
HEAD_PAD = LANES


def _dot(a, b):
    return jnp.dot(a.astype(BF16), b.astype(BF16), preferred_element_type=F32)


def _dot_t(a, b):
    return lax.dot_general(a.astype(BF16), b.astype(BF16), (((1,), (1,)), ((), ())),
                           preferred_element_type=F32)


def _split3(x):
    h = x.astype(BF16)
    r1 = x - h.astype(F32)
    m = r1.astype(BF16)
    lo = (r1 - m.astype(F32)).astype(BF16)
    return h, m, lo


def _dot_hi(a_bf16_exact, x):
    h, m, lo = _split3(x)
    d = lambda y: jnp.dot(a_bf16_exact, y, preferred_element_type=F32)
    return d(h) + d(m) + d(lo)


def _wkv_chunk(r, v, kk, lw, bk, kd, s0, reverse, emit):
    L, G = r.shape
    row = lax.broadcasted_iota(jnp.int32, (L, G), 0)
    lane = lax.broadcasted_iota(jnp.int32, (L, G), 1)
    diff = (lane % L - row) if reverse else (row - lane % L)
    strict = diff > 0
    incl = diff >= 0
    r2 = lax.broadcasted_iota(jnp.int32, (L, L), 0)
    c2 = lax.broadcasted_iota(jnp.int32, (L, L), 1)
    tri = jnp.where(((c2 - r2) if reverse else (r2 - c2)) >= 0, 1.0, 0.0).astype(BF16)
    tri_ones = jnp.concatenate([tri, jnp.ones((L, L), BF16)], axis=0)
    lane_head = lane // RW_HEAD
    rowb = lax.broadcasted_iota(jnp.int32, (G, G), 0) // RW_HEAD
    colb = lax.broadcasted_iota(jnp.int32, (G, G), 1) // RW_HEAD

    def stack(x):
        return jnp.concatenate(
            [jnp.where(lane_head == h, x, 0.0) for h in range(HEADS_PER_GROUP)], axis=0).astype(BF16)

    sums = _dot_hi(tri_ones, lw)
    cum_in = sums[:L]
    g_in = jnp.exp(cum_in)
    g_inv = jnp.exp(-cum_in)
    g_ex = jnp.exp(cum_in - lw)
    g_tot = jnp.exp(sums[L:])
    a_h = -kk * g_ex
    b_h = bk * g_inv
    k_h = kd * g_inv
    r_h = r * g_in
    yield

    gram = _dot_t(jnp.concatenate([a_h, r_h], axis=0),
                  jnp.concatenate([stack(b_h), stack(k_h)], axis=0))
    ab = jnp.where(strict, gram[:L, :G], 0.0)
    ak = jnp.where(strict, gram[:L, G:], 0.0)
    rb = jnp.where(incl, gram[L:, :G], 0.0)
    rk = jnp.where(incl, gram[L:, G:], 0.0)
    yield

    tm = jnp.where(diff == 0, 1.0, 0.0) + ab
    p = ab
    v_st = stack(v)
    akv = _dot(ak, v_st)
    for _ in range(int(np.log2(L)) - 1):
        p = _dot(p, stack(p))
        yield
        tm = tm + _dot(tm, stack(p))
        yield

    wu = _dot(tm, jnp.concatenate([stack(a_h), stack(akv)], axis=1))
    w_t, u_t = wu[:, :G], wu[:, G:]
    yield
    rbwu = _dot(rb, jnp.concatenate([stack(w_t), stack(u_t)], axis=1))
    r_t = r_h + rbwu[:, :G]
    y_t = rbwu[:, G:] + _dot(rk, v_st)
    yield

    s0_st = stack(s0)
    y = _dot_t(r_t, s0_st) + y_t
    u = _dot_t(w_t, s0_st) + u_t
    yield
    uv = jnp.concatenate([u, v], axis=0)
    bkc = jnp.concatenate([b_h, k_h], axis=0)
    upd = lax.dot_general(uv.astype(BF16), bkc.astype(BF16), (((0,), (0,)), ((), ())),
                          preferred_element_type=F32)
    upd = jnp.where(rowb == colb, upd, 0.0)
    upd_d = upd[0:L]
    for h in range(1, HEADS_PER_GROUP):
        upd_d = upd_d + upd[h * L:(h + 1) * L]
    emit(y, (s0 + upd_d) * g_tot)


def _wkv_kernel(*refs, n_groups, n_batch):
    ins, (yf_ref, yb_ref, s_ref) = refs[:12], refs[12:]
    G = MXU_DIM

    @pl.when(pl.program_id(1) == 0)
    def _():
        s_ref[...] = jnp.zeros_like(s_ref)

    chains = []
    for d, y_ref in enumerate((yf_ref, yb_ref)):
        r_ref, v_ref, kk_ref, lw_ref, bk_ref, kd_ref = ins[6 * d:6 * d + 6]
        for bi in range(n_batch):
            for g in range(n_groups):
                sl = slice(g * G, (g + 1) * G)

                def emit(y, s_new, y_ref=y_ref, d=d, bi=bi, g=g, sl=sl):
                    y_ref[bi, :, sl] = y
                    s_ref[d, bi, g] = s_new

                chains.append(_wkv_chunk(
                    r_ref[bi, :, sl], v_ref[bi, :, sl], kk_ref[bi, :, sl], lw_ref[0, bi, :, sl],
                    bk_ref[0, bi, :, sl], kd_ref[0, bi, :, sl], s_ref[d, bi, g], d == 1, emit))
    while chains:
        chains = [c for c in chains if next(c, StopIteration) is not StopIteration]


WKV_BATCH = 2


def _wkv_scan(r, v, kk, lw, bk, kd, n_ctx):
    B, Ttot, C = r.shape
    L = WKV_CHUNK
    nc = Ttot // L
    ncc = n_ctx // L
    nl = nc - ncc
    n_groups = C // MXU_DIM
    nb = WKV_BATCH

    cid = (lambda s: s, lambda s: jnp.where(s < ncc, ncc - 1 - s, nc + ncc - 1 - s))
    first_out = (0, nl - 1)
    in_specs, out_specs = [], []
    for d in range(2):
        shared = pl.BlockSpec((nb, L, C), lambda b, s, d=d: (b, cid[d](s), 0))
        per_dir = pl.BlockSpec((1, nb, L, C), lambda b, s, d=d: (d, b, cid[d](s), 0))
        in_specs += [shared, shared, shared, per_dir, per_dir, per_dir]
        out_specs.append(pl.BlockSpec(
            (nb, L, C), lambda b, s, d=d: (b, jnp.where(s < ncc, first_out[d], cid[d](s) - ncc), 0)))
    out = jax.ShapeDtypeStruct((B, nl * L, C), F32)
    return pl.pallas_call(
        functools.partial(_wkv_kernel, n_groups=n_groups, n_batch=nb),
        grid=(B // nb, nc),
        in_specs=in_specs,
        out_specs=out_specs,
        out_shape=[out, out],
        scratch_shapes=[pltpu.VMEM((2, nb, n_groups, L, MXU_DIM), F32)],
        compiler_params=pltpu.CompilerParams(
            dimension_semantics=("arbitrary", "arbitrary"), vmem_limit_bytes=VMEM_LIMIT),
        name="wkv_scan",
    )(r, v, kk, lw, bk, kd, r, v, kk, lw, bk, kd)


def kernel(x, c, ctx, c_ctx, ada_w, ada_b, norm_mix, norm_ffn, w_in, shift_conv, q_lat_norm, w_uq, kv_lat_norm, w_ukv, q_norm, k_norm, w_o_mla, decay_w0, decay_w2, aicl_a0, aicl_a2, k_k, k_a, r_k, gn_w, gn_b, gate_g2, w_o_rwkv, w_out, router_w, router_bias, expert_w1, expert_w3, expert_w2, shared_w1, shared_w3, shared_w2):
    B, T, D = x.shape
    n_ctx = ctx.shape[1]
    i0 = 0
    p = dict(norm_mix=norm_mix[i0], norm_ffn=norm_ffn[i0], w_in=w_in[i0], shift_conv=shift_conv[i0],
             q_lat_norm=q_lat_norm[i0], w_uq=w_uq[i0], kv_lat_norm=kv_lat_norm[i0], w_ukv=w_ukv[i0],
             q_norm=q_norm[i0], k_norm=k_norm[i0], w_o_mla=w_o_mla[i0],
             decay_w0=decay_w0[i0], decay_w2=decay_w2[i0], aicl_a0=aicl_a0[i0], aicl_a2=aicl_a2[i0],
             k_k=k_k[i0], k_a=k_a[i0], r_k=r_k[i0], gn_w=gn_w[i0], gn_b=gn_b[i0], gate_g2=gate_g2[i0],
             w_o_rwkv=w_o_rwkv[i0], w_out=w_out[i0], router_w=router_w[i0], router_bias=router_bias[i0],
             expert_w1=expert_w1[i0], expert_w3=expert_w3[i0], expert_w2=expert_w2[i0],
             shared_w1=shared_w1[i0], shared_w3=shared_w3[i0], shared_w2=shared_w2[i0])

    mod = _ada_modulation(c, c_ctx, ada_w[i0], ada_b[i0])
    sh_a, sc_a, g_a, sh_m, sc_m, g_m = jnp.split(mod[:B], 6, axis=-1)
    csh_a, csc_a = jnp.split(mod[B], 6, axis=-1)[:2]
    sh2 = jnp.stack([jnp.broadcast_to(csh_a, (B, D)), sh_a], axis=1)
    sc2 = jnp.stack([jnp.broadcast_to(csc_a, (B, D)), sc_a], axis=1)

    proj = _in_proj(ctx, x, sh2, sc2, p['norm_mix'], _pack_w_in(p['w_in']))

    q, k, v = _mla_prep(proj, p, n_ctx, T)
    att = _attention(q, k, v)

    r, vv, kk, lw, bk, kd, bonus, gg = _rwkv_prep(proj, p, n_ctx)
    y_f, y_b = _wkv_scan(r, vv, kk, lw, bk, kd, n_ctx)

    x1 = _merge(att, y_f, y_b, bonus, gg, proj, x, g_a, p, n_ctx)
    return _moe(x1, sh_m, sc_m, g_m, p)


SUBLANES = 8


def _ada_kernel(c_ref, w_ref, b_ref, o_ref):
    cc = c_ref[...]
    w = w_ref[...]
    w_hi = w.astype(BF16)
    w_lo = (w - w_hi.astype(F32)).astype(BF16)
    o_ref[...] = _dot3(cc * jax.nn.sigmoid(cc), w_hi, w_lo) + b_ref[...]


def _ada_modulation(c, c_ctx, ada_w, ada_b):
    B, D = c.shape
    n_out = ada_w.shape[1]
    rows = -(-(B + 1) // SUBLANES) * SUBLANES
    c_all = jnp.concatenate([c, c_ctx[None, :], jnp.zeros((rows - B - 1, D), F32)], axis=0)
    out = pl.pallas_call(
        _ada_kernel,
        grid=(n_out // D,),
        in_specs=[pl.BlockSpec((rows, D), lambda j: (0, 0)),
                  pl.BlockSpec((D, D), lambda j: (0, j)),
                  pl.BlockSpec((1, D), lambda j: (0, j))],
        out_specs=pl.BlockSpec((rows, D), lambda j: (0, j)),
        out_shape=jax.ShapeDtypeStruct((rows, n_out), F32),
        compiler_params=pltpu.CompilerParams(dimension_semantics=("arbitrary",), vmem_limit_bytes=VMEM_LIMIT),
        name="ada_modulation",
    )(c_all, ada_w, ada_b.reshape(1, n_out))
    return out[:B + 1]


ROW_TILE = 256
PROJ_RW = 2048
PROJ_GATE = 2048
PROJ_MLA = 512
PROJ_W = PROJ_RW + PROJ_GATE + PROJ_MLA
MLA_IN = Q_LORA + KV_LORA + QK_ROPE
RW_SPLITS = (RW_DIM, RW_DIM, RW_DIM, DECAY_LORA, DECAY_LORA, AICL_LORA, AICL_LORA, GATE_LORA)
RW_IN = sum(RW_SPLITS)


def _pack_w_in(w_in):
    w_mla = w_in[:, :MLA_IN]
    w_rw = w_in[:, MLA_IN:MLA_IN + RW_IN]
    w_gate = w_in[:, MLA_IN + RW_IN:]
    pad = lambda w, n: jnp.pad(w, ((0, 0), (0, n - w.shape[1])))
    return jnp.concatenate([pad(w_rw, PROJ_RW), w_gate, pad(w_mla, PROJ_MLA)], axis=1).astype(BF16)


def _in_proj_kernel(ctx_ref, x_ref, sh_ref, sc_ref, gain_ref, w_ref, o_ref):
    is_ctx = pl.program_id(1) == 0
    xt = jnp.where(is_ctx, ctx_ref[0], x_ref[0])
    sh = jnp.where(is_ctx, sh_ref[0, 0:1], sh_ref[0, 1:2])
    sc = jnp.where(is_ctx, sc_ref[0, 0:1], sc_ref[0, 1:2])
    y = xt * lax.rsqrt(jnp.mean(xt * xt, axis=-1, keepdims=True) + EPS) * gain_ref[...]
    h = y * (1.0 + sc) + sh
    o_ref[0] = _dot(h, w_ref[...])


def _in_proj(ctx, x, sh2, sc2, gain, w):
    B, T, D = x.shape
    n_ctx = ctx.shape[1]
    assert n_ctx == ROW_TILE and T % ROW_TILE == 0
    nt = 1 + T // ROW_TILE
    return pl.pallas_call(
        _in_proj_kernel,
        grid=(B, nt),
        in_specs=[
            pl.BlockSpec((1, ROW_TILE, D), lambda b, i: (b, 0, 0)),
            pl.BlockSpec((1, ROW_TILE, D), lambda b, i: (b, jnp.maximum(i - 1, 0), 0)),
            pl.BlockSpec((1, 2, D), lambda b, i: (b, 0, 0)),
            pl.BlockSpec((1, 2, D), lambda b, i: (b, 0, 0)),
            pl.BlockSpec((1, D), lambda b, i: (0, 0)),
            pl.BlockSpec((D, PROJ_W), lambda b, i: (0, 0)),
        ],
        out_specs=pl.BlockSpec((1, ROW_TILE, PROJ_W), lambda b, i: (b, i, 0)),
        out_shape=jax.ShapeDtypeStruct((B, n_ctx + T, PROJ_W), F32),
        compiler_params=pltpu.CompilerParams(
            dimension_semantics=("arbitrary", "arbitrary"), vmem_limit_bytes=VMEM_LIMIT),
        name="in_proj",
    )(ctx, x, sh2, sc2, gain.reshape(1, D), w)


def _rms(x, gain):
    return x * lax.rsqrt(jnp.mean(x * x, axis=-1, keepdims=True) + EPS) * gain


def _rope_tables(n_tokens):
    rows = n_tokens // GRID_W
    row = jnp.repeat(jnp.arange(rows, dtype=F32), GRID_W)
    col = jnp.tile(jnp.arange(GRID_W, dtype=F32), rows)
    n_freq = QK_ROPE // 4
    inv_freq = ROPE_BASE ** (-jnp.arange(n_freq, dtype=F32) / n_freq)
    ang = jnp.concatenate([row[:, None] * inv_freq, col[:, None] * inv_freq], axis=-1)
    return jnp.cos(ang), jnp.sin(ang)


ROPE_HALF = QK_ROPE // 2
X1 = slice(QK_NOPE, QK_NOPE + ROPE_HALF)
X2 = slice(QK_NOPE + ROPE_HALF, QK_HEAD)


def _rot_cols(w):
    w3 = w.reshape(w.shape[0], MLA_HEADS, HEAD_PAD)
    rot = jnp.zeros_like(w3).at[:, :, X1].set(-w3[:, :, X2]).at[:, :, X2].set(w3[:, :, X1])
    return rot.reshape(w.shape)


def _swap_halves(g):
    return jnp.zeros_like(g).at[:, X1].set(g[:, X2]).at[:, X2].set(g[:, X1])


def _mla_weights(p):
    H = MLA_HEADS
    wq = jnp.pad(p['w_uq'].reshape(Q_LORA, H, QK_HEAD), ((0, 0), (0, 0), (0, HEAD_PAD - QK_HEAD)))
    wq = wq.reshape(Q_LORA, H * HEAD_PAD)
    wkv = p['w_ukv'].reshape(KV_LORA, H, QK_NOPE + V_HEAD)
    wk_lat = jnp.pad(wkv[:, :, :QK_NOPE], ((0, 0), (0, 0), (0, HEAD_PAD - QK_NOPE)))
    place = jnp.zeros((LANES, H, HEAD_PAD), F32).at[:QK_ROPE, :, QK_NOPE:QK_HEAD].set(
        jnp.broadcast_to(jnp.eye(QK_ROPE, dtype=F32)[:, None, :], (QK_ROPE, H, QK_ROPE)))
    wk = jnp.concatenate([wk_lat, place], axis=0).reshape(KV_LORA + LANES, H * HEAD_PAD)
    wv = wkv[:, :, QK_NOPE:].reshape(KV_LORA, H * V_HEAD)
    gq = jnp.pad(p['q_norm'], (0, HEAD_PAD - QK_HEAD)).reshape(1, HEAD_PAD)
    gk = jnp.pad(p['k_norm'], (0, HEAD_PAD - QK_HEAD)).reshape(1, HEAD_PAD)
    bf = lambda w: w.astype(BF16)
    return (bf(wq), bf(_rot_cols(wq)), bf(wk), bf(_rot_cols(wk)), bf(wv), gq, _swap_halves(gq), gk, _swap_halves(gk))


def _mla_tables(n_ctx, T):
    cos, sin = _rope_tables(T)
    c = jnp.ones((n_ctx + T, HEAD_PAD), F32).at[n_ctx:, X1].set(cos).at[n_ctx:, X2].set(cos)
    s = jnp.zeros((n_ctx + T, HEAD_PAD), F32).at[n_ctx:, X1].set(sin).at[n_ctx:, X2].set(sin)
    return c, s


def _mla_prep_kernel(x_ref, c_ref, s_ref, qlg_ref, kvg_ref, wq_ref, wqr_ref, wk_ref, wkr_ref, wv_ref,
                     gq_ref, gqp_ref, gk_ref, gkp_ref, q_ref, k_ref, v_ref):
    x = x_ref[0]
    ql = _rms(x[:, :Q_LORA], qlg_ref[...])
    kvl = _rms(x[:, Q_LORA:Q_LORA + KV_LORA], kvg_ref[...])
    k_in = jnp.concatenate([kvl, x[:, Q_LORA + KV_LORA:]], axis=1)
    cos, sin = c_ref[...], s_ref[...]

    def finish(raw, partner, g, g_swapped, scale, o_ref):
        gc, gs = g * cos, g_swapped * sin
        for h in range(MLA_HEADS):
            sl = slice(h * HEAD_PAD, (h + 1) * HEAD_PAD)
            rh = raw[:, sl]
            inv = lax.rsqrt(jnp.sum(rh * rh, axis=-1, keepdims=True) * (1.0 / QK_HEAD) + EPS) * scale
            o_ref[0, :, sl] = ((rh * gc + partner[:, sl] * gs) * inv).astype(o_ref.dtype)

    finish(_dot(ql, wq_ref[...]), _dot(ql, wqr_ref[...]), gq_ref[...], gqp_ref[...], ATTN_SCALE, q_ref)
    finish(_dot(k_in, wk_ref[...]), _dot(k_in, wkr_ref[...]), gk_ref[...], gkp_ref[...], 1.0, k_ref)
    v_ref[0] = _dot(kvl, wv_ref[...]).astype(v_ref.dtype)


def _mla_prep(proj, p, n_ctx, T):
    B, Tt, _ = proj.shape
    H = MLA_HEADS
    ctx_tiles = n_ctx // ROW_TILE
    mla_blk = (PROJ_RW + PROJ_GATE) // PROJ_MLA
    weights = _mla_weights(p)
    cos, sin = _mla_tables(n_ctx, T)
    const = lambda b, i: (0, 0)
    row = lambda b, i: (b, i, 0)
    tab = pl.BlockSpec((ROW_TILE, HEAD_PAD), lambda b, i: (i, 0))
    full = lambda a: pl.BlockSpec(a.shape, const)
    return pl.pallas_call(
        _mla_prep_kernel,
        grid=(B, Tt // ROW_TILE),
        in_specs=[pl.BlockSpec((1, ROW_TILE, PROJ_MLA), lambda b, i: (b, i, mla_blk)), tab, tab,
                  pl.BlockSpec((1, Q_LORA), const), pl.BlockSpec((1, KV_LORA), const)]
                 + [full(w) for w in weights],
        out_specs=[
            pl.BlockSpec((1, ROW_TILE, H * HEAD_PAD), lambda b, i: (b, jnp.maximum(i - ctx_tiles, 0), 0)),
            pl.BlockSpec((1, ROW_TILE, H * HEAD_PAD), row),
            pl.BlockSpec((1, ROW_TILE, H * V_HEAD), row),
        ],
        out_shape=[
            jax.ShapeDtypeStruct((B, T, H * HEAD_PAD), BF16),
            jax.ShapeDtypeStruct((B, Tt, H * HEAD_PAD), BF16),
            jax.ShapeDtypeStruct((B, Tt, H * V_HEAD), BF16),
        ],
        compiler_params=pltpu.CompilerParams(
            dimension_semantics=("arbitrary", "arbitrary"), vmem_limit_bytes=VMEM_LIMIT),
        name="mla_prep",
    )(proj, cos, sin, p['q_lat_norm'].reshape(1, Q_LORA), p['kv_lat_norm'].reshape(1, KV_LORA), *weights)


ATTN_Q_TILE = 1024
HEADS_PER_STEP = LANES // V_HEAD


ATTN_ROW_SPLIT = 8


def _attn_kernel(q_ref, k_ref, v_ref, o_ref):
    v2 = v_ref[0]
    rows = q_ref.shape[1] // ATTN_ROW_SPLIT
    work = [(hh, rs) for rs in range(ATTN_ROW_SPLIT) for hh in range(HEADS_PER_STEP)]

    def scores(hh, rs):
        sl = slice(hh * HEAD_PAD, (hh + 1) * HEAD_PAD)
        return _dot_t(q_ref[0, rs * rows:(rs + 1) * rows, sl], k_ref[0, :, sl])

    outs = {}
    s_next = scores(*work[0])
    for i, (hh, rs) in enumerate(work):
        s = s_next
        if i + 1 < len(work):
            s_next = scores(*work[i + 1])
        e = jnp.exp(s - jnp.max(s, axis=-1, keepdims=True))
        outs[hh, rs] = _dot(e, v2) / jnp.sum(e, axis=-1, keepdims=True)
    lane = lax.broadcasted_iota(jnp.int32, (rows, LANES), 1)
    for rs in range(ATTN_ROW_SPLIT):
        o_ref[0, rs * rows:(rs + 1) * rows] = jnp.where(lane < V_HEAD, outs[0, rs], outs[1, rs])


def _attention(q, k, v):
    B, T, _ = q.shape
    Kt = k.shape[1]
    assert T % ATTN_Q_TILE == 0 and ATTN_Q_TILE % (ATTN_ROW_SPLIT * SUBLANES) == 0
    hp = MLA_HEADS // HEADS_PER_STEP
    qw = HEADS_PER_STEP * HEAD_PAD
    return pl.pallas_call(
        _attn_kernel,
        grid=(B, hp, T // ATTN_Q_TILE),
        in_specs=[
            pl.BlockSpec((1, ATTN_Q_TILE, qw), lambda b, h, i: (b, i, h)),
            pl.BlockSpec((1, Kt, qw), lambda b, h, i: (b, 0, h)),
            pl.BlockSpec((1, Kt, LANES), lambda b, h, i: (b, 0, h)),
        ],
        out_specs=pl.BlockSpec((1, ATTN_Q_TILE, LANES), lambda b, h, i: (b, i, h)),
        out_shape=jax.ShapeDtypeStruct((B, T, MLA_HEADS * V_HEAD), F32),
        compiler_params=pltpu.CompilerParams(
            dimension_semantics=("arbitrary", "arbitrary", "arbitrary"), vmem_limit_bytes=VMEM_LIMIT),
        name="attention",
    )(q, k, v)


HALO = 8
LORA_W = LANES
GATE_W = PROJ_RW - 3 * RW_DIM - 2 * LORA_W


def _head_ones(width, head):
    i = np.arange(width) // head
    return jnp.asarray(i[:, None] == i[None, :], BF16)


def _head_sum(x, ones_bd):
    hi = x.astype(BF16)
    lo = (x - hi.astype(F32)).astype(BF16)
    return (jnp.dot(hi, ones_bd, preferred_element_type=F32)
            + jnp.dot(lo, ones_bd, preferred_element_type=F32))


def _dot3(a, b_hi, b_lo):
    hi = a.astype(BF16)
    lo = (a - hi.astype(F32)).astype(BF16)
    d = lambda u, w: jnp.dot(u, w, preferred_element_type=F32)
    return d(hi, b_hi) + (d(hi, b_lo) + d(lo, b_hi))


def _rwkv_prep_kernel(x_ref, prev_ref, next_ref, conv_ref, kkg_ref, ka_ref, rk_ref, w0_ref, a0_ref,
                      w2h_ref, w2l_ref, a2_ref, g2_ref, ones_ref,
                      r_ref, v_ref, kk_ref, lw_ref, bk_ref, kd_ref, bonus_ref, gg_ref, *, ctx_tiles, n_tiles):
    i = pl.program_id(1)
    x = x_ref[0]
    tm, W = x.shape
    C = RW_DIM
    first = (i == 0) | (i == ctx_tiles)
    last = (i == ctx_tiles - 1) | (i == n_tiles - 1)
    prev_row = jnp.where(first, 0.0, prev_ref[0, HALO - 1:HALO])
    next_row = jnp.where(last, 0.0, next_ref[0, 0:1])
    row = lax.broadcasted_iota(jnp.int32, (tm, W), 0)
    x_dn = jnp.where(row == 0, prev_row, pltpu.roll(x, 1, 0))
    x_up = jnp.where(row == tm - 1, next_row, pltpu.roll(x, tm - 1, 0))
    xc = x_dn * conv_ref[0:1] + x * conv_ref[1:2] + x_up * conv_ref[2:3]

    r, k, v = xc[:, :C], xc[:, C:2 * C], xc[:, 2 * C:3 * C]
    lora_w = jnp.tanh(xc[:, 3 * C:3 * C + LORA_W])
    lora_a = xc[:, 3 * C + LORA_W:3 * C + 2 * LORA_W]
    lg = xc[:, 3 * C + 2 * LORA_W:]
    ones_bd = ones_ref[...]
    kq = k * kkg_ref[...]
    kk = kq * lax.rsqrt(_head_sum(kq * kq, ones_bd) + 1e-12)
    r_ref[0], v_ref[0], kk_ref[0] = r, v, kk

    k_sum = jnp.zeros_like(k)
    for d in range(2):
        z = w0_ref[d:d + 1] + _dot3(lora_w, w2h_ref[d], w2l_ref[d])
        softplus_neg = jnp.maximum(-z, 0.0) + jnp.log(1.0 + jnp.exp(-jnp.abs(z)))
        lw_ref[d, 0] = -jnp.exp(-softplus_neg - 0.5)
        a = jax.nn.sigmoid(a0_ref[d:d + 1] + _dot(lora_a, a2_ref[d]))
        kd = k * (1.0 + (a - 1.0) * ka_ref[...])
        bk_ref[d, 0] = kk * a
        kd_ref[d, 0] = kd
        k_sum = k_sum + kd
    bonus_ref[0] = _head_sum(r * k_sum * rk_ref[...], ones_bd) * v
    gg_ref[0] = _dot(jax.nn.sigmoid(lg), g2_ref[...])


def _rwkv_prep(proj, p, n_ctx):
    B, Tt, _ = proj.shape
    C = RW_DIM
    nt = Tt // ROW_TILE
    hb = ROW_TILE // HALO
    pad_cols = lambda w: jnp.pad(w, ((0, 0), (0, PROJ_RW - w.shape[1])))
    w2 = jnp.stack([jnp.pad(p['decay_w2'][0], ((0, LORA_W - DECAY_LORA), (0, 0))),
                    jnp.pad(p['decay_w2'][1], ((DECAY_LORA, 0), (0, 0)))])
    w2h = w2.astype(BF16)
    w2l = (w2 - w2h.astype(F32)).astype(BF16)
    a2 = jnp.stack([jnp.pad(p['aicl_a2'][0], ((0, LORA_W - AICL_LORA), (0, 0))),
                    jnp.pad(p['aicl_a2'][1], ((AICL_LORA, 0), (0, 0)))]).astype(BF16)
    g2 = jnp.pad(p['gate_g2'], ((0, GATE_W - GATE_LORA), (0, 0))).astype(BF16)
    row = lambda b, i: (b, i, 0)
    drow = lambda b, i: (0, b, i, 0)
    const2 = lambda b, i: (0, 0)
    const3 = lambda b, i: (0, 0, 0)
    vec = pl.BlockSpec((1, C), const2)
    out_row = pl.BlockSpec((1, ROW_TILE, C), row)
    out_dir = pl.BlockSpec((2, 1, ROW_TILE, C), drow)
    sds = jax.ShapeDtypeStruct((B, Tt, C), F32)
    sds2 = jax.ShapeDtypeStruct((2, B, Tt, C), F32)
    return pl.pallas_call(
        functools.partial(_rwkv_prep_kernel, ctx_tiles=n_ctx // ROW_TILE, n_tiles=nt),
        grid=(B, nt),
        in_specs=[
            pl.BlockSpec((1, ROW_TILE, PROJ_RW), row),
            pl.BlockSpec((1, HALO, PROJ_RW), lambda b, i: (b, jnp.maximum(i * hb - 1, 0), 0)),
            pl.BlockSpec((1, HALO, PROJ_RW), lambda b, i: (b, jnp.minimum((i + 1) * hb, nt * hb - 1), 0)),
            pl.BlockSpec((3, PROJ_RW), const2),
            vec, vec, vec,
            pl.BlockSpec((2, C), const2),
            pl.BlockSpec((2, C), const2),
            pl.BlockSpec((2, LORA_W, C), const3),
            pl.BlockSpec((2, LORA_W, C), const3),
            pl.BlockSpec((2, LORA_W, C), const3),
            pl.BlockSpec((GATE_W, C), const2),
            pl.BlockSpec((C, C), const2),
        ],
        out_specs=[out_row, out_row, out_row, out_dir, out_dir, out_dir, out_row, out_row],
        out_shape=[sds, sds, sds, sds2, sds2, sds2, sds, sds],
        compiler_params=pltpu.CompilerParams(
            dimension_semantics=("arbitrary", "arbitrary"), vmem_limit_bytes=VMEM_LIMIT),
        name="rwkv_prep",
    )(proj, proj, proj, pad_cols(p['shift_conv']), p['k_k'].reshape(1, C), p['k_a'].reshape(1, C),
      p['r_k'].reshape(1, C), p['decay_w0'], p['aicl_a0'], w2h, w2l, a2, g2, _head_ones(C, RW_HEAD))


def _merge_kernel(att_ref, yf_ref, yb_ref, bonus_ref, gg_ref, gate_ref, x_ref, ga_ref, gnw_ref, gnb_ref, ones_ref,
                  wm_ref, wr_ref, wo_ref, o_ref):
    D = x_ref.shape[-1]
    ones_bd = ones_ref[...]
    y = yf_ref[0] + yb_ref[0]
    yc = y - _head_sum(y, ones_bd) * (1.0 / RW_HEAD)
    var = _head_sum(yc * yc, ones_bd) * (1.0 / RW_HEAD)
    y_n = yc * lax.rsqrt(var + GN_EPS) * gnw_ref[...] + gnb_ref[...]
    rw = (y_n + bonus_ref[0]) * gg_ref[0]
    g = jax.nn.sigmoid(gate_ref[0])
    mix = g[:, :D] * _dot(att_ref[0], wm_ref[...]) + g[:, D:] * _dot(rw, wr_ref[...])
    o_ref[0] = x_ref[0] + ga_ref[0] * _dot(mix, wo_ref[...])


def _merge(att, y_f, y_b, bonus, gg, proj, x, g_a, p, n_ctx):
    B, T, D = x.shape
    C = RW_DIM
    ctx_tiles = n_ctx // ROW_TILE
    gate_blk = PROJ_RW // PROJ_GATE
    row = lambda b, i: (b, i, 0)
    lat_row = lambda b, i: (b, i + ctx_tiles, 0)
    const = lambda b, i: (0, 0)
    return pl.pallas_call(
        _merge_kernel,
        grid=(B, T // ROW_TILE),
        in_specs=[
            pl.BlockSpec((1, ROW_TILE, att.shape[-1]), row),
            pl.BlockSpec((1, ROW_TILE, C), row),
            pl.BlockSpec((1, ROW_TILE, C), row),
            pl.BlockSpec((1, ROW_TILE, C), lat_row),
            pl.BlockSpec((1, ROW_TILE, C), lat_row),
            pl.BlockSpec((1, ROW_TILE, PROJ_GATE), lambda b, i: (b, i + ctx_tiles, gate_blk)),
            pl.BlockSpec((1, ROW_TILE, D), row),
            pl.BlockSpec((1, 1, D), lambda b, i: (b, 0, 0)),
            pl.BlockSpec((1, C), const),
            pl.BlockSpec((1, C), const),
            pl.BlockSpec((C, C), const),
            pl.BlockSpec(p['w_o_mla'].shape, const),
            pl.BlockSpec(p['w_o_rwkv'].shape, const),
            pl.BlockSpec(p['w_out'].shape, const),
        ],
        out_specs=pl.BlockSpec((1, ROW_TILE, D), row),
        out_shape=jax.ShapeDtypeStruct((B, T, D), F32),
        compiler_params=pltpu.CompilerParams(
            dimension_semantics=("arbitrary", "arbitrary"), vmem_limit_bytes=VMEM_LIMIT),
        name="merge",
    )(att, y_f, y_b, bonus, gg, proj, x, g_a[:, None, :], p['gn_w'].reshape(1, C), p['gn_b'].reshape(1, C),
      _head_ones(C, RW_HEAD), p['w_o_mla'].astype(BF16), p['w_o_rwkv'].astype(BF16), p['w_out'].astype(BF16))


TILE_ROWS = 8
MOE_BLOCK = 512


NEG_INF = float("-inf")
HI_MASK = 0xFFFF0000


def _pack_pair(lo, hi):
    lo_b = lax.bitcast_convert_type(lo.astype(BF16).astype(F32), jnp.uint32)
    hi_b = lax.bitcast_convert_type(hi.astype(BF16).astype(F32), jnp.uint32)
    return (lo_b >> 16) | (hi_b & jnp.uint32(HI_MASK))


def _unpack_pair(w):
    lo = lax.bitcast_convert_type(w << 16, F32)
    hi = lax.bitcast_convert_type(w & jnp.uint32(HI_MASK), F32)
    return lo.astype(BF16), hi.astype(BF16)


def _row_max(x):
    return jnp.max(x, axis=-1, keepdims=True)


def _first_index_of(x, value, lane_f):
    return jnp.min(jnp.where(x == value, lane_f, float(x.shape[-1])), axis=-1, keepdims=True)


def _router_kernel(x_ref, gain_ref, sh_ref, sc_ref, wh_ref, wm_ref, wl_ref, bias_ref,
                   u_ref, ids_ref, gates_ref, ranks_ref, counts_ref, carry_ref):
    @pl.when((pl.program_id(0) == 0) & (pl.program_id(1) == 0))
    def _():
        carry_ref[...] = jnp.zeros_like(carry_ref)

    x = x_ref[0]
    tm, D = x.shape
    E = bias_ref.shape[-1]
    u = x * lax.rsqrt(jnp.mean(x * x, axis=-1, keepdims=True) + EPS) * gain_ref[...]
    u = u * (1.0 + sc_ref[0]) + sh_ref[0]
    u_ref[0] = _pack_pair(u[:, :D // 2], u[:, D // 2:])

    uh, um, ul = _split3(u)
    wh, wm, wl = wh_ref[...], wm_ref[...], wl_ref[...]
    d = lambda a, b: jnp.dot(a, b, preferred_element_type=F32)
    logits = d(uh, wh) + (d(uh, wm) + d(um, wh)) + (d(uh, wl) + d(um, wm) + d(ul, wh))
    scores = jax.nn.sigmoid(logits)
    sel = scores + bias_ref[...]

    lane_i = lax.broadcasted_iota(jnp.int32, (tm, E), 1)
    lane_f = lane_i.astype(F32)
    out_f = lax.broadcasted_iota(jnp.int32, (tm, LANES), 1).astype(F32)
    per_group = E // N_GROUPS
    grp_f = jnp.floor(lane_f * (1.0 / per_group))

    gs = jnp.full((tm, LANES), NEG_INF, F32)
    for g in range(N_GROUPS):
        sg = jnp.where(lane_i >= g * per_group, jnp.where(lane_i < (g + 1) * per_group, sel, NEG_INF), NEG_INF)
        m1 = _row_max(sg)
        i1 = _first_index_of(sg, m1, lane_f)
        m2 = _row_max(jnp.where(lane_f == i1, NEG_INF, sg))
        gs = jnp.where(out_f == g, m1 + m2, gs)

    allow = jnp.zeros((tm, E), F32)
    for _ in range(TOPK_GROUPS):
        m = _row_max(gs)
        i = _first_index_of(gs, m, out_f)
        gs = jnp.where(out_f == i, NEG_INF, gs)
        allow = jnp.where(grp_f == i, 1.0, allow)
    selm = jnp.where(allow > 0.0, sel, NEG_INF)

    ids = jnp.zeros((tm, LANES), F32)
    gts = jnp.zeros((tm, LANES), F32)
    member = jnp.zeros((tm, E), F32)
    idx_cols = []
    gsum = jnp.zeros((tm, 1), F32)
    for k in range(TOP_K):
        m = _row_max(selm)
        i = _first_index_of(selm, m, lane_f)
        hit = lane_f == i
        gk = jnp.sum(jnp.where(hit, scores, 0.0), axis=-1, keepdims=True)
        selm = jnp.where(hit, NEG_INF, selm)
        member = jnp.where(hit, 1.0, member)
        ids = jnp.where(out_f == k, i, ids)
        gts = jnp.where(out_f == k, gk, gts)
        idx_cols.append(i)
        gsum = gsum + gk
    gts = gts / gsum * ROUTED_SCALE

    r2 = lax.broadcasted_iota(jnp.int32, (tm, tm), 0)
    c2 = lax.broadcasted_iota(jnp.int32, (tm, tm), 1)
    before = jnp.where(r2 > c2, 1.0, 0.0).astype(BF16)
    mem_b = member.astype(BF16)
    carry = carry_ref[...]
    pos = carry + jnp.dot(before, mem_b, preferred_element_type=F32)
    rk = jnp.zeros((tm, LANES), F32)
    for k in range(TOP_K):
        rk = jnp.where(out_f == k, jnp.sum(jnp.where(lane_f == idx_cols[k], pos, 0.0), axis=-1, keepdims=True), rk)
    colsum = jnp.dot(jnp.ones((8, tm), BF16), mem_b, preferred_element_type=F32)[0:1]
    carry_ref[...] = carry + colsum
    counts_ref[...] = carry + colsum
    ids_ref[0] = ids.astype(jnp.int32)
    ranks_ref[0] = rk.astype(jnp.int32)
    gates_ref[0] = gts


def _router(x1, sh_m, sc_m, p):
    B, T, D = x1.shape
    E = N_EXPERTS
    wh, wm, wl = _split3(p['router_w'])
    row = lambda b, i: (b, i, 0)
    const = lambda b, i: (0, 0)
    vec = lambda b, i: (b, 0, 0)
    lane_out = lambda dt: jax.ShapeDtypeStruct((B, T, LANES), dt)
    return pl.pallas_call(
        _router_kernel,
        grid=(B, T // ROW_TILE),
        in_specs=[
            pl.BlockSpec((1, ROW_TILE, D), row),
            pl.BlockSpec((1, D), const),
            pl.BlockSpec((1, 1, D), vec),
            pl.BlockSpec((1, 1, D), vec),
            pl.BlockSpec((D, E), const),
            pl.BlockSpec((D, E), const),
            pl.BlockSpec((D, E), const),
            pl.BlockSpec((1, E), const),
        ],
        out_specs=[
            pl.BlockSpec((1, ROW_TILE, D // 2), row),
            pl.BlockSpec((1, ROW_TILE, LANES), row),
            pl.BlockSpec((1, ROW_TILE, LANES), row),
            pl.BlockSpec((1, ROW_TILE, LANES), row),
            pl.BlockSpec((1, E), const),
        ],
        out_shape=[
            jax.ShapeDtypeStruct((B, T, D // 2), jnp.uint32),
            lane_out(jnp.int32), lane_out(F32), lane_out(jnp.int32),
            jax.ShapeDtypeStruct((1, E), F32),
        ],
        scratch_shapes=[pltpu.VMEM((1, E), F32)],
        compiler_params=pltpu.CompilerParams(
            dimension_semantics=("arbitrary", "arbitrary"), vmem_limit_bytes=VMEM_LIMIT),
        name="router",
    )(x1, p['norm_ffn'].reshape(1, D), sh_m[:, None, :], sc_m[:, None, :], wh, wm, wl,
      p['router_bias'].reshape(1, E))


def _slot_kernel(ids_ref, ranks_ref, base_ref, o_ref):
    ids = ids_ref[...].astype(F32)
    tm = ids.shape[0]
    E = base_ref.shape[-1]
    lane_e = lax.broadcasted_iota(jnp.int32, (tm, E), 1).astype(F32)
    out_lane = lax.broadcasted_iota(jnp.int32, (tm, LANES), 1)
    first = jnp.zeros((tm, LANES), F32)
    for k in range(TOP_K):
        fk = jnp.sum(jnp.where(lane_e == ids[:, k:k + 1], base_ref[...], 0.0), axis=-1, keepdims=True)
        first = jnp.where(out_lane == k, fk, first)
    o_ref[...] = first.astype(jnp.int32) + ranks_ref[...]


def _slots(ids, ranks, base):
    n_tok = ids.shape[0]
    E = base.shape[-1]
    row = pl.BlockSpec((ROW_TILE, LANES), lambda i: (i, 0))
    return pl.pallas_call(
        _slot_kernel,
        grid=(n_tok // ROW_TILE,),
        in_specs=[row, row, pl.BlockSpec((1, E), lambda i: (0, 0))],
        out_specs=row,
        out_shape=jax.ShapeDtypeStruct((n_tok, LANES), jnp.int32),
        compiler_params=pltpu.CompilerParams(dimension_semantics=("arbitrary",)),
        name="slots",
    )(ids, ranks, base)


DMA_PRIORITIES = 2


def _row_copy(src, src_row, dst, dst_row, sem):
    return pltpu.make_async_copy(src.at[pl.ds(src_row, 1)], dst.at[pl.ds(dst_row, 1)], sem)


def _dispatch_kernel(dest_ref, u_ref, slots_in_ref, slots_ref, sem):
    del slots_in_ref
    tm = u_ref.shape[0]

    def issue(r, carry):
        for k in range(TOP_K):
            _row_copy(u_ref, r, slots_ref, dest_ref[0, 0, r * TOP_K + k], sem).start(priority=k % DMA_PRIORITIES)
        return carry

    lax.fori_loop(0, tm, issue, 0)
    for k in range(TOP_K):
        pltpu.make_async_copy(u_ref, slots_ref.at[pl.ds(0, tm)], sem).wait()


def _dispatch(dest, u_rows, n_slots):
    n_tok, W = u_rows.shape
    nt = n_tok // ROW_TILE
    return pl.pallas_call(
        _dispatch_kernel,
        grid=(nt,),
        in_specs=[
            pl.BlockSpec((1, 1, ROW_TILE * TOP_K), lambda i: (i, 0, 0), memory_space=pltpu.SMEM),
            pl.BlockSpec((ROW_TILE, W), lambda i: (i, 0)),
            pl.BlockSpec(memory_space=pl.ANY),
        ],
        out_specs=pl.BlockSpec(memory_space=pl.ANY),
        out_shape=jax.ShapeDtypeStruct((n_slots, W), jnp.uint32),
        scratch_shapes=[pltpu.SemaphoreType.DMA(())],
        input_output_aliases={2: 0},
        compiler_params=pltpu.CompilerParams(
            dimension_semantics=("arbitrary",), vmem_limit_bytes=VMEM_LIMIT),
        name="dispatch",
    )(dest, u_rows, jnp.zeros((n_slots, W), jnp.uint32))


def _moe_ffn_kernel(be_ref, nu_ref, x_ref, w1_ref, w3_ref, w2_ref, o_ref, w13_ref, w2b_ref):
    j = pl.program_id(0)
    F = w1_ref.shape[-1]
    half = x_ref.shape[-1]

    @pl.when((j == 0) | (be_ref[j] != be_ref[jnp.maximum(j - 1, 0)]))
    def _():
        w13_ref[:, :F] = w1_ref[0].astype(BF16)
        w13_ref[:, F:] = w3_ref[0].astype(BF16)
        w2b_ref[...] = w2_ref[0].astype(BF16)

    @pl.when(j < nu_ref[0])
    def _():
        lo, hi = _unpack_pair(x_ref[...])
        h = (jnp.dot(lo, w13_ref[:half], preferred_element_type=F32)
             + jnp.dot(hi, w13_ref[half:], preferred_element_type=F32))
        h1, h3 = h[:, :F], h[:, F:]
        y = _dot(h1 * jax.nn.sigmoid(h1) * h3, w2b_ref[...])
        for c in range(TILE_ROWS):
            o_ref[pl.ds(c, MOE_BLOCK, stride=TILE_ROWS), :] = y[:, c * LANES:(c + 1) * LANES]

    @pl.when(j >= nu_ref[0])
    def _():
        o_ref[...] = jnp.zeros_like(o_ref)


def _moe_ffn(block_e, n_used, x_slots, w1, w3, w2):
    slots, W = x_slots.shape
    E, D, F = w1.shape
    nblk = slots // MOE_BLOCK
    grid_spec = pltpu.PrefetchScalarGridSpec(
        num_scalar_prefetch=2,
        grid=(nblk,),
        in_specs=[
            pl.BlockSpec((MOE_BLOCK, W), lambda j, be, nu: (jnp.minimum(j, nu[0] - 1), 0)),
            pl.BlockSpec((1, D, F), lambda j, be, nu: (be[j], 0, 0)),
            pl.BlockSpec((1, D, F), lambda j, be, nu: (be[j], 0, 0)),
            pl.BlockSpec((1, F, D), lambda j, be, nu: (be[j], 0, 0)),
        ],
        out_specs=pl.BlockSpec((MOE_BLOCK * TILE_ROWS, LANES), lambda j, be, nu: (j, 0)),
        scratch_shapes=[pltpu.VMEM((D, 2 * F), BF16), pltpu.VMEM((F, D), BF16)],
    )
    return pl.pallas_call(
        _moe_ffn_kernel,
        grid_spec=grid_spec,
        out_shape=jax.ShapeDtypeStruct((slots * TILE_ROWS, LANES), F32),
        compiler_params=pltpu.CompilerParams(
            dimension_semantics=("arbitrary",), vmem_limit_bytes=VMEM_LIMIT),
        name="moe_ffn",
    )(block_e, n_used, x_slots, w1, w3, w2)


COMBINE_ROWS = 32


def _combine_kernel(dest_ref, gates_ref, u_ref, x_ref, gm_ref, w1_ref, w3_ref, w2_ref, ys_ref, o_ref, buf_ref, sem):
    tm, half = u_ref.shape

    def tile(i):
        return pl.ds(pl.multiple_of(i * TILE_ROWS, TILE_ROWS), TILE_ROWS)

    def issue(r, carry):
        for k in range(TOP_K):
            pltpu.make_async_copy(ys_ref.at[tile(dest_ref[0, 0, r * TOP_K + k])], buf_ref.at[k, tile(r)], sem).start(
                priority=k % DMA_PRIORITIES)
        return carry

    lax.fori_loop(0, tm, issue, 0)

    ulo, uhi = _unpack_pair(u_ref[...])
    both = lambda w_ref: (jnp.dot(ulo, w_ref[:half], preferred_element_type=F32)
                          + jnp.dot(uhi, w_ref[half:], preferred_element_type=F32))
    h1, h3 = both(w1_ref), both(w3_ref)
    o_ref[...] = x_ref[...] + gm_ref[0] * _dot(h1 * jax.nn.sigmoid(h1) * h3, w2_ref[...])

    for k in range(TOP_K):
        pltpu.make_async_copy(ys_ref.at[pl.ds(0, tm * TILE_ROWS)], buf_ref.at[k], sem).wait()

    gm = gm_ref[0]
    for rb in range(tm // COMBINE_ROWS):
        rows = slice(rb * COMBINE_ROWS, (rb + 1) * COMBINE_ROWS)
        gates = gates_ref[rows, :]
        for c in range(TILE_ROWS):
            acc = jnp.zeros((COMBINE_ROWS, LANES), F32)
            for k in range(TOP_K):
                acc = acc + gates[:, k:k + 1] * buf_ref[
                    k, pl.ds(rb * COMBINE_ROWS * TILE_ROWS + c, COMBINE_ROWS, stride=TILE_ROWS), :]
            cols = slice(c * LANES, (c + 1) * LANES)
            o_ref[rows, cols] = o_ref[rows, cols] + gm[:, cols] * acc


def _combine(dest, gates, u_rows, x1, g_m, y_slots, p):
    B, T, D = x1.shape
    n_tok = B * T
    W = u_rows.shape[1]
    tiles_per_batch = T // ROW_TILE
    row = lambda i: (i, 0)
    const = lambda i: (0, 0)
    out = pl.pallas_call(
        _combine_kernel,
        grid=(n_tok // ROW_TILE,),
        in_specs=[
            pl.BlockSpec((1, 1, ROW_TILE * TOP_K), lambda i: (i, 0, 0), memory_space=pltpu.SMEM),
            pl.BlockSpec((ROW_TILE, LANES), row),
            pl.BlockSpec((ROW_TILE, W), row),
            pl.BlockSpec((ROW_TILE, D), row),
            pl.BlockSpec((1, 1, D), lambda i: (i // tiles_per_batch, 0, 0)),
            pl.BlockSpec(p['shared_w1'].shape, const),
            pl.BlockSpec(p['shared_w3'].shape, const),
            pl.BlockSpec(p['shared_w2'].shape, const),
            pl.BlockSpec(memory_space=pl.ANY),
        ],
        out_specs=pl.BlockSpec((ROW_TILE, D), row),
        out_shape=jax.ShapeDtypeStruct((n_tok, D), F32),
        scratch_shapes=[pltpu.VMEM((TOP_K, ROW_TILE * TILE_ROWS, LANES), F32), pltpu.SemaphoreType.DMA(())],
        compiler_params=pltpu.CompilerParams(
            dimension_semantics=("arbitrary",), vmem_limit_bytes=VMEM_LIMIT),
        name="combine",
    )(dest, gates.reshape(n_tok, LANES), u_rows, x1.reshape(n_tok, D), g_m[:, None, :],
      p['shared_w1'].astype(BF16), p['shared_w3'].astype(BF16), p['shared_w2'].astype(BF16), y_slots)
    return out.reshape(B, T, D)


def _moe(x1, sh_m, sc_m, g_m, p):
    B, T, D = x1.shape
    n_tok = B * T
    u_packed, ids, gates, ranks, counts = _router(x1, sh_m, sc_m, p)

    counts = counts[0].astype(jnp.int32)
    padded = (counts + MOE_BLOCK - 1) // MOE_BLOCK * MOE_BLOCK
    padded_end = jnp.cumsum(padded)
    base = padded_end - padded
    n_blocks = n_tok * TOP_K // MOE_BLOCK + N_EXPERTS
    n_used = (padded_end[-1] // MOE_BLOCK).astype(jnp.int32)
    blk = jnp.minimum(jnp.arange(n_blocks), n_used - 1) * MOE_BLOCK
    block_e = jnp.minimum(jnp.sum(padded_end[None, :] <= blk[:, None], axis=1), N_EXPERTS - 1).astype(jnp.int32)
    dest = _slots(ids.reshape(n_tok, LANES), ranks.reshape(n_tok, LANES), base.astype(F32).reshape(1, N_EXPERTS))
    dest = dest[:, :TOP_K].reshape(n_tok // ROW_TILE, 1, ROW_TILE * TOP_K)

    u_rows = u_packed.reshape(n_tok, D // 2)
    x_slots = _dispatch(dest, u_rows, n_blocks * MOE_BLOCK)
    y_slots = _moe_ffn(block_e, n_used.reshape(1), x_slots, p['expert_w1'], p['expert_w3'], p['expert_w2'])
    return _combine(dest, gates, u_rows, x1, g_m, y_slots, p)
```

```python
import functools

import jax
import jax.numpy as jnp
import numpy as np
from jax import lax
from jax.experimental import pallas as pl
from jax.experimental.pallas import tpu as pltpu

F32 = jnp.float32
BF16 = jnp.bfloat16

GRID_W = 64
EPS = 1e-6
MLA_HEADS = 8
QK_NOPE = 64
QK_ROPE = 32
QK_HEAD = QK_NOPE + QK_ROPE
V_HEAD = 64
Q_LORA = 256
KV_LORA = 128
ROPE_BASE = 10000.0
ATTN_SCALE = QK_HEAD ** -0.5
RW_HEADS = 8
RW_HEAD = 64
RW_DIM = RW_HEADS * RW_HEAD
DECAY_LORA = 64
AICL_LORA = 64
GATE_LORA = 160
GN_EPS = 64e-5
N_EXPERTS = 256
TOP_K = 8
N_GROUPS = 8
TOPK_GROUPS = 4
ROUTED_SCALE = 2.5

LANES = 128
MXU_DIM = 256
VMEM_LIMIT = 56 * 1024 * 1024

WKV_CHUNK = 64
HEADS_PER_GROUP = MXU_DIM // RW_HEAD
HEAD_PAD = LANES


def _dot(a, b):
    return jnp.dot(a.astype(BF16), b.astype(BF16), preferred_element_type=F32)


def _dot_t(a, b):
    return lax.dot_general(a.astype(BF16), b.astype(BF16), (((1,), (1,)), ((), ())),
                           preferred_element_type=F32)


def _split3(x):
    h = x.astype(BF16)
    r1 = x - h.astype(F32)
    m = r1.astype(BF16)
    lo = (r1 - m.astype(F32)).astype(BF16)
    return h, m, lo


def _dot_hi(a_bf16_exact, x):
    h, m, lo = _split3(x)
    d = lambda y: jnp.dot(a_bf16_exact, y, preferred_element_type=F32)
    return d(h) + d(m) + d(lo)


def _wkv_chunk(r, v, kk, lw, bk, kd, s0, reverse, emit):
    L, G = r.shape
    row = lax.broadcasted_iota(jnp.int32, (L, G), 0)
    lane = lax.broadcasted_iota(jnp.int32, (L, G), 1)
    diff = (lane % L - row) if reverse else (row - lane % L)
    strict = diff > 0
    incl = diff >= 0
    r2 = lax.broadcasted_iota(jnp.int32, (L, L), 0)
    c2 = lax.broadcasted_iota(jnp.int32, (L, L), 1)
    tri = jnp.where(((c2 - r2) if reverse else (r2 - c2)) >= 0, 1.0, 0.0).astype(BF16)
    tri_ones = jnp.concatenate([tri, jnp.ones((L, L), BF16)], axis=0)
    lane_head = lane // RW_HEAD
    rowb = lax.broadcasted_iota(jnp.int32, (G, G), 0) // RW_HEAD
    colb = lax.broadcasted_iota(jnp.int32, (G, G), 1) // RW_HEAD

    def stack(x):
        return jnp.concatenate(
            [jnp.where(lane_head == h, x, 0.0) for h in range(HEADS_PER_GROUP)], axis=0).astype(BF16)

    sums = _dot_hi(tri_ones, lw)
    cum_in = sums[:L]
    g_in = jnp.exp(cum_in)
    g_inv = jnp.exp(-cum_in)
    g_ex = jnp.exp(cum_in - lw)
    g_tot = jnp.exp(sums[L:])
    a_h = -kk * g_ex
    b_h = bk * g_inv
    k_h = kd * g_inv
    r_h = r * g_in
    yield

    gram = _dot_t(jnp.concatenate([a_h, r_h], axis=0),
                  jnp.concatenate([stack(b_h), stack(k_h)], axis=0))
    ab = jnp.where(strict, gram[:L, :G], 0.0)
    ak = jnp.where(strict, gram[:L, G:], 0.0)
    rb = jnp.where(incl, gram[L:, :G], 0.0)
    rk = jnp.where(incl, gram[L:, G:], 0.0)
    yield

    tm = jnp.where(diff == 0, 1.0, 0.0) + ab
    p = ab
    v_st = stack(v)
    akv = _dot(ak, v_st)
    for _ in range(int(np.log2(L)) - 1):
        p = _dot(p, stack(p))
        yield
        tm = tm + _dot(tm, stack(p))
        yield

    wu = _dot(tm, jnp.concatenate([stack(a_h), stack(akv)], axis=1))
    w_t, u_t = wu[:, :G], wu[:, G:]
    yield
    rbwu = _dot(rb, jnp.concatenate([stack(w_t), stack(u_t)], axis=1))
    r_t = r_h + rbwu[:, :G]
    y_t = rbwu[:, G:] + _dot(rk, v_st)
    yield

    s0_st = stack(s0)
    y = _dot_t(r_t, s0_st) + y_t
    u = _dot_t(w_t, s0_st) + u_t
    yield
    uv = jnp.concatenate([u, v], axis=0)
    bkc = jnp.concatenate([b_h, k_h], axis=0)
    upd = lax.dot_general(uv.astype(BF16), bkc.astype(BF16), (((0,), (0,)), ((), ())),
                          preferred_element_type=F32)
    upd = jnp.where(rowb == colb, upd, 0.0)
    upd_d = upd[0:L]
    for h in range(1, HEADS_PER_GROUP):
        upd_d = upd_d + upd[h * L:(h + 1) * L]
    emit(y, (s0 + upd_d) * g_tot)


def _wkv_kernel(*refs, n_groups, n_batch):
    ins, (yf_ref, yb_ref, s_ref) = refs[:12], refs[12:]
    G = MXU_DIM

    @pl.when(pl.program_id(1) == 0)
    def _():
        s_ref[...] = jnp.zeros_like(s_ref)

    chains = []
    for d, y_ref in enumerate((yf_ref, yb_ref)):
        r_ref, v_ref, kk_ref, lw_ref, bk_ref, kd_ref = ins[6 * d:6 * d + 6]
        for bi in range(n_batch):
            for g in range(n_groups):
                sl = slice(g * G, (g + 1) * G)

                def emit(y, s_new, y_ref=y_ref, d=d, bi=bi, g=g, sl=sl):
                    y_ref[bi, :, sl] = y
                    s_ref[d, bi, g] = s_new

                chains.append(_wkv_chunk(
                    r_ref[bi, :, sl], v_ref[bi, :, sl], kk_ref[bi, :, sl], lw_ref[0, bi, :, sl],
                    bk_ref[0, bi, :, sl], kd_ref[0, bi, :, sl], s_ref[d, bi, g], d == 1, emit))
    while chains:
        chains = [c for c in chains if next(c, StopIteration) is not StopIteration]


WKV_BATCH = 2


def _wkv_scan(r, v, kk, lw, bk, kd, n_ctx):
    B, Ttot, C = r.shape
    L = WKV_CHUNK
    nc = Ttot // L
    ncc = n_ctx // L
    nl = nc - ncc
    n_groups = C // MXU_DIM
    nb = WKV_BATCH

    cid = (lambda s: s, lambda s: jnp.where(s < ncc, ncc - 1 - s, nc + ncc - 1 - s))
    first_out = (0, nl - 1)
    in_specs, out_specs = [], []
    for d in range(2):
        shared = pl.BlockSpec((nb, L, C), lambda b, s, d=d: (b, cid[d](s), 0))
        per_dir = pl.BlockSpec((1, nb, L, C), lambda b, s, d=d: (d, b, cid[d](s), 0))
        in_specs += [shared, shared, shared, per_dir, per_dir, per_dir]
        out_specs.append(pl.BlockSpec(
            (nb, L, C), lambda b, s, d=d: (b, jnp.where(s < ncc, first_out[d], cid[d](s) - ncc), 0)))
    out = jax.ShapeDtypeStruct((B, nl * L, C), F32)
    return pl.pallas_call(
        functools.partial(_wkv_kernel, n_groups=n_groups, n_batch=nb),
        grid=(B // nb, nc),
        in_specs=in_specs,
        out_specs=out_specs,
        out_shape=[out, out],
        scratch_shapes=[pltpu.VMEM((2, nb, n_groups, L, MXU_DIM), F32)],
        compiler_params=pltpu.CompilerParams(
            dimension_semantics=("arbitrary", "arbitrary"), vmem_limit_bytes=VMEM_LIMIT),
        name="wkv_scan",
    )(r, v, kk, lw, bk, kd, r, v, kk, lw, bk, kd)


def kernel(x, c, ctx, c_ctx, ada_w, ada_b, norm_mix, norm_ffn, w_in, shift_conv, q_lat_norm, w_uq, kv_lat_norm, w_ukv, q_norm, k_norm, w_o_mla, decay_w0, decay_w2, aicl_a0, aicl_a2, k_k, k_a, r_k, gn_w, gn_b, gate_g2, w_o_rwkv, w_out, router_w, router_bias, expert_w1, expert_w3, expert_w2, shared_w1, shared_w3, shared_w2):
    B, T, D = x.shape
    n_ctx = ctx.shape[1]
    i0 = 0
    p = dict(norm_mix=norm_mix[i0], norm_ffn=norm_ffn[i0], w_in=w_in[i0], shift_conv=shift_conv[i0],
             q_lat_norm=q_lat_norm[i0], w_uq=w_uq[i0], kv_lat_norm=kv_lat_norm[i0], w_ukv=w_ukv[i0],
             q_norm=q_norm[i0], k_norm=k_norm[i0], w_o_mla=w_o_mla[i0],
             decay_w0=decay_w0[i0], decay_w2=decay_w2[i0], aicl_a0=aicl_a0[i0], aicl_a2=aicl_a2[i0],
             k_k=k_k[i0], k_a=k_a[i0], r_k=r_k[i0], gn_w=gn_w[i0], gn_b=gn_b[i0], gate_g2=gate_g2[i0],
             w_o_rwkv=w_o_rwkv[i0], w_out=w_out[i0], router_w=router_w[i0], router_bias=router_bias[i0],
             expert_w1=expert_w1[i0], expert_w3=expert_w3[i0], expert_w2=expert_w2[i0],
             shared_w1=shared_w1[i0], shared_w3=shared_w3[i0], shared_w2=shared_w2[i0])

    mod = _ada_modulation(c, c_ctx, ada_w[i0], ada_b[i0])
    sh_a, sc_a, g_a, sh_m, sc_m, g_m = jnp.split(mod[:B], 6, axis=-1)
    csh_a, csc_a = jnp.split(mod[B], 6, axis=-1)[:2]
    sh2 = jnp.stack([jnp.broadcast_to(csh_a, (B, D)), sh_a], axis=1)
    sc2 = jnp.stack([jnp.broadcast_to(csc_a, (B, D)), sc_a], axis=1)

    proj = _in_proj(ctx, x, sh2, sc2, p['norm_mix'], _pack_w_in(p['w_in']))

    q, k, v = _mla_prep(proj, p, n_ctx, T)
    att = _attention(q, k, v)

    r, vv, kk, lw, bk, kd, bonus, gg = _rwkv_prep(proj, p, n_ctx)
    y_f, y_b = _wkv_scan(r, vv, kk, lw, bk, kd, n_ctx)

    x1 = _merge(att, y_f, y_b, bonus, gg, proj, x, g_a, p, n_ctx)
    return _moe(x1, sh_m, sc_m, g_m, p)


SUBLANES = 8


def _ada_kernel(c_ref, w_ref, b_ref, o_ref):
    cc = c_ref[...]
    w = w_ref[...]
    w_hi = w.astype(BF16)
    w_lo = (w - w_hi.astype(F32)).astype(BF16)
    o_ref[...] = _dot3(cc * jax.nn.sigmoid(cc), w_hi, w_lo) + b_ref[...]


def _ada_modulation(c, c_ctx, ada_w, ada_b):
    B, D = c.shape
    n_out = ada_w.shape[1]
    rows = -(-(B + 1) // SUBLANES) * SUBLANES
    c_all = jnp.concatenate([c, c_ctx[None, :], jnp.zeros((rows - B - 1, D), F32)], axis=0)
    out = pl.pallas_call(
        _ada_kernel,
        grid=(n_out // D,),
        in_specs=[pl.BlockSpec((rows, D), lambda j: (0, 0)),
                  pl.BlockSpec((D, D), lambda j: (0, j)),
                  pl.BlockSpec((1, D), lambda j: (0, j))],
        out_specs=pl.BlockSpec((rows, D), lambda j: (0, j)),
        out_shape=jax.ShapeDtypeStruct((rows, n_out), F32),
        compiler_params=pltpu.CompilerParams(dimension_semantics=("arbitrary",), vmem_limit_bytes=VMEM_LIMIT),
        name="ada_modulation",
    )(c_all, ada_w, ada_b.reshape(1, n_out))
    return out[:B + 1]


ROW_TILE = 256
PROJ_RW = 2048
PROJ_GATE = 2048
PROJ_MLA = 512
PROJ_W = PROJ_RW + PROJ_GATE + PROJ_MLA
MLA_IN = Q_LORA + KV_LORA + QK_ROPE
RW_SPLITS = (RW_DIM, RW_DIM, RW_DIM, DECAY_LORA, DECAY_LORA, AICL_LORA, AICL_LORA, GATE_LORA)
RW_IN = sum(RW_SPLITS)


def _pack_w_in(w_in):
    w_mla = w_in[:, :MLA_IN]
    w_rw = w_in[:, MLA_IN:MLA_IN + RW_IN]
    w_gate = w_in[:, MLA_IN + RW_IN:]
    pad = lambda w, n: jnp.pad(w, ((0, 0), (0, n - w.shape[1])))
    return jnp.concatenate([pad(w_rw, PROJ_RW), w_gate, pad(w_mla, PROJ_MLA)], axis=1).astype(BF16)


def _in_proj_kernel(ctx_ref, x_ref, sh_ref, sc_ref, gain_ref, w_ref, o_ref):
    is_ctx = pl.program_id(1) == 0
    xt = jnp.where(is_ctx, ctx_ref[0], x_ref[0])
    sh = jnp.where(is_ctx, sh_ref[0, 0:1], sh_ref[0, 1:2])
    sc = jnp.where(is_ctx, sc_ref[0, 0:1], sc_ref[0, 1:2])
    y = xt * lax.rsqrt(jnp.mean(xt * xt, axis=-1, keepdims=True) + EPS) * gain_ref[...]
    h = y * (1.0 + sc) + sh
    o_ref[0] = _dot(h, w_ref[...]).astype(o_ref.dtype)


def _in_proj(ctx, x, sh2, sc2, gain, w):
    B, T, D = x.shape
    n_ctx = ctx.shape[1]
    assert n_ctx == ROW_TILE and T % ROW_TILE == 0
    nt = 1 + T // ROW_TILE
    return pl.pallas_call(
        _in_proj_kernel,
        grid=(B, nt),
        in_specs=[
            pl.BlockSpec((1, ROW_TILE, D), lambda b, i: (b, 0, 0)),
            pl.BlockSpec((1, ROW_TILE, D), lambda b, i: (b, jnp.maximum(i - 1, 0), 0)),
            pl.BlockSpec((1, 2, D), lambda b, i: (b, 0, 0)),
            pl.BlockSpec((1, 2, D), lambda b, i: (b, 0, 0)),
            pl.BlockSpec((1, D), lambda b, i: (0, 0)),
            pl.BlockSpec((D, PROJ_W), lambda b, i: (0, 0)),
        ],
        out_specs=pl.BlockSpec((1, ROW_TILE, PROJ_W), lambda b, i: (b, i, 0)),
        out_shape=jax.ShapeDtypeStruct((B, n_ctx + T, PROJ_W), BF16),
        compiler_params=pltpu.CompilerParams(
            dimension_semantics=("arbitrary", "arbitrary"), vmem_limit_bytes=VMEM_LIMIT),
        name="in_proj",
    )(ctx, x, sh2, sc2, gain.reshape(1, D), w)


def _rms(x, gain):
    return x * lax.rsqrt(jnp.mean(x * x, axis=-1, keepdims=True) + EPS) * gain


def _rope_tables(n_tokens):
    rows = n_tokens // GRID_W
    row = jnp.repeat(jnp.arange(rows, dtype=F32), GRID_W)
    col = jnp.tile(jnp.arange(GRID_W, dtype=F32), rows)
    n_freq = QK_ROPE // 4
    inv_freq = ROPE_BASE ** (-jnp.arange(n_freq, dtype=F32) / n_freq)
    ang = jnp.concatenate([row[:, None] * inv_freq, col[:, None] * inv_freq], axis=-1)
    return jnp.cos(ang), jnp.sin(ang)


ROPE_HALF = QK_ROPE // 2
X1 = slice(QK_NOPE, QK_NOPE + ROPE_HALF)
X2 = slice(QK_NOPE + ROPE_HALF, QK_HEAD)


def _rot_cols(w):
    w3 = w.reshape(w.shape[0], MLA_HEADS, HEAD_PAD)
    rot = jnp.zeros_like(w3).at[:, :, X1].set(-w3[:, :, X2]).at[:, :, X2].set(w3[:, :, X1])
    return rot.reshape(w.shape)


def _swap_halves(g):
    return jnp.zeros_like(g).at[:, X1].set(g[:, X2]).at[:, X2].set(g[:, X1])


def _mla_weights(p):
    H = MLA_HEADS
    wq = jnp.pad(p['w_uq'].reshape(Q_LORA, H, QK_HEAD), ((0, 0), (0, 0), (0, HEAD_PAD - QK_HEAD)))
    wq = wq.reshape(Q_LORA, H * HEAD_PAD)
    wkv = p['w_ukv'].reshape(KV_LORA, H, QK_NOPE + V_HEAD)
    wk_lat = jnp.pad(wkv[:, :, :QK_NOPE], ((0, 0), (0, 0), (0, HEAD_PAD - QK_NOPE)))
    place = jnp.zeros((LANES, H, HEAD_PAD), F32).at[:QK_ROPE, :, QK_NOPE:QK_HEAD].set(
        jnp.broadcast_to(jnp.eye(QK_ROPE, dtype=F32)[:, None, :], (QK_ROPE, H, QK_ROPE)))
    wk = jnp.concatenate([wk_lat, place], axis=0).reshape(KV_LORA + LANES, H * HEAD_PAD)
    wv = wkv[:, :, QK_NOPE:].reshape(KV_LORA, H * V_HEAD)
    gq = jnp.pad(p['q_norm'], (0, HEAD_PAD - QK_HEAD)).reshape(1, HEAD_PAD)
    gk = jnp.pad(p['k_norm'], (0, HEAD_PAD - QK_HEAD)).reshape(1, HEAD_PAD)
    bf = lambda w: w.astype(BF16)
    return (bf(wq), bf(_rot_cols(wq)), bf(wk), bf(_rot_cols(wk)), bf(wv), gq, _swap_halves(gq), gk, _swap_halves(gk))


def _mla_tables(n_ctx, T):
    cos, sin = _rope_tables(T)
    c = jnp.ones((n_ctx + T, HEAD_PAD), F32).at[n_ctx:, X1].set(cos).at[n_ctx:, X2].set(cos)
    s = jnp.zeros((n_ctx + T, HEAD_PAD), F32).at[n_ctx:, X1].set(sin).at[n_ctx:, X2].set(sin)
    return c, s


def _mla_prep_kernel(x_ref, c_ref, s_ref, qlg_ref, kvg_ref, wq_ref, wqr_ref, wk_ref, wkr_ref, wv_ref,
                     gq_ref, gqp_ref, gk_ref, gkp_ref, q_ref, k_ref, v_ref):
    x = x_ref[0].astype(F32)
    ql = _rms(x[:, :Q_LORA], qlg_ref[...])
    kvl = _rms(x[:, Q_LORA:Q_LORA + KV_LORA], kvg_ref[...])
    k_in = jnp.concatenate([kvl, x[:, Q_LORA + KV_LORA:]], axis=1)
    cos, sin = c_ref[...], s_ref[...]

    def finish(raw, partner, g, g_swapped, scale, o_ref):
        gc, gs = g * cos, g_swapped * sin
        for h in range(MLA_HEADS):
            sl = slice(h * HEAD_PAD, (h + 1) * HEAD_PAD)
            rh = raw[:, sl]
            inv = lax.rsqrt(jnp.sum(rh * rh, axis=-1, keepdims=True) * (1.0 / QK_HEAD) + EPS) * scale
            o_ref[0, :, sl] = ((rh * gc + partner[:, sl] * gs) * inv).astype(o_ref.dtype)

    finish(_dot(ql, wq_ref[...]), _dot(ql, wqr_ref[...]), gq_ref[...], gqp_ref[...], ATTN_SCALE, q_ref)
    finish(_dot(k_in, wk_ref[...]), _dot(k_in, wkr_ref[...]), gk_ref[...], gkp_ref[...], 1.0, k_ref)
    v_ref[0] = _dot(kvl, wv_ref[...]).astype(v_ref.dtype)


def _mla_prep(proj, p, n_ctx, T):
    B, Tt, _ = proj.shape
    H = MLA_HEADS
    ctx_tiles = n_ctx // ROW_TILE
    mla_blk = (PROJ_RW + PROJ_GATE) // PROJ_MLA
    weights = _mla_weights(p)
    cos, sin = _mla_tables(n_ctx, T)
    const = lambda b, i: (0, 0)
    row = lambda b, i: (b, i, 0)
    tab = pl.BlockSpec((ROW_TILE, HEAD_PAD), lambda b, i: (i, 0))
    full = lambda a: pl.BlockSpec(a.shape, const)
    return pl.pallas_call(
        _mla_prep_kernel,
        grid=(B, Tt // ROW_TILE),
        in_specs=[pl.BlockSpec((1, ROW_TILE, PROJ_MLA), lambda b, i: (b, i, mla_blk)), tab, tab,
                  pl.BlockSpec((1, Q_LORA), const), pl.BlockSpec((1, KV_LORA), const)]
                 + [full(w) for w in weights],
        out_specs=[
            pl.BlockSpec((1, ROW_TILE, H * HEAD_PAD), lambda b, i: (b, jnp.maximum(i - ctx_tiles, 0), 0)),
            pl.BlockSpec((1, ROW_TILE, H * HEAD_PAD), row),
            pl.BlockSpec((1, ROW_TILE, H * V_HEAD), row),
        ],
        out_shape=[
            jax.ShapeDtypeStruct((B, T, H * HEAD_PAD), BF16),
            jax.ShapeDtypeStruct((B, Tt, H * HEAD_PAD), BF16),
            jax.ShapeDtypeStruct((B, Tt, H * V_HEAD), BF16),
        ],
        compiler_params=pltpu.CompilerParams(
            dimension_semantics=("arbitrary", "arbitrary"), vmem_limit_bytes=VMEM_LIMIT),
        name="mla_prep",
    )(proj, cos, sin, p['q_lat_norm'].reshape(1, Q_LORA), p['kv_lat_norm'].reshape(1, KV_LORA), *weights)


ATTN_Q_TILE = 1024
HEADS_PER_STEP = LANES // V_HEAD


ATTN_ROW_SPLIT = 8


def _attn_kernel(q_ref, k_ref, v_ref, o_ref):
    v2 = v_ref[0]
    rows = q_ref.shape[1] // ATTN_ROW_SPLIT
    work = [(hh, rs) for rs in range(ATTN_ROW_SPLIT) for hh in range(HEADS_PER_STEP)]

    def scores(hh, rs):
        sl = slice(hh * HEAD_PAD, (hh + 1) * HEAD_PAD)
        return _dot_t(q_ref[0, rs * rows:(rs + 1) * rows, sl], k_ref[0, :, sl])

    outs = {}
    s_next = scores(*work[0])
    for i, (hh, rs) in enumerate(work):
        s = s_next
        if i + 1 < len(work):
            s_next = scores(*work[i + 1])
        e = jnp.exp(s - jnp.max(s, axis=-1, keepdims=True))
        outs[hh, rs] = _dot(e, v2) / jnp.sum(e, axis=-1, keepdims=True)
    lane = lax.broadcasted_iota(jnp.int32, (rows, LANES), 1)
    for rs in range(ATTN_ROW_SPLIT):
        o_ref[0, rs * rows:(rs + 1) * rows] = jnp.where(lane < V_HEAD, outs[0, rs], outs[1, rs])


def _attention(q, k, v):
    B, T, _ = q.shape
    Kt = k.shape[1]
    assert T % ATTN_Q_TILE == 0 and ATTN_Q_TILE % (ATTN_ROW_SPLIT * SUBLANES) == 0
    hp = MLA_HEADS // HEADS_PER_STEP
    qw = HEADS_PER_STEP * HEAD_PAD
    return pl.pallas_call(
        _attn_kernel,
        grid=(B, hp, T // ATTN_Q_TILE),
        in_specs=[
            pl.BlockSpec((1, ATTN_Q_TILE, qw), lambda b, h, i: (b, i, h)),
            pl.BlockSpec((1, Kt, qw), lambda b, h, i: (b, 0, h)),
            pl.BlockSpec((1, Kt, LANES), lambda b, h, i: (b, 0, h)),
        ],
        out_specs=pl.BlockSpec((1, ATTN_Q_TILE, LANES), lambda b, h, i: (b, i, h)),
        out_shape=jax.ShapeDtypeStruct((B, T, MLA_HEADS * V_HEAD), F32),
        compiler_params=pltpu.CompilerParams(
            dimension_semantics=("arbitrary", "arbitrary", "arbitrary"), vmem_limit_bytes=VMEM_LIMIT),
        name="attention",
    )(q, k, v)


HALO = 16
LORA_W = LANES
GATE_W = PROJ_RW - 3 * RW_DIM - 2 * LORA_W


def _head_ones(width, head):
    i = np.arange(width) // head
    return jnp.asarray(i[:, None] == i[None, :], BF16)


def _head_sum(x, ones_bd):
    hi = x.astype(BF16)
    lo = (x - hi.astype(F32)).astype(BF16)
    return (jnp.dot(hi, ones_bd, preferred_element_type=F32)
            + jnp.dot(lo, ones_bd, preferred_element_type=F32))


def _dot3(a, b_hi, b_lo):
    hi = a.astype(BF16)
    lo = (a - hi.astype(F32)).astype(BF16)
    d = lambda u, w: jnp.dot(u, w, preferred_element_type=F32)
    return d(hi, b_hi) + (d(hi, b_lo) + d(lo, b_hi))


def _rwkv_prep_kernel(x_ref, prev_ref, next_ref, conv_ref, kkg_ref, ka_ref, rk_ref, w0_ref, a0_ref,
                      w2h_ref, w2l_ref, a2_ref, g2_ref, ones_ref,
                      r_ref, v_ref, kk_ref, lw_ref, bk_ref, kd_ref, bonus_ref, gg_ref, *, ctx_tiles, n_tiles):
    i = pl.program_id(1)
    x = x_ref[0].astype(F32)
    tm, W = x.shape
    C = RW_DIM
    first = (i == 0) | (i == ctx_tiles)
    last = (i == ctx_tiles - 1) | (i == n_tiles - 1)
    prev_row = jnp.where(first, 0.0, prev_ref[0, HALO - 1:HALO].astype(F32))
    next_row = jnp.where(last, 0.0, next_ref[0, 0:1].astype(F32))
    row = lax.broadcasted_iota(jnp.int32, (tm, W), 0)
    x_dn = jnp.where(row == 0, prev_row, pltpu.roll(x, 1, 0))
    x_up = jnp.where(row == tm - 1, next_row, pltpu.roll(x, tm - 1, 0))
    xc = x_dn * conv_ref[0:1] + x * conv_ref[1:2] + x_up * conv_ref[2:3]

    r, k, v = xc[:, :C], xc[:, C:2 * C], xc[:, 2 * C:3 * C]
    lora_w = jnp.tanh(xc[:, 3 * C:3 * C + LORA_W])
    lora_a = xc[:, 3 * C + LORA_W:3 * C + 2 * LORA_W]
    lg = xc[:, 3 * C + 2 * LORA_W:]
    ones_bd = ones_ref[...]
    kq = k * kkg_ref[...]
    kk = kq * lax.rsqrt(_head_sum(kq * kq, ones_bd) + 1e-12)
    r_ref[0], v_ref[0], kk_ref[0] = r, v, kk

    k_sum = jnp.zeros_like(k)
    for d in range(2):
        z = w0_ref[d:d + 1] + _dot3(lora_w, w2h_ref[d], w2l_ref[d])
        softplus_neg = jnp.maximum(-z, 0.0) + jnp.log(1.0 + jnp.exp(-jnp.abs(z)))
        lw_ref[d, 0] = -jnp.exp(-softplus_neg - 0.5)
        a = jax.nn.sigmoid(a0_ref[d:d + 1] + _dot(lora_a, a2_ref[d]))
        kd = k * (1.0 + (a - 1.0) * ka_ref[...])
        bk_ref[d, 0] = kk * a
        kd_ref[d, 0] = kd
        k_sum = k_sum + kd
    bonus_ref[0] = _head_sum(r * k_sum * rk_ref[...], ones_bd) * v
    gg_ref[0] = _dot(jax.nn.sigmoid(lg), g2_ref[...])


def _rwkv_prep(proj, p, n_ctx):
    B, Tt, _ = proj.shape
    C = RW_DIM
    nt = Tt // ROW_TILE
    hb = ROW_TILE // HALO
    pad_cols = lambda w: jnp.pad(w, ((0, 0), (0, PROJ_RW - w.shape[1])))
    w2 = jnp.stack([jnp.pad(p['decay_w2'][0], ((0, LORA_W - DECAY_LORA), (0, 0))),
                    jnp.pad(p['decay_w2'][1], ((DECAY_LORA, 0), (0, 0)))])
    w2h = w2.astype(BF16)
    w2l = (w2 - w2h.astype(F32)).astype(BF16)
    a2 = jnp.stack([jnp.pad(p['aicl_a2'][0], ((0, LORA_W - AICL_LORA), (0, 0))),
                    jnp.pad(p['aicl_a2'][1], ((AICL_LORA, 0), (0, 0)))]).astype(BF16)
    g2 = jnp.pad(p['gate_g2'], ((0, GATE_W - GATE_LORA), (0, 0))).astype(BF16)
    row = lambda b, i: (b, i, 0)
    drow = lambda b, i: (0, b, i, 0)
    const2 = lambda b, i: (0, 0)
    const3 = lambda b, i: (0, 0, 0)
    vec = pl.BlockSpec((1, C), const2)
    out_row = pl.BlockSpec((1, ROW_TILE, C), row)
    out_dir = pl.BlockSpec((2, 1, ROW_TILE, C), drow)
    sds = jax.ShapeDtypeStruct((B, Tt, C), F32)
    sds2 = jax.ShapeDtypeStruct((2, B, Tt, C), F32)
    return pl.pallas_call(
        functools.partial(_rwkv_prep_kernel, ctx_tiles=n_ctx // ROW_TILE, n_tiles=nt),
        grid=(B, nt),
        in_specs=[
            pl.BlockSpec((1, ROW_TILE, PROJ_RW), row),
            pl.BlockSpec((1, HALO, PROJ_RW), lambda b, i: (b, jnp.maximum(i * hb - 1, 0), 0)),
            pl.BlockSpec((1, HALO, PROJ_RW), lambda b, i: (b, jnp.minimum((i + 1) * hb, nt * hb - 1), 0)),
            pl.BlockSpec((3, PROJ_RW), const2),
            vec, vec, vec,
            pl.BlockSpec((2, C), const2),
            pl.BlockSpec((2, C), const2),
            pl.BlockSpec((2, LORA_W, C), const3),
            pl.BlockSpec((2, LORA_W, C), const3),
            pl.BlockSpec((2, LORA_W, C), const3),
            pl.BlockSpec((GATE_W, C), const2),
            pl.BlockSpec((C, C), const2),
        ],
        out_specs=[out_row, out_row, out_row, out_dir, out_dir, out_dir, out_row, out_row],
        out_shape=[sds, sds, sds, sds2, sds2, sds2, sds, sds],
        compiler_params=pltpu.CompilerParams(
            dimension_semantics=("arbitrary", "arbitrary"), vmem_limit_bytes=VMEM_LIMIT),
        name="rwkv_prep",
    )(proj, proj, proj, pad_cols(p['shift_conv']), p['k_k'].reshape(1, C), p['k_a'].reshape(1, C),
      p['r_k'].reshape(1, C), p['decay_w0'], p['aicl_a0'], w2h, w2l, a2, g2, _head_ones(C, RW_HEAD))


def _merge_kernel(att_ref, yf_ref, yb_ref, bonus_ref, gg_ref, gate_ref, x_ref, ga_ref, gnw_ref, gnb_ref, ones_ref,
                  wm_ref, wr_ref, wo_ref, o_ref):
    D = x_ref.shape[-1]
    ones_bd = ones_ref[...]
    y = yf_ref[0] + yb_ref[0]
    yc = y - _head_sum(y, ones_bd) * (1.0 / RW_HEAD)
    var = _head_sum(yc * yc, ones_bd) * (1.0 / RW_HEAD)
    y_n = yc * lax.rsqrt(var + GN_EPS) * gnw_ref[...] + gnb_ref[...]
    rw = (y_n + bonus_ref[0]) * gg_ref[0]
    g = jax.nn.sigmoid(gate_ref[0].astype(F32))
    mix = g[:, :D] * _dot(att_ref[0], wm_ref[...]) + g[:, D:] * _dot(rw, wr_ref[...])
    o_ref[0] = x_ref[0] + ga_ref[0] * _dot(mix, wo_ref[...])


def _merge(att, y_f, y_b, bonus, gg, proj, x, g_a, p, n_ctx):
    B, T, D = x.shape
    C = RW_DIM
    ctx_tiles = n_ctx // ROW_TILE
    gate_blk = PROJ_RW // PROJ_GATE
    row = lambda b, i: (b, i, 0)
    lat_row = lambda b, i: (b, i + ctx_tiles, 0)
    const = lambda b, i: (0, 0)
    return pl.pallas_call(
        _merge_kernel,
        grid=(B, T // ROW_TILE),
        in_specs=[
            pl.BlockSpec((1, ROW_TILE, att.shape[-1]), row),
            pl.BlockSpec((1, ROW_TILE, C), row),
            pl.BlockSpec((1, ROW_TILE, C), row),
            pl.BlockSpec((1, ROW_TILE, C), lat_row),
            pl.BlockSpec((1, ROW_TILE, C), lat_row),
            pl.BlockSpec((1, ROW_TILE, PROJ_GATE), lambda b, i: (b, i + ctx_tiles, gate_blk)),
            pl.BlockSpec((1, ROW_TILE, D), row),
            pl.BlockSpec((1, 1, D), lambda b, i: (b, 0, 0)),
            pl.BlockSpec((1, C), const),
            pl.BlockSpec((1, C), const),
            pl.BlockSpec((C, C), const),
            pl.BlockSpec(p['w_o_mla'].shape, const),
            pl.BlockSpec(p['w_o_rwkv'].shape, const),
            pl.BlockSpec(p['w_out'].shape, const),
        ],
        out_specs=pl.BlockSpec((1, ROW_TILE, D), row),
        out_shape=jax.ShapeDtypeStruct((B, T, D), F32),
        compiler_params=pltpu.CompilerParams(
            dimension_semantics=("arbitrary", "arbitrary"), vmem_limit_bytes=VMEM_LIMIT),
        name="merge",
    )(att, y_f, y_b, bonus, gg, proj, x, g_a[:, None, :], p['gn_w'].reshape(1, C), p['gn_b'].reshape(1, C),
      _head_ones(C, RW_HEAD), p['w_o_mla'].astype(BF16), p['w_o_rwkv'].astype(BF16), p['w_out'].astype(BF16))


TILE_ROWS = 8
MOE_BLOCK = 512
ZERO_ROWS = MOE_BLOCK + TILE_ROWS


NEG_INF = float("-inf")
HI_MASK = 0xFFFF0000


def _pack_pair(lo, hi):
    lo_b = lax.bitcast_convert_type(lo.astype(BF16).astype(F32), jnp.uint32)
    hi_b = lax.bitcast_convert_type(hi.astype(BF16).astype(F32), jnp.uint32)
    return (lo_b >> 16) | (hi_b & jnp.uint32(HI_MASK))


def _unpack_pair(w):
    lo = lax.bitcast_convert_type(w << 16, F32)
    hi = lax.bitcast_convert_type(w & jnp.uint32(HI_MASK), F32)
    return lo.astype(BF16), hi.astype(BF16)


def _row_max(x):
    return jnp.max(x, axis=-1, keepdims=True)


def _first_index_of(x, value, lane_f):
    return jnp.min(jnp.where(x == value, lane_f, float(x.shape[-1])), axis=-1, keepdims=True)


def _router_kernel(x_ref, gain_ref, sh_ref, sc_ref, wh_ref, wm_ref, wl_ref, bias_ref,
                   u_ref, ids_ref, gates_ref, ranks_ref, counts_ref, carry_ref):
    @pl.when((pl.program_id(0) == 0) & (pl.program_id(1) == 0))
    def _():
        carry_ref[...] = jnp.zeros_like(carry_ref)

    x = x_ref[0]
    tm, D = x.shape
    E = bias_ref.shape[-1]
    u = x * lax.rsqrt(jnp.mean(x * x, axis=-1, keepdims=True) + EPS) * gain_ref[...]
    u = u * (1.0 + sc_ref[0]) + sh_ref[0]
    u_ref[0] = _pack_pair(u[:, :D // 2], u[:, D // 2:])

    uh, um, ul = _split3(u)
    wh, wm, wl = wh_ref[...], wm_ref[...], wl_ref[...]
    d = lambda a, b: jnp.dot(a, b, preferred_element_type=F32)
    logits = d(uh, wh) + (d(uh, wm) + d(um, wh)) + (d(uh, wl) + d(um, wm) + d(ul, wh))
    scores = jax.nn.sigmoid(logits)
    sel = scores + bias_ref[...]

    lane_i = lax.broadcasted_iota(jnp.int32, (tm, E), 1)
    lane_f = lane_i.astype(F32)
    out_f = lax.broadcasted_iota(jnp.int32, (tm, LANES), 1).astype(F32)
    per_group = E // N_GROUPS
    grp_f = jnp.floor(lane_f * (1.0 / per_group))

    gs = jnp.full((tm, LANES), NEG_INF, F32)
    for g in range(N_GROUPS):
        sg = jnp.where(lane_i >= g * per_group, jnp.where(lane_i < (g + 1) * per_group, sel, NEG_INF), NEG_INF)
        m1 = _row_max(sg)
        i1 = _first_index_of(sg, m1, lane_f)
        m2 = _row_max(jnp.where(lane_f == i1, NEG_INF, sg))
        gs = jnp.where(out_f == g, m1 + m2, gs)

    allow = jnp.zeros((tm, E), F32)
    for _ in range(TOPK_GROUPS):
        m = _row_max(gs)
        i = _first_index_of(gs, m, out_f)
        gs = jnp.where(out_f == i, NEG_INF, gs)
        allow = jnp.where(grp_f == i, 1.0, allow)
    selm = jnp.where(allow > 0.0, sel, NEG_INF)

    ids = jnp.zeros((tm, LANES), F32)
    gts = jnp.zeros((tm, LANES), F32)
    member = jnp.zeros((tm, E), F32)
    idx_cols = []
    gsum = jnp.zeros((tm, 1), F32)
    for k in range(TOP_K):
        m = _row_max(selm)
        i = _first_index_of(selm, m, lane_f)
        hit = lane_f == i
        gk = jnp.sum(jnp.where(hit, scores, 0.0), axis=-1, keepdims=True)
        selm = jnp.where(hit, NEG_INF, selm)
        member = jnp.where(hit, 1.0, member)
        ids = jnp.where(out_f == k, i, ids)
        gts = jnp.where(out_f == k, gk, gts)
        idx_cols.append(i)
        gsum = gsum + gk
    gts = gts / gsum * ROUTED_SCALE

    r2 = lax.broadcasted_iota(jnp.int32, (tm, tm), 0)
    c2 = lax.broadcasted_iota(jnp.int32, (tm, tm), 1)
    before = jnp.where(r2 > c2, 1.0, 0.0).astype(BF16)
    mem_b = member.astype(BF16)
    carry = carry_ref[...]
    pos = carry + jnp.dot(before, mem_b, preferred_element_type=F32)
    rk = jnp.zeros((tm, LANES), F32)
    for k in range(TOP_K):
        rk = jnp.where(out_f == k, jnp.sum(jnp.where(lane_f == idx_cols[k], pos, 0.0), axis=-1, keepdims=True), rk)
    colsum = jnp.dot(jnp.ones((8, tm), BF16), mem_b, preferred_element_type=F32)[0:1]
    carry_ref[...] = carry + colsum
    counts_ref[...] = carry + colsum
    ids_ref[0] = ids.astype(jnp.int32)
    ranks_ref[0] = rk.astype(jnp.int32)
    gates_ref[0] = gts


def _router(x1, sh_m, sc_m, p):
    B, T, D = x1.shape
    E = N_EXPERTS
    wh, wm, wl = _split3(p['router_w'])
    row = lambda b, i: (b, i, 0)
    const = lambda b, i: (0, 0)
    vec = lambda b, i: (b, 0, 0)
    lane_out = lambda dt: jax.ShapeDtypeStruct((B, T, LANES), dt)
    return pl.pallas_call(
        _router_kernel,
        grid=(B, T // ROW_TILE),
        in_specs=[
            pl.BlockSpec((1, ROW_TILE, D), row),
            pl.BlockSpec((1, D), const),
            pl.BlockSpec((1, 1, D), vec),
            pl.BlockSpec((1, 1, D), vec),
            pl.BlockSpec((D, E), const),
            pl.BlockSpec((D, E), const),
            pl.BlockSpec((D, E), const),
            pl.BlockSpec((1, E), const),
        ],
        out_specs=[
            pl.BlockSpec((1, ROW_TILE, D // 2), row),
            pl.BlockSpec((1, ROW_TILE, LANES), row),
            pl.BlockSpec((1, ROW_TILE, LANES), row),
            pl.BlockSpec((1, ROW_TILE, LANES), row),
            pl.BlockSpec((1, E), const),
        ],
        out_shape=[
            jax.ShapeDtypeStruct((B, T, D // 2), jnp.uint32),
            lane_out(jnp.int32), lane_out(F32), lane_out(jnp.int32),
            jax.ShapeDtypeStruct((1, E), F32),
        ],
        scratch_shapes=[pltpu.VMEM((1, E), F32)],
        compiler_params=pltpu.CompilerParams(
            dimension_semantics=("arbitrary", "arbitrary"), vmem_limit_bytes=VMEM_LIMIT),
        name="router",
    )(x1, p['norm_ffn'].reshape(1, D), sh_m[:, None, :], sc_m[:, None, :], wh, wm, wl,
      p['router_bias'].reshape(1, E))


def _slot_kernel(ids_ref, ranks_ref, base_ref, o_ref):
    ids = ids_ref[...].astype(F32)
    tm = ids.shape[0]
    E = base_ref.shape[-1]
    lane_e = lax.broadcasted_iota(jnp.int32, (tm, E), 1).astype(F32)
    out_lane = lax.broadcasted_iota(jnp.int32, (tm, LANES), 1)
    first = jnp.zeros((tm, LANES), F32)
    for k in range(TOP_K):
        fk = jnp.sum(jnp.where(lane_e == ids[:, k:k + 1], base_ref[...], 0.0), axis=-1, keepdims=True)
        first = jnp.where(out_lane == k, fk, first)
    o_ref[...] = first.astype(jnp.int32) + ranks_ref[...]


def _slots(ids, ranks, base):
    n_tok = ids.shape[0]
    E = base.shape[-1]
    row = pl.BlockSpec((ROW_TILE, LANES), lambda i: (i, 0))
    return pl.pallas_call(
        _slot_kernel,
        grid=(n_tok // ROW_TILE,),
        in_specs=[row, row, pl.BlockSpec((1, E), lambda i: (0, 0))],
        out_specs=row,
        out_shape=jax.ShapeDtypeStruct((n_tok, LANES), jnp.int32),
        compiler_params=pltpu.CompilerParams(dimension_semantics=("arbitrary",)),
        name="slots",
    )(ids, ranks, base)


DMA_PRIORITIES = 2


def _row_copy(src, src_row, dst, dst_row, sem):
    return pltpu.make_async_copy(src.at[pl.ds(src_row, 1)], dst.at[pl.ds(dst_row, 1)], sem)


def _dispatch_kernel(pad_ref, dest_ref, u_ref, slots_ref, zero_ref, sem):
    tm = u_ref.shape[0]
    n_experts = pad_ref.shape[0]

    @pl.when(pl.program_id(0) == 0)
    def _():
        zero_ref[...] = jnp.zeros_like(zero_ref)

        def fill(e, carry):
            start = pl.multiple_of(pad_ref[e], TILE_ROWS)
            pltpu.make_async_copy(zero_ref, slots_ref.at[pl.ds(start, ZERO_ROWS)], sem).start()
            return carry

        lax.fori_loop(0, n_experts, fill, 0)

        def filled(e, carry):
            pltpu.make_async_copy(zero_ref, slots_ref.at[pl.ds(0, ZERO_ROWS)], sem).wait()
            return carry

        lax.fori_loop(0, n_experts, filled, 0)

    def issue(r, carry):
        for k in range(TOP_K):
            _row_copy(u_ref, r, slots_ref, dest_ref[0, 0, r * TOP_K + k], sem).start(priority=k % DMA_PRIORITIES)
        return carry

    lax.fori_loop(0, tm, issue, 0)
    for k in range(TOP_K):
        pltpu.make_async_copy(u_ref, slots_ref.at[pl.ds(0, tm)], sem).wait()


def _dispatch(pad_start, dest, u_rows, n_slots):
    n_tok, W = u_rows.shape
    nt = n_tok // ROW_TILE
    grid_spec = pltpu.PrefetchScalarGridSpec(
        num_scalar_prefetch=1,
        grid=(nt,),
        in_specs=[
            pl.BlockSpec((1, 1, ROW_TILE * TOP_K), lambda i, pad: (i, 0, 0), memory_space=pltpu.SMEM),
            pl.BlockSpec((ROW_TILE, W), lambda i, pad: (i, 0)),
        ],
        out_specs=pl.BlockSpec(memory_space=pl.ANY),
        scratch_shapes=[pltpu.VMEM((ZERO_ROWS, W), jnp.uint32), pltpu.SemaphoreType.DMA(())],
    )
    return pl.pallas_call(
        _dispatch_kernel,
        grid_spec=grid_spec,
        out_shape=jax.ShapeDtypeStruct((n_slots + ZERO_ROWS, W), jnp.uint32),
        compiler_params=pltpu.CompilerParams(
            dimension_semantics=("arbitrary",), vmem_limit_bytes=VMEM_LIMIT),
        name="dispatch",
    )(pad_start, dest, u_rows)


def _moe_ffn_kernel(be_ref, nu_ref, x_ref, w1_ref, w3_ref, w2_ref, o_ref, w13_ref, w2b_ref):
    j = pl.program_id(0)
    F = w1_ref.shape[-1]
    half = x_ref.shape[-1]

    @pl.when((j == 0) | (be_ref[j] != be_ref[jnp.maximum(j - 1, 0)]))
    def _():
        w13_ref[:, :F] = w1_ref[0].astype(BF16)
        w13_ref[:, F:] = w3_ref[0].astype(BF16)
        w2b_ref[...] = w2_ref[0].astype(BF16)

    @pl.when(j < nu_ref[0])
    def _():
        lo, hi = _unpack_pair(x_ref[...])
        h = (jnp.dot(lo, w13_ref[:half], preferred_element_type=F32)
             + jnp.dot(hi, w13_ref[half:], preferred_element_type=F32))
        h1, h3 = h[:, :F], h[:, F:]
        y = _dot(h1 * jax.nn.sigmoid(h1) * h3, w2b_ref[...])
        o_ref[...] = pltpu.einshape("s(cl)->(sc)l", y, c=TILE_ROWS)

    @pl.when(j >= nu_ref[0])
    def _():
        o_ref[...] = jnp.zeros_like(o_ref)


def _moe_ffn(block_e, n_used, x_slots, w1, w3, w2):
    W = x_slots.shape[1]
    E, D, F = w1.shape
    nblk = block_e.shape[0]
    slots = nblk * MOE_BLOCK
    grid_spec = pltpu.PrefetchScalarGridSpec(
        num_scalar_prefetch=2,
        grid=(nblk,),
        in_specs=[
            pl.BlockSpec((MOE_BLOCK, W), lambda j, be, nu: (jnp.minimum(j, nu[0] - 1), 0)),
            pl.BlockSpec((1, D, F), lambda j, be, nu: (be[j], 0, 0)),
            pl.BlockSpec((1, D, F), lambda j, be, nu: (be[j], 0, 0)),
            pl.BlockSpec((1, F, D), lambda j, be, nu: (be[j], 0, 0)),
        ],
        out_specs=pl.BlockSpec((MOE_BLOCK * TILE_ROWS, LANES), lambda j, be, nu: (j, 0)),
        scratch_shapes=[pltpu.VMEM((D, 2 * F), BF16), pltpu.VMEM((F, D), BF16)],
    )
    return pl.pallas_call(
        _moe_ffn_kernel,
        grid_spec=grid_spec,
        out_shape=jax.ShapeDtypeStruct((slots * TILE_ROWS, LANES), F32),
        compiler_params=pltpu.CompilerParams(
            dimension_semantics=("arbitrary",), vmem_limit_bytes=VMEM_LIMIT),
        name="moe_ffn",
    )(block_e, n_used, x_slots, w1, w3, w2)


COMBINE_ROWS = 32


def _combine_kernel(dest_ref, gates_ref, u_ref, x_ref, gm_ref, w1_ref, w3_ref, w2_ref, ys_ref, o_ref, buf_ref, sem):
    tm, half = u_ref.shape

    def tile(i):
        return pl.ds(pl.multiple_of(i * TILE_ROWS, TILE_ROWS), TILE_ROWS)

    def issue(r, carry):
        for k in range(TOP_K):
            pltpu.make_async_copy(ys_ref.at[tile(dest_ref[0, 0, r * TOP_K + k])], buf_ref.at[k, tile(r)], sem).start(
                priority=k % DMA_PRIORITIES)
        return carry

    lax.fori_loop(0, tm, issue, 0)

    ulo, uhi = _unpack_pair(u_ref[...])
    both = lambda w_ref: (jnp.dot(ulo, w_ref[:half], preferred_element_type=F32)
                          + jnp.dot(uhi, w_ref[half:], preferred_element_type=F32))
    h1, h3 = both(w1_ref), both(w3_ref)
    o_ref[...] = x_ref[...] + gm_ref[0] * _dot(h1 * jax.nn.sigmoid(h1) * h3, w2_ref[...])

    for k in range(TOP_K):
        pltpu.make_async_copy(ys_ref.at[pl.ds(0, tm * TILE_ROWS)], buf_ref.at[k], sem).wait()

    gm = gm_ref[0]
    for rb in range(tm // COMBINE_ROWS):
        rows = slice(rb * COMBINE_ROWS, (rb + 1) * COMBINE_ROWS)
        gates = gates_ref[rows, :]
        for c in range(TILE_ROWS):
            acc = jnp.zeros((COMBINE_ROWS, LANES), F32)
            for k in range(TOP_K):
                acc = acc + gates[:, k:k + 1] * buf_ref[
                    k, pl.ds(rb * COMBINE_ROWS * TILE_ROWS + c, COMBINE_ROWS, stride=TILE_ROWS), :]
            cols = slice(c * LANES, (c + 1) * LANES)
            o_ref[rows, cols] = o_ref[rows, cols] + gm[:, cols] * acc


def _combine(dest, gates, u_rows, x1, g_m, y_slots, p):
    B, T, D = x1.shape
    n_tok = B * T
    W = u_rows.shape[1]
    tiles_per_batch = T // ROW_TILE
    row = lambda i: (i, 0)
    const = lambda i: (0, 0)
    out = pl.pallas_call(
        _combine_kernel,
        grid=(n_tok // ROW_TILE,),
        in_specs=[
            pl.BlockSpec((1, 1, ROW_TILE * TOP_K), lambda i: (i, 0, 0), memory_space=pltpu.SMEM),
            pl.BlockSpec((ROW_TILE, LANES), row),
            pl.BlockSpec((ROW_TILE, W), row),
            pl.BlockSpec((ROW_TILE, D), row),
            pl.BlockSpec((1, 1, D), lambda i: (i // tiles_per_batch, 0, 0)),
            pl.BlockSpec(p['shared_w1'].shape, const),
            pl.BlockSpec(p['shared_w3'].shape, const),
            pl.BlockSpec(p['shared_w2'].shape, const),
            pl.BlockSpec(memory_space=pl.ANY),
        ],
        out_specs=pl.BlockSpec((ROW_TILE, D), row),
        out_shape=jax.ShapeDtypeStruct((n_tok, D), F32),
        scratch_shapes=[pltpu.VMEM((TOP_K, ROW_TILE * TILE_ROWS, LANES), F32), pltpu.SemaphoreType.DMA(())],
        compiler_params=pltpu.CompilerParams(
            dimension_semantics=("arbitrary",), vmem_limit_bytes=VMEM_LIMIT),
        name="combine",
    )(dest, gates.reshape(n_tok, LANES), u_rows, x1.reshape(n_tok, D), g_m[:, None, :],
      p['shared_w1'].astype(BF16), p['shared_w3'].astype(BF16), p['shared_w2'].astype(BF16), y_slots)
    return out.reshape(B, T, D)


def _moe(x1, sh_m, sc_m, g_m, p):
    B, T, D = x1.shape
    n_tok = B * T
    u_packed, ids, gates, ranks, counts = _router(x1, sh_m, sc_m, p)

    counts = counts[0].astype(jnp.int32)
    padded = (counts + MOE_BLOCK - 1) // MOE_BLOCK * MOE_BLOCK
    padded_end = jnp.cumsum(padded)
    base = padded_end - padded
    n_blocks = n_tok * TOP_K // MOE_BLOCK + N_EXPERTS
    n_used = (padded_end[-1] // MOE_BLOCK).astype(jnp.int32)
    blk = jnp.minimum(jnp.arange(n_blocks), n_used - 1) * MOE_BLOCK
    block_e = jnp.minimum(jnp.sum(padded_end[None, :] <= blk[:, None], axis=1), N_EXPERTS - 1).astype(jnp.int32)
    dest = _slots(ids.reshape(n_tok, LANES), ranks.reshape(n_tok, LANES), base.astype(F32).reshape(1, N_EXPERTS))
    dest = dest[:, :TOP_K].reshape(n_tok // ROW_TILE, 1, ROW_TILE * TOP_K)

    u_rows = u_packed.reshape(n_tok, D // 2)
    pad_start = ((base + counts) // TILE_ROWS * TILE_ROWS).astype(jnp.int32)
    x_slots = _dispatch(pad_start, dest, u_rows, n_blocks * MOE_BLOCK)
    y_slots = _moe_ffn(block_e, n_used.reshape(1), x_slots, p['expert_w1'], p['expert_w3'], p['expert_w2'])
    return _combine(dest, gates, u_rows, x1, g_m, y_slots, p)
```

```python
import functools

import jax
import jax.numpy as jnp
import numpy as np
from jax import lax
from jax.experimental import pallas as pl
from jax.experimental.pallas import tpu as pltpu

F32 = jnp.float32
BF16 = jnp.bfloat16

GRID_W = 64
EPS = 1e-6
MLA_HEADS = 8
QK_NOPE = 64
QK_ROPE = 32
QK_HEAD = QK_NOPE + QK_ROPE
V_HEAD = 64
Q_LORA = 256
KV_LORA = 128
ROPE_BASE = 10000.0
ATTN_SCALE = QK_HEAD ** -0.5
RW_HEADS = 8
RW_HEAD = 64
RW_DIM = RW_HEADS * RW_HEAD
DECAY_LORA = 64
AICL_LORA = 64
GATE_LORA = 160
GN_EPS = 64e-5
N_EXPERTS = 256
TOP_K = 8
N_GROUPS = 8
TOPK_GROUPS = 4
ROUTED_SCALE = 2.5

LANES = 128
MXU_DIM = 256
VMEM_LIMIT = 56 * 1024 * 1024

WKV_CHUNK = 64
HEADS_PER_GROUP = MXU_DIM // RW_HEAD
HEAD_PAD = LANES


def _dot(a, b):
    return jnp.dot(a.astype(BF16), b.astype(BF16), preferred_element_type=F32)


def _dot_t(a, b):
    return lax.dot_general(a.astype(BF16), b.astype(BF16), (((1,), (1,)), ((), ())),
                           preferred_element_type=F32)


def _split3(x):
    h = x.astype(BF16)
    r1 = x - h.astype(F32)
    m = r1.astype(BF16)
    lo = (r1 - m.astype(F32)).astype(BF16)
    return h, m, lo


def _dot_hi(a_bf16_exact, x):
    h, m, lo = _split3(x)
    d = lambda y: jnp.dot(a_bf16_exact, y, preferred_element_type=F32)
    return d(h) + d(m) + d(lo)


def _wkv_chunk(r, v, kk, lw, bk, kd, s0, reverse, emit):
    L, G = r.shape
    row = lax.broadcasted_iota(jnp.int32, (L, G), 0)
    lane = lax.broadcasted_iota(jnp.int32, (L, G), 1)
    diff = (lane % L - row) if reverse else (row - lane % L)
    strict = diff > 0
    incl = diff >= 0
    r2 = lax.broadcasted_iota(jnp.int32, (L, L), 0)
    c2 = lax.broadcasted_iota(jnp.int32, (L, L), 1)
    tri = jnp.where(((c2 - r2) if reverse else (r2 - c2)) >= 0, 1.0, 0.0).astype(BF16)
    tri_ones = jnp.concatenate([tri, jnp.ones((L, L), BF16)], axis=0)
    lane_head = lane // RW_HEAD
    rowb = lax.broadcasted_iota(jnp.int32, (G, G), 0) // RW_HEAD
    colb = lax.broadcasted_iota(jnp.int32, (G, G), 1) // RW_HEAD

    def stack(x):
        return jnp.concatenate(
            [jnp.where(lane_head == h, x, 0.0) for h in range(HEADS_PER_GROUP)], axis=0).astype(BF16)

    sums = _dot_hi(tri_ones, lw)
    cum_in = sums[:L]
    g_in = jnp.exp(cum_in)
    g_inv = jnp.exp(-cum_in)
    g_ex = jnp.exp(cum_in - lw)
    g_tot = jnp.exp(sums[L:])
    a_h = -kk * g_ex
    b_h = bk * g_inv
    k_h = kd * g_inv
    r_h = r * g_in
    yield

    gram = _dot_t(jnp.concatenate([a_h, r_h], axis=0),
                  jnp.concatenate([stack(b_h), stack(k_h)], axis=0))
    ab = jnp.where(strict, gram[:L, :G], 0.0)
    ak = jnp.where(strict, gram[:L, G:], 0.0)
    rb = jnp.where(incl, gram[L:, :G], 0.0)
    rk = jnp.where(incl, gram[L:, G:], 0.0)
    yield

    tm = jnp.where(diff == 0, 1.0, 0.0) + ab
    p = ab
    v_st = stack(v)
    akv = _dot(ak, v_st)
    for _ in range(int(np.log2(L)) - 1):
        p = _dot(p, stack(p))
        yield
        tm = tm + _dot(tm, stack(p))
        yield

    wu = _dot(tm, jnp.concatenate([stack(a_h), stack(akv)], axis=1))
    w_t, u_t = wu[:, :G], wu[:, G:]
    yield
    rbwu = _dot(rb, jnp.concatenate([stack(w_t), stack(u_t)], axis=1))
    r_t = r_h + rbwu[:, :G]
    y_t = rbwu[:, G:] + _dot(rk, v_st)
    yield

    s0_st = stack(s0)
    y = _dot_t(r_t, s0_st) + y_t
    u = _dot_t(w_t, s0_st) + u_t
    yield
    uv = jnp.concatenate([u, v], axis=0)
    bkc = jnp.concatenate([b_h, k_h], axis=0)
    upd = lax.dot_general(uv.astype(BF16), bkc.astype(BF16), (((0,), (0,)), ((), ())),
                          preferred_element_type=F32)
    upd = jnp.where(rowb == colb, upd, 0.0)
    upd_d = upd[0:L]
    for h in range(1, HEADS_PER_GROUP):
        upd_d = upd_d + upd[h * L:(h + 1) * L]
    emit(y, (s0 + upd_d) * g_tot)


def _wkv_kernel(*refs, n_groups, n_batch):
    ins, (yf_ref, yb_ref, s_ref) = refs[:12], refs[12:]
    G = MXU_DIM

    @pl.when(pl.program_id(1) == 0)
    def _():
        s_ref[...] = jnp.zeros_like(s_ref)

    chains = []
    for d, y_ref in enumerate((yf_ref, yb_ref)):
        r_ref, v_ref, kk_ref, lw_ref, bk_ref, kd_ref = ins[6 * d:6 * d + 6]
        for bi in range(n_batch):
            for g in range(n_groups):
                sl = slice(g * G, (g + 1) * G)

                def emit(y, s_new, y_ref=y_ref, d=d, bi=bi, g=g, sl=sl):
                    y_ref[bi, :, sl] = y
                    s_ref[d, bi, g] = s_new

                chains.append(_wkv_chunk(
                    r_ref[bi, :, sl], v_ref[bi, :, sl], kk_ref[bi, :, sl], lw_ref[0, bi, :, sl],
                    bk_ref[0, bi, :, sl], kd_ref[0, bi, :, sl], s_ref[d, bi, g], d == 1, emit))
    while chains:
        chains = [c for c in chains if next(c, StopIteration) is not StopIteration]


WKV_BATCH = 2


def _wkv_scan(r, v, kk, lw, bk, kd, n_ctx):
    B, Ttot, C = r.shape
    L = WKV_CHUNK
    nc = Ttot // L
    ncc = n_ctx // L
    nl = nc - ncc
    n_groups = C // MXU_DIM
    nb = WKV_BATCH

    cid = (lambda s: s, lambda s: jnp.where(s < ncc, ncc - 1 - s, nc + ncc - 1 - s))
    first_out = (0, nl - 1)
    in_specs, out_specs = [], []
    for d in range(2):
        shared = pl.BlockSpec((nb, L, C), lambda b, s, d=d: (b, cid[d](s), 0))
        per_dir = pl.BlockSpec((1, nb, L, C), lambda b, s, d=d: (d, b, cid[d](s), 0))
        in_specs += [shared, shared, shared, per_dir, per_dir, per_dir]
        out_specs.append(pl.BlockSpec(
            (nb, L, C), lambda b, s, d=d: (b, jnp.where(s < ncc, first_out[d], cid[d](s) - ncc), 0)))
    out = jax.ShapeDtypeStruct((B, nl * L, C), F32)
    return pl.pallas_call(
        functools.partial(_wkv_kernel, n_groups=n_groups, n_batch=nb),
        grid=(B // nb, nc),
        in_specs=in_specs,
        out_specs=out_specs,
        out_shape=[out, out],
        scratch_shapes=[pltpu.VMEM((2, nb, n_groups, L, MXU_DIM), F32)],
        compiler_params=pltpu.CompilerParams(
            dimension_semantics=("arbitrary", "arbitrary"), vmem_limit_bytes=VMEM_LIMIT),
        name="wkv_scan",
    )(r, v, kk, lw, bk, kd, r, v, kk, lw, bk, kd)


def kernel(x, c, ctx, c_ctx, ada_w, ada_b, norm_mix, norm_ffn, w_in, shift_conv, q_lat_norm, w_uq, kv_lat_norm, w_ukv, q_norm, k_norm, w_o_mla, decay_w0, decay_w2, aicl_a0, aicl_a2, k_k, k_a, r_k, gn_w, gn_b, gate_g2, w_o_rwkv, w_out, router_w, router_bias, expert_w1, expert_w3, expert_w2, shared_w1, shared_w3, shared_w2):
    B, T, D = x.shape
    n_ctx = ctx.shape[1]
    i0 = 0
    p = dict(norm_mix=norm_mix[i0], norm_ffn=norm_ffn[i0], w_in=w_in[i0], shift_conv=shift_conv[i0],
             q_lat_norm=q_lat_norm[i0], w_uq=w_uq[i0], kv_lat_norm=kv_lat_norm[i0], w_ukv=w_ukv[i0],
             q_norm=q_norm[i0], k_norm=k_norm[i0], w_o_mla=w_o_mla[i0],
             decay_w0=decay_w0[i0], decay_w2=decay_w2[i0], aicl_a0=aicl_a0[i0], aicl_a2=aicl_a2[i0],
             k_k=k_k[i0], k_a=k_a[i0], r_k=r_k[i0], gn_w=gn_w[i0], gn_b=gn_b[i0], gate_g2=gate_g2[i0],
             w_o_rwkv=w_o_rwkv[i0], w_out=w_out[i0], router_w=router_w[i0], router_bias=router_bias[i0],
             expert_w1=expert_w1[i0], expert_w3=expert_w3[i0], expert_w2=expert_w2[i0],
             shared_w1=shared_w1[i0], shared_w3=shared_w3[i0], shared_w2=shared_w2[i0])

    mod = _ada_modulation(c, c_ctx, ada_w[i0], ada_b[i0])
    sh_a, sc_a, g_a, sh_m, sc_m, g_m = jnp.split(mod[:B], 6, axis=-1)
    csh_a, csc_a = jnp.split(mod[B], 6, axis=-1)[:2]
    sh2 = jnp.stack([jnp.broadcast_to(csh_a, (B, D)), sh_a], axis=1)
    sc2 = jnp.stack([jnp.broadcast_to(csc_a, (B, D)), sc_a], axis=1)

    proj = _in_proj(ctx, x, sh2, sc2, p['norm_mix'], _pack_w_in(p['w_in']))

    q, k, v = _mla_prep(proj, p, n_ctx, T)
    att = _attention(q, k, v)

    r, vv, kk, lw, bk, kd, bonus, gg = _rwkv_prep(proj, p, n_ctx)
    y_f, y_b = _wkv_scan(r, vv, kk, lw, bk, kd, n_ctx)

    x1 = _merge(att, y_f, y_b, bonus, gg, proj, x, g_a, p, n_ctx)
    return _moe(x1, sh_m, sc_m, g_m, p)


SUBLANES = 8


def _ada_kernel(c_ref, w_ref, b_ref, o_ref):
    cc = c_ref[...]
    w = w_ref[...]
    w_hi = w.astype(BF16)
    w_lo = (w - w_hi.astype(F32)).astype(BF16)
    o_ref[...] = _dot3(cc * jax.nn.sigmoid(cc), w_hi, w_lo) + b_ref[...]


def _ada_modulation(c, c_ctx, ada_w, ada_b):
    B, D = c.shape
    n_out = ada_w.shape[1]
    rows = -(-(B + 1) // SUBLANES) * SUBLANES
    c_all = jnp.concatenate([c, c_ctx[None, :], jnp.zeros((rows - B - 1, D), F32)], axis=0)
    out = pl.pallas_call(
        _ada_kernel,
        grid=(n_out // D,),
        in_specs=[pl.BlockSpec((rows, D), lambda j: (0, 0)),
                  pl.BlockSpec((D, D), lambda j: (0, j)),
                  pl.BlockSpec((1, D), lambda j: (0, j))],
        out_specs=pl.BlockSpec((rows, D), lambda j: (0, j)),
        out_shape=jax.ShapeDtypeStruct((rows, n_out), F32),
        compiler_params=pltpu.CompilerParams(dimension_semantics=("arbitrary",), vmem_limit_bytes=VMEM_LIMIT),
        name="ada_modulation",
    )(c_all, ada_w, ada_b.reshape(1, n_out))
    return out[:B + 1]


ROW_TILE = 256
PROJ_RW = 2048
PROJ_GATE = 2048
PROJ_MLA = 512
PROJ_W = PROJ_RW + PROJ_GATE + PROJ_MLA
MLA_IN = Q_LORA + KV_LORA + QK_ROPE
RW_SPLITS = (RW_DIM, RW_DIM, RW_DIM, DECAY_LORA, DECAY_LORA, AICL_LORA, AICL_LORA, GATE_LORA)
RW_IN = sum(RW_SPLITS)


def _pack_w_in(w_in):
    w_mla = w_in[:, :MLA_IN]
    w_rw = w_in[:, MLA_IN:MLA_IN + RW_IN]
    w_gate = w_in[:, MLA_IN + RW_IN:]
    pad = lambda w, n: jnp.pad(w, ((0, 0), (0, n - w.shape[1])))
    return jnp.concatenate([pad(w_rw, PROJ_RW), w_gate, pad(w_mla, PROJ_MLA)], axis=1).astype(BF16)


def _in_proj_kernel(ctx_ref, x_ref, sh_ref, sc_ref, gain_ref, w_ref, o_ref):
    is_ctx = pl.program_id(1) == 0
    xt = jnp.where(is_ctx, ctx_ref[0], x_ref[0])
    sh = jnp.where(is_ctx, sh_ref[0, 0:1], sh_ref[0, 1:2])
    sc = jnp.where(is_ctx, sc_ref[0, 0:1], sc_ref[0, 1:2])
    y = xt * lax.rsqrt(jnp.mean(xt * xt, axis=-1, keepdims=True) + EPS) * gain_ref[...]
    h = y * (1.0 + sc) + sh
    o_ref[0] = _dot(h, w_ref[...]).astype(o_ref.dtype)


def _in_proj(ctx, x, sh2, sc2, gain, w):
    B, T, D = x.shape
    n_ctx = ctx.shape[1]
    assert n_ctx == ROW_TILE and T % ROW_TILE == 0
    nt = 1 + T // ROW_TILE
    return pl.pallas_call(
        _in_proj_kernel,
        grid=(B, nt),
        in_specs=[
            pl.BlockSpec((1, ROW_TILE, D), lambda b, i: (b, 0, 0)),
            pl.BlockSpec((1, ROW_TILE, D), lambda b, i: (b, jnp.maximum(i - 1, 0), 0)),
            pl.BlockSpec((1, 2, D), lambda b, i: (b, 0, 0)),
            pl.BlockSpec((1, 2, D), lambda b, i: (b, 0, 0)),
            pl.BlockSpec((1, D), lambda b, i: (0, 0)),
            pl.BlockSpec((D, PROJ_W), lambda b, i: (0, 0)),
        ],
        out_specs=pl.BlockSpec((1, ROW_TILE, PROJ_W), lambda b, i: (b, i, 0)),
        out_shape=jax.ShapeDtypeStruct((B, n_ctx + T, PROJ_W), BF16),
        compiler_params=pltpu.CompilerParams(
            dimension_semantics=("arbitrary", "arbitrary"), vmem_limit_bytes=VMEM_LIMIT),
        name="in_proj",
    )(ctx, x, sh2, sc2, gain.reshape(1, D), w)


def _rms(x, gain):
    return x * lax.rsqrt(jnp.mean(x * x, axis=-1, keepdims=True) + EPS) * gain


def _rope_tables(n_tokens):
    rows = n_tokens // GRID_W
    row = jnp.repeat(jnp.arange(rows, dtype=F32), GRID_W)
    col = jnp.tile(jnp.arange(GRID_W, dtype=F32), rows)
    n_freq = QK_ROPE // 4
    inv_freq = ROPE_BASE ** (-jnp.arange(n_freq, dtype=F32) / n_freq)
    ang = jnp.concatenate([row[:, None] * inv_freq, col[:, None] * inv_freq], axis=-1)
    return jnp.cos(ang), jnp.sin(ang)


ROPE_HALF = QK_ROPE // 2
X1 = slice(QK_NOPE, QK_NOPE + ROPE_HALF)
X2 = slice(QK_NOPE + ROPE_HALF, QK_HEAD)


def _rot_cols(w):
    w3 = w.reshape(w.shape[0], MLA_HEADS, HEAD_PAD)
    rot = jnp.zeros_like(w3).at[:, :, X1].set(-w3[:, :, X2]).at[:, :, X2].set(w3[:, :, X1])
    return rot.reshape(w.shape)


def _swap_halves(g):
    return jnp.zeros_like(g).at[:, X1].set(g[:, X2]).at[:, X2].set(g[:, X1])


def _mla_weights(p):
    H = MLA_HEADS
    wq = jnp.pad(p['w_uq'].reshape(Q_LORA, H, QK_HEAD), ((0, 0), (0, 0), (0, HEAD_PAD - QK_HEAD)))
    wq = wq.reshape(Q_LORA, H * HEAD_PAD)
    wkv = p['w_ukv'].reshape(KV_LORA, H, QK_NOPE + V_HEAD)
    wk_lat = jnp.pad(wkv[:, :, :QK_NOPE], ((0, 0), (0, 0), (0, HEAD_PAD - QK_NOPE)))
    place = jnp.zeros((LANES, H, HEAD_PAD), F32).at[:QK_ROPE, :, QK_NOPE:QK_HEAD].set(
        jnp.broadcast_to(jnp.eye(QK_ROPE, dtype=F32)[:, None, :], (QK_ROPE, H, QK_ROPE)))
    wk = jnp.concatenate([wk_lat, place], axis=0).reshape(KV_LORA + LANES, H * HEAD_PAD)
    wv = wkv[:, :, QK_NOPE:].reshape(KV_LORA, H * V_HEAD)
    gq = jnp.pad(p['q_norm'], (0, HEAD_PAD - QK_HEAD)).reshape(1, HEAD_PAD)
    gk = jnp.pad(p['k_norm'], (0, HEAD_PAD - QK_HEAD)).reshape(1, HEAD_PAD)
    bf = lambda w: w.astype(BF16)
    return (bf(wq), bf(_rot_cols(wq)), bf(wk), bf(_rot_cols(wk)), bf(wv), gq, _swap_halves(gq), gk, _swap_halves(gk))


def _mla_tables(n_ctx, T):
    cos, sin = _rope_tables(T)
    c = jnp.ones((n_ctx + T, HEAD_PAD), F32).at[n_ctx:, X1].set(cos).at[n_ctx:, X2].set(cos)
    s = jnp.zeros((n_ctx + T, HEAD_PAD), F32).at[n_ctx:, X1].set(sin).at[n_ctx:, X2].set(sin)
    return c, s


def _mla_prep_kernel(x_ref, c_ref, s_ref, qlg_ref, kvg_ref, wq_ref, wqr_ref, wk_ref, wkr_ref, wv_ref,
                     gq_ref, gqp_ref, gk_ref, gkp_ref, q_ref, k_ref, v_ref):
    x = x_ref[0].astype(F32)
    ql = _rms(x[:, :Q_LORA], qlg_ref[...])
    kvl = _rms(x[:, Q_LORA:Q_LORA + KV_LORA], kvg_ref[...])
    k_in = jnp.concatenate([kvl, x[:, Q_LORA + KV_LORA:]], axis=1)
    cos, sin = c_ref[...], s_ref[...]

    def finish(raw, partner, g, g_swapped, scale, o_ref):
        gc, gs = g * cos, g_swapped * sin
        for h in range(MLA_HEADS):
            sl = slice(h * HEAD_PAD, (h + 1) * HEAD_PAD)
            rh = raw[:, sl]
            inv = lax.rsqrt(jnp.sum(rh * rh, axis=-1, keepdims=True) * (1.0 / QK_HEAD) + EPS) * scale
            o_ref[0, :, sl] = ((rh * gc + partner[:, sl] * gs) * inv).astype(o_ref.dtype)

    finish(_dot(ql, wq_ref[...]), _dot(ql, wqr_ref[...]), gq_ref[...], gqp_ref[...], ATTN_SCALE, q_ref)
    finish(_dot(k_in, wk_ref[...]), _dot(k_in, wkr_ref[...]), gk_ref[...], gkp_ref[...], 1.0, k_ref)
    v_ref[0] = _dot(kvl, wv_ref[...]).astype(v_ref.dtype)


def _mla_prep(proj, p, n_ctx, T):
    B, Tt, _ = proj.shape
    H = MLA_HEADS
    ctx_tiles = n_ctx // ROW_TILE
    mla_blk = (PROJ_RW + PROJ_GATE) // PROJ_MLA
    weights = _mla_weights(p)
    cos, sin = _mla_tables(n_ctx, T)
    const = lambda b, i: (0, 0)
    row = lambda b, i: (b, i, 0)
    tab = pl.BlockSpec((ROW_TILE, HEAD_PAD), lambda b, i: (i, 0))
    full = lambda a: pl.BlockSpec(a.shape, const)
    return pl.pallas_call(
        _mla_prep_kernel,
        grid=(B, Tt // ROW_TILE),
        in_specs=[pl.BlockSpec((1, ROW_TILE, PROJ_MLA), lambda b, i: (b, i, mla_blk)), tab, tab,
                  pl.BlockSpec((1, Q_LORA), const), pl.BlockSpec((1, KV_LORA), const)]
                 + [full(w) for w in weights],
        out_specs=[
            pl.BlockSpec((1, ROW_TILE, H * HEAD_PAD), lambda b, i: (b, jnp.maximum(i - ctx_tiles, 0), 0)),
            pl.BlockSpec((1, ROW_TILE, H * HEAD_PAD), row),
            pl.BlockSpec((1, ROW_TILE, H * V_HEAD), row),
        ],
        out_shape=[
            jax.ShapeDtypeStruct((B, T, H * HEAD_PAD), BF16),
            jax.ShapeDtypeStruct((B, Tt, H * HEAD_PAD), BF16),
            jax.ShapeDtypeStruct((B, Tt, H * V_HEAD), BF16),
        ],
        compiler_params=pltpu.CompilerParams(
            dimension_semantics=("arbitrary", "arbitrary"), vmem_limit_bytes=VMEM_LIMIT),
        name="mla_prep",
    )(proj, cos, sin, p['q_lat_norm'].reshape(1, Q_LORA), p['kv_lat_norm'].reshape(1, KV_LORA), *weights)


ATTN_Q_TILE = 1024
HEADS_PER_STEP = LANES // V_HEAD


ATTN_ROW_SPLIT = 8


def _attn_kernel(q_ref, k_ref, v_ref, o_ref):
    v2 = v_ref[0]
    rows = q_ref.shape[1] // ATTN_ROW_SPLIT
    work = [(hh, rs) for rs in range(ATTN_ROW_SPLIT) for hh in range(HEADS_PER_STEP)]

    def scores(hh, rs):
        sl = slice(hh * HEAD_PAD, (hh + 1) * HEAD_PAD)
        return _dot_t(q_ref[0, rs * rows:(rs + 1) * rows, sl], k_ref[0, :, sl])

    outs = {}
    s_next = scores(*work[0])
    for i, (hh, rs) in enumerate(work):
        s = s_next
        if i + 1 < len(work):
            s_next = scores(*work[i + 1])
        e = jnp.exp(s - jnp.max(s, axis=-1, keepdims=True))
        outs[hh, rs] = _dot(e, v2) / jnp.sum(e, axis=-1, keepdims=True)
    lane = lax.broadcasted_iota(jnp.int32, (rows, LANES), 1)
    for rs in range(ATTN_ROW_SPLIT):
        o_ref[0, rs * rows:(rs + 1) * rows] = jnp.where(lane < V_HEAD, outs[0, rs], outs[1, rs])


def _attention(q, k, v):
    B, T, _ = q.shape
    Kt = k.shape[1]
    assert T % ATTN_Q_TILE == 0 and ATTN_Q_TILE % (ATTN_ROW_SPLIT * SUBLANES) == 0
    hp = MLA_HEADS // HEADS_PER_STEP
    qw = HEADS_PER_STEP * HEAD_PAD
    return pl.pallas_call(
        _attn_kernel,
        grid=(B, hp, T // ATTN_Q_TILE),
        in_specs=[
            pl.BlockSpec((1, ATTN_Q_TILE, qw), lambda b, h, i: (b, i, h)),
            pl.BlockSpec((1, Kt, qw), lambda b, h, i: (b, 0, h)),
            pl.BlockSpec((1, Kt, LANES), lambda b, h, i: (b, 0, h)),
        ],
        out_specs=pl.BlockSpec((1, ATTN_Q_TILE, LANES), lambda b, h, i: (b, i, h)),
        out_shape=jax.ShapeDtypeStruct((B, T, MLA_HEADS * V_HEAD), F32),
        compiler_params=pltpu.CompilerParams(
            dimension_semantics=("arbitrary", "arbitrary", "arbitrary"), vmem_limit_bytes=VMEM_LIMIT),
        name="attention",
    )(q, k, v)


HALO = 16
LORA_W = LANES
GATE_W = PROJ_RW - 3 * RW_DIM - 2 * LORA_W


def _head_ones(width, head):
    i = np.arange(width) // head
    return jnp.asarray(i[:, None] == i[None, :], BF16)


def _head_sum(x, ones_bd):
    hi = x.astype(BF16)
    lo = (x - hi.astype(F32)).astype(BF16)
    return (jnp.dot(hi, ones_bd, preferred_element_type=F32)
            + jnp.dot(lo, ones_bd, preferred_element_type=F32))


def _dot3(a, b_hi, b_lo):
    hi = a.astype(BF16)
    lo = (a - hi.astype(F32)).astype(BF16)
    d = lambda u, w: jnp.dot(u, w, preferred_element_type=F32)
    return d(hi, b_hi) + (d(hi, b_lo) + d(lo, b_hi))


def _rwkv_prep_kernel(x_ref, prev_ref, next_ref, conv_ref, kkg_ref, ka_ref, rk_ref, w0_ref, a0_ref,
                      w2h_ref, w2l_ref, a2_ref, g2_ref, ones_ref,
                      r_ref, v_ref, kk_ref, lw_ref, bk_ref, kd_ref, bonus_ref, gg_ref, *, ctx_tiles, n_tiles):
    i = pl.program_id(1)
    x = x_ref[0].astype(F32)
    tm, W = x.shape
    C = RW_DIM
    first = (i == 0) | (i == ctx_tiles)
    last = (i == ctx_tiles - 1) | (i == n_tiles - 1)
    prev_row = jnp.where(first, 0.0, prev_ref[0, HALO - 1:HALO].astype(F32))
    next_row = jnp.where(last, 0.0, next_ref[0, 0:1].astype(F32))
    row = lax.broadcasted_iota(jnp.int32, (tm, W), 0)
    x_dn = jnp.where(row == 0, prev_row, pltpu.roll(x, 1, 0))
    x_up = jnp.where(row == tm - 1, next_row, pltpu.roll(x, tm - 1, 0))
    xc = x_dn * conv_ref[0:1] + x * conv_ref[1:2] + x_up * conv_ref[2:3]

    r, k, v = xc[:, :C], xc[:, C:2 * C], xc[:, 2 * C:3 * C]
    lora_w = jnp.tanh(xc[:, 3 * C:3 * C + LORA_W])
    lora_a = xc[:, 3 * C + LORA_W:3 * C + 2 * LORA_W]
    lg = xc[:, 3 * C + 2 * LORA_W:]
    ones_bd = ones_ref[...]
    kq = k * kkg_ref[...]
    kk = kq * lax.rsqrt(_head_sum(kq * kq, ones_bd) + 1e-12)
    r_ref[0], v_ref[0], kk_ref[0] = r, v, kk

    k_sum = jnp.zeros_like(k)
    for d in range(2):
        z = w0_ref[d:d + 1] + _dot3(lora_w, w2h_ref[d], w2l_ref[d])
        softplus_neg = jnp.maximum(-z, 0.0) + jnp.log(1.0 + jnp.exp(-jnp.abs(z)))
        lw_ref[d, 0] = -jnp.exp(-softplus_neg - 0.5)
        a = jax.nn.sigmoid(a0_ref[d:d + 1] + _dot(lora_a, a2_ref[d]))
        kd = k * (1.0 + (a - 1.0) * ka_ref[...])
        bk_ref[d, 0] = kk * a
        kd_ref[d, 0] = kd
        k_sum = k_sum + kd
    bonus_ref[0] = _head_sum(r * k_sum * rk_ref[...], ones_bd) * v
    gg_ref[0] = _dot(jax.nn.sigmoid(lg), g2_ref[...])


def _rwkv_prep(proj, p, n_ctx):
    B, Tt, _ = proj.shape
    C = RW_DIM
    nt = Tt // ROW_TILE
    hb = ROW_TILE // HALO
    pad_cols = lambda w: jnp.pad(w, ((0, 0), (0, PROJ_RW - w.shape[1])))
    w2 = jnp.stack([jnp.pad(p['decay_w2'][0], ((0, LORA_W - DECAY_LORA), (0, 0))),
                    jnp.pad(p['decay_w2'][1], ((DECAY_LORA, 0), (0, 0)))])
    w2h = w2.astype(BF16)
    w2l = (w2 - w2h.astype(F32)).astype(BF16)
    a2 = jnp.stack([jnp.pad(p['aicl_a2'][0], ((0, LORA_W - AICL_LORA), (0, 0))),
                    jnp.pad(p['aicl_a2'][1], ((AICL_LORA, 0), (0, 0)))]).astype(BF16)
    g2 = jnp.pad(p['gate_g2'], ((0, GATE_W - GATE_LORA), (0, 0))).astype(BF16)
    row = lambda b, i: (b, i, 0)
    drow = lambda b, i: (0, b, i, 0)
    const2 = lambda b, i: (0, 0)
    const3 = lambda b, i: (0, 0, 0)
    vec = pl.BlockSpec((1, C), const2)
    out_row = pl.BlockSpec((1, ROW_TILE, C), row)
    out_dir = pl.BlockSpec((2, 1, ROW_TILE, C), drow)
    sds = jax.ShapeDtypeStruct((B, Tt, C), F32)
    sds2 = jax.ShapeDtypeStruct((2, B, Tt, C), F32)
    return pl.pallas_call(
        functools.partial(_rwkv_prep_kernel, ctx_tiles=n_ctx // ROW_TILE, n_tiles=nt),
        grid=(B, nt),
        in_specs=[
            pl.BlockSpec((1, ROW_TILE, PROJ_RW), row),
            pl.BlockSpec((1, HALO, PROJ_RW), lambda b, i: (b, jnp.maximum(i * hb - 1, 0), 0)),
            pl.BlockSpec((1, HALO, PROJ_RW), lambda b, i: (b, jnp.minimum((i + 1) * hb, nt * hb - 1), 0)),
            pl.BlockSpec((3, PROJ_RW), const2),
            vec, vec, vec,
            pl.BlockSpec((2, C), const2),
            pl.BlockSpec((2, C), const2),
            pl.BlockSpec((2, LORA_W, C), const3),
            pl.BlockSpec((2, LORA_W, C), const3),
            pl.BlockSpec((2, LORA_W, C), const3),
            pl.BlockSpec((GATE_W, C), const2),
            pl.BlockSpec((C, C), const2),
        ],
        out_specs=[out_row, out_row, out_row, out_dir, out_dir, out_dir, out_row, out_row],
        out_shape=[sds, sds, sds, sds2, sds2, sds2, sds, sds],
        compiler_params=pltpu.CompilerParams(
            dimension_semantics=("arbitrary", "arbitrary"), vmem_limit_bytes=VMEM_LIMIT),
        name="rwkv_prep",
    )(proj, proj, proj, pad_cols(p['shift_conv']), p['k_k'].reshape(1, C), p['k_a'].reshape(1, C),
      p['r_k'].reshape(1, C), p['decay_w0'], p['aicl_a0'], w2h, w2l, a2, g2, _head_ones(C, RW_HEAD))


def _merge_kernel(att_ref, yf_ref, yb_ref, bonus_ref, gg_ref, gate_ref, x_ref, ga_ref, gnw_ref, gnb_ref, ones_ref,
                  wm_ref, wr_ref, wo_ref, o_ref):
    D = x_ref.shape[-1]
    ones_bd = ones_ref[...]
    y = yf_ref[0] + yb_ref[0]
    yc = y - _head_sum(y, ones_bd) * (1.0 / RW_HEAD)
    var = _head_sum(yc * yc, ones_bd) * (1.0 / RW_HEAD)
    y_n = yc * lax.rsqrt(var + GN_EPS) * gnw_ref[...] + gnb_ref[...]
    rw = (y_n + bonus_ref[0]) * gg_ref[0]
    g = jax.nn.sigmoid(gate_ref[0].astype(F32))
    mix = g[:, :D] * _dot(att_ref[0], wm_ref[...]) + g[:, D:] * _dot(rw, wr_ref[...])
    o_ref[0] = x_ref[0] + ga_ref[0] * _dot(mix, wo_ref[...])


def _merge(att, y_f, y_b, bonus, gg, proj, x, g_a, p, n_ctx):
    B, T, D = x.shape
    C = RW_DIM
    ctx_tiles = n_ctx // ROW_TILE
    gate_blk = PROJ_RW // PROJ_GATE
    row = lambda b, i: (b, i, 0)
    lat_row = lambda b, i: (b, i + ctx_tiles, 0)
    const = lambda b, i: (0, 0)
    return pl.pallas_call(
        _merge_kernel,
        grid=(B, T // ROW_TILE),
        in_specs=[
            pl.BlockSpec((1, ROW_TILE, att.shape[-1]), row),
            pl.BlockSpec((1, ROW_TILE, C), row),
            pl.BlockSpec((1, ROW_TILE, C), row),
            pl.BlockSpec((1, ROW_TILE, C), lat_row),
            pl.BlockSpec((1, ROW_TILE, C), lat_row),
            pl.BlockSpec((1, ROW_TILE, PROJ_GATE), lambda b, i: (b, i + ctx_tiles, gate_blk)),
            pl.BlockSpec((1, ROW_TILE, D), row),
            pl.BlockSpec((1, 1, D), lambda b, i: (b, 0, 0)),
            pl.BlockSpec((1, C), const),
            pl.BlockSpec((1, C), const),
            pl.BlockSpec((C, C), const),
            pl.BlockSpec(p['w_o_mla'].shape, const),
            pl.BlockSpec(p['w_o_rwkv'].shape, const),
            pl.BlockSpec(p['w_out'].shape, const),
        ],
        out_specs=pl.BlockSpec((1, ROW_TILE, D), row),
        out_shape=jax.ShapeDtypeStruct((B, T, D), F32),
        compiler_params=pltpu.CompilerParams(
            dimension_semantics=("arbitrary", "arbitrary"), vmem_limit_bytes=VMEM_LIMIT),
        name="merge",
    )(att, y_f, y_b, bonus, gg, proj, x, g_a[:, None, :], p['gn_w'].reshape(1, C), p['gn_b'].reshape(1, C),
      _head_ones(C, RW_HEAD), p['w_o_mla'].astype(BF16), p['w_o_rwkv'].astype(BF16), p['w_out'].astype(BF16))


TILE_ROWS = 8
MOE_BLOCK = 512
ZERO_ROWS = MOE_BLOCK + TILE_ROWS


NEG_INF = float("-inf")
HI_MASK = 0xFFFF0000


def _pack_pair(lo, hi):
    lo_b = lax.bitcast_convert_type(lo.astype(BF16).astype(F32), jnp.uint32)
    hi_b = lax.bitcast_convert_type(hi.astype(BF16).astype(F32), jnp.uint32)
    return (lo_b >> 16) | (hi_b & jnp.uint32(HI_MASK))


def _unpack_pair(w):
    lo = lax.bitcast_convert_type(w << 16, F32)
    hi = lax.bitcast_convert_type(w & jnp.uint32(HI_MASK), F32)
    return lo.astype(BF16), hi.astype(BF16)


def _row_max(x):
    return jnp.max(x, axis=-1, keepdims=True)


def _first_index_of(x, value, lane_f):
    return jnp.min(jnp.where(x == value, lane_f, float(x.shape[-1])), axis=-1, keepdims=True)


def _router_kernel(x_ref, gain_ref, sh_ref, sc_ref, wh_ref, wm_ref, wl_ref, bias_ref,
                   u_ref, ids_ref, gates_ref, ranks_ref, counts_ref, carry_ref):
    @pl.when((pl.program_id(0) == 0) & (pl.program_id(1) == 0))
    def _():
        carry_ref[...] = jnp.zeros_like(carry_ref)

    x = x_ref[0]
    tm, D = x.shape
    E = bias_ref.shape[-1]
    u = x * lax.rsqrt(jnp.mean(x * x, axis=-1, keepdims=True) + EPS) * gain_ref[...]
    u = u * (1.0 + sc_ref[0]) + sh_ref[0]
    u_ref[0] = _pack_pair(u[:, :D // 2], u[:, D // 2:])

    uh, um, ul = _split3(u)
    wh, wm, wl = wh_ref[...], wm_ref[...], wl_ref[...]
    d = lambda a, b: jnp.dot(a, b, preferred_element_type=F32)
    logits = d(uh, wh) + (d(uh, wm) + d(um, wh)) + (d(uh, wl) + d(um, wm) + d(ul, wh))
    scores = jax.nn.sigmoid(logits)
    sel = scores + bias_ref[...]

    lane_i = lax.broadcasted_iota(jnp.int32, (tm, E), 1)
    lane_f = lane_i.astype(F32)
    out_f = lax.broadcasted_iota(jnp.int32, (tm, LANES), 1).astype(F32)
    per_group = E // N_GROUPS
    grp_f = jnp.floor(lane_f * (1.0 / per_group))

    gs = jnp.full((tm, LANES), NEG_INF, F32)
    for g in range(N_GROUPS):
        sg = jnp.where(lane_i >= g * per_group, jnp.where(lane_i < (g + 1) * per_group, sel, NEG_INF), NEG_INF)
        m1 = _row_max(sg)
        i1 = _first_index_of(sg, m1, lane_f)
        m2 = _row_max(jnp.where(lane_f == i1, NEG_INF, sg))
        gs = jnp.where(out_f == g, m1 + m2, gs)

    allow = jnp.zeros((tm, E), F32)
    for _ in range(TOPK_GROUPS):
        m = _row_max(gs)
        i = _first_index_of(gs, m, out_f)
        gs = jnp.where(out_f == i, NEG_INF, gs)
        allow = jnp.where(grp_f == i, 1.0, allow)
    selm = jnp.where(allow > 0.0, sel, NEG_INF)

    ids = jnp.zeros((tm, LANES), F32)
    gts = jnp.zeros((tm, LANES), F32)
    member = jnp.zeros((tm, E), F32)
    idx_cols = []
    gsum = jnp.zeros((tm, 1), F32)
    for k in range(TOP_K):
        m = _row_max(selm)
        i = _first_index_of(selm, m, lane_f)
        hit = lane_f == i
        gk = jnp.sum(jnp.where(hit, scores, 0.0), axis=-1, keepdims=True)
        selm = jnp.where(hit, NEG_INF, selm)
        member = jnp.where(hit, 1.0, member)
        ids = jnp.where(out_f == k, i, ids)
        gts = jnp.where(out_f == k, gk, gts)
        idx_cols.append(i)
        gsum = gsum + gk
    gts = gts / gsum * ROUTED_SCALE

    r2 = lax.broadcasted_iota(jnp.int32, (tm, tm), 0)
    c2 = lax.broadcasted_iota(jnp.int32, (tm, tm), 1)
    before = jnp.where(r2 > c2, 1.0, 0.0).astype(BF16)
    mem_b = member.astype(BF16)
    carry = carry_ref[...]
    pos = carry + jnp.dot(before, mem_b, preferred_element_type=F32)
    rk = jnp.zeros((tm, LANES), F32)
    for k in range(TOP_K):
        rk = jnp.where(out_f == k, jnp.sum(jnp.where(lane_f == idx_cols[k], pos, 0.0), axis=-1, keepdims=True), rk)
    colsum = jnp.dot(jnp.ones((8, tm), BF16), mem_b, preferred_element_type=F32)[0:1]
    carry_ref[...] = carry + colsum
    counts_ref[...] = carry + colsum
    ids_ref[0] = ids.astype(jnp.int32)
    ranks_ref[0] = rk.astype(jnp.int32)
    gates_ref[0] = gts


def _router(x1, sh_m, sc_m, p):
    B, T, D = x1.shape
    E = N_EXPERTS
    wh, wm, wl = _split3(p['router_w'])
    row = lambda b, i: (b, i, 0)
    const = lambda b, i: (0, 0)
    vec = lambda b, i: (b, 0, 0)
    lane_out = lambda dt: jax.ShapeDtypeStruct((B, T, LANES), dt)
    return pl.pallas_call(
        _router_kernel,
        grid=(B, T // ROW_TILE),
        in_specs=[
            pl.BlockSpec((1, ROW_TILE, D), row),
            pl.BlockSpec((1, D), const),
            pl.BlockSpec((1, 1, D), vec),
            pl.BlockSpec((1, 1, D), vec),
            pl.BlockSpec((D, E), const),
            pl.BlockSpec((D, E), const),
            pl.BlockSpec((D, E), const),
            pl.BlockSpec((1, E), const),
        ],
        out_specs=[
            pl.BlockSpec((1, ROW_TILE, D // 2), row),
            pl.BlockSpec((1, ROW_TILE, LANES), row),
            pl.BlockSpec((1, ROW_TILE, LANES), row),
            pl.BlockSpec((1, ROW_TILE, LANES), row),
            pl.BlockSpec((1, E), const),
        ],
        out_shape=[
            jax.ShapeDtypeStruct((B, T, D // 2), jnp.uint32),
            lane_out(jnp.int32), lane_out(F32), lane_out(jnp.int32),
            jax.ShapeDtypeStruct((1, E), F32),
        ],
        scratch_shapes=[pltpu.VMEM((1, E), F32)],
        compiler_params=pltpu.CompilerParams(
            dimension_semantics=("arbitrary", "arbitrary"), vmem_limit_bytes=VMEM_LIMIT),
        name="router",
    )(x1, p['norm_ffn'].reshape(1, D), sh_m[:, None, :], sc_m[:, None, :], wh, wm, wl,
      p['router_bias'].reshape(1, E))


def _slot_kernel(ids_ref, ranks_ref, base_ref, o_ref):
    ids = ids_ref[...].astype(F32)
    tm = ids.shape[0]
    E = base_ref.shape[-1]
    lane_e = lax.broadcasted_iota(jnp.int32, (tm, E), 1).astype(F32)
    out_lane = lax.broadcasted_iota(jnp.int32, (tm, LANES), 1)
    first = jnp.zeros((tm, LANES), F32)
    for k in range(TOP_K):
        fk = jnp.sum(jnp.where(lane_e == ids[:, k:k + 1], base_ref[...], 0.0), axis=-1, keepdims=True)
        first = jnp.where(out_lane == k, fk, first)
    o_ref[...] = first.astype(jnp.int32) + ranks_ref[...]


def _slots(ids, ranks, base):
    n_tok = ids.shape[0]
    E = base.shape[-1]
    row = pl.BlockSpec((ROW_TILE, LANES), lambda i: (i, 0))
    return pl.pallas_call(
        _slot_kernel,
        grid=(n_tok // ROW_TILE,),
        in_specs=[row, row, pl.BlockSpec((1, E), lambda i: (0, 0))],
        out_specs=row,
        out_shape=jax.ShapeDtypeStruct((n_tok, LANES), jnp.int32),
        compiler_params=pltpu.CompilerParams(dimension_semantics=("arbitrary",)),
        name="slots",
    )(ids, ranks, base)


DMA_PRIORITIES = 2


def _row_copy(src, src_row, dst, dst_row, sem):
    return pltpu.make_async_copy(src.at[pl.ds(src_row, 1)], dst.at[pl.ds(dst_row, 1)], sem)


def _dispatch_kernel(pad_ref, dest_ref, u_ref, slots_ref, zero_ref, sem):
    tm = u_ref.shape[0]
    n_experts = pad_ref.shape[0]

    @pl.when(pl.program_id(0) == 0)
    def _():
        zero_ref[...] = jnp.zeros_like(zero_ref)

        def fill(e, carry):
            start = pl.multiple_of(pad_ref[e], TILE_ROWS)
            pltpu.make_async_copy(zero_ref, slots_ref.at[pl.ds(start, ZERO_ROWS)], sem).start()
            return carry

        lax.fori_loop(0, n_experts, fill, 0)

        def filled(e, carry):
            pltpu.make_async_copy(zero_ref, slots_ref.at[pl.ds(0, ZERO_ROWS)], sem).wait()
            return carry

        lax.fori_loop(0, n_experts, filled, 0)

    def issue(r, carry):
        for k in range(TOP_K):
            _row_copy(u_ref, r, slots_ref, dest_ref[0, 0, r * TOP_K + k], sem).start(priority=k % DMA_PRIORITIES)
        return carry

    lax.fori_loop(0, tm, issue, 0)
    for k in range(TOP_K):
        pltpu.make_async_copy(u_ref, slots_ref.at[pl.ds(0, tm)], sem).wait()


def _dispatch(pad_start, dest, u_rows, n_slots):
    n_tok, W = u_rows.shape
    nt = n_tok // ROW_TILE
    grid_spec = pltpu.PrefetchScalarGridSpec(
        num_scalar_prefetch=1,
        grid=(nt,),
        in_specs=[
            pl.BlockSpec((1, 1, ROW_TILE * TOP_K), lambda i, pad: (i, 0, 0), memory_space=pltpu.SMEM),
            pl.BlockSpec((ROW_TILE, W), lambda i, pad: (i, 0)),
        ],
        out_specs=pl.BlockSpec(memory_space=pl.ANY),
        scratch_shapes=[pltpu.VMEM((ZERO_ROWS, W), jnp.uint32), pltpu.SemaphoreType.DMA(())],
    )
    return pl.pallas_call(
        _dispatch_kernel,
        grid_spec=grid_spec,
        out_shape=jax.ShapeDtypeStruct((n_slots + ZERO_ROWS, W), jnp.uint32),
        compiler_params=pltpu.CompilerParams(
            dimension_semantics=("arbitrary",), vmem_limit_bytes=VMEM_LIMIT),
        name="dispatch",
    )(pad_start, dest, u_rows)


WEIGHT_SLOTS = 2


def _expert_weight_copies(hbm_refs, stage_refs, expert, slot, sems):
    return [pltpu.make_async_copy(w.at[expert], s.at[slot], sems.at[slot, i])
            for i, (w, s) in enumerate(zip(hbm_refs, stage_refs))]


def _moe_ffn_kernel(be_ref, first_ref, slot_ref, next_ref, nu_ref, x_ref, w1_ref, w3_ref, w2_ref, o_ref,
                    s1_ref, s3_ref, s2_ref, w13_ref, w2b_ref, sems):
    j = pl.program_id(0)
    F = w1_ref.shape[-1]
    half = x_ref.shape[-1]
    copies = functools.partial(_expert_weight_copies, (w1_ref, w3_ref, w2_ref), (s1_ref, s3_ref, s2_ref), sems=sems)

    @pl.when(j == 0)
    def _():
        for c in copies(be_ref[0], 0):
            c.start()

    @pl.when(first_ref[j] == 1)
    def _():
        slot = slot_ref[j]
        for c in copies(be_ref[j], slot):
            c.wait()

        @pl.when(next_ref[j] >= 0)
        def _():
            for c in copies(next_ref[j], 1 - slot):
                c.start()

        w13_ref[:, :F] = s1_ref[slot].astype(BF16)
        w13_ref[:, F:] = s3_ref[slot].astype(BF16)
        w2b_ref[...] = s2_ref[slot].astype(BF16)

    @pl.when(j < nu_ref[0])
    def _():
        lo, hi = _unpack_pair(x_ref[...])
        h = (jnp.dot(lo, w13_ref[:half], preferred_element_type=F32)
             + jnp.dot(hi, w13_ref[half:], preferred_element_type=F32))
        h1, h3 = h[:, :F], h[:, F:]
        y = _dot(h1 * jax.nn.sigmoid(h1) * h3, w2b_ref[...])
        o_ref[...] = pltpu.einshape("s(cl)->(sc)l", y, c=TILE_ROWS)

    @pl.when(j >= nu_ref[0])
    def _():
        o_ref[...] = jnp.zeros_like(o_ref)


def _moe_ffn(block_e, n_used, x_slots, w1, w3, w2):
    W = x_slots.shape[1]
    E, D, F = w1.shape
    nblk = block_e.shape[0]
    slots = nblk * MOE_BLOCK
    idx = jnp.arange(nblk)
    first = jnp.concatenate([jnp.ones((1,), bool), block_e[1:] != block_e[:-1]])
    slot = ((jnp.cumsum(first) - 1) % WEIGHT_SLOTS).astype(jnp.int32)
    later_first = lax.cummin(jnp.where(first, idx, nblk)[::-1])[::-1]
    next_first = jnp.concatenate([later_first[1:], jnp.full((1,), nblk)])
    next_e = jnp.where(next_first < nblk, block_e[jnp.minimum(next_first, nblk - 1)], -1).astype(jnp.int32)
    n_prefetch = 5
    blk = lambda j, *_: (jnp.minimum(j, _[4][0] - 1), 0)
    grid_spec = pltpu.PrefetchScalarGridSpec(
        num_scalar_prefetch=n_prefetch,
        grid=(nblk,),
        in_specs=[
            pl.BlockSpec((MOE_BLOCK, W), blk),
            pl.BlockSpec(memory_space=pl.ANY),
            pl.BlockSpec(memory_space=pl.ANY),
            pl.BlockSpec(memory_space=pl.ANY),
        ],
        out_specs=pl.BlockSpec((MOE_BLOCK * TILE_ROWS, LANES), lambda j, *_: (j, 0)),
        scratch_shapes=[pltpu.VMEM((WEIGHT_SLOTS, D, F), F32), pltpu.VMEM((WEIGHT_SLOTS, D, F), F32),
                        pltpu.VMEM((WEIGHT_SLOTS, F, D), F32),
                        pltpu.VMEM((D, 2 * F), BF16), pltpu.VMEM((F, D), BF16),
                        pltpu.SemaphoreType.DMA((WEIGHT_SLOTS, 3))],
    )
    return pl.pallas_call(
        _moe_ffn_kernel,
        grid_spec=grid_spec,
        out_shape=jax.ShapeDtypeStruct((slots * TILE_ROWS, LANES), F32),
        compiler_params=pltpu.CompilerParams(
            dimension_semantics=("arbitrary",), vmem_limit_bytes=VMEM_LIMIT),
        name="moe_ffn",
    )(block_e, first.astype(jnp.int32), slot, next_e, n_used, x_slots, w1, w3, w2)


COMBINE_ROWS = 32


def _combine_kernel(dest_ref, gates_ref, u_ref, x_ref, gm_ref, w1_ref, w3_ref, w2_ref, ys_ref, o_ref, buf_ref, sem):
    tm, half = u_ref.shape

    def tile(i):
        return pl.ds(pl.multiple_of(i * TILE_ROWS, TILE_ROWS), TILE_ROWS)

    def issue(r, carry):
        for k in range(TOP_K):
            pltpu.make_async_copy(ys_ref.at[tile(dest_ref[0, 0, r * TOP_K + k])], buf_ref.at[k, tile(r)], sem).start(
                priority=k % DMA_PRIORITIES)
        return carry

    lax.fori_loop(0, tm, issue, 0)

    ulo, uhi = _unpack_pair(u_ref[...])
    both = lambda w_ref: (jnp.dot(ulo, w_ref[:half], preferred_element_type=F32)
                          + jnp.dot(uhi, w_ref[half:], preferred_element_type=F32))
    h1, h3 = both(w1_ref), both(w3_ref)
    o_ref[...] = x_ref[...] + gm_ref[0] * _dot(h1 * jax.nn.sigmoid(h1) * h3, w2_ref[...])

    for k in range(TOP_K):
        pltpu.make_async_copy(ys_ref.at[pl.ds(0, tm * TILE_ROWS)], buf_ref.at[k], sem).wait()

    gm = gm_ref[0]
    for rb in range(tm // COMBINE_ROWS):
        rows = slice(rb * COMBINE_ROWS, (rb + 1) * COMBINE_ROWS)
        gates = gates_ref[rows, :]
        for c in range(TILE_ROWS):
            acc = jnp.zeros((COMBINE_ROWS, LANES), F32)
            for k in range(TOP_K):
                acc = acc + gates[:, k:k + 1] * buf_ref[
                    k, pl.ds(rb * COMBINE_ROWS * TILE_ROWS + c, COMBINE_ROWS, stride=TILE_ROWS), :]
            cols = slice(c * LANES, (c + 1) * LANES)
            o_ref[rows, cols] = o_ref[rows, cols] + gm[:, cols] * acc


def _combine(dest, gates, u_rows, x1, g_m, y_slots, p):
    B, T, D = x1.shape
    n_tok = B * T
    W = u_rows.shape[1]
    tiles_per_batch = T // ROW_TILE
    row = lambda i: (i, 0)
    const = lambda i: (0, 0)
    out = pl.pallas_call(
        _combine_kernel,
        grid=(n_tok // ROW_TILE,),
        in_specs=[
            pl.BlockSpec((1, 1, ROW_TILE * TOP_K), lambda i: (i, 0, 0), memory_space=pltpu.SMEM),
            pl.BlockSpec((ROW_TILE, LANES), row),
            pl.BlockSpec((ROW_TILE, W), row),
            pl.BlockSpec((ROW_TILE, D), row),
            pl.BlockSpec((1, 1, D), lambda i: (i // tiles_per_batch, 0, 0)),
            pl.BlockSpec(p['shared_w1'].shape, const),
            pl.BlockSpec(p['shared_w3'].shape, const),
            pl.BlockSpec(p['shared_w2'].shape, const),
            pl.BlockSpec(memory_space=pl.ANY),
        ],
        out_specs=pl.BlockSpec((ROW_TILE, D), row),
        out_shape=jax.ShapeDtypeStruct((n_tok, D), F32),
        scratch_shapes=[pltpu.VMEM((TOP_K, ROW_TILE * TILE_ROWS, LANES), F32), pltpu.SemaphoreType.DMA(())],
        compiler_params=pltpu.CompilerParams(
            dimension_semantics=("arbitrary",), vmem_limit_bytes=VMEM_LIMIT),
        name="combine",
    )(dest, gates.reshape(n_tok, LANES), u_rows, x1.reshape(n_tok, D), g_m[:, None, :],
      p['shared_w1'].astype(BF16), p['shared_w3'].astype(BF16), p['shared_w2'].astype(BF16), y_slots)
    return out.reshape(B, T, D)


def _moe(x1, sh_m, sc_m, g_m, p):
    B, T, D = x1.shape
    n_tok = B * T
    u_packed, ids, gates, ranks, counts = _router(x1, sh_m, sc_m, p)

    counts = counts[0].astype(jnp.int32)
    padded = (counts + MOE_BLOCK - 1) // MOE_BLOCK * MOE_BLOCK
    padded_end = jnp.cumsum(padded)
    base = padded_end - padded
    n_blocks = n_tok * TOP_K // MOE_BLOCK + N_EXPERTS
    n_used = (padded_end[-1] // MOE_BLOCK).astype(jnp.int32)
    blk = jnp.minimum(jnp.arange(n_blocks), n_used - 1) * MOE_BLOCK
    block_e = jnp.minimum(jnp.sum(padded_end[None, :] <= blk[:, None], axis=1), N_EXPERTS - 1).astype(jnp.int32)
    dest = _slots(ids.reshape(n_tok, LANES), ranks.reshape(n_tok, LANES), base.astype(F32).reshape(1, N_EXPERTS))
    dest = dest[:, :TOP_K].reshape(n_tok // ROW_TILE, 1, ROW_TILE * TOP_K)

    u_rows = u_packed.reshape(n_tok, D // 2)
    pad_start = ((base + counts) // TILE_ROWS * TILE_ROWS).astype(jnp.int32)
    x_slots = _dispatch(pad_start, dest, u_rows, n_blocks * MOE_BLOCK)
    y_slots = _moe_ffn(block_e, n_used.reshape(1), x_slots, p['expert_w1'], p['expert_w3'], p['expert_w2'])
    return _combine(dest, gates, u_rows, x1, g_m, y_slots, p)
```

```python
import functools

import jax
import jax.numpy as jnp
import numpy as np
from jax import lax
from jax.experimental import pallas as pl
from jax.experimental.pallas import tpu as pltpu

F32 = jnp.float32
BF16 = jnp.bfloat16

GRID_W = 64
EPS = 1e-6
MLA_HEADS = 8
QK_NOPE = 64
QK_ROPE = 32
QK_HEAD = QK_NOPE + QK_ROPE
V_HEAD = 64
Q_LORA = 256
KV_LORA = 128
ROPE_BASE = 10000.0
ATTN_SCALE = QK_HEAD ** -0.5
RW_HEADS = 8
RW_HEAD = 64
RW_DIM = RW_HEADS * RW_HEAD
DECAY_LORA = 64
AICL_LORA = 64
GATE_LORA = 160
GN_EPS = 64e-5
N_EXPERTS = 256
TOP_K = 8
N_GROUPS = 8
TOPK_GROUPS = 4
ROUTED_SCALE = 2.5

LANES = 128
MXU_DIM = 256
VMEM_LIMIT = 56 * 1024 * 1024

WKV_CHUNK = 64
WKV_GROUP = 128
HEADS_PER_GROUP = WKV_GROUP // RW_HEAD
HEAD_PAD = LANES


def _dot(a, b):
    return jnp.dot(a.astype(BF16), b.astype(BF16), preferred_element_type=F32)


def _dot_t(a, b):
    return lax.dot_general(a.astype(BF16), b.astype(BF16), (((1,), (1,)), ((), ())),
                           preferred_element_type=F32)


def _split3(x):
    h = x.astype(BF16)
    r1 = x - h.astype(F32)
    m = r1.astype(BF16)
    lo = (r1 - m.astype(F32)).astype(BF16)
    return h, m, lo


def _dot_hi(a_bf16_exact, x):
    h, m, lo = _split3(x)
    d = lambda y: jnp.dot(a_bf16_exact, y, preferred_element_type=F32)
    return d(h) + d(m) + d(lo)


def _wkv_chunk(r, v, kk, lw, bk, kd, s0, reverse, emit):
    L, G = r.shape
    row = lax.broadcasted_iota(jnp.int32, (L, G), 0)
    lane = lax.broadcasted_iota(jnp.int32, (L, G), 1)
    diff = (lane % L - row) if reverse else (row - lane % L)
    strict = diff > 0
    incl = diff >= 0
    r2 = lax.broadcasted_iota(jnp.int32, (L, L), 0)
    c2 = lax.broadcasted_iota(jnp.int32, (L, L), 1)
    tri = jnp.where(((c2 - r2) if reverse else (r2 - c2)) >= 0, 1.0, 0.0).astype(BF16)
    tri_ones = jnp.concatenate([tri, jnp.ones((L, L), BF16)], axis=0)
    lane_head = lane // RW_HEAD
    rowb = lax.broadcasted_iota(jnp.int32, (G, G), 0) // RW_HEAD
    colb = lax.broadcasted_iota(jnp.int32, (G, G), 1) // RW_HEAD

    def stack(x):
        return jnp.concatenate(
            [jnp.where(lane_head == h, x, 0.0) for h in range(HEADS_PER_GROUP)], axis=0).astype(BF16)

    sums = _dot_hi(tri_ones, lw)
    cum_in = sums[:L]
    g_in = jnp.exp(cum_in)
    g_inv = jnp.exp(-cum_in)
    g_ex = jnp.exp(cum_in - lw)
    g_tot = jnp.exp(sums[L:])
    a_h = -kk * g_ex
    b_h = bk * g_inv
    k_h = kd * g_inv
    r_h = r * g_in
    yield

    gram = _dot_t(jnp.concatenate([a_h, r_h], axis=0),
                  jnp.concatenate([stack(b_h), stack(k_h)], axis=0))
    ab = jnp.where(strict, gram[:L, :G], 0.0)
    ak = jnp.where(strict, gram[:L, G:], 0.0)
    rb = jnp.where(incl, gram[L:, :G], 0.0)
    rk = jnp.where(incl, gram[L:, G:], 0.0)
    yield

    tm = jnp.where(diff == 0, 1.0, 0.0) + ab
    p = ab
    v_st = stack(v)
    akv = _dot(ak, v_st)
    for _ in range(int(np.log2(L)) - 1):
        p = _dot(p, stack(p))
        yield
        tm = tm + _dot(tm, stack(p))
        yield

    wu = _dot(tm, jnp.concatenate([stack(a_h), stack(akv)], axis=1))
    w_t, u_t = wu[:, :G], wu[:, G:]
    yield
    rbwu = _dot(rb, jnp.concatenate([stack(w_t), stack(u_t)], axis=1))
    r_t = r_h + rbwu[:, :G]
    y_t = rbwu[:, G:] + _dot(rk, v_st)
    yield

    s0_st = stack(s0)
    y = _dot_t(r_t, s0_st) + y_t
    u = _dot_t(w_t, s0_st) + u_t
    yield
    uv = jnp.concatenate([u, v], axis=0)
    bkc = jnp.concatenate([b_h, k_h], axis=0)
    upd = lax.dot_general(uv.astype(BF16), bkc.astype(BF16), (((0,), (0,)), ((), ())),
                          preferred_element_type=F32)
    upd = jnp.where(rowb == colb, upd, 0.0)
    upd_d = upd[0:L]
    for h in range(1, HEADS_PER_GROUP):
        upd_d = upd_d + upd[h * L:(h + 1) * L]
    emit(y, (s0 + upd_d) * g_tot)


def _wkv_kernel(*refs, n_groups, n_batch):
    ins, (yf_ref, yb_ref, s_ref) = refs[:12], refs[12:]
    G = WKV_GROUP

    @pl.when(pl.program_id(1) == 0)
    def _():
        s_ref[...] = jnp.zeros_like(s_ref)

    chains = []
    for d, y_ref in enumerate((yf_ref, yb_ref)):
        r_ref, v_ref, kk_ref, lw_ref, bk_ref, kd_ref = ins[6 * d:6 * d + 6]
        for bi in range(n_batch):
            for g in range(n_groups):
                sl = slice(g * G, (g + 1) * G)

                def emit(y, s_new, y_ref=y_ref, d=d, bi=bi, g=g, sl=sl):
                    y_ref[bi, :, sl] = y
                    s_ref[d, bi, g] = s_new

                chains.append(_wkv_chunk(
                    r_ref[bi, :, sl], v_ref[bi, :, sl], kk_ref[bi, :, sl], lw_ref[0, bi, :, sl],
                    bk_ref[0, bi, :, sl], kd_ref[0, bi, :, sl], s_ref[d, bi, g], d == 1, emit))
    while chains:
        chains = [c for c in chains if next(c, StopIteration) is not StopIteration]


WKV_BATCH = 2


def _wkv_scan(r, v, kk, lw, bk, kd, n_ctx):
    B, Ttot, C = r.shape
    L = WKV_CHUNK
    nc = Ttot // L
    ncc = n_ctx // L
    nl = nc - ncc
    n_groups = C // WKV_GROUP
    nb = WKV_BATCH

    cid = (lambda s: s, lambda s: jnp.where(s < ncc, ncc - 1 - s, nc + ncc - 1 - s))
    first_out = (0, nl - 1)
    in_specs, out_specs = [], []
    for d in range(2):
        shared = pl.BlockSpec((nb, L, C), lambda b, s, d=d: (b, cid[d](s), 0))
        per_dir = pl.BlockSpec((1, nb, L, C), lambda b, s, d=d: (d, b, cid[d](s), 0))
        in_specs += [shared, shared, shared, per_dir, per_dir, per_dir]
        out_specs.append(pl.BlockSpec(
            (nb, L, C), lambda b, s, d=d: (b, jnp.where(s < ncc, first_out[d], cid[d](s) - ncc), 0)))
    out = jax.ShapeDtypeStruct((B, nl * L, C), F32)
    return pl.pallas_call(
        functools.partial(_wkv_kernel, n_groups=n_groups, n_batch=nb),
        grid=(B // nb, nc),
        in_specs=in_specs,
        out_specs=out_specs,
        out_shape=[out, out],
        scratch_shapes=[pltpu.VMEM((2, nb, n_groups, L, WKV_GROUP), F32)],
        compiler_params=pltpu.CompilerParams(
            dimension_semantics=("arbitrary", "arbitrary"), vmem_limit_bytes=VMEM_LIMIT),
        name="wkv_scan",
    )(r, v, kk, lw, bk, kd, r, v, kk, lw, bk, kd)


def kernel(x, c, ctx, c_ctx, ada_w, ada_b, norm_mix, norm_ffn, w_in, shift_conv, q_lat_norm, w_uq, kv_lat_norm, w_ukv, q_norm, k_norm, w_o_mla, decay_w0, decay_w2, aicl_a0, aicl_a2, k_k, k_a, r_k, gn_w, gn_b, gate_g2, w_o_rwkv, w_out, router_w, router_bias, expert_w1, expert_w3, expert_w2, shared_w1, shared_w3, shared_w2):
    B, T, D = x.shape
    n_ctx = ctx.shape[1]
    i0 = 0
    p = dict(norm_mix=norm_mix[i0], norm_ffn=norm_ffn[i0], w_in=w_in[i0], shift_conv=shift_conv[i0],
             q_lat_norm=q_lat_norm[i0], w_uq=w_uq[i0], kv_lat_norm=kv_lat_norm[i0], w_ukv=w_ukv[i0],
             q_norm=q_norm[i0], k_norm=k_norm[i0], w_o_mla=w_o_mla[i0],
             decay_w0=decay_w0[i0], decay_w2=decay_w2[i0], aicl_a0=aicl_a0[i0], aicl_a2=aicl_a2[i0],
             k_k=k_k[i0], k_a=k_a[i0], r_k=r_k[i0], gn_w=gn_w[i0], gn_b=gn_b[i0], gate_g2=gate_g2[i0],
             w_o_rwkv=w_o_rwkv[i0], w_out=w_out[i0], router_w=router_w[i0], router_bias=router_bias[i0],
             expert_w1=expert_w1[i0], expert_w3=expert_w3[i0], expert_w2=expert_w2[i0],
             shared_w1=shared_w1[i0], shared_w3=shared_w3[i0], shared_w2=shared_w2[i0])

    mod = _ada_modulation(c, c_ctx, ada_w[i0], ada_b[i0])
    sh_a, sc_a, g_a, sh_m, sc_m, g_m = jnp.split(mod[:B], 6, axis=-1)
    csh_a, csc_a = jnp.split(mod[B], 6, axis=-1)[:2]
    sh2 = jnp.stack([jnp.broadcast_to(csh_a, (B, D)), sh_a], axis=1)
    sc2 = jnp.stack([jnp.broadcast_to(csc_a, (B, D)), sc_a], axis=1)

    proj = _in_proj(ctx, x, sh2, sc2, p['norm_mix'], _pack_w_in(p['w_in']))

    q, k, v = _mla_prep(proj, p, n_ctx, T)
    att = _attention(q, k, v)

    r, vv, kk, lw, bk, kd, bonus, gg = _rwkv_prep(proj, p, n_ctx)
    y_f, y_b = _wkv_scan(r, vv, kk, lw, bk, kd, n_ctx)

    x1 = _merge(att, y_f, y_b, bonus, gg, proj, x, g_a, p, n_ctx)
    return _moe(x1, sh_m, sc_m, g_m, p)


SUBLANES = 8


def _ada_kernel(c_ref, w_ref, b_ref, o_ref):
    cc = c_ref[...]
    w = w_ref[...]
    w_hi = w.astype(BF16)
    w_lo = (w - w_hi.astype(F32)).astype(BF16)
    o_ref[...] = _dot3(cc * jax.nn.sigmoid(cc), w_hi, w_lo) + b_ref[...]


def _ada_modulation(c, c_ctx, ada_w, ada_b):
    B, D = c.shape
    n_out = ada_w.shape[1]
    rows = -(-(B + 1) // SUBLANES) * SUBLANES
    c_all = jnp.concatenate([c, c_ctx[None, :], jnp.zeros((rows - B - 1, D), F32)], axis=0)
    out = pl.pallas_call(
        _ada_kernel,
        grid=(n_out // D,),
        in_specs=[pl.BlockSpec((rows, D), lambda j: (0, 0)),
                  pl.BlockSpec((D, D), lambda j: (0, j)),
                  pl.BlockSpec((1, D), lambda j: (0, j))],
        out_specs=pl.BlockSpec((rows, D), lambda j: (0, j)),
        out_shape=jax.ShapeDtypeStruct((rows, n_out), F32),
        compiler_params=pltpu.CompilerParams(dimension_semantics=("arbitrary",), vmem_limit_bytes=VMEM_LIMIT),
        name="ada_modulation",
    )(c_all, ada_w, ada_b.reshape(1, n_out))
    return out[:B + 1]


ROW_TILE = 256
PROJ_RW = 2048
PROJ_GATE = 2048
PROJ_MLA = 512
PROJ_W = PROJ_RW + PROJ_GATE + PROJ_MLA
MLA_IN = Q_LORA + KV_LORA + QK_ROPE
RW_SPLITS = (RW_DIM, RW_DIM, RW_DIM, DECAY_LORA, DECAY_LORA, AICL_LORA, AICL_LORA, GATE_LORA)
RW_IN = sum(RW_SPLITS)


def _pack_w_in(w_in):
    w_mla = w_in[:, :MLA_IN]
    w_rw = w_in[:, MLA_IN:MLA_IN + RW_IN]
    w_gate = w_in[:, MLA_IN + RW_IN:]
    pad = lambda w, n: jnp.pad(w, ((0, 0), (0, n - w.shape[1])))
    return jnp.concatenate([pad(w_rw, PROJ_RW), w_gate, pad(w_mla, PROJ_MLA)], axis=1).astype(BF16)


def _in_proj_kernel(ctx_ref, x_ref, sh_ref, sc_ref, gain_ref, w_ref, o_ref):
    is_ctx = pl.program_id(1) == 0
    xt = jnp.where(is_ctx, ctx_ref[0], x_ref[0])
    sh = jnp.where(is_ctx, sh_ref[0, 0:1], sh_ref[0, 1:2])
    sc = jnp.where(is_ctx, sc_ref[0, 0:1], sc_ref[0, 1:2])
    y = xt * lax.rsqrt(jnp.mean(xt * xt, axis=-1, keepdims=True) + EPS) * gain_ref[...]
    h = y * (1.0 + sc) + sh
    o_ref[0] = _dot(h, w_ref[...]).astype(o_ref.dtype)


def _in_proj(ctx, x, sh2, sc2, gain, w):
    B, T, D = x.shape
    n_ctx = ctx.shape[1]
    assert n_ctx == ROW_TILE and T % ROW_TILE == 0
    nt = 1 + T // ROW_TILE
    return pl.pallas_call(
        _in_proj_kernel,
        grid=(B, nt),
        in_specs=[
            pl.BlockSpec((1, ROW_TILE, D), lambda b, i: (b, 0, 0)),
            pl.BlockSpec((1, ROW_TILE, D), lambda b, i: (b, jnp.maximum(i - 1, 0), 0)),
            pl.BlockSpec((1, 2, D), lambda b, i: (b, 0, 0)),
            pl.BlockSpec((1, 2, D), lambda b, i: (b, 0, 0)),
            pl.BlockSpec((1, D), lambda b, i: (0, 0)),
            pl.BlockSpec((D, PROJ_W), lambda b, i: (0, 0)),
        ],
        out_specs=pl.BlockSpec((1, ROW_TILE, PROJ_W), lambda b, i: (b, i, 0)),
        out_shape=jax.ShapeDtypeStruct((B, n_ctx + T, PROJ_W), BF16),
        compiler_params=pltpu.CompilerParams(
            dimension_semantics=("arbitrary", "arbitrary"), vmem_limit_bytes=VMEM_LIMIT),
        name="in_proj",
    )(ctx, x, sh2, sc2, gain.reshape(1, D), w)


def _rms(x, gain):
    return x * lax.rsqrt(jnp.mean(x * x, axis=-1, keepdims=True) + EPS) * gain


def _rope_tables(n_tokens):
    rows = n_tokens // GRID_W
    row = jnp.repeat(jnp.arange(rows, dtype=F32), GRID_W)
    col = jnp.tile(jnp.arange(GRID_W, dtype=F32), rows)
    n_freq = QK_ROPE // 4
    inv_freq = ROPE_BASE ** (-jnp.arange(n_freq, dtype=F32) / n_freq)
    ang = jnp.concatenate([row[:, None] * inv_freq, col[:, None] * inv_freq], axis=-1)
    return jnp.cos(ang), jnp.sin(ang)


ROPE_HALF = QK_ROPE // 2
X1 = slice(QK_NOPE, QK_NOPE + ROPE_HALF)
X2 = slice(QK_NOPE + ROPE_HALF, QK_HEAD)


def _rot_cols(w):
    w3 = w.reshape(w.shape[0], MLA_HEADS, HEAD_PAD)
    rot = jnp.zeros_like(w3).at[:, :, X1].set(-w3[:, :, X2]).at[:, :, X2].set(w3[:, :, X1])
    return rot.reshape(w.shape)


def _swap_halves(g):
    return jnp.zeros_like(g).at[:, X1].set(g[:, X2]).at[:, X2].set(g[:, X1])


def _mla_weights(p):
    H = MLA_HEADS
    wq = jnp.pad(p['w_uq'].reshape(Q_LORA, H, QK_HEAD), ((0, 0), (0, 0), (0, HEAD_PAD - QK_HEAD)))
    wq = wq.reshape(Q_LORA, H * HEAD_PAD)
    wkv = p['w_ukv'].reshape(KV_LORA, H, QK_NOPE + V_HEAD)
    wk_lat = jnp.pad(wkv[:, :, :QK_NOPE], ((0, 0), (0, 0), (0, HEAD_PAD - QK_NOPE)))
    place = jnp.zeros((LANES, H, HEAD_PAD), F32).at[:QK_ROPE, :, QK_NOPE:QK_HEAD].set(
        jnp.broadcast_to(jnp.eye(QK_ROPE, dtype=F32)[:, None, :], (QK_ROPE, H, QK_ROPE)))
    wk = jnp.concatenate([wk_lat, place], axis=0).reshape(KV_LORA + LANES, H * HEAD_PAD)
    wv = wkv[:, :, QK_NOPE:].reshape(KV_LORA, H * V_HEAD)
    gq = jnp.pad(p['q_norm'], (0, HEAD_PAD - QK_HEAD)).reshape(1, HEAD_PAD)
    gk = jnp.pad(p['k_norm'], (0, HEAD_PAD - QK_HEAD)).reshape(1, HEAD_PAD)
    bf = lambda w: w.astype(BF16)
    return (bf(wq), bf(_rot_cols(wq)), bf(wk), bf(_rot_cols(wk)), bf(wv), gq, _swap_halves(gq), gk, _swap_halves(gk))


def _mla_tables(n_ctx, T):
    cos, sin = _rope_tables(T)
    c = jnp.ones((n_ctx + T, HEAD_PAD), F32).at[n_ctx:, X1].set(cos).at[n_ctx:, X2].set(cos)
    s = jnp.zeros((n_ctx + T, HEAD_PAD), F32).at[n_ctx:, X1].set(sin).at[n_ctx:, X2].set(sin)
    return c, s


def _mla_prep_kernel(x_ref, c_ref, s_ref, qlg_ref, kvg_ref, wq_ref, wqr_ref, wk_ref, wkr_ref, wv_ref,
                     gq_ref, gqp_ref, gk_ref, gkp_ref, q_ref, k_ref, v_ref):
    x = x_ref[0].astype(F32)
    ql = _rms(x[:, :Q_LORA], qlg_ref[...])
    kvl = _rms(x[:, Q_LORA:Q_LORA + KV_LORA], kvg_ref[...])
    k_in = jnp.concatenate([kvl, x[:, Q_LORA + KV_LORA:]], axis=1)
    cos, sin = c_ref[...], s_ref[...]

    def finish(raw, partner, g, g_swapped, scale, o_ref):
        gc, gs = g * cos, g_swapped * sin
        for h in range(MLA_HEADS):
            sl = slice(h * HEAD_PAD, (h + 1) * HEAD_PAD)
            rh = raw[:, sl]
            inv = lax.rsqrt(jnp.sum(rh * rh, axis=-1, keepdims=True) * (1.0 / QK_HEAD) + EPS) * scale
            o_ref[0, :, sl] = ((rh * gc + partner[:, sl] * gs) * inv).astype(o_ref.dtype)

    finish(_dot(ql, wq_ref[...]), _dot(ql, wqr_ref[...]), gq_ref[...], gqp_ref[...], ATTN_SCALE, q_ref)
    finish(_dot(k_in, wk_ref[...]), _dot(k_in, wkr_ref[...]), gk_ref[...], gkp_ref[...], 1.0, k_ref)
    v_ref[0] = _dot(kvl, wv_ref[...]).astype(v_ref.dtype)


def _mla_prep(proj, p, n_ctx, T):
    B, Tt, _ = proj.shape
    H = MLA_HEADS
    ctx_tiles = n_ctx // ROW_TILE
    mla_blk = (PROJ_RW + PROJ_GATE) // PROJ_MLA
    weights = _mla_weights(p)
    cos, sin = _mla_tables(n_ctx, T)
    const = lambda b, i: (0, 0)
    row = lambda b, i: (b, i, 0)
    tab = pl.BlockSpec((ROW_TILE, HEAD_PAD), lambda b, i: (i, 0))
    full = lambda a: pl.BlockSpec(a.shape, const)
    return pl.pallas_call(
        _mla_prep_kernel,
        grid=(B, Tt // ROW_TILE),
        in_specs=[pl.BlockSpec((1, ROW_TILE, PROJ_MLA), lambda b, i: (b, i, mla_blk)), tab, tab,
                  pl.BlockSpec((1, Q_LORA), const), pl.BlockSpec((1, KV_LORA), const)]
                 + [full(w) for w in weights],
        out_specs=[
            pl.BlockSpec((1, ROW_TILE, H * HEAD_PAD), lambda b, i: (b, jnp.maximum(i - ctx_tiles, 0), 0)),
            pl.BlockSpec((1, ROW_TILE, H * HEAD_PAD), row),
            pl.BlockSpec((1, ROW_TILE, H * V_HEAD), row),
        ],
        out_shape=[
            jax.ShapeDtypeStruct((B, T, H * HEAD_PAD), BF16),
            jax.ShapeDtypeStruct((B, Tt, H * HEAD_PAD), BF16),
            jax.ShapeDtypeStruct((B, Tt, H * V_HEAD), BF16),
        ],
        compiler_params=pltpu.CompilerParams(
            dimension_semantics=("arbitrary", "arbitrary"), vmem_limit_bytes=VMEM_LIMIT),
        name="mla_prep",
    )(proj, cos, sin, p['q_lat_norm'].reshape(1, Q_LORA), p['kv_lat_norm'].reshape(1, KV_LORA), *weights)


ATTN_Q_TILE = 1024
HEADS_PER_STEP = LANES // V_HEAD


ATTN_ROW_SPLIT = 8


def _attn_kernel(q_ref, k_ref, v_ref, o_ref):
    v2 = v_ref[0]
    rows = q_ref.shape[1] // ATTN_ROW_SPLIT
    work = [(hh, rs) for rs in range(ATTN_ROW_SPLIT) for hh in range(HEADS_PER_STEP)]

    def scores(hh, rs):
        sl = slice(hh * HEAD_PAD, (hh + 1) * HEAD_PAD)
        return _dot_t(q_ref[0, rs * rows:(rs + 1) * rows, sl], k_ref[0, :, sl])

    outs = {}
    s_next = scores(*work[0])
    for i, (hh, rs) in enumerate(work):
        s = s_next
        if i + 1 < len(work):
            s_next = scores(*work[i + 1])
        e = jnp.exp(s - jnp.max(s, axis=-1, keepdims=True))
        outs[hh, rs] = _dot(e, v2) / jnp.sum(e, axis=-1, keepdims=True)
    lane = lax.broadcasted_iota(jnp.int32, (rows, LANES), 1)
    for rs in range(ATTN_ROW_SPLIT):
        o_ref[0, rs * rows:(rs + 1) * rows] = jnp.where(lane < V_HEAD, outs[0, rs], outs[1, rs])


def _attention(q, k, v):
    B, T, _ = q.shape
    Kt = k.shape[1]
    assert T % ATTN_Q_TILE == 0 and ATTN_Q_TILE % (ATTN_ROW_SPLIT * SUBLANES) == 0
    hp = MLA_HEADS // HEADS_PER_STEP
    qw = HEADS_PER_STEP * HEAD_PAD
    return pl.pallas_call(
        _attn_kernel,
        grid=(B, hp, T // ATTN_Q_TILE),
        in_specs=[
            pl.BlockSpec((1, ATTN_Q_TILE, qw), lambda b, h, i: (b, i, h)),
            pl.BlockSpec((1, Kt, qw), lambda b, h, i: (b, 0, h)),
            pl.BlockSpec((1, Kt, LANES), lambda b, h, i: (b, 0, h)),
        ],
        out_specs=pl.BlockSpec((1, ATTN_Q_TILE, LANES), lambda b, h, i: (b, i, h)),
        out_shape=jax.ShapeDtypeStruct((B, T, MLA_HEADS * V_HEAD), F32),
        compiler_params=pltpu.CompilerParams(
            dimension_semantics=("arbitrary", "arbitrary", "arbitrary"), vmem_limit_bytes=VMEM_LIMIT),
        name="attention",
    )(q, k, v)


HALO = 16
LORA_W = LANES
GATE_W = PROJ_RW - 3 * RW_DIM - 2 * LORA_W


def _head_ones(width, head):
    i = np.arange(width) // head
    return jnp.asarray(i[:, None] == i[None, :], BF16)


def _head_sum(x, ones_bd):
    hi = x.astype(BF16)
    lo = (x - hi.astype(F32)).astype(BF16)
    return (jnp.dot(hi, ones_bd, preferred_element_type=F32)
            + jnp.dot(lo, ones_bd, preferred_element_type=F32))


def _dot3(a, b_hi, b_lo):
    hi = a.astype(BF16)
    lo = (a - hi.astype(F32)).astype(BF16)
    d = lambda u, w: jnp.dot(u, w, preferred_element_type=F32)
    return d(hi, b_hi) + (d(hi, b_lo) + d(lo, b_hi))


def _rwkv_prep_kernel(x_ref, prev_ref, next_ref, conv_ref, kkg_ref, ka_ref, rk_ref, w0_ref, a0_ref,
                      w2h_ref, w2l_ref, a2_ref, g2_ref, ones_ref,
                      r_ref, v_ref, kk_ref, lw_ref, bk_ref, kd_ref, bonus_ref, gg_ref, *, ctx_tiles, n_tiles):
    i = pl.program_id(1)
    x = x_ref[0].astype(F32)
    tm, W = x.shape
    C = RW_DIM
    first = (i == 0) | (i == ctx_tiles)
    last = (i == ctx_tiles - 1) | (i == n_tiles - 1)
    prev_row = jnp.where(first, 0.0, prev_ref[0, HALO - 1:HALO].astype(F32))
    next_row = jnp.where(last, 0.0, next_ref[0, 0:1].astype(F32))
    row = lax.broadcasted_iota(jnp.int32, (tm, W), 0)
    x_dn = jnp.where(row == 0, prev_row, pltpu.roll(x, 1, 0))
    x_up = jnp.where(row == tm - 1, next_row, pltpu.roll(x, tm - 1, 0))
    xc = x_dn * conv_ref[0:1] + x * conv_ref[1:2] + x_up * conv_ref[2:3]

    r, k, v = xc[:, :C], xc[:, C:2 * C], xc[:, 2 * C:3 * C]
    lora_w = jnp.tanh(xc[:, 3 * C:3 * C + LORA_W])
    lora_a = xc[:, 3 * C + LORA_W:3 * C + 2 * LORA_W]
    lg = xc[:, 3 * C + 2 * LORA_W:]
    ones_bd = ones_ref[...]
    kq = k * kkg_ref[...]
    kk = kq * lax.rsqrt(_head_sum(kq * kq, ones_bd) + 1e-12)
    r_ref[0], v_ref[0], kk_ref[0] = r, v, kk

    k_sum = jnp.zeros_like(k)
    for d in range(2):
        z = w0_ref[d:d + 1] + _dot3(lora_w, w2h_ref[d], w2l_ref[d])
        softplus_neg = jnp.maximum(-z, 0.0) + jnp.log(1.0 + jnp.exp(-jnp.abs(z)))
        lw_ref[d, 0] = -jnp.exp(-softplus_neg - 0.5)
        a = jax.nn.sigmoid(a0_ref[d:d + 1] + _dot(lora_a, a2_ref[d]))
        kd = k * (1.0 + (a - 1.0) * ka_ref[...])
        bk_ref[d, 0] = kk * a
        kd_ref[d, 0] = kd
        k_sum = k_sum + kd
    bonus_ref[0] = _head_sum(r * k_sum * rk_ref[...], ones_bd) * v
    gg_ref[0] = _dot(jax.nn.sigmoid(lg), g2_ref[...])


def _rwkv_prep(proj, p, n_ctx):
    B, Tt, _ = proj.shape
    C = RW_DIM
    nt = Tt // ROW_TILE
    hb = ROW_TILE // HALO
    pad_cols = lambda w: jnp.pad(w, ((0, 0), (0, PROJ_RW - w.shape[1])))
    w2 = jnp.stack([jnp.pad(p['decay_w2'][0], ((0, LORA_W - DECAY_LORA), (0, 0))),
                    jnp.pad(p['decay_w2'][1], ((DECAY_LORA, 0), (0, 0)))])
    w2h = w2.astype(BF16)
    w2l = (w2 - w2h.astype(F32)).astype(BF16)
    a2 = jnp.stack([jnp.pad(p['aicl_a2'][0], ((0, LORA_W - AICL_LORA), (0, 0))),
                    jnp.pad(p['aicl_a2'][1], ((AICL_LORA, 0), (0, 0)))]).astype(BF16)
    g2 = jnp.pad(p['gate_g2'], ((0, GATE_W - GATE_LORA), (0, 0))).astype(BF16)
    row = lambda b, i: (b, i, 0)
    drow = lambda b, i: (0, b, i, 0)
    const2 = lambda b, i: (0, 0)
    const3 = lambda b, i: (0, 0, 0)
    vec = pl.BlockSpec((1, C), const2)
    out_row = pl.BlockSpec((1, ROW_TILE, C), row)
    out_dir = pl.BlockSpec((2, 1, ROW_TILE, C), drow)
    sds = jax.ShapeDtypeStruct((B, Tt, C), F32)
    sds2 = jax.ShapeDtypeStruct((2, B, Tt, C), F32)
    return pl.pallas_call(
        functools.partial(_rwkv_prep_kernel, ctx_tiles=n_ctx // ROW_TILE, n_tiles=nt),
        grid=(B, nt),
        in_specs=[
            pl.BlockSpec((1, ROW_TILE, PROJ_RW), row),
            pl.BlockSpec((1, HALO, PROJ_RW), lambda b, i: (b, jnp.maximum(i * hb - 1, 0), 0)),
            pl.BlockSpec((1, HALO, PROJ_RW), lambda b, i: (b, jnp.minimum((i + 1) * hb, nt * hb - 1), 0)),
            pl.BlockSpec((3, PROJ_RW), const2),
            vec, vec, vec,
            pl.BlockSpec((2, C), const2),
            pl.BlockSpec((2, C), const2),
            pl.BlockSpec((2, LORA_W, C), const3),
            pl.BlockSpec((2, LORA_W, C), const3),
            pl.BlockSpec((2, LORA_W, C), const3),
            pl.BlockSpec((GATE_W, C), const2),
            pl.BlockSpec((C, C), const2),
        ],
        out_specs=[out_row, out_row, out_row, out_dir, out_dir, out_dir, out_row, out_row],
        out_shape=[sds, sds, sds, sds2, sds2, sds2, sds, sds],
        compiler_params=pltpu.CompilerParams(
            dimension_semantics=("arbitrary", "arbitrary"), vmem_limit_bytes=VMEM_LIMIT),
        name="rwkv_prep",
    )(proj, proj, proj, pad_cols(p['shift_conv']), p['k_k'].reshape(1, C), p['k_a'].reshape(1, C),
      p['r_k'].reshape(1, C), p['decay_w0'], p['aicl_a0'], w2h, w2l, a2, g2, _head_ones(C, RW_HEAD))


def _merge_kernel(att_ref, yf_ref, yb_ref, bonus_ref, gg_ref, gate_ref, x_ref, ga_ref, gnw_ref, gnb_ref, ones_ref,
                  wm_ref, wr_ref, wo_ref, o_ref):
    D = x_ref.shape[-1]
    ones_bd = ones_ref[...]
    y = yf_ref[0] + yb_ref[0]
    yc = y - _head_sum(y, ones_bd) * (1.0 / RW_HEAD)
    var = _head_sum(yc * yc, ones_bd) * (1.0 / RW_HEAD)
    y_n = yc * lax.rsqrt(var + GN_EPS) * gnw_ref[...] + gnb_ref[...]
    rw = (y_n + bonus_ref[0]) * gg_ref[0]
    g = jax.nn.sigmoid(gate_ref[0].astype(F32))
    mix = g[:, :D] * _dot(att_ref[0], wm_ref[...]) + g[:, D:] * _dot(rw, wr_ref[...])
    o_ref[0] = x_ref[0] + ga_ref[0] * _dot(mix, wo_ref[...])


def _merge(att, y_f, y_b, bonus, gg, proj, x, g_a, p, n_ctx):
    B, T, D = x.shape
    C = RW_DIM
    ctx_tiles = n_ctx // ROW_TILE
    gate_blk = PROJ_RW // PROJ_GATE
    row = lambda b, i: (b, i, 0)
    lat_row = lambda b, i: (b, i + ctx_tiles, 0)
    const = lambda b, i: (0, 0)
    return pl.pallas_call(
        _merge_kernel,
        grid=(B, T // ROW_TILE),
        in_specs=[
            pl.BlockSpec((1, ROW_TILE, att.shape[-1]), row),
            pl.BlockSpec((1, ROW_TILE, C), row),
            pl.BlockSpec((1, ROW_TILE, C), row),
            pl.BlockSpec((1, ROW_TILE, C), lat_row),
            pl.BlockSpec((1, ROW_TILE, C), lat_row),
            pl.BlockSpec((1, ROW_TILE, PROJ_GATE), lambda b, i: (b, i + ctx_tiles, gate_blk)),
            pl.BlockSpec((1, ROW_TILE, D), row),
            pl.BlockSpec((1, 1, D), lambda b, i: (b, 0, 0)),
            pl.BlockSpec((1, C), const),
            pl.BlockSpec((1, C), const),
            pl.BlockSpec((C, C), const),
            pl.BlockSpec(p['w_o_mla'].shape, const),
            pl.BlockSpec(p['w_o_rwkv'].shape, const),
            pl.BlockSpec(p['w_out'].shape, const),
        ],
        out_specs=pl.BlockSpec((1, ROW_TILE, D), row),
        out_shape=jax.ShapeDtypeStruct((B, T, D), F32),
        compiler_params=pltpu.CompilerParams(
            dimension_semantics=("arbitrary", "arbitrary"), vmem_limit_bytes=VMEM_LIMIT),
        name="merge",
    )(att, y_f, y_b, bonus, gg, proj, x, g_a[:, None, :], p['gn_w'].reshape(1, C), p['gn_b'].reshape(1, C),
      _head_ones(C, RW_HEAD), p['w_o_mla'].astype(BF16), p['w_o_rwkv'].astype(BF16), p['w_out'].astype(BF16))


TILE_ROWS = 8
MOE_BLOCK = 512
ZERO_ROWS = MOE_BLOCK + TILE_ROWS


NEG_INF = float("-inf")
HI_MASK = 0xFFFF0000


def _pack_pair(lo, hi):
    lo_b = lax.bitcast_convert_type(lo.astype(BF16).astype(F32), jnp.uint32)
    hi_b = lax.bitcast_convert_type(hi.astype(BF16).astype(F32), jnp.uint32)
    return (lo_b >> 16) | (hi_b & jnp.uint32(HI_MASK))


def _unpack_pair(w):
    lo = lax.bitcast_convert_type(w << 16, F32)
    hi = lax.bitcast_convert_type(w & jnp.uint32(HI_MASK), F32)
    return lo.astype(BF16), hi.astype(BF16)


def _row_max(x):
    return jnp.max(x, axis=-1, keepdims=True)


def _first_index_of(x, value, lane_f):
    return jnp.min(jnp.where(x == value, lane_f, float(x.shape[-1])), axis=-1, keepdims=True)


def _router_kernel(x_ref, gain_ref, sh_ref, sc_ref, wh_ref, wm_ref, wl_ref, bias_ref,
                   u_ref, ids_ref, gates_ref, ranks_ref, counts_ref, carry_ref):
    @pl.when((pl.program_id(0) == 0) & (pl.program_id(1) == 0))
    def _():
        carry_ref[...] = jnp.zeros_like(carry_ref)

    x = x_ref[0]
    tm, D = x.shape
    E = bias_ref.shape[-1]
    u = x * lax.rsqrt(jnp.mean(x * x, axis=-1, keepdims=True) + EPS) * gain_ref[...]
    u = u * (1.0 + sc_ref[0]) + sh_ref[0]
    u_ref[0] = _pack_pair(u[:, :D // 2], u[:, D // 2:])

    uh, um, ul = _split3(u)
    wh, wm, wl = wh_ref[...], wm_ref[...], wl_ref[...]
    d = lambda a, b: jnp.dot(a, b, preferred_element_type=F32)
    logits = d(uh, wh) + (d(uh, wm) + d(um, wh)) + (d(uh, wl) + d(um, wm) + d(ul, wh))
    scores = jax.nn.sigmoid(logits)
    sel = scores + bias_ref[...]

    lane_i = lax.broadcasted_iota(jnp.int32, (tm, E), 1)
    lane_f = lane_i.astype(F32)
    out_f = lax.broadcasted_iota(jnp.int32, (tm, LANES), 1).astype(F32)
    per_group = E // N_GROUPS
    grp_f = jnp.floor(lane_f * (1.0 / per_group))

    gs = jnp.full((tm, LANES), NEG_INF, F32)
    for g in range(N_GROUPS):
        sg = jnp.where(lane_i >= g * per_group, jnp.where(lane_i < (g + 1) * per_group, sel, NEG_INF), NEG_INF)
        m1 = _row_max(sg)
        i1 = _first_index_of(sg, m1, lane_f)
        m2 = _row_max(jnp.where(lane_f == i1, NEG_INF, sg))
        gs = jnp.where(out_f == g, m1 + m2, gs)

    allow = jnp.zeros((tm, E), F32)
    for _ in range(TOPK_GROUPS):
        m = _row_max(gs)
        i = _first_index_of(gs, m, out_f)
        gs = jnp.where(out_f == i, NEG_INF, gs)
        allow = jnp.where(grp_f == i, 1.0, allow)
    selm = jnp.where(allow > 0.0, sel, NEG_INF)

    ids = jnp.zeros((tm, LANES), F32)
    gts = jnp.zeros((tm, LANES), F32)
    member = jnp.zeros((tm, E), F32)
    idx_cols = []
    gsum = jnp.zeros((tm, 1), F32)
    for k in range(TOP_K):
        m = _row_max(selm)
        i = _first_index_of(selm, m, lane_f)
        hit = lane_f == i
        gk = jnp.sum(jnp.where(hit, scores, 0.0), axis=-1, keepdims=True)
        selm = jnp.where(hit, NEG_INF, selm)
        member = jnp.where(hit, 1.0, member)
        ids = jnp.where(out_f == k, i, ids)
        gts = jnp.where(out_f == k, gk, gts)
        idx_cols.append(i)
        gsum = gsum + gk
    gts = gts / gsum * ROUTED_SCALE

    r2 = lax.broadcasted_iota(jnp.int32, (tm, tm), 0)
    c2 = lax.broadcasted_iota(jnp.int32, (tm, tm), 1)
    before = jnp.where(r2 > c2, 1.0, 0.0).astype(BF16)
    mem_b = member.astype(BF16)
    carry = carry_ref[...]
    pos = carry + jnp.dot(before, mem_b, preferred_element_type=F32)
    rk = jnp.zeros((tm, LANES), F32)
    for k in range(TOP_K):
        rk = jnp.where(out_f == k, jnp.sum(jnp.where(lane_f == idx_cols[k], pos, 0.0), axis=-1, keepdims=True), rk)
    colsum = jnp.dot(jnp.ones((8, tm), BF16), mem_b, preferred_element_type=F32)[0:1]
    carry_ref[...] = carry + colsum
    counts_ref[...] = carry + colsum
    ids_ref[0] = ids.astype(jnp.int32)
    ranks_ref[0] = rk.astype(jnp.int32)
    gates_ref[0] = gts


def _router(x1, sh_m, sc_m, p):
    B, T, D = x1.shape
    E = N_EXPERTS
    wh, wm, wl = _split3(p['router_w'])
    row = lambda b, i: (b, i, 0)
    const = lambda b, i: (0, 0)
    vec = lambda b, i: (b, 0, 0)
    lane_out = lambda dt: jax.ShapeDtypeStruct((B, T, LANES), dt)
    return pl.pallas_call(
        _router_kernel,
        grid=(B, T // ROW_TILE),
        in_specs=[
            pl.BlockSpec((1, ROW_TILE, D), row),
            pl.BlockSpec((1, D), const),
            pl.BlockSpec((1, 1, D), vec),
            pl.BlockSpec((1, 1, D), vec),
            pl.BlockSpec((D, E), const),
            pl.BlockSpec((D, E), const),
            pl.BlockSpec((D, E), const),
            pl.BlockSpec((1, E), const),
        ],
        out_specs=[
            pl.BlockSpec((1, ROW_TILE, D // 2), row),
            pl.BlockSpec((1, ROW_TILE, LANES), row),
            pl.BlockSpec((1, ROW_TILE, LANES), row),
            pl.BlockSpec((1, ROW_TILE, LANES), row),
            pl.BlockSpec((1, E), const),
        ],
        out_shape=[
            jax.ShapeDtypeStruct((B, T, D // 2), jnp.uint32),
            lane_out(jnp.int32), lane_out(F32), lane_out(jnp.int32),
            jax.ShapeDtypeStruct((1, E), F32),
        ],
        scratch_shapes=[pltpu.VMEM((1, E), F32)],
        compiler_params=pltpu.CompilerParams(
            dimension_semantics=("arbitrary", "arbitrary"), vmem_limit_bytes=VMEM_LIMIT),
        name="router",
    )(x1, p['norm_ffn'].reshape(1, D), sh_m[:, None, :], sc_m[:, None, :], wh, wm, wl,
      p['router_bias'].reshape(1, E))


def _slot_kernel(ids_ref, ranks_ref, base_ref, o_ref):
    ids = ids_ref[...].astype(F32)
    tm = ids.shape[0]
    E = base_ref.shape[-1]
    lane_e = lax.broadcasted_iota(jnp.int32, (tm, E), 1).astype(F32)
    out_lane = lax.broadcasted_iota(jnp.int32, (tm, LANES), 1)
    first = jnp.zeros((tm, LANES), F32)
    for k in range(TOP_K):
        fk = jnp.sum(jnp.where(lane_e == ids[:, k:k + 1], base_ref[...], 0.0), axis=-1, keepdims=True)
        first = jnp.where(out_lane == k, fk, first)
    o_ref[...] = first.astype(jnp.int32) + ranks_ref[...]


def _slots(ids, ranks, base):
    n_tok = ids.shape[0]
    E = base.shape[-1]
    row = pl.BlockSpec((ROW_TILE, LANES), lambda i: (i, 0))
    return pl.pallas_call(
        _slot_kernel,
        grid=(n_tok // ROW_TILE,),
        in_specs=[row, row, pl.BlockSpec((1, E), lambda i: (0, 0))],
        out_specs=row,
        out_shape=jax.ShapeDtypeStruct((n_tok, LANES), jnp.int32),
        compiler_params=pltpu.CompilerParams(dimension_semantics=("arbitrary",)),
        name="slots",
    )(ids, ranks, base)


DMA_PRIORITIES = 2


def _row_copy(src, src_row, dst, dst_row, sem):
    return pltpu.make_async_copy(src.at[pl.ds(src_row, 1)], dst.at[pl.ds(dst_row, 1)], sem)


def _dispatch_kernel(pad_ref, dest_ref, u_ref, slots_ref, zero_ref, sem):
    tm = u_ref.shape[0]
    n_experts = pad_ref.shape[0]

    @pl.when(pl.program_id(0) == 0)
    def _():
        zero_ref[...] = jnp.zeros_like(zero_ref)

        def fill(e, carry):
            start = pl.multiple_of(pad_ref[e], TILE_ROWS)
            pltpu.make_async_copy(zero_ref, slots_ref.at[pl.ds(start, ZERO_ROWS)], sem).start()
            return carry

        lax.fori_loop(0, n_experts, fill, 0)

        def filled(e, carry):
            pltpu.make_async_copy(zero_ref, slots_ref.at[pl.ds(0, ZERO_ROWS)], sem).wait()
            return carry

        lax.fori_loop(0, n_experts, filled, 0)

    def issue(r, carry):
        for k in range(TOP_K):
            _row_copy(u_ref, r, slots_ref, dest_ref[0, 0, r * TOP_K + k], sem).start(priority=k % DMA_PRIORITIES)
        return carry

    lax.fori_loop(0, tm, issue, 0)
    for k in range(TOP_K):
        pltpu.make_async_copy(u_ref, slots_ref.at[pl.ds(0, tm)], sem).wait()


def _dispatch(pad_start, dest, u_rows, n_slots):
    n_tok, W = u_rows.shape
    nt = n_tok // ROW_TILE
    grid_spec = pltpu.PrefetchScalarGridSpec(
        num_scalar_prefetch=1,
        grid=(nt,),
        in_specs=[
            pl.BlockSpec((1, 1, ROW_TILE * TOP_K), lambda i, pad: (i, 0, 0), memory_space=pltpu.SMEM),
            pl.BlockSpec((ROW_TILE, W), lambda i, pad: (i, 0)),
        ],
        out_specs=pl.BlockSpec(memory_space=pl.ANY),
        scratch_shapes=[pltpu.VMEM((ZERO_ROWS, W), jnp.uint32), pltpu.SemaphoreType.DMA(())],
    )
    return pl.pallas_call(
        _dispatch_kernel,
        grid_spec=grid_spec,
        out_shape=jax.ShapeDtypeStruct((n_slots + ZERO_ROWS, W), jnp.uint32),
        compiler_params=pltpu.CompilerParams(
            dimension_semantics=("arbitrary",), vmem_limit_bytes=VMEM_LIMIT),
        name="dispatch",
    )(pad_start, dest, u_rows)


WEIGHT_SLOTS = 2


def _expert_weight_copies(hbm_refs, stage_refs, expert, slot, sems):
    return [pltpu.make_async_copy(w.at[expert], s.at[slot], sems.at[slot, i])
            for i, (w, s) in enumerate(zip(hbm_refs, stage_refs))]


def _moe_ffn_kernel(be_ref, first_ref, slot_ref, next_ref, nu_ref, x_ref, w1_ref, w3_ref, w2_ref, o_ref,
                    s1_ref, s3_ref, s2_ref, w13_ref, w2b_ref, sems):
    j = pl.program_id(0)
    F = w1_ref.shape[-1]
    half = x_ref.shape[-1]
    copies = functools.partial(_expert_weight_copies, (w1_ref, w3_ref, w2_ref), (s1_ref, s3_ref, s2_ref), sems=sems)

    @pl.when(j == 0)
    def _():
        for c in copies(be_ref[0], 0):
            c.start()

    @pl.when(first_ref[j] == 1)
    def _():
        slot = slot_ref[j]
        for c in copies(be_ref[j], slot):
            c.wait()

        @pl.when(next_ref[j] >= 0)
        def _():
            for c in copies(next_ref[j], 1 - slot):
                c.start()

        w13_ref[:, :F] = s1_ref[slot].astype(BF16)
        w13_ref[:, F:] = s3_ref[slot].astype(BF16)
        w2b_ref[...] = s2_ref[slot].astype(BF16)

    @pl.when(j < nu_ref[0])
    def _():
        lo, hi = _unpack_pair(x_ref[...])
        h = (jnp.dot(lo, w13_ref[:half], preferred_element_type=F32)
             + jnp.dot(hi, w13_ref[half:], preferred_element_type=F32))
        h1, h3 = h[:, :F], h[:, F:]
        y = _dot(h1 * jax.nn.sigmoid(h1) * h3, w2b_ref[...])
        o_ref[...] = pltpu.einshape("s(cl)->(sc)l", y, c=TILE_ROWS)

    @pl.when(j >= nu_ref[0])
    def _():
        o_ref[...] = jnp.zeros_like(o_ref)


def _moe_ffn(block_e, n_used, x_slots, w1, w3, w2):
    W = x_slots.shape[1]
    E, D, F = w1.shape
    nblk = block_e.shape[0]
    slots = nblk * MOE_BLOCK
    idx = jnp.arange(nblk)
    first = jnp.concatenate([jnp.ones((1,), bool), block_e[1:] != block_e[:-1]])
    slot = ((jnp.cumsum(first) - 1) % WEIGHT_SLOTS).astype(jnp.int32)
    later_first = lax.cummin(jnp.where(first, idx, nblk)[::-1])[::-1]
    next_first = jnp.concatenate([later_first[1:], jnp.full((1,), nblk)])
    next_e = jnp.where(next_first < nblk, block_e[jnp.minimum(next_first, nblk - 1)], -1).astype(jnp.int32)
    n_prefetch = 5
    blk = lambda j, *_: (jnp.minimum(j, _[4][0] - 1), 0)
    grid_spec = pltpu.PrefetchScalarGridSpec(
        num_scalar_prefetch=n_prefetch,
        grid=(nblk,),
        in_specs=[
            pl.BlockSpec((MOE_BLOCK, W), blk),
            pl.BlockSpec(memory_space=pl.ANY),
            pl.BlockSpec(memory_space=pl.ANY),
            pl.BlockSpec(memory_space=pl.ANY),
        ],
        out_specs=pl.BlockSpec((MOE_BLOCK * TILE_ROWS, LANES), lambda j, *_: (j, 0)),
        scratch_shapes=[pltpu.VMEM((WEIGHT_SLOTS, D, F), F32), pltpu.VMEM((WEIGHT_SLOTS, D, F), F32),
                        pltpu.VMEM((WEIGHT_SLOTS, F, D), F32),
                        pltpu.VMEM((D, 2 * F), BF16), pltpu.VMEM((F, D), BF16),
                        pltpu.SemaphoreType.DMA((WEIGHT_SLOTS, 3))],
    )
    return pl.pallas_call(
        _moe_ffn_kernel,
        grid_spec=grid_spec,
        out_shape=jax.ShapeDtypeStruct((slots * TILE_ROWS, LANES), F32),
        compiler_params=pltpu.CompilerParams(
            dimension_semantics=("arbitrary",), vmem_limit_bytes=VMEM_LIMIT),
        name="moe_ffn",
    )(block_e, first.astype(jnp.int32), slot, next_e, n_used, x_slots, w1, w3, w2)


COMBINE_ROWS = 32


def _combine_kernel(dest_ref, gates_ref, u_ref, x_ref, gm_ref, w1_ref, w3_ref, w2_ref, ys_ref, o_ref, buf_ref, sem):
    tm, half = u_ref.shape

    def tile(i):
        return pl.ds(pl.multiple_of(i * TILE_ROWS, TILE_ROWS), TILE_ROWS)

    def issue(r, carry):
        for k in range(TOP_K):
            pltpu.make_async_copy(ys_ref.at[tile(dest_ref[0, 0, r * TOP_K + k])], buf_ref.at[k, tile(r)], sem).start(
                priority=k % DMA_PRIORITIES)
        return carry

    lax.fori_loop(0, tm, issue, 0)

    ulo, uhi = _unpack_pair(u_ref[...])
    both = lambda w_ref: (jnp.dot(ulo, w_ref[:half], preferred_element_type=F32)
                          + jnp.dot(uhi, w_ref[half:], preferred_element_type=F32))
    h1, h3 = both(w1_ref), both(w3_ref)
    o_ref[...] = x_ref[...] + gm_ref[0] * _dot(h1 * jax.nn.sigmoid(h1) * h3, w2_ref[...])

    for k in range(TOP_K):
        pltpu.make_async_copy(ys_ref.at[pl.ds(0, tm * TILE_ROWS)], buf_ref.at[k], sem).wait()

    gm = gm_ref[0]
    for rb in range(tm // COMBINE_ROWS):
        rows = slice(rb * COMBINE_ROWS, (rb + 1) * COMBINE_ROWS)
        gates = gates_ref[rows, :]
        for c in range(TILE_ROWS):
            acc = jnp.zeros((COMBINE_ROWS, LANES), F32)
            for k in range(TOP_K):
                acc = acc + gates[:, k:k + 1] * buf_ref[
                    k, pl.ds(rb * COMBINE_ROWS * TILE_ROWS + c, COMBINE_ROWS, stride=TILE_ROWS), :]
            cols = slice(c * LANES, (c + 1) * LANES)
            o_ref[rows, cols] = o_ref[rows, cols] + gm[:, cols] * acc


def _combine(dest, gates, u_rows, x1, g_m, y_slots, p):
    B, T, D = x1.shape
    n_tok = B * T
    W = u_rows.shape[1]
    tiles_per_batch = T // ROW_TILE
    row = lambda i: (i, 0)
    const = lambda i: (0, 0)
    out = pl.pallas_call(
        _combine_kernel,
        grid=(n_tok // ROW_TILE,),
        in_specs=[
            pl.BlockSpec((1, 1, ROW_TILE * TOP_K), lambda i: (i, 0, 0), memory_space=pltpu.SMEM),
            pl.BlockSpec((ROW_TILE, LANES), row),
            pl.BlockSpec((ROW_TILE, W), row),
            pl.BlockSpec((ROW_TILE, D), row),
            pl.BlockSpec((1, 1, D), lambda i: (i // tiles_per_batch, 0, 0)),
            pl.BlockSpec(p['shared_w1'].shape, const),
            pl.BlockSpec(p['shared_w3'].shape, const),
            pl.BlockSpec(p['shared_w2'].shape, const),
            pl.BlockSpec(memory_space=pl.ANY),
        ],
        out_specs=pl.BlockSpec((ROW_TILE, D), row),
        out_shape=jax.ShapeDtypeStruct((n_tok, D), F32),
        scratch_shapes=[pltpu.VMEM((TOP_K, ROW_TILE * TILE_ROWS, LANES), F32), pltpu.SemaphoreType.DMA(())],
        compiler_params=pltpu.CompilerParams(
            dimension_semantics=("arbitrary",), vmem_limit_bytes=VMEM_LIMIT),
        name="combine",
    )(dest, gates.reshape(n_tok, LANES), u_rows, x1.reshape(n_tok, D), g_m[:, None, :],
      p['shared_w1'].astype(BF16), p['shared_w3'].astype(BF16), p['shared_w2'].astype(BF16), y_slots)
    return out.reshape(B, T, D)


def _moe(x1, sh_m, sc_m, g_m, p):
    B, T, D = x1.shape
    n_tok = B * T
    u_packed, ids, gates, ranks, counts = _router(x1, sh_m, sc_m, p)

    counts = counts[0].astype(jnp.int32)
    padded = (counts + MOE_BLOCK - 1) // MOE_BLOCK * MOE_BLOCK
    padded_end = jnp.cumsum(padded)
    base = padded_end - padded
    n_blocks = n_tok * TOP_K // MOE_BLOCK + N_EXPERTS
    n_used = (padded_end[-1] // MOE_BLOCK).astype(jnp.int32)
    blk = jnp.minimum(jnp.arange(n_blocks), n_used - 1) * MOE_BLOCK
    block_e = jnp.minimum(jnp.sum(padded_end[None, :] <= blk[:, None], axis=1), N_EXPERTS - 1).astype(jnp.int32)
    dest = _slots(ids.reshape(n_tok, LANES), ranks.reshape(n_tok, LANES), base.astype(F32).reshape(1, N_EXPERTS))
    dest = dest[:, :TOP_K].reshape(n_tok // ROW_TILE, 1, ROW_TILE * TOP_K)

    u_rows = u_packed.reshape(n_tok, D // 2)
    pad_start = ((base + counts) // TILE_ROWS * TILE_ROWS).astype(jnp.int32)
    x_slots = _dispatch(pad_start, dest, u_rows, n_blocks * MOE_BLOCK)
    y_slots = _moe_ffn(block_e, n_used.reshape(1), x_slots, p['expert_w1'], p['expert_w3'], p['expert_w2'])
    return _combine(dest, gates, u_rows, x1, g_m, y_slots, p)
```

```python
import functools

import jax
import jax.numpy as jnp
import numpy as np
from jax import lax
from jax.experimental import pallas as pl
from jax.experimental.pallas import tpu as pltpu

F32 = jnp.float32
BF16 = jnp.bfloat16

GRID_W = 64
EPS = 1e-6
MLA_HEADS = 8
QK_NOPE = 64
QK_ROPE = 32
QK_HEAD = QK_NOPE + QK_ROPE
V_HEAD = 64
Q_LORA = 256
KV_LORA = 128
ROPE_BASE = 10000.0
ATTN_SCALE = QK_HEAD ** -0.5
RW_HEADS = 8
RW_HEAD = 64
RW_DIM = RW_HEADS * RW_HEAD
DECAY_LORA = 64
AICL_LORA = 64
GATE_LORA = 160
GN_EPS = 64e-5
N_EXPERTS = 256
TOP_K = 8
N_GROUPS = 8
TOPK_GROUPS = 4
ROUTED_SCALE = 2.5

LANES = 128
MXU_DIM = 256
VMEM_LIMIT = 56 * 1024 * 1024

WKV_CHUNK = 64
WKV_GROUP = 128
HEADS_PER_GROUP = WKV_GROUP // RW_HEAD
HEAD_PAD = LANES


def _dot(a, b):
    return jnp.dot(a.astype(BF16), b.astype(BF16), preferred_element_type=F32)


def _dot_t(a, b):
    return lax.dot_general(a.astype(BF16), b.astype(BF16), (((1,), (1,)), ((), ())),
                           preferred_element_type=F32)


def _split3(x):
    h = x.astype(BF16)
    r1 = x - h.astype(F32)
    m = r1.astype(BF16)
    lo = (r1 - m.astype(F32)).astype(BF16)
    return h, m, lo


def _dot_hi(a_bf16_exact, x):
    h, m, lo = _split3(x)
    d = lambda y: jnp.dot(a_bf16_exact, y, preferred_element_type=F32)
    return d(h) + d(m) + d(lo)


def _wkv_chunk(r, v, kk, lw, bk, kd, s0, reverse, emit):
    L, G = r.shape
    row = lax.broadcasted_iota(jnp.int32, (L, G), 0)
    lane = lax.broadcasted_iota(jnp.int32, (L, G), 1)
    diff = (lane % L - row) if reverse else (row - lane % L)
    strict = diff > 0
    incl = diff >= 0
    r2 = lax.broadcasted_iota(jnp.int32, (L, L), 0)
    c2 = lax.broadcasted_iota(jnp.int32, (L, L), 1)
    tri = jnp.where(((c2 - r2) if reverse else (r2 - c2)) >= 0, 1.0, 0.0).astype(BF16)
    tri_ones = jnp.concatenate([tri, jnp.ones((L, L), BF16)], axis=0)
    lane_head = lane // RW_HEAD
    rowb = lax.broadcasted_iota(jnp.int32, (G, G), 0) // RW_HEAD
    colb = lax.broadcasted_iota(jnp.int32, (G, G), 1) // RW_HEAD

    def stack(x):
        return jnp.concatenate(
            [jnp.where(lane_head == h, x, 0.0) for h in range(HEADS_PER_GROUP)], axis=0).astype(BF16)

    sums = _dot_hi(tri_ones, lw)
    cum_in = sums[:L]
    g_in = jnp.exp(cum_in)
    g_inv = jnp.exp(-cum_in)
    g_ex = jnp.exp(cum_in - lw)
    g_tot = jnp.exp(sums[L:])
    a_h = -kk * g_ex
    b_h = bk * g_inv
    k_h = kd * g_inv
    r_h = r * g_in
    yield

    gram = _dot_t(jnp.concatenate([a_h, r_h], axis=0),
                  jnp.concatenate([stack(b_h), stack(k_h)], axis=0))
    ab = jnp.where(strict, gram[:L, :G], 0.0)
    ak = jnp.where(strict, gram[:L, G:], 0.0)
    rb = jnp.where(incl, gram[L:, :G], 0.0)
    rk = jnp.where(incl, gram[L:, G:], 0.0)
    yield

    tm = jnp.where(diff == 0, 1.0, 0.0) + ab
    p = ab
    v_st = stack(v)
    akv = _dot(ak, v_st)
    for _ in range(int(np.log2(L)) - 1):
        p = _dot(p, stack(p))
        yield
        tm = tm + _dot(tm, stack(p))
        yield

    wu = _dot(tm, jnp.concatenate([stack(a_h), stack(akv)], axis=1))
    w_t, u_t = wu[:, :G], wu[:, G:]
    yield
    rbwu = _dot(rb, jnp.concatenate([stack(w_t), stack(u_t)], axis=1))
    r_t = r_h + rbwu[:, :G]
    y_t = rbwu[:, G:] + _dot(rk, v_st)
    yield

    s0_st = stack(s0)
    y = _dot_t(r_t, s0_st) + y_t
    u = _dot_t(w_t, s0_st) + u_t
    yield
    uv = jnp.concatenate([u, v], axis=0)
    bkc = jnp.concatenate([b_h, k_h], axis=0)
    upd = lax.dot_general(uv.astype(BF16), bkc.astype(BF16), (((0,), (0,)), ((), ())),
                          preferred_element_type=F32)
    upd = jnp.where(rowb == colb, upd, 0.0)
    upd_d = upd[0:L]
    for h in range(1, HEADS_PER_GROUP):
        upd_d = upd_d + upd[h * L:(h + 1) * L]
    emit(y, (s0 + upd_d) * g_tot)


def _wkv_kernel(*refs, n_groups, n_batch):
    ins, (yf_ref, yb_ref, s_ref) = refs[:12], refs[12:]
    G = WKV_GROUP

    @pl.when(pl.program_id(1) == 0)
    def _():
        s_ref[...] = jnp.zeros_like(s_ref)

    chains = []
    for d, y_ref in enumerate((yf_ref, yb_ref)):
        r_ref, v_ref, kk_ref, lw_ref, bk_ref, kd_ref = ins[6 * d:6 * d + 6]
        for bi in range(n_batch):
            for g in range(n_groups):
                sl = slice(g * G, (g + 1) * G)

                def emit(y, s_new, y_ref=y_ref, d=d, bi=bi, g=g, sl=sl):
                    y_ref[bi, :, sl] = y
                    s_ref[d, bi, g] = s_new

                chains.append(_wkv_chunk(
                    r_ref[bi, :, sl], v_ref[bi, :, sl], kk_ref[bi, :, sl], lw_ref[0, bi, :, sl],
                    bk_ref[0, bi, :, sl], kd_ref[0, bi, :, sl], s_ref[d, bi, g], d == 1, emit))
    while chains:
        chains = [c for c in chains if next(c, StopIteration) is not StopIteration]


WKV_BATCH = 2


def _wkv_scan(r, v, kk, lw, bk, kd, n_ctx):
    B, Ttot, C = r.shape
    L = WKV_CHUNK
    nc = Ttot // L
    ncc = n_ctx // L
    nl = nc - ncc
    n_groups = C // WKV_GROUP
    nb = WKV_BATCH

    cid = (lambda s: s, lambda s: jnp.where(s < ncc, ncc - 1 - s, nc + ncc - 1 - s))
    first_out = (0, nl - 1)
    in_specs, out_specs = [], []
    for d in range(2):
        shared = pl.BlockSpec((nb, L, C), lambda b, s, d=d: (b, cid[d](s), 0))
        per_dir = pl.BlockSpec((1, nb, L, C), lambda b, s, d=d: (d, b, cid[d](s), 0))
        in_specs += [shared, shared, shared, per_dir, per_dir, per_dir]
        out_specs.append(pl.BlockSpec(
            (nb, L, C), lambda b, s, d=d: (b, jnp.where(s < ncc, first_out[d], cid[d](s) - ncc), 0)))
    out = jax.ShapeDtypeStruct((B, nl * L, C), F32)
    return pl.pallas_call(
        functools.partial(_wkv_kernel, n_groups=n_groups, n_batch=nb),
        grid=(B // nb, nc),
        in_specs=in_specs,
        out_specs=out_specs,
        out_shape=[out, out],
        scratch_shapes=[pltpu.VMEM((2, nb, n_groups, L, WKV_GROUP), F32)],
        compiler_params=pltpu.CompilerParams(
            dimension_semantics=("arbitrary", "arbitrary"), vmem_limit_bytes=VMEM_LIMIT),
        name="wkv_scan",
    )(r, v, kk, lw, bk, kd, r, v, kk, lw, bk, kd)


def kernel(x, c, ctx, c_ctx, ada_w, ada_b, norm_mix, norm_ffn, w_in, shift_conv, q_lat_norm, w_uq, kv_lat_norm, w_ukv, q_norm, k_norm, w_o_mla, decay_w0, decay_w2, aicl_a0, aicl_a2, k_k, k_a, r_k, gn_w, gn_b, gate_g2, w_o_rwkv, w_out, router_w, router_bias, expert_w1, expert_w3, expert_w2, shared_w1, shared_w3, shared_w2):
    B, T, D = x.shape
    n_ctx = ctx.shape[1]
    i0 = 0
    p = dict(norm_mix=norm_mix[i0], norm_ffn=norm_ffn[i0], w_in=w_in[i0], shift_conv=shift_conv[i0],
             q_lat_norm=q_lat_norm[i0], w_uq=w_uq[i0], kv_lat_norm=kv_lat_norm[i0], w_ukv=w_ukv[i0],
             q_norm=q_norm[i0], k_norm=k_norm[i0], w_o_mla=w_o_mla[i0],
             decay_w0=decay_w0[i0], decay_w2=decay_w2[i0], aicl_a0=aicl_a0[i0], aicl_a2=aicl_a2[i0],
             k_k=k_k[i0], k_a=k_a[i0], r_k=r_k[i0], gn_w=gn_w[i0], gn_b=gn_b[i0], gate_g2=gate_g2[i0],
             w_o_rwkv=w_o_rwkv[i0], w_out=w_out[i0], router_w=router_w[i0], router_bias=router_bias[i0],
             expert_w1=expert_w1[i0], expert_w3=expert_w3[i0], expert_w2=expert_w2[i0],
             shared_w1=shared_w1[i0], shared_w3=shared_w3[i0], shared_w2=shared_w2[i0])

    mod = _ada_modulation(c, c_ctx, ada_w[i0], ada_b[i0])
    sh_a, sc_a, g_a, sh_m, sc_m, g_m = jnp.split(mod[:B], 6, axis=-1)
    csh_a, csc_a = jnp.split(mod[B], 6, axis=-1)[:2]
    sh2 = jnp.stack([jnp.broadcast_to(csh_a, (B, D)), sh_a], axis=1)
    sc2 = jnp.stack([jnp.broadcast_to(csc_a, (B, D)), sc_a], axis=1)

    proj = _in_proj(ctx, x, sh2, sc2, p['norm_mix'], _pack_w_in(p['w_in']))

    q, k, v = _mla_prep(proj, p, n_ctx, T)
    att = _attention(q, k, v)

    r, vv, kk, lw, bk, kd, bonus, gg = _rwkv_prep(proj, p, n_ctx)
    y_f, y_b = _wkv_scan(r, vv, kk, lw, bk, kd, n_ctx)

    x1 = _merge(att, y_f, y_b, bonus, gg, proj, x, g_a, p, n_ctx)
    return _moe(x1, sh_m, sc_m, g_m, p)


SUBLANES = 8


def _ada_kernel(c_ref, w_ref, b_ref, o_ref):
    cc = c_ref[...]
    w = w_ref[...]
    w_hi = w.astype(BF16)
    w_lo = (w - w_hi.astype(F32)).astype(BF16)
    o_ref[...] = _dot3(cc * jax.nn.sigmoid(cc), w_hi, w_lo) + b_ref[...]


def _ada_modulation(c, c_ctx, ada_w, ada_b):
    B, D = c.shape
    n_out = ada_w.shape[1]
    rows = -(-(B + 1) // SUBLANES) * SUBLANES
    c_all = jnp.concatenate([c, c_ctx[None, :], jnp.zeros((rows - B - 1, D), F32)], axis=0)
    out = pl.pallas_call(
        _ada_kernel,
        grid=(n_out // D,),
        in_specs=[pl.BlockSpec((rows, D), lambda j: (0, 0)),
                  pl.BlockSpec((D, D), lambda j: (0, j)),
                  pl.BlockSpec((1, D), lambda j: (0, j))],
        out_specs=pl.BlockSpec((rows, D), lambda j: (0, j)),
        out_shape=jax.ShapeDtypeStruct((rows, n_out), F32),
        compiler_params=pltpu.CompilerParams(dimension_semantics=("arbitrary",), vmem_limit_bytes=VMEM_LIMIT),
        name="ada_modulation",
    )(c_all, ada_w, ada_b.reshape(1, n_out))
    return out[:B + 1]


ROW_TILE = 256
PROJ_RW = 2048
PROJ_GATE = 2048
PROJ_MLA = 512
PROJ_W = PROJ_RW + PROJ_GATE + PROJ_MLA
MLA_IN = Q_LORA + KV_LORA + QK_ROPE
RW_SPLITS = (RW_DIM, RW_DIM, RW_DIM, DECAY_LORA, DECAY_LORA, AICL_LORA, AICL_LORA, GATE_LORA)
RW_IN = sum(RW_SPLITS)


def _pack_w_in(w_in):
    w_mla = w_in[:, :MLA_IN]
    w_rw = w_in[:, MLA_IN:MLA_IN + RW_IN]
    w_gate = w_in[:, MLA_IN + RW_IN:]
    pad = lambda w, n: jnp.pad(w, ((0, 0), (0, n - w.shape[1])))
    return jnp.concatenate([pad(w_rw, PROJ_RW), w_gate, pad(w_mla, PROJ_MLA)], axis=1).astype(BF16)


def _in_proj_kernel(ctx_ref, x_ref, sh_ref, sc_ref, gain_ref, w_ref, o_ref):
    is_ctx = pl.program_id(1) == 0
    xt = jnp.where(is_ctx, ctx_ref[0], x_ref[0])
    sh = jnp.where(is_ctx, sh_ref[0, 0:1], sh_ref[0, 1:2])
    sc = jnp.where(is_ctx, sc_ref[0, 0:1], sc_ref[0, 1:2])
    y = xt * lax.rsqrt(jnp.mean(xt * xt, axis=-1, keepdims=True) + EPS) * gain_ref[...]
    h = y * (1.0 + sc) + sh
    o_ref[0] = _dot(h, w_ref[...]).astype(o_ref.dtype)


def _in_proj(ctx, x, sh2, sc2, gain, w):
    B, T, D = x.shape
    n_ctx = ctx.shape[1]
    assert n_ctx == ROW_TILE and T % ROW_TILE == 0
    nt = 1 + T // ROW_TILE
    return pl.pallas_call(
        _in_proj_kernel,
        grid=(B, nt),
        in_specs=[
            pl.BlockSpec((1, ROW_TILE, D), lambda b, i: (b, 0, 0)),
            pl.BlockSpec((1, ROW_TILE, D), lambda b, i: (b, jnp.maximum(i - 1, 0), 0)),
            pl.BlockSpec((1, 2, D), lambda b, i: (b, 0, 0)),
            pl.BlockSpec((1, 2, D), lambda b, i: (b, 0, 0)),
            pl.BlockSpec((1, D), lambda b, i: (0, 0)),
            pl.BlockSpec((D, PROJ_W), lambda b, i: (0, 0)),
        ],
        out_specs=pl.BlockSpec((1, ROW_TILE, PROJ_W), lambda b, i: (b, i, 0)),
        out_shape=jax.ShapeDtypeStruct((B, n_ctx + T, PROJ_W), BF16),
        compiler_params=pltpu.CompilerParams(
            dimension_semantics=("arbitrary", "arbitrary"), vmem_limit_bytes=VMEM_LIMIT),
        name="in_proj",
    )(ctx, x, sh2, sc2, gain.reshape(1, D), w)


def _rms(x, gain):
    return x * lax.rsqrt(jnp.mean(x * x, axis=-1, keepdims=True) + EPS) * gain


def _rope_tables(n_tokens):
    rows = n_tokens // GRID_W
    row = jnp.repeat(jnp.arange(rows, dtype=F32), GRID_W)
    col = jnp.tile(jnp.arange(GRID_W, dtype=F32), rows)
    n_freq = QK_ROPE // 4
    inv_freq = ROPE_BASE ** (-jnp.arange(n_freq, dtype=F32) / n_freq)
    ang = jnp.concatenate([row[:, None] * inv_freq, col[:, None] * inv_freq], axis=-1)
    return jnp.cos(ang), jnp.sin(ang)


ROPE_HALF = QK_ROPE // 2
X1 = slice(QK_NOPE, QK_NOPE + ROPE_HALF)
X2 = slice(QK_NOPE + ROPE_HALF, QK_HEAD)


def _rot_cols(w):
    w3 = w.reshape(w.shape[0], MLA_HEADS, HEAD_PAD)
    rot = jnp.zeros_like(w3).at[:, :, X1].set(-w3[:, :, X2]).at[:, :, X2].set(w3[:, :, X1])
    return rot.reshape(w.shape)


def _swap_halves(g):
    return jnp.zeros_like(g).at[:, X1].set(g[:, X2]).at[:, X2].set(g[:, X1])


def _mla_weights(p):
    H = MLA_HEADS
    wq = jnp.pad(p['w_uq'].reshape(Q_LORA, H, QK_HEAD), ((0, 0), (0, 0), (0, HEAD_PAD - QK_HEAD)))
    wq = wq.reshape(Q_LORA, H * HEAD_PAD)
    wkv = p['w_ukv'].reshape(KV_LORA, H, QK_NOPE + V_HEAD)
    wk_lat = jnp.pad(wkv[:, :, :QK_NOPE], ((0, 0), (0, 0), (0, HEAD_PAD - QK_NOPE)))
    place = jnp.zeros((LANES, H, HEAD_PAD), F32).at[:QK_ROPE, :, QK_NOPE:QK_HEAD].set(
        jnp.broadcast_to(jnp.eye(QK_ROPE, dtype=F32)[:, None, :], (QK_ROPE, H, QK_ROPE)))
    wk = jnp.concatenate([wk_lat, place], axis=0).reshape(KV_LORA + LANES, H * HEAD_PAD)
    wv = wkv[:, :, QK_NOPE:].reshape(KV_LORA, H * V_HEAD)
    gq = jnp.pad(p['q_norm'], (0, HEAD_PAD - QK_HEAD)).reshape(1, HEAD_PAD)
    gk = jnp.pad(p['k_norm'], (0, HEAD_PAD - QK_HEAD)).reshape(1, HEAD_PAD)
    bf = lambda w: w.astype(BF16)
    return (bf(wq), bf(_rot_cols(wq)), bf(wk), bf(_rot_cols(wk)), bf(wv), gq, _swap_halves(gq), gk, _swap_halves(gk))


def _mla_tables(n_ctx, T):
    cos, sin = _rope_tables(T)
    c = jnp.ones((n_ctx + T, HEAD_PAD), F32).at[n_ctx:, X1].set(cos).at[n_ctx:, X2].set(cos)
    s = jnp.zeros((n_ctx + T, HEAD_PAD), F32).at[n_ctx:, X1].set(sin).at[n_ctx:, X2].set(sin)
    return c, s


def _mla_prep_kernel(x_ref, c_ref, s_ref, qlg_ref, kvg_ref, wq_ref, wqr_ref, wk_ref, wkr_ref, wv_ref,
                     gq_ref, gqp_ref, gk_ref, gkp_ref, q_ref, k_ref, v_ref):
    x = x_ref[0].astype(F32)
    ql = _rms(x[:, :Q_LORA], qlg_ref[...])
    kvl = _rms(x[:, Q_LORA:Q_LORA + KV_LORA], kvg_ref[...])
    k_in = jnp.concatenate([kvl, x[:, Q_LORA + KV_LORA:]], axis=1)
    cos, sin = c_ref[...], s_ref[...]

    def finish(raw, partner, g, g_swapped, scale, o_ref):
        gc, gs = g * cos, g_swapped * sin
        for h in range(MLA_HEADS):
            sl = slice(h * HEAD_PAD, (h + 1) * HEAD_PAD)
            rh = raw[:, sl]
            inv = lax.rsqrt(jnp.sum(rh * rh, axis=-1, keepdims=True) * (1.0 / QK_HEAD) + EPS) * scale
            o_ref[0, :, sl] = ((rh * gc + partner[:, sl] * gs) * inv).astype(o_ref.dtype)

    finish(_dot(ql, wq_ref[...]), _dot(ql, wqr_ref[...]), gq_ref[...], gqp_ref[...], ATTN_SCALE, q_ref)
    finish(_dot(k_in, wk_ref[...]), _dot(k_in, wkr_ref[...]), gk_ref[...], gkp_ref[...], 1.0, k_ref)
    v_ref[0] = _dot(kvl, wv_ref[...]).astype(v_ref.dtype)


def _mla_prep(proj, p, n_ctx, T):
    B, Tt, _ = proj.shape
    H = MLA_HEADS
    ctx_tiles = n_ctx // ROW_TILE
    mla_blk = (PROJ_RW + PROJ_GATE) // PROJ_MLA
    weights = _mla_weights(p)
    cos, sin = _mla_tables(n_ctx, T)
    const = lambda b, i: (0, 0)
    row = lambda b, i: (b, i, 0)
    tab = pl.BlockSpec((ROW_TILE, HEAD_PAD), lambda b, i: (i, 0))
    full = lambda a: pl.BlockSpec(a.shape, const)
    return pl.pallas_call(
        _mla_prep_kernel,
        grid=(B, Tt // ROW_TILE),
        in_specs=[pl.BlockSpec((1, ROW_TILE, PROJ_MLA), lambda b, i: (b, i, mla_blk)), tab, tab,
                  pl.BlockSpec((1, Q_LORA), const), pl.BlockSpec((1, KV_LORA), const)]
                 + [full(w) for w in weights],
        out_specs=[
            pl.BlockSpec((1, ROW_TILE, H * HEAD_PAD), lambda b, i: (b, jnp.maximum(i - ctx_tiles, 0), 0)),
            pl.BlockSpec((1, ROW_TILE, H * HEAD_PAD), row),
            pl.BlockSpec((1, ROW_TILE, H * V_HEAD), row),
        ],
        out_shape=[
            jax.ShapeDtypeStruct((B, T, H * HEAD_PAD), BF16),
            jax.ShapeDtypeStruct((B, Tt, H * HEAD_PAD), BF16),
            jax.ShapeDtypeStruct((B, Tt, H * V_HEAD), BF16),
        ],
        compiler_params=pltpu.CompilerParams(
            dimension_semantics=("arbitrary", "arbitrary"), vmem_limit_bytes=VMEM_LIMIT),
        name="mla_prep",
    )(proj, cos, sin, p['q_lat_norm'].reshape(1, Q_LORA), p['kv_lat_norm'].reshape(1, KV_LORA), *weights)


ATTN_Q_TILE = 1024
HEADS_PER_STEP = LANES // V_HEAD


ATTN_ROW_SPLIT = 8


def _attn_kernel(q_ref, k_ref, v_ref, o_ref):
    v2 = v_ref[0]
    rows = q_ref.shape[1] // ATTN_ROW_SPLIT
    work = [(hh, rs) for rs in range(ATTN_ROW_SPLIT) for hh in range(HEADS_PER_STEP)]

    def scores(hh, rs):
        sl = slice(hh * HEAD_PAD, (hh + 1) * HEAD_PAD)
        return _dot_t(q_ref[0, rs * rows:(rs + 1) * rows, sl], k_ref[0, :, sl])

    outs = {}
    s_next = scores(*work[0])
    for i, (hh, rs) in enumerate(work):
        s = s_next
        if i + 1 < len(work):
            s_next = scores(*work[i + 1])
        e = jnp.exp(s - jnp.max(s, axis=-1, keepdims=True))
        outs[hh, rs] = _dot(e, v2) / jnp.sum(e, axis=-1, keepdims=True)
    lane = lax.broadcasted_iota(jnp.int32, (rows, LANES), 1)
    for rs in range(ATTN_ROW_SPLIT):
        o_ref[0, rs * rows:(rs + 1) * rows] = jnp.where(lane < V_HEAD, outs[0, rs], outs[1, rs])


def _attention(q, k, v):
    B, T, _ = q.shape
    Kt = k.shape[1]
    assert T % ATTN_Q_TILE == 0 and ATTN_Q_TILE % (ATTN_ROW_SPLIT * SUBLANES) == 0
    hp = MLA_HEADS // HEADS_PER_STEP
    qw = HEADS_PER_STEP * HEAD_PAD
    return pl.pallas_call(
        _attn_kernel,
        grid=(B, hp, T // ATTN_Q_TILE),
        in_specs=[
            pl.BlockSpec((1, ATTN_Q_TILE, qw), lambda b, h, i: (b, i, h)),
            pl.BlockSpec((1, Kt, qw), lambda b, h, i: (b, 0, h)),
            pl.BlockSpec((1, Kt, LANES), lambda b, h, i: (b, 0, h)),
        ],
        out_specs=pl.BlockSpec((1, ATTN_Q_TILE, LANES), lambda b, h, i: (b, i, h)),
        out_shape=jax.ShapeDtypeStruct((B, T, MLA_HEADS * V_HEAD), F32),
        compiler_params=pltpu.CompilerParams(
            dimension_semantics=("arbitrary", "arbitrary", "arbitrary"), vmem_limit_bytes=VMEM_LIMIT),
        name="attention",
    )(q, k, v)


HALO = 16
LORA_W = LANES
GATE_W = PROJ_RW - 3 * RW_DIM - 2 * LORA_W


def _head_ones(width, head):
    i = np.arange(width) // head
    return jnp.asarray(i[:, None] == i[None, :], BF16)


def _head_sum(x, ones_bd):
    hi = x.astype(BF16)
    lo = (x - hi.astype(F32)).astype(BF16)
    return (jnp.dot(hi, ones_bd, preferred_element_type=F32)
            + jnp.dot(lo, ones_bd, preferred_element_type=F32))


def _dot3(a, b_hi, b_lo):
    hi = a.astype(BF16)
    lo = (a - hi.astype(F32)).astype(BF16)
    d = lambda u, w: jnp.dot(u, w, preferred_element_type=F32)
    return d(hi, b_hi) + (d(hi, b_lo) + d(lo, b_hi))


def _rwkv_prep_kernel(x_ref, prev_ref, next_ref, conv_ref, kkg_ref, ka_ref, rk_ref, w0_ref, a0_ref,
                      w2h_ref, w2l_ref, a2_ref, g2_ref, ones_ref,
                      r_ref, v_ref, kk_ref, lw_ref, bk_ref, kd_ref, bonus_ref, gg_ref, *, ctx_tiles, n_tiles):
    i = pl.program_id(1)
    x = x_ref[0].astype(F32)
    tm, W = x.shape
    C = RW_DIM
    first = (i == 0) | (i == ctx_tiles)
    last = (i == ctx_tiles - 1) | (i == n_tiles - 1)
    prev_row = jnp.where(first, 0.0, prev_ref[0, HALO - 1:HALO].astype(F32))
    next_row = jnp.where(last, 0.0, next_ref[0, 0:1].astype(F32))
    row = lax.broadcasted_iota(jnp.int32, (tm, W), 0)
    x_dn = jnp.where(row == 0, prev_row, pltpu.roll(x, 1, 0))
    x_up = jnp.where(row == tm - 1, next_row, pltpu.roll(x, tm - 1, 0))
    xc = x_dn * conv_ref[0:1] + x * conv_ref[1:2] + x_up * conv_ref[2:3]

    r, k, v = xc[:, :C], xc[:, C:2 * C], xc[:, 2 * C:3 * C]
    lora_w = jnp.tanh(xc[:, 3 * C:3 * C + LORA_W])
    lora_a = xc[:, 3 * C + LORA_W:3 * C + 2 * LORA_W]
    lg = xc[:, 3 * C + 2 * LORA_W:]
    ones_bd = ones_ref[...]
    kq = k * kkg_ref[...]
    kk = kq * lax.rsqrt(_head_sum(kq * kq, ones_bd) + 1e-12)
    r_ref[0], v_ref[0], kk_ref[0] = r, v, kk

    k_sum = jnp.zeros_like(k)
    for d in range(2):
        z = w0_ref[d:d + 1] + _dot3(lora_w, w2h_ref[d], w2l_ref[d])
        softplus_neg = jnp.maximum(-z, 0.0) + jnp.log(1.0 + jnp.exp(-jnp.abs(z)))
        lw_ref[d, 0] = -jnp.exp(-softplus_neg - 0.5)
        a = jax.nn.sigmoid(a0_ref[d:d + 1] + _dot(lora_a, a2_ref[d]))
        kd = k * (1.0 + (a - 1.0) * ka_ref[...])
        bk_ref[d, 0] = kk * a
        kd_ref[d, 0] = kd
        k_sum = k_sum + kd
    bonus_ref[0] = _head_sum(r * k_sum * rk_ref[...], ones_bd) * v
    gg_ref[0] = _dot(jax.nn.sigmoid(lg), g2_ref[...])


def _rwkv_prep(proj, p, n_ctx):
    B, Tt, _ = proj.shape
    C = RW_DIM
    nt = Tt // ROW_TILE
    hb = ROW_TILE // HALO
    pad_cols = lambda w: jnp.pad(w, ((0, 0), (0, PROJ_RW - w.shape[1])))
    w2 = jnp.stack([jnp.pad(p['decay_w2'][0], ((0, LORA_W - DECAY_LORA), (0, 0))),
                    jnp.pad(p['decay_w2'][1], ((DECAY_LORA, 0), (0, 0)))])
    w2h = w2.astype(BF16)
    w2l = (w2 - w2h.astype(F32)).astype(BF16)
    a2 = jnp.stack([jnp.pad(p['aicl_a2'][0], ((0, LORA_W - AICL_LORA), (0, 0))),
                    jnp.pad(p['aicl_a2'][1], ((AICL_LORA, 0), (0, 0)))]).astype(BF16)
    g2 = jnp.pad(p['gate_g2'], ((0, GATE_W - GATE_LORA), (0, 0))).astype(BF16)
    row = lambda b, i: (b, i, 0)
    drow = lambda b, i: (0, b, i, 0)
    const2 = lambda b, i: (0, 0)
    const3 = lambda b, i: (0, 0, 0)
    vec = pl.BlockSpec((1, C), const2)
    out_row = pl.BlockSpec((1, ROW_TILE, C), row)
    out_dir = pl.BlockSpec((2, 1, ROW_TILE, C), drow)
    sds = jax.ShapeDtypeStruct((B, Tt, C), F32)
    sds2 = jax.ShapeDtypeStruct((2, B, Tt, C), F32)
    return pl.pallas_call(
        functools.partial(_rwkv_prep_kernel, ctx_tiles=n_ctx // ROW_TILE, n_tiles=nt),
        grid=(B, nt),
        in_specs=[
            pl.BlockSpec((1, ROW_TILE, PROJ_RW), row),
            pl.BlockSpec((1, HALO, PROJ_RW), lambda b, i: (b, jnp.maximum(i * hb - 1, 0), 0)),
            pl.BlockSpec((1, HALO, PROJ_RW), lambda b, i: (b, jnp.minimum((i + 1) * hb, nt * hb - 1), 0)),
            pl.BlockSpec((3, PROJ_RW), const2),
            vec, vec, vec,
            pl.BlockSpec((2, C), const2),
            pl.BlockSpec((2, C), const2),
            pl.BlockSpec((2, LORA_W, C), const3),
            pl.BlockSpec((2, LORA_W, C), const3),
            pl.BlockSpec((2, LORA_W, C), const3),
            pl.BlockSpec((GATE_W, C), const2),
            pl.BlockSpec((C, C), const2),
        ],
        out_specs=[out_row, out_row, out_row, out_dir, out_dir, out_dir, out_row, out_row],
        out_shape=[sds, sds, sds, sds2, sds2, sds2, sds, sds],
        compiler_params=pltpu.CompilerParams(
            dimension_semantics=("arbitrary", "arbitrary"), vmem_limit_bytes=VMEM_LIMIT),
        name="rwkv_prep",
    )(proj, proj, proj, pad_cols(p['shift_conv']), p['k_k'].reshape(1, C), p['k_a'].reshape(1, C),
      p['r_k'].reshape(1, C), p['decay_w0'], p['aicl_a0'], w2h, w2l, a2, g2, _head_ones(C, RW_HEAD))


def _merge_kernel(att_ref, yf_ref, yb_ref, bonus_ref, gg_ref, gate_ref, x_ref, ga_ref, gnw_ref, gnb_ref, ones_ref,
                  wm_ref, wr_ref, wo_ref, o_ref):
    D = x_ref.shape[-1]
    ones_bd = ones_ref[...]
    y = yf_ref[0] + yb_ref[0]
    yc = y - _head_sum(y, ones_bd) * (1.0 / RW_HEAD)
    var = _head_sum(yc * yc, ones_bd) * (1.0 / RW_HEAD)
    y_n = yc * lax.rsqrt(var + GN_EPS) * gnw_ref[...] + gnb_ref[...]
    rw = (y_n + bonus_ref[0]) * gg_ref[0]
    g = jax.nn.sigmoid(gate_ref[0].astype(F32))
    mix = g[:, :D] * _dot(att_ref[0], wm_ref[...]) + g[:, D:] * _dot(rw, wr_ref[...])
    o_ref[0] = x_ref[0] + ga_ref[0] * _dot(mix, wo_ref[...])


def _merge(att, y_f, y_b, bonus, gg, proj, x, g_a, p, n_ctx):
    B, T, D = x.shape
    C = RW_DIM
    ctx_tiles = n_ctx // ROW_TILE
    gate_blk = PROJ_RW // PROJ_GATE
    row = lambda b, i: (b, i, 0)
    lat_row = lambda b, i: (b, i + ctx_tiles, 0)
    const = lambda b, i: (0, 0)
    return pl.pallas_call(
        _merge_kernel,
        grid=(B, T // ROW_TILE),
        in_specs=[
            pl.BlockSpec((1, ROW_TILE, att.shape[-1]), row),
            pl.BlockSpec((1, ROW_TILE, C), row),
            pl.BlockSpec((1, ROW_TILE, C), row),
            pl.BlockSpec((1, ROW_TILE, C), lat_row),
            pl.BlockSpec((1, ROW_TILE, C), lat_row),
            pl.BlockSpec((1, ROW_TILE, PROJ_GATE), lambda b, i: (b, i + ctx_tiles, gate_blk)),
            pl.BlockSpec((1, ROW_TILE, D), row),
            pl.BlockSpec((1, 1, D), lambda b, i: (b, 0, 0)),
            pl.BlockSpec((1, C), const),
            pl.BlockSpec((1, C), const),
            pl.BlockSpec((C, C), const),
            pl.BlockSpec(p['w_o_mla'].shape, const),
            pl.BlockSpec(p['w_o_rwkv'].shape, const),
            pl.BlockSpec(p['w_out'].shape, const),
        ],
        out_specs=pl.BlockSpec((1, ROW_TILE, D), row),
        out_shape=jax.ShapeDtypeStruct((B, T, D), F32),
        compiler_params=pltpu.CompilerParams(
            dimension_semantics=("arbitrary", "arbitrary"), vmem_limit_bytes=VMEM_LIMIT),
        name="merge",
    )(att, y_f, y_b, bonus, gg, proj, x, g_a[:, None, :], p['gn_w'].reshape(1, C), p['gn_b'].reshape(1, C),
      _head_ones(C, RW_HEAD), p['w_o_mla'].astype(BF16), p['w_o_rwkv'].astype(BF16), p['w_out'].astype(BF16))


TILE_ROWS = 8
MOE_BLOCK = 256
ZERO_ROWS = MOE_BLOCK + TILE_ROWS


NEG_INF = float("-inf")
HI_MASK = 0xFFFF0000


def _pack_pair(lo, hi):
    lo_b = lax.bitcast_convert_type(lo.astype(BF16).astype(F32), jnp.uint32)
    hi_b = lax.bitcast_convert_type(hi.astype(BF16).astype(F32), jnp.uint32)
    return (lo_b >> 16) | (hi_b & jnp.uint32(HI_MASK))


def _unpack_pair(w):
    lo = lax.bitcast_convert_type(w << 16, F32)
    hi = lax.bitcast_convert_type(w & jnp.uint32(HI_MASK), F32)
    return lo.astype(BF16), hi.astype(BF16)


def _row_max(x):
    return jnp.max(x, axis=-1, keepdims=True)


def _first_index_of(x, value, lane_f):
    return jnp.min(jnp.where(x == value, lane_f, float(x.shape[-1])), axis=-1, keepdims=True)


def _router_kernel(x_ref, gain_ref, sh_ref, sc_ref, wh_ref, wm_ref, wl_ref, bias_ref,
                   u_ref, ids_ref, gates_ref, ranks_ref, counts_ref, carry_ref):
    @pl.when((pl.program_id(0) == 0) & (pl.program_id(1) == 0))
    def _():
        carry_ref[...] = jnp.zeros_like(carry_ref)

    x = x_ref[0]
    tm, D = x.shape
    E = bias_ref.shape[-1]
    u = x * lax.rsqrt(jnp.mean(x * x, axis=-1, keepdims=True) + EPS) * gain_ref[...]
    u = u * (1.0 + sc_ref[0]) + sh_ref[0]
    u_ref[0] = _pack_pair(u[:, :D // 2], u[:, D // 2:])

    uh, um, ul = _split3(u)
    wh, wm, wl = wh_ref[...], wm_ref[...], wl_ref[...]
    d = lambda a, b: jnp.dot(a, b, preferred_element_type=F32)
    logits = d(uh, wh) + (d(uh, wm) + d(um, wh)) + (d(uh, wl) + d(um, wm) + d(ul, wh))
    scores = jax.nn.sigmoid(logits)
    sel = scores + bias_ref[...]

    lane_i = lax.broadcasted_iota(jnp.int32, (tm, E), 1)
    lane_f = lane_i.astype(F32)
    out_f = lax.broadcasted_iota(jnp.int32, (tm, LANES), 1).astype(F32)
    per_group = E // N_GROUPS
    grp_f = jnp.floor(lane_f * (1.0 / per_group))

    gs = jnp.full((tm, LANES), NEG_INF, F32)
    for g in range(N_GROUPS):
        sg = jnp.where(lane_i >= g * per_group, jnp.where(lane_i < (g + 1) * per_group, sel, NEG_INF), NEG_INF)
        m1 = _row_max(sg)
        i1 = _first_index_of(sg, m1, lane_f)
        m2 = _row_max(jnp.where(lane_f == i1, NEG_INF, sg))
        gs = jnp.where(out_f == g, m1 + m2, gs)

    allow = jnp.zeros((tm, E), F32)
    for _ in range(TOPK_GROUPS):
        m = _row_max(gs)
        i = _first_index_of(gs, m, out_f)
        gs = jnp.where(out_f == i, NEG_INF, gs)
        allow = jnp.where(grp_f == i, 1.0, allow)
    selm = jnp.where(allow > 0.0, sel, NEG_INF)

    ids = jnp.zeros((tm, LANES), F32)
    gts = jnp.zeros((tm, LANES), F32)
    member = jnp.zeros((tm, E), F32)
    idx_cols = []
    gsum = jnp.zeros((tm, 1), F32)
    for k in range(TOP_K):
        m = _row_max(selm)
        i = _first_index_of(selm, m, lane_f)
        hit = lane_f == i
        gk = jnp.sum(jnp.where(hit, scores, 0.0), axis=-1, keepdims=True)
        selm = jnp.where(hit, NEG_INF, selm)
        member = jnp.where(hit, 1.0, member)
        ids = jnp.where(out_f == k, i, ids)
        gts = jnp.where(out_f == k, gk, gts)
        idx_cols.append(i)
        gsum = gsum + gk
    gts = gts / gsum * ROUTED_SCALE

    r2 = lax.broadcasted_iota(jnp.int32, (tm, tm), 0)
    c2 = lax.broadcasted_iota(jnp.int32, (tm, tm), 1)
    before = jnp.where(r2 > c2, 1.0, 0.0).astype(BF16)
    mem_b = member.astype(BF16)
    carry = carry_ref[...]
    pos = carry + jnp.dot(before, mem_b, preferred_element_type=F32)
    rk = jnp.zeros((tm, LANES), F32)
    for k in range(TOP_K):
        rk = jnp.where(out_f == k, jnp.sum(jnp.where(lane_f == idx_cols[k], pos, 0.0), axis=-1, keepdims=True), rk)
    colsum = jnp.dot(jnp.ones((8, tm), BF16), mem_b, preferred_element_type=F32)[0:1]
    carry_ref[...] = carry + colsum
    counts_ref[...] = carry + colsum
    ids_ref[0] = ids.astype(jnp.int32)
    ranks_ref[0] = rk.astype(jnp.int32)
    gates_ref[0] = gts


def _router(x1, sh_m, sc_m, p):
    B, T, D = x1.shape
    E = N_EXPERTS
    wh, wm, wl = _split3(p['router_w'])
    row = lambda b, i: (b, i, 0)
    const = lambda b, i: (0, 0)
    vec = lambda b, i: (b, 0, 0)
    lane_out = lambda dt: jax.ShapeDtypeStruct((B, T, LANES), dt)
    return pl.pallas_call(
        _router_kernel,
        grid=(B, T // ROW_TILE),
        in_specs=[
            pl.BlockSpec((1, ROW_TILE, D), row),
            pl.BlockSpec((1, D), const),
            pl.BlockSpec((1, 1, D), vec),
            pl.BlockSpec((1, 1, D), vec),
            pl.BlockSpec((D, E), const),
            pl.BlockSpec((D, E), const),
            pl.BlockSpec((D, E), const),
            pl.BlockSpec((1, E), const),
        ],
        out_specs=[
            pl.BlockSpec((1, ROW_TILE, D // 2), row),
            pl.BlockSpec((1, ROW_TILE, LANES), row),
            pl.BlockSpec((1, ROW_TILE, LANES), row),
            pl.BlockSpec((1, ROW_TILE, LANES), row),
            pl.BlockSpec((1, E), const),
        ],
        out_shape=[
            jax.ShapeDtypeStruct((B, T, D // 2), jnp.uint32),
            lane_out(jnp.int32), lane_out(F32), lane_out(jnp.int32),
            jax.ShapeDtypeStruct((1, E), F32),
        ],
        scratch_shapes=[pltpu.VMEM((1, E), F32)],
        compiler_params=pltpu.CompilerParams(
            dimension_semantics=("arbitrary", "arbitrary"), vmem_limit_bytes=VMEM_LIMIT),
        name="router",
    )(x1, p['norm_ffn'].reshape(1, D), sh_m[:, None, :], sc_m[:, None, :], wh, wm, wl,
      p['router_bias'].reshape(1, E))


def _slot_kernel(ids_ref, ranks_ref, base_ref, o_ref):
    ids = ids_ref[...].astype(F32)
    tm = ids.shape[0]
    E = base_ref.shape[-1]
    lane_e = lax.broadcasted_iota(jnp.int32, (tm, E), 1).astype(F32)
    out_lane = lax.broadcasted_iota(jnp.int32, (tm, LANES), 1)
    first = jnp.zeros((tm, LANES), F32)
    for k in range(TOP_K):
        fk = jnp.sum(jnp.where(lane_e == ids[:, k:k + 1], base_ref[...], 0.0), axis=-1, keepdims=True)
        first = jnp.where(out_lane == k, fk, first)
    o_ref[...] = first.astype(jnp.int32) + ranks_ref[...]


def _slots(ids, ranks, base):
    n_tok = ids.shape[0]
    E = base.shape[-1]
    row = pl.BlockSpec((ROW_TILE, LANES), lambda i: (i, 0))
    return pl.pallas_call(
        _slot_kernel,
        grid=(n_tok // ROW_TILE,),
        in_specs=[row, row, pl.BlockSpec((1, E), lambda i: (0, 0))],
        out_specs=row,
        out_shape=jax.ShapeDtypeStruct((n_tok, LANES), jnp.int32),
        compiler_params=pltpu.CompilerParams(dimension_semantics=("arbitrary",)),
        name="slots",
    )(ids, ranks, base)


DMA_PRIORITIES = 2


def _row_copy(src, src_row, dst, dst_row, sem):
    return pltpu.make_async_copy(src.at[pl.ds(src_row, 1)], dst.at[pl.ds(dst_row, 1)], sem)


def _dispatch_kernel(pad_ref, dest_ref, u_ref, slots_ref, zero_ref, sem):
    tm = u_ref.shape[0]
    n_experts = pad_ref.shape[0]

    @pl.when(pl.program_id(0) == 0)
    def _():
        zero_ref[...] = jnp.zeros_like(zero_ref)

        def fill(e, carry):
            start = pl.multiple_of(pad_ref[e], TILE_ROWS)
            pltpu.make_async_copy(zero_ref, slots_ref.at[pl.ds(start, ZERO_ROWS)], sem).start()
            return carry

        lax.fori_loop(0, n_experts, fill, 0)

        def filled(e, carry):
            pltpu.make_async_copy(zero_ref, slots_ref.at[pl.ds(0, ZERO_ROWS)], sem).wait()
            return carry

        lax.fori_loop(0, n_experts, filled, 0)

    def issue(r, carry):
        for k in range(TOP_K):
            _row_copy(u_ref, r, slots_ref, dest_ref[0, 0, r * TOP_K + k], sem).start(priority=k % DMA_PRIORITIES)
        return carry

    lax.fori_loop(0, tm, issue, 0)
    for k in range(TOP_K):
        pltpu.make_async_copy(u_ref, slots_ref.at[pl.ds(0, tm)], sem).wait()


def _dispatch(pad_start, dest, u_rows, n_slots):
    n_tok, W = u_rows.shape
    nt = n_tok // ROW_TILE
    grid_spec = pltpu.PrefetchScalarGridSpec(
        num_scalar_prefetch=1,
        grid=(nt,),
        in_specs=[
            pl.BlockSpec((1, 1, ROW_TILE * TOP_K), lambda i, pad: (i, 0, 0), memory_space=pltpu.SMEM),
            pl.BlockSpec((ROW_TILE, W), lambda i, pad: (i, 0)),
        ],
        out_specs=pl.BlockSpec(memory_space=pl.ANY),
        scratch_shapes=[pltpu.VMEM((ZERO_ROWS, W), jnp.uint32), pltpu.SemaphoreType.DMA(())],
    )
    return pl.pallas_call(
        _dispatch_kernel,
        grid_spec=grid_spec,
        out_shape=jax.ShapeDtypeStruct((n_slots + ZERO_ROWS, W), jnp.uint32),
        compiler_params=pltpu.CompilerParams(
            dimension_semantics=("arbitrary",), vmem_limit_bytes=VMEM_LIMIT),
        name="dispatch",
    )(pad_start, dest, u_rows)


WEIGHT_SLOTS = 2


def _expert_weight_copies(hbm_refs, stage_refs, expert, slot, sems):
    return [pltpu.make_async_copy(w.at[expert], s.at[slot], sems.at[slot, i])
            for i, (w, s) in enumerate(zip(hbm_refs, stage_refs))]


def _moe_ffn_kernel(be_ref, first_ref, slot_ref, next_ref, nu_ref, x_ref, w1_ref, w3_ref, w2_ref, o_ref,
                    s1_ref, s3_ref, s2_ref, w13_ref, w2b_ref, sems):
    j = pl.program_id(0)
    F = w1_ref.shape[-1]
    half = x_ref.shape[-1]
    copies = functools.partial(_expert_weight_copies, (w1_ref, w3_ref, w2_ref), (s1_ref, s3_ref, s2_ref), sems=sems)

    @pl.when(j == 0)
    def _():
        for c in copies(be_ref[0], 0):
            c.start()

    @pl.when(first_ref[j] == 1)
    def _():
        slot = slot_ref[j]
        for c in copies(be_ref[j], slot):
            c.wait()

        @pl.when(next_ref[j] >= 0)
        def _():
            for c in copies(next_ref[j], 1 - slot):
                c.start()

        w13_ref[:, :F] = s1_ref[slot].astype(BF16)
        w13_ref[:, F:] = s3_ref[slot].astype(BF16)
        w2b_ref[...] = s2_ref[slot].astype(BF16)

    @pl.when(j < nu_ref[0])
    def _():
        lo, hi = _unpack_pair(x_ref[...])
        h = (jnp.dot(lo, w13_ref[:half], preferred_element_type=F32)
             + jnp.dot(hi, w13_ref[half:], preferred_element_type=F32))
        h1, h3 = h[:, :F], h[:, F:]
        y = _dot(h1 * jax.nn.sigmoid(h1) * h3, w2b_ref[...])
        o_ref[...] = pltpu.einshape("s(cl)->(sc)l", y, c=TILE_ROWS)

    @pl.when(j >= nu_ref[0])
    def _():
        o_ref[...] = jnp.zeros_like(o_ref)


def _moe_ffn(block_e, n_used, x_slots, w1, w3, w2):
    W = x_slots.shape[1]
    E, D, F = w1.shape
    nblk = block_e.shape[0]
    slots = nblk * MOE_BLOCK
    idx = jnp.arange(nblk)
    first = jnp.concatenate([jnp.ones((1,), bool), block_e[1:] != block_e[:-1]])
    slot = ((jnp.cumsum(first) - 1) % WEIGHT_SLOTS).astype(jnp.int32)
    later_first = lax.cummin(jnp.where(first, idx, nblk)[::-1])[::-1]
    next_first = jnp.concatenate([later_first[1:], jnp.full((1,), nblk)])
    next_e = jnp.where(next_first < nblk, block_e[jnp.minimum(next_first, nblk - 1)], -1).astype(jnp.int32)
    n_prefetch = 5
    blk = lambda j, *_: (jnp.minimum(j, _[4][0] - 1), 0)
    grid_spec = pltpu.PrefetchScalarGridSpec(
        num_scalar_prefetch=n_prefetch,
        grid=(nblk,),
        in_specs=[
            pl.BlockSpec((MOE_BLOCK, W), blk),
            pl.BlockSpec(memory_space=pl.ANY),
            pl.BlockSpec(memory_space=pl.ANY),
            pl.BlockSpec(memory_space=pl.ANY),
        ],
        out_specs=pl.BlockSpec((MOE_BLOCK * TILE_ROWS, LANES), lambda j, *_: (j, 0)),
        scratch_shapes=[pltpu.VMEM((WEIGHT_SLOTS, D, F), F32), pltpu.VMEM((WEIGHT_SLOTS, D, F), F32),
                        pltpu.VMEM((WEIGHT_SLOTS, F, D), F32),
                        pltpu.VMEM((D, 2 * F), BF16), pltpu.VMEM((F, D), BF16),
                        pltpu.SemaphoreType.DMA((WEIGHT_SLOTS, 3))],
    )
    return pl.pallas_call(
        _moe_ffn_kernel,
        grid_spec=grid_spec,
        out_shape=jax.ShapeDtypeStruct((slots * TILE_ROWS, LANES), F32),
        compiler_params=pltpu.CompilerParams(
            dimension_semantics=("arbitrary",), vmem_limit_bytes=VMEM_LIMIT),
        name="moe_ffn",
    )(block_e, first.astype(jnp.int32), slot, next_e, n_used, x_slots, w1, w3, w2)


COMBINE_ROWS = 32


def _combine_kernel(dest_ref, gates_ref, u_ref, x_ref, gm_ref, w1_ref, w3_ref, w2_ref, ys_ref, o_ref, buf_ref, sem):
    tm, half = u_ref.shape

    def tile(i):
        return pl.ds(pl.multiple_of(i * TILE_ROWS, TILE_ROWS), TILE_ROWS)

    def issue(r, carry):
        for k in range(TOP_K):
            pltpu.make_async_copy(ys_ref.at[tile(dest_ref[0, 0, r * TOP_K + k])], buf_ref.at[k, tile(r)], sem).start(
                priority=k % DMA_PRIORITIES)
        return carry

    lax.fori_loop(0, tm, issue, 0)

    ulo, uhi = _unpack_pair(u_ref[...])
    both = lambda w_ref: (jnp.dot(ulo, w_ref[:half], preferred_element_type=F32)
                          + jnp.dot(uhi, w_ref[half:], preferred_element_type=F32))
    h1, h3 = both(w1_ref), both(w3_ref)
    o_ref[...] = x_ref[...] + gm_ref[0] * _dot(h1 * jax.nn.sigmoid(h1) * h3, w2_ref[...])

    for k in range(TOP_K):
        pltpu.make_async_copy(ys_ref.at[pl.ds(0, tm * TILE_ROWS)], buf_ref.at[k], sem).wait()

    gm = gm_ref[0]
    for rb in range(tm // COMBINE_ROWS):
        rows = slice(rb * COMBINE_ROWS, (rb + 1) * COMBINE_ROWS)
        gates = gates_ref[rows, :]
        for c in range(TILE_ROWS):
            acc = jnp.zeros((COMBINE_ROWS, LANES), F32)
            for k in range(TOP_K):
                acc = acc + gates[:, k:k + 1] * buf_ref[
                    k, pl.ds(rb * COMBINE_ROWS * TILE_ROWS + c, COMBINE_ROWS, stride=TILE_ROWS), :]
            cols = slice(c * LANES, (c + 1) * LANES)
            o_ref[rows, cols] = o_ref[rows, cols] + gm[:, cols] * acc


def _combine(dest, gates, u_rows, x1, g_m, y_slots, p):
    B, T, D = x1.shape
    n_tok = B * T
    W = u_rows.shape[1]
    tiles_per_batch = T // ROW_TILE
    row = lambda i: (i, 0)
    const = lambda i: (0, 0)
    out = pl.pallas_call(
        _combine_kernel,
        grid=(n_tok // ROW_TILE,),
        in_specs=[
            pl.BlockSpec((1, 1, ROW_TILE * TOP_K), lambda i: (i, 0, 0), memory_space=pltpu.SMEM),
            pl.BlockSpec((ROW_TILE, LANES), row),
            pl.BlockSpec((ROW_TILE, W), row),
            pl.BlockSpec((ROW_TILE, D), row),
            pl.BlockSpec((1, 1, D), lambda i: (i // tiles_per_batch, 0, 0)),
            pl.BlockSpec(p['shared_w1'].shape, const),
            pl.BlockSpec(p['shared_w3'].shape, const),
            pl.BlockSpec(p['shared_w2'].shape, const),
            pl.BlockSpec(memory_space=pl.ANY),
        ],
        out_specs=pl.BlockSpec((ROW_TILE, D), row),
        out_shape=jax.ShapeDtypeStruct((n_tok, D), F32),
        scratch_shapes=[pltpu.VMEM((TOP_K, ROW_TILE * TILE_ROWS, LANES), F32), pltpu.SemaphoreType.DMA(())],
        compiler_params=pltpu.CompilerParams(
            dimension_semantics=("arbitrary",), vmem_limit_bytes=VMEM_LIMIT),
        name="combine",
    )(dest, gates.reshape(n_tok, LANES), u_rows, x1.reshape(n_tok, D), g_m[:, None, :],
      p['shared_w1'].astype(BF16), p['shared_w3'].astype(BF16), p['shared_w2'].astype(BF16), y_slots)
    return out.reshape(B, T, D)


def _moe(x1, sh_m, sc_m, g_m, p):
    B, T, D = x1.shape
    n_tok = B * T
    u_packed, ids, gates, ranks, counts = _router(x1, sh_m, sc_m, p)

    counts = counts[0].astype(jnp.int32)
    padded = (counts + MOE_BLOCK - 1) // MOE_BLOCK * MOE_BLOCK
    padded_end = jnp.cumsum(padded)
    base = padded_end - padded
    n_blocks = n_tok * TOP_K // MOE_BLOCK + N_EXPERTS
    n_used = (padded_end[-1] // MOE_BLOCK).astype(jnp.int32)
    blk = jnp.minimum(jnp.arange(n_blocks), n_used - 1) * MOE_BLOCK
    block_e = jnp.minimum(jnp.sum(padded_end[None, :] <= blk[:, None], axis=1), N_EXPERTS - 1).astype(jnp.int32)
    dest = _slots(ids.reshape(n_tok, LANES), ranks.reshape(n_tok, LANES), base.astype(F32).reshape(1, N_EXPERTS))
    dest = dest[:, :TOP_K].reshape(n_tok // ROW_TILE, 1, ROW_TILE * TOP_K)

    u_rows = u_packed.reshape(n_tok, D // 2)
    pad_start = ((base + counts) // TILE_ROWS * TILE_ROWS).astype(jnp.int32)
    x_slots = _dispatch(pad_start, dest, u_rows, n_blocks * MOE_BLOCK)
    y_slots = _moe_ffn(block_e, n_used.reshape(1), x_slots, p['expert_w1'], p['expert_w3'], p['expert_w2'])
    return _combine(dest, gates, u_rows, x1, g_m, y_slots, p)
```

```python
import functools

import jax
import jax.numpy as jnp
import numpy as np
from jax import lax
from jax.experimental import pallas as pl
from jax.experimental.pallas import tpu as pltpu

F32 = jnp.float32
BF16 = jnp.bfloat16

GRID_W = 64
EPS = 1e-6
MLA_HEADS = 8
QK_NOPE = 64
QK_ROPE = 32
QK_HEAD = QK_NOPE + QK_ROPE
V_HEAD = 64
Q_LORA = 256
KV_LORA = 128
ROPE_BASE = 10000.0
ATTN_SCALE = QK_HEAD ** -0.5
RW_HEADS = 8
RW_HEAD = 64
RW_DIM = RW_HEADS * RW_HEAD
DECAY_LORA = 64
AICL_LORA = 64
GATE_LORA = 160
GN_EPS = 64e-5
N_EXPERTS = 256
TOP_K = 8
N_GROUPS = 8
TOPK_GROUPS = 4
ROUTED_SCALE = 2.5

LANES = 128
MXU_DIM = 256
VMEM_LIMIT = 56 * 1024 * 1024

WKV_CHUNK = 64
WKV_GROUP = 128
HEADS_PER_GROUP = WKV_GROUP // RW_HEAD
HEAD_PAD = LANES


def _dot(a, b):
    return jnp.dot(a.astype(BF16), b.astype(BF16), preferred_element_type=F32)


def _dot_t(a, b):
    return lax.dot_general(a.astype(BF16), b.astype(BF16), (((1,), (1,)), ((), ())),
                           preferred_element_type=F32)


def _split3(x):
    h = x.astype(BF16)
    r1 = x - h.astype(F32)
    m = r1.astype(BF16)
    lo = (r1 - m.astype(F32)).astype(BF16)
    return h, m, lo


def _dot_hi(a_bf16_exact, x):
    h, m, lo = _split3(x)
    d = lambda y: jnp.dot(a_bf16_exact, y, preferred_element_type=F32)
    return d(h) + d(m) + d(lo)


def _wkv_chunk(r, v, kk, lw, bk, kd, s0, reverse, emit):
    L, G = r.shape
    row = lax.broadcasted_iota(jnp.int32, (L, G), 0)
    lane = lax.broadcasted_iota(jnp.int32, (L, G), 1)
    diff = (lane % L - row) if reverse else (row - lane % L)
    strict = diff > 0
    incl = diff >= 0
    r2 = lax.broadcasted_iota(jnp.int32, (L, L), 0)
    c2 = lax.broadcasted_iota(jnp.int32, (L, L), 1)
    tri = jnp.where(((c2 - r2) if reverse else (r2 - c2)) >= 0, 1.0, 0.0).astype(BF16)
    tri_ones = jnp.concatenate([tri, jnp.ones((L, L), BF16)], axis=0)
    lane_head = lane // RW_HEAD
    rowb = lax.broadcasted_iota(jnp.int32, (G, G), 0) // RW_HEAD
    colb = lax.broadcasted_iota(jnp.int32, (G, G), 1) // RW_HEAD

    def stack(x):
        return jnp.concatenate(
            [jnp.where(lane_head == h, x, 0.0) for h in range(HEADS_PER_GROUP)], axis=0).astype(BF16)

    sums = _dot_hi(tri_ones, lw)
    cum_in = sums[:L]
    g_in = jnp.exp(cum_in)
    g_inv = jnp.exp(-cum_in)
    g_ex = jnp.exp(cum_in - lw)
    g_tot = jnp.exp(sums[L:])
    a_h = -kk * g_ex
    b_h = bk * g_inv
    k_h = kd * g_inv
    r_h = r * g_in
    yield

    gram = _dot_t(jnp.concatenate([a_h, r_h], axis=0),
                  jnp.concatenate([stack(b_h), stack(k_h)], axis=0))
    ab = jnp.where(strict, gram[:L, :G], 0.0)
    ak = jnp.where(strict, gram[:L, G:], 0.0)
    rb = jnp.where(incl, gram[L:, :G], 0.0)
    rk = jnp.where(incl, gram[L:, G:], 0.0)
    yield

    tm = jnp.where(diff == 0, 1.0, 0.0) + ab
    p = ab
    v_st = stack(v)
    akv = _dot(ak, v_st)
    for _ in range(int(np.log2(L)) - 1):
        p = _dot(p, stack(p))
        yield
        tm = tm + _dot(tm, stack(p))
        yield

    wu = _dot(tm, jnp.concatenate([stack(a_h), stack(akv)], axis=1))
    w_t, u_t = wu[:, :G], wu[:, G:]
    yield
    rbwu = _dot(rb, jnp.concatenate([stack(w_t), stack(u_t)], axis=1))
    r_t = r_h + rbwu[:, :G]
    y_t = rbwu[:, G:] + _dot(rk, v_st)
    yield

    s0_st = stack(s0)
    y = _dot_t(r_t, s0_st) + y_t
    u = _dot_t(w_t, s0_st) + u_t
    yield
    uv = jnp.concatenate([u, v], axis=0)
    bkc = jnp.concatenate([b_h, k_h], axis=0)
    upd = lax.dot_general(uv.astype(BF16), bkc.astype(BF16), (((0,), (0,)), ((), ())),
                          preferred_element_type=F32)
    upd = jnp.where(rowb == colb, upd, 0.0)
    upd_d = upd[0:L]
    for h in range(1, HEADS_PER_GROUP):
        upd_d = upd_d + upd[h * L:(h + 1) * L]
    emit(y, (s0 + upd_d) * g_tot)


def _wkv_kernel(*refs, n_groups, n_batch):
    ins, (yf_ref, yb_ref, s_ref) = refs[:12], refs[12:]
    G = WKV_GROUP

    @pl.when(pl.program_id(1) == 0)
    def _():
        s_ref[...] = jnp.zeros_like(s_ref)

    chains = []
    for d, y_ref in enumerate((yf_ref, yb_ref)):
        r_ref, v_ref, kk_ref, lw_ref, bk_ref, kd_ref = ins[6 * d:6 * d + 6]
        for bi in range(n_batch):
            for g in range(n_groups):
                sl = slice(g * G, (g + 1) * G)

                def emit(y, s_new, y_ref=y_ref, d=d, bi=bi, g=g, sl=sl):
                    y_ref[bi, :, sl] = y
                    s_ref[d, bi, g] = s_new

                chains.append(_wkv_chunk(
                    r_ref[bi, :, sl], v_ref[bi, :, sl], kk_ref[bi, :, sl], lw_ref[0, bi, :, sl],
                    bk_ref[0, bi, :, sl], kd_ref[0, bi, :, sl], s_ref[d, bi, g], d == 1, emit))
    while chains:
        chains = [c for c in chains if next(c, StopIteration) is not StopIteration]


WKV_BATCH = 2


def _wkv_scan(r, v, kk, lw, bk, kd, n_ctx):
    B, Ttot, C = r.shape
    L = WKV_CHUNK
    nc = Ttot // L
    ncc = n_ctx // L
    nl = nc - ncc
    n_groups = C // WKV_GROUP
    nb = WKV_BATCH

    cid = (lambda s: s, lambda s: jnp.where(s < ncc, ncc - 1 - s, nc + ncc - 1 - s))
    first_out = (0, nl - 1)
    in_specs, out_specs = [], []
    for d in range(2):
        shared = pl.BlockSpec((nb, L, C), lambda b, s, d=d: (b, cid[d](s), 0))
        per_dir = pl.BlockSpec((1, nb, L, C), lambda b, s, d=d: (d, b, cid[d](s), 0))
        in_specs += [shared, shared, shared, per_dir, per_dir, per_dir]
        out_specs.append(pl.BlockSpec(
            (nb, L, C), lambda b, s, d=d: (b, jnp.where(s < ncc, first_out[d], cid[d](s) - ncc), 0)))
    out = jax.ShapeDtypeStruct((B, nl * L, C), F32)
    return pl.pallas_call(
        functools.partial(_wkv_kernel, n_groups=n_groups, n_batch=nb),
        grid=(B // nb, nc),
        in_specs=in_specs,
        out_specs=out_specs,
        out_shape=[out, out],
        scratch_shapes=[pltpu.VMEM((2, nb, n_groups, L, WKV_GROUP), F32)],
        compiler_params=pltpu.CompilerParams(
            dimension_semantics=("arbitrary", "arbitrary"), vmem_limit_bytes=VMEM_LIMIT),
        name="wkv_scan",
    )(r, v, kk, lw, bk, kd, r, v, kk, lw, bk, kd)


def kernel(x, c, ctx, c_ctx, ada_w, ada_b, norm_mix, norm_ffn, w_in, shift_conv, q_lat_norm, w_uq, kv_lat_norm, w_ukv, q_norm, k_norm, w_o_mla, decay_w0, decay_w2, aicl_a0, aicl_a2, k_k, k_a, r_k, gn_w, gn_b, gate_g2, w_o_rwkv, w_out, router_w, router_bias, expert_w1, expert_w3, expert_w2, shared_w1, shared_w3, shared_w2):
    B, T, D = x.shape
    n_ctx = ctx.shape[1]
    i0 = 0
    p = dict(norm_mix=norm_mix[i0], norm_ffn=norm_ffn[i0], w_in=w_in[i0], shift_conv=shift_conv[i0],
             q_lat_norm=q_lat_norm[i0], w_uq=w_uq[i0], kv_lat_norm=kv_lat_norm[i0], w_ukv=w_ukv[i0],
             q_norm=q_norm[i0], k_norm=k_norm[i0], w_o_mla=w_o_mla[i0],
             decay_w0=decay_w0[i0], decay_w2=decay_w2[i0], aicl_a0=aicl_a0[i0], aicl_a2=aicl_a2[i0],
             k_k=k_k[i0], k_a=k_a[i0], r_k=r_k[i0], gn_w=gn_w[i0], gn_b=gn_b[i0], gate_g2=gate_g2[i0],
             w_o_rwkv=w_o_rwkv[i0], w_out=w_out[i0], router_w=router_w[i0], router_bias=router_bias[i0],
             expert_w1=expert_w1[i0], expert_w3=expert_w3[i0], expert_w2=expert_w2[i0],
             shared_w1=shared_w1[i0], shared_w3=shared_w3[i0], shared_w2=shared_w2[i0])

    mod = _ada_modulation(c, c_ctx, ada_w[i0], ada_b[i0])
    sh_a, sc_a, g_a, sh_m, sc_m, g_m = jnp.split(mod[:B], 6, axis=-1)
    csh_a, csc_a = jnp.split(mod[B], 6, axis=-1)[:2]
    sh2 = jnp.stack([jnp.broadcast_to(csh_a, (B, D)), sh_a], axis=1)
    sc2 = jnp.stack([jnp.broadcast_to(csc_a, (B, D)), sc_a], axis=1)

    proj = _in_proj(ctx, x, sh2, sc2, p['norm_mix'], _pack_w_in(p['w_in']))

    q, k, v = _mla_prep(proj, p, n_ctx, T)
    att = _attention(q, k, v)

    r, vv, kk, lw, bk, kd, bonus, gg = _rwkv_prep(proj, p, n_ctx)
    y_f, y_b = _wkv_scan(r, vv, kk, lw, bk, kd, n_ctx)

    x1 = _merge(att, y_f, y_b, bonus, gg, proj, x, g_a, p, n_ctx)
    return _moe(x1, sh_m, sc_m, g_m, p)


SUBLANES = 8


def _ada_kernel(c_ref, w_ref, b_ref, o_ref):
    cc = c_ref[...]
    w = w_ref[...]
    w_hi = w.astype(BF16)
    w_lo = (w - w_hi.astype(F32)).astype(BF16)
    o_ref[...] = _dot3(cc * jax.nn.sigmoid(cc), w_hi, w_lo) + b_ref[...]


def _ada_modulation(c, c_ctx, ada_w, ada_b):
    B, D = c.shape
    n_out = ada_w.shape[1]
    rows = -(-(B + 1) // SUBLANES) * SUBLANES
    c_all = jnp.concatenate([c, c_ctx[None, :], jnp.zeros((rows - B - 1, D), F32)], axis=0)
    out = pl.pallas_call(
        _ada_kernel,
        grid=(n_out // D,),
        in_specs=[pl.BlockSpec((rows, D), lambda j: (0, 0)),
                  pl.BlockSpec((D, D), lambda j: (0, j)),
                  pl.BlockSpec((1, D), lambda j: (0, j))],
        out_specs=pl.BlockSpec((rows, D), lambda j: (0, j)),
        out_shape=jax.ShapeDtypeStruct((rows, n_out), F32),
        compiler_params=pltpu.CompilerParams(dimension_semantics=("arbitrary",), vmem_limit_bytes=VMEM_LIMIT),
        name="ada_modulation",
    )(c_all, ada_w, ada_b.reshape(1, n_out))
    return out[:B + 1]


ROW_TILE = 256
PROJ_RW = 2048
PROJ_GATE = 2048
PROJ_MLA = 512
PROJ_W = PROJ_RW + PROJ_GATE + PROJ_MLA
MLA_IN = Q_LORA + KV_LORA + QK_ROPE
RW_SPLITS = (RW_DIM, RW_DIM, RW_DIM, DECAY_LORA, DECAY_LORA, AICL_LORA, AICL_LORA, GATE_LORA)
RW_IN = sum(RW_SPLITS)


def _pack_w_in(w_in):
    w_mla = w_in[:, :MLA_IN]
    w_rw = w_in[:, MLA_IN:MLA_IN + RW_IN]
    w_gate = w_in[:, MLA_IN + RW_IN:]
    pad = lambda w, n: jnp.pad(w, ((0, 0), (0, n - w.shape[1])))
    return jnp.concatenate([pad(w_rw, PROJ_RW), w_gate, pad(w_mla, PROJ_MLA)], axis=1).astype(BF16)


def _in_proj_kernel(ctx_ref, x_ref, sh_ref, sc_ref, gain_ref, w_ref, o_ref):
    is_ctx = pl.program_id(1) == 0
    xt = jnp.where(is_ctx, ctx_ref[0], x_ref[0])
    sh = jnp.where(is_ctx, sh_ref[0, 0:1], sh_ref[0, 1:2])
    sc = jnp.where(is_ctx, sc_ref[0, 0:1], sc_ref[0, 1:2])
    y = xt * lax.rsqrt(jnp.mean(xt * xt, axis=-1, keepdims=True) + EPS) * gain_ref[...]
    h = y * (1.0 + sc) + sh
    o_ref[0] = _dot(h, w_ref[...]).astype(o_ref.dtype)


def _in_proj(ctx, x, sh2, sc2, gain, w):
    B, T, D = x.shape
    n_ctx = ctx.shape[1]
    assert n_ctx == ROW_TILE and T % ROW_TILE == 0
    nt = 1 + T // ROW_TILE
    return pl.pallas_call(
        _in_proj_kernel,
        grid=(B, nt),
        in_specs=[
            pl.BlockSpec((1, ROW_TILE, D), lambda b, i: (b, 0, 0)),
            pl.BlockSpec((1, ROW_TILE, D), lambda b, i: (b, jnp.maximum(i - 1, 0), 0)),
            pl.BlockSpec((1, 2, D), lambda b, i: (b, 0, 0)),
            pl.BlockSpec((1, 2, D), lambda b, i: (b, 0, 0)),
            pl.BlockSpec((1, D), lambda b, i: (0, 0)),
            pl.BlockSpec((D, PROJ_W), lambda b, i: (0, 0)),
        ],
        out_specs=pl.BlockSpec((1, ROW_TILE, PROJ_W), lambda b, i: (b, i, 0)),
        out_shape=jax.ShapeDtypeStruct((B, n_ctx + T, PROJ_W), BF16),
        compiler_params=pltpu.CompilerParams(
            dimension_semantics=("arbitrary", "arbitrary"), vmem_limit_bytes=VMEM_LIMIT),
        name="in_proj",
    )(ctx, x, sh2, sc2, gain.reshape(1, D), w)


def _rms(x, gain):
    return x * lax.rsqrt(jnp.mean(x * x, axis=-1, keepdims=True) + EPS) * gain


def _rope_tables(n_tokens):
    rows = n_tokens // GRID_W
    row = jnp.repeat(jnp.arange(rows, dtype=F32), GRID_W)
    col = jnp.tile(jnp.arange(GRID_W, dtype=F32), rows)
    n_freq = QK_ROPE // 4
    inv_freq = ROPE_BASE ** (-jnp.arange(n_freq, dtype=F32) / n_freq)
    ang = jnp.concatenate([row[:, None] * inv_freq, col[:, None] * inv_freq], axis=-1)
    return jnp.cos(ang), jnp.sin(ang)


ROPE_HALF = QK_ROPE // 2
X1 = slice(QK_NOPE, QK_NOPE + ROPE_HALF)
X2 = slice(QK_NOPE + ROPE_HALF, QK_HEAD)


def _rot_cols(w):
    w3 = w.reshape(w.shape[0], MLA_HEADS, HEAD_PAD)
    rot = jnp.zeros_like(w3).at[:, :, X1].set(-w3[:, :, X2]).at[:, :, X2].set(w3[:, :, X1])
    return rot.reshape(w.shape)


def _swap_halves(g):
    return jnp.zeros_like(g).at[:, X1].set(g[:, X2]).at[:, X2].set(g[:, X1])


def _mla_weights(p):
    H = MLA_HEADS
    wq = jnp.pad(p['w_uq'].reshape(Q_LORA, H, QK_HEAD), ((0, 0), (0, 0), (0, HEAD_PAD - QK_HEAD)))
    wq = wq.reshape(Q_LORA, H * HEAD_PAD)
    wkv = p['w_ukv'].reshape(KV_LORA, H, QK_NOPE + V_HEAD)
    wk_lat = jnp.pad(wkv[:, :, :QK_NOPE], ((0, 0), (0, 0), (0, HEAD_PAD - QK_NOPE)))
    place = jnp.zeros((LANES, H, HEAD_PAD), F32).at[:QK_ROPE, :, QK_NOPE:QK_HEAD].set(
        jnp.broadcast_to(jnp.eye(QK_ROPE, dtype=F32)[:, None, :], (QK_ROPE, H, QK_ROPE)))
    wk = jnp.concatenate([wk_lat, place], axis=0).reshape(KV_LORA + LANES, H * HEAD_PAD)
    wv = wkv[:, :, QK_NOPE:].reshape(KV_LORA, H * V_HEAD)
    gq = jnp.pad(p['q_norm'], (0, HEAD_PAD - QK_HEAD)).reshape(1, HEAD_PAD)
    gk = jnp.pad(p['k_norm'], (0, HEAD_PAD - QK_HEAD)).reshape(1, HEAD_PAD)
    bf = lambda w: w.astype(BF16)
    return (bf(wq), bf(_rot_cols(wq)), bf(wk), bf(_rot_cols(wk)), bf(wv), gq, _swap_halves(gq), gk, _swap_halves(gk))


def _mla_tables(n_ctx, T):
    cos, sin = _rope_tables(T)
    c = jnp.ones((n_ctx + T, HEAD_PAD), F32).at[n_ctx:, X1].set(cos).at[n_ctx:, X2].set(cos)
    s = jnp.zeros((n_ctx + T, HEAD_PAD), F32).at[n_ctx:, X1].set(sin).at[n_ctx:, X2].set(sin)
    return c, s


def _mla_prep_kernel(x_ref, c_ref, s_ref, qlg_ref, kvg_ref, wq_ref, wqr_ref, wk_ref, wkr_ref, wv_ref,
                     gq_ref, gqp_ref, gk_ref, gkp_ref, q_ref, k_ref, v_ref):
    x = x_ref[0].astype(F32)
    ql = _rms(x[:, :Q_LORA], qlg_ref[...])
    kvl = _rms(x[:, Q_LORA:Q_LORA + KV_LORA], kvg_ref[...])
    k_in = jnp.concatenate([kvl, x[:, Q_LORA + KV_LORA:]], axis=1)
    cos, sin = c_ref[...], s_ref[...]

    def finish(raw, partner, g, g_swapped, scale, o_ref):
        gc, gs = g * cos, g_swapped * sin
        for h in range(MLA_HEADS):
            sl = slice(h * HEAD_PAD, (h + 1) * HEAD_PAD)
            rh = raw[:, sl]
            inv = lax.rsqrt(jnp.sum(rh * rh, axis=-1, keepdims=True) * (1.0 / QK_HEAD) + EPS) * scale
            o_ref[0, :, sl] = ((rh * gc + partner[:, sl] * gs) * inv).astype(o_ref.dtype)

    finish(_dot(ql, wq_ref[...]), _dot(ql, wqr_ref[...]), gq_ref[...], gqp_ref[...], ATTN_SCALE, q_ref)
    finish(_dot(k_in, wk_ref[...]), _dot(k_in, wkr_ref[...]), gk_ref[...], gkp_ref[...], 1.0, k_ref)
    v_ref[0] = _dot(kvl, wv_ref[...]).astype(v_ref.dtype)


def _mla_prep(proj, p, n_ctx, T):
    B, Tt, _ = proj.shape
    H = MLA_HEADS
    ctx_tiles = n_ctx // ROW_TILE
    mla_blk = (PROJ_RW + PROJ_GATE) // PROJ_MLA
    weights = _mla_weights(p)
    cos, sin = _mla_tables(n_ctx, T)
    const = lambda b, i: (0, 0)
    row = lambda b, i: (b, i, 0)
    tab = pl.BlockSpec((ROW_TILE, HEAD_PAD), lambda b, i: (i, 0))
    full = lambda a: pl.BlockSpec(a.shape, const)
    return pl.pallas_call(
        _mla_prep_kernel,
        grid=(B, Tt // ROW_TILE),
        in_specs=[pl.BlockSpec((1, ROW_TILE, PROJ_MLA), lambda b, i: (b, i, mla_blk)), tab, tab,
                  pl.BlockSpec((1, Q_LORA), const), pl.BlockSpec((1, KV_LORA), const)]
                 + [full(w) for w in weights],
        out_specs=[
            pl.BlockSpec((1, ROW_TILE, H * HEAD_PAD), lambda b, i: (b, jnp.maximum(i - ctx_tiles, 0), 0)),
            pl.BlockSpec((1, ROW_TILE, H * HEAD_PAD), row),
            pl.BlockSpec((1, ROW_TILE, H * V_HEAD), row),
        ],
        out_shape=[
            jax.ShapeDtypeStruct((B, T, H * HEAD_PAD), BF16),
            jax.ShapeDtypeStruct((B, Tt, H * HEAD_PAD), BF16),
            jax.ShapeDtypeStruct((B, Tt, H * V_HEAD), BF16),
        ],
        compiler_params=pltpu.CompilerParams(
            dimension_semantics=("arbitrary", "arbitrary"), vmem_limit_bytes=VMEM_LIMIT),
        name="mla_prep",
    )(proj, cos, sin, p['q_lat_norm'].reshape(1, Q_LORA), p['kv_lat_norm'].reshape(1, KV_LORA), *weights)


ATTN_Q_TILE = 1024
HEADS_PER_STEP = LANES // V_HEAD


ATTN_ROW_SPLIT = 8


def _attn_kernel(q_ref, k_ref, v_ref, o_ref):
    v2 = v_ref[0]
    rows = q_ref.shape[1] // ATTN_ROW_SPLIT
    work = [(hh, rs) for rs in range(ATTN_ROW_SPLIT) for hh in range(HEADS_PER_STEP)]

    def scores(hh, rs):
        sl = slice(hh * HEAD_PAD, (hh + 1) * HEAD_PAD)
        return _dot_t(q_ref[0, rs * rows:(rs + 1) * rows, sl], k_ref[0, :, sl])

    outs = {}
    s_next = scores(*work[0])
    for i, (hh, rs) in enumerate(work):
        s = s_next
        if i + 1 < len(work):
            s_next = scores(*work[i + 1])
        e = jnp.exp(s - jnp.max(s, axis=-1, keepdims=True))
        outs[hh, rs] = _dot(e, v2) / jnp.sum(e, axis=-1, keepdims=True)
    lane = lax.broadcasted_iota(jnp.int32, (rows, LANES), 1)
    for rs in range(ATTN_ROW_SPLIT):
        o_ref[0, rs * rows:(rs + 1) * rows] = jnp.where(lane < V_HEAD, outs[0, rs], outs[1, rs])


def _attention(q, k, v):
    B, T, _ = q.shape
    Kt = k.shape[1]
    assert T % ATTN_Q_TILE == 0 and ATTN_Q_TILE % (ATTN_ROW_SPLIT * SUBLANES) == 0
    hp = MLA_HEADS // HEADS_PER_STEP
    qw = HEADS_PER_STEP * HEAD_PAD
    return pl.pallas_call(
        _attn_kernel,
        grid=(B, hp, T // ATTN_Q_TILE),
        in_specs=[
            pl.BlockSpec((1, ATTN_Q_TILE, qw), lambda b, h, i: (b, i, h)),
            pl.BlockSpec((1, Kt, qw), lambda b, h, i: (b, 0, h)),
            pl.BlockSpec((1, Kt, LANES), lambda b, h, i: (b, 0, h)),
        ],
        out_specs=pl.BlockSpec((1, ATTN_Q_TILE, LANES), lambda b, h, i: (b, i, h)),
        out_shape=jax.ShapeDtypeStruct((B, T, MLA_HEADS * V_HEAD), F32),
        compiler_params=pltpu.CompilerParams(
            dimension_semantics=("arbitrary", "arbitrary", "arbitrary"), vmem_limit_bytes=VMEM_LIMIT),
        name="attention",
    )(q, k, v)


HALO = 16
LORA_W = LANES
GATE_W = PROJ_RW - 3 * RW_DIM - 2 * LORA_W


def _head_ones(width, head):
    i = np.arange(width) // head
    return jnp.asarray(i[:, None] == i[None, :], BF16)


def _head_sum(x, ones_bd):
    hi = x.astype(BF16)
    lo = (x - hi.astype(F32)).astype(BF16)
    return (jnp.dot(hi, ones_bd, preferred_element_type=F32)
            + jnp.dot(lo, ones_bd, preferred_element_type=F32))


def _dot3(a, b_hi, b_lo):
    hi = a.astype(BF16)
    lo = (a - hi.astype(F32)).astype(BF16)
    d = lambda u, w: jnp.dot(u, w, preferred_element_type=F32)
    return d(hi, b_hi) + (d(hi, b_lo) + d(lo, b_hi))


def _rwkv_prep_kernel(x_ref, prev_ref, next_ref, conv_ref, kkg_ref, ka_ref, rk_ref, w0_ref, a0_ref,
                      w2h_ref, w2l_ref, a2_ref, g2_ref, ones_ref,
                      r_ref, v_ref, kk_ref, lw_ref, bk_ref, kd_ref, bonus_ref, gg_ref, *, ctx_tiles, n_tiles):
    i = pl.program_id(1)
    x = x_ref[0].astype(F32)
    tm, W = x.shape
    C = RW_DIM
    first = (i == 0) | (i == ctx_tiles)
    last = (i == ctx_tiles - 1) | (i == n_tiles - 1)
    prev_row = jnp.where(first, 0.0, prev_ref[0, HALO - 1:HALO].astype(F32))
    next_row = jnp.where(last, 0.0, next_ref[0, 0:1].astype(F32))
    row = lax.broadcasted_iota(jnp.int32, (tm, W), 0)
    x_dn = jnp.where(row == 0, prev_row, pltpu.roll(x, 1, 0))
    x_up = jnp.where(row == tm - 1, next_row, pltpu.roll(x, tm - 1, 0))
    xc = x_dn * conv_ref[0:1] + x * conv_ref[1:2] + x_up * conv_ref[2:3]

    r, k, v = xc[:, :C], xc[:, C:2 * C], xc[:, 2 * C:3 * C]
    lora_w = jnp.tanh(xc[:, 3 * C:3 * C + LORA_W])
    lora_a = xc[:, 3 * C + LORA_W:3 * C + 2 * LORA_W]
    lg = xc[:, 3 * C + 2 * LORA_W:]
    ones_bd = ones_ref[...]
    kq = k * kkg_ref[...]
    kk = kq * lax.rsqrt(_head_sum(kq * kq, ones_bd) + 1e-12)
    r_ref[0], v_ref[0], kk_ref[0] = r, v, kk

    k_sum = jnp.zeros_like(k)
    for d in range(2):
        z = w0_ref[d:d + 1] + _dot3(lora_w, w2h_ref[d], w2l_ref[d])
        softplus_neg = jnp.maximum(-z, 0.0) + jnp.log(1.0 + jnp.exp(-jnp.abs(z)))
        lw_ref[d, 0] = -jnp.exp(-softplus_neg - 0.5)
        a = jax.nn.sigmoid(a0_ref[d:d + 1] + _dot(lora_a, a2_ref[d]))
        kd = k * (1.0 + (a - 1.0) * ka_ref[...])
        bk_ref[d, 0] = kk * a
        kd_ref[d, 0] = kd
        k_sum = k_sum + kd
    bonus_ref[0] = _head_sum(r * k_sum * rk_ref[...], ones_bd) * v
    gg_ref[0] = _dot(jax.nn.sigmoid(lg), g2_ref[...])


def _rwkv_prep(proj, p, n_ctx):
    B, Tt, _ = proj.shape
    C = RW_DIM
    nt = Tt // ROW_TILE
    hb = ROW_TILE // HALO
    pad_cols = lambda w: jnp.pad(w, ((0, 0), (0, PROJ_RW - w.shape[1])))
    w2 = jnp.stack([jnp.pad(p['decay_w2'][0], ((0, LORA_W - DECAY_LORA), (0, 0))),
                    jnp.pad(p['decay_w2'][1], ((DECAY_LORA, 0), (0, 0)))])
    w2h = w2.astype(BF16)
    w2l = (w2 - w2h.astype(F32)).astype(BF16)
    a2 = jnp.stack([jnp.pad(p['aicl_a2'][0], ((0, LORA_W - AICL_LORA), (0, 0))),
                    jnp.pad(p['aicl_a2'][1], ((AICL_LORA, 0), (0, 0)))]).astype(BF16)
    g2 = jnp.pad(p['gate_g2'], ((0, GATE_W - GATE_LORA), (0, 0))).astype(BF16)
    row = lambda b, i: (b, i, 0)
    drow = lambda b, i: (0, b, i, 0)
    const2 = lambda b, i: (0, 0)
    const3 = lambda b, i: (0, 0, 0)
    vec = pl.BlockSpec((1, C), const2)
    out_row = pl.BlockSpec((1, ROW_TILE, C), row)
    out_dir = pl.BlockSpec((2, 1, ROW_TILE, C), drow)
    sds = jax.ShapeDtypeStruct((B, Tt, C), F32)
    sds2 = jax.ShapeDtypeStruct((2, B, Tt, C), F32)
    return pl.pallas_call(
        functools.partial(_rwkv_prep_kernel, ctx_tiles=n_ctx // ROW_TILE, n_tiles=nt),
        grid=(B, nt),
        in_specs=[
            pl.BlockSpec((1, ROW_TILE, PROJ_RW), row),
            pl.BlockSpec((1, HALO, PROJ_RW), lambda b, i: (b, jnp.maximum(i * hb - 1, 0), 0)),
            pl.BlockSpec((1, HALO, PROJ_RW), lambda b, i: (b, jnp.minimum((i + 1) * hb, nt * hb - 1), 0)),
            pl.BlockSpec((3, PROJ_RW), const2),
            vec, vec, vec,
            pl.BlockSpec((2, C), const2),
            pl.BlockSpec((2, C), const2),
            pl.BlockSpec((2, LORA_W, C), const3),
            pl.BlockSpec((2, LORA_W, C), const3),
            pl.BlockSpec((2, LORA_W, C), const3),
            pl.BlockSpec((GATE_W, C), const2),
            pl.BlockSpec((C, C), const2),
        ],
        out_specs=[out_row, out_row, out_row, out_dir, out_dir, out_dir, out_row, out_row],
        out_shape=[sds, sds, sds, sds2, sds2, sds2, sds, sds],
        compiler_params=pltpu.CompilerParams(
            dimension_semantics=("arbitrary", "arbitrary"), vmem_limit_bytes=VMEM_LIMIT),
        name="rwkv_prep",
    )(proj, proj, proj, pad_cols(p['shift_conv']), p['k_k'].reshape(1, C), p['k_a'].reshape(1, C),
      p['r_k'].reshape(1, C), p['decay_w0'], p['aicl_a0'], w2h, w2l, a2, g2, _head_ones(C, RW_HEAD))


def _merge_kernel(att_ref, yf_ref, yb_ref, bonus_ref, gg_ref, gate_ref, x_ref, ga_ref, gnw_ref, gnb_ref, ones_ref,
                  wm_ref, wr_ref, wo_ref, o_ref):
    D = x_ref.shape[-1]
    ones_bd = ones_ref[...]
    y = yf_ref[0] + yb_ref[0]
    yc = y - _head_sum(y, ones_bd) * (1.0 / RW_HEAD)
    var = _head_sum(yc * yc, ones_bd) * (1.0 / RW_HEAD)
    y_n = yc * lax.rsqrt(var + GN_EPS) * gnw_ref[...] + gnb_ref[...]
    rw = (y_n + bonus_ref[0]) * gg_ref[0]
    g = jax.nn.sigmoid(gate_ref[0].astype(F32))
    mix = g[:, :D] * _dot(att_ref[0], wm_ref[...]) + g[:, D:] * _dot(rw, wr_ref[...])
    o_ref[0] = x_ref[0] + ga_ref[0] * _dot(mix, wo_ref[...])


def _merge(att, y_f, y_b, bonus, gg, proj, x, g_a, p, n_ctx):
    B, T, D = x.shape
    C = RW_DIM
    ctx_tiles = n_ctx // ROW_TILE
    gate_blk = PROJ_RW // PROJ_GATE
    row = lambda b, i: (b, i, 0)
    lat_row = lambda b, i: (b, i + ctx_tiles, 0)
    const = lambda b, i: (0, 0)
    return pl.pallas_call(
        _merge_kernel,
        grid=(B, T // ROW_TILE),
        in_specs=[
            pl.BlockSpec((1, ROW_TILE, att.shape[-1]), row),
            pl.BlockSpec((1, ROW_TILE, C), row),
            pl.BlockSpec((1, ROW_TILE, C), row),
            pl.BlockSpec((1, ROW_TILE, C), lat_row),
            pl.BlockSpec((1, ROW_TILE, C), lat_row),
            pl.BlockSpec((1, ROW_TILE, PROJ_GATE), lambda b, i: (b, i + ctx_tiles, gate_blk)),
            pl.BlockSpec((1, ROW_TILE, D), row),
            pl.BlockSpec((1, 1, D), lambda b, i: (b, 0, 0)),
            pl.BlockSpec((1, C), const),
            pl.BlockSpec((1, C), const),
            pl.BlockSpec((C, C), const),
            pl.BlockSpec(p['w_o_mla'].shape, const),
            pl.BlockSpec(p['w_o_rwkv'].shape, const),
            pl.BlockSpec(p['w_out'].shape, const),
        ],
        out_specs=pl.BlockSpec((1, ROW_TILE, D), row),
        out_shape=jax.ShapeDtypeStruct((B, T, D), F32),
        compiler_params=pltpu.CompilerParams(
            dimension_semantics=("arbitrary", "arbitrary"), vmem_limit_bytes=VMEM_LIMIT),
        name="merge",
    )(att, y_f, y_b, bonus, gg, proj, x, g_a[:, None, :], p['gn_w'].reshape(1, C), p['gn_b'].reshape(1, C),
      _head_ones(C, RW_HEAD), p['w_o_mla'].astype(BF16), p['w_o_rwkv'].astype(BF16), p['w_out'].astype(BF16))


TILE_ROWS = 8
MOE_BLOCK = 512
ZERO_ROWS = MOE_BLOCK + TILE_ROWS


ROUTER_TILE = 1024
NEG_INF = float("-inf")
HI_MASK = 0xFFFF0000


def _pack_pair(lo, hi):
    lo_b = lax.bitcast_convert_type(lo.astype(BF16).astype(F32), jnp.uint32)
    hi_b = lax.bitcast_convert_type(hi.astype(BF16).astype(F32), jnp.uint32)
    return (lo_b >> 16) | (hi_b & jnp.uint32(HI_MASK))


def _unpack_pair(w):
    lo = lax.bitcast_convert_type(w << 16, F32)
    hi = lax.bitcast_convert_type(w & jnp.uint32(HI_MASK), F32)
    return lo.astype(BF16), hi.astype(BF16)


def _row_max(x):
    return jnp.max(x, axis=-1, keepdims=True)


def _first_index_of(x, value, lane_f):
    return jnp.min(jnp.where(x == value, lane_f, float(x.shape[-1])), axis=-1, keepdims=True)


def _router_kernel(x_ref, gain_ref, sh_ref, sc_ref, wh_ref, wm_ref, wl_ref, bias_ref,
                   u_ref, ids_ref, gates_ref, ranks_ref, counts_ref, carry_ref):
    @pl.when((pl.program_id(0) == 0) & (pl.program_id(1) == 0))
    def _():
        carry_ref[...] = jnp.zeros_like(carry_ref)

    x = x_ref[0]
    tm, D = x.shape
    E = bias_ref.shape[-1]
    u = x * lax.rsqrt(jnp.mean(x * x, axis=-1, keepdims=True) + EPS) * gain_ref[...]
    u = u * (1.0 + sc_ref[0]) + sh_ref[0]
    u_ref[0] = _pack_pair(u[:, :D // 2], u[:, D // 2:])

    uh, um, ul = _split3(u)
    wh, wm, wl = wh_ref[...], wm_ref[...], wl_ref[...]
    d = lambda a, b: jnp.dot(a, b, preferred_element_type=F32)
    logits = d(uh, wh) + (d(uh, wm) + d(um, wh)) + (d(uh, wl) + d(um, wm) + d(ul, wh))
    scores = jax.nn.sigmoid(logits)
    sel = scores + bias_ref[...]

    lane_i = lax.broadcasted_iota(jnp.int32, (tm, E), 1)
    lane_f = lane_i.astype(F32)
    out_f = lax.broadcasted_iota(jnp.int32, (tm, LANES), 1).astype(F32)
    per_group = E // N_GROUPS
    grp_f = jnp.floor(lane_f * (1.0 / per_group))

    gs = jnp.full((tm, LANES), NEG_INF, F32)
    for g in range(N_GROUPS):
        sg = jnp.where(lane_i >= g * per_group, jnp.where(lane_i < (g + 1) * per_group, sel, NEG_INF), NEG_INF)
        m1 = _row_max(sg)
        i1 = _first_index_of(sg, m1, lane_f)
        m2 = _row_max(jnp.where(lane_f == i1, NEG_INF, sg))
        gs = jnp.where(out_f == g, m1 + m2, gs)

    allow = jnp.zeros((tm, E), F32)
    for _ in range(TOPK_GROUPS):
        m = _row_max(gs)
        i = _first_index_of(gs, m, out_f)
        gs = jnp.where(out_f == i, NEG_INF, gs)
        allow = jnp.where(grp_f == i, 1.0, allow)
    selm = jnp.where(allow > 0.0, sel, NEG_INF)

    ids = jnp.zeros((tm, LANES), F32)
    gts = jnp.zeros((tm, LANES), F32)
    member = jnp.zeros((tm, E), F32)
    idx_cols = []
    gsum = jnp.zeros((tm, 1), F32)
    for k in range(TOP_K):
        m = _row_max(selm)
        i = _first_index_of(selm, m, lane_f)
        hit = lane_f == i
        gk = jnp.sum(jnp.where(hit, scores, 0.0), axis=-1, keepdims=True)
        selm = jnp.where(hit, NEG_INF, selm)
        member = jnp.where(hit, 1.0, member)
        ids = jnp.where(out_f == k, i, ids)
        gts = jnp.where(out_f == k, gk, gts)
        idx_cols.append(i)
        gsum = gsum + gk
    gts = gts / gsum * ROUTED_SCALE

    r2 = lax.broadcasted_iota(jnp.int32, (tm, tm), 0)
    c2 = lax.broadcasted_iota(jnp.int32, (tm, tm), 1)
    before = jnp.where(r2 > c2, 1.0, 0.0).astype(BF16)
    mem_b = member.astype(BF16)
    carry = carry_ref[...]
    pos = carry + jnp.dot(before, mem_b, preferred_element_type=F32)
    rk = jnp.zeros((tm, LANES), F32)
    for k in range(TOP_K):
        rk = jnp.where(out_f == k, jnp.sum(jnp.where(lane_f == idx_cols[k], pos, 0.0), axis=-1, keepdims=True), rk)
    colsum = jnp.dot(jnp.ones((8, tm), BF16), mem_b, preferred_element_type=F32)[0:1]
    carry_ref[...] = carry + colsum
    counts_ref[...] = carry + colsum
    ids_ref[0] = ids.astype(jnp.int32)
    ranks_ref[0] = rk.astype(jnp.int32)
    gates_ref[0] = gts


def _router(x1, sh_m, sc_m, p):
    B, T, D = x1.shape
    E = N_EXPERTS
    wh, wm, wl = _split3(p['router_w'])
    row = lambda b, i: (b, i, 0)
    const = lambda b, i: (0, 0)
    vec = lambda b, i: (b, 0, 0)
    lane_out = lambda dt: jax.ShapeDtypeStruct((B, T, LANES), dt)
    return pl.pallas_call(
        _router_kernel,
        grid=(B, T // ROUTER_TILE),
        in_specs=[
            pl.BlockSpec((1, ROUTER_TILE, D), row),
            pl.BlockSpec((1, D), const),
            pl.BlockSpec((1, 1, D), vec),
            pl.BlockSpec((1, 1, D), vec),
            pl.BlockSpec((D, E), const),
            pl.BlockSpec((D, E), const),
            pl.BlockSpec((D, E), const),
            pl.BlockSpec((1, E), const),
        ],
        out_specs=[
            pl.BlockSpec((1, ROUTER_TILE, D // 2), row),
            pl.BlockSpec((1, ROUTER_TILE, LANES), row),
            pl.BlockSpec((1, ROUTER_TILE, LANES), row),
            pl.BlockSpec((1, ROUTER_TILE, LANES), row),
            pl.BlockSpec((1, E), const),
        ],
        out_shape=[
            jax.ShapeDtypeStruct((B, T, D // 2), jnp.uint32),
            lane_out(jnp.int32), lane_out(F32), lane_out(jnp.int32),
            jax.ShapeDtypeStruct((1, E), F32),
        ],
        scratch_shapes=[pltpu.VMEM((1, E), F32)],
        compiler_params=pltpu.CompilerParams(
            dimension_semantics=("arbitrary", "arbitrary"), vmem_limit_bytes=VMEM_LIMIT),
        name="router",
    )(x1, p['norm_ffn'].reshape(1, D), sh_m[:, None, :], sc_m[:, None, :], wh, wm, wl,
      p['router_bias'].reshape(1, E))


def _slot_kernel(ids_ref, ranks_ref, base_ref, o_ref):
    ids = ids_ref[...].astype(F32)
    tm = ids.shape[0]
    E = base_ref.shape[-1]
    lane_e = lax.broadcasted_iota(jnp.int32, (tm, E), 1).astype(F32)
    out_lane = lax.broadcasted_iota(jnp.int32, (tm, LANES), 1)
    first = jnp.zeros((tm, LANES), F32)
    for k in range(TOP_K):
        fk = jnp.sum(jnp.where(lane_e == ids[:, k:k + 1], base_ref[...], 0.0), axis=-1, keepdims=True)
        first = jnp.where(out_lane == k, fk, first)
    o_ref[...] = first.astype(jnp.int32) + ranks_ref[...]


def _slots(ids, ranks, base):
    n_tok = ids.shape[0]
    E = base.shape[-1]
    row = pl.BlockSpec((ROUTER_TILE, LANES), lambda i: (i, 0))
    return pl.pallas_call(
        _slot_kernel,
        grid=(n_tok // ROUTER_TILE,),
        in_specs=[row, row, pl.BlockSpec((1, E), lambda i: (0, 0))],
        out_specs=row,
        out_shape=jax.ShapeDtypeStruct((n_tok, LANES), jnp.int32),
        compiler_params=pltpu.CompilerParams(dimension_semantics=("arbitrary",)),
        name="slots",
    )(ids, ranks, base)


DMA_PRIORITIES = 2


def _row_copy(src, src_row, dst, dst_row, sem):
    return pltpu.make_async_copy(src.at[pl.ds(src_row, 1)], dst.at[pl.ds(dst_row, 1)], sem)


def _dispatch_kernel(pad_ref, dest_ref, u_ref, slots_ref, zero_ref, sem):
    tm = u_ref.shape[0]
    n_experts = pad_ref.shape[0]

    @pl.when(pl.program_id(0) == 0)
    def _():
        zero_ref[...] = jnp.zeros_like(zero_ref)

        def fill(e, carry):
            start = pl.multiple_of(pad_ref[e], TILE_ROWS)
            pltpu.make_async_copy(zero_ref, slots_ref.at[pl.ds(start, ZERO_ROWS)], sem).start()
            return carry

        lax.fori_loop(0, n_experts, fill, 0)

        def filled(e, carry):
            pltpu.make_async_copy(zero_ref, slots_ref.at[pl.ds(0, ZERO_ROWS)], sem).wait()
            return carry

        lax.fori_loop(0, n_experts, filled, 0)

    def issue(r, carry):
        for k in range(TOP_K):
            _row_copy(u_ref, r, slots_ref, dest_ref[0, 0, r * TOP_K + k], sem).start(priority=k % DMA_PRIORITIES)
        return carry

    lax.fori_loop(0, tm, issue, 0)
    for k in range(TOP_K):
        pltpu.make_async_copy(u_ref, slots_ref.at[pl.ds(0, tm)], sem).wait()


def _dispatch(pad_start, dest, u_rows, n_slots):
    n_tok, W = u_rows.shape
    nt = n_tok // ROW_TILE
    grid_spec = pltpu.PrefetchScalarGridSpec(
        num_scalar_prefetch=1,
        grid=(nt,),
        in_specs=[
            pl.BlockSpec((1, 1, ROW_TILE * TOP_K), lambda i, pad: (i, 0, 0), memory_space=pltpu.SMEM),
            pl.BlockSpec((ROW_TILE, W), lambda i, pad: (i, 0)),
        ],
        out_specs=pl.BlockSpec(memory_space=pl.ANY),
        scratch_shapes=[pltpu.VMEM((ZERO_ROWS, W), jnp.uint32), pltpu.SemaphoreType.DMA(())],
    )
    return pl.pallas_call(
        _dispatch_kernel,
        grid_spec=grid_spec,
        out_shape=jax.ShapeDtypeStruct((n_slots + ZERO_ROWS, W), jnp.uint32),
        compiler_params=pltpu.CompilerParams(
            dimension_semantics=("arbitrary",), vmem_limit_bytes=VMEM_LIMIT),
        name="dispatch",
    )(pad_start, dest, u_rows)


WEIGHT_SLOTS = 2


def _expert_weight_copies(hbm_refs, stage_refs, expert, slot, sems):
    return [pltpu.make_async_copy(w.at[expert], s.at[slot], sems.at[slot, i])
            for i, (w, s) in enumerate(zip(hbm_refs, stage_refs))]


def _moe_ffn_kernel(be_ref, first_ref, slot_ref, next_ref, nu_ref, x_ref, w1_ref, w3_ref, w2_ref, o_ref,
                    s1_ref, s3_ref, s2_ref, w13_ref, w2b_ref, sems):
    j = pl.program_id(0)
    F = w1_ref.shape[-1]
    half = x_ref.shape[-1]
    copies = functools.partial(_expert_weight_copies, (w1_ref, w3_ref, w2_ref), (s1_ref, s3_ref, s2_ref), sems=sems)

    @pl.when(j == 0)
    def _():
        for c in copies(be_ref[0], 0):
            c.start()

    @pl.when(first_ref[j] == 1)
    def _():
        slot = slot_ref[j]
        for c in copies(be_ref[j], slot):
            c.wait()

        @pl.when(next_ref[j] >= 0)
        def _():
            for c in copies(next_ref[j], 1 - slot):
                c.start()

        w13_ref[:, :F] = s1_ref[slot].astype(BF16)
        w13_ref[:, F:] = s3_ref[slot].astype(BF16)
        w2b_ref[...] = s2_ref[slot].astype(BF16)

    @pl.when(j < nu_ref[0])
    def _():
        lo, hi = _unpack_pair(x_ref[...])
        h = (jnp.dot(lo, w13_ref[:half], preferred_element_type=F32)
             + jnp.dot(hi, w13_ref[half:], preferred_element_type=F32))
        h1, h3 = h[:, :F], h[:, F:]
        y = _dot(h1 * jax.nn.sigmoid(h1) * h3, w2b_ref[...])
        o_ref[...] = pltpu.einshape("s(cl)->(sc)l", y, c=TILE_ROWS)

    @pl.when(j >= nu_ref[0])
    def _():
        o_ref[...] = jnp.zeros_like(o_ref)


def _moe_ffn(block_e, n_used, x_slots, w1, w3, w2):
    W = x_slots.shape[1]
    E, D, F = w1.shape
    nblk = block_e.shape[0]
    slots = nblk * MOE_BLOCK
    idx = jnp.arange(nblk)
    first = jnp.concatenate([jnp.ones((1,), bool), block_e[1:] != block_e[:-1]])
    slot = ((jnp.cumsum(first) - 1) % WEIGHT_SLOTS).astype(jnp.int32)
    later_first = lax.cummin(jnp.where(first, idx, nblk)[::-1])[::-1]
    next_first = jnp.concatenate([later_first[1:], jnp.full((1,), nblk)])
    next_e = jnp.where(next_first < nblk, block_e[jnp.minimum(next_first, nblk - 1)], -1).astype(jnp.int32)
    n_prefetch = 5
    blk = lambda j, *_: (jnp.minimum(j, _[4][0] - 1), 0)
    grid_spec = pltpu.PrefetchScalarGridSpec(
        num_scalar_prefetch=n_prefetch,
        grid=(nblk,),
        in_specs=[
            pl.BlockSpec((MOE_BLOCK, W), blk),
            pl.BlockSpec(memory_space=pl.ANY),
            pl.BlockSpec(memory_space=pl.ANY),
            pl.BlockSpec(memory_space=pl.ANY),
        ],
        out_specs=pl.BlockSpec((MOE_BLOCK * TILE_ROWS, LANES), lambda j, *_: (j, 0)),
        scratch_shapes=[pltpu.VMEM((WEIGHT_SLOTS, D, F), F32), pltpu.VMEM((WEIGHT_SLOTS, D, F), F32),
                        pltpu.VMEM((WEIGHT_SLOTS, F, D), F32),
                        pltpu.VMEM((D, 2 * F), BF16), pltpu.VMEM((F, D), BF16),
                        pltpu.SemaphoreType.DMA((WEIGHT_SLOTS, 3))],
    )
    return pl.pallas_call(
        _moe_ffn_kernel,
        grid_spec=grid_spec,
        out_shape=jax.ShapeDtypeStruct((slots * TILE_ROWS, LANES), F32),
        compiler_params=pltpu.CompilerParams(
            dimension_semantics=("arbitrary",), vmem_limit_bytes=VMEM_LIMIT),
        name="moe_ffn",
    )(block_e, first.astype(jnp.int32), slot, next_e, n_used, x_slots, w1, w3, w2)


COMBINE_ROWS = 32


def _combine_kernel(dest_ref, gates_ref, u_ref, x_ref, gm_ref, w1_ref, w3_ref, w2_ref, ys_ref, o_ref, buf_ref, sem):
    tm, half = u_ref.shape

    def tile(i):
        return pl.ds(pl.multiple_of(i * TILE_ROWS, TILE_ROWS), TILE_ROWS)

    def issue(r, carry):
        for k in range(TOP_K):
            pltpu.make_async_copy(ys_ref.at[tile(dest_ref[0, 0, r * TOP_K + k])], buf_ref.at[k, tile(r)], sem).start(
                priority=k % DMA_PRIORITIES)
        return carry

    lax.fori_loop(0, tm, issue, 0)

    ulo, uhi = _unpack_pair(u_ref[...])
    both = lambda w_ref: (jnp.dot(ulo, w_ref[:half], preferred_element_type=F32)
                          + jnp.dot(uhi, w_ref[half:], preferred_element_type=F32))
    h1, h3 = both(w1_ref), both(w3_ref)
    o_ref[...] = x_ref[...] + gm_ref[0] * _dot(h1 * jax.nn.sigmoid(h1) * h3, w2_ref[...])

    for k in range(TOP_K):
        pltpu.make_async_copy(ys_ref.at[pl.ds(0, tm * TILE_ROWS)], buf_ref.at[k], sem).wait()

    gm = gm_ref[0]
    for rb in range(tm // COMBINE_ROWS):
        rows = slice(rb * COMBINE_ROWS, (rb + 1) * COMBINE_ROWS)
        gates = gates_ref[rows, :]
        for c in range(TILE_ROWS):
            acc = jnp.zeros((COMBINE_ROWS, LANES), F32)
            for k in range(TOP_K):
                acc = acc + gates[:, k:k + 1] * buf_ref[
                    k, pl.ds(rb * COMBINE_ROWS * TILE_ROWS + c, COMBINE_ROWS, stride=TILE_ROWS), :]
            cols = slice(c * LANES, (c + 1) * LANES)
            o_ref[rows, cols] = o_ref[rows, cols] + gm[:, cols] * acc


def _combine(dest, gates, u_rows, x1, g_m, y_slots, p):
    B, T, D = x1.shape
    n_tok = B * T
    W = u_rows.shape[1]
    tiles_per_batch = T // ROW_TILE
    row = lambda i: (i, 0)
    const = lambda i: (0, 0)
    out = pl.pallas_call(
        _combine_kernel,
        grid=(n_tok // ROW_TILE,),
        in_specs=[
            pl.BlockSpec((1, 1, ROW_TILE * TOP_K), lambda i: (i, 0, 0), memory_space=pltpu.SMEM),
            pl.BlockSpec((ROW_TILE, LANES), row),
            pl.BlockSpec((ROW_TILE, W), row),
            pl.BlockSpec((ROW_TILE, D), row),
            pl.BlockSpec((1, 1, D), lambda i: (i // tiles_per_batch, 0, 0)),
            pl.BlockSpec(p['shared_w1'].shape, const),
            pl.BlockSpec(p['shared_w3'].shape, const),
            pl.BlockSpec(p['shared_w2'].shape, const),
            pl.BlockSpec(memory_space=pl.ANY),
        ],
        out_specs=pl.BlockSpec((ROW_TILE, D), row),
        out_shape=jax.ShapeDtypeStruct((n_tok, D), F32),
        scratch_shapes=[pltpu.VMEM((TOP_K, ROW_TILE * TILE_ROWS, LANES), F32), pltpu.SemaphoreType.DMA(())],
        compiler_params=pltpu.CompilerParams(
            dimension_semantics=("arbitrary",), vmem_limit_bytes=VMEM_LIMIT),
        name="combine",
    )(dest, gates.reshape(n_tok, LANES), u_rows, x1.reshape(n_tok, D), g_m[:, None, :],
      p['shared_w1'].astype(BF16), p['shared_w3'].astype(BF16), p['shared_w2'].astype(BF16), y_slots)
    return out.reshape(B, T, D)


def _moe(x1, sh_m, sc_m, g_m, p):
    B, T, D = x1.shape
    n_tok = B * T
    u_packed, ids, gates, ranks, counts = _router(x1, sh_m, sc_m, p)

    counts = counts[0].astype(jnp.int32)
    padded = (counts + MOE_BLOCK - 1) // MOE_BLOCK * MOE_BLOCK
    padded_end = jnp.cumsum(padded)
    base = padded_end - padded
    n_blocks = n_tok * TOP_K // MOE_BLOCK + N_EXPERTS
    n_used = (padded_end[-1] // MOE_BLOCK).astype(jnp.int32)
    blk = jnp.minimum(jnp.arange(n_blocks), n_used - 1) * MOE_BLOCK
    block_e = jnp.minimum(jnp.sum(padded_end[None, :] <= blk[:, None], axis=1), N_EXPERTS - 1).astype(jnp.int32)
    dest = _slots(ids.reshape(n_tok, LANES), ranks.reshape(n_tok, LANES), base.astype(F32).reshape(1, N_EXPERTS))
    dest = dest[:, :TOP_K].reshape(n_tok // ROW_TILE, 1, ROW_TILE * TOP_K)

    u_rows = u_packed.reshape(n_tok, D // 2)
    pad_start = ((base + counts) // TILE_ROWS * TILE_ROWS).astype(jnp.int32)
    x_slots = _dispatch(pad_start, dest, u_rows, n_blocks * MOE_BLOCK)
    y_slots = _moe_ffn(block_e, n_used.reshape(1), x_slots, p['expert_w1'], p['expert_w3'], p['expert_w2'])
    return _combine(dest, gates, u_rows, x1, g_m, y_slots, p)
```

```python
import functools

import jax
import jax.numpy as jnp
import numpy as np
from jax import lax
from jax.experimental import pallas as pl
from jax.experimental.pallas import tpu as pltpu

F32 = jnp.float32
BF16 = jnp.bfloat16

GRID_W = 64
EPS = 1e-6
MLA_HEADS = 8
QK_NOPE = 64
QK_ROPE = 32
QK_HEAD = QK_NOPE + QK_ROPE
V_HEAD = 64
Q_LORA = 256
KV_LORA = 128
ROPE_BASE = 10000.0
ATTN_SCALE = QK_HEAD ** -0.5
RW_HEADS = 8
RW_HEAD = 64
RW_DIM = RW_HEADS * RW_HEAD
DECAY_LORA = 64
AICL_LORA = 64
GATE_LORA = 160
GN_EPS = 64e-5
N_EXPERTS = 256
TOP_K = 8
N_GROUPS = 8
TOPK_GROUPS = 4
ROUTED_SCALE = 2.5

LANES = 128
MXU_DIM = 256
VMEM_LIMIT = 56 * 1024 * 1024

WKV_CHUNK = 64
WKV_GROUP = 128
HEADS_PER_GROUP = WKV_GROUP // RW_HEAD
HEAD_PAD = LANES


def _dot(a, b):
    return jnp.dot(a.astype(BF16), b.astype(BF16), preferred_element_type=F32)


def _dot_t(a, b):
    return lax.dot_general(a.astype(BF16), b.astype(BF16), (((1,), (1,)), ((), ())),
                           preferred_element_type=F32)


def _split3(x):
    h = x.astype(BF16)
    r1 = x - h.astype(F32)
    m = r1.astype(BF16)
    lo = (r1 - m.astype(F32)).astype(BF16)
    return h, m, lo


def _dot_hi(a_bf16_exact, x):
    h, m, lo = _split3(x)
    d = lambda y: jnp.dot(a_bf16_exact, y, preferred_element_type=F32)
    return d(h) + d(m) + d(lo)


def _wkv_chunk(r, v, kk, lw, bk, kd, s0, reverse, emit):
    L, G = r.shape
    row = lax.broadcasted_iota(jnp.int32, (L, G), 0)
    lane = lax.broadcasted_iota(jnp.int32, (L, G), 1)
    diff = (lane % L - row) if reverse else (row - lane % L)
    strict = diff > 0
    incl = diff >= 0
    r2 = lax.broadcasted_iota(jnp.int32, (L, L), 0)
    c2 = lax.broadcasted_iota(jnp.int32, (L, L), 1)
    tri = jnp.where(((c2 - r2) if reverse else (r2 - c2)) >= 0, 1.0, 0.0).astype(BF16)
    tri_ones = jnp.concatenate([tri, jnp.ones((L, L), BF16)], axis=0)
    lane_head = lane // RW_HEAD
    rowb = lax.broadcasted_iota(jnp.int32, (G, G), 0) // RW_HEAD
    colb = lax.broadcasted_iota(jnp.int32, (G, G), 1) // RW_HEAD

    def stack(x):
        return jnp.concatenate(
            [jnp.where(lane_head == h, x, 0.0) for h in range(HEADS_PER_GROUP)], axis=0).astype(BF16)

    sums = _dot_hi(tri_ones, lw)
    cum_in = sums[:L]
    g_in = jnp.exp(cum_in)
    g_inv = jnp.exp(-cum_in)
    g_ex = jnp.exp(cum_in - lw)
    g_tot = jnp.exp(sums[L:])
    a_h = -kk * g_ex
    b_h = bk * g_inv
    k_h = kd * g_inv
    r_h = r * g_in
    yield

    gram = _dot_t(jnp.concatenate([a_h, r_h], axis=0),
                  jnp.concatenate([stack(b_h), stack(k_h)], axis=0))
    ab = jnp.where(strict, gram[:L, :G], 0.0)
    ak = jnp.where(strict, gram[:L, G:], 0.0)
    rb = jnp.where(incl, gram[L:, :G], 0.0)
    rk = jnp.where(incl, gram[L:, G:], 0.0)
    yield

    tm = jnp.where(diff == 0, 1.0, 0.0) + ab
    p = ab
    v_st = stack(v)
    akv = _dot(ak, v_st)
    for _ in range(int(np.log2(L)) - 1):
        p = _dot(p, stack(p))
        yield
        tm = tm + _dot(tm, stack(p))
        yield

    wu = _dot(tm, jnp.concatenate([stack(a_h), stack(akv)], axis=1))
    w_t, u_t = wu[:, :G], wu[:, G:]
    yield
    rbwu = _dot(rb, jnp.concatenate([stack(w_t), stack(u_t)], axis=1))
    r_t = r_h + rbwu[:, :G]
    y_t = rbwu[:, G:] + _dot(rk, v_st)
    yield

    s0_st = stack(s0)
    y = _dot_t(r_t, s0_st) + y_t
    u = _dot_t(w_t, s0_st) + u_t
    yield
    uv = jnp.concatenate([u, v], axis=0)
    bkc = jnp.concatenate([b_h, k_h], axis=0)
    upd = lax.dot_general(uv.astype(BF16), bkc.astype(BF16), (((0,), (0,)), ((), ())),
                          preferred_element_type=F32)
    upd = jnp.where(rowb == colb, upd, 0.0)
    upd_d = upd[0:L]
    for h in range(1, HEADS_PER_GROUP):
        upd_d = upd_d + upd[h * L:(h + 1) * L]
    emit(y, (s0 + upd_d) * g_tot)


def _wkv_kernel(*refs, n_groups, n_batch):
    ins, (yf_ref, yb_ref, s_ref) = refs[:12], refs[12:]
    G = WKV_GROUP

    @pl.when(pl.program_id(1) == 0)
    def _():
        s_ref[...] = jnp.zeros_like(s_ref)

    chains = []
    for d, y_ref in enumerate((yf_ref, yb_ref)):
        r_ref, v_ref, kk_ref, lw_ref, bk_ref, kd_ref = ins[6 * d:6 * d + 6]
        for bi in range(n_batch):
            for g in range(n_groups):
                sl = slice(g * G, (g + 1) * G)

                def emit(y, s_new, y_ref=y_ref, d=d, bi=bi, g=g, sl=sl):
                    y_ref[bi, :, sl] = y
                    s_ref[d, bi, g] = s_new

                chains.append(_wkv_chunk(
                    r_ref[bi, :, sl], v_ref[bi, :, sl], kk_ref[bi, :, sl], lw_ref[0, bi, :, sl],
                    bk_ref[0, bi, :, sl], kd_ref[0, bi, :, sl], s_ref[d, bi, g], d == 1, emit))
    while chains:
        chains = [c for c in chains if next(c, StopIteration) is not StopIteration]


WKV_BATCH = 2


def _wkv_scan(r, v, kk, lw, bk, kd, n_ctx):
    B, Ttot, C = r.shape
    L = WKV_CHUNK
    nc = Ttot // L
    ncc = n_ctx // L
    nl = nc - ncc
    n_groups = C // WKV_GROUP
    nb = WKV_BATCH

    cid = (lambda s: s, lambda s: jnp.where(s < ncc, ncc - 1 - s, nc + ncc - 1 - s))
    first_out = (0, nl - 1)
    in_specs, out_specs = [], []
    for d in range(2):
        shared = pl.BlockSpec((nb, L, C), lambda b, s, d=d: (b, cid[d](s), 0))
        per_dir = pl.BlockSpec((1, nb, L, C), lambda b, s, d=d: (d, b, cid[d](s), 0))
        in_specs += [shared, shared, shared, per_dir, per_dir, per_dir]
        out_specs.append(pl.BlockSpec(
            (nb, L, C), lambda b, s, d=d: (b, jnp.where(s < ncc, first_out[d], cid[d](s) - ncc), 0)))
    out = jax.ShapeDtypeStruct((B, nl * L, C), F32)
    return pl.pallas_call(
        functools.partial(_wkv_kernel, n_groups=n_groups, n_batch=nb),
        grid=(B // nb, nc),
        in_specs=in_specs,
        out_specs=out_specs,
        out_shape=[out, out],
        scratch_shapes=[pltpu.VMEM((2, nb, n_groups, L, WKV_GROUP), F32)],
        compiler_params=pltpu.CompilerParams(
            dimension_semantics=("arbitrary", "arbitrary"), vmem_limit_bytes=VMEM_LIMIT),
        name="wkv_scan",
    )(r, v, kk, lw, bk, kd, r, v, kk, lw, bk, kd)


def kernel(x, c, ctx, c_ctx, ada_w, ada_b, norm_mix, norm_ffn, w_in, shift_conv, q_lat_norm, w_uq, kv_lat_norm, w_ukv, q_norm, k_norm, w_o_mla, decay_w0, decay_w2, aicl_a0, aicl_a2, k_k, k_a, r_k, gn_w, gn_b, gate_g2, w_o_rwkv, w_out, router_w, router_bias, expert_w1, expert_w3, expert_w2, shared_w1, shared_w3, shared_w2):
    B, T, D = x.shape
    n_ctx = ctx.shape[1]
    i0 = 0
    p = dict(norm_mix=norm_mix[i0], norm_ffn=norm_ffn[i0], w_in=w_in[i0], shift_conv=shift_conv[i0],
             q_lat_norm=q_lat_norm[i0], w_uq=w_uq[i0], kv_lat_norm=kv_lat_norm[i0], w_ukv=w_ukv[i0],
             q_norm=q_norm[i0], k_norm=k_norm[i0], w_o_mla=w_o_mla[i0],
             decay_w0=decay_w0[i0], decay_w2=decay_w2[i0], aicl_a0=aicl_a0[i0], aicl_a2=aicl_a2[i0],
             k_k=k_k[i0], k_a=k_a[i0], r_k=r_k[i0], gn_w=gn_w[i0], gn_b=gn_b[i0], gate_g2=gate_g2[i0],
             w_o_rwkv=w_o_rwkv[i0], w_out=w_out[i0], router_w=router_w[i0], router_bias=router_bias[i0],
             expert_w1=expert_w1[i0], expert_w3=expert_w3[i0], expert_w2=expert_w2[i0],
             shared_w1=shared_w1[i0], shared_w3=shared_w3[i0], shared_w2=shared_w2[i0])

    mod = _ada_modulation(c, c_ctx, ada_w[i0], ada_b[i0])
    sh_a, sc_a, g_a, sh_m, sc_m, g_m = jnp.split(mod[:B], 6, axis=-1)
    csh_a, csc_a = jnp.split(mod[B], 6, axis=-1)[:2]
    sh2 = jnp.stack([jnp.broadcast_to(csh_a, (B, D)), sh_a], axis=1)
    sc2 = jnp.stack([jnp.broadcast_to(csc_a, (B, D)), sc_a], axis=1)

    proj = _in_proj(ctx, x, sh2, sc2, p['norm_mix'], _pack_w_in(p['w_in']))

    q, k, v = _mla_prep(proj, p, n_ctx, T)
    att = _attention(q, k, v)

    r, vv, kk, lw, bk, kd, bonus, gg = _rwkv_prep(proj, p, n_ctx)
    y_f, y_b = _wkv_scan(r, vv, kk, lw, bk, kd, n_ctx)

    x1 = _merge(att, y_f, y_b, bonus, gg, proj, x, g_a, p, n_ctx)
    return _moe(x1, sh_m, sc_m, g_m, p)


SUBLANES = 8


def _ada_kernel(c_ref, w_ref, b_ref, o_ref):
    cc = c_ref[...]
    w = w_ref[...]
    w_hi = w.astype(BF16)
    w_lo = (w - w_hi.astype(F32)).astype(BF16)
    o_ref[...] = _dot3(cc * jax.nn.sigmoid(cc), w_hi, w_lo) + b_ref[...]


def _ada_modulation(c, c_ctx, ada_w, ada_b):
    B, D = c.shape
    n_out = ada_w.shape[1]
    rows = -(-(B + 1) // SUBLANES) * SUBLANES
    c_all = jnp.concatenate([c, c_ctx[None, :], jnp.zeros((rows - B - 1, D), F32)], axis=0)
    out = pl.pallas_call(
        _ada_kernel,
        grid=(n_out // D,),
        in_specs=[pl.BlockSpec((rows, D), lambda j: (0, 0)),
                  pl.BlockSpec((D, D), lambda j: (0, j)),
                  pl.BlockSpec((1, D), lambda j: (0, j))],
        out_specs=pl.BlockSpec((rows, D), lambda j: (0, j)),
        out_shape=jax.ShapeDtypeStruct((rows, n_out), F32),
        compiler_params=pltpu.CompilerParams(dimension_semantics=("arbitrary",), vmem_limit_bytes=VMEM_LIMIT),
        name="ada_modulation",
    )(c_all, ada_w, ada_b.reshape(1, n_out))
    return out[:B + 1]


ROW_TILE = 256
PROJ_RW = 2048
PROJ_GATE = 2048
PROJ_MLA = 512
PROJ_W = PROJ_RW + PROJ_GATE + PROJ_MLA
MLA_IN = Q_LORA + KV_LORA + QK_ROPE
RW_SPLITS = (RW_DIM, RW_DIM, RW_DIM, DECAY_LORA, DECAY_LORA, AICL_LORA, AICL_LORA, GATE_LORA)
RW_IN = sum(RW_SPLITS)


def _pack_w_in(w_in):
    w_mla = w_in[:, :MLA_IN]
    w_rw = w_in[:, MLA_IN:MLA_IN + RW_IN]
    w_gate = w_in[:, MLA_IN + RW_IN:]
    pad = lambda w, n: jnp.pad(w, ((0, 0), (0, n - w.shape[1])))
    return jnp.concatenate([pad(w_rw, PROJ_RW), w_gate, pad(w_mla, PROJ_MLA)], axis=1).astype(BF16)


PROJ_PIECES = 3
PROJ_COL_CHUNK = PROJ_W // 3


def _in_proj_kernel(ctx_ref, *refs):
    x_refs, (sh_ref, sc_ref, gain_ref, w_ref, o_ref) = refs[:PROJ_PIECES], refs[PROJ_PIECES:]
    first = pl.program_id(1) == 0
    pieces = []
    for n, x_ref in enumerate(x_refs):
        is_ctx = first if n == 0 else False
        xt = jnp.where(is_ctx, ctx_ref[0], x_ref[0]) if n == 0 else x_ref[0]
        sh = jnp.where(is_ctx, sh_ref[0, 0:1], sh_ref[0, 1:2])
        sc = jnp.where(is_ctx, sc_ref[0, 0:1], sc_ref[0, 1:2])
        y = xt * lax.rsqrt(jnp.mean(xt * xt, axis=-1, keepdims=True) + EPS) * gain_ref[...]
        pieces.append((y * (1.0 + sc) + sh).astype(BF16))
    h = jnp.concatenate(pieces, axis=0)
    for c in range(PROJ_W // PROJ_COL_CHUNK):
        cols = slice(c * PROJ_COL_CHUNK, (c + 1) * PROJ_COL_CHUNK)
        o_ref[0, :, cols] = jnp.dot(h, w_ref[:, cols], preferred_element_type=F32).astype(o_ref.dtype)


def _in_proj(ctx, x, sh2, sc2, gain, w):
    B, T, D = x.shape
    n_ctx = ctx.shape[1]
    rows = PROJ_PIECES * ROW_TILE
    assert n_ctx == ROW_TILE and (n_ctx + T) % rows == 0
    nt = (n_ctx + T) // rows
    piece = lambda n: pl.BlockSpec(
        (1, ROW_TILE, D), lambda b, i, n=n: (b, jnp.maximum(PROJ_PIECES * i + n - 1, 0), 0))
    return pl.pallas_call(
        _in_proj_kernel,
        grid=(B, nt),
        in_specs=[pl.BlockSpec((1, ROW_TILE, D), lambda b, i: (b, 0, 0))]
                 + [piece(n) for n in range(PROJ_PIECES)]
                 + [pl.BlockSpec((1, 2, D), lambda b, i: (b, 0, 0)),
                    pl.BlockSpec((1, 2, D), lambda b, i: (b, 0, 0)),
                    pl.BlockSpec((1, D), lambda b, i: (0, 0)),
                    pl.BlockSpec((D, PROJ_W), lambda b, i: (0, 0))],
        out_specs=pl.BlockSpec((1, rows, PROJ_W), lambda b, i: (b, i, 0)),
        out_shape=jax.ShapeDtypeStruct((B, n_ctx + T, PROJ_W), BF16),
        compiler_params=pltpu.CompilerParams(
            dimension_semantics=("arbitrary", "arbitrary"), vmem_limit_bytes=VMEM_LIMIT),
        name="in_proj",
    )(ctx, *([x] * PROJ_PIECES), sh2, sc2, gain.reshape(1, D), w)


def _rms(x, gain):
    return x * lax.rsqrt(jnp.mean(x * x, axis=-1, keepdims=True) + EPS) * gain


def _rope_tables(n_tokens):
    rows = n_tokens // GRID_W
    row = jnp.repeat(jnp.arange(rows, dtype=F32), GRID_W)
    col = jnp.tile(jnp.arange(GRID_W, dtype=F32), rows)
    n_freq = QK_ROPE // 4
    inv_freq = ROPE_BASE ** (-jnp.arange(n_freq, dtype=F32) / n_freq)
    ang = jnp.concatenate([row[:, None] * inv_freq, col[:, None] * inv_freq], axis=-1)
    return jnp.cos(ang), jnp.sin(ang)


ROPE_HALF = QK_ROPE // 2
X1 = slice(QK_NOPE, QK_NOPE + ROPE_HALF)
X2 = slice(QK_NOPE + ROPE_HALF, QK_HEAD)


def _rot_cols(w):
    w3 = w.reshape(w.shape[0], MLA_HEADS, HEAD_PAD)
    rot = jnp.zeros_like(w3).at[:, :, X1].set(-w3[:, :, X2]).at[:, :, X2].set(w3[:, :, X1])
    return rot.reshape(w.shape)


def _swap_halves(g):
    return jnp.zeros_like(g).at[:, X1].set(g[:, X2]).at[:, X2].set(g[:, X1])


def _mla_weights(p):
    H = MLA_HEADS
    wq = jnp.pad(p['w_uq'].reshape(Q_LORA, H, QK_HEAD), ((0, 0), (0, 0), (0, HEAD_PAD - QK_HEAD)))
    wq = wq.reshape(Q_LORA, H * HEAD_PAD)
    wkv = p['w_ukv'].reshape(KV_LORA, H, QK_NOPE + V_HEAD)
    wk_lat = jnp.pad(wkv[:, :, :QK_NOPE], ((0, 0), (0, 0), (0, HEAD_PAD - QK_NOPE)))
    place = jnp.zeros((LANES, H, HEAD_PAD), F32).at[:QK_ROPE, :, QK_NOPE:QK_HEAD].set(
        jnp.broadcast_to(jnp.eye(QK_ROPE, dtype=F32)[:, None, :], (QK_ROPE, H, QK_ROPE)))
    wk = jnp.concatenate([wk_lat, place], axis=0).reshape(KV_LORA + LANES, H * HEAD_PAD)
    wv = wkv[:, :, QK_NOPE:].reshape(KV_LORA, H * V_HEAD)
    gq = jnp.pad(p['q_norm'], (0, HEAD_PAD - QK_HEAD)).reshape(1, HEAD_PAD)
    gk = jnp.pad(p['k_norm'], (0, HEAD_PAD - QK_HEAD)).reshape(1, HEAD_PAD)
    bf = lambda w: w.astype(BF16)
    return (bf(wq), bf(_rot_cols(wq)), bf(wk), bf(_rot_cols(wk)), bf(wv), gq, _swap_halves(gq), gk, _swap_halves(gk))


def _mla_tables(n_ctx, T):
    cos, sin = _rope_tables(T)
    c = jnp.ones((n_ctx + T, HEAD_PAD), F32).at[n_ctx:, X1].set(cos).at[n_ctx:, X2].set(cos)
    s = jnp.zeros((n_ctx + T, HEAD_PAD), F32).at[n_ctx:, X1].set(sin).at[n_ctx:, X2].set(sin)
    return c, s


def _mla_prep_kernel(x_ref, c_ref, s_ref, qlg_ref, kvg_ref, wq_ref, wqr_ref, wk_ref, wkr_ref, wv_ref,
                     gq_ref, gqp_ref, gk_ref, gkp_ref, q_ref, k_ref, v_ref):
    x = x_ref[0].astype(F32)
    ql = _rms(x[:, :Q_LORA], qlg_ref[...])
    kvl = _rms(x[:, Q_LORA:Q_LORA + KV_LORA], kvg_ref[...])
    k_in = jnp.concatenate([kvl, x[:, Q_LORA + KV_LORA:]], axis=1)
    cos, sin = c_ref[...], s_ref[...]

    def finish(raw, partner, g, g_swapped, scale, o_ref):
        gc, gs = g * cos, g_swapped * sin
        for h in range(MLA_HEADS):
            sl = slice(h * HEAD_PAD, (h + 1) * HEAD_PAD)
            rh = raw[:, sl]
            inv = lax.rsqrt(jnp.sum(rh * rh, axis=-1, keepdims=True) * (1.0 / QK_HEAD) + EPS) * scale
            o_ref[0, :, sl] = ((rh * gc + partner[:, sl] * gs) * inv).astype(o_ref.dtype)

    finish(_dot(ql, wq_ref[...]), _dot(ql, wqr_ref[...]), gq_ref[...], gqp_ref[...], ATTN_SCALE, q_ref)
    finish(_dot(k_in, wk_ref[...]), _dot(k_in, wkr_ref[...]), gk_ref[...], gkp_ref[...], 1.0, k_ref)
    v_ref[0] = _dot(kvl, wv_ref[...]).astype(v_ref.dtype)


def _mla_prep(proj, p, n_ctx, T):
    B, Tt, _ = proj.shape
    H = MLA_HEADS
    ctx_tiles = n_ctx // ROW_TILE
    mla_blk = (PROJ_RW + PROJ_GATE) // PROJ_MLA
    weights = _mla_weights(p)
    cos, sin = _mla_tables(n_ctx, T)
    const = lambda b, i: (0, 0)
    row = lambda b, i: (b, i, 0)
    tab = pl.BlockSpec((ROW_TILE, HEAD_PAD), lambda b, i: (i, 0))
    full = lambda a: pl.BlockSpec(a.shape, const)
    return pl.pallas_call(
        _mla_prep_kernel,
        grid=(B, Tt // ROW_TILE),
        in_specs=[pl.BlockSpec((1, ROW_TILE, PROJ_MLA), lambda b, i: (b, i, mla_blk)), tab, tab,
                  pl.BlockSpec((1, Q_LORA), const), pl.BlockSpec((1, KV_LORA), const)]
                 + [full(w) for w in weights],
        out_specs=[
            pl.BlockSpec((1, ROW_TILE, H * HEAD_PAD), lambda b, i: (b, jnp.maximum(i - ctx_tiles, 0), 0)),
            pl.BlockSpec((1, ROW_TILE, H * HEAD_PAD), row),
            pl.BlockSpec((1, ROW_TILE, H * V_HEAD), row),
        ],
        out_shape=[
            jax.ShapeDtypeStruct((B, T, H * HEAD_PAD), BF16),
            jax.ShapeDtypeStruct((B, Tt, H * HEAD_PAD), BF16),
            jax.ShapeDtypeStruct((B, Tt, H * V_HEAD), BF16),
        ],
        compiler_params=pltpu.CompilerParams(
            dimension_semantics=("arbitrary", "arbitrary"), vmem_limit_bytes=VMEM_LIMIT),
        name="mla_prep",
    )(proj, cos, sin, p['q_lat_norm'].reshape(1, Q_LORA), p['kv_lat_norm'].reshape(1, KV_LORA), *weights)


ATTN_Q_TILE = 1024
HEADS_PER_STEP = LANES // V_HEAD


ATTN_ROW_SPLIT = 8


def _attn_kernel(q_ref, k_ref, v_ref, o_ref):
    v2 = v_ref[0]
    rows = q_ref.shape[1] // ATTN_ROW_SPLIT
    work = [(hh, rs) for rs in range(ATTN_ROW_SPLIT) for hh in range(HEADS_PER_STEP)]

    def scores(hh, rs):
        sl = slice(hh * HEAD_PAD, (hh + 1) * HEAD_PAD)
        return _dot_t(q_ref[0, rs * rows:(rs + 1) * rows, sl], k_ref[0, :, sl])

    outs = {}
    s_next = scores(*work[0])
    for i, (hh, rs) in enumerate(work):
        s = s_next
        if i + 1 < len(work):
            s_next = scores(*work[i + 1])
        e = jnp.exp(s - jnp.max(s, axis=-1, keepdims=True))
        outs[hh, rs] = _dot(e, v2) / jnp.sum(e, axis=-1, keepdims=True)
    lane = lax.broadcasted_iota(jnp.int32, (rows, LANES), 1)
    for rs in range(ATTN_ROW_SPLIT):
        o_ref[0, rs * rows:(rs + 1) * rows] = jnp.where(lane < V_HEAD, outs[0, rs], outs[1, rs])


def _attention(q, k, v):
    B, T, _ = q.shape
    Kt = k.shape[1]
    assert T % ATTN_Q_TILE == 0 and ATTN_Q_TILE % (ATTN_ROW_SPLIT * SUBLANES) == 0
    hp = MLA_HEADS // HEADS_PER_STEP
    qw = HEADS_PER_STEP * HEAD_PAD
    return pl.pallas_call(
        _attn_kernel,
        grid=(B, hp, T // ATTN_Q_TILE),
        in_specs=[
            pl.BlockSpec((1, ATTN_Q_TILE, qw), lambda b, h, i: (b, i, h)),
            pl.BlockSpec((1, Kt, qw), lambda b, h, i: (b, 0, h)),
            pl.BlockSpec((1, Kt, LANES), lambda b, h, i: (b, 0, h)),
        ],
        out_specs=pl.BlockSpec((1, ATTN_Q_TILE, LANES), lambda b, h, i: (b, i, h)),
        out_shape=jax.ShapeDtypeStruct((B, T, MLA_HEADS * V_HEAD), F32),
        compiler_params=pltpu.CompilerParams(
            dimension_semantics=("arbitrary", "arbitrary", "arbitrary"), vmem_limit_bytes=VMEM_LIMIT),
        name="attention",
    )(q, k, v)


HALO = 16
LORA_W = LANES
GATE_W = PROJ_RW - 3 * RW_DIM - 2 * LORA_W


def _head_ones(width, head):
    i = np.arange(width) // head
    return jnp.asarray(i[:, None] == i[None, :], BF16)


def _head_sum(x, ones_bd):
    hi = x.astype(BF16)
    lo = (x - hi.astype(F32)).astype(BF16)
    return (jnp.dot(hi, ones_bd, preferred_element_type=F32)
            + jnp.dot(lo, ones_bd, preferred_element_type=F32))


def _dot3(a, b_hi, b_lo):
    hi = a.astype(BF16)
    lo = (a - hi.astype(F32)).astype(BF16)
    d = lambda u, w: jnp.dot(u, w, preferred_element_type=F32)
    return d(hi, b_hi) + (d(hi, b_lo) + d(lo, b_hi))


def _rwkv_prep_kernel(x_ref, prev_ref, next_ref, conv_ref, kkg_ref, ka_ref, rk_ref, w0_ref, a0_ref,
                      w2h_ref, w2l_ref, a2_ref, g2_ref, ones_ref,
                      r_ref, v_ref, kk_ref, lw_ref, bk_ref, kd_ref, bonus_ref, gg_ref, *, ctx_tiles, n_tiles):
    i = pl.program_id(1)
    x = x_ref[0].astype(F32)
    tm, W = x.shape
    C = RW_DIM
    first = (i == 0) | (i == ctx_tiles)
    last = (i == ctx_tiles - 1) | (i == n_tiles - 1)
    prev_row = jnp.where(first, 0.0, prev_ref[0, HALO - 1:HALO].astype(F32))
    next_row = jnp.where(last, 0.0, next_ref[0, 0:1].astype(F32))
    row = lax.broadcasted_iota(jnp.int32, (tm, W), 0)
    x_dn = jnp.where(row == 0, prev_row, pltpu.roll(x, 1, 0))
    x_up = jnp.where(row == tm - 1, next_row, pltpu.roll(x, tm - 1, 0))
    xc = x_dn * conv_ref[0:1] + x * conv_ref[1:2] + x_up * conv_ref[2:3]

    r, k, v = xc[:, :C], xc[:, C:2 * C], xc[:, 2 * C:3 * C]
    lora_w = jnp.tanh(xc[:, 3 * C:3 * C + LORA_W])
    lora_a = xc[:, 3 * C + LORA_W:3 * C + 2 * LORA_W]
    lg = xc[:, 3 * C + 2 * LORA_W:]
    ones_bd = ones_ref[...]
    kq = k * kkg_ref[...]
    kk = kq * lax.rsqrt(_head_sum(kq * kq, ones_bd) + 1e-12)
    r_ref[0], v_ref[0], kk_ref[0] = r, v, kk

    k_sum = jnp.zeros_like(k)
    for d in range(2):
        z = w0_ref[d:d + 1] + _dot3(lora_w, w2h_ref[d], w2l_ref[d])
        softplus_neg = jnp.maximum(-z, 0.0) + jnp.log(1.0 + jnp.exp(-jnp.abs(z)))
        lw_ref[d, 0] = -jnp.exp(-softplus_neg - 0.5)
        a = jax.nn.sigmoid(a0_ref[d:d + 1] + _dot(lora_a, a2_ref[d]))
        kd = k * (1.0 + (a - 1.0) * ka_ref[...])
        bk_ref[d, 0] = kk * a
        kd_ref[d, 0] = kd
        k_sum = k_sum + kd
    bonus_ref[0] = _head_sum(r * k_sum * rk_ref[...], ones_bd) * v
    gg_ref[0] = _dot(jax.nn.sigmoid(lg), g2_ref[...])


def _rwkv_prep(proj, p, n_ctx):
    B, Tt, _ = proj.shape
    C = RW_DIM
    nt = Tt // ROW_TILE
    hb = ROW_TILE // HALO
    pad_cols = lambda w: jnp.pad(w, ((0, 0), (0, PROJ_RW - w.shape[1])))
    w2 = jnp.stack([jnp.pad(p['decay_w2'][0], ((0, LORA_W - DECAY_LORA), (0, 0))),
                    jnp.pad(p['decay_w2'][1], ((DECAY_LORA, 0), (0, 0)))])
    w2h = w2.astype(BF16)
    w2l = (w2 - w2h.astype(F32)).astype(BF16)
    a2 = jnp.stack([jnp.pad(p['aicl_a2'][0], ((0, LORA_W - AICL_LORA), (0, 0))),
                    jnp.pad(p['aicl_a2'][1], ((AICL_LORA, 0), (0, 0)))]).astype(BF16)
    g2 = jnp.pad(p['gate_g2'], ((0, GATE_W - GATE_LORA), (0, 0))).astype(BF16)
    row = lambda b, i: (b, i, 0)
    drow = lambda b, i: (0, b, i, 0)
    const2 = lambda b, i: (0, 0)
    const3 = lambda b, i: (0, 0, 0)
    vec = pl.BlockSpec((1, C), const2)
    out_row = pl.BlockSpec((1, ROW_TILE, C), row)
    out_dir = pl.BlockSpec((2, 1, ROW_TILE, C), drow)
    sds = jax.ShapeDtypeStruct((B, Tt, C), F32)
    sds2 = jax.ShapeDtypeStruct((2, B, Tt, C), F32)
    ctx_tiles = n_ctx // ROW_TILE
    out_lat = pl.BlockSpec((1, ROW_TILE, C), lambda b, i: (b, jnp.maximum(i - ctx_tiles, 0), 0))
    sds_lat = jax.ShapeDtypeStruct((B, Tt - n_ctx, C), F32)
    return pl.pallas_call(
        functools.partial(_rwkv_prep_kernel, ctx_tiles=n_ctx // ROW_TILE, n_tiles=nt),
        grid=(B, nt),
        in_specs=[
            pl.BlockSpec((1, ROW_TILE, PROJ_RW), row),
            pl.BlockSpec((1, HALO, PROJ_RW), lambda b, i: (b, jnp.maximum(i * hb - 1, 0), 0)),
            pl.BlockSpec((1, HALO, PROJ_RW), lambda b, i: (b, jnp.minimum((i + 1) * hb, nt * hb - 1), 0)),
            pl.BlockSpec((3, PROJ_RW), const2),
            vec, vec, vec,
            pl.BlockSpec((2, C), const2),
            pl.BlockSpec((2, C), const2),
            pl.BlockSpec((2, LORA_W, C), const3),
            pl.BlockSpec((2, LORA_W, C), const3),
            pl.BlockSpec((2, LORA_W, C), const3),
            pl.BlockSpec((GATE_W, C), const2),
            pl.BlockSpec((C, C), const2),
        ],
        out_specs=[out_row, out_row, out_row, out_dir, out_dir, out_dir, out_lat, out_lat],
        out_shape=[sds, sds, sds, sds2, sds2, sds2, sds_lat, sds_lat],
        compiler_params=pltpu.CompilerParams(
            dimension_semantics=("arbitrary", "arbitrary"), vmem_limit_bytes=VMEM_LIMIT),
        name="rwkv_prep",
    )(proj, proj, proj, pad_cols(p['shift_conv']), p['k_k'].reshape(1, C), p['k_a'].reshape(1, C),
      p['r_k'].reshape(1, C), p['decay_w0'], p['aicl_a0'], w2h, w2l, a2, g2, _head_ones(C, RW_HEAD))


MERGE_PIECES = 2


def _merge_kernel(att_ref, yf_ref, yb_ref, bonus_ref, gg_ref, *refs):
    gate_refs, (x_ref, ga_ref, gnw_ref, gnb_ref, ones_ref, wm_ref, wr_ref, wo_ref, o_ref) = (
        refs[:MERGE_PIECES], refs[MERGE_PIECES:])
    D = x_ref.shape[-1]
    ones_bd = ones_ref[...]
    y = yf_ref[0] + yb_ref[0]
    yc = y - _head_sum(y, ones_bd) * (1.0 / RW_HEAD)
    var = _head_sum(yc * yc, ones_bd) * (1.0 / RW_HEAD)
    y_n = yc * lax.rsqrt(var + GN_EPS) * gnw_ref[...] + gnb_ref[...]
    rw = (y_n + bonus_ref[0]) * gg_ref[0]
    g = jax.nn.sigmoid(jnp.concatenate([r[0] for r in gate_refs], axis=0).astype(F32))
    mix = g[:, :D] * _dot(att_ref[0], wm_ref[...]) + g[:, D:] * _dot(rw, wr_ref[...])
    o_ref[0] = x_ref[0] + ga_ref[0] * _dot(mix, wo_ref[...])


def _merge(att, y_f, y_b, bonus, gg, proj, x, g_a, p, n_ctx):
    B, T, D = x.shape
    C = RW_DIM
    rows = MERGE_PIECES * ROW_TILE
    ctx_tiles = n_ctx // ROW_TILE
    gate_blk = PROJ_RW // PROJ_GATE
    row = lambda b, i: (b, i, 0)
    const = lambda b, i: (0, 0)
    gate_piece = lambda n: pl.BlockSpec(
        (1, ROW_TILE, PROJ_GATE), lambda b, i, n=n: (b, MERGE_PIECES * i + n + ctx_tiles, gate_blk))
    return pl.pallas_call(
        _merge_kernel,
        grid=(B, T // rows),
        in_specs=[
            pl.BlockSpec((1, rows, att.shape[-1]), row),
            pl.BlockSpec((1, rows, C), row),
            pl.BlockSpec((1, rows, C), row),
            pl.BlockSpec((1, rows, C), row),
            pl.BlockSpec((1, rows, C), row),
        ] + [gate_piece(n) for n in range(MERGE_PIECES)] + [
            pl.BlockSpec((1, rows, D), row),
            pl.BlockSpec((1, 1, D), lambda b, i: (b, 0, 0)),
            pl.BlockSpec((1, C), const),
            pl.BlockSpec((1, C), const),
            pl.BlockSpec((C, C), const),
            pl.BlockSpec(p['w_o_mla'].shape, const),
            pl.BlockSpec(p['w_o_rwkv'].shape, const),
            pl.BlockSpec(p['w_out'].shape, const),
        ],
        out_specs=pl.BlockSpec((1, rows, D), row),
        out_shape=jax.ShapeDtypeStruct((B, T, D), F32),
        compiler_params=pltpu.CompilerParams(
            dimension_semantics=("arbitrary", "arbitrary"), vmem_limit_bytes=VMEM_LIMIT),
        name="merge",
    )(att, y_f, y_b, bonus, gg, *([proj] * MERGE_PIECES), x, g_a[:, None, :], p['gn_w'].reshape(1, C),
      p['gn_b'].reshape(1, C), _head_ones(C, RW_HEAD), p['w_o_mla'].astype(BF16), p['w_o_rwkv'].astype(BF16),
      p['w_out'].astype(BF16))


TILE_ROWS = 8
MOE_BLOCK = 512
ZERO_ROWS = MOE_BLOCK + TILE_ROWS


ROUTER_TILE = 1024
NEG_INF = float("-inf")
HI_MASK = 0xFFFF0000


def _pack_pair(lo, hi):
    lo_b = lax.bitcast_convert_type(lo.astype(BF16).astype(F32), jnp.uint32)
    hi_b = lax.bitcast_convert_type(hi.astype(BF16).astype(F32), jnp.uint32)
    return (lo_b >> 16) | (hi_b & jnp.uint32(HI_MASK))


def _unpack_pair(w):
    lo = lax.bitcast_convert_type(w << 16, F32)
    hi = lax.bitcast_convert_type(w & jnp.uint32(HI_MASK), F32)
    return lo.astype(BF16), hi.astype(BF16)


def _row_max(x):
    return jnp.max(x, axis=-1, keepdims=True)


def _first_index_of(x, value, lane_f):
    return jnp.min(jnp.where(x == value, lane_f, float(x.shape[-1])), axis=-1, keepdims=True)


def _router_kernel(x_ref, gain_ref, sh_ref, sc_ref, wh_ref, wm_ref, wl_ref, bias_ref,
                   u_ref, ids_ref, gates_ref, ranks_ref, counts_ref, carry_ref):
    @pl.when((pl.program_id(0) == 0) & (pl.program_id(1) == 0))
    def _():
        carry_ref[...] = jnp.zeros_like(carry_ref)

    x = x_ref[0]
    tm, D = x.shape
    E = bias_ref.shape[-1]
    u = x * lax.rsqrt(jnp.mean(x * x, axis=-1, keepdims=True) + EPS) * gain_ref[...]
    u = u * (1.0 + sc_ref[0]) + sh_ref[0]
    u_ref[0] = _pack_pair(u[:, :D // 2], u[:, D // 2:])

    uh, um, ul = _split3(u)
    wh, wm, wl = wh_ref[...], wm_ref[...], wl_ref[...]
    d = lambda a, b: jnp.dot(a, b, preferred_element_type=F32)
    logits = d(uh, wh) + (d(uh, wm) + d(um, wh)) + (d(uh, wl) + d(um, wm) + d(ul, wh))
    scores = jax.nn.sigmoid(logits)
    sel = scores + bias_ref[...]

    lane_i = lax.broadcasted_iota(jnp.int32, (tm, E), 1)
    lane_f = lane_i.astype(F32)
    out_f = lax.broadcasted_iota(jnp.int32, (tm, LANES), 1).astype(F32)
    per_group = E // N_GROUPS
    grp_f = jnp.floor(lane_f * (1.0 / per_group))

    gs = jnp.full((tm, LANES), NEG_INF, F32)
    for g in range(N_GROUPS):
        sg = jnp.where(lane_i >= g * per_group, jnp.where(lane_i < (g + 1) * per_group, sel, NEG_INF), NEG_INF)
        m1 = _row_max(sg)
        i1 = _first_index_of(sg, m1, lane_f)
        m2 = _row_max(jnp.where(lane_f == i1, NEG_INF, sg))
        gs = jnp.where(out_f == g, m1 + m2, gs)

    allow = jnp.zeros((tm, E), F32)
    for _ in range(TOPK_GROUPS):
        m = _row_max(gs)
        i = _first_index_of(gs, m, out_f)
        gs = jnp.where(out_f == i, NEG_INF, gs)
        allow = jnp.where(grp_f == i, 1.0, allow)
    selm = jnp.where(allow > 0.0, sel, NEG_INF)

    ids = jnp.zeros((tm, LANES), F32)
    gts = jnp.zeros((tm, LANES), F32)
    member = jnp.zeros((tm, E), F32)
    idx_cols = []
    gsum = jnp.zeros((tm, 1), F32)
    for k in range(TOP_K):
        m = _row_max(selm)
        i = _first_index_of(selm, m, lane_f)
        hit = lane_f == i
        gk = jnp.sum(jnp.where(hit, scores, 0.0), axis=-1, keepdims=True)
        selm = jnp.where(hit, NEG_INF, selm)
        member = jnp.where(hit, 1.0, member)
        ids = jnp.where(out_f == k, i, ids)
        gts = jnp.where(out_f == k, gk, gts)
        idx_cols.append(i)
        gsum = gsum + gk
    gts = gts / gsum * ROUTED_SCALE

    r2 = lax.broadcasted_iota(jnp.int32, (tm, tm), 0)
    c2 = lax.broadcasted_iota(jnp.int32, (tm, tm), 1)
    before = jnp.where(r2 > c2, 1.0, 0.0).astype(BF16)
    mem_b = member.astype(BF16)
    carry = carry_ref[...]
    pos = carry + jnp.dot(before, mem_b, preferred_element_type=F32)
    rk = jnp.zeros((tm, LANES), F32)
    for k in range(TOP_K):
        rk = jnp.where(out_f == k, jnp.sum(jnp.where(lane_f == idx_cols[k], pos, 0.0), axis=-1, keepdims=True), rk)
    colsum = jnp.dot(jnp.ones((8, tm), BF16), mem_b, preferred_element_type=F32)[0:1]
    carry_ref[...] = carry + colsum
    counts_ref[...] = carry + colsum
    ids_ref[0] = ids.astype(jnp.int32)
    ranks_ref[0] = rk.astype(jnp.int32)
    gates_ref[0] = gts


def _router(x1, sh_m, sc_m, p):
    B, T, D = x1.shape
    E = N_EXPERTS
    wh, wm, wl = _split3(p['router_w'])
    row = lambda b, i: (b, i, 0)
    const = lambda b, i: (0, 0)
    vec = lambda b, i: (b, 0, 0)
    lane_out = lambda dt: jax.ShapeDtypeStruct((B, T, LANES), dt)
    return pl.pallas_call(
        _router_kernel,
        grid=(B, T // ROUTER_TILE),
        in_specs=[
            pl.BlockSpec((1, ROUTER_TILE, D), row),
            pl.BlockSpec((1, D), const),
            pl.BlockSpec((1, 1, D), vec),
            pl.BlockSpec((1, 1, D), vec),
            pl.BlockSpec((D, E), const),
            pl.BlockSpec((D, E), const),
            pl.BlockSpec((D, E), const),
            pl.BlockSpec((1, E), const),
        ],
        out_specs=[
            pl.BlockSpec((1, ROUTER_TILE, D // 2), row),
            pl.BlockSpec((1, ROUTER_TILE, LANES), row),
            pl.BlockSpec((1, ROUTER_TILE, LANES), row),
            pl.BlockSpec((1, ROUTER_TILE, LANES), row),
            pl.BlockSpec((1, E), const),
        ],
        out_shape=[
            jax.ShapeDtypeStruct((B, T, D // 2), jnp.uint32),
            lane_out(jnp.int32), lane_out(F32), lane_out(jnp.int32),
            jax.ShapeDtypeStruct((1, E), F32),
        ],
        scratch_shapes=[pltpu.VMEM((1, E), F32)],
        compiler_params=pltpu.CompilerParams(
            dimension_semantics=("arbitrary", "arbitrary"), vmem_limit_bytes=VMEM_LIMIT),
        name="router",
    )(x1, p['norm_ffn'].reshape(1, D), sh_m[:, None, :], sc_m[:, None, :], wh, wm, wl,
      p['router_bias'].reshape(1, E))


def _slot_kernel(ids_ref, ranks_ref, base_ref, o_ref):
    ids = ids_ref[...].astype(F32)
    tm = ids.shape[0]
    E = base_ref.shape[-1]
    lane_e = lax.broadcasted_iota(jnp.int32, (tm, E), 1).astype(F32)
    out_lane = lax.broadcasted_iota(jnp.int32, (tm, LANES), 1)
    first = jnp.zeros((tm, LANES), F32)
    for k in range(TOP_K):
        fk = jnp.sum(jnp.where(lane_e == ids[:, k:k + 1], base_ref[...], 0.0), axis=-1, keepdims=True)
        first = jnp.where(out_lane == k, fk, first)
    o_ref[...] = first.astype(jnp.int32) + ranks_ref[...]


def _slots(ids, ranks, base):
    n_tok = ids.shape[0]
    E = base.shape[-1]
    row = pl.BlockSpec((ROUTER_TILE, LANES), lambda i: (i, 0))
    return pl.pallas_call(
        _slot_kernel,
        grid=(n_tok // ROUTER_TILE,),
        in_specs=[row, row, pl.BlockSpec((1, E), lambda i: (0, 0))],
        out_specs=row,
        out_shape=jax.ShapeDtypeStruct((n_tok, LANES), jnp.int32),
        compiler_params=pltpu.CompilerParams(dimension_semantics=("arbitrary",)),
        name="slots",
    )(ids, ranks, base)


DMA_PRIORITIES = 2


def _row_copy(src, src_row, dst, dst_row, sem):
    return pltpu.make_async_copy(src.at[pl.ds(src_row, 1)], dst.at[pl.ds(dst_row, 1)], sem)


def _dispatch_kernel(pad_ref, dest_ref, u_ref, slots_ref, zero_ref, sem):
    tm = u_ref.shape[0]
    n_experts = pad_ref.shape[0]

    @pl.when(pl.program_id(0) == 0)
    def _():
        zero_ref[...] = jnp.zeros_like(zero_ref)

        def fill(e, carry):
            start = pl.multiple_of(pad_ref[e], TILE_ROWS)
            pltpu.make_async_copy(zero_ref, slots_ref.at[pl.ds(start, ZERO_ROWS)], sem).start()
            return carry

        lax.fori_loop(0, n_experts, fill, 0)

        def filled(e, carry):
            pltpu.make_async_copy(zero_ref, slots_ref.at[pl.ds(0, ZERO_ROWS)], sem).wait()
            return carry

        lax.fori_loop(0, n_experts, filled, 0)

    def issue(r, carry):
        for k in range(TOP_K):
            _row_copy(u_ref, r, slots_ref, dest_ref[0, 0, r * TOP_K + k], sem).start(priority=k % DMA_PRIORITIES)
        return carry

    lax.fori_loop(0, tm, issue, 0)
    for k in range(TOP_K):
        pltpu.make_async_copy(u_ref, slots_ref.at[pl.ds(0, tm)], sem).wait()


def _dispatch(pad_start, dest, u_rows, n_slots):
    n_tok, W = u_rows.shape
    nt = n_tok // ROW_TILE
    grid_spec = pltpu.PrefetchScalarGridSpec(
        num_scalar_prefetch=1,
        grid=(nt,),
        in_specs=[
            pl.BlockSpec((1, 1, ROW_TILE * TOP_K), lambda i, pad: (i, 0, 0), memory_space=pltpu.SMEM),
            pl.BlockSpec((ROW_TILE, W), lambda i, pad: (i, 0)),
        ],
        out_specs=pl.BlockSpec(memory_space=pl.ANY),
        scratch_shapes=[pltpu.VMEM((ZERO_ROWS, W), jnp.uint32), pltpu.SemaphoreType.DMA(())],
    )
    return pl.pallas_call(
        _dispatch_kernel,
        grid_spec=grid_spec,
        out_shape=jax.ShapeDtypeStruct((n_slots + ZERO_ROWS, W), jnp.uint32),
        compiler_params=pltpu.CompilerParams(
            dimension_semantics=("arbitrary",), vmem_limit_bytes=VMEM_LIMIT),
        name="dispatch",
    )(pad_start, dest, u_rows)


WEIGHT_SLOTS = 2


def _expert_weight_copies(hbm_refs, stage_refs, expert, slot, sems):
    return [pltpu.make_async_copy(w.at[expert], s.at[slot], sems.at[slot, i])
            for i, (w, s) in enumerate(zip(hbm_refs, stage_refs))]


def _moe_ffn_kernel(be_ref, first_ref, slot_ref, next_ref, nu_ref, x_ref, w1_ref, w3_ref, w2_ref, o_ref,
                    s1_ref, s3_ref, s2_ref, w13_ref, w2b_ref, sems):
    j = pl.program_id(0)
    F = w1_ref.shape[-1]
    half = x_ref.shape[-1]
    copies = functools.partial(_expert_weight_copies, (w1_ref, w3_ref, w2_ref), (s1_ref, s3_ref, s2_ref), sems=sems)

    @pl.when(j == 0)
    def _():
        for c in copies(be_ref[0], 0):
            c.start()

    @pl.when(first_ref[j] == 1)
    def _():
        slot = slot_ref[j]
        for c in copies(be_ref[j], slot):
            c.wait()

        @pl.when(next_ref[j] >= 0)
        def _():
            for c in copies(next_ref[j], 1 - slot):
                c.start()

        w13_ref[:, :F] = s1_ref[slot].astype(BF16)
        w13_ref[:, F:] = s3_ref[slot].astype(BF16)
        w2b_ref[...] = s2_ref[slot].astype(BF16)

    @pl.when(j < nu_ref[0])
    def _():
        lo, hi = _unpack_pair(x_ref[...])
        h = (jnp.dot(lo, w13_ref[:half], preferred_element_type=F32)
             + jnp.dot(hi, w13_ref[half:], preferred_element_type=F32))
        h1, h3 = h[:, :F], h[:, F:]
        y = _dot(h1 * jax.nn.sigmoid(h1) * h3, w2b_ref[...])
        o_ref[...] = pltpu.einshape("s(cl)->(sc)l", y, c=TILE_ROWS)

    @pl.when(j >= nu_ref[0])
    def _():
        o_ref[...] = jnp.zeros_like(o_ref)


def _moe_ffn(block_e, n_used, x_slots, w1, w3, w2):
    W = x_slots.shape[1]
    E, D, F = w1.shape
    nblk = block_e.shape[0]
    slots = nblk * MOE_BLOCK
    idx = jnp.arange(nblk)
    first = jnp.concatenate([jnp.ones((1,), bool), block_e[1:] != block_e[:-1]])
    slot = ((jnp.cumsum(first) - 1) % WEIGHT_SLOTS).astype(jnp.int32)
    later_first = lax.cummin(jnp.where(first, idx, nblk)[::-1])[::-1]
    next_first = jnp.concatenate([later_first[1:], jnp.full((1,), nblk)])
    next_e = jnp.where(next_first < nblk, block_e[jnp.minimum(next_first, nblk - 1)], -1).astype(jnp.int32)
    n_prefetch = 5
    blk = lambda j, *_: (jnp.minimum(j, _[4][0] - 1), 0)
    grid_spec = pltpu.PrefetchScalarGridSpec(
        num_scalar_prefetch=n_prefetch,
        grid=(nblk,),
        in_specs=[
            pl.BlockSpec((MOE_BLOCK, W), blk),
            pl.BlockSpec(memory_space=pl.ANY),
            pl.BlockSpec(memory_space=pl.ANY),
            pl.BlockSpec(memory_space=pl.ANY),
        ],
        out_specs=pl.BlockSpec((MOE_BLOCK * TILE_ROWS, LANES), lambda j, *_: (j, 0)),
        scratch_shapes=[pltpu.VMEM((WEIGHT_SLOTS, D, F), F32), pltpu.VMEM((WEIGHT_SLOTS, D, F), F32),
                        pltpu.VMEM((WEIGHT_SLOTS, F, D), F32),
                        pltpu.VMEM((D, 2 * F), BF16), pltpu.VMEM((F, D), BF16),
                        pltpu.SemaphoreType.DMA((WEIGHT_SLOTS, 3))],
    )
    return pl.pallas_call(
        _moe_ffn_kernel,
        grid_spec=grid_spec,
        out_shape=jax.ShapeDtypeStruct((slots * TILE_ROWS, LANES), F32),
        compiler_params=pltpu.CompilerParams(
            dimension_semantics=("arbitrary",), vmem_limit_bytes=VMEM_LIMIT),
        name="moe_ffn",
    )(block_e, first.astype(jnp.int32), slot, next_e, n_used, x_slots, w1, w3, w2)


COMBINE_ROWS = 32


def _combine_kernel(dest_ref, gates_ref, u_ref, x_ref, gm_ref, w1_ref, w3_ref, w2_ref, ys_ref, o_ref, buf_ref, sem):
    tm, half = u_ref.shape

    def tile(i):
        return pl.ds(pl.multiple_of(i * TILE_ROWS, TILE_ROWS), TILE_ROWS)

    def issue(r, carry):
        for k in range(TOP_K):
            pltpu.make_async_copy(ys_ref.at[tile(dest_ref[0, 0, r * TOP_K + k])], buf_ref.at[k, tile(r)], sem).start(
                priority=k % DMA_PRIORITIES)
        return carry

    lax.fori_loop(0, tm, issue, 0)

    ulo, uhi = _unpack_pair(u_ref[...])
    both = lambda w_ref: (jnp.dot(ulo, w_ref[:half], preferred_element_type=F32)
                          + jnp.dot(uhi, w_ref[half:], preferred_element_type=F32))
    h1, h3 = both(w1_ref), both(w3_ref)
    o_ref[...] = x_ref[...] + gm_ref[0] * _dot(h1 * jax.nn.sigmoid(h1) * h3, w2_ref[...])

    for k in range(TOP_K):
        pltpu.make_async_copy(ys_ref.at[pl.ds(0, tm * TILE_ROWS)], buf_ref.at[k], sem).wait()

    gm = gm_ref[0]
    for rb in range(tm // COMBINE_ROWS):
        rows = slice(rb * COMBINE_ROWS, (rb + 1) * COMBINE_ROWS)
        gates = gates_ref[rows, :]
        for c in range(TILE_ROWS):
            acc = jnp.zeros((COMBINE_ROWS, LANES), F32)
            for k in range(TOP_K):
                acc = acc + gates[:, k:k + 1] * buf_ref[
                    k, pl.ds(rb * COMBINE_ROWS * TILE_ROWS + c, COMBINE_ROWS, stride=TILE_ROWS), :]
            cols = slice(c * LANES, (c + 1) * LANES)
            o_ref[rows, cols] = o_ref[rows, cols] + gm[:, cols] * acc


def _combine(dest, gates, u_rows, x1, g_m, y_slots, p):
    B, T, D = x1.shape
    n_tok = B * T
    W = u_rows.shape[1]
    tiles_per_batch = T // ROW_TILE
    row = lambda i: (i, 0)
    const = lambda i: (0, 0)
    out = pl.pallas_call(
        _combine_kernel,
        grid=(n_tok // ROW_TILE,),
        in_specs=[
            pl.BlockSpec((1, 1, ROW_TILE * TOP_K), lambda i: (i, 0, 0), memory_space=pltpu.SMEM),
            pl.BlockSpec((ROW_TILE, LANES), row),
            pl.BlockSpec((ROW_TILE, W), row),
            pl.BlockSpec((ROW_TILE, D), row),
            pl.BlockSpec((1, 1, D), lambda i: (i // tiles_per_batch, 0, 0)),
            pl.BlockSpec(p['shared_w1'].shape, const),
            pl.BlockSpec(p['shared_w3'].shape, const),
            pl.BlockSpec(p['shared_w2'].shape, const),
            pl.BlockSpec(memory_space=pl.ANY),
        ],
        out_specs=pl.BlockSpec((ROW_TILE, D), row),
        out_shape=jax.ShapeDtypeStruct((n_tok, D), F32),
        scratch_shapes=[pltpu.VMEM((TOP_K, ROW_TILE * TILE_ROWS, LANES), F32), pltpu.SemaphoreType.DMA(())],
        compiler_params=pltpu.CompilerParams(
            dimension_semantics=("arbitrary",), vmem_limit_bytes=VMEM_LIMIT),
        name="combine",
    )(dest, gates.reshape(n_tok, LANES), u_rows, x1.reshape(n_tok, D), g_m[:, None, :],
      p['shared_w1'].astype(BF16), p['shared_w3'].astype(BF16), p['shared_w2'].astype(BF16), y_slots)
    return out.reshape(B, T, D)


def _moe(x1, sh_m, sc_m, g_m, p):
    B, T, D = x1.shape
    n_tok = B * T
    u_packed, ids, gates, ranks, counts = _router(x1, sh_m, sc_m, p)

    counts = counts[0].astype(jnp.int32)
    padded = (counts + MOE_BLOCK - 1) // MOE_BLOCK * MOE_BLOCK
    padded_end = jnp.cumsum(padded)
    base = padded_end - padded
    n_blocks = n_tok * TOP_K // MOE_BLOCK + N_EXPERTS
    n_used = (padded_end[-1] // MOE_BLOCK).astype(jnp.int32)
    blk = jnp.minimum(jnp.arange(n_blocks), n_used - 1) * MOE_BLOCK
    block_e = jnp.minimum(jnp.sum(padded_end[None, :] <= blk[:, None], axis=1), N_EXPERTS - 1).astype(jnp.int32)
    dest = _slots(ids.reshape(n_tok, LANES), ranks.reshape(n_tok, LANES), base.astype(F32).reshape(1, N_EXPERTS))
    dest = dest[:, :TOP_K].reshape(n_tok // ROW_TILE, 1, ROW_TILE * TOP_K)

    u_rows = u_packed.reshape(n_tok, D // 2)
    pad_start = ((base + counts) // TILE_ROWS * TILE_ROWS).astype(jnp.int32)
    x_slots = _dispatch(pad_start, dest, u_rows, n_blocks * MOE_BLOCK)
    y_slots = _moe_ffn(block_e, n_used.reshape(1), x_slots, p['expert_w1'], p['expert_w3'], p['expert_w2'])
    return _combine(dest, gates, u_rows, x1, g_m, y_slots, p)
```

```python
import functools

import jax
import jax.numpy as jnp
import numpy as np
from jax import lax
from jax.experimental import pallas as pl
from jax.experimental.pallas import tpu as pltpu

F32 = jnp.float32
BF16 = jnp.bfloat16

GRID_W = 64
EPS = 1e-6
MLA_HEADS = 8
QK_NOPE = 64
QK_ROPE = 32
QK_HEAD = QK_NOPE + QK_ROPE
V_HEAD = 64
Q_LORA = 256
KV_LORA = 128
ROPE_BASE = 10000.0
ATTN_SCALE = QK_HEAD ** -0.5
RW_HEADS = 8
RW_HEAD = 64
RW_DIM = RW_HEADS * RW_HEAD
DECAY_LORA = 64
AICL_LORA = 64
GATE_LORA = 160
GN_EPS = 64e-5
N_EXPERTS = 256
TOP_K = 8
N_GROUPS = 8
TOPK_GROUPS = 4
ROUTED_SCALE = 2.5

LANES = 128
MXU_DIM = 256
VMEM_LIMIT = 56 * 1024 * 1024

WKV_CHUNK = 64
WKV_GROUP = 128
HEADS_PER_GROUP = WKV_GROUP // RW_HEAD
HEAD_PAD = LANES


def _dot(a, b):
    return jnp.dot(a.astype(BF16), b.astype(BF16), preferred_element_type=F32)


def _dot_t(a, b):
    return lax.dot_general(a.astype(BF16), b.astype(BF16), (((1,), (1,)), ((), ())),
                           preferred_element_type=F32)


def _split3(x):
    h = x.astype(BF16)
    r1 = x - h.astype(F32)
    m = r1.astype(BF16)
    lo = (r1 - m.astype(F32)).astype(BF16)
    return h, m, lo


def _dot_hi(a_bf16_exact, x):
    h, m, lo = _split3(x)
    d = lambda y: jnp.dot(a_bf16_exact, y, preferred_element_type=F32)
    return d(h) + d(m) + d(lo)


def _wkv_chunk(r, v, kk, lw, bk, kd, s0, reverse, emit):
    L, G = r.shape
    row = lax.broadcasted_iota(jnp.int32, (L, G), 0)
    lane = lax.broadcasted_iota(jnp.int32, (L, G), 1)
    diff = (lane % L - row) if reverse else (row - lane % L)
    strict = diff > 0
    incl = diff >= 0
    r2 = lax.broadcasted_iota(jnp.int32, (L, L), 0)
    c2 = lax.broadcasted_iota(jnp.int32, (L, L), 1)
    tri = jnp.where(((c2 - r2) if reverse else (r2 - c2)) >= 0, 1.0, 0.0).astype(BF16)
    tri_ones = jnp.concatenate([tri, jnp.ones((L, L), BF16)], axis=0)
    lane_head = lane // RW_HEAD
    rowb = lax.broadcasted_iota(jnp.int32, (G, G), 0) // RW_HEAD
    colb = lax.broadcasted_iota(jnp.int32, (G, G), 1) // RW_HEAD

    def stack(x):
        return jnp.concatenate(
            [jnp.where(lane_head == h, x, 0.0) for h in range(HEADS_PER_GROUP)], axis=0).astype(BF16)

    sums = _dot_hi(tri_ones, lw)
    cum_in = sums[:L]
    g_in = jnp.exp(cum_in)
    g_inv = jnp.exp(-cum_in)
    g_ex = jnp.exp(cum_in - lw)
    g_tot = jnp.exp(sums[L:])
    a_h = -kk * g_ex
    b_h = bk * g_inv
    k_h = kd * g_inv
    r_h = r * g_in
    yield

    gram = _dot_t(jnp.concatenate([a_h, r_h], axis=0),
                  jnp.concatenate([stack(b_h), stack(k_h)], axis=0))
    ab = jnp.where(strict, gram[:L, :G], 0.0)
    ak = jnp.where(strict, gram[:L, G:], 0.0)
    rb = jnp.where(incl, gram[L:, :G], 0.0)
    rk = jnp.where(incl, gram[L:, G:], 0.0)
    yield

    tm = jnp.where(diff == 0, 1.0, 0.0) + ab
    p = ab
    v_st = stack(v)
    akv = _dot(ak, v_st)
    for _ in range(int(np.log2(L)) - 1):
        p = _dot(p, stack(p))
        yield
        tm = tm + _dot(tm, stack(p))
        yield

    wu = _dot(tm, jnp.concatenate([stack(a_h), stack(akv)], axis=1))
    w_t, u_t = wu[:, :G], wu[:, G:]
    yield
    rbwu = _dot(rb, jnp.concatenate([stack(w_t), stack(u_t)], axis=1))
    r_t = r_h + rbwu[:, :G]
    y_t = rbwu[:, G:] + _dot(rk, v_st)
    yield

    s0_st = stack(s0)
    y = _dot_t(r_t, s0_st) + y_t
    u = _dot_t(w_t, s0_st) + u_t
    yield
    uv = jnp.concatenate([u, v], axis=0)
    bkc = jnp.concatenate([b_h, k_h], axis=0)
    upd = lax.dot_general(uv.astype(BF16), bkc.astype(BF16), (((0,), (0,)), ((), ())),
                          preferred_element_type=F32)
    upd = jnp.where(rowb == colb, upd, 0.0)
    upd_d = upd[0:L]
    for h in range(1, HEADS_PER_GROUP):
        upd_d = upd_d + upd[h * L:(h + 1) * L]
    emit(y, (s0 + upd_d) * g_tot)


def _wkv_kernel(*refs, n_groups, n_batch):
    ins, (yf_ref, yb_ref, s_ref) = refs[:12], refs[12:]
    G = WKV_GROUP

    @pl.when(pl.program_id(1) == 0)
    def _():
        s_ref[...] = jnp.zeros_like(s_ref)

    chains = []
    for d, y_ref in enumerate((yf_ref, yb_ref)):
        r_ref, v_ref, kk_ref, lw_ref, bk_ref, kd_ref = ins[6 * d:6 * d + 6]
        for bi in range(n_batch):
            for g in range(n_groups):
                sl = slice(g * G, (g + 1) * G)

                def emit(y, s_new, y_ref=y_ref, d=d, bi=bi, g=g, sl=sl):
                    y_ref[bi, :, sl] = y
                    s_ref[d, bi, g] = s_new

                chains.append(_wkv_chunk(
                    r_ref[bi, :, sl], v_ref[bi, :, sl], kk_ref[bi, :, sl], lw_ref[0, bi, :, sl],
                    bk_ref[0, bi, :, sl], kd_ref[0, bi, :, sl], s_ref[d, bi, g], d == 1, emit))
    while chains:
        chains = [c for c in chains if next(c, StopIteration) is not StopIteration]


WKV_BATCH = 2


def _wkv_scan(r, v, kk, lw, bk, kd, n_ctx):
    B, Ttot, C = r.shape
    L = WKV_CHUNK
    nc = Ttot // L
    ncc = n_ctx // L
    nl = nc - ncc
    n_groups = C // WKV_GROUP
    nb = WKV_BATCH

    cid = (lambda s: s, lambda s: jnp.where(s < ncc, ncc - 1 - s, nc + ncc - 1 - s))
    first_out = (0, nl - 1)
    in_specs, out_specs = [], []
    for d in range(2):
        shared = pl.BlockSpec((nb, L, C), lambda b, s, d=d: (b, cid[d](s), 0))
        per_dir = pl.BlockSpec((1, nb, L, C), lambda b, s, d=d: (d, b, cid[d](s), 0))
        in_specs += [shared, shared, shared, per_dir, per_dir, per_dir]
        out_specs.append(pl.BlockSpec(
            (nb, L, C), lambda b, s, d=d: (b, jnp.where(s < ncc, first_out[d], cid[d](s) - ncc), 0)))
    out = jax.ShapeDtypeStruct((B, nl * L, C), F32)
    return pl.pallas_call(
        functools.partial(_wkv_kernel, n_groups=n_groups, n_batch=nb),
        grid=(B // nb, nc),
        in_specs=in_specs,
        out_specs=out_specs,
        out_shape=[out, out],
        scratch_shapes=[pltpu.VMEM((2, nb, n_groups, L, WKV_GROUP), F32)],
        compiler_params=pltpu.CompilerParams(
            dimension_semantics=("arbitrary", "arbitrary"), vmem_limit_bytes=VMEM_LIMIT),
        name="wkv_scan",
    )(r, v, kk, lw, bk, kd, r, v, kk, lw, bk, kd)


def kernel(x, c, ctx, c_ctx, ada_w, ada_b, norm_mix, norm_ffn, w_in, shift_conv, q_lat_norm, w_uq, kv_lat_norm, w_ukv, q_norm, k_norm, w_o_mla, decay_w0, decay_w2, aicl_a0, aicl_a2, k_k, k_a, r_k, gn_w, gn_b, gate_g2, w_o_rwkv, w_out, router_w, router_bias, expert_w1, expert_w3, expert_w2, shared_w1, shared_w3, shared_w2):
    B, T, D = x.shape
    n_ctx = ctx.shape[1]
    i0 = 0
    p = dict(norm_mix=norm_mix[i0], norm_ffn=norm_ffn[i0], w_in=w_in[i0], shift_conv=shift_conv[i0],
             q_lat_norm=q_lat_norm[i0], w_uq=w_uq[i0], kv_lat_norm=kv_lat_norm[i0], w_ukv=w_ukv[i0],
             q_norm=q_norm[i0], k_norm=k_norm[i0], w_o_mla=w_o_mla[i0],
             decay_w0=decay_w0[i0], decay_w2=decay_w2[i0], aicl_a0=aicl_a0[i0], aicl_a2=aicl_a2[i0],
             k_k=k_k[i0], k_a=k_a[i0], r_k=r_k[i0], gn_w=gn_w[i0], gn_b=gn_b[i0], gate_g2=gate_g2[i0],
             w_o_rwkv=w_o_rwkv[i0], w_out=w_out[i0], router_w=router_w[i0], router_bias=router_bias[i0],
             expert_w1=expert_w1[i0], expert_w3=expert_w3[i0], expert_w2=expert_w2[i0],
             shared_w1=shared_w1[i0], shared_w3=shared_w3[i0], shared_w2=shared_w2[i0])

    mod = _ada_modulation(c, c_ctx, ada_w[i0], ada_b[i0])
    sh_a, sc_a, g_a, sh_m, sc_m, g_m = jnp.split(mod[:B], 6, axis=-1)
    csh_a, csc_a = jnp.split(mod[B], 6, axis=-1)[:2]
    sh2 = jnp.stack([jnp.broadcast_to(csh_a, (B, D)), sh_a], axis=1)
    sc2 = jnp.stack([jnp.broadcast_to(csc_a, (B, D)), sc_a], axis=1)

    proj = _in_proj(ctx, x, sh2, sc2, p['norm_mix'], _pack_w_in(p['w_in']))

    q, k, v = _mla_prep(proj, p, n_ctx, T)
    att = _attention(q, k, v)

    r, vv, kk, lw, bk, kd, bonus, gg = _rwkv_prep(proj, p, n_ctx)
    y_f, y_b = _wkv_scan(r, vv, kk, lw, bk, kd, n_ctx)

    x1 = _merge(att, y_f, y_b, bonus, gg, proj, x, g_a, p, n_ctx)
    return _moe(x1, sh_m, sc_m, g_m, p)


SUBLANES = 8


def _ada_kernel(c_ref, w_ref, b_ref, o_ref):
    cc = c_ref[...]
    w = w_ref[...]
    w_hi = w.astype(BF16)
    w_lo = (w - w_hi.astype(F32)).astype(BF16)
    o_ref[...] = _dot3(cc * jax.nn.sigmoid(cc), w_hi, w_lo) + b_ref[...]


def _ada_modulation(c, c_ctx, ada_w, ada_b):
    B, D = c.shape
    n_out = ada_w.shape[1]
    rows = -(-(B + 1) // SUBLANES) * SUBLANES
    c_all = jnp.concatenate([c, c_ctx[None, :], jnp.zeros((rows - B - 1, D), F32)], axis=0)
    out = pl.pallas_call(
        _ada_kernel,
        grid=(n_out // D,),
        in_specs=[pl.BlockSpec((rows, D), lambda j: (0, 0)),
                  pl.BlockSpec((D, D), lambda j: (0, j)),
                  pl.BlockSpec((1, D), lambda j: (0, j))],
        out_specs=pl.BlockSpec((rows, D), lambda j: (0, j)),
        out_shape=jax.ShapeDtypeStruct((rows, n_out), F32),
        compiler_params=pltpu.CompilerParams(dimension_semantics=("arbitrary",), vmem_limit_bytes=VMEM_LIMIT),
        name="ada_modulation",
    )(c_all, ada_w, ada_b.reshape(1, n_out))
    return out[:B + 1]


ROW_TILE = 256
PROJ_RW = 2048
PROJ_GATE = 2048
PROJ_MLA = 512
PROJ_W = PROJ_RW + PROJ_GATE + PROJ_MLA
MLA_IN = Q_LORA + KV_LORA + QK_ROPE
RW_SPLITS = (RW_DIM, RW_DIM, RW_DIM, DECAY_LORA, DECAY_LORA, AICL_LORA, AICL_LORA, GATE_LORA)
RW_IN = sum(RW_SPLITS)


def _pack_w_in(w_in):
    w_mla = w_in[:, :MLA_IN]
    w_rw = w_in[:, MLA_IN:MLA_IN + RW_IN]
    w_gate = w_in[:, MLA_IN + RW_IN:]
    pad = lambda w, n: jnp.pad(w, ((0, 0), (0, n - w.shape[1])))
    return jnp.concatenate([pad(w_rw, PROJ_RW), w_gate, pad(w_mla, PROJ_MLA)], axis=1).astype(BF16)


PROJ_PIECES = 3
PROJ_COL_CHUNK = PROJ_W // 3


def _in_proj_kernel(ctx_ref, *refs):
    x_refs, (sh_ref, sc_ref, gain_ref, w_ref, o_ref) = refs[:PROJ_PIECES], refs[PROJ_PIECES:]
    first = pl.program_id(1) == 0
    pieces = []
    for n, x_ref in enumerate(x_refs):
        is_ctx = first if n == 0 else False
        xt = jnp.where(is_ctx, ctx_ref[0], x_ref[0]) if n == 0 else x_ref[0]
        sh = jnp.where(is_ctx, sh_ref[0, 0:1], sh_ref[0, 1:2])
        sc = jnp.where(is_ctx, sc_ref[0, 0:1], sc_ref[0, 1:2])
        y = xt * lax.rsqrt(jnp.mean(xt * xt, axis=-1, keepdims=True) + EPS) * gain_ref[...]
        pieces.append((y * (1.0 + sc) + sh).astype(BF16))
    h = jnp.concatenate(pieces, axis=0)
    for c in range(PROJ_W // PROJ_COL_CHUNK):
        cols = slice(c * PROJ_COL_CHUNK, (c + 1) * PROJ_COL_CHUNK)
        o_ref[0, :, cols] = jnp.dot(h, w_ref[:, cols], preferred_element_type=F32).astype(o_ref.dtype)


def _in_proj(ctx, x, sh2, sc2, gain, w):
    B, T, D = x.shape
    n_ctx = ctx.shape[1]
    rows = PROJ_PIECES * ROW_TILE
    assert n_ctx == ROW_TILE and (n_ctx + T) % rows == 0
    nt = (n_ctx + T) // rows
    piece = lambda n: pl.BlockSpec(
        (1, ROW_TILE, D), lambda b, i, n=n: (b, jnp.maximum(PROJ_PIECES * i + n - 1, 0), 0))
    return pl.pallas_call(
        _in_proj_kernel,
        grid=(B, nt),
        in_specs=[pl.BlockSpec((1, ROW_TILE, D), lambda b, i: (b, 0, 0))]
                 + [piece(n) for n in range(PROJ_PIECES)]
                 + [pl.BlockSpec((1, 2, D), lambda b, i: (b, 0, 0)),
                    pl.BlockSpec((1, 2, D), lambda b, i: (b, 0, 0)),
                    pl.BlockSpec((1, D), lambda b, i: (0, 0)),
                    pl.BlockSpec((D, PROJ_W), lambda b, i: (0, 0))],
        out_specs=pl.BlockSpec((1, rows, PROJ_W), lambda b, i: (b, i, 0)),
        out_shape=jax.ShapeDtypeStruct((B, n_ctx + T, PROJ_W), BF16),
        compiler_params=pltpu.CompilerParams(
            dimension_semantics=("arbitrary", "arbitrary"), vmem_limit_bytes=VMEM_LIMIT),
        name="in_proj",
    )(ctx, *([x] * PROJ_PIECES), sh2, sc2, gain.reshape(1, D), w)


def _rms(x, gain):
    return x * lax.rsqrt(jnp.mean(x * x, axis=-1, keepdims=True) + EPS) * gain


def _rope_tables(n_tokens):
    rows = n_tokens // GRID_W
    row = jnp.repeat(jnp.arange(rows, dtype=F32), GRID_W)
    col = jnp.tile(jnp.arange(GRID_W, dtype=F32), rows)
    n_freq = QK_ROPE // 4
    inv_freq = ROPE_BASE ** (-jnp.arange(n_freq, dtype=F32) / n_freq)
    ang = jnp.concatenate([row[:, None] * inv_freq, col[:, None] * inv_freq], axis=-1)
    return jnp.cos(ang), jnp.sin(ang)


ROPE_HALF = QK_ROPE // 2
X1 = slice(QK_NOPE, QK_NOPE + ROPE_HALF)
X2 = slice(QK_NOPE + ROPE_HALF, QK_HEAD)


def _rot_cols(w):
    w3 = w.reshape(w.shape[0], MLA_HEADS, HEAD_PAD)
    rot = jnp.zeros_like(w3).at[:, :, X1].set(-w3[:, :, X2]).at[:, :, X2].set(w3[:, :, X1])
    return rot.reshape(w.shape)


def _swap_halves(g):
    return jnp.zeros_like(g).at[:, X1].set(g[:, X2]).at[:, X2].set(g[:, X1])


def _mla_weights(p):
    H = MLA_HEADS
    wq = jnp.pad(p['w_uq'].reshape(Q_LORA, H, QK_HEAD), ((0, 0), (0, 0), (0, HEAD_PAD - QK_HEAD)))
    wq = wq.reshape(Q_LORA, H * HEAD_PAD)
    wkv = p['w_ukv'].reshape(KV_LORA, H, QK_NOPE + V_HEAD)
    wk_lat = jnp.pad(wkv[:, :, :QK_NOPE], ((0, 0), (0, 0), (0, HEAD_PAD - QK_NOPE)))
    place = jnp.zeros((LANES, H, HEAD_PAD), F32).at[:QK_ROPE, :, QK_NOPE:QK_HEAD].set(
        jnp.broadcast_to(jnp.eye(QK_ROPE, dtype=F32)[:, None, :], (QK_ROPE, H, QK_ROPE)))
    wk = jnp.concatenate([wk_lat, place], axis=0).reshape(KV_LORA + LANES, H * HEAD_PAD)
    wv = wkv[:, :, QK_NOPE:].reshape(KV_LORA, H * V_HEAD)
    gq = jnp.pad(p['q_norm'], (0, HEAD_PAD - QK_HEAD)).reshape(1, HEAD_PAD)
    gk = jnp.pad(p['k_norm'], (0, HEAD_PAD - QK_HEAD)).reshape(1, HEAD_PAD)
    bf = lambda w: w.astype(BF16)
    return (bf(wq), bf(_rot_cols(wq)), bf(wk), bf(_rot_cols(wk)), bf(wv), gq, _swap_halves(gq), gk, _swap_halves(gk))


def _mla_tables(n_ctx, T):
    cos, sin = _rope_tables(T)
    c = jnp.ones((n_ctx + T, HEAD_PAD), F32).at[n_ctx:, X1].set(cos).at[n_ctx:, X2].set(cos)
    s = jnp.zeros((n_ctx + T, HEAD_PAD), F32).at[n_ctx:, X1].set(sin).at[n_ctx:, X2].set(sin)
    return c, s


def _mla_prep_kernel(x_ref, c_ref, s_ref, qlg_ref, kvg_ref, wq_ref, wqr_ref, wk_ref, wkr_ref, wv_ref,
                     gq_ref, gqp_ref, gk_ref, gkp_ref, q_ref, k_ref, v_ref):
    x = x_ref[0].astype(F32)
    ql = _rms(x[:, :Q_LORA], qlg_ref[...])
    kvl = _rms(x[:, Q_LORA:Q_LORA + KV_LORA], kvg_ref[...])
    k_in = jnp.concatenate([kvl, x[:, Q_LORA + KV_LORA:]], axis=1)
    cos, sin = c_ref[...], s_ref[...]

    def finish(raw, partner, g, g_swapped, scale, o_ref):
        gc, gs = g * cos, g_swapped * sin
        for h in range(MLA_HEADS):
            sl = slice(h * HEAD_PAD, (h + 1) * HEAD_PAD)
            rh = raw[:, sl]
            inv = lax.rsqrt(jnp.sum(rh * rh, axis=-1, keepdims=True) * (1.0 / QK_HEAD) + EPS) * scale
            o_ref[0, :, sl] = ((rh * gc + partner[:, sl] * gs) * inv).astype(o_ref.dtype)

    finish(_dot(ql, wq_ref[...]), _dot(ql, wqr_ref[...]), gq_ref[...], gqp_ref[...], ATTN_SCALE, q_ref)
    finish(_dot(k_in, wk_ref[...]), _dot(k_in, wkr_ref[...]), gk_ref[...], gkp_ref[...], 1.0, k_ref)
    v_ref[0] = _dot(kvl, wv_ref[...]).astype(v_ref.dtype)


def _mla_prep(proj, p, n_ctx, T):
    B, Tt, _ = proj.shape
    H = MLA_HEADS
    ctx_tiles = n_ctx // ROW_TILE
    mla_blk = (PROJ_RW + PROJ_GATE) // PROJ_MLA
    weights = _mla_weights(p)
    cos, sin = _mla_tables(n_ctx, T)
    const = lambda b, i: (0, 0)
    row = lambda b, i: (b, i, 0)
    tab = pl.BlockSpec((ROW_TILE, HEAD_PAD), lambda b, i: (i, 0))
    full = lambda a: pl.BlockSpec(a.shape, const)
    return pl.pallas_call(
        _mla_prep_kernel,
        grid=(B, Tt // ROW_TILE),
        in_specs=[pl.BlockSpec((1, ROW_TILE, PROJ_MLA), lambda b, i: (b, i, mla_blk)), tab, tab,
                  pl.BlockSpec((1, Q_LORA), const), pl.BlockSpec((1, KV_LORA), const)]
                 + [full(w) for w in weights],
        out_specs=[
            pl.BlockSpec((1, ROW_TILE, H * HEAD_PAD), lambda b, i: (b, jnp.maximum(i - ctx_tiles, 0), 0)),
            pl.BlockSpec((1, ROW_TILE, H * HEAD_PAD), row),
            pl.BlockSpec((1, ROW_TILE, H * V_HEAD), row),
        ],
        out_shape=[
            jax.ShapeDtypeStruct((B, T, H * HEAD_PAD), BF16),
            jax.ShapeDtypeStruct((B, Tt, H * HEAD_PAD), BF16),
            jax.ShapeDtypeStruct((B, Tt, H * V_HEAD), BF16),
        ],
        compiler_params=pltpu.CompilerParams(
            dimension_semantics=("arbitrary", "arbitrary"), vmem_limit_bytes=VMEM_LIMIT),
        name="mla_prep",
    )(proj, cos, sin, p['q_lat_norm'].reshape(1, Q_LORA), p['kv_lat_norm'].reshape(1, KV_LORA), *weights)


ATTN_Q_TILE = 1024
HEADS_PER_STEP = LANES // V_HEAD


ATTN_ROW_SPLIT = 8


def _attn_kernel(q_ref, k_ref, v_ref, o_ref):
    v2 = v_ref[0]
    rows = q_ref.shape[1] // ATTN_ROW_SPLIT
    work = [(hh, rs) for rs in range(ATTN_ROW_SPLIT) for hh in range(HEADS_PER_STEP)]

    def scores(hh, rs):
        sl = slice(hh * HEAD_PAD, (hh + 1) * HEAD_PAD)
        return _dot_t(q_ref[0, rs * rows:(rs + 1) * rows, sl], k_ref[0, :, sl])

    outs = {}
    s_next = scores(*work[0])
    for i, (hh, rs) in enumerate(work):
        s = s_next
        if i + 1 < len(work):
            s_next = scores(*work[i + 1])
        e = jnp.exp(s - jnp.max(s, axis=-1, keepdims=True))
        outs[hh, rs] = _dot(e, v2) / jnp.sum(e, axis=-1, keepdims=True)
    lane = lax.broadcasted_iota(jnp.int32, (rows, LANES), 1)
    for rs in range(ATTN_ROW_SPLIT):
        o_ref[0, rs * rows:(rs + 1) * rows] = jnp.where(lane < V_HEAD, outs[0, rs], outs[1, rs])


def _attention(q, k, v):
    B, T, _ = q.shape
    Kt = k.shape[1]
    assert T % ATTN_Q_TILE == 0 and ATTN_Q_TILE % (ATTN_ROW_SPLIT * SUBLANES) == 0
    hp = MLA_HEADS // HEADS_PER_STEP
    qw = HEADS_PER_STEP * HEAD_PAD
    return pl.pallas_call(
        _attn_kernel,
        grid=(B, hp, T // ATTN_Q_TILE),
        in_specs=[
            pl.BlockSpec((1, ATTN_Q_TILE, qw), lambda b, h, i: (b, i, h)),
            pl.BlockSpec((1, Kt, qw), lambda b, h, i: (b, 0, h)),
            pl.BlockSpec((1, Kt, LANES), lambda b, h, i: (b, 0, h)),
        ],
        out_specs=pl.BlockSpec((1, ATTN_Q_TILE, LANES), lambda b, h, i: (b, i, h)),
        out_shape=jax.ShapeDtypeStruct((B, T, MLA_HEADS * V_HEAD), F32),
        compiler_params=pltpu.CompilerParams(
            dimension_semantics=("arbitrary", "arbitrary", "arbitrary"), vmem_limit_bytes=VMEM_LIMIT),
        name="attention",
    )(q, k, v)


HALO = 16
LORA_W = LANES
GATE_W = PROJ_RW - 3 * RW_DIM - 2 * LORA_W


def _head_ones(width, head):
    i = np.arange(width) // head
    return jnp.asarray(i[:, None] == i[None, :], BF16)


def _head_sum(x, ones_bd):
    hi = x.astype(BF16)
    lo = (x - hi.astype(F32)).astype(BF16)
    return (jnp.dot(hi, ones_bd, preferred_element_type=F32)
            + jnp.dot(lo, ones_bd, preferred_element_type=F32))


def _dot3(a, b_hi, b_lo):
    hi = a.astype(BF16)
    lo = (a - hi.astype(F32)).astype(BF16)
    d = lambda u, w: jnp.dot(u, w, preferred_element_type=F32)
    return d(hi, b_hi) + (d(hi, b_lo) + d(lo, b_hi))


def _rwkv_prep_kernel(x_ref, prev_ref, next_ref, conv_ref, kkg_ref, ka_ref, rk_ref, w0_ref, a0_ref,
                      w2h_ref, w2l_ref, a2_ref, g2_ref, ones_ref,
                      r_ref, v_ref, kk_ref, lw_ref, bk_ref, kd_ref, bonus_ref, gg_ref, *, ctx_tiles, n_tiles):
    i = pl.program_id(1)
    x = x_ref[0].astype(F32)
    tm, W = x.shape
    C = RW_DIM
    first = (i == 0) | (i == ctx_tiles)
    last = (i == ctx_tiles - 1) | (i == n_tiles - 1)
    prev_row = jnp.where(first, 0.0, prev_ref[0, HALO - 1:HALO].astype(F32))
    next_row = jnp.where(last, 0.0, next_ref[0, 0:1].astype(F32))
    row = lax.broadcasted_iota(jnp.int32, (tm, W), 0)
    x_dn = jnp.where(row == 0, prev_row, pltpu.roll(x, 1, 0))
    x_up = jnp.where(row == tm - 1, next_row, pltpu.roll(x, tm - 1, 0))
    xc = x_dn * conv_ref[0:1] + x * conv_ref[1:2] + x_up * conv_ref[2:3]

    r, k, v = xc[:, :C], xc[:, C:2 * C], xc[:, 2 * C:3 * C]
    lora_w = jnp.tanh(xc[:, 3 * C:3 * C + LORA_W])
    lora_a = xc[:, 3 * C + LORA_W:3 * C + 2 * LORA_W]
    lg = xc[:, 3 * C + 2 * LORA_W:]
    ones_bd = ones_ref[...]
    kq = k * kkg_ref[...]
    kk = kq * lax.rsqrt(_head_sum(kq * kq, ones_bd) + 1e-12)
    r_ref[0], v_ref[0], kk_ref[0] = r, v, kk

    k_sum = jnp.zeros_like(k)
    for d in range(2):
        z = w0_ref[d:d + 1] + _dot3(lora_w, w2h_ref[d], w2l_ref[d])
        softplus_neg = jnp.maximum(-z, 0.0) + jnp.log(1.0 + jnp.exp(-jnp.abs(z)))
        lw_ref[d, 0] = -jnp.exp(-softplus_neg - 0.5)
        a = jax.nn.sigmoid(a0_ref[d:d + 1] + _dot(lora_a, a2_ref[d]))
        kd = k * (1.0 + (a - 1.0) * ka_ref[...])
        bk_ref[d, 0] = kk * a
        kd_ref[d, 0] = kd
        k_sum = k_sum + kd
    bonus_ref[0] = _head_sum(r * k_sum * rk_ref[...], ones_bd) * v
    gg_ref[0] = _dot(jax.nn.sigmoid(lg), g2_ref[...])


def _rwkv_prep(proj, p, n_ctx):
    B, Tt, _ = proj.shape
    C = RW_DIM
    nt = Tt // ROW_TILE
    hb = ROW_TILE // HALO
    pad_cols = lambda w: jnp.pad(w, ((0, 0), (0, PROJ_RW - w.shape[1])))
    w2 = jnp.stack([jnp.pad(p['decay_w2'][0], ((0, LORA_W - DECAY_LORA), (0, 0))),
                    jnp.pad(p['decay_w2'][1], ((DECAY_LORA, 0), (0, 0)))])
    w2h = w2.astype(BF16)
    w2l = (w2 - w2h.astype(F32)).astype(BF16)
    a2 = jnp.stack([jnp.pad(p['aicl_a2'][0], ((0, LORA_W - AICL_LORA), (0, 0))),
                    jnp.pad(p['aicl_a2'][1], ((AICL_LORA, 0), (0, 0)))]).astype(BF16)
    g2 = jnp.pad(p['gate_g2'], ((0, GATE_W - GATE_LORA), (0, 0))).astype(BF16)
    row = lambda b, i: (b, i, 0)
    drow = lambda b, i: (0, b, i, 0)
    const2 = lambda b, i: (0, 0)
    const3 = lambda b, i: (0, 0, 0)
    vec = pl.BlockSpec((1, C), const2)
    out_row = pl.BlockSpec((1, ROW_TILE, C), row)
    out_dir = pl.BlockSpec((2, 1, ROW_TILE, C), drow)
    sds = jax.ShapeDtypeStruct((B, Tt, C), F32)
    sds2 = jax.ShapeDtypeStruct((2, B, Tt, C), F32)
    ctx_tiles = n_ctx // ROW_TILE
    out_lat = pl.BlockSpec((1, ROW_TILE, C), lambda b, i: (b, jnp.maximum(i - ctx_tiles, 0), 0))
    sds_lat = jax.ShapeDtypeStruct((B, Tt - n_ctx, C), F32)
    return pl.pallas_call(
        functools.partial(_rwkv_prep_kernel, ctx_tiles=n_ctx // ROW_TILE, n_tiles=nt),
        grid=(B, nt),
        in_specs=[
            pl.BlockSpec((1, ROW_TILE, PROJ_RW), row),
            pl.BlockSpec((1, HALO, PROJ_RW), lambda b, i: (b, jnp.maximum(i * hb - 1, 0), 0)),
            pl.BlockSpec((1, HALO, PROJ_RW), lambda b, i: (b, jnp.minimum((i + 1) * hb, nt * hb - 1), 0)),
            pl.BlockSpec((3, PROJ_RW), const2),
            vec, vec, vec,
            pl.BlockSpec((2, C), const2),
            pl.BlockSpec((2, C), const2),
            pl.BlockSpec((2, LORA_W, C), const3),
            pl.BlockSpec((2, LORA_W, C), const3),
            pl.BlockSpec((2, LORA_W, C), const3),
            pl.BlockSpec((GATE_W, C), const2),
            pl.BlockSpec((C, C), const2),
        ],
        out_specs=[out_row, out_row, out_row, out_dir, out_dir, out_dir, out_lat, out_lat],
        out_shape=[sds, sds, sds, sds2, sds2, sds2, sds_lat, sds_lat],
        compiler_params=pltpu.CompilerParams(
            dimension_semantics=("arbitrary", "arbitrary"), vmem_limit_bytes=VMEM_LIMIT),
        name="rwkv_prep",
    )(proj, proj, proj, pad_cols(p['shift_conv']), p['k_k'].reshape(1, C), p['k_a'].reshape(1, C),
      p['r_k'].reshape(1, C), p['decay_w0'], p['aicl_a0'], w2h, w2l, a2, g2, _head_ones(C, RW_HEAD))


MERGE_PIECES = 2


def _merge_kernel(att_ref, yf_ref, yb_ref, bonus_ref, gg_ref, *refs):
    gate_refs, (x_ref, ga_ref, gnw_ref, gnb_ref, ones_ref, wm_ref, wr_ref, wo_ref, o_ref) = (
        refs[:MERGE_PIECES], refs[MERGE_PIECES:])
    D = x_ref.shape[-1]
    ones_bd = ones_ref[...]
    y = yf_ref[0] + yb_ref[0]
    yc = y - _head_sum(y, ones_bd) * (1.0 / RW_HEAD)
    var = _head_sum(yc * yc, ones_bd) * (1.0 / RW_HEAD)
    y_n = yc * lax.rsqrt(var + GN_EPS) * gnw_ref[...] + gnb_ref[...]
    rw = (y_n + bonus_ref[0]) * gg_ref[0]
    g = jax.nn.sigmoid(jnp.concatenate([r[0] for r in gate_refs], axis=0).astype(F32))
    mix = g[:, :D] * _dot(att_ref[0], wm_ref[...]) + g[:, D:] * _dot(rw, wr_ref[...])
    o_ref[0] = x_ref[0] + ga_ref[0] * _dot(mix, wo_ref[...])


def _merge(att, y_f, y_b, bonus, gg, proj, x, g_a, p, n_ctx):
    B, T, D = x.shape
    C = RW_DIM
    rows = MERGE_PIECES * ROW_TILE
    ctx_tiles = n_ctx // ROW_TILE
    gate_blk = PROJ_RW // PROJ_GATE
    row = lambda b, i: (b, i, 0)
    const = lambda b, i: (0, 0)
    gate_piece = lambda n: pl.BlockSpec(
        (1, ROW_TILE, PROJ_GATE), lambda b, i, n=n: (b, MERGE_PIECES * i + n + ctx_tiles, gate_blk))
    return pl.pallas_call(
        _merge_kernel,
        grid=(B, T // rows),
        in_specs=[
            pl.BlockSpec((1, rows, att.shape[-1]), row),
            pl.BlockSpec((1, rows, C), row),
            pl.BlockSpec((1, rows, C), row),
            pl.BlockSpec((1, rows, C), row),
            pl.BlockSpec((1, rows, C), row),
        ] + [gate_piece(n) for n in range(MERGE_PIECES)] + [
            pl.BlockSpec((1, rows, D), row),
            pl.BlockSpec((1, 1, D), lambda b, i: (b, 0, 0)),
            pl.BlockSpec((1, C), const),
            pl.BlockSpec((1, C), const),
            pl.BlockSpec((C, C), const),
            pl.BlockSpec(p['w_o_mla'].shape, const),
            pl.BlockSpec(p['w_o_rwkv'].shape, const),
            pl.BlockSpec(p['w_out'].shape, const),
        ],
        out_specs=pl.BlockSpec((1, rows, D), row),
        out_shape=jax.ShapeDtypeStruct((B, T, D), F32),
        compiler_params=pltpu.CompilerParams(
            dimension_semantics=("arbitrary", "arbitrary"), vmem_limit_bytes=VMEM_LIMIT),
        name="merge",
    )(att, y_f, y_b, bonus, gg, *([proj] * MERGE_PIECES), x, g_a[:, None, :], p['gn_w'].reshape(1, C),
      p['gn_b'].reshape(1, C), _head_ones(C, RW_HEAD), p['w_o_mla'].astype(BF16), p['w_o_rwkv'].astype(BF16),
      p['w_out'].astype(BF16))


PACK_CHUNKS = 4
TILE_ROWS = 8
MOE_BLOCK = 512


ROUTER_TILE = 1024
NEG_INF = float("-inf")
HI_MASK = 0xFFFF0000


def _pack_pair(lo, hi):
    lo_b = lax.bitcast_convert_type(lo.astype(BF16).astype(F32), jnp.uint32)
    hi_b = lax.bitcast_convert_type(hi.astype(BF16).astype(F32), jnp.uint32)
    return (lo_b >> 16) | (hi_b & jnp.uint32(HI_MASK))


def _unpack_pair(w):
    lo = lax.bitcast_convert_type(w << 16, F32)
    hi = lax.bitcast_convert_type(w & jnp.uint32(HI_MASK), F32)
    return lo.astype(BF16), hi.astype(BF16)


def _row_max(x):
    return jnp.max(x, axis=-1, keepdims=True)


def _first_index_of(x, value, lane_f):
    return jnp.min(jnp.where(x == value, lane_f, float(x.shape[-1])), axis=-1, keepdims=True)


def _router_kernel(x_ref, gain_ref, sh_ref, sc_ref, wh_ref, wm_ref, wl_ref, bias_ref,
                   u_ref, ut_ref, ids_ref, gates_ref, ranks_ref, counts_ref, carry_ref):
    @pl.when((pl.program_id(0) == 0) & (pl.program_id(1) == 0))
    def _():
        carry_ref[...] = jnp.zeros_like(carry_ref)

    x = x_ref[0]
    tm, D = x.shape
    E = bias_ref.shape[-1]
    u = x * lax.rsqrt(jnp.mean(x * x, axis=-1, keepdims=True) + EPS) * gain_ref[...]
    u = u * (1.0 + sc_ref[0]) + sh_ref[0]
    packed = _pack_pair(u[:, :D // 2], u[:, D // 2:])
    u_ref[0] = packed
    for c in range(PACK_CHUNKS):
        ut_ref[:, c, :] = packed[:, c * LANES:(c + 1) * LANES]

    uh, um, ul = _split3(u)
    wh, wm, wl = wh_ref[...], wm_ref[...], wl_ref[...]
    d = lambda a, b: jnp.dot(a, b, preferred_element_type=F32)
    logits = d(uh, wh) + (d(uh, wm) + d(um, wh)) + (d(uh, wl) + d(um, wm) + d(ul, wh))
    scores = jax.nn.sigmoid(logits)
    sel = scores + bias_ref[...]

    lane_i = lax.broadcasted_iota(jnp.int32, (tm, E), 1)
    lane_f = lane_i.astype(F32)
    out_f = lax.broadcasted_iota(jnp.int32, (tm, LANES), 1).astype(F32)
    per_group = E // N_GROUPS
    grp_f = jnp.floor(lane_f * (1.0 / per_group))

    gs = jnp.full((tm, LANES), NEG_INF, F32)
    for g in range(N_GROUPS):
        sg = jnp.where(lane_i >= g * per_group, jnp.where(lane_i < (g + 1) * per_group, sel, NEG_INF), NEG_INF)
        m1 = _row_max(sg)
        i1 = _first_index_of(sg, m1, lane_f)
        m2 = _row_max(jnp.where(lane_f == i1, NEG_INF, sg))
        gs = jnp.where(out_f == g, m1 + m2, gs)

    allow = jnp.zeros((tm, E), F32)
    for _ in range(TOPK_GROUPS):
        m = _row_max(gs)
        i = _first_index_of(gs, m, out_f)
        gs = jnp.where(out_f == i, NEG_INF, gs)
        allow = jnp.where(grp_f == i, 1.0, allow)
    selm = jnp.where(allow > 0.0, sel, NEG_INF)

    ids = jnp.zeros((tm, LANES), F32)
    gts = jnp.zeros((tm, LANES), F32)
    member = jnp.zeros((tm, E), F32)
    idx_cols = []
    gsum = jnp.zeros((tm, 1), F32)
    for k in range(TOP_K):
        m = _row_max(selm)
        i = _first_index_of(selm, m, lane_f)
        hit = lane_f == i
        gk = jnp.sum(jnp.where(hit, scores, 0.0), axis=-1, keepdims=True)
        selm = jnp.where(hit, NEG_INF, selm)
        member = jnp.where(hit, 1.0, member)
        ids = jnp.where(out_f == k, i, ids)
        gts = jnp.where(out_f == k, gk, gts)
        idx_cols.append(i)
        gsum = gsum + gk
    gts = gts / gsum * ROUTED_SCALE

    r2 = lax.broadcasted_iota(jnp.int32, (tm, tm), 0)
    c2 = lax.broadcasted_iota(jnp.int32, (tm, tm), 1)
    before = jnp.where(r2 > c2, 1.0, 0.0).astype(BF16)
    mem_b = member.astype(BF16)
    carry = carry_ref[...]
    pos = carry + jnp.dot(before, mem_b, preferred_element_type=F32)
    rk = jnp.zeros((tm, LANES), F32)
    for k in range(TOP_K):
        rk = jnp.where(out_f == k, jnp.sum(jnp.where(lane_f == idx_cols[k], pos, 0.0), axis=-1, keepdims=True), rk)
    colsum = jnp.dot(jnp.ones((8, tm), BF16), mem_b, preferred_element_type=F32)[0:1]
    carry_ref[...] = carry + colsum
    counts_ref[...] = carry + colsum
    ids_ref[0] = ids.astype(jnp.int32)
    ranks_ref[0] = rk.astype(jnp.int32)
    gates_ref[0] = gts


def _router(x1, sh_m, sc_m, p):
    B, T, D = x1.shape
    E = N_EXPERTS
    wh, wm, wl = _split3(p['router_w'])
    row = lambda b, i: (b, i, 0)
    const = lambda b, i: (0, 0)
    vec = lambda b, i: (b, 0, 0)
    lane_out = lambda dt: jax.ShapeDtypeStruct((B, T, LANES), dt)
    return pl.pallas_call(
        _router_kernel,
        grid=(B, T // ROUTER_TILE),
        in_specs=[
            pl.BlockSpec((1, ROUTER_TILE, D), row),
            pl.BlockSpec((1, D), const),
            pl.BlockSpec((1, 1, D), vec),
            pl.BlockSpec((1, 1, D), vec),
            pl.BlockSpec((D, E), const),
            pl.BlockSpec((D, E), const),
            pl.BlockSpec((D, E), const),
            pl.BlockSpec((1, E), const),
        ],
        out_specs=[
            pl.BlockSpec((1, ROUTER_TILE, D // 2), row),
            pl.BlockSpec((ROUTER_TILE, PACK_CHUNKS, LANES), lambda b, i: (b * (T // ROUTER_TILE) + i, 0, 0)),
            pl.BlockSpec((1, ROUTER_TILE, LANES), row),
            pl.BlockSpec((1, ROUTER_TILE, LANES), row),
            pl.BlockSpec((1, ROUTER_TILE, LANES), row),
            pl.BlockSpec((1, E), const),
        ],
        out_shape=[
            jax.ShapeDtypeStruct((B, T, D // 2), jnp.uint32),
            jax.ShapeDtypeStruct((B * T, PACK_CHUNKS, LANES), jnp.uint32),
            lane_out(jnp.int32), lane_out(F32), lane_out(jnp.int32),
            jax.ShapeDtypeStruct((1, E), F32),
        ],
        scratch_shapes=[pltpu.VMEM((1, E), F32)],
        compiler_params=pltpu.CompilerParams(
            dimension_semantics=("arbitrary", "arbitrary"), vmem_limit_bytes=VMEM_LIMIT),
        name="router",
    )(x1, p['norm_ffn'].reshape(1, D), sh_m[:, None, :], sc_m[:, None, :], wh, wm, wl,
      p['router_bias'].reshape(1, E))


def _slot_kernel(ids_ref, ranks_ref, base_ref, o_ref):
    ids = ids_ref[...].astype(F32)
    tm = ids.shape[0]
    E = base_ref.shape[-1]
    lane_e = lax.broadcasted_iota(jnp.int32, (tm, E), 1).astype(F32)
    out_lane = lax.broadcasted_iota(jnp.int32, (tm, LANES), 1)
    first = jnp.zeros((tm, LANES), F32)
    for k in range(TOP_K):
        fk = jnp.sum(jnp.where(lane_e == ids[:, k:k + 1], base_ref[...], 0.0), axis=-1, keepdims=True)
        first = jnp.where(out_lane == k, fk, first)
    o_ref[...] = first.astype(jnp.int32) + ranks_ref[...]


def _slots(ids, ranks, base):
    n_tok = ids.shape[0]
    E = base.shape[-1]
    row = pl.BlockSpec((ROUTER_TILE, LANES), lambda i: (i, 0))
    return pl.pallas_call(
        _slot_kernel,
        grid=(n_tok // ROUTER_TILE,),
        in_specs=[row, row, pl.BlockSpec((1, E), lambda i: (0, 0))],
        out_specs=row,
        out_shape=jax.ShapeDtypeStruct((n_tok, LANES), jnp.int32),
        compiler_params=pltpu.CompilerParams(dimension_semantics=("arbitrary",)),
        name="slots",
    )(ids, ranks, base)


DMA_PRIORITIES = 2


def _row_copy(src, src_row, dst, dst_row, sem):
    return pltpu.make_async_copy(src.at[pl.ds(src_row, 1)], dst.at[pl.ds(dst_row, 1)], sem)


def _dispatch_kernel(pad_ref, dest_ref, u_ref, slots_ref, zero_ref, sem):
    tm = u_ref.shape[0]
    n_experts = pad_ref.shape[0]

    @pl.when(pl.program_id(0) == 0)
    def _():
        zero_ref[...] = jnp.zeros_like(zero_ref)

        def fill(e, carry):
            pltpu.make_async_copy(zero_ref, slots_ref.at[pl.ds(pad_ref[e], MOE_BLOCK)], sem).start()
            return carry

        lax.fori_loop(0, n_experts, fill, 0)

        def filled(e, carry):
            pltpu.make_async_copy(zero_ref, slots_ref.at[pl.ds(0, MOE_BLOCK)], sem).wait()
            return carry

        lax.fori_loop(0, n_experts, filled, 0)

    def issue(r, carry):
        for k in range(TOP_K):
            pltpu.make_async_copy(u_ref.at[r], slots_ref.at[dest_ref[0, 0, r * TOP_K + k]], sem).start(
                priority=k % DMA_PRIORITIES)
        return carry

    lax.fori_loop(0, tm, issue, 0)
    for k in range(TOP_K):
        pltpu.make_async_copy(u_ref, slots_ref.at[pl.ds(0, tm)], sem).wait()


def _dispatch(pad_start, dest, u_tiles, n_slots):
    n_tok = u_tiles.shape[0]
    tile = u_tiles.shape[1:]
    nt = n_tok // ROW_TILE
    grid_spec = pltpu.PrefetchScalarGridSpec(
        num_scalar_prefetch=1,
        grid=(nt,),
        in_specs=[
            pl.BlockSpec((1, 1, ROW_TILE * TOP_K), lambda i, pad: (i, 0, 0), memory_space=pltpu.SMEM),
            pl.BlockSpec((ROW_TILE,) + tile, lambda i, pad: (i, 0, 0)),
        ],
        out_specs=pl.BlockSpec(memory_space=pl.ANY),
        scratch_shapes=[pltpu.VMEM((MOE_BLOCK,) + tile, jnp.uint32), pltpu.SemaphoreType.DMA(())],
    )
    return pl.pallas_call(
        _dispatch_kernel,
        grid_spec=grid_spec,
        out_shape=jax.ShapeDtypeStruct((n_slots + MOE_BLOCK,) + tile, jnp.uint32),
        compiler_params=pltpu.CompilerParams(
            dimension_semantics=("arbitrary",), vmem_limit_bytes=VMEM_LIMIT),
        name="dispatch",
    )(pad_start, dest, u_tiles)


WEIGHT_SLOTS = 2


def _expert_weight_copies(hbm_refs, stage_refs, expert, slot, sems):
    return [pltpu.make_async_copy(w.at[expert], s.at[slot], sems.at[slot, i])
            for i, (w, s) in enumerate(zip(hbm_refs, stage_refs))]


def _moe_ffn_kernel(be_ref, first_ref, slot_ref, next_ref, nu_ref, x_ref, w1_ref, w3_ref, w2_ref, o_ref,
                    s1_ref, s3_ref, s2_ref, w13_ref, w2b_ref, sems):
    j = pl.program_id(0)
    F = w1_ref.shape[-1]
    half = PACK_CHUNKS * LANES
    copies = functools.partial(_expert_weight_copies, (w1_ref, w3_ref, w2_ref), (s1_ref, s3_ref, s2_ref), sems=sems)

    @pl.when(j == 0)
    def _():
        for c in copies(be_ref[0], 0):
            c.start()

    @pl.when(first_ref[j] == 1)
    def _():
        slot = slot_ref[j]
        for c in copies(be_ref[j], slot):
            c.wait()

        @pl.when(next_ref[j] >= 0)
        def _():
            for c in copies(next_ref[j], 1 - slot):
                c.start()

        w13_ref[:, :F] = s1_ref[slot].astype(BF16)
        w13_ref[:, F:] = s3_ref[slot].astype(BF16)
        w2b_ref[...] = s2_ref[slot].astype(BF16)

    @pl.when(j < nu_ref[0])
    def _():
        lo, hi = _unpack_pair(jnp.concatenate([x_ref[:, c, :] for c in range(PACK_CHUNKS)], axis=1))
        h = (jnp.dot(lo, w13_ref[:half], preferred_element_type=F32)
             + jnp.dot(hi, w13_ref[half:], preferred_element_type=F32))
        h1, h3 = h[:, :F], h[:, F:]
        y = _dot(h1 * jax.nn.sigmoid(h1) * h3, w2b_ref[...])
        o_ref[...] = pltpu.einshape("s(cl)->(sc)l", y, c=TILE_ROWS)

    @pl.when(j >= nu_ref[0])
    def _():
        o_ref[...] = jnp.zeros_like(o_ref)


def _moe_ffn(block_e, n_used, x_slots, w1, w3, w2):
    E, D, F = w1.shape
    nblk = block_e.shape[0]
    slots = nblk * MOE_BLOCK
    idx = jnp.arange(nblk)
    first = jnp.concatenate([jnp.ones((1,), bool), block_e[1:] != block_e[:-1]])
    slot = ((jnp.cumsum(first) - 1) % WEIGHT_SLOTS).astype(jnp.int32)
    later_first = lax.cummin(jnp.where(first, idx, nblk)[::-1])[::-1]
    next_first = jnp.concatenate([later_first[1:], jnp.full((1,), nblk)])
    next_e = jnp.where(next_first < nblk, block_e[jnp.minimum(next_first, nblk - 1)], -1).astype(jnp.int32)
    n_prefetch = 5
    blk = lambda j, *_: (jnp.minimum(j, _[4][0] - 1), 0, 0)
    grid_spec = pltpu.PrefetchScalarGridSpec(
        num_scalar_prefetch=n_prefetch,
        grid=(nblk,),
        in_specs=[
            pl.BlockSpec((MOE_BLOCK, PACK_CHUNKS, LANES), blk),
            pl.BlockSpec(memory_space=pl.ANY),
            pl.BlockSpec(memory_space=pl.ANY),
            pl.BlockSpec(memory_space=pl.ANY),
        ],
        out_specs=pl.BlockSpec((MOE_BLOCK * TILE_ROWS, LANES), lambda j, *_: (j, 0)),
        scratch_shapes=[pltpu.VMEM((WEIGHT_SLOTS, D, F), F32), pltpu.VMEM((WEIGHT_SLOTS, D, F), F32),
                        pltpu.VMEM((WEIGHT_SLOTS, F, D), F32),
                        pltpu.VMEM((D, 2 * F), BF16), pltpu.VMEM((F, D), BF16),
                        pltpu.SemaphoreType.DMA((WEIGHT_SLOTS, 3))],
    )
    return pl.pallas_call(
        _moe_ffn_kernel,
        grid_spec=grid_spec,
        out_shape=jax.ShapeDtypeStruct((slots * TILE_ROWS, LANES), F32),
        compiler_params=pltpu.CompilerParams(
            dimension_semantics=("arbitrary",), vmem_limit_bytes=VMEM_LIMIT),
        name="moe_ffn",
    )(block_e, first.astype(jnp.int32), slot, next_e, n_used, x_slots, w1, w3, w2)


COMBINE_ROWS = 32


def _combine_kernel(dest_ref, gates_ref, u_ref, x_ref, gm_ref, w1_ref, w3_ref, w2_ref, ys_ref, o_ref, buf_ref, sem):
    tm, half = u_ref.shape

    def tile(i):
        return pl.ds(pl.multiple_of(i * TILE_ROWS, TILE_ROWS), TILE_ROWS)

    def issue(r, carry):
        for k in range(TOP_K):
            pltpu.make_async_copy(ys_ref.at[tile(dest_ref[0, 0, r * TOP_K + k])], buf_ref.at[k, tile(r)], sem).start(
                priority=k % DMA_PRIORITIES)
        return carry

    lax.fori_loop(0, tm, issue, 0)

    ulo, uhi = _unpack_pair(u_ref[...])
    both = lambda w_ref: (jnp.dot(ulo, w_ref[:half], preferred_element_type=F32)
                          + jnp.dot(uhi, w_ref[half:], preferred_element_type=F32))
    h1, h3 = both(w1_ref), both(w3_ref)
    o_ref[...] = x_ref[...] + gm_ref[0] * _dot(h1 * jax.nn.sigmoid(h1) * h3, w2_ref[...])

    for k in range(TOP_K):
        pltpu.make_async_copy(ys_ref.at[pl.ds(0, tm * TILE_ROWS)], buf_ref.at[k], sem).wait()

    gm = gm_ref[0]
    for rb in range(tm // COMBINE_ROWS):
        rows = slice(rb * COMBINE_ROWS, (rb + 1) * COMBINE_ROWS)
        gates = gates_ref[rows, :]
        for c in range(TILE_ROWS):
            acc = jnp.zeros((COMBINE_ROWS, LANES), F32)
            for k in range(TOP_K):
                acc = acc + gates[:, k:k + 1] * buf_ref[
                    k, pl.ds(rb * COMBINE_ROWS * TILE_ROWS + c, COMBINE_ROWS, stride=TILE_ROWS), :]
            cols = slice(c * LANES, (c + 1) * LANES)
            o_ref[rows, cols] = o_ref[rows, cols] + gm[:, cols] * acc


def _combine(dest, gates, u_rows, x1, g_m, y_slots, p):
    B, T, D = x1.shape
    n_tok = B * T
    W = u_rows.shape[1]
    tiles_per_batch = T // ROW_TILE
    row = lambda i: (i, 0)
    const = lambda i: (0, 0)
    out = pl.pallas_call(
        _combine_kernel,
        grid=(n_tok // ROW_TILE,),
        in_specs=[
            pl.BlockSpec((1, 1, ROW_TILE * TOP_K), lambda i: (i, 0, 0), memory_space=pltpu.SMEM),
            pl.BlockSpec((ROW_TILE, LANES), row),
            pl.BlockSpec((ROW_TILE, W), row),
            pl.BlockSpec((ROW_TILE, D), row),
            pl.BlockSpec((1, 1, D), lambda i: (i // tiles_per_batch, 0, 0)),
            pl.BlockSpec(p['shared_w1'].shape, const),
            pl.BlockSpec(p['shared_w3'].shape, const),
            pl.BlockSpec(p['shared_w2'].shape, const),
            pl.BlockSpec(memory_space=pl.ANY),
        ],
        out_specs=pl.BlockSpec((ROW_TILE, D), row),
        out_shape=jax.ShapeDtypeStruct((n_tok, D), F32),
        scratch_shapes=[pltpu.VMEM((TOP_K, ROW_TILE * TILE_ROWS, LANES), F32), pltpu.SemaphoreType.DMA(())],
        compiler_params=pltpu.CompilerParams(
            dimension_semantics=("arbitrary",), vmem_limit_bytes=VMEM_LIMIT),
        name="combine",
    )(dest, gates.reshape(n_tok, LANES), u_rows, x1.reshape(n_tok, D), g_m[:, None, :],
      p['shared_w1'].astype(BF16), p['shared_w3'].astype(BF16), p['shared_w2'].astype(BF16), y_slots)
    return out.reshape(B, T, D)


def _moe(x1, sh_m, sc_m, g_m, p):
    B, T, D = x1.shape
    n_tok = B * T
    u_packed, u_tiles, ids, gates, ranks, counts = _router(x1, sh_m, sc_m, p)

    counts = counts[0].astype(jnp.int32)
    padded = (counts + MOE_BLOCK - 1) // MOE_BLOCK * MOE_BLOCK
    padded_end = jnp.cumsum(padded)
    base = padded_end - padded
    n_blocks = n_tok * TOP_K // MOE_BLOCK + N_EXPERTS
    n_used = (padded_end[-1] // MOE_BLOCK).astype(jnp.int32)
    blk = jnp.minimum(jnp.arange(n_blocks), n_used - 1) * MOE_BLOCK
    block_e = jnp.minimum(jnp.sum(padded_end[None, :] <= blk[:, None], axis=1), N_EXPERTS - 1).astype(jnp.int32)
    dest = _slots(ids.reshape(n_tok, LANES), ranks.reshape(n_tok, LANES), base.astype(F32).reshape(1, N_EXPERTS))
    dest = dest[:, :TOP_K].reshape(n_tok // ROW_TILE, 1, ROW_TILE * TOP_K)

    u_rows = u_packed.reshape(n_tok, D // 2)
    x_slots = _dispatch((base + counts).astype(jnp.int32), dest, u_tiles, n_blocks * MOE_BLOCK)
    y_slots = _moe_ffn(block_e, n_used.reshape(1), x_slots, p['expert_w1'], p['expert_w3'], p['expert_w2'])
    return _combine(dest, gates, u_rows, x1, g_m, y_slots, p)
```

```python
import functools

import jax
import jax.numpy as jnp
import numpy as np
from jax import lax
from jax.experimental import pallas as pl
from jax.experimental.pallas import tpu as pltpu

F32 = jnp.float32
BF16 = jnp.bfloat16

GRID_W = 64
EPS = 1e-6
MLA_HEADS = 8
QK_NOPE = 64
QK_ROPE = 32
QK_HEAD = QK_NOPE + QK_ROPE
V_HEAD = 64
Q_LORA = 256
KV_LORA = 128
ROPE_BASE = 10000.0
ATTN_SCALE = QK_HEAD ** -0.5
RW_HEADS = 8
RW_HEAD = 64
RW_DIM = RW_HEADS * RW_HEAD
DECAY_LORA = 64
AICL_LORA = 64
GATE_LORA = 160
GN_EPS = 64e-5
N_EXPERTS = 256
TOP_K = 8
N_GROUPS = 8
TOPK_GROUPS = 4
ROUTED_SCALE = 2.5

LANES = 128
MXU_DIM = 256
VMEM_LIMIT = 56 * 1024 * 1024

WKV_CHUNK = 64
WKV_GROUP = 128
HEADS_PER_GROUP = WKV_GROUP // RW_HEAD
HEAD_PAD = LANES


def _dot(a, b):
    return jnp.dot(a.astype(BF16), b.astype(BF16), preferred_element_type=F32)


def _dot_t(a, b):
    return lax.dot_general(a.astype(BF16), b.astype(BF16), (((1,), (1,)), ((), ())),
                           preferred_element_type=F32)


def _split3(x):
    h = x.astype(BF16)
    r1 = x - h.astype(F32)
    m = r1.astype(BF16)
    lo = (r1 - m.astype(F32)).astype(BF16)
    return h, m, lo


def _dot_hi(a_bf16_exact, x):
    h, m, lo = _split3(x)
    d = lambda y: jnp.dot(a_bf16_exact, y, preferred_element_type=F32)
    return d(h) + d(m) + d(lo)


def _wkv_chunk(r, v, kk, lw, bk, kd, s0, reverse, emit):
    L, G = r.shape
    row = lax.broadcasted_iota(jnp.int32, (L, G), 0)
    lane = lax.broadcasted_iota(jnp.int32, (L, G), 1)
    diff = (lane % L - row) if reverse else (row - lane % L)
    strict = diff > 0
    incl = diff >= 0
    r2 = lax.broadcasted_iota(jnp.int32, (L, L), 0)
    c2 = lax.broadcasted_iota(jnp.int32, (L, L), 1)
    tri = jnp.where(((c2 - r2) if reverse else (r2 - c2)) >= 0, 1.0, 0.0).astype(BF16)
    tri_ones = jnp.concatenate([tri, jnp.ones((L, L), BF16)], axis=0)
    lane_head = lane // RW_HEAD
    rowb = lax.broadcasted_iota(jnp.int32, (G, G), 0) // RW_HEAD
    colb = lax.broadcasted_iota(jnp.int32, (G, G), 1) // RW_HEAD

    def stack(x):
        return jnp.concatenate(
            [jnp.where(lane_head == h, x, 0.0) for h in range(HEADS_PER_GROUP)], axis=0).astype(BF16)

    sums = _dot_hi(tri_ones, lw)
    cum_in = sums[:L]
    g_in = jnp.exp(cum_in)
    g_inv = jnp.exp(-cum_in)
    g_ex = jnp.exp(cum_in - lw)
    g_tot = jnp.exp(sums[L:])
    a_h = -kk * g_ex
    b_h = bk * g_inv
    k_h = kd * g_inv
    r_h = r * g_in
    yield

    gram = _dot_t(jnp.concatenate([a_h, r_h], axis=0),
                  jnp.concatenate([stack(b_h), stack(k_h)], axis=0))
    ab = jnp.where(strict, gram[:L, :G], 0.0)
    ak = jnp.where(strict, gram[:L, G:], 0.0)
    rb = jnp.where(incl, gram[L:, :G], 0.0)
    rk = jnp.where(incl, gram[L:, G:], 0.0)
    yield

    tm = jnp.where(diff == 0, 1.0, 0.0) + ab
    p = ab
    v_st = stack(v)
    akv = _dot(ak, v_st)
    for _ in range(int(np.log2(L)) - 1):
        p = _dot(p, stack(p))
        yield
        tm = tm + _dot(tm, stack(p))
        yield

    wu = _dot(tm, jnp.concatenate([stack(a_h), stack(akv)], axis=1))
    w_t, u_t = wu[:, :G], wu[:, G:]
    yield
    rbwu = _dot(rb, jnp.concatenate([stack(w_t), stack(u_t)], axis=1))
    r_t = r_h + rbwu[:, :G]
    y_t = rbwu[:, G:] + _dot(rk, v_st)
    yield

    s0_st = stack(s0)
    y = _dot_t(r_t, s0_st) + y_t
    u = _dot_t(w_t, s0_st) + u_t
    yield
    uv = jnp.concatenate([u, v], axis=0)
    bkc = jnp.concatenate([b_h, k_h], axis=0)
    upd = lax.dot_general(uv.astype(BF16), bkc.astype(BF16), (((0,), (0,)), ((), ())),
                          preferred_element_type=F32)
    upd = jnp.where(rowb == colb, upd, 0.0)
    upd_d = upd[0:L]
    for h in range(1, HEADS_PER_GROUP):
        upd_d = upd_d + upd[h * L:(h + 1) * L]
    emit(y, (s0 + upd_d) * g_tot)


def _wkv_kernel(*refs, n_groups, n_batch):
    ins, (yf_ref, yb_ref, s_ref) = refs[:12], refs[12:]
    G = WKV_GROUP

    @pl.when(pl.program_id(1) == 0)
    def _():
        s_ref[...] = jnp.zeros_like(s_ref)

    chains = []
    for d, y_ref in enumerate((yf_ref, yb_ref)):
        r_ref, v_ref, kk_ref, lw_ref, bk_ref, kd_ref = ins[6 * d:6 * d + 6]
        for bi in range(n_batch):
            for g in range(n_groups):
                sl = slice(g * G, (g + 1) * G)

                def emit(y, s_new, y_ref=y_ref, d=d, bi=bi, g=g, sl=sl):
                    y_ref[bi, :, sl] = y
                    s_ref[d, bi, g] = s_new

                chains.append(_wkv_chunk(
                    r_ref[bi, :, sl], v_ref[bi, :, sl], kk_ref[bi, :, sl], lw_ref[0, bi, :, sl],
                    bk_ref[0, bi, :, sl], kd_ref[0, bi, :, sl], s_ref[d, bi, g], d == 1, emit))
    while chains:
        chains = [c for c in chains if next(c, StopIteration) is not StopIteration]


WKV_BATCH = 4


def _wkv_scan(r, v, kk, lw, bk, kd, n_ctx):
    B, Ttot, C = r.shape
    L = WKV_CHUNK
    nc = Ttot // L
    ncc = n_ctx // L
    nl = nc - ncc
    n_groups = C // WKV_GROUP
    nb = WKV_BATCH

    cid = (lambda s: s, lambda s: jnp.where(s < ncc, ncc - 1 - s, nc + ncc - 1 - s))
    first_out = (0, nl - 1)
    in_specs, out_specs = [], []
    for d in range(2):
        shared = pl.BlockSpec((nb, L, C), lambda b, s, d=d: (b, cid[d](s), 0))
        per_dir = pl.BlockSpec((1, nb, L, C), lambda b, s, d=d: (d, b, cid[d](s), 0))
        in_specs += [shared, shared, shared, per_dir, per_dir, per_dir]
        out_specs.append(pl.BlockSpec(
            (nb, L, C), lambda b, s, d=d: (b, jnp.where(s < ncc, first_out[d], cid[d](s) - ncc), 0)))
    out = jax.ShapeDtypeStruct((B, nl * L, C), F32)
    return pl.pallas_call(
        functools.partial(_wkv_kernel, n_groups=n_groups, n_batch=nb),
        grid=(B // nb, nc),
        in_specs=in_specs,
        out_specs=out_specs,
        out_shape=[out, out],
        scratch_shapes=[pltpu.VMEM((2, nb, n_groups, L, WKV_GROUP), F32)],
        compiler_params=pltpu.CompilerParams(
            dimension_semantics=("arbitrary", "arbitrary"), vmem_limit_bytes=VMEM_LIMIT),
        name="wkv_scan",
    )(r, v, kk, lw, bk, kd, r, v, kk, lw, bk, kd)


def kernel(x, c, ctx, c_ctx, ada_w, ada_b, norm_mix, norm_ffn, w_in, shift_conv, q_lat_norm, w_uq, kv_lat_norm, w_ukv, q_norm, k_norm, w_o_mla, decay_w0, decay_w2, aicl_a0, aicl_a2, k_k, k_a, r_k, gn_w, gn_b, gate_g2, w_o_rwkv, w_out, router_w, router_bias, expert_w1, expert_w3, expert_w2, shared_w1, shared_w3, shared_w2):
    B, T, D = x.shape
    n_ctx = ctx.shape[1]
    i0 = 0
    p = dict(norm_mix=norm_mix[i0], norm_ffn=norm_ffn[i0], w_in=w_in[i0], shift_conv=shift_conv[i0],
             q_lat_norm=q_lat_norm[i0], w_uq=w_uq[i0], kv_lat_norm=kv_lat_norm[i0], w_ukv=w_ukv[i0],
             q_norm=q_norm[i0], k_norm=k_norm[i0], w_o_mla=w_o_mla[i0],
             decay_w0=decay_w0[i0], decay_w2=decay_w2[i0], aicl_a0=aicl_a0[i0], aicl_a2=aicl_a2[i0],
             k_k=k_k[i0], k_a=k_a[i0], r_k=r_k[i0], gn_w=gn_w[i0], gn_b=gn_b[i0], gate_g2=gate_g2[i0],
             w_o_rwkv=w_o_rwkv[i0], w_out=w_out[i0], router_w=router_w[i0], router_bias=router_bias[i0],
             expert_w1=expert_w1[i0], expert_w3=expert_w3[i0], expert_w2=expert_w2[i0],
             shared_w1=shared_w1[i0], shared_w3=shared_w3[i0], shared_w2=shared_w2[i0])

    mod = _ada_modulation(c, c_ctx, ada_w[i0], ada_b[i0])
    sh_a, sc_a, g_a, sh_m, sc_m, g_m = jnp.split(mod[:B], 6, axis=-1)
    csh_a, csc_a = jnp.split(mod[B], 6, axis=-1)[:2]
    sh2 = jnp.stack([jnp.broadcast_to(csh_a, (B, D)), sh_a], axis=1)
    sc2 = jnp.stack([jnp.broadcast_to(csc_a, (B, D)), sc_a], axis=1)

    proj = _in_proj(ctx, x, sh2, sc2, p['norm_mix'], _pack_w_in(p['w_in']))

    q, k, v = _mla_prep(proj, p, n_ctx, T)
    att = _attention(q, k, v)

    r, vv, kk, lw, bk, kd, bonus, gg = _rwkv_prep(proj, p, n_ctx)
    y_f, y_b = _wkv_scan(r, vv, kk, lw, bk, kd, n_ctx)

    x1 = _merge(att, y_f, y_b, bonus, gg, proj, x, g_a, p, n_ctx)
    return _moe(x1, sh_m, sc_m, g_m, p)


SUBLANES = 8


def _ada_kernel(c_ref, w_ref, b_ref, o_ref):
    cc = c_ref[...]
    w = w_ref[...]
    w_hi = w.astype(BF16)
    w_lo = (w - w_hi.astype(F32)).astype(BF16)
    o_ref[...] = _dot3(cc * jax.nn.sigmoid(cc), w_hi, w_lo) + b_ref[...]


def _ada_modulation(c, c_ctx, ada_w, ada_b):
    B, D = c.shape
    n_out = ada_w.shape[1]
    rows = -(-(B + 1) // SUBLANES) * SUBLANES
    c_all = jnp.concatenate([c, c_ctx[None, :], jnp.zeros((rows - B - 1, D), F32)], axis=0)
    out = pl.pallas_call(
        _ada_kernel,
        grid=(n_out // D,),
        in_specs=[pl.BlockSpec((rows, D), lambda j: (0, 0)),
                  pl.BlockSpec((D, D), lambda j: (0, j)),
                  pl.BlockSpec((1, D), lambda j: (0, j))],
        out_specs=pl.BlockSpec((rows, D), lambda j: (0, j)),
        out_shape=jax.ShapeDtypeStruct((rows, n_out), F32),
        compiler_params=pltpu.CompilerParams(dimension_semantics=("arbitrary",), vmem_limit_bytes=VMEM_LIMIT),
        name="ada_modulation",
    )(c_all, ada_w, ada_b.reshape(1, n_out))
    return out[:B + 1]


ROW_TILE = 256
PROJ_RW = 2048
PROJ_GATE = 2048
PROJ_MLA = 512
PROJ_W = PROJ_RW + PROJ_GATE + PROJ_MLA
MLA_IN = Q_LORA + KV_LORA + QK_ROPE
RW_SPLITS = (RW_DIM, RW_DIM, RW_DIM, DECAY_LORA, DECAY_LORA, AICL_LORA, AICL_LORA, GATE_LORA)
RW_IN = sum(RW_SPLITS)


def _pack_w_in(w_in):
    w_mla = w_in[:, :MLA_IN]
    w_rw = w_in[:, MLA_IN:MLA_IN + RW_IN]
    w_gate = w_in[:, MLA_IN + RW_IN:]
    pad = lambda w, n: jnp.pad(w, ((0, 0), (0, n - w.shape[1])))
    return jnp.concatenate([pad(w_rw, PROJ_RW), w_gate, pad(w_mla, PROJ_MLA)], axis=1).astype(BF16)


PROJ_PIECES = 3
PROJ_COL_CHUNK = PROJ_W // 3


def _in_proj_kernel(ctx_ref, *refs):
    x_refs, (sh_ref, sc_ref, gain_ref, w_ref, o_ref) = refs[:PROJ_PIECES], refs[PROJ_PIECES:]
    first = pl.program_id(1) == 0
    pieces = []
    for n, x_ref in enumerate(x_refs):
        is_ctx = first if n == 0 else False
        xt = jnp.where(is_ctx, ctx_ref[0], x_ref[0]) if n == 0 else x_ref[0]
        sh = jnp.where(is_ctx, sh_ref[0, 0:1], sh_ref[0, 1:2])
        sc = jnp.where(is_ctx, sc_ref[0, 0:1], sc_ref[0, 1:2])
        y = xt * lax.rsqrt(jnp.mean(xt * xt, axis=-1, keepdims=True) + EPS) * gain_ref[...]
        pieces.append((y * (1.0 + sc) + sh).astype(BF16))
    h = jnp.concatenate(pieces, axis=0)
    for c in range(PROJ_W // PROJ_COL_CHUNK):
        cols = slice(c * PROJ_COL_CHUNK, (c + 1) * PROJ_COL_CHUNK)
        o_ref[0, :, cols] = jnp.dot(h, w_ref[:, cols], preferred_element_type=F32).astype(o_ref.dtype)


def _in_proj(ctx, x, sh2, sc2, gain, w):
    B, T, D = x.shape
    n_ctx = ctx.shape[1]
    rows = PROJ_PIECES * ROW_TILE
    assert n_ctx == ROW_TILE and (n_ctx + T) % rows == 0
    nt = (n_ctx + T) // rows
    piece = lambda n: pl.BlockSpec(
        (1, ROW_TILE, D), lambda b, i, n=n: (b, jnp.maximum(PROJ_PIECES * i + n - 1, 0), 0))
    return pl.pallas_call(
        _in_proj_kernel,
        grid=(B, nt),
        in_specs=[pl.BlockSpec((1, ROW_TILE, D), lambda b, i: (b, 0, 0))]
                 + [piece(n) for n in range(PROJ_PIECES)]
                 + [pl.BlockSpec((1, 2, D), lambda b, i: (b, 0, 0)),
                    pl.BlockSpec((1, 2, D), lambda b, i: (b, 0, 0)),
                    pl.BlockSpec((1, D), lambda b, i: (0, 0)),
                    pl.BlockSpec((D, PROJ_W), lambda b, i: (0, 0))],
        out_specs=pl.BlockSpec((1, rows, PROJ_W), lambda b, i: (b, i, 0)),
        out_shape=jax.ShapeDtypeStruct((B, n_ctx + T, PROJ_W), BF16),
        compiler_params=pltpu.CompilerParams(
            dimension_semantics=("arbitrary", "arbitrary"), vmem_limit_bytes=VMEM_LIMIT),
        name="in_proj",
    )(ctx, *([x] * PROJ_PIECES), sh2, sc2, gain.reshape(1, D), w)


def _rms(x, gain):
    return x * lax.rsqrt(jnp.mean(x * x, axis=-1, keepdims=True) + EPS) * gain


def _rope_tables(n_tokens):
    rows = n_tokens // GRID_W
    row = jnp.repeat(jnp.arange(rows, dtype=F32), GRID_W)
    col = jnp.tile(jnp.arange(GRID_W, dtype=F32), rows)
    n_freq = QK_ROPE // 4
    inv_freq = ROPE_BASE ** (-jnp.arange(n_freq, dtype=F32) / n_freq)
    ang = jnp.concatenate([row[:, None] * inv_freq, col[:, None] * inv_freq], axis=-1)
    return jnp.cos(ang), jnp.sin(ang)


ROPE_HALF = QK_ROPE // 2
X1 = slice(QK_NOPE, QK_NOPE + ROPE_HALF)
X2 = slice(QK_NOPE + ROPE_HALF, QK_HEAD)


def _rot_cols(w):
    w3 = w.reshape(w.shape[0], MLA_HEADS, HEAD_PAD)
    rot = jnp.zeros_like(w3).at[:, :, X1].set(-w3[:, :, X2]).at[:, :, X2].set(w3[:, :, X1])
    return rot.reshape(w.shape)


def _swap_halves(g):
    return jnp.zeros_like(g).at[:, X1].set(g[:, X2]).at[:, X2].set(g[:, X1])


def _mla_weights(p):
    H = MLA_HEADS
    wq = jnp.pad(p['w_uq'].reshape(Q_LORA, H, QK_HEAD), ((0, 0), (0, 0), (0, HEAD_PAD - QK_HEAD)))
    wq = wq.reshape(Q_LORA, H * HEAD_PAD)
    wkv = p['w_ukv'].reshape(KV_LORA, H, QK_NOPE + V_HEAD)
    wk_lat = jnp.pad(wkv[:, :, :QK_NOPE], ((0, 0), (0, 0), (0, HEAD_PAD - QK_NOPE)))
    place = jnp.zeros((LANES, H, HEAD_PAD), F32).at[:QK_ROPE, :, QK_NOPE:QK_HEAD].set(
        jnp.broadcast_to(jnp.eye(QK_ROPE, dtype=F32)[:, None, :], (QK_ROPE, H, QK_ROPE)))
    wk = jnp.concatenate([wk_lat, place], axis=0).reshape(KV_LORA + LANES, H * HEAD_PAD)
    wv = wkv[:, :, QK_NOPE:].reshape(KV_LORA, H * V_HEAD)
    gq = jnp.pad(p['q_norm'], (0, HEAD_PAD - QK_HEAD)).reshape(1, HEAD_PAD)
    gk = jnp.pad(p['k_norm'], (0, HEAD_PAD - QK_HEAD)).reshape(1, HEAD_PAD)
    bf = lambda w: w.astype(BF16)
    return (bf(wq), bf(_rot_cols(wq)), bf(wk), bf(_rot_cols(wk)), bf(wv), gq, _swap_halves(gq), gk, _swap_halves(gk))


def _mla_tables(n_ctx, T):
    cos, sin = _rope_tables(T)
    c = jnp.ones((n_ctx + T, HEAD_PAD), F32).at[n_ctx:, X1].set(cos).at[n_ctx:, X2].set(cos)
    s = jnp.zeros((n_ctx + T, HEAD_PAD), F32).at[n_ctx:, X1].set(sin).at[n_ctx:, X2].set(sin)
    return c, s


def _mla_prep_kernel(x_ref, c_ref, s_ref, qlg_ref, kvg_ref, wq_ref, wqr_ref, wk_ref, wkr_ref, wv_ref,
                     gq_ref, gqp_ref, gk_ref, gkp_ref, q_ref, k_ref, v_ref):
    x = x_ref[0].astype(F32)
    ql = _rms(x[:, :Q_LORA], qlg_ref[...])
    kvl = _rms(x[:, Q_LORA:Q_LORA + KV_LORA], kvg_ref[...])
    k_in = jnp.concatenate([kvl, x[:, Q_LORA + KV_LORA:]], axis=1)
    cos, sin = c_ref[...], s_ref[...]

    def finish(raw, partner, g, g_swapped, scale, o_ref):
        gc, gs = g * cos, g_swapped * sin
        for h in range(MLA_HEADS):
            sl = slice(h * HEAD_PAD, (h + 1) * HEAD_PAD)
            rh = raw[:, sl]
            inv = lax.rsqrt(jnp.sum(rh * rh, axis=-1, keepdims=True) * (1.0 / QK_HEAD) + EPS) * scale
            o_ref[0, :, sl] = ((rh * gc + partner[:, sl] * gs) * inv).astype(o_ref.dtype)

    finish(_dot(ql, wq_ref[...]), _dot(ql, wqr_ref[...]), gq_ref[...], gqp_ref[...], ATTN_SCALE, q_ref)
    finish(_dot(k_in, wk_ref[...]), _dot(k_in, wkr_ref[...]), gk_ref[...], gkp_ref[...], 1.0, k_ref)
    v_ref[0] = _dot(kvl, wv_ref[...]).astype(v_ref.dtype)


def _mla_prep(proj, p, n_ctx, T):
    B, Tt, _ = proj.shape
    H = MLA_HEADS
    ctx_tiles = n_ctx // ROW_TILE
    mla_blk = (PROJ_RW + PROJ_GATE) // PROJ_MLA
    weights = _mla_weights(p)
    cos, sin = _mla_tables(n_ctx, T)
    const = lambda b, i: (0, 0)
    row = lambda b, i: (b, i, 0)
    tab = pl.BlockSpec((ROW_TILE, HEAD_PAD), lambda b, i: (i, 0))
    full = lambda a: pl.BlockSpec(a.shape, const)
    return pl.pallas_call(
        _mla_prep_kernel,
        grid=(B, Tt // ROW_TILE),
        in_specs=[pl.BlockSpec((1, ROW_TILE, PROJ_MLA), lambda b, i: (b, i, mla_blk)), tab, tab,
                  pl.BlockSpec((1, Q_LORA), const), pl.BlockSpec((1, KV_LORA), const)]
                 + [full(w) for w in weights],
        out_specs=[
            pl.BlockSpec((1, ROW_TILE, H * HEAD_PAD), lambda b, i: (b, jnp.maximum(i - ctx_tiles, 0), 0)),
            pl.BlockSpec((1, ROW_TILE, H * HEAD_PAD), row),
            pl.BlockSpec((1, ROW_TILE, H * V_HEAD), row),
        ],
        out_shape=[
            jax.ShapeDtypeStruct((B, T, H * HEAD_PAD), BF16),
            jax.ShapeDtypeStruct((B, Tt, H * HEAD_PAD), BF16),
            jax.ShapeDtypeStruct((B, Tt, H * V_HEAD), BF16),
        ],
        compiler_params=pltpu.CompilerParams(
            dimension_semantics=("arbitrary", "arbitrary"), vmem_limit_bytes=VMEM_LIMIT),
        name="mla_prep",
    )(proj, cos, sin, p['q_lat_norm'].reshape(1, Q_LORA), p['kv_lat_norm'].reshape(1, KV_LORA), *weights)


ATTN_Q_TILE = 2048
HEADS_PER_STEP = LANES // V_HEAD


ATTN_ROW_SPLIT = 16


def _attn_kernel(q_ref, k_ref, v_ref, o_ref):
    v2 = v_ref[0]
    rows = q_ref.shape[1] // ATTN_ROW_SPLIT
    work = [(hh, rs) for rs in range(ATTN_ROW_SPLIT) for hh in range(HEADS_PER_STEP)]

    def scores(hh, rs):
        sl = slice(hh * HEAD_PAD, (hh + 1) * HEAD_PAD)
        return _dot_t(q_ref[0, rs * rows:(rs + 1) * rows, sl], k_ref[0, :, sl])

    outs = {}
    s_next = scores(*work[0])
    for i, (hh, rs) in enumerate(work):
        s = s_next
        if i + 1 < len(work):
            s_next = scores(*work[i + 1])
        e = jnp.exp(s - jnp.max(s, axis=-1, keepdims=True))
        outs[hh, rs] = _dot(e, v2) / jnp.sum(e, axis=-1, keepdims=True)
    lane = lax.broadcasted_iota(jnp.int32, (rows, LANES), 1)
    for rs in range(ATTN_ROW_SPLIT):
        o_ref[0, rs * rows:(rs + 1) * rows] = jnp.where(lane < V_HEAD, outs[0, rs], outs[1, rs])


def _attention(q, k, v):
    B, T, _ = q.shape
    Kt = k.shape[1]
    assert T % ATTN_Q_TILE == 0 and ATTN_Q_TILE % (ATTN_ROW_SPLIT * SUBLANES) == 0
    hp = MLA_HEADS // HEADS_PER_STEP
    qw = HEADS_PER_STEP * HEAD_PAD
    return pl.pallas_call(
        _attn_kernel,
        grid=(B, hp, T // ATTN_Q_TILE),
        in_specs=[
            pl.BlockSpec((1, ATTN_Q_TILE, qw), lambda b, h, i: (b, i, h)),
            pl.BlockSpec((1, Kt, qw), lambda b, h, i: (b, 0, h)),
            pl.BlockSpec((1, Kt, LANES), lambda b, h, i: (b, 0, h)),
        ],
        out_specs=pl.BlockSpec((1, ATTN_Q_TILE, LANES), lambda b, h, i: (b, i, h)),
        out_shape=jax.ShapeDtypeStruct((B, T, MLA_HEADS * V_HEAD), F32),
        compiler_params=pltpu.CompilerParams(
            dimension_semantics=("arbitrary", "arbitrary", "arbitrary"), vmem_limit_bytes=VMEM_LIMIT),
        name="attention",
    )(q, k, v)


HALO = 16
LORA_W = LANES
GATE_W = PROJ_RW - 3 * RW_DIM - 2 * LORA_W


def _head_ones(width, head):
    i = np.arange(width) // head
    return jnp.asarray(i[:, None] == i[None, :], BF16)


def _head_sum(x, ones_bd):
    hi = x.astype(BF16)
    lo = (x - hi.astype(F32)).astype(BF16)
    return (jnp.dot(hi, ones_bd, preferred_element_type=F32)
            + jnp.dot(lo, ones_bd, preferred_element_type=F32))


def _dot3(a, b_hi, b_lo):
    hi = a.astype(BF16)
    lo = (a - hi.astype(F32)).astype(BF16)
    d = lambda u, w: jnp.dot(u, w, preferred_element_type=F32)
    return d(hi, b_hi) + (d(hi, b_lo) + d(lo, b_hi))


def _rwkv_prep_kernel(x_ref, prev_ref, next_ref, conv_ref, kkg_ref, ka_ref, rk_ref, w0_ref, a0_ref,
                      w2h_ref, w2l_ref, a2_ref, g2_ref, ones_ref,
                      r_ref, v_ref, kk_ref, lw_ref, bk_ref, kd_ref, bonus_ref, gg_ref, *, ctx_tiles, n_tiles):
    i = pl.program_id(1)
    x = x_ref[0].astype(F32)
    tm, W = x.shape
    C = RW_DIM
    first = (i == 0) | (i == ctx_tiles)
    last = (i == ctx_tiles - 1) | (i == n_tiles - 1)
    prev_row = jnp.where(first, 0.0, prev_ref[0, HALO - 1:HALO].astype(F32))
    next_row = jnp.where(last, 0.0, next_ref[0, 0:1].astype(F32))
    row = lax.broadcasted_iota(jnp.int32, (tm, W), 0)
    x_dn = jnp.where(row == 0, prev_row, pltpu.roll(x, 1, 0))
    x_up = jnp.where(row == tm - 1, next_row, pltpu.roll(x, tm - 1, 0))
    xc = x_dn * conv_ref[0:1] + x * conv_ref[1:2] + x_up * conv_ref[2:3]

    r, k, v = xc[:, :C], xc[:, C:2 * C], xc[:, 2 * C:3 * C]
    lora_w = jnp.tanh(xc[:, 3 * C:3 * C + LORA_W])
    lora_a = xc[:, 3 * C + LORA_W:3 * C + 2 * LORA_W]
    lg = xc[:, 3 * C + 2 * LORA_W:]
    ones_bd = ones_ref[...]
    kq = k * kkg_ref[...]
    kk = kq * lax.rsqrt(_head_sum(kq * kq, ones_bd) + 1e-12)
    r_ref[0], v_ref[0], kk_ref[0] = r, v, kk

    k_sum = jnp.zeros_like(k)
    for d in range(2):
        z = w0_ref[d:d + 1] + _dot3(lora_w, w2h_ref[d], w2l_ref[d])
        softplus_neg = jnp.maximum(-z, 0.0) + jnp.log(1.0 + jnp.exp(-jnp.abs(z)))
        lw_ref[d, 0] = -jnp.exp(-softplus_neg - 0.5)
        a = jax.nn.sigmoid(a0_ref[d:d + 1] + _dot(lora_a, a2_ref[d]))
        kd = k * (1.0 + (a - 1.0) * ka_ref[...])
        bk_ref[d, 0] = kk * a
        kd_ref[d, 0] = kd
        k_sum = k_sum + kd
    bonus_ref[0] = _head_sum(r * k_sum * rk_ref[...], ones_bd) * v
    gg_ref[0] = _dot(jax.nn.sigmoid(lg), g2_ref[...])


def _rwkv_prep(proj, p, n_ctx):
    B, Tt, _ = proj.shape
    C = RW_DIM
    nt = Tt // ROW_TILE
    hb = ROW_TILE // HALO
    pad_cols = lambda w: jnp.pad(w, ((0, 0), (0, PROJ_RW - w.shape[1])))
    w2 = jnp.stack([jnp.pad(p['decay_w2'][0], ((0, LORA_W - DECAY_LORA), (0, 0))),
                    jnp.pad(p['decay_w2'][1], ((DECAY_LORA, 0), (0, 0)))])
    w2h = w2.astype(BF16)
    w2l = (w2 - w2h.astype(F32)).astype(BF16)
    a2 = jnp.stack([jnp.pad(p['aicl_a2'][0], ((0, LORA_W - AICL_LORA), (0, 0))),
                    jnp.pad(p['aicl_a2'][1], ((AICL_LORA, 0), (0, 0)))]).astype(BF16)
    g2 = jnp.pad(p['gate_g2'], ((0, GATE_W - GATE_LORA), (0, 0))).astype(BF16)
    row = lambda b, i: (b, i, 0)
    drow = lambda b, i: (0, b, i, 0)
    const2 = lambda b, i: (0, 0)
    const3 = lambda b, i: (0, 0, 0)
    vec = pl.BlockSpec((1, C), const2)
    out_row = pl.BlockSpec((1, ROW_TILE, C), row)
    out_dir = pl.BlockSpec((2, 1, ROW_TILE, C), drow)
    sds = jax.ShapeDtypeStruct((B, Tt, C), F32)
    sds2 = jax.ShapeDtypeStruct((2, B, Tt, C), F32)
    ctx_tiles = n_ctx // ROW_TILE
    out_lat = pl.BlockSpec((1, ROW_TILE, C), lambda b, i: (b, jnp.maximum(i - ctx_tiles, 0), 0))
    sds_lat = jax.ShapeDtypeStruct((B, Tt - n_ctx, C), F32)
    return pl.pallas_call(
        functools.partial(_rwkv_prep_kernel, ctx_tiles=n_ctx // ROW_TILE, n_tiles=nt),
        grid=(B, nt),
        in_specs=[
            pl.BlockSpec((1, ROW_TILE, PROJ_RW), row),
            pl.BlockSpec((1, HALO, PROJ_RW), lambda b, i: (b, jnp.maximum(i * hb - 1, 0), 0)),
            pl.BlockSpec((1, HALO, PROJ_RW), lambda b, i: (b, jnp.minimum((i + 1) * hb, nt * hb - 1), 0)),
            pl.BlockSpec((3, PROJ_RW), const2),
            vec, vec, vec,
            pl.BlockSpec((2, C), const2),
            pl.BlockSpec((2, C), const2),
            pl.BlockSpec((2, LORA_W, C), const3),
            pl.BlockSpec((2, LORA_W, C), const3),
            pl.BlockSpec((2, LORA_W, C), const3),
            pl.BlockSpec((GATE_W, C), const2),
            pl.BlockSpec((C, C), const2),
        ],
        out_specs=[out_row, out_row, out_row, out_dir, out_dir, out_dir, out_lat, out_lat],
        out_shape=[sds, sds, sds, sds2, sds2, sds2, sds_lat, sds_lat],
        compiler_params=pltpu.CompilerParams(
            dimension_semantics=("arbitrary", "arbitrary"), vmem_limit_bytes=VMEM_LIMIT),
        name="rwkv_prep",
    )(proj, proj, proj, pad_cols(p['shift_conv']), p['k_k'].reshape(1, C), p['k_a'].reshape(1, C),
      p['r_k'].reshape(1, C), p['decay_w0'], p['aicl_a0'], w2h, w2l, a2, g2, _head_ones(C, RW_HEAD))


MERGE_PIECES = 2


def _merge_kernel(att_ref, yf_ref, yb_ref, bonus_ref, gg_ref, *refs):
    gate_refs, (x_ref, ga_ref, gnw_ref, gnb_ref, ones_ref, wm_ref, wr_ref, wo_ref, o_ref) = (
        refs[:MERGE_PIECES], refs[MERGE_PIECES:])
    D = x_ref.shape[-1]
    ones_bd = ones_ref[...]
    y = yf_ref[0] + yb_ref[0]
    yc = y - _head_sum(y, ones_bd) * (1.0 / RW_HEAD)
    var = _head_sum(yc * yc, ones_bd) * (1.0 / RW_HEAD)
    y_n = yc * lax.rsqrt(var + GN_EPS) * gnw_ref[...] + gnb_ref[...]
    rw = (y_n + bonus_ref[0]) * gg_ref[0]
    g = jax.nn.sigmoid(jnp.concatenate([r[0] for r in gate_refs], axis=0).astype(F32))
    mix = g[:, :D] * _dot(att_ref[0], wm_ref[...]) + g[:, D:] * _dot(rw, wr_ref[...])
    o_ref[0] = x_ref[0] + ga_ref[0] * _dot(mix, wo_ref[...])


def _merge(att, y_f, y_b, bonus, gg, proj, x, g_a, p, n_ctx):
    B, T, D = x.shape
    C = RW_DIM
    rows = MERGE_PIECES * ROW_TILE
    ctx_tiles = n_ctx // ROW_TILE
    gate_blk = PROJ_RW // PROJ_GATE
    row = lambda b, i: (b, i, 0)
    const = lambda b, i: (0, 0)
    gate_piece = lambda n: pl.BlockSpec(
        (1, ROW_TILE, PROJ_GATE), lambda b, i, n=n: (b, MERGE_PIECES * i + n + ctx_tiles, gate_blk))
    return pl.pallas_call(
        _merge_kernel,
        grid=(B, T // rows),
        in_specs=[
            pl.BlockSpec((1, rows, att.shape[-1]), row),
            pl.BlockSpec((1, rows, C), row),
            pl.BlockSpec((1, rows, C), row),
            pl.BlockSpec((1, rows, C), row),
            pl.BlockSpec((1, rows, C), row),
        ] + [gate_piece(n) for n in range(MERGE_PIECES)] + [
            pl.BlockSpec((1, rows, D), row),
            pl.BlockSpec((1, 1, D), lambda b, i: (b, 0, 0)),
            pl.BlockSpec((1, C), const),
            pl.BlockSpec((1, C), const),
            pl.BlockSpec((C, C), const),
            pl.BlockSpec(p['w_o_mla'].shape, const),
            pl.BlockSpec(p['w_o_rwkv'].shape, const),
            pl.BlockSpec(p['w_out'].shape, const),
        ],
        out_specs=pl.BlockSpec((1, rows, D), row),
        out_shape=jax.ShapeDtypeStruct((B, T, D), F32),
        compiler_params=pltpu.CompilerParams(
            dimension_semantics=("arbitrary", "arbitrary"), vmem_limit_bytes=VMEM_LIMIT),
        name="merge",
    )(att, y_f, y_b, bonus, gg, *([proj] * MERGE_PIECES), x, g_a[:, None, :], p['gn_w'].reshape(1, C),
      p['gn_b'].reshape(1, C), _head_ones(C, RW_HEAD), p['w_o_mla'].astype(BF16), p['w_o_rwkv'].astype(BF16),
      p['w_out'].astype(BF16))


PACK_CHUNKS = 4
TILE_ROWS = 8
MOE_BLOCK = 512


ROUTER_TILE = 1024
NEG_INF = float("-inf")
HI_MASK = 0xFFFF0000


def _pack_pair(lo, hi):
    lo_b = lax.bitcast_convert_type(lo.astype(BF16).astype(F32), jnp.uint32)
    hi_b = lax.bitcast_convert_type(hi.astype(BF16).astype(F32), jnp.uint32)
    return (lo_b >> 16) | (hi_b & jnp.uint32(HI_MASK))


def _unpack_pair(w):
    lo = lax.bitcast_convert_type(w << 16, F32)
    hi = lax.bitcast_convert_type(w & jnp.uint32(HI_MASK), F32)
    return lo.astype(BF16), hi.astype(BF16)


def _row_max(x):
    return jnp.max(x, axis=-1, keepdims=True)


def _first_index_of(x, value, lane_f):
    return jnp.min(jnp.where(x == value, lane_f, float(x.shape[-1])), axis=-1, keepdims=True)


def _router_kernel(x_ref, gain_ref, sh_ref, sc_ref, wh_ref, wm_ref, wl_ref, bias_ref,
                   u_ref, ut_ref, ids_ref, gates_ref, ranks_ref, counts_ref, carry_ref):
    @pl.when((pl.program_id(0) == 0) & (pl.program_id(1) == 0))
    def _():
        carry_ref[...] = jnp.zeros_like(carry_ref)

    x = x_ref[0]
    tm, D = x.shape
    E = bias_ref.shape[-1]
    u = x * lax.rsqrt(jnp.mean(x * x, axis=-1, keepdims=True) + EPS) * gain_ref[...]
    u = u * (1.0 + sc_ref[0]) + sh_ref[0]
    packed = _pack_pair(u[:, :D // 2], u[:, D // 2:])
    u_ref[0] = packed
    for c in range(PACK_CHUNKS):
        ut_ref[:, c, :] = packed[:, c * LANES:(c + 1) * LANES]

    uh, um, ul = _split3(u)
    wh, wm, wl = wh_ref[...], wm_ref[...], wl_ref[...]
    d = lambda a, b: jnp.dot(a, b, preferred_element_type=F32)
    logits = d(uh, wh) + (d(uh, wm) + d(um, wh)) + (d(uh, wl) + d(um, wm) + d(ul, wh))
    scores = jax.nn.sigmoid(logits)
    sel = scores + bias_ref[...]

    lane_i = lax.broadcasted_iota(jnp.int32, (tm, E), 1)
    lane_f = lane_i.astype(F32)
    out_f = lax.broadcasted_iota(jnp.int32, (tm, LANES), 1).astype(F32)
    per_group = E // N_GROUPS
    grp_f = jnp.floor(lane_f * (1.0 / per_group))

    gs = jnp.full((tm, LANES), NEG_INF, F32)
    for g in range(N_GROUPS):
        sg = jnp.where(lane_i >= g * per_group, jnp.where(lane_i < (g + 1) * per_group, sel, NEG_INF), NEG_INF)
        m1 = _row_max(sg)
        i1 = _first_index_of(sg, m1, lane_f)
        m2 = _row_max(jnp.where(lane_f == i1, NEG_INF, sg))
        gs = jnp.where(out_f == g, m1 + m2, gs)

    allow = jnp.zeros((tm, E), F32)
    for _ in range(TOPK_GROUPS):
        m = _row_max(gs)
        i = _first_index_of(gs, m, out_f)
        gs = jnp.where(out_f == i, NEG_INF, gs)
        allow = jnp.where(grp_f == i, 1.0, allow)
    selm = jnp.where(allow > 0.0, sel, NEG_INF)

    ids = jnp.zeros((tm, LANES), F32)
    gts = jnp.zeros((tm, LANES), F32)
    member = jnp.zeros((tm, E), F32)
    idx_cols = []
    gsum = jnp.zeros((tm, 1), F32)
    for k in range(TOP_K):
        m = _row_max(selm)
        i = _first_index_of(selm, m, lane_f)
        hit = lane_f == i
        gk = jnp.sum(jnp.where(hit, scores, 0.0), axis=-1, keepdims=True)
        selm = jnp.where(hit, NEG_INF, selm)
        member = jnp.where(hit, 1.0, member)
        ids = jnp.where(out_f == k, i, ids)
        gts = jnp.where(out_f == k, gk, gts)
        idx_cols.append(i)
        gsum = gsum + gk
    gts = gts / gsum * ROUTED_SCALE

    r2 = lax.broadcasted_iota(jnp.int32, (tm, tm), 0)
    c2 = lax.broadcasted_iota(jnp.int32, (tm, tm), 1)
    before = jnp.where(r2 > c2, 1.0, 0.0).astype(BF16)
    mem_b = member.astype(BF16)
    carry = carry_ref[...]
    pos = carry + jnp.dot(before, mem_b, preferred_element_type=F32)
    rk = jnp.zeros((tm, LANES), F32)
    for k in range(TOP_K):
        rk = jnp.where(out_f == k, jnp.sum(jnp.where(lane_f == idx_cols[k], pos, 0.0), axis=-1, keepdims=True), rk)
    colsum = jnp.dot(jnp.ones((8, tm), BF16), mem_b, preferred_element_type=F32)[0:1]
    carry_ref[...] = carry + colsum
    counts_ref[...] = carry + colsum
    ids_ref[0] = ids.astype(jnp.int32)
    ranks_ref[0] = rk.astype(jnp.int32)
    gates_ref[0] = gts


def _router(x1, sh_m, sc_m, p):
    B, T, D = x1.shape
    E = N_EXPERTS
    wh, wm, wl = _split3(p['router_w'])
    row = lambda b, i: (b, i, 0)
    const = lambda b, i: (0, 0)
    vec = lambda b, i: (b, 0, 0)
    lane_out = lambda dt: jax.ShapeDtypeStruct((B, T, LANES), dt)
    return pl.pallas_call(
        _router_kernel,
        grid=(B, T // ROUTER_TILE),
        in_specs=[
            pl.BlockSpec((1, ROUTER_TILE, D), row),
            pl.BlockSpec((1, D), const),
            pl.BlockSpec((1, 1, D), vec),
            pl.BlockSpec((1, 1, D), vec),
            pl.BlockSpec((D, E), const),
            pl.BlockSpec((D, E), const),
            pl.BlockSpec((D, E), const),
            pl.BlockSpec((1, E), const),
        ],
        out_specs=[
            pl.BlockSpec((1, ROUTER_TILE, D // 2), row),
            pl.BlockSpec((ROUTER_TILE, PACK_CHUNKS, LANES), lambda b, i: (b * (T // ROUTER_TILE) + i, 0, 0)),
            pl.BlockSpec((1, ROUTER_TILE, LANES), row),
            pl.BlockSpec((1, ROUTER_TILE, LANES), row),
            pl.BlockSpec((1, ROUTER_TILE, LANES), row),
            pl.BlockSpec((1, E), const),
        ],
        out_shape=[
            jax.ShapeDtypeStruct((B, T, D // 2), jnp.uint32),
            jax.ShapeDtypeStruct((B * T, PACK_CHUNKS, LANES), jnp.uint32),
            lane_out(jnp.int32), lane_out(F32), lane_out(jnp.int32),
            jax.ShapeDtypeStruct((1, E), F32),
        ],
        scratch_shapes=[pltpu.VMEM((1, E), F32)],
        compiler_params=pltpu.CompilerParams(
            dimension_semantics=("arbitrary", "arbitrary"), vmem_limit_bytes=VMEM_LIMIT),
        name="router",
    )(x1, p['norm_ffn'].reshape(1, D), sh_m[:, None, :], sc_m[:, None, :], wh, wm, wl,
      p['router_bias'].reshape(1, E))


def _slot_kernel(ids_ref, ranks_ref, base_ref, o_ref):
    ids = ids_ref[...].astype(F32)
    tm = ids.shape[0]
    E = base_ref.shape[-1]
    lane_e = lax.broadcasted_iota(jnp.int32, (tm, E), 1).astype(F32)
    out_lane = lax.broadcasted_iota(jnp.int32, (tm, LANES), 1)
    first = jnp.zeros((tm, LANES), F32)
    for k in range(TOP_K):
        fk = jnp.sum(jnp.where(lane_e == ids[:, k:k + 1], base_ref[...], 0.0), axis=-1, keepdims=True)
        first = jnp.where(out_lane == k, fk, first)
    o_ref[...] = first.astype(jnp.int32) + ranks_ref[...]


def _slots(ids, ranks, base):
    n_tok = ids.shape[0]
    E = base.shape[-1]
    row = pl.BlockSpec((ROUTER_TILE, LANES), lambda i: (i, 0))
    return pl.pallas_call(
        _slot_kernel,
        grid=(n_tok // ROUTER_TILE,),
        in_specs=[row, row, pl.BlockSpec((1, E), lambda i: (0, 0))],
        out_specs=row,
        out_shape=jax.ShapeDtypeStruct((n_tok, LANES), jnp.int32),
        compiler_params=pltpu.CompilerParams(dimension_semantics=("arbitrary",)),
        name="slots",
    )(ids, ranks, base)


DMA_PRIORITIES = 2


def _row_copy(src, src_row, dst, dst_row, sem):
    return pltpu.make_async_copy(src.at[pl.ds(src_row, 1)], dst.at[pl.ds(dst_row, 1)], sem)


def _dispatch_kernel(pad_ref, dest_ref, u_ref, slots_ref, zero_ref, sem):
    tm = u_ref.shape[0]
    n_experts = pad_ref.shape[0]

    @pl.when(pl.program_id(0) == 0)
    def _():
        zero_ref[...] = jnp.zeros_like(zero_ref)

        def fill(e, carry):
            pltpu.make_async_copy(zero_ref, slots_ref.at[pl.ds(pad_ref[e], MOE_BLOCK)], sem).start()
            return carry

        lax.fori_loop(0, n_experts, fill, 0)

        def filled(e, carry):
            pltpu.make_async_copy(zero_ref, slots_ref.at[pl.ds(0, MOE_BLOCK)], sem).wait()
            return carry

        lax.fori_loop(0, n_experts, filled, 0)

    def issue(r, carry):
        for k in range(TOP_K):
            pltpu.make_async_copy(u_ref.at[r], slots_ref.at[dest_ref[0, 0, r * TOP_K + k]], sem).start(
                priority=k % DMA_PRIORITIES)
        return carry

    lax.fori_loop(0, tm, issue, 0)
    for k in range(TOP_K):
        pltpu.make_async_copy(u_ref, slots_ref.at[pl.ds(0, tm)], sem).wait()


def _dispatch(pad_start, dest, u_tiles, n_slots):
    n_tok = u_tiles.shape[0]
    tile = u_tiles.shape[1:]
    nt = n_tok // ROW_TILE
    grid_spec = pltpu.PrefetchScalarGridSpec(
        num_scalar_prefetch=1,
        grid=(nt,),
        in_specs=[
            pl.BlockSpec((1, 1, ROW_TILE * TOP_K), lambda i, pad: (i, 0, 0), memory_space=pltpu.SMEM),
            pl.BlockSpec((ROW_TILE,) + tile, lambda i, pad: (i, 0, 0)),
        ],
        out_specs=pl.BlockSpec(memory_space=pl.ANY),
        scratch_shapes=[pltpu.VMEM((MOE_BLOCK,) + tile, jnp.uint32), pltpu.SemaphoreType.DMA(())],
    )
    return pl.pallas_call(
        _dispatch_kernel,
        grid_spec=grid_spec,
        out_shape=jax.ShapeDtypeStruct((n_slots + MOE_BLOCK,) + tile, jnp.uint32),
        compiler_params=pltpu.CompilerParams(
            dimension_semantics=("arbitrary",), vmem_limit_bytes=VMEM_LIMIT),
        name="dispatch",
    )(pad_start, dest, u_tiles)


WEIGHT_SLOTS = 2


def _expert_weight_copies(hbm_refs, stage_refs, expert, slot, sems):
    return [pltpu.make_async_copy(w.at[expert], s.at[slot], sems.at[slot, i])
            for i, (w, s) in enumerate(zip(hbm_refs, stage_refs))]


def _moe_ffn_kernel(be_ref, first_ref, slot_ref, next_ref, nu_ref, x_ref, w1_ref, w3_ref, w2_ref, o_ref,
                    s1_ref, s3_ref, s2_ref, w13_ref, w2b_ref, sems):
    j = pl.program_id(0)
    F = w1_ref.shape[-1]
    half = PACK_CHUNKS * LANES
    copies = functools.partial(_expert_weight_copies, (w1_ref, w3_ref, w2_ref), (s1_ref, s3_ref, s2_ref), sems=sems)

    @pl.when(j == 0)
    def _():
        for c in copies(be_ref[0], 0):
            c.start()

    @pl.when(first_ref[j] == 1)
    def _():
        slot = slot_ref[j]
        for c in copies(be_ref[j], slot):
            c.wait()

        @pl.when(next_ref[j] >= 0)
        def _():
            for c in copies(next_ref[j], 1 - slot):
                c.start()

        w13_ref[:, :F] = s1_ref[slot].astype(BF16)
        w13_ref[:, F:] = s3_ref[slot].astype(BF16)
        w2b_ref[...] = s2_ref[slot].astype(BF16)

    @pl.when(j < nu_ref[0])
    def _():
        lo, hi = _unpack_pair(jnp.concatenate([x_ref[:, c, :] for c in range(PACK_CHUNKS)], axis=1))
        h = (jnp.dot(lo, w13_ref[:half], preferred_element_type=F32)
             + jnp.dot(hi, w13_ref[half:], preferred_element_type=F32))
        h1, h3 = h[:, :F], h[:, F:]
        y = _dot(h1 * jax.nn.sigmoid(h1) * h3, w2b_ref[...])
        o_ref[...] = pltpu.einshape("s(cl)->(sc)l", y, c=TILE_ROWS)

    @pl.when(j >= nu_ref[0])
    def _():
        o_ref[...] = jnp.zeros_like(o_ref)


def _moe_ffn(block_e, n_used, x_slots, w1, w3, w2):
    E, D, F = w1.shape
    nblk = block_e.shape[0]
    slots = nblk * MOE_BLOCK
    idx = jnp.arange(nblk)
    first = jnp.concatenate([jnp.ones((1,), bool), block_e[1:] != block_e[:-1]])
    slot = ((jnp.cumsum(first) - 1) % WEIGHT_SLOTS).astype(jnp.int32)
    later_first = lax.cummin(jnp.where(first, idx, nblk)[::-1])[::-1]
    next_first = jnp.concatenate([later_first[1:], jnp.full((1,), nblk)])
    next_e = jnp.where(next_first < nblk, block_e[jnp.minimum(next_first, nblk - 1)], -1).astype(jnp.int32)
    n_prefetch = 5
    blk = lambda j, *_: (jnp.minimum(j, _[4][0] - 1), 0, 0)
    grid_spec = pltpu.PrefetchScalarGridSpec(
        num_scalar_prefetch=n_prefetch,
        grid=(nblk,),
        in_specs=[
            pl.BlockSpec((MOE_BLOCK, PACK_CHUNKS, LANES), blk),
            pl.BlockSpec(memory_space=pl.ANY),
            pl.BlockSpec(memory_space=pl.ANY),
            pl.BlockSpec(memory_space=pl.ANY),
        ],
        out_specs=pl.BlockSpec((MOE_BLOCK * TILE_ROWS, LANES), lambda j, *_: (j, 0)),
        scratch_shapes=[pltpu.VMEM((WEIGHT_SLOTS, D, F), F32), pltpu.VMEM((WEIGHT_SLOTS, D, F), F32),
                        pltpu.VMEM((WEIGHT_SLOTS, F, D), F32),
                        pltpu.VMEM((D, 2 * F), BF16), pltpu.VMEM((F, D), BF16),
                        pltpu.SemaphoreType.DMA((WEIGHT_SLOTS, 3))],
    )
    return pl.pallas_call(
        _moe_ffn_kernel,
        grid_spec=grid_spec,
        out_shape=jax.ShapeDtypeStruct((slots * TILE_ROWS, LANES), F32),
        compiler_params=pltpu.CompilerParams(
            dimension_semantics=("arbitrary",), vmem_limit_bytes=VMEM_LIMIT),
        name="moe_ffn",
    )(block_e, first.astype(jnp.int32), slot, next_e, n_used, x_slots, w1, w3, w2)


COMBINE_ROWS = 32


def _combine_kernel(dest_ref, gates_ref, u_ref, x_ref, gm_ref, w1_ref, w3_ref, w2_ref, ys_ref, o_ref, buf_ref, sem):
    tm, half = u_ref.shape

    def tile(i):
        return pl.ds(pl.multiple_of(i * TILE_ROWS, TILE_ROWS), TILE_ROWS)

    def issue(r, carry):
        for k in range(TOP_K):
            pltpu.make_async_copy(ys_ref.at[tile(dest_ref[0, 0, r * TOP_K + k])], buf_ref.at[k, tile(r)], sem).start(
                priority=k % DMA_PRIORITIES)
        return carry

    lax.fori_loop(0, tm, issue, 0)

    ulo, uhi = _unpack_pair(u_ref[...])
    both = lambda w_ref: (jnp.dot(ulo, w_ref[:half], preferred_element_type=F32)
                          + jnp.dot(uhi, w_ref[half:], preferred_element_type=F32))
    h1, h3 = both(w1_ref), both(w3_ref)
    o_ref[...] = x_ref[...] + gm_ref[0] * _dot(h1 * jax.nn.sigmoid(h1) * h3, w2_ref[...])

    for k in range(TOP_K):
        pltpu.make_async_copy(ys_ref.at[pl.ds(0, tm * TILE_ROWS)], buf_ref.at[k], sem).wait()

    gm = gm_ref[0]
    for rb in range(tm // COMBINE_ROWS):
        rows = slice(rb * COMBINE_ROWS, (rb + 1) * COMBINE_ROWS)
        gates = gates_ref[rows, :]
        for c in range(TILE_ROWS):
            acc = jnp.zeros((COMBINE_ROWS, LANES), F32)
            for k in range(TOP_K):
                acc = acc + gates[:, k:k + 1] * buf_ref[
                    k, pl.ds(rb * COMBINE_ROWS * TILE_ROWS + c, COMBINE_ROWS, stride=TILE_ROWS), :]
            cols = slice(c * LANES, (c + 1) * LANES)
            o_ref[rows, cols] = o_ref[rows, cols] + gm[:, cols] * acc


def _combine(dest, gates, u_rows, x1, g_m, y_slots, p):
    B, T, D = x1.shape
    n_tok = B * T
    W = u_rows.shape[1]
    tiles_per_batch = T // ROW_TILE
    row = lambda i: (i, 0)
    const = lambda i: (0, 0)
    out = pl.pallas_call(
        _combine_kernel,
        grid=(n_tok // ROW_TILE,),
        in_specs=[
            pl.BlockSpec((1, 1, ROW_TILE * TOP_K), lambda i: (i, 0, 0), memory_space=pltpu.SMEM),
            pl.BlockSpec((ROW_TILE, LANES), row),
            pl.BlockSpec((ROW_TILE, W), row),
            pl.BlockSpec((ROW_TILE, D), row),
            pl.BlockSpec((1, 1, D), lambda i: (i // tiles_per_batch, 0, 0)),
            pl.BlockSpec(p['shared_w1'].shape, const),
            pl.BlockSpec(p['shared_w3'].shape, const),
            pl.BlockSpec(p['shared_w2'].shape, const),
            pl.BlockSpec(memory_space=pl.ANY),
        ],
        out_specs=pl.BlockSpec((ROW_TILE, D), row),
        out_shape=jax.ShapeDtypeStruct((n_tok, D), F32),
        scratch_shapes=[pltpu.VMEM((TOP_K, ROW_TILE * TILE_ROWS, LANES), F32), pltpu.SemaphoreType.DMA(())],
        compiler_params=pltpu.CompilerParams(
            dimension_semantics=("arbitrary",), vmem_limit_bytes=VMEM_LIMIT),
        name="combine",
    )(dest, gates.reshape(n_tok, LANES), u_rows, x1.reshape(n_tok, D), g_m[:, None, :],
      p['shared_w1'].astype(BF16), p['shared_w3'].astype(BF16), p['shared_w2'].astype(BF16), y_slots)
    return out.reshape(B, T, D)


def _moe(x1, sh_m, sc_m, g_m, p):
    B, T, D = x1.shape
    n_tok = B * T
    u_packed, u_tiles, ids, gates, ranks, counts = _router(x1, sh_m, sc_m, p)

    counts = counts[0].astype(jnp.int32)
    padded = (counts + MOE_BLOCK - 1) // MOE_BLOCK * MOE_BLOCK
    padded_end = jnp.cumsum(padded)
    base = padded_end - padded
    n_blocks = n_tok * TOP_K // MOE_BLOCK + N_EXPERTS
    n_used = (padded_end[-1] // MOE_BLOCK).astype(jnp.int32)
    blk = jnp.minimum(jnp.arange(n_blocks), n_used - 1) * MOE_BLOCK
    block_e = jnp.minimum(jnp.sum(padded_end[None, :] <= blk[:, None], axis=1), N_EXPERTS - 1).astype(jnp.int32)
    dest = _slots(ids.reshape(n_tok, LANES), ranks.reshape(n_tok, LANES), base.astype(F32).reshape(1, N_EXPERTS))
    dest = dest[:, :TOP_K].reshape(n_tok // ROW_TILE, 1, ROW_TILE * TOP_K)

    u_rows = u_packed.reshape(n_tok, D // 2)
    x_slots = _dispatch((base + counts).astype(jnp.int32), dest, u_tiles, n_blocks * MOE_BLOCK)
    y_slots = _moe_ffn(block_e, n_used.reshape(1), x_slots, p['expert_w1'], p['expert_w3'], p['expert_w2'])
    return _combine(dest, gates, u_rows, x1, g_m, y_slots, p)
```

```python
import functools

import jax
import jax.numpy as jnp
import numpy as np
from jax import lax
from jax.experimental import pallas as pl
from jax.experimental.pallas import tpu as pltpu

F32 = jnp.float32
BF16 = jnp.bfloat16

GRID_W = 64
EPS = 1e-6
MLA_HEADS = 8
QK_NOPE = 64
QK_ROPE = 32
QK_HEAD = QK_NOPE + QK_ROPE
V_HEAD = 64
Q_LORA = 256
KV_LORA = 128
ROPE_BASE = 10000.0
ATTN_SCALE = QK_HEAD ** -0.5
RW_HEADS = 8
RW_HEAD = 64
RW_DIM = RW_HEADS * RW_HEAD
DECAY_LORA = 64
AICL_LORA = 64
GATE_LORA = 160
GN_EPS = 64e-5
N_EXPERTS = 256
TOP_K = 8
N_GROUPS = 8
TOPK_GROUPS = 4
ROUTED_SCALE = 2.5

LANES = 128
VMEM_LIMIT = 56 * 1024 * 1024

WKV_CHUNK = 64
WKV_GROUP = LANES
HEADS_PER_GROUP = WKV_GROUP // RW_HEAD
HEAD_PAD = LANES


def _dot(a, b):
    return jnp.dot(a.astype(BF16), b.astype(BF16), preferred_element_type=F32)


def _dot_t(a, b):
    return lax.dot_general(a.astype(BF16), b.astype(BF16), (((1,), (1,)), ((), ())),
                           preferred_element_type=F32)


def _split3(x):
    h = x.astype(BF16)
    r1 = x - h.astype(F32)
    m = r1.astype(BF16)
    lo = (r1 - m.astype(F32)).astype(BF16)
    return h, m, lo


def _dot_hi(a_bf16_exact, x):
    h, m, lo = _split3(x)
    d = lambda y: jnp.dot(a_bf16_exact, y, preferred_element_type=F32)
    return d(h) + d(m) + d(lo)


def _wkv_chunk(r, v, kk, lw, bk, kd, s0, reverse, emit):
    L, G = r.shape
    row = lax.broadcasted_iota(jnp.int32, (L, G), 0)
    lane = lax.broadcasted_iota(jnp.int32, (L, G), 1)
    diff = (lane % L - row) if reverse else (row - lane % L)
    strict = diff > 0
    incl = diff >= 0
    r2 = lax.broadcasted_iota(jnp.int32, (L, L), 0)
    c2 = lax.broadcasted_iota(jnp.int32, (L, L), 1)
    tri = jnp.where(((c2 - r2) if reverse else (r2 - c2)) >= 0, 1.0, 0.0).astype(BF16)
    tri_ones = jnp.concatenate([tri, jnp.ones((L, L), BF16)], axis=0)
    lane_head = lane // RW_HEAD
    rowb = lax.broadcasted_iota(jnp.int32, (G, G), 0) // RW_HEAD
    colb = lax.broadcasted_iota(jnp.int32, (G, G), 1) // RW_HEAD

    def stack(x):
        return jnp.concatenate(
            [jnp.where(lane_head == h, x, 0.0) for h in range(HEADS_PER_GROUP)], axis=0).astype(BF16)

    sums = _dot_hi(tri_ones, lw)
    cum_in = sums[:L]
    g_in = jnp.exp(cum_in)
    g_inv = jnp.exp(-cum_in)
    g_ex = jnp.exp(cum_in - lw)
    g_tot = jnp.exp(sums[L:])
    a_h = -kk * g_ex
    b_h = bk * g_inv
    k_h = kd * g_inv
    r_h = r * g_in
    yield

    gram = _dot_t(jnp.concatenate([a_h, r_h], axis=0),
                  jnp.concatenate([stack(b_h), stack(k_h)], axis=0))
    ab = jnp.where(strict, gram[:L, :G], 0.0)
    ak = jnp.where(strict, gram[:L, G:], 0.0)
    rb = jnp.where(incl, gram[L:, :G], 0.0)
    rk = jnp.where(incl, gram[L:, G:], 0.0)
    yield

    tm = jnp.where(diff == 0, 1.0, 0.0) + ab
    p = ab
    v_st = stack(v)
    akv = _dot(ak, v_st)
    for _ in range(int(np.log2(L)) - 1):
        p = _dot(p, stack(p))
        yield
        tm = tm + _dot(tm, stack(p))
        yield

    wu = _dot(tm, jnp.concatenate([stack(a_h), stack(akv)], axis=1))
    w_t, u_t = wu[:, :G], wu[:, G:]
    yield
    rbwu = _dot(rb, jnp.concatenate([stack(w_t), stack(u_t)], axis=1))
    r_t = r_h + rbwu[:, :G]
    y_t = rbwu[:, G:] + _dot(rk, v_st)
    yield

    s0_st = stack(s0)
    y = _dot_t(r_t, s0_st) + y_t
    u = _dot_t(w_t, s0_st) + u_t
    yield
    uv = jnp.concatenate([u, v], axis=0)
    bkc = jnp.concatenate([b_h, k_h], axis=0)
    upd = lax.dot_general(uv.astype(BF16), bkc.astype(BF16), (((0,), (0,)), ((), ())),
                          preferred_element_type=F32)
    upd = jnp.where(rowb == colb, upd, 0.0)
    upd_d = upd[0:L]
    for h in range(1, HEADS_PER_GROUP):
        upd_d = upd_d + upd[h * L:(h + 1) * L]
    emit(y, (s0 + upd_d) * g_tot)


def _wkv_kernel(*refs, n_groups, n_batch):
    ins, (yf_ref, yb_ref, s_ref) = refs[:12], refs[12:]
    G = WKV_GROUP

    @pl.when(pl.program_id(1) == 0)
    def _():
        s_ref[...] = jnp.zeros_like(s_ref)

    chains = []
    for d, y_ref in enumerate((yf_ref, yb_ref)):
        r_ref, v_ref, kk_ref, lw_ref, bk_ref, kd_ref = ins[6 * d:6 * d + 6]
        for bi in range(n_batch):
            for g in range(n_groups):
                sl = slice(g * G, (g + 1) * G)

                def emit(y, s_new, y_ref=y_ref, d=d, bi=bi, g=g, sl=sl):
                    y_ref[bi, :, sl] = y
                    s_ref[d, bi, g] = s_new

                chains.append(_wkv_chunk(
                    r_ref[bi, :, sl], v_ref[bi, :, sl], kk_ref[bi, :, sl], lw_ref[0, bi, :, sl],
                    bk_ref[0, bi, :, sl], kd_ref[0, bi, :, sl], s_ref[d, bi, g], d == 1, emit))
    while chains:
        chains = [c for c in chains if next(c, StopIteration) is not StopIteration]


WKV_BATCH = 4


def _wkv_scan(r, v, kk, lw, bk, kd, n_ctx):
    B, Ttot, C = r.shape
    L = WKV_CHUNK
    nc = Ttot // L
    ncc = n_ctx // L
    nl = nc - ncc
    n_groups = C // WKV_GROUP
    nb = WKV_BATCH

    cid = (lambda s: s, lambda s: jnp.where(s < ncc, ncc - 1 - s, nc + ncc - 1 - s))
    first_out = (0, nl - 1)
    in_specs, out_specs = [], []
    for d in range(2):
        shared = pl.BlockSpec((nb, L, C), lambda b, s, d=d: (b, cid[d](s), 0))
        per_dir = pl.BlockSpec((1, nb, L, C), lambda b, s, d=d: (d, b, cid[d](s), 0))
        in_specs += [shared, shared, shared, per_dir, per_dir, per_dir]
        out_specs.append(pl.BlockSpec(
            (nb, L, C), lambda b, s, d=d: (b, jnp.where(s < ncc, first_out[d], cid[d](s) - ncc), 0)))
    out = jax.ShapeDtypeStruct((B, nl * L, C), F32)
    return pl.pallas_call(
        functools.partial(_wkv_kernel, n_groups=n_groups, n_batch=nb),
        grid=(B // nb, nc),
        in_specs=in_specs,
        out_specs=out_specs,
        out_shape=[out, out],
        scratch_shapes=[pltpu.VMEM((2, nb, n_groups, L, WKV_GROUP), F32)],
        compiler_params=pltpu.CompilerParams(
            dimension_semantics=("arbitrary", "arbitrary"), vmem_limit_bytes=VMEM_LIMIT),
        name="wkv_scan",
    )(r, v, kk, lw, bk, kd, r, v, kk, lw, bk, kd)


def kernel(x, c, ctx, c_ctx, ada_w, ada_b, norm_mix, norm_ffn, w_in, shift_conv, q_lat_norm, w_uq, kv_lat_norm, w_ukv, q_norm, k_norm, w_o_mla, decay_w0, decay_w2, aicl_a0, aicl_a2, k_k, k_a, r_k, gn_w, gn_b, gate_g2, w_o_rwkv, w_out, router_w, router_bias, expert_w1, expert_w3, expert_w2, shared_w1, shared_w3, shared_w2):
    B, T, D = x.shape
    n_ctx = ctx.shape[1]
    i0 = 0
    p = dict(norm_mix=norm_mix[i0], norm_ffn=norm_ffn[i0], w_in=w_in[i0], shift_conv=shift_conv[i0],
             q_lat_norm=q_lat_norm[i0], w_uq=w_uq[i0], kv_lat_norm=kv_lat_norm[i0], w_ukv=w_ukv[i0],
             q_norm=q_norm[i0], k_norm=k_norm[i0], w_o_mla=w_o_mla[i0],
             decay_w0=decay_w0[i0], decay_w2=decay_w2[i0], aicl_a0=aicl_a0[i0], aicl_a2=aicl_a2[i0],
             k_k=k_k[i0], k_a=k_a[i0], r_k=r_k[i0], gn_w=gn_w[i0], gn_b=gn_b[i0], gate_g2=gate_g2[i0],
             w_o_rwkv=w_o_rwkv[i0], w_out=w_out[i0], router_w=router_w[i0], router_bias=router_bias[i0],
             expert_w1=expert_w1[i0], expert_w3=expert_w3[i0], expert_w2=expert_w2[i0],
             shared_w1=shared_w1[i0], shared_w3=shared_w3[i0], shared_w2=shared_w2[i0])

    mod = _ada_modulation(c, c_ctx, ada_w[i0], ada_b[i0])
    sh_a, sc_a, g_a, sh_m, sc_m, g_m = jnp.split(mod[:B], 6, axis=-1)
    csh_a, csc_a = jnp.split(mod[B], 6, axis=-1)[:2]
    sh2 = jnp.stack([jnp.broadcast_to(csh_a, (B, D)), sh_a], axis=1)
    sc2 = jnp.stack([jnp.broadcast_to(csc_a, (B, D)), sc_a], axis=1)

    proj = _in_proj(ctx, x, sh2, sc2, p['norm_mix'], _pack_w_in(p['w_in']))

    q, k, v = _mla_prep(proj, p, n_ctx, T)
    att = _attention(q, k, v)

    r, vv, kk, lw, bk, kd, bonus, gg = _rwkv_prep(proj, p, n_ctx)
    y_f, y_b = _wkv_scan(r, vv, kk, lw, bk, kd, n_ctx)

    x1 = _merge(att, y_f, y_b, bonus, gg, proj, x, g_a, p, n_ctx)
    return _moe(x1, sh_m, sc_m, g_m, p)


SUBLANES = 8


def _ada_kernel(c_ref, w_ref, b_ref, o_ref):
    cc = c_ref[...]
    w = w_ref[...]
    w_hi = w.astype(BF16)
    w_lo = (w - w_hi.astype(F32)).astype(BF16)
    o_ref[...] = _dot3(cc * jax.nn.sigmoid(cc), w_hi, w_lo) + b_ref[...]


def _ada_modulation(c, c_ctx, ada_w, ada_b):
    B, D = c.shape
    n_out = ada_w.shape[1]
    rows = -(-(B + 1) // SUBLANES) * SUBLANES
    c_all = jnp.concatenate([c, c_ctx[None, :], jnp.zeros((rows - B - 1, D), F32)], axis=0)
    out = pl.pallas_call(
        _ada_kernel,
        grid=(n_out // D,),
        in_specs=[pl.BlockSpec((rows, D), lambda j: (0, 0)),
                  pl.BlockSpec((D, D), lambda j: (0, j)),
                  pl.BlockSpec((1, D), lambda j: (0, j))],
        out_specs=pl.BlockSpec((rows, D), lambda j: (0, j)),
        out_shape=jax.ShapeDtypeStruct((rows, n_out), F32),
        compiler_params=pltpu.CompilerParams(dimension_semantics=("arbitrary",), vmem_limit_bytes=VMEM_LIMIT),
        name="ada_modulation",
    )(c_all, ada_w, ada_b.reshape(1, n_out))
    return out[:B + 1]


ROW_TILE = 256
PROJ_RW = 2048
PROJ_GATE = 2048
PROJ_MLA = 512
PROJ_W = PROJ_RW + PROJ_GATE + PROJ_MLA
MLA_IN = Q_LORA + KV_LORA + QK_ROPE
RW_SPLITS = (RW_DIM, RW_DIM, RW_DIM, DECAY_LORA, DECAY_LORA, AICL_LORA, AICL_LORA, GATE_LORA)
RW_IN = sum(RW_SPLITS)


def _pack_w_in(w_in):
    w_mla = w_in[:, :MLA_IN]
    w_rw = w_in[:, MLA_IN:MLA_IN + RW_IN]
    w_gate = w_in[:, MLA_IN + RW_IN:]
    pad = lambda w, n: jnp.pad(w, ((0, 0), (0, n - w.shape[1])))
    return jnp.concatenate([pad(w_rw, PROJ_RW), w_gate, pad(w_mla, PROJ_MLA)], axis=1).astype(BF16)


PROJ_PIECES = 3
PROJ_COL_CHUNK = PROJ_W // 3


def _in_proj_kernel(ctx_ref, *refs):
    x_refs, (sh_ref, sc_ref, gain_ref, w_ref, o_ref) = refs[:PROJ_PIECES], refs[PROJ_PIECES:]
    first = pl.program_id(1) == 0
    pieces = []
    for n, x_ref in enumerate(x_refs):
        is_ctx = first if n == 0 else False
        xt = jnp.where(is_ctx, ctx_ref[0], x_ref[0]) if n == 0 else x_ref[0]
        sh = jnp.where(is_ctx, sh_ref[0, 0:1], sh_ref[0, 1:2])
        sc = jnp.where(is_ctx, sc_ref[0, 0:1], sc_ref[0, 1:2])
        y = xt * lax.rsqrt(jnp.mean(xt * xt, axis=-1, keepdims=True) + EPS) * gain_ref[...]
        pieces.append((y * (1.0 + sc) + sh).astype(BF16))
    h = jnp.concatenate(pieces, axis=0)
    for c in range(PROJ_W // PROJ_COL_CHUNK):
        cols = slice(c * PROJ_COL_CHUNK, (c + 1) * PROJ_COL_CHUNK)
        o_ref[0, :, cols] = jnp.dot(h, w_ref[:, cols], preferred_element_type=F32).astype(o_ref.dtype)


def _in_proj(ctx, x, sh2, sc2, gain, w):
    B, T, D = x.shape
    n_ctx = ctx.shape[1]
    rows = PROJ_PIECES * ROW_TILE
    assert n_ctx == ROW_TILE and (n_ctx + T) % rows == 0
    nt = (n_ctx + T) // rows
    piece = lambda n: pl.BlockSpec(
        (1, ROW_TILE, D), lambda b, i, n=n: (b, jnp.maximum(PROJ_PIECES * i + n - 1, 0), 0))
    return pl.pallas_call(
        _in_proj_kernel,
        grid=(B, nt),
        in_specs=[pl.BlockSpec((1, ROW_TILE, D), lambda b, i: (b, 0, 0))]
                 + [piece(n) for n in range(PROJ_PIECES)]
                 + [pl.BlockSpec((1, 2, D), lambda b, i: (b, 0, 0)),
                    pl.BlockSpec((1, 2, D), lambda b, i: (b, 0, 0)),
                    pl.BlockSpec((1, D), lambda b, i: (0, 0)),
                    pl.BlockSpec((D, PROJ_W), lambda b, i: (0, 0))],
        out_specs=pl.BlockSpec((1, rows, PROJ_W), lambda b, i: (b, i, 0)),
        out_shape=jax.ShapeDtypeStruct((B, n_ctx + T, PROJ_W), BF16),
        compiler_params=pltpu.CompilerParams(
            dimension_semantics=("arbitrary", "arbitrary"), vmem_limit_bytes=VMEM_LIMIT),
        name="in_proj",
    )(ctx, *([x] * PROJ_PIECES), sh2, sc2, gain.reshape(1, D), w)


def _rms(x, gain):
    return x * lax.rsqrt(jnp.mean(x * x, axis=-1, keepdims=True) + EPS) * gain


def _rope_tables(n_tokens):
    rows = n_tokens // GRID_W
    row = jnp.repeat(jnp.arange(rows, dtype=F32), GRID_W)
    col = jnp.tile(jnp.arange(GRID_W, dtype=F32), rows)
    n_freq = QK_ROPE // 4
    inv_freq = ROPE_BASE ** (-jnp.arange(n_freq, dtype=F32) / n_freq)
    ang = jnp.concatenate([row[:, None] * inv_freq, col[:, None] * inv_freq], axis=-1)
    return jnp.cos(ang), jnp.sin(ang)


ROPE_HALF = QK_ROPE // 2
X1 = slice(QK_NOPE, QK_NOPE + ROPE_HALF)
X2 = slice(QK_NOPE + ROPE_HALF, QK_HEAD)


def _rot_cols(w):
    w3 = w.reshape(w.shape[0], MLA_HEADS, HEAD_PAD)
    rot = jnp.zeros_like(w3).at[:, :, X1].set(-w3[:, :, X2]).at[:, :, X2].set(w3[:, :, X1])
    return rot.reshape(w.shape)


def _swap_halves(g):
    return jnp.zeros_like(g).at[:, X1].set(g[:, X2]).at[:, X2].set(g[:, X1])


def _mla_weights(p):
    H = MLA_HEADS
    wq = jnp.pad(p['w_uq'].reshape(Q_LORA, H, QK_HEAD), ((0, 0), (0, 0), (0, HEAD_PAD - QK_HEAD)))
    wq = wq.reshape(Q_LORA, H * HEAD_PAD)
    wkv = p['w_ukv'].reshape(KV_LORA, H, QK_NOPE + V_HEAD)
    wk_lat = jnp.pad(wkv[:, :, :QK_NOPE], ((0, 0), (0, 0), (0, HEAD_PAD - QK_NOPE)))
    place = jnp.zeros((LANES, H, HEAD_PAD), F32).at[:QK_ROPE, :, QK_NOPE:QK_HEAD].set(
        jnp.broadcast_to(jnp.eye(QK_ROPE, dtype=F32)[:, None, :], (QK_ROPE, H, QK_ROPE)))
    wk = jnp.concatenate([wk_lat, place], axis=0).reshape(KV_LORA + LANES, H * HEAD_PAD)
    wv = wkv[:, :, QK_NOPE:].reshape(KV_LORA, H * V_HEAD)
    gq = jnp.pad(p['q_norm'], (0, HEAD_PAD - QK_HEAD)).reshape(1, HEAD_PAD)
    gk = jnp.pad(p['k_norm'], (0, HEAD_PAD - QK_HEAD)).reshape(1, HEAD_PAD)
    bf = lambda w: w.astype(BF16)
    return (bf(wq), bf(_rot_cols(wq)), bf(wk), bf(_rot_cols(wk)), bf(wv), gq, _swap_halves(gq), gk, _swap_halves(gk))


def _mla_tables(n_ctx, T):
    cos, sin = _rope_tables(T)
    c = jnp.ones((n_ctx + T, HEAD_PAD), F32).at[n_ctx:, X1].set(cos).at[n_ctx:, X2].set(cos)
    s = jnp.zeros((n_ctx + T, HEAD_PAD), F32).at[n_ctx:, X1].set(sin).at[n_ctx:, X2].set(sin)
    return c, s


def _mla_prep_kernel(x_ref, c_ref, s_ref, qlg_ref, kvg_ref, wq_ref, wqr_ref, wk_ref, wkr_ref, wv_ref,
                     gq_ref, gqp_ref, gk_ref, gkp_ref, q_ref, k_ref, v_ref):
    x = x_ref[0].astype(F32)
    ql = _rms(x[:, :Q_LORA], qlg_ref[...])
    kvl = _rms(x[:, Q_LORA:Q_LORA + KV_LORA], kvg_ref[...])
    k_in = jnp.concatenate([kvl, x[:, Q_LORA + KV_LORA:]], axis=1)
    cos, sin = c_ref[...], s_ref[...]

    def finish(raw, partner, g, g_swapped, scale, o_ref):
        gc, gs = g * cos, g_swapped * sin
        for h in range(MLA_HEADS):
            sl = slice(h * HEAD_PAD, (h + 1) * HEAD_PAD)
            rh = raw[:, sl]
            inv = lax.rsqrt(jnp.sum(rh * rh, axis=-1, keepdims=True) * (1.0 / QK_HEAD) + EPS) * scale
            o_ref[0, :, sl] = ((rh * gc + partner[:, sl] * gs) * inv).astype(o_ref.dtype)

    finish(_dot(ql, wq_ref[...]), _dot(ql, wqr_ref[...]), gq_ref[...], gqp_ref[...], ATTN_SCALE, q_ref)
    finish(_dot(k_in, wk_ref[...]), _dot(k_in, wkr_ref[...]), gk_ref[...], gkp_ref[...], 1.0, k_ref)
    v_ref[0] = _dot(kvl, wv_ref[...]).astype(v_ref.dtype)


def _mla_prep(proj, p, n_ctx, T):
    B, Tt, _ = proj.shape
    H = MLA_HEADS
    ctx_tiles = n_ctx // ROW_TILE
    mla_blk = (PROJ_RW + PROJ_GATE) // PROJ_MLA
    weights = _mla_weights(p)
    cos, sin = _mla_tables(n_ctx, T)
    const = lambda b, i: (0, 0)
    row = lambda b, i: (b, i, 0)
    tab = pl.BlockSpec((ROW_TILE, HEAD_PAD), lambda b, i: (i, 0))
    full = lambda a: pl.BlockSpec(a.shape, const)
    return pl.pallas_call(
        _mla_prep_kernel,
        grid=(B, Tt // ROW_TILE),
        in_specs=[pl.BlockSpec((1, ROW_TILE, PROJ_MLA), lambda b, i: (b, i, mla_blk)), tab, tab,
                  pl.BlockSpec((1, Q_LORA), const), pl.BlockSpec((1, KV_LORA), const)]
                 + [full(w) for w in weights],
        out_specs=[
            pl.BlockSpec((1, ROW_TILE, H * HEAD_PAD), lambda b, i: (b, jnp.maximum(i - ctx_tiles, 0), 0)),
            pl.BlockSpec((1, ROW_TILE, H * HEAD_PAD), row),
            pl.BlockSpec((1, ROW_TILE, H * V_HEAD), row),
        ],
        out_shape=[
            jax.ShapeDtypeStruct((B, T, H * HEAD_PAD), BF16),
            jax.ShapeDtypeStruct((B, Tt, H * HEAD_PAD), BF16),
            jax.ShapeDtypeStruct((B, Tt, H * V_HEAD), BF16),
        ],
        compiler_params=pltpu.CompilerParams(
            dimension_semantics=("arbitrary", "arbitrary"), vmem_limit_bytes=VMEM_LIMIT),
        name="mla_prep",
    )(proj, cos, sin, p['q_lat_norm'].reshape(1, Q_LORA), p['kv_lat_norm'].reshape(1, KV_LORA), *weights)


ATTN_Q_TILE = 2048
HEADS_PER_STEP = LANES // V_HEAD


ATTN_ROW_SPLIT = 16


def _attn_kernel(q_ref, k_ref, v_ref, o_ref):
    v2 = v_ref[0]
    rows = q_ref.shape[1] // ATTN_ROW_SPLIT
    work = [(hh, rs) for rs in range(ATTN_ROW_SPLIT) for hh in range(HEADS_PER_STEP)]

    def scores(hh, rs):
        sl = slice(hh * HEAD_PAD, (hh + 1) * HEAD_PAD)
        return _dot_t(q_ref[0, rs * rows:(rs + 1) * rows, sl], k_ref[0, :, sl])

    outs = {}
    s_next = scores(*work[0])
    for i, (hh, rs) in enumerate(work):
        s = s_next
        if i + 1 < len(work):
            s_next = scores(*work[i + 1])
        e = jnp.exp(s - jnp.max(s, axis=-1, keepdims=True))
        outs[hh, rs] = _dot(e, v2) / jnp.sum(e, axis=-1, keepdims=True)
    lane = lax.broadcasted_iota(jnp.int32, (rows, LANES), 1)
    for rs in range(ATTN_ROW_SPLIT):
        o_ref[0, rs * rows:(rs + 1) * rows] = jnp.where(lane < V_HEAD, outs[0, rs], outs[1, rs])


def _attention(q, k, v):
    B, T, _ = q.shape
    Kt = k.shape[1]
    assert T % ATTN_Q_TILE == 0 and ATTN_Q_TILE % (ATTN_ROW_SPLIT * SUBLANES) == 0
    hp = MLA_HEADS // HEADS_PER_STEP
    qw = HEADS_PER_STEP * HEAD_PAD
    return pl.pallas_call(
        _attn_kernel,
        grid=(B, hp, T // ATTN_Q_TILE),
        in_specs=[
            pl.BlockSpec((1, ATTN_Q_TILE, qw), lambda b, h, i: (b, i, h)),
            pl.BlockSpec((1, Kt, qw), lambda b, h, i: (b, 0, h)),
            pl.BlockSpec((1, Kt, LANES), lambda b, h, i: (b, 0, h)),
        ],
        out_specs=pl.BlockSpec((1, ATTN_Q_TILE, LANES), lambda b, h, i: (b, i, h)),
        out_shape=jax.ShapeDtypeStruct((B, T, MLA_HEADS * V_HEAD), F32),
        compiler_params=pltpu.CompilerParams(
            dimension_semantics=("arbitrary", "arbitrary", "arbitrary"), vmem_limit_bytes=VMEM_LIMIT),
        name="attention",
    )(q, k, v)


HALO = 16
LORA_W = LANES
GATE_W = PROJ_RW - 3 * RW_DIM - 2 * LORA_W


def _head_ones(width, head):
    i = np.arange(width) // head
    return jnp.asarray(i[:, None] == i[None, :], BF16)


def _head_sum(x, ones_bd):
    hi = x.astype(BF16)
    lo = (x - hi.astype(F32)).astype(BF16)
    return (jnp.dot(hi, ones_bd, preferred_element_type=F32)
            + jnp.dot(lo, ones_bd, preferred_element_type=F32))


def _dot3(a, b_hi, b_lo):
    hi = a.astype(BF16)
    lo = (a - hi.astype(F32)).astype(BF16)
    d = lambda u, w: jnp.dot(u, w, preferred_element_type=F32)
    return d(hi, b_hi) + (d(hi, b_lo) + d(lo, b_hi))


def _rwkv_prep_kernel(x_ref, prev_ref, next_ref, conv_ref, kkg_ref, ka_ref, rk_ref, w0_ref, a0_ref,
                      w2h_ref, w2l_ref, a2_ref, g2_ref, ones_ref,
                      r_ref, v_ref, kk_ref, lw_ref, bk_ref, kd_ref, bonus_ref, gg_ref, *, ctx_tiles, n_tiles):
    i = pl.program_id(1)
    x = x_ref[0].astype(F32)
    tm, W = x.shape
    C = RW_DIM
    first = (i == 0) | (i == ctx_tiles)
    last = (i == ctx_tiles - 1) | (i == n_tiles - 1)
    prev_row = jnp.where(first, 0.0, prev_ref[0, HALO - 1:HALO].astype(F32))
    next_row = jnp.where(last, 0.0, next_ref[0, 0:1].astype(F32))
    row = lax.broadcasted_iota(jnp.int32, (tm, W), 0)
    x_dn = jnp.where(row == 0, prev_row, pltpu.roll(x, 1, 0))
    x_up = jnp.where(row == tm - 1, next_row, pltpu.roll(x, tm - 1, 0))
    xc = x_dn * conv_ref[0:1] + x * conv_ref[1:2] + x_up * conv_ref[2:3]

    r, k, v = xc[:, :C], xc[:, C:2 * C], xc[:, 2 * C:3 * C]
    lora_w = jnp.tanh(xc[:, 3 * C:3 * C + LORA_W])
    lora_a = xc[:, 3 * C + LORA_W:3 * C + 2 * LORA_W]
    lg = xc[:, 3 * C + 2 * LORA_W:]
    ones_bd = ones_ref[...]
    kq = k * kkg_ref[...]
    kk = kq * lax.rsqrt(_head_sum(kq * kq, ones_bd) + 1e-12)
    r_ref[0], v_ref[0], kk_ref[0] = r, v, kk

    k_sum = jnp.zeros_like(k)
    for d in range(2):
        z = w0_ref[d:d + 1] + _dot3(lora_w, w2h_ref[d], w2l_ref[d])
        softplus_neg = jnp.maximum(-z, 0.0) + jnp.log(1.0 + jnp.exp(-jnp.abs(z)))
        lw_ref[d, 0] = -jnp.exp(-softplus_neg - 0.5)
        a = jax.nn.sigmoid(a0_ref[d:d + 1] + _dot(lora_a, a2_ref[d]))
        kd = k * (1.0 + (a - 1.0) * ka_ref[...])
        bk_ref[d, 0] = kk * a
        kd_ref[d, 0] = kd
        k_sum = k_sum + kd
    bonus_ref[0] = _head_sum(r * k_sum * rk_ref[...], ones_bd) * v
    gg_ref[0] = _dot(jax.nn.sigmoid(lg), g2_ref[...])


def _rwkv_prep(proj, p, n_ctx):
    B, Tt, _ = proj.shape
    C = RW_DIM
    nt = Tt // ROW_TILE
    hb = ROW_TILE // HALO
    pad_cols = lambda w: jnp.pad(w, ((0, 0), (0, PROJ_RW - w.shape[1])))
    w2 = jnp.stack([jnp.pad(p['decay_w2'][0], ((0, LORA_W - DECAY_LORA), (0, 0))),
                    jnp.pad(p['decay_w2'][1], ((DECAY_LORA, 0), (0, 0)))])
    w2h = w2.astype(BF16)
    w2l = (w2 - w2h.astype(F32)).astype(BF16)
    a2 = jnp.stack([jnp.pad(p['aicl_a2'][0], ((0, LORA_W - AICL_LORA), (0, 0))),
                    jnp.pad(p['aicl_a2'][1], ((AICL_LORA, 0), (0, 0)))]).astype(BF16)
    g2 = jnp.pad(p['gate_g2'], ((0, GATE_W - GATE_LORA), (0, 0))).astype(BF16)
    row = lambda b, i: (b, i, 0)
    drow = lambda b, i: (0, b, i, 0)
    const2 = lambda b, i: (0, 0)
    const3 = lambda b, i: (0, 0, 0)
    vec = pl.BlockSpec((1, C), const2)
    out_row = pl.BlockSpec((1, ROW_TILE, C), row)
    out_dir = pl.BlockSpec((2, 1, ROW_TILE, C), drow)
    sds = jax.ShapeDtypeStruct((B, Tt, C), F32)
    sds2 = jax.ShapeDtypeStruct((2, B, Tt, C), F32)
    ctx_tiles = n_ctx // ROW_TILE
    out_lat = pl.BlockSpec((1, ROW_TILE, C), lambda b, i: (b, jnp.maximum(i - ctx_tiles, 0), 0))
    sds_lat = jax.ShapeDtypeStruct((B, Tt - n_ctx, C), F32)
    return pl.pallas_call(
        functools.partial(_rwkv_prep_kernel, ctx_tiles=n_ctx // ROW_TILE, n_tiles=nt),
        grid=(B, nt),
        in_specs=[
            pl.BlockSpec((1, ROW_TILE, PROJ_RW), row),
            pl.BlockSpec((1, HALO, PROJ_RW), lambda b, i: (b, jnp.maximum(i * hb - 1, 0), 0)),
            pl.BlockSpec((1, HALO, PROJ_RW), lambda b, i: (b, jnp.minimum((i + 1) * hb, nt * hb - 1), 0)),
            pl.BlockSpec((3, PROJ_RW), const2),
            vec, vec, vec,
            pl.BlockSpec((2, C), const2),
            pl.BlockSpec((2, C), const2),
            pl.BlockSpec((2, LORA_W, C), const3),
            pl.BlockSpec((2, LORA_W, C), const3),
            pl.BlockSpec((2, LORA_W, C), const3),
            pl.BlockSpec((GATE_W, C), const2),
            pl.BlockSpec((C, C), const2),
        ],
        out_specs=[out_row, out_row, out_row, out_dir, out_dir, out_dir, out_lat, out_lat],
        out_shape=[sds, sds, sds, sds2, sds2, sds2, sds_lat, sds_lat],
        compiler_params=pltpu.CompilerParams(
            dimension_semantics=("arbitrary", "arbitrary"), vmem_limit_bytes=VMEM_LIMIT),
        name="rwkv_prep",
    )(proj, proj, proj, pad_cols(p['shift_conv']), p['k_k'].reshape(1, C), p['k_a'].reshape(1, C),
      p['r_k'].reshape(1, C), p['decay_w0'], p['aicl_a0'], w2h, w2l, a2, g2, _head_ones(C, RW_HEAD))


MERGE_PIECES = 2


def _merge_kernel(att_ref, yf_ref, yb_ref, bonus_ref, gg_ref, *refs):
    gate_refs, (x_ref, ga_ref, gnw_ref, gnb_ref, ones_ref, wm_ref, wr_ref, wo_ref, o_ref) = (
        refs[:MERGE_PIECES], refs[MERGE_PIECES:])
    D = x_ref.shape[-1]
    ones_bd = ones_ref[...]
    y = yf_ref[0] + yb_ref[0]
    yc = y - _head_sum(y, ones_bd) * (1.0 / RW_HEAD)
    var = _head_sum(yc * yc, ones_bd) * (1.0 / RW_HEAD)
    y_n = yc * lax.rsqrt(var + GN_EPS) * gnw_ref[...] + gnb_ref[...]
    rw = (y_n + bonus_ref[0]) * gg_ref[0]
    g = jax.nn.sigmoid(jnp.concatenate([r[0] for r in gate_refs], axis=0).astype(F32))
    mix = g[:, :D] * _dot(att_ref[0], wm_ref[...]) + g[:, D:] * _dot(rw, wr_ref[...])
    o_ref[0] = x_ref[0] + ga_ref[0] * _dot(mix, wo_ref[...])


def _merge(att, y_f, y_b, bonus, gg, proj, x, g_a, p, n_ctx):
    B, T, D = x.shape
    C = RW_DIM
    rows = MERGE_PIECES * ROW_TILE
    ctx_tiles = n_ctx // ROW_TILE
    gate_blk = PROJ_RW // PROJ_GATE
    row = lambda b, i: (b, i, 0)
    const = lambda b, i: (0, 0)
    gate_piece = lambda n: pl.BlockSpec(
        (1, ROW_TILE, PROJ_GATE), lambda b, i, n=n: (b, MERGE_PIECES * i + n + ctx_tiles, gate_blk))
    return pl.pallas_call(
        _merge_kernel,
        grid=(B, T // rows),
        in_specs=[
            pl.BlockSpec((1, rows, att.shape[-1]), row),
            pl.BlockSpec((1, rows, C), row),
            pl.BlockSpec((1, rows, C), row),
            pl.BlockSpec((1, rows, C), row),
            pl.BlockSpec((1, rows, C), row),
        ] + [gate_piece(n) for n in range(MERGE_PIECES)] + [
            pl.BlockSpec((1, rows, D), row),
            pl.BlockSpec((1, 1, D), lambda b, i: (b, 0, 0)),
            pl.BlockSpec((1, C), const),
            pl.BlockSpec((1, C), const),
            pl.BlockSpec((C, C), const),
            pl.BlockSpec(p['w_o_mla'].shape, const),
            pl.BlockSpec(p['w_o_rwkv'].shape, const),
            pl.BlockSpec(p['w_out'].shape, const),
        ],
        out_specs=pl.BlockSpec((1, rows, D), row),
        out_shape=jax.ShapeDtypeStruct((B, T, D), F32),
        compiler_params=pltpu.CompilerParams(
            dimension_semantics=("arbitrary", "arbitrary"), vmem_limit_bytes=VMEM_LIMIT),
        name="merge",
    )(att, y_f, y_b, bonus, gg, *([proj] * MERGE_PIECES), x, g_a[:, None, :], p['gn_w'].reshape(1, C),
      p['gn_b'].reshape(1, C), _head_ones(C, RW_HEAD), p['w_o_mla'].astype(BF16), p['w_o_rwkv'].astype(BF16),
      p['w_out'].astype(BF16))


PACK_CHUNKS = 4
TILE_ROWS = 8
MOE_BLOCK = 512


ROUTER_TILE = 1024
NEG_INF = float("-inf")
HI_MASK = 0xFFFF0000


def _pack_pair(lo, hi):
    lo_b = lax.bitcast_convert_type(lo.astype(BF16).astype(F32), jnp.uint32)
    hi_b = lax.bitcast_convert_type(hi.astype(BF16).astype(F32), jnp.uint32)
    return (lo_b >> 16) | (hi_b & jnp.uint32(HI_MASK))


def _unpack_pair(w):
    lo = lax.bitcast_convert_type(w << 16, F32)
    hi = lax.bitcast_convert_type(w & jnp.uint32(HI_MASK), F32)
    return lo.astype(BF16), hi.astype(BF16)


def _row_max(x):
    return jnp.max(x, axis=-1, keepdims=True)


def _first_index_of(x, value, lane_f):
    return jnp.min(jnp.where(x == value, lane_f, float(x.shape[-1])), axis=-1, keepdims=True)


def _router_kernel(x_ref, gain_ref, sh_ref, sc_ref, wh_ref, wm_ref, wl_ref, bias_ref,
                   u_ref, ut_ref, ids_ref, gates_ref, ranks_ref, counts_ref, carry_ref):
    @pl.when((pl.program_id(0) == 0) & (pl.program_id(1) == 0))
    def _():
        carry_ref[...] = jnp.zeros_like(carry_ref)

    x = x_ref[0]
    tm, D = x.shape
    E = bias_ref.shape[-1]
    u = x * lax.rsqrt(jnp.mean(x * x, axis=-1, keepdims=True) + EPS) * gain_ref[...]
    u = u * (1.0 + sc_ref[0]) + sh_ref[0]
    packed = _pack_pair(u[:, :D // 2], u[:, D // 2:])
    u_ref[0] = packed
    for c in range(PACK_CHUNKS):
        ut_ref[:, c, :] = packed[:, c * LANES:(c + 1) * LANES]

    uh, um, ul = _split3(u)
    wh, wm, wl = wh_ref[...], wm_ref[...], wl_ref[...]
    d = lambda a, b: jnp.dot(a, b, preferred_element_type=F32)
    logits = d(uh, wh) + (d(uh, wm) + d(um, wh)) + (d(uh, wl) + d(um, wm) + d(ul, wh))
    scores = jax.nn.sigmoid(logits)
    sel = scores + bias_ref[...]

    lane_i = lax.broadcasted_iota(jnp.int32, (tm, E), 1)
    lane_f = lane_i.astype(F32)
    out_f = lax.broadcasted_iota(jnp.int32, (tm, LANES), 1).astype(F32)
    per_group = E // N_GROUPS
    grp_f = jnp.floor(lane_f * (1.0 / per_group))

    gs = jnp.full((tm, LANES), NEG_INF, F32)
    for g in range(N_GROUPS):
        sg = jnp.where(lane_i >= g * per_group, jnp.where(lane_i < (g + 1) * per_group, sel, NEG_INF), NEG_INF)
        m1 = _row_max(sg)
        i1 = _first_index_of(sg, m1, lane_f)
        m2 = _row_max(jnp.where(lane_f == i1, NEG_INF, sg))
        gs = jnp.where(out_f == g, m1 + m2, gs)

    allow = jnp.zeros((tm, E), F32)
    for _ in range(TOPK_GROUPS):
        m = _row_max(gs)
        i = _first_index_of(gs, m, out_f)
        gs = jnp.where(out_f == i, NEG_INF, gs)
        allow = jnp.where(grp_f == i, 1.0, allow)
    selm = jnp.where(allow > 0.0, sel, NEG_INF)

    ids = jnp.zeros((tm, LANES), F32)
    gts = jnp.zeros((tm, LANES), F32)
    member = jnp.zeros((tm, E), F32)
    idx_cols = []
    gsum = jnp.zeros((tm, 1), F32)
    for k in range(TOP_K):
        m = _row_max(selm)
        i = _first_index_of(selm, m, lane_f)
        hit = lane_f == i
        gk = jnp.sum(jnp.where(hit, scores, 0.0), axis=-1, keepdims=True)
        selm = jnp.where(hit, NEG_INF, selm)
        member = jnp.where(hit, 1.0, member)
        ids = jnp.where(out_f == k, i, ids)
        gts = jnp.where(out_f == k, gk, gts)
        idx_cols.append(i)
        gsum = gsum + gk
    gts = gts / gsum * ROUTED_SCALE

    r2 = lax.broadcasted_iota(jnp.int32, (tm, tm), 0)
    c2 = lax.broadcasted_iota(jnp.int32, (tm, tm), 1)
    before = jnp.where(r2 > c2, 1.0, 0.0).astype(BF16)
    mem_b = member.astype(BF16)
    carry = carry_ref[...]
    pos = carry + jnp.dot(before, mem_b, preferred_element_type=F32)
    rk = jnp.zeros((tm, LANES), F32)
    for k in range(TOP_K):
        rk = jnp.where(out_f == k, jnp.sum(jnp.where(lane_f == idx_cols[k], pos, 0.0), axis=-1, keepdims=True), rk)
    colsum = jnp.dot(jnp.ones((8, tm), BF16), mem_b, preferred_element_type=F32)[0:1]
    carry_ref[...] = carry + colsum
    counts_ref[...] = carry + colsum
    ids_ref[0] = ids.astype(jnp.int32)
    ranks_ref[0] = rk.astype(jnp.int32)
    gates_ref[0] = gts


def _router(x1, sh_m, sc_m, p):
    B, T, D = x1.shape
    E = N_EXPERTS
    wh, wm, wl = _split3(p['router_w'])
    row = lambda b, i: (b, i, 0)
    const = lambda b, i: (0, 0)
    vec = lambda b, i: (b, 0, 0)
    lane_out = lambda dt: jax.ShapeDtypeStruct((B, T, LANES), dt)
    return pl.pallas_call(
        _router_kernel,
        grid=(B, T // ROUTER_TILE),
        in_specs=[
            pl.BlockSpec((1, ROUTER_TILE, D), row),
            pl.BlockSpec((1, D), const),
            pl.BlockSpec((1, 1, D), vec),
            pl.BlockSpec((1, 1, D), vec),
            pl.BlockSpec((D, E), const),
            pl.BlockSpec((D, E), const),
            pl.BlockSpec((D, E), const),
            pl.BlockSpec((1, E), const),
        ],
        out_specs=[
            pl.BlockSpec((1, ROUTER_TILE, D // 2), row),
            pl.BlockSpec((ROUTER_TILE, PACK_CHUNKS, LANES), lambda b, i: (b * (T // ROUTER_TILE) + i, 0, 0)),
            pl.BlockSpec((1, ROUTER_TILE, LANES), row),
            pl.BlockSpec((1, ROUTER_TILE, LANES), row),
            pl.BlockSpec((1, ROUTER_TILE, LANES), row),
            pl.BlockSpec((1, E), const),
        ],
        out_shape=[
            jax.ShapeDtypeStruct((B, T, D // 2), jnp.uint32),
            jax.ShapeDtypeStruct((B * T, PACK_CHUNKS, LANES), jnp.uint32),
            lane_out(jnp.int32), lane_out(F32), lane_out(jnp.int32),
            jax.ShapeDtypeStruct((1, E), F32),
        ],
        scratch_shapes=[pltpu.VMEM((1, E), F32)],
        compiler_params=pltpu.CompilerParams(
            dimension_semantics=("arbitrary", "arbitrary"), vmem_limit_bytes=VMEM_LIMIT),
        name="router",
    )(x1, p['norm_ffn'].reshape(1, D), sh_m[:, None, :], sc_m[:, None, :], wh, wm, wl,
      p['router_bias'].reshape(1, E))


def _slot_kernel(ids_ref, ranks_ref, base_ref, o_ref):
    ids = ids_ref[...].astype(F32)
    tm = ids.shape[0]
    E = base_ref.shape[-1]
    lane_e = lax.broadcasted_iota(jnp.int32, (tm, E), 1).astype(F32)
    out_lane = lax.broadcasted_iota(jnp.int32, (tm, LANES), 1)
    first = jnp.zeros((tm, LANES), F32)
    for k in range(TOP_K):
        fk = jnp.sum(jnp.where(lane_e == ids[:, k:k + 1], base_ref[...], 0.0), axis=-1, keepdims=True)
        first = jnp.where(out_lane == k, fk, first)
    o_ref[...] = first.astype(jnp.int32) + ranks_ref[...]


def _slots(ids, ranks, base):
    n_tok = ids.shape[0]
    E = base.shape[-1]
    row = pl.BlockSpec((ROUTER_TILE, LANES), lambda i: (i, 0))
    return pl.pallas_call(
        _slot_kernel,
        grid=(n_tok // ROUTER_TILE,),
        in_specs=[row, row, pl.BlockSpec((1, E), lambda i: (0, 0))],
        out_specs=row,
        out_shape=jax.ShapeDtypeStruct((n_tok, LANES), jnp.int32),
        compiler_params=pltpu.CompilerParams(dimension_semantics=("arbitrary",)),
        name="slots",
    )(ids, ranks, base)


DMA_PRIORITIES = 2


def _dispatch_kernel(pad_ref, dest_ref, u_ref, slots_ref, zero_ref, sem):
    tm = u_ref.shape[0]
    n_experts = pad_ref.shape[0]

    @pl.when(pl.program_id(0) == 0)
    def _():
        zero_ref[...] = jnp.zeros_like(zero_ref)

        def fill(e, carry):
            pltpu.make_async_copy(zero_ref, slots_ref.at[pl.ds(pad_ref[e], MOE_BLOCK)], sem).start()
            return carry

        lax.fori_loop(0, n_experts, fill, 0)

        def filled(e, carry):
            pltpu.make_async_copy(zero_ref, slots_ref.at[pl.ds(0, MOE_BLOCK)], sem).wait()
            return carry

        lax.fori_loop(0, n_experts, filled, 0)

    def issue(r, carry):
        for k in range(TOP_K):
            pltpu.make_async_copy(u_ref.at[r], slots_ref.at[dest_ref[0, 0, r * TOP_K + k]], sem).start(
                priority=k % DMA_PRIORITIES)
        return carry

    lax.fori_loop(0, tm, issue, 0)
    for k in range(TOP_K):
        pltpu.make_async_copy(u_ref, slots_ref.at[pl.ds(0, tm)], sem).wait()


def _dispatch(pad_start, dest, u_tiles, n_slots):
    n_tok = u_tiles.shape[0]
    tile = u_tiles.shape[1:]
    nt = n_tok // ROW_TILE
    grid_spec = pltpu.PrefetchScalarGridSpec(
        num_scalar_prefetch=1,
        grid=(nt,),
        in_specs=[
            pl.BlockSpec((1, 1, ROW_TILE * TOP_K), lambda i, pad: (i, 0, 0), memory_space=pltpu.SMEM),
            pl.BlockSpec((ROW_TILE,) + tile, lambda i, pad: (i, 0, 0)),
        ],
        out_specs=pl.BlockSpec(memory_space=pl.ANY),
        scratch_shapes=[pltpu.VMEM((MOE_BLOCK,) + tile, jnp.uint32), pltpu.SemaphoreType.DMA(())],
    )
    return pl.pallas_call(
        _dispatch_kernel,
        grid_spec=grid_spec,
        out_shape=jax.ShapeDtypeStruct((n_slots + MOE_BLOCK,) + tile, jnp.uint32),
        compiler_params=pltpu.CompilerParams(
            dimension_semantics=("arbitrary",), vmem_limit_bytes=VMEM_LIMIT),
        name="dispatch",
    )(pad_start, dest, u_tiles)


WEIGHT_SLOTS = 2


def _expert_weight_copies(hbm_refs, stage_refs, expert, slot, sems):
    return [pltpu.make_async_copy(w.at[expert], s.at[slot], sems.at[slot, i])
            for i, (w, s) in enumerate(zip(hbm_refs, stage_refs))]


def _moe_ffn_kernel(be_ref, first_ref, slot_ref, next_ref, nu_ref, x_ref, w1_ref, w3_ref, w2_ref, o_ref,
                    s1_ref, s3_ref, s2_ref, w13_ref, w2b_ref, sems):
    j = pl.program_id(0)
    F = w1_ref.shape[-1]
    half = PACK_CHUNKS * LANES
    copies = functools.partial(_expert_weight_copies, (w1_ref, w3_ref, w2_ref), (s1_ref, s3_ref, s2_ref), sems=sems)

    @pl.when(j == 0)
    def _():
        for c in copies(be_ref[0], 0):
            c.start()

    @pl.when(first_ref[j] == 1)
    def _():
        slot = slot_ref[j]
        for c in copies(be_ref[j], slot):
            c.wait()

        @pl.when(next_ref[j] >= 0)
        def _():
            for c in copies(next_ref[j], 1 - slot):
                c.start()

        w13_ref[:, :F] = s1_ref[slot].astype(BF16)
        w13_ref[:, F:] = s3_ref[slot].astype(BF16)
        w2b_ref[...] = s2_ref[slot].astype(BF16)

    @pl.when(j < nu_ref[0])
    def _():
        lo, hi = _unpack_pair(jnp.concatenate([x_ref[:, c, :] for c in range(PACK_CHUNKS)], axis=1))
        h = (jnp.dot(lo, w13_ref[:half], preferred_element_type=F32)
             + jnp.dot(hi, w13_ref[half:], preferred_element_type=F32))
        h1, h3 = h[:, :F], h[:, F:]
        y = _dot(h1 * jax.nn.sigmoid(h1) * h3, w2b_ref[...])
        o_ref[...] = pltpu.einshape("s(cl)->(sc)l", y, c=TILE_ROWS)

    @pl.when(j >= nu_ref[0])
    def _():
        o_ref[...] = jnp.zeros_like(o_ref)


def _moe_ffn(block_e, n_used, x_slots, w1, w3, w2):
    E, D, F = w1.shape
    nblk = block_e.shape[0]
    slots = nblk * MOE_BLOCK
    idx = jnp.arange(nblk)
    first = jnp.concatenate([jnp.ones((1,), bool), block_e[1:] != block_e[:-1]])
    slot = ((jnp.cumsum(first) - 1) % WEIGHT_SLOTS).astype(jnp.int32)
    later_first = lax.cummin(jnp.where(first, idx, nblk)[::-1])[::-1]
    next_first = jnp.concatenate([later_first[1:], jnp.full((1,), nblk)])
    next_e = jnp.where(next_first < nblk, block_e[jnp.minimum(next_first, nblk - 1)], -1).astype(jnp.int32)
    blk = lambda j, be, first, slot, nxt, nu: (jnp.minimum(j, nu[0] - 1), 0, 0)
    grid_spec = pltpu.PrefetchScalarGridSpec(
        num_scalar_prefetch=5,
        grid=(nblk,),
        in_specs=[
            pl.BlockSpec((MOE_BLOCK, PACK_CHUNKS, LANES), blk),
            pl.BlockSpec(memory_space=pl.ANY),
            pl.BlockSpec(memory_space=pl.ANY),
            pl.BlockSpec(memory_space=pl.ANY),
        ],
        out_specs=pl.BlockSpec((MOE_BLOCK * TILE_ROWS, LANES), lambda j, *_: (j, 0)),
        scratch_shapes=[pltpu.VMEM((WEIGHT_SLOTS, D, F), F32), pltpu.VMEM((WEIGHT_SLOTS, D, F), F32),
                        pltpu.VMEM((WEIGHT_SLOTS, F, D), F32),
                        pltpu.VMEM((D, 2 * F), BF16), pltpu.VMEM((F, D), BF16),
                        pltpu.SemaphoreType.DMA((WEIGHT_SLOTS, 3))],
    )
    return pl.pallas_call(
        _moe_ffn_kernel,
        grid_spec=grid_spec,
        out_shape=jax.ShapeDtypeStruct((slots * TILE_ROWS, LANES), F32),
        compiler_params=pltpu.CompilerParams(
            dimension_semantics=("arbitrary",), vmem_limit_bytes=VMEM_LIMIT),
        name="moe_ffn",
    )(block_e, first.astype(jnp.int32), slot, next_e, n_used, x_slots, w1, w3, w2)


COMBINE_ROWS = 32


def _combine_kernel(dest_ref, gates_ref, u_ref, x_ref, gm_ref, w1_ref, w3_ref, w2_ref, ys_ref, o_ref, buf_ref, sem):
    tm, half = u_ref.shape

    def tile(i):
        return pl.ds(pl.multiple_of(i * TILE_ROWS, TILE_ROWS), TILE_ROWS)

    def issue(r, carry):
        for k in range(TOP_K):
            pltpu.make_async_copy(ys_ref.at[tile(dest_ref[0, 0, r * TOP_K + k])], buf_ref.at[k, tile(r)], sem).start(
                priority=k % DMA_PRIORITIES)
        return carry

    lax.fori_loop(0, tm, issue, 0)

    ulo, uhi = _unpack_pair(u_ref[...])
    both = lambda w_ref: (jnp.dot(ulo, w_ref[:half], preferred_element_type=F32)
                          + jnp.dot(uhi, w_ref[half:], preferred_element_type=F32))
    h1, h3 = both(w1_ref), both(w3_ref)
    o_ref[...] = x_ref[...] + gm_ref[0] * _dot(h1 * jax.nn.sigmoid(h1) * h3, w2_ref[...])

    for k in range(TOP_K):
        pltpu.make_async_copy(ys_ref.at[pl.ds(0, tm * TILE_ROWS)], buf_ref.at[k], sem).wait()

    gm = gm_ref[0]
    for rb in range(tm // COMBINE_ROWS):
        rows = slice(rb * COMBINE_ROWS, (rb + 1) * COMBINE_ROWS)
        gates = gates_ref[rows, :]
        for c in range(TILE_ROWS):
            acc = jnp.zeros((COMBINE_ROWS, LANES), F32)
            for k in range(TOP_K):
                acc = acc + gates[:, k:k + 1] * buf_ref[
                    k, pl.ds(rb * COMBINE_ROWS * TILE_ROWS + c, COMBINE_ROWS, stride=TILE_ROWS), :]
            cols = slice(c * LANES, (c + 1) * LANES)
            o_ref[rows, cols] = o_ref[rows, cols] + gm[:, cols] * acc


def _combine(dest, gates, u_rows, x1, g_m, y_slots, p):
    B, T, D = x1.shape
    n_tok = B * T
    W = u_rows.shape[1]
    tiles_per_batch = T // ROW_TILE
    row = lambda i: (i, 0)
    const = lambda i: (0, 0)
    out = pl.pallas_call(
        _combine_kernel,
        grid=(n_tok // ROW_TILE,),
        in_specs=[
            pl.BlockSpec((1, 1, ROW_TILE * TOP_K), lambda i: (i, 0, 0), memory_space=pltpu.SMEM),
            pl.BlockSpec((ROW_TILE, LANES), row),
            pl.BlockSpec((ROW_TILE, W), row),
            pl.BlockSpec((ROW_TILE, D), row),
            pl.BlockSpec((1, 1, D), lambda i: (i // tiles_per_batch, 0, 0)),
            pl.BlockSpec(p['shared_w1'].shape, const),
            pl.BlockSpec(p['shared_w3'].shape, const),
            pl.BlockSpec(p['shared_w2'].shape, const),
            pl.BlockSpec(memory_space=pl.ANY),
        ],
        out_specs=pl.BlockSpec((ROW_TILE, D), row),
        out_shape=jax.ShapeDtypeStruct((n_tok, D), F32),
        scratch_shapes=[pltpu.VMEM((TOP_K, ROW_TILE * TILE_ROWS, LANES), F32), pltpu.SemaphoreType.DMA(())],
        compiler_params=pltpu.CompilerParams(
            dimension_semantics=("arbitrary",), vmem_limit_bytes=VMEM_LIMIT),
        name="combine",
    )(dest, gates.reshape(n_tok, LANES), u_rows, x1.reshape(n_tok, D), g_m[:, None, :],
      p['shared_w1'].astype(BF16), p['shared_w3'].astype(BF16), p['shared_w2'].astype(BF16), y_slots)
    return out.reshape(B, T, D)


def _moe(x1, sh_m, sc_m, g_m, p):
    B, T, D = x1.shape
    n_tok = B * T
    u_packed, u_tiles, ids, gates, ranks, counts = _router(x1, sh_m, sc_m, p)

    counts = counts[0].astype(jnp.int32)
    padded = (counts + MOE_BLOCK - 1) // MOE_BLOCK * MOE_BLOCK
    padded_end = jnp.cumsum(padded)
    base = padded_end - padded
    n_blocks = n_tok * TOP_K // MOE_BLOCK + N_EXPERTS
    n_used = (padded_end[-1] // MOE_BLOCK).astype(jnp.int32)
    blk = jnp.minimum(jnp.arange(n_blocks), n_used - 1) * MOE_BLOCK
    block_e = jnp.minimum(jnp.sum(padded_end[None, :] <= blk[:, None], axis=1), N_EXPERTS - 1).astype(jnp.int32)
    dest = _slots(ids.reshape(n_tok, LANES), ranks.reshape(n_tok, LANES), base.astype(F32).reshape(1, N_EXPERTS))
    dest = dest[:, :TOP_K].reshape(n_tok // ROW_TILE, 1, ROW_TILE * TOP_K)

    u_rows = u_packed.reshape(n_tok, D // 2)
    x_slots = _dispatch((base + counts).astype(jnp.int32), dest, u_tiles, n_blocks * MOE_BLOCK)
    y_slots = _moe_ffn(block_e, n_used.reshape(1), x_slots, p['expert_w1'], p['expert_w3'], p['expert_w2'])
    return _combine(dest, gates, u_rows, x1, g_m, y_slots, p)
```

```python
import functools

import jax
import jax.numpy as jnp
import numpy as np
from jax import lax
from jax.experimental import pallas as pl
from jax.experimental.pallas import tpu as pltpu

F32 = jnp.float32
BF16 = jnp.bfloat16

GRID_W = 64
EPS = 1e-6
MLA_HEADS = 8
QK_NOPE = 64
QK_ROPE = 32
QK_HEAD = QK_NOPE + QK_ROPE
V_HEAD = 64
Q_LORA = 256
KV_LORA = 128
ROPE_BASE = 10000.0
ATTN_SCALE = QK_HEAD ** -0.5
RW_HEADS = 8
RW_HEAD = 64
RW_DIM = RW_HEADS * RW_HEAD
DECAY_LORA = 64
AICL_LORA = 64
GATE_LORA = 160
GN_EPS = 64e-5
N_EXPERTS = 256
TOP_K = 8
N_GROUPS = 8
TOPK_GROUPS = 4
ROUTED_SCALE = 2.5

LANES = 128
VMEM_LIMIT = 56 * 1024 * 1024

WKV_CHUNK = 64
WKV_GROUP = LANES
HEADS_PER_GROUP = WKV_GROUP // RW_HEAD
HEAD_PAD = LANES


def _dot(a, b):
    return jnp.dot(a.astype(BF16), b.astype(BF16), preferred_element_type=F32)


def _dot_t(a, b):
    return lax.dot_general(a.astype(BF16), b.astype(BF16), (((1,), (1,)), ((), ())),
                           preferred_element_type=F32)


def _split3(x):
    h = x.astype(BF16)
    r1 = x - h.astype(F32)
    m = r1.astype(BF16)
    lo = (r1 - m.astype(F32)).astype(BF16)
    return h, m, lo


def _dot_hi(a_bf16_exact, x):
    h, m, lo = _split3(x)
    d = lambda y: jnp.dot(a_bf16_exact, y, preferred_element_type=F32)
    return d(h) + d(m) + d(lo)


def _wkv_chunk(r, v, kk, lw, bk, kd, s0, reverse, emit):
    L, G = r.shape
    row = lax.broadcasted_iota(jnp.int32, (L, G), 0)
    lane = lax.broadcasted_iota(jnp.int32, (L, G), 1)
    diff = (lane % L - row) if reverse else (row - lane % L)
    strict = diff > 0
    incl = diff >= 0
    r2 = lax.broadcasted_iota(jnp.int32, (L, L), 0)
    c2 = lax.broadcasted_iota(jnp.int32, (L, L), 1)
    tri = jnp.where(((c2 - r2) if reverse else (r2 - c2)) >= 0, 1.0, 0.0).astype(BF16)
    tri_ones = jnp.concatenate([tri, jnp.ones((L, L), BF16)], axis=0)
    lane_head = lane // RW_HEAD
    rowb = lax.broadcasted_iota(jnp.int32, (G, G), 0) // RW_HEAD
    colb = lax.broadcasted_iota(jnp.int32, (G, G), 1) // RW_HEAD

    def stack(x):
        return jnp.concatenate(
            [jnp.where(lane_head == h, x, 0.0) for h in range(HEADS_PER_GROUP)], axis=0).astype(BF16)

    sums = _dot_hi(tri_ones, lw)
    cum_in = sums[:L]
    g_in = jnp.exp(cum_in)
    g_inv = jnp.exp(-cum_in)
    g_ex = jnp.exp(cum_in - lw)
    g_tot = jnp.exp(sums[L:])
    a_h = -kk * g_ex
    b_h = bk * g_inv
    k_h = kd * g_inv
    r_h = r * g_in
    yield

    gram = _dot_t(jnp.concatenate([a_h, r_h], axis=0),
                  jnp.concatenate([stack(b_h), stack(k_h)], axis=0))
    ab = jnp.where(strict, gram[:L, :G], 0.0)
    ak = jnp.where(strict, gram[:L, G:], 0.0)
    rb = jnp.where(incl, gram[L:, :G], 0.0)
    rk = jnp.where(incl, gram[L:, G:], 0.0)
    yield

    tm = jnp.where(diff == 0, 1.0, 0.0) + ab
    p = ab
    v_st = stack(v)
    akv = _dot(ak, v_st)
    for _ in range(int(np.log2(L)) - 1):
        p = _dot(p, stack(p))
        yield
        tm = tm + _dot(tm, stack(p))
        yield

    wu = _dot(tm, jnp.concatenate([stack(a_h), stack(akv)], axis=1))
    w_t, u_t = wu[:, :G], wu[:, G:]
    yield
    rbwu = _dot(rb, jnp.concatenate([stack(w_t), stack(u_t)], axis=1))
    r_t = r_h + rbwu[:, :G]
    y_t = rbwu[:, G:] + _dot(rk, v_st)
    yield

    s0_st = stack(s0)
    y = _dot_t(r_t, s0_st) + y_t
    u = _dot_t(w_t, s0_st) + u_t
    yield
    uv = jnp.concatenate([u, v], axis=0)
    bkc = jnp.concatenate([b_h, k_h], axis=0)
    upd = lax.dot_general(uv.astype(BF16), bkc.astype(BF16), (((0,), (0,)), ((), ())),
                          preferred_element_type=F32)
    upd = jnp.where(rowb == colb, upd, 0.0)
    upd_d = upd[0:L]
    for h in range(1, HEADS_PER_GROUP):
        upd_d = upd_d + upd[h * L:(h + 1) * L]
    emit(y, (s0 + upd_d) * g_tot)


def _wkv_kernel(*refs, n_groups, n_batch):
    ins, (yf_ref, yb_ref, s_ref) = refs[:12], refs[12:]
    G = WKV_GROUP

    @pl.when(pl.program_id(1) == 0)
    def _():
        s_ref[...] = jnp.zeros_like(s_ref)

    chains = []
    for d, y_ref in enumerate((yf_ref, yb_ref)):
        r_ref, v_ref, kk_ref, lw_ref, bk_ref, kd_ref = ins[6 * d:6 * d + 6]
        for bi in range(n_batch):
            for g in range(n_groups):
                sl = slice(g * G, (g + 1) * G)

                def emit(y, s_new, y_ref=y_ref, d=d, bi=bi, g=g, sl=sl):
                    y_ref[bi, :, sl] = y
                    s_ref[d, bi, g] = s_new

                chains.append(_wkv_chunk(
                    r_ref[bi, :, sl], v_ref[bi, :, sl], kk_ref[bi, :, sl], lw_ref[0, bi, :, sl],
                    bk_ref[0, bi, :, sl], kd_ref[0, bi, :, sl], s_ref[d, bi, g], d == 1, emit))
    while chains:
        chains = [c for c in chains if next(c, StopIteration) is not StopIteration]


WKV_BATCH = 4


def _wkv_scan(r, v, kk, lw, bk, kd, n_ctx):
    B, Ttot, C = r.shape
    L = WKV_CHUNK
    nc = Ttot // L
    ncc = n_ctx // L
    nl = nc - ncc
    n_groups = C // WKV_GROUP
    nb = WKV_BATCH

    cid = (lambda s: s, lambda s: jnp.where(s < ncc, ncc - 1 - s, nc + ncc - 1 - s))
    first_out = (0, nl - 1)
    in_specs, out_specs = [], []
    for d in range(2):
        shared = pl.BlockSpec((nb, L, C), lambda b, s, d=d: (b, cid[d](s), 0))
        per_dir = pl.BlockSpec((1, nb, L, C), lambda b, s, d=d: (d, b, cid[d](s), 0))
        in_specs += [shared, shared, shared, per_dir, per_dir, per_dir]
        out_specs.append(pl.BlockSpec(
            (nb, L, C), lambda b, s, d=d: (b, jnp.where(s < ncc, first_out[d], cid[d](s) - ncc), 0)))
    out = jax.ShapeDtypeStruct((B, nl * L, C), F32)
    return pl.pallas_call(
        functools.partial(_wkv_kernel, n_groups=n_groups, n_batch=nb),
        grid=(B // nb, nc),
        in_specs=in_specs,
        out_specs=out_specs,
        out_shape=[out, out],
        scratch_shapes=[pltpu.VMEM((2, nb, n_groups, L, WKV_GROUP), F32)],
        compiler_params=pltpu.CompilerParams(
            dimension_semantics=("arbitrary", "arbitrary"), vmem_limit_bytes=VMEM_LIMIT),
        name="wkv_scan",
    )(r, v, kk, lw, bk, kd, r, v, kk, lw, bk, kd)


def kernel(x, c, ctx, c_ctx, ada_w, ada_b, norm_mix, norm_ffn, w_in, shift_conv, q_lat_norm, w_uq, kv_lat_norm, w_ukv, q_norm, k_norm, w_o_mla, decay_w0, decay_w2, aicl_a0, aicl_a2, k_k, k_a, r_k, gn_w, gn_b, gate_g2, w_o_rwkv, w_out, router_w, router_bias, expert_w1, expert_w3, expert_w2, shared_w1, shared_w3, shared_w2):
    B, T, D = x.shape
    n_ctx = ctx.shape[1]
    i0 = 0
    p = dict(norm_mix=norm_mix[i0], norm_ffn=norm_ffn[i0], w_in=w_in[i0], shift_conv=shift_conv[i0],
             q_lat_norm=q_lat_norm[i0], w_uq=w_uq[i0], kv_lat_norm=kv_lat_norm[i0], w_ukv=w_ukv[i0],
             q_norm=q_norm[i0], k_norm=k_norm[i0], w_o_mla=w_o_mla[i0],
             decay_w0=decay_w0[i0], decay_w2=decay_w2[i0], aicl_a0=aicl_a0[i0], aicl_a2=aicl_a2[i0],
             k_k=k_k[i0], k_a=k_a[i0], r_k=r_k[i0], gn_w=gn_w[i0], gn_b=gn_b[i0], gate_g2=gate_g2[i0],
             w_o_rwkv=w_o_rwkv[i0], w_out=w_out[i0], router_w=router_w[i0], router_bias=router_bias[i0],
             expert_w1=expert_w1[i0], expert_w3=expert_w3[i0], expert_w2=expert_w2[i0],
             shared_w1=shared_w1[i0], shared_w3=shared_w3[i0], shared_w2=shared_w2[i0])

    mod = _ada_modulation(c, c_ctx, ada_w[i0], ada_b[i0])
    sh_a, sc_a, g_a, sh_m, sc_m, g_m = jnp.split(mod[:B], 6, axis=-1)
    csh_a, csc_a = jnp.split(mod[B], 6, axis=-1)[:2]
    sh2 = jnp.stack([jnp.broadcast_to(csh_a, (B, D)), sh_a], axis=1)
    sc2 = jnp.stack([jnp.broadcast_to(csc_a, (B, D)), sc_a], axis=1)

    proj = _in_proj(ctx, x, sh2, sc2, p['norm_mix'], _pack_w_in(p['w_in']))

    q, k, v = _mla_prep(proj, p, n_ctx, T)
    att = _attention(q, k, v)

    r, vv, kk, lw, bk, kd, bonus, gg = _rwkv_prep(proj, p, n_ctx)
    y_f, y_b = _wkv_scan(r, vv, kk, lw, bk, kd, n_ctx)

    x1 = _merge(att, y_f, y_b, bonus, gg, proj, x, g_a, p, n_ctx)
    return _moe(x1, sh_m, sc_m, g_m, p)


SUBLANES = 8


def _ada_kernel(c_ref, w_ref, b_ref, o_ref):
    cc = c_ref[...]
    w = w_ref[...]
    w_hi = w.astype(BF16)
    w_lo = (w - w_hi.astype(F32)).astype(BF16)
    o_ref[...] = _dot3(cc * jax.nn.sigmoid(cc), w_hi, w_lo) + b_ref[...]


def _ada_modulation(c, c_ctx, ada_w, ada_b):
    B, D = c.shape
    n_out = ada_w.shape[1]
    rows = -(-(B + 1) // SUBLANES) * SUBLANES
    c_all = jnp.concatenate([c, c_ctx[None, :], jnp.zeros((rows - B - 1, D), F32)], axis=0)
    out = pl.pallas_call(
        _ada_kernel,
        grid=(n_out // D,),
        in_specs=[pl.BlockSpec((rows, D), lambda j: (0, 0)),
                  pl.BlockSpec((D, D), lambda j: (0, j)),
                  pl.BlockSpec((1, D), lambda j: (0, j))],
        out_specs=pl.BlockSpec((rows, D), lambda j: (0, j)),
        out_shape=jax.ShapeDtypeStruct((rows, n_out), F32),
        compiler_params=pltpu.CompilerParams(dimension_semantics=("arbitrary",), vmem_limit_bytes=VMEM_LIMIT),
        name="ada_modulation",
    )(c_all, ada_w, ada_b.reshape(1, n_out))
    return out[:B + 1]


ROW_TILE = 256
PROJ_RW = 2048
PROJ_GATE = 2048
PROJ_MLA = 512
PROJ_W = PROJ_RW + PROJ_GATE + PROJ_MLA
MLA_IN = Q_LORA + KV_LORA + QK_ROPE
RW_SPLITS = (RW_DIM, RW_DIM, RW_DIM, DECAY_LORA, DECAY_LORA, AICL_LORA, AICL_LORA, GATE_LORA)
RW_IN = sum(RW_SPLITS)


def _pack_w_in(w_in):
    w_mla = w_in[:, :MLA_IN]
    w_rw = w_in[:, MLA_IN:MLA_IN + RW_IN]
    w_gate = w_in[:, MLA_IN + RW_IN:]
    pad = lambda w, n: jnp.pad(w, ((0, 0), (0, n - w.shape[1])))
    return jnp.concatenate([pad(w_rw, PROJ_RW), w_gate, pad(w_mla, PROJ_MLA)], axis=1).astype(BF16)


PROJ_PIECES = 3
PROJ_COL_CHUNK = PROJ_W // 3


def _in_proj_kernel(ctx_ref, *refs):
    x_refs, (sh_ref, sc_ref, gain_ref, w_ref, o_ref) = refs[:PROJ_PIECES], refs[PROJ_PIECES:]
    first = pl.program_id(1) == 0
    pieces = []
    for n, x_ref in enumerate(x_refs):
        is_ctx = first if n == 0 else False
        xt = jnp.where(is_ctx, ctx_ref[0], x_ref[0]) if n == 0 else x_ref[0]
        sh = jnp.where(is_ctx, sh_ref[0, 0:1], sh_ref[0, 1:2])
        sc = jnp.where(is_ctx, sc_ref[0, 0:1], sc_ref[0, 1:2])
        y = xt * lax.rsqrt(jnp.mean(xt * xt, axis=-1, keepdims=True) + EPS) * gain_ref[...]
        pieces.append((y * (1.0 + sc) + sh).astype(BF16))
    h = jnp.concatenate(pieces, axis=0)
    for c in range(PROJ_W // PROJ_COL_CHUNK):
        cols = slice(c * PROJ_COL_CHUNK, (c + 1) * PROJ_COL_CHUNK)
        o_ref[0, :, cols] = jnp.dot(h, w_ref[:, cols], preferred_element_type=F32).astype(o_ref.dtype)


def _in_proj(ctx, x, sh2, sc2, gain, w):
    B, T, D = x.shape
    n_ctx = ctx.shape[1]
    rows = PROJ_PIECES * ROW_TILE
    assert n_ctx == ROW_TILE and (n_ctx + T) % rows == 0
    nt = (n_ctx + T) // rows
    piece = lambda n: pl.BlockSpec(
        (1, ROW_TILE, D), lambda b, i, n=n: (b, jnp.maximum(PROJ_PIECES * i + n - 1, 0), 0))
    return pl.pallas_call(
        _in_proj_kernel,
        grid=(B, nt),
        in_specs=[pl.BlockSpec((1, ROW_TILE, D), lambda b, i: (b, 0, 0))]
                 + [piece(n) for n in range(PROJ_PIECES)]
                 + [pl.BlockSpec((1, 2, D), lambda b, i: (b, 0, 0)),
                    pl.BlockSpec((1, 2, D), lambda b, i: (b, 0, 0)),
                    pl.BlockSpec((1, D), lambda b, i: (0, 0)),
                    pl.BlockSpec((D, PROJ_W), lambda b, i: (0, 0))],
        out_specs=pl.BlockSpec((1, rows, PROJ_W), lambda b, i: (b, i, 0)),
        out_shape=jax.ShapeDtypeStruct((B, n_ctx + T, PROJ_W), BF16),
        compiler_params=pltpu.CompilerParams(
            dimension_semantics=("arbitrary", "arbitrary"), vmem_limit_bytes=VMEM_LIMIT),
        name="in_proj",
    )(ctx, *([x] * PROJ_PIECES), sh2, sc2, gain.reshape(1, D), w)


def _rms(x, gain):
    return x * lax.rsqrt(jnp.mean(x * x, axis=-1, keepdims=True) + EPS) * gain


def _rope_tables(n_tokens):
    rows = n_tokens // GRID_W
    row = jnp.repeat(jnp.arange(rows, dtype=F32), GRID_W)
    col = jnp.tile(jnp.arange(GRID_W, dtype=F32), rows)
    n_freq = QK_ROPE // 4
    inv_freq = ROPE_BASE ** (-jnp.arange(n_freq, dtype=F32) / n_freq)
    ang = jnp.concatenate([row[:, None] * inv_freq, col[:, None] * inv_freq], axis=-1)
    return jnp.cos(ang), jnp.sin(ang)


ROPE_HALF = QK_ROPE // 2
X1 = slice(QK_NOPE, QK_NOPE + ROPE_HALF)
X2 = slice(QK_NOPE + ROPE_HALF, QK_HEAD)


def _rot_cols(w):
    w3 = w.reshape(w.shape[0], MLA_HEADS, HEAD_PAD)
    rot = jnp.zeros_like(w3).at[:, :, X1].set(-w3[:, :, X2]).at[:, :, X2].set(w3[:, :, X1])
    return rot.reshape(w.shape)


def _swap_halves(g):
    return jnp.zeros_like(g).at[:, X1].set(g[:, X2]).at[:, X2].set(g[:, X1])


def _mla_weights(p):
    H = MLA_HEADS
    wq = jnp.pad(p['w_uq'].reshape(Q_LORA, H, QK_HEAD), ((0, 0), (0, 0), (0, HEAD_PAD - QK_HEAD)))
    wq = wq.reshape(Q_LORA, H * HEAD_PAD)
    wkv = p['w_ukv'].reshape(KV_LORA, H, QK_NOPE + V_HEAD)
    wk_lat = jnp.pad(wkv[:, :, :QK_NOPE], ((0, 0), (0, 0), (0, HEAD_PAD - QK_NOPE)))
    place = jnp.zeros((LANES, H, HEAD_PAD), F32).at[:QK_ROPE, :, QK_NOPE:QK_HEAD].set(
        jnp.broadcast_to(jnp.eye(QK_ROPE, dtype=F32)[:, None, :], (QK_ROPE, H, QK_ROPE)))
    wk = jnp.concatenate([wk_lat, place], axis=0).reshape(KV_LORA + LANES, H * HEAD_PAD)
    wv = wkv[:, :, QK_NOPE:].reshape(KV_LORA, H * V_HEAD)
    gq = jnp.pad(p['q_norm'], (0, HEAD_PAD - QK_HEAD)).reshape(1, HEAD_PAD)
    gk = jnp.pad(p['k_norm'], (0, HEAD_PAD - QK_HEAD)).reshape(1, HEAD_PAD)
    bf = lambda w: w.astype(BF16)
    return (bf(wq), bf(_rot_cols(wq)), bf(wk), bf(_rot_cols(wk)), bf(wv), gq, _swap_halves(gq), gk, _swap_halves(gk))


def _mla_tables(n_ctx, T):
    cos, sin = _rope_tables(T)
    c = jnp.ones((n_ctx + T, HEAD_PAD), F32).at[n_ctx:, X1].set(cos).at[n_ctx:, X2].set(cos)
    s = jnp.zeros((n_ctx + T, HEAD_PAD), F32).at[n_ctx:, X1].set(sin).at[n_ctx:, X2].set(sin)
    return c, s


def _mla_prep_kernel(x_ref, c_ref, s_ref, qlg_ref, kvg_ref, wq_ref, wqr_ref, wk_ref, wkr_ref, wv_ref,
                     gq_ref, gqp_ref, gk_ref, gkp_ref, q_ref, k_ref, v_ref):
    x = x_ref[0].astype(F32)
    ql = _rms(x[:, :Q_LORA], qlg_ref[...])
    kvl = _rms(x[:, Q_LORA:Q_LORA + KV_LORA], kvg_ref[...])
    k_in = jnp.concatenate([kvl, x[:, Q_LORA + KV_LORA:]], axis=1)
    cos, sin = c_ref[...], s_ref[...]

    def finish(raw, partner, g, g_swapped, scale, o_ref):
        gc, gs = g * cos, g_swapped * sin
        for h in range(MLA_HEADS):
            sl = slice(h * HEAD_PAD, (h + 1) * HEAD_PAD)
            rh = raw[:, sl]
            inv = lax.rsqrt(jnp.sum(rh * rh, axis=-1, keepdims=True) * (1.0 / QK_HEAD) + EPS) * scale
            o_ref[0, :, sl] = ((rh * gc + partner[:, sl] * gs) * inv).astype(o_ref.dtype)

    finish(_dot(ql, wq_ref[...]), _dot(ql, wqr_ref[...]), gq_ref[...], gqp_ref[...], ATTN_SCALE, q_ref)
    finish(_dot(k_in, wk_ref[...]), _dot(k_in, wkr_ref[...]), gk_ref[...], gkp_ref[...], 1.0, k_ref)
    v_ref[0] = _dot(kvl, wv_ref[...]).astype(v_ref.dtype)


def _mla_prep(proj, p, n_ctx, T):
    B, Tt, _ = proj.shape
    H = MLA_HEADS
    ctx_tiles = n_ctx // ROW_TILE
    mla_blk = (PROJ_RW + PROJ_GATE) // PROJ_MLA
    weights = _mla_weights(p)
    cos, sin = _mla_tables(n_ctx, T)
    const = lambda b, i: (0, 0)
    row = lambda b, i: (b, i, 0)
    tab = pl.BlockSpec((ROW_TILE, HEAD_PAD), lambda b, i: (i, 0))
    full = lambda a: pl.BlockSpec(a.shape, const)
    return pl.pallas_call(
        _mla_prep_kernel,
        grid=(B, Tt // ROW_TILE),
        in_specs=[pl.BlockSpec((1, ROW_TILE, PROJ_MLA), lambda b, i: (b, i, mla_blk)), tab, tab,
                  pl.BlockSpec((1, Q_LORA), const), pl.BlockSpec((1, KV_LORA), const)]
                 + [full(w) for w in weights],
        out_specs=[
            pl.BlockSpec((1, ROW_TILE, H * HEAD_PAD), lambda b, i: (b, jnp.maximum(i - ctx_tiles, 0), 0)),
            pl.BlockSpec((1, ROW_TILE, H * HEAD_PAD), row),
            pl.BlockSpec((1, ROW_TILE, H * V_HEAD), row),
        ],
        out_shape=[
            jax.ShapeDtypeStruct((B, T, H * HEAD_PAD), BF16),
            jax.ShapeDtypeStruct((B, Tt, H * HEAD_PAD), BF16),
            jax.ShapeDtypeStruct((B, Tt, H * V_HEAD), BF16),
        ],
        compiler_params=pltpu.CompilerParams(
            dimension_semantics=("arbitrary", "arbitrary"), vmem_limit_bytes=VMEM_LIMIT),
        name="mla_prep",
    )(proj, cos, sin, p['q_lat_norm'].reshape(1, Q_LORA), p['kv_lat_norm'].reshape(1, KV_LORA), *weights)


ATTN_Q_TILE = 2048
HEADS_PER_STEP = LANES // V_HEAD


ATTN_ROW_SPLIT = 16


def _attn_kernel(q_ref, k_ref, v_ref, o_ref):
    v2 = v_ref[0]
    rows = q_ref.shape[1] // ATTN_ROW_SPLIT
    work = [(hh, rs) for rs in range(ATTN_ROW_SPLIT) for hh in range(HEADS_PER_STEP)]

    def scores(hh, rs):
        sl = slice(hh * HEAD_PAD, (hh + 1) * HEAD_PAD)
        return _dot_t(q_ref[0, rs * rows:(rs + 1) * rows, sl], k_ref[0, :, sl])

    outs = {}
    s_next = scores(*work[0])
    for i, (hh, rs) in enumerate(work):
        s = s_next
        if i + 1 < len(work):
            s_next = scores(*work[i + 1])
        e = jnp.exp(s - jnp.max(s, axis=-1, keepdims=True))
        outs[hh, rs] = _dot(e, v2) / jnp.sum(e, axis=-1, keepdims=True)
    lane = lax.broadcasted_iota(jnp.int32, (rows, LANES), 1)
    for rs in range(ATTN_ROW_SPLIT):
        o_ref[0, rs * rows:(rs + 1) * rows] = jnp.where(lane < V_HEAD, outs[0, rs], outs[1, rs])


def _attention(q, k, v):
    B, T, _ = q.shape
    Kt = k.shape[1]
    assert T % ATTN_Q_TILE == 0 and ATTN_Q_TILE % (ATTN_ROW_SPLIT * SUBLANES) == 0
    hp = MLA_HEADS // HEADS_PER_STEP
    qw = HEADS_PER_STEP * HEAD_PAD
    return pl.pallas_call(
        _attn_kernel,
        grid=(B, hp, T // ATTN_Q_TILE),
        in_specs=[
            pl.BlockSpec((1, ATTN_Q_TILE, qw), lambda b, h, i: (b, i, h)),
            pl.BlockSpec((1, Kt, qw), lambda b, h, i: (b, 0, h)),
            pl.BlockSpec((1, Kt, LANES), lambda b, h, i: (b, 0, h)),
        ],
        out_specs=pl.BlockSpec((1, ATTN_Q_TILE, LANES), lambda b, h, i: (b, i, h)),
        out_shape=jax.ShapeDtypeStruct((B, T, MLA_HEADS * V_HEAD), F32),
        compiler_params=pltpu.CompilerParams(
            dimension_semantics=("arbitrary", "arbitrary", "arbitrary"), vmem_limit_bytes=VMEM_LIMIT),
        name="attention",
    )(q, k, v)


HALO = 16
LORA_W = LANES
GATE_W = PROJ_RW - 3 * RW_DIM - 2 * LORA_W


def _head_ones(width, head):
    i = np.arange(width) // head
    return jnp.asarray(i[:, None] == i[None, :], BF16)


def _head_sum(x, ones_bd):
    hi = x.astype(BF16)
    lo = (x - hi.astype(F32)).astype(BF16)
    return (jnp.dot(hi, ones_bd, preferred_element_type=F32)
            + jnp.dot(lo, ones_bd, preferred_element_type=F32))


def _dot3(a, b_hi, b_lo):
    hi = a.astype(BF16)
    lo = (a - hi.astype(F32)).astype(BF16)
    d = lambda u, w: jnp.dot(u, w, preferred_element_type=F32)
    return d(hi, b_hi) + (d(hi, b_lo) + d(lo, b_hi))


def _rwkv_prep_kernel(x_ref, prev_ref, next_ref, conv_ref, kkg_ref, ka_ref, rk_ref, w0_ref, a0_ref,
                      w2h_ref, w2l_ref, a2_ref, g2_ref, ones_ref,
                      r_ref, v_ref, kk_ref, lw_ref, bk_ref, kd_ref, bonus_ref, gg_ref, *, ctx_tiles, n_tiles):
    i = pl.program_id(1)
    x = x_ref[0].astype(F32)
    tm, W = x.shape
    C = RW_DIM
    first = (i == 0) | (i == ctx_tiles)
    last = (i == ctx_tiles - 1) | (i == n_tiles - 1)
    prev_row = jnp.where(first, 0.0, prev_ref[0, HALO - 1:HALO].astype(F32))
    next_row = jnp.where(last, 0.0, next_ref[0, 0:1].astype(F32))
    row = lax.broadcasted_iota(jnp.int32, (tm, W), 0)
    x_dn = jnp.where(row == 0, prev_row, pltpu.roll(x, 1, 0))
    x_up = jnp.where(row == tm - 1, next_row, pltpu.roll(x, tm - 1, 0))
    xc = x_dn * conv_ref[0:1] + x * conv_ref[1:2] + x_up * conv_ref[2:3]

    r, k, v = xc[:, :C], xc[:, C:2 * C], xc[:, 2 * C:3 * C]
    lora_w = jnp.tanh(xc[:, 3 * C:3 * C + LORA_W])
    lora_a = xc[:, 3 * C + LORA_W:3 * C + 2 * LORA_W]
    lg = xc[:, 3 * C + 2 * LORA_W:]
    ones_bd = ones_ref[...]
    kq = k * kkg_ref[...]
    kk = kq * lax.rsqrt(_head_sum(kq * kq, ones_bd) + 1e-12)
    r_ref[0], v_ref[0], kk_ref[0] = r, v, kk

    k_sum = jnp.zeros_like(k)
    for d in range(2):
        z = w0_ref[d:d + 1] + _dot3(lora_w, w2h_ref[d], w2l_ref[d])
        softplus_neg = jnp.maximum(-z, 0.0) + jnp.log(1.0 + jnp.exp(-jnp.abs(z)))
        lw_ref[d, 0] = -jnp.exp(-softplus_neg - 0.5)
        a = jax.nn.sigmoid(a0_ref[d:d + 1] + _dot(lora_a, a2_ref[d]))
        kd = k * (1.0 + (a - 1.0) * ka_ref[...])
        bk_ref[d, 0] = kk * a
        kd_ref[d, 0] = kd
        k_sum = k_sum + kd
    bonus_ref[0] = _head_sum(r * k_sum * rk_ref[...], ones_bd) * v
    gg_ref[0] = _dot(jax.nn.sigmoid(lg), g2_ref[...])


def _rwkv_prep(proj, p, n_ctx):
    B, Tt, _ = proj.shape
    C = RW_DIM
    nt = Tt // ROW_TILE
    hb = ROW_TILE // HALO
    pad_cols = lambda w: jnp.pad(w, ((0, 0), (0, PROJ_RW - w.shape[1])))
    w2 = jnp.stack([jnp.pad(p['decay_w2'][0], ((0, LORA_W - DECAY_LORA), (0, 0))),
                    jnp.pad(p['decay_w2'][1], ((DECAY_LORA, 0), (0, 0)))])
    w2h = w2.astype(BF16)
    w2l = (w2 - w2h.astype(F32)).astype(BF16)
    a2 = jnp.stack([jnp.pad(p['aicl_a2'][0], ((0, LORA_W - AICL_LORA), (0, 0))),
                    jnp.pad(p['aicl_a2'][1], ((AICL_LORA, 0), (0, 0)))]).astype(BF16)
    g2 = jnp.pad(p['gate_g2'], ((0, GATE_W - GATE_LORA), (0, 0))).astype(BF16)
    row = lambda b, i: (b, i, 0)
    drow = lambda b, i: (0, b, i, 0)
    const2 = lambda b, i: (0, 0)
    const3 = lambda b, i: (0, 0, 0)
    vec = pl.BlockSpec((1, C), const2)
    out_row = pl.BlockSpec((1, ROW_TILE, C), row)
    out_dir = pl.BlockSpec((2, 1, ROW_TILE, C), drow)
    sds = jax.ShapeDtypeStruct((B, Tt, C), F32)
    sds2 = jax.ShapeDtypeStruct((2, B, Tt, C), F32)
    ctx_tiles = n_ctx // ROW_TILE
    out_lat = pl.BlockSpec((1, ROW_TILE, C), lambda b, i: (b, jnp.maximum(i - ctx_tiles, 0), 0))
    sds_lat = jax.ShapeDtypeStruct((B, Tt - n_ctx, C), F32)
    return pl.pallas_call(
        functools.partial(_rwkv_prep_kernel, ctx_tiles=n_ctx // ROW_TILE, n_tiles=nt),
        grid=(B, nt),
        in_specs=[
            pl.BlockSpec((1, ROW_TILE, PROJ_RW), row),
            pl.BlockSpec((1, HALO, PROJ_RW), lambda b, i: (b, jnp.maximum(i * hb - 1, 0), 0)),
            pl.BlockSpec((1, HALO, PROJ_RW), lambda b, i: (b, jnp.minimum((i + 1) * hb, nt * hb - 1), 0)),
            pl.BlockSpec((3, PROJ_RW), const2),
            vec, vec, vec,
            pl.BlockSpec((2, C), const2),
            pl.BlockSpec((2, C), const2),
            pl.BlockSpec((2, LORA_W, C), const3),
            pl.BlockSpec((2, LORA_W, C), const3),
            pl.BlockSpec((2, LORA_W, C), const3),
            pl.BlockSpec((GATE_W, C), const2),
            pl.BlockSpec((C, C), const2),
        ],
        out_specs=[out_row, out_row, out_row, out_dir, out_dir, out_dir, out_lat, out_lat],
        out_shape=[sds, sds, sds, sds2, sds2, sds2, sds_lat, sds_lat],
        compiler_params=pltpu.CompilerParams(
            dimension_semantics=("arbitrary", "arbitrary"), vmem_limit_bytes=VMEM_LIMIT),
        name="rwkv_prep",
    )(proj, proj, proj, pad_cols(p['shift_conv']), p['k_k'].reshape(1, C), p['k_a'].reshape(1, C),
      p['r_k'].reshape(1, C), p['decay_w0'], p['aicl_a0'], w2h, w2l, a2, g2, _head_ones(C, RW_HEAD))


MERGE_PIECES = 2


def _merge_kernel(att_ref, yf_ref, yb_ref, bonus_ref, gg_ref, *refs):
    gate_refs, (x_ref, ga_ref, gnw_ref, gnb_ref, ones_ref, wm_ref, wr_ref, wo_ref, o_ref) = (
        refs[:MERGE_PIECES], refs[MERGE_PIECES:])
    D = x_ref.shape[-1]
    ones_bd = ones_ref[...]
    y = yf_ref[0] + yb_ref[0]
    yc = y - _head_sum(y, ones_bd) * (1.0 / RW_HEAD)
    var = _head_sum(yc * yc, ones_bd) * (1.0 / RW_HEAD)
    y_n = yc * lax.rsqrt(var + GN_EPS) * gnw_ref[...] + gnb_ref[...]
    rw = (y_n + bonus_ref[0]) * gg_ref[0]
    g = jax.nn.sigmoid(jnp.concatenate([r[0] for r in gate_refs], axis=0).astype(F32))
    mix = g[:, :D] * _dot(att_ref[0], wm_ref[...]) + g[:, D:] * _dot(rw, wr_ref[...])
    o_ref[0] = x_ref[0] + ga_ref[0] * _dot(mix, wo_ref[...])


def _merge(att, y_f, y_b, bonus, gg, proj, x, g_a, p, n_ctx):
    B, T, D = x.shape
    C = RW_DIM
    rows = MERGE_PIECES * ROW_TILE
    ctx_tiles = n_ctx // ROW_TILE
    gate_blk = PROJ_RW // PROJ_GATE
    row = lambda b, i: (b, i, 0)
    const = lambda b, i: (0, 0)
    gate_piece = lambda n: pl.BlockSpec(
        (1, ROW_TILE, PROJ_GATE), lambda b, i, n=n: (b, MERGE_PIECES * i + n + ctx_tiles, gate_blk))
    return pl.pallas_call(
        _merge_kernel,
        grid=(B, T // rows),
        in_specs=[
            pl.BlockSpec((1, rows, att.shape[-1]), row),
            pl.BlockSpec((1, rows, C), row),
            pl.BlockSpec((1, rows, C), row),
            pl.BlockSpec((1, rows, C), row),
            pl.BlockSpec((1, rows, C), row),
        ] + [gate_piece(n) for n in range(MERGE_PIECES)] + [
            pl.BlockSpec((1, rows, D), row),
            pl.BlockSpec((1, 1, D), lambda b, i: (b, 0, 0)),
            pl.BlockSpec((1, C), const),
            pl.BlockSpec((1, C), const),
            pl.BlockSpec((C, C), const),
            pl.BlockSpec(p['w_o_mla'].shape, const),
            pl.BlockSpec(p['w_o_rwkv'].shape, const),
            pl.BlockSpec(p['w_out'].shape, const),
        ],
        out_specs=pl.BlockSpec((1, rows, D), row),
        out_shape=jax.ShapeDtypeStruct((B, T, D), F32),
        compiler_params=pltpu.CompilerParams(
            dimension_semantics=("arbitrary", "arbitrary"), vmem_limit_bytes=VMEM_LIMIT),
        name="merge",
    )(att, y_f, y_b, bonus, gg, *([proj] * MERGE_PIECES), x, g_a[:, None, :], p['gn_w'].reshape(1, C),
      p['gn_b'].reshape(1, C), _head_ones(C, RW_HEAD), p['w_o_mla'].astype(BF16), p['w_o_rwkv'].astype(BF16),
      p['w_out'].astype(BF16))


PACK_CHUNKS = 4
TILE_ROWS = 8
MOE_BLOCK = 512
ISSUE_UNROLL = 4


ROUTER_TILE = 1024
NEG_INF = float("-inf")
HI_MASK = 0xFFFF0000


def _pack_pair(lo, hi):
    lo_b = lax.bitcast_convert_type(lo.astype(BF16).astype(F32), jnp.uint32)
    hi_b = lax.bitcast_convert_type(hi.astype(BF16).astype(F32), jnp.uint32)
    return (lo_b >> 16) | (hi_b & jnp.uint32(HI_MASK))


def _unpack_pair(w):
    lo = lax.bitcast_convert_type(w << 16, F32)
    hi = lax.bitcast_convert_type(w & jnp.uint32(HI_MASK), F32)
    return lo.astype(BF16), hi.astype(BF16)


def _row_max(x):
    return jnp.max(x, axis=-1, keepdims=True)


def _first_index_of(x, value, lane_f):
    return jnp.min(jnp.where(x == value, lane_f, float(x.shape[-1])), axis=-1, keepdims=True)


def _router_kernel(x_ref, gain_ref, sh_ref, sc_ref, wh_ref, wm_ref, wl_ref, bias_ref,
                   u_ref, ut_ref, ids_ref, gates_ref, ranks_ref, counts_ref, carry_ref):
    @pl.when((pl.program_id(0) == 0) & (pl.program_id(1) == 0))
    def _():
        carry_ref[...] = jnp.zeros_like(carry_ref)

    x = x_ref[0]
    tm, D = x.shape
    E = bias_ref.shape[-1]
    u = x * lax.rsqrt(jnp.mean(x * x, axis=-1, keepdims=True) + EPS) * gain_ref[...]
    u = u * (1.0 + sc_ref[0]) + sh_ref[0]
    packed = _pack_pair(u[:, :D // 2], u[:, D // 2:])
    u_ref[0] = packed
    for c in range(PACK_CHUNKS):
        ut_ref[:, c, :] = packed[:, c * LANES:(c + 1) * LANES]

    uh, um, ul = _split3(u)
    wh, wm, wl = wh_ref[...], wm_ref[...], wl_ref[...]
    d = lambda a, b: jnp.dot(a, b, preferred_element_type=F32)
    logits = d(uh, wh) + (d(uh, wm) + d(um, wh)) + (d(uh, wl) + d(um, wm) + d(ul, wh))
    scores = jax.nn.sigmoid(logits)
    sel = scores + bias_ref[...]

    lane_i = lax.broadcasted_iota(jnp.int32, (tm, E), 1)
    lane_f = lane_i.astype(F32)
    out_f = lax.broadcasted_iota(jnp.int32, (tm, LANES), 1).astype(F32)
    per_group = E // N_GROUPS
    grp_f = jnp.floor(lane_f * (1.0 / per_group))

    gs = jnp.full((tm, LANES), NEG_INF, F32)
    for g in range(N_GROUPS):
        sg = jnp.where(lane_i >= g * per_group, jnp.where(lane_i < (g + 1) * per_group, sel, NEG_INF), NEG_INF)
        m1 = _row_max(sg)
        i1 = _first_index_of(sg, m1, lane_f)
        m2 = _row_max(jnp.where(lane_f == i1, NEG_INF, sg))
        gs = jnp.where(out_f == g, m1 + m2, gs)

    allow = jnp.zeros((tm, E), F32)
    for _ in range(TOPK_GROUPS):
        m = _row_max(gs)
        i = _first_index_of(gs, m, out_f)
        gs = jnp.where(out_f == i, NEG_INF, gs)
        allow = jnp.where(grp_f == i, 1.0, allow)
    selm = jnp.where(allow > 0.0, sel, NEG_INF)

    ids = jnp.zeros((tm, LANES), F32)
    gts = jnp.zeros((tm, LANES), F32)
    member = jnp.zeros((tm, E), F32)
    idx_cols = []
    gsum = jnp.zeros((tm, 1), F32)
    for k in range(TOP_K):
        m = _row_max(selm)
        i = _first_index_of(selm, m, lane_f)
        hit = lane_f == i
        gk = jnp.sum(jnp.where(hit, scores, 0.0), axis=-1, keepdims=True)
        selm = jnp.where(hit, NEG_INF, selm)
        member = jnp.where(hit, 1.0, member)
        ids = jnp.where(out_f == k, i, ids)
        gts = jnp.where(out_f == k, gk, gts)
        idx_cols.append(i)
        gsum = gsum + gk
    gts = gts / gsum * ROUTED_SCALE

    r2 = lax.broadcasted_iota(jnp.int32, (tm, tm), 0)
    c2 = lax.broadcasted_iota(jnp.int32, (tm, tm), 1)
    before = jnp.where(r2 > c2, 1.0, 0.0).astype(BF16)
    mem_b = member.astype(BF16)
    carry = carry_ref[...]
    pos = carry + jnp.dot(before, mem_b, preferred_element_type=F32)
    rk = jnp.zeros((tm, LANES), F32)
    for k in range(TOP_K):
        rk = jnp.where(out_f == k, jnp.sum(jnp.where(lane_f == idx_cols[k], pos, 0.0), axis=-1, keepdims=True), rk)
    colsum = jnp.dot(jnp.ones((8, tm), BF16), mem_b, preferred_element_type=F32)[0:1]
    carry_ref[...] = carry + colsum
    counts_ref[...] = carry + colsum
    ids_ref[0] = ids.astype(jnp.int32)
    ranks_ref[0] = rk.astype(jnp.int32)
    gates_ref[0] = gts


def _router(x1, sh_m, sc_m, p):
    B, T, D = x1.shape
    E = N_EXPERTS
    wh, wm, wl = _split3(p['router_w'])
    row = lambda b, i: (b, i, 0)
    const = lambda b, i: (0, 0)
    vec = lambda b, i: (b, 0, 0)
    lane_out = lambda dt: jax.ShapeDtypeStruct((B, T, LANES), dt)
    return pl.pallas_call(
        _router_kernel,
        grid=(B, T // ROUTER_TILE),
        in_specs=[
            pl.BlockSpec((1, ROUTER_TILE, D), row),
            pl.BlockSpec((1, D), const),
            pl.BlockSpec((1, 1, D), vec),
            pl.BlockSpec((1, 1, D), vec),
            pl.BlockSpec((D, E), const),
            pl.BlockSpec((D, E), const),
            pl.BlockSpec((D, E), const),
            pl.BlockSpec((1, E), const),
        ],
        out_specs=[
            pl.BlockSpec((1, ROUTER_TILE, D // 2), row),
            pl.BlockSpec((ROUTER_TILE, PACK_CHUNKS, LANES), lambda b, i: (b * (T // ROUTER_TILE) + i, 0, 0)),
            pl.BlockSpec((1, ROUTER_TILE, LANES), row),
            pl.BlockSpec((1, ROUTER_TILE, LANES), row),
            pl.BlockSpec((1, ROUTER_TILE, LANES), row),
            pl.BlockSpec((1, E), const),
        ],
        out_shape=[
            jax.ShapeDtypeStruct((B, T, D // 2), jnp.uint32),
            jax.ShapeDtypeStruct((B * T, PACK_CHUNKS, LANES), jnp.uint32),
            lane_out(jnp.int32), lane_out(F32), lane_out(jnp.int32),
            jax.ShapeDtypeStruct((1, E), F32),
        ],
        scratch_shapes=[pltpu.VMEM((1, E), F32)],
        compiler_params=pltpu.CompilerParams(
            dimension_semantics=("arbitrary", "arbitrary"), vmem_limit_bytes=VMEM_LIMIT),
        name="router",
    )(x1, p['norm_ffn'].reshape(1, D), sh_m[:, None, :], sc_m[:, None, :], wh, wm, wl,
      p['router_bias'].reshape(1, E))


def _slot_kernel(ids_ref, ranks_ref, base_ref, o_ref):
    ids = ids_ref[...].astype(F32)
    tm = ids.shape[0]
    E = base_ref.shape[-1]
    lane_e = lax.broadcasted_iota(jnp.int32, (tm, E), 1).astype(F32)
    out_lane = lax.broadcasted_iota(jnp.int32, (tm, LANES), 1)
    first = jnp.zeros((tm, LANES), F32)
    for k in range(TOP_K):
        fk = jnp.sum(jnp.where(lane_e == ids[:, k:k + 1], base_ref[...], 0.0), axis=-1, keepdims=True)
        first = jnp.where(out_lane == k, fk, first)
    o_ref[...] = first.astype(jnp.int32) + ranks_ref[...]


def _slots(ids, ranks, base):
    n_tok = ids.shape[0]
    E = base.shape[-1]
    row = pl.BlockSpec((ROUTER_TILE, LANES), lambda i: (i, 0))
    return pl.pallas_call(
        _slot_kernel,
        grid=(n_tok // ROUTER_TILE,),
        in_specs=[row, row, pl.BlockSpec((1, E), lambda i: (0, 0))],
        out_specs=row,
        out_shape=jax.ShapeDtypeStruct((n_tok, LANES), jnp.int32),
        compiler_params=pltpu.CompilerParams(dimension_semantics=("arbitrary",)),
        name="slots",
    )(ids, ranks, base)


def _dispatch_kernel(pad_ref, dest_ref, u_ref, slots_ref, zero_ref, sem):
    tm = u_ref.shape[0]
    n_experts = pad_ref.shape[0]

    @pl.when(pl.program_id(0) == 0)
    def _():
        zero_ref[...] = jnp.zeros_like(zero_ref)

        def fill(e, carry):
            pltpu.make_async_copy(zero_ref, slots_ref.at[pl.ds(pad_ref[e], MOE_BLOCK)], sem).start()
            return carry

        lax.fori_loop(0, n_experts, fill, 0)

        def filled(e, carry):
            pltpu.make_async_copy(zero_ref, slots_ref.at[pl.ds(0, MOE_BLOCK)], sem).wait()
            return carry

        lax.fori_loop(0, n_experts, filled, 0)

    def issue(r, carry):
        for k in range(TOP_K):
            pltpu.make_async_copy(u_ref.at[r], slots_ref.at[dest_ref[0, 0, r * TOP_K + k]], sem).start()
        return carry

    lax.fori_loop(0, tm, issue, 0, unroll=ISSUE_UNROLL)
    for k in range(TOP_K):
        pltpu.make_async_copy(u_ref, slots_ref.at[pl.ds(0, tm)], sem).wait()


def _dispatch(pad_start, dest, u_tiles, n_slots):
    n_tok = u_tiles.shape[0]
    tile = u_tiles.shape[1:]
    nt = n_tok // ROW_TILE
    grid_spec = pltpu.PrefetchScalarGridSpec(
        num_scalar_prefetch=1,
        grid=(nt,),
        in_specs=[
            pl.BlockSpec((1, 1, ROW_TILE * TOP_K), lambda i, pad: (i, 0, 0), memory_space=pltpu.SMEM),
            pl.BlockSpec((ROW_TILE,) + tile, lambda i, pad: (i, 0, 0)),
        ],
        out_specs=pl.BlockSpec(memory_space=pl.ANY),
        scratch_shapes=[pltpu.VMEM((MOE_BLOCK,) + tile, jnp.uint32), pltpu.SemaphoreType.DMA(())],
    )
    return pl.pallas_call(
        _dispatch_kernel,
        grid_spec=grid_spec,
        out_shape=jax.ShapeDtypeStruct((n_slots + MOE_BLOCK,) + tile, jnp.uint32),
        compiler_params=pltpu.CompilerParams(
            dimension_semantics=("arbitrary",), vmem_limit_bytes=VMEM_LIMIT),
        name="dispatch",
    )(pad_start, dest, u_tiles)


WEIGHT_SLOTS = 2


def _expert_weight_copies(hbm_refs, stage_refs, expert, slot, sems):
    return [pltpu.make_async_copy(w.at[expert], s.at[slot], sems.at[slot, i])
            for i, (w, s) in enumerate(zip(hbm_refs, stage_refs))]


def _moe_ffn_kernel(be_ref, first_ref, slot_ref, next_ref, nu_ref, x_ref, w1_ref, w3_ref, w2_ref, o_ref,
                    s1_ref, s3_ref, s2_ref, w13_ref, w2b_ref, sems):
    j = pl.program_id(0)
    F = w1_ref.shape[-1]
    half = PACK_CHUNKS * LANES
    copies = functools.partial(_expert_weight_copies, (w1_ref, w3_ref, w2_ref), (s1_ref, s3_ref, s2_ref), sems=sems)

    @pl.when(j == 0)
    def _():
        for c in copies(be_ref[0], 0):
            c.start()

    @pl.when(first_ref[j] == 1)
    def _():
        slot = slot_ref[j]
        for c in copies(be_ref[j], slot):
            c.wait()

        @pl.when(next_ref[j] >= 0)
        def _():
            for c in copies(next_ref[j], 1 - slot):
                c.start()

        w13_ref[:, :F] = s1_ref[slot].astype(BF16)
        w13_ref[:, F:] = s3_ref[slot].astype(BF16)
        w2b_ref[...] = s2_ref[slot].astype(BF16)

    @pl.when(j < nu_ref[0])
    def _():
        lo, hi = _unpack_pair(jnp.concatenate([x_ref[:, c, :] for c in range(PACK_CHUNKS)], axis=1))
        h = (jnp.dot(lo, w13_ref[:half], preferred_element_type=F32)
             + jnp.dot(hi, w13_ref[half:], preferred_element_type=F32))
        h1, h3 = h[:, :F], h[:, F:]
        y = _dot(h1 * jax.nn.sigmoid(h1) * h3, w2b_ref[...])
        o_ref[...] = pltpu.einshape("s(cl)->(sc)l", y, c=TILE_ROWS)

    @pl.when(j >= nu_ref[0])
    def _():
        o_ref[...] = jnp.zeros_like(o_ref)


def _moe_ffn(block_e, n_used, x_slots, w1, w3, w2):
    E, D, F = w1.shape
    nblk = block_e.shape[0]
    slots = nblk * MOE_BLOCK
    idx = jnp.arange(nblk)
    first = jnp.concatenate([jnp.ones((1,), bool), block_e[1:] != block_e[:-1]])
    slot = ((jnp.cumsum(first) - 1) % WEIGHT_SLOTS).astype(jnp.int32)
    later_first = lax.cummin(jnp.where(first, idx, nblk)[::-1])[::-1]
    next_first = jnp.concatenate([later_first[1:], jnp.full((1,), nblk)])
    next_e = jnp.where(next_first < nblk, block_e[jnp.minimum(next_first, nblk - 1)], -1).astype(jnp.int32)
    blk = lambda j, be, first, slot, nxt, nu: (jnp.minimum(j, nu[0] - 1), 0, 0)
    grid_spec = pltpu.PrefetchScalarGridSpec(
        num_scalar_prefetch=5,
        grid=(nblk,),
        in_specs=[
            pl.BlockSpec((MOE_BLOCK, PACK_CHUNKS, LANES), blk),
            pl.BlockSpec(memory_space=pl.ANY),
            pl.BlockSpec(memory_space=pl.ANY),
            pl.BlockSpec(memory_space=pl.ANY),
        ],
        out_specs=pl.BlockSpec((MOE_BLOCK * TILE_ROWS, LANES), lambda j, *_: (j, 0)),
        scratch_shapes=[pltpu.VMEM((WEIGHT_SLOTS, D, F), F32), pltpu.VMEM((WEIGHT_SLOTS, D, F), F32),
                        pltpu.VMEM((WEIGHT_SLOTS, F, D), F32),
                        pltpu.VMEM((D, 2 * F), BF16), pltpu.VMEM((F, D), BF16),
                        pltpu.SemaphoreType.DMA((WEIGHT_SLOTS, 3))],
    )
    return pl.pallas_call(
        _moe_ffn_kernel,
        grid_spec=grid_spec,
        out_shape=jax.ShapeDtypeStruct((slots * TILE_ROWS, LANES), F32),
        compiler_params=pltpu.CompilerParams(
            dimension_semantics=("arbitrary",), vmem_limit_bytes=VMEM_LIMIT),
        name="moe_ffn",
    )(block_e, first.astype(jnp.int32), slot, next_e, n_used, x_slots, w1, w3, w2)


COMBINE_ROWS = 32


def _combine_kernel(dest_ref, gates_ref, u_ref, x_ref, gm_ref, w1_ref, w3_ref, w2_ref, ys_ref, o_ref, buf_ref, sem):
    tm, half = u_ref.shape

    def tile(i):
        return pl.ds(pl.multiple_of(i * TILE_ROWS, TILE_ROWS), TILE_ROWS)

    def issue(r, carry):
        for k in range(TOP_K):
            pltpu.make_async_copy(ys_ref.at[tile(dest_ref[0, 0, r * TOP_K + k])], buf_ref.at[k, tile(r)], sem).start()
        return carry

    lax.fori_loop(0, tm, issue, 0, unroll=ISSUE_UNROLL)

    ulo, uhi = _unpack_pair(u_ref[...])
    both = lambda w_ref: (jnp.dot(ulo, w_ref[:half], preferred_element_type=F32)
                          + jnp.dot(uhi, w_ref[half:], preferred_element_type=F32))
    h1, h3 = both(w1_ref), both(w3_ref)
    o_ref[...] = x_ref[...] + gm_ref[0] * _dot(h1 * jax.nn.sigmoid(h1) * h3, w2_ref[...])

    for k in range(TOP_K):
        pltpu.make_async_copy(ys_ref.at[pl.ds(0, tm * TILE_ROWS)], buf_ref.at[k], sem).wait()

    gm = gm_ref[0]
    for rb in range(tm // COMBINE_ROWS):
        rows = slice(rb * COMBINE_ROWS, (rb + 1) * COMBINE_ROWS)
        gates = gates_ref[rows, :]
        for c in range(TILE_ROWS):
            acc = jnp.zeros((COMBINE_ROWS, LANES), F32)
            for k in range(TOP_K):
                acc = acc + gates[:, k:k + 1] * buf_ref[
                    k, pl.ds(rb * COMBINE_ROWS * TILE_ROWS + c, COMBINE_ROWS, stride=TILE_ROWS), :]
            cols = slice(c * LANES, (c + 1) * LANES)
            o_ref[rows, cols] = o_ref[rows, cols] + gm[:, cols] * acc


def _combine(dest, gates, u_rows, x1, g_m, y_slots, p):
    B, T, D = x1.shape
    n_tok = B * T
    W = u_rows.shape[1]
    tiles_per_batch = T // ROW_TILE
    row = lambda i: (i, 0)
    const = lambda i: (0, 0)
    out = pl.pallas_call(
        _combine_kernel,
        grid=(n_tok // ROW_TILE,),
        in_specs=[
            pl.BlockSpec((1, 1, ROW_TILE * TOP_K), lambda i: (i, 0, 0), memory_space=pltpu.SMEM),
            pl.BlockSpec((ROW_TILE, LANES), row),
            pl.BlockSpec((ROW_TILE, W), row),
            pl.BlockSpec((ROW_TILE, D), row),
            pl.BlockSpec((1, 1, D), lambda i: (i // tiles_per_batch, 0, 0)),
            pl.BlockSpec(p['shared_w1'].shape, const),
            pl.BlockSpec(p['shared_w3'].shape, const),
            pl.BlockSpec(p['shared_w2'].shape, const),
            pl.BlockSpec(memory_space=pl.ANY),
        ],
        out_specs=pl.BlockSpec((ROW_TILE, D), row),
        out_shape=jax.ShapeDtypeStruct((n_tok, D), F32),
        scratch_shapes=[pltpu.VMEM((TOP_K, ROW_TILE * TILE_ROWS, LANES), F32), pltpu.SemaphoreType.DMA(())],
        compiler_params=pltpu.CompilerParams(
            dimension_semantics=("arbitrary",), vmem_limit_bytes=VMEM_LIMIT),
        name="combine",
    )(dest, gates.reshape(n_tok, LANES), u_rows, x1.reshape(n_tok, D), g_m[:, None, :],
      p['shared_w1'].astype(BF16), p['shared_w3'].astype(BF16), p['shared_w2'].astype(BF16), y_slots)
    return out.reshape(B, T, D)


def _moe(x1, sh_m, sc_m, g_m, p):
    B, T, D = x1.shape
    n_tok = B * T
    u_packed, u_tiles, ids, gates, ranks, counts = _router(x1, sh_m, sc_m, p)

    counts = counts[0].astype(jnp.int32)
    padded = (counts + MOE_BLOCK - 1) // MOE_BLOCK * MOE_BLOCK
    padded_end = jnp.cumsum(padded)
    base = padded_end - padded
    n_blocks = n_tok * TOP_K // MOE_BLOCK + N_EXPERTS
    n_used = (padded_end[-1] // MOE_BLOCK).astype(jnp.int32)
    blk = jnp.minimum(jnp.arange(n_blocks), n_used - 1) * MOE_BLOCK
    block_e = jnp.minimum(jnp.sum(padded_end[None, :] <= blk[:, None], axis=1), N_EXPERTS - 1).astype(jnp.int32)
    dest = _slots(ids.reshape(n_tok, LANES), ranks.reshape(n_tok, LANES), base.astype(F32).reshape(1, N_EXPERTS))
    dest = dest[:, :TOP_K].reshape(n_tok // ROW_TILE, 1, ROW_TILE * TOP_K)

    u_rows = u_packed.reshape(n_tok, D // 2)
    x_slots = _dispatch((base + counts).astype(jnp.int32), dest, u_tiles, n_blocks * MOE_BLOCK)
    y_slots = _moe_ffn(block_e, n_used.reshape(1), x_slots, p['expert_w1'], p['expert_w3'], p['expert_w2'])
    return _combine(dest, gates, u_rows, x1, g_m, y_slots, p)
```

```python
import functools

import jax
import jax.numpy as jnp
import numpy as np
from jax import lax
from jax.experimental import pallas as pl
from jax.experimental.pallas import tpu as pltpu

F32 = jnp.float32
BF16 = jnp.bfloat16

GRID_W = 64
EPS = 1e-6
MLA_HEADS = 8
QK_NOPE = 64
QK_ROPE = 32
QK_HEAD = QK_NOPE + QK_ROPE
V_HEAD = 64
Q_LORA = 256
KV_LORA = 128
ROPE_BASE = 10000.0
ATTN_SCALE = QK_HEAD ** -0.5
RW_HEADS = 8
RW_HEAD = 64
RW_DIM = RW_HEADS * RW_HEAD
DECAY_LORA = 64
AICL_LORA = 64
GATE_LORA = 160
GN_EPS = 64e-5
N_EXPERTS = 256
TOP_K = 8
N_GROUPS = 8
TOPK_GROUPS = 4
ROUTED_SCALE = 2.5

LANES = 128
VMEM_LIMIT = 56 * 1024 * 1024

WKV_CHUNK = 64
WKV_GROUP = LANES
HEADS_PER_GROUP = WKV_GROUP // RW_HEAD
HEAD_PAD = LANES


def _dot(a, b):
    return jnp.dot(a.astype(BF16), b.astype(BF16), preferred_element_type=F32)


def _dot_t(a, b):
    return lax.dot_general(a.astype(BF16), b.astype(BF16), (((1,), (1,)), ((), ())),
                           preferred_element_type=F32)


def _split3(x):
    h = x.astype(BF16)
    r1 = x - h.astype(F32)
    m = r1.astype(BF16)
    lo = (r1 - m.astype(F32)).astype(BF16)
    return h, m, lo


def _dot_hi(a_bf16_exact, x):
    h, m, lo = _split3(x)
    d = lambda y: jnp.dot(a_bf16_exact, y, preferred_element_type=F32)
    return d(h) + d(m) + d(lo)


def _wkv_chunk(r, v, kk, lw, bk, kd, s0, reverse, emit):
    L, G = r.shape
    row = lax.broadcasted_iota(jnp.int32, (L, G), 0)
    lane = lax.broadcasted_iota(jnp.int32, (L, G), 1)
    diff = (lane % L - row) if reverse else (row - lane % L)
    strict = diff > 0
    incl = diff >= 0
    r2 = lax.broadcasted_iota(jnp.int32, (L, L), 0)
    c2 = lax.broadcasted_iota(jnp.int32, (L, L), 1)
    tri = jnp.where(((c2 - r2) if reverse else (r2 - c2)) >= 0, 1.0, 0.0).astype(BF16)
    tri_ones = jnp.concatenate([tri, jnp.ones((L, L), BF16)], axis=0)
    lane_head = lane // RW_HEAD
    rowb = lax.broadcasted_iota(jnp.int32, (G, G), 0) // RW_HEAD
    colb = lax.broadcasted_iota(jnp.int32, (G, G), 1) // RW_HEAD

    def stack(x):
        return jnp.concatenate(
            [jnp.where(lane_head == h, x, 0.0) for h in range(HEADS_PER_GROUP)], axis=0).astype(BF16)

    sums = _dot_hi(tri_ones, lw)
    cum_in = sums[:L]
    g_in = jnp.exp(cum_in)
    g_inv = jnp.exp(-cum_in)
    g_ex = jnp.exp(cum_in - lw)
    g_tot = jnp.exp(sums[L:])
    a_h = -kk * g_ex
    b_h = bk * g_inv
    k_h = kd * g_inv
    r_h = r * g_in
    yield

    gram = _dot_t(jnp.concatenate([a_h, r_h], axis=0),
                  jnp.concatenate([stack(b_h), stack(k_h)], axis=0))
    ab = jnp.where(strict, gram[:L, :G], 0.0)
    ak = jnp.where(strict, gram[:L, G:], 0.0)
    rb = jnp.where(incl, gram[L:, :G], 0.0)
    rk = jnp.where(incl, gram[L:, G:], 0.0)
    yield

    tm = jnp.where(diff == 0, 1.0, 0.0) + ab
    p = ab
    v_st = stack(v)
    akv = _dot(ak, v_st)
    for _ in range(int(np.log2(L)) - 1):
        p = _dot(p, stack(p))
        yield
        tm = tm + _dot(tm, stack(p))
        yield

    wu = _dot(tm, jnp.concatenate([stack(a_h), stack(akv)], axis=1))
    w_t, u_t = wu[:, :G], wu[:, G:]
    yield
    rbwu = _dot(rb, jnp.concatenate([stack(w_t), stack(u_t)], axis=1))
    r_t = r_h + rbwu[:, :G]
    y_t = rbwu[:, G:] + _dot(rk, v_st)
    yield

    s0_st = stack(s0)
    y = _dot_t(r_t, s0_st) + y_t
    u = _dot_t(w_t, s0_st) + u_t
    yield
    uv = jnp.concatenate([u, v], axis=0)
    bkc = jnp.concatenate([b_h, k_h], axis=0)
    upd = lax.dot_general(uv.astype(BF16), bkc.astype(BF16), (((0,), (0,)), ((), ())),
                          preferred_element_type=F32)
    upd = jnp.where(rowb == colb, upd, 0.0)
    upd_d = upd[0:L]
    for h in range(1, HEADS_PER_GROUP):
        upd_d = upd_d + upd[h * L:(h + 1) * L]
    emit(y, (s0 + upd_d) * g_tot)


def _wkv_kernel(*refs, n_groups, n_batch):
    ins, (yf_ref, yb_ref, s_ref) = refs[:12], refs[12:]
    G = WKV_GROUP

    @pl.when(pl.program_id(1) == 0)
    def _():
        s_ref[...] = jnp.zeros_like(s_ref)

    chains = []
    for d, y_ref in enumerate((yf_ref, yb_ref)):
        r_ref, v_ref, kk_ref, lw_ref, bk_ref, kd_ref = ins[6 * d:6 * d + 6]
        for bi in range(n_batch):
            for g in range(n_groups):
                sl = slice(g * G, (g + 1) * G)

                def emit(y, s_new, y_ref=y_ref, d=d, bi=bi, g=g, sl=sl):
                    y_ref[bi, :, sl] = y
                    s_ref[d, bi, g] = s_new

                chains.append(_wkv_chunk(
                    r_ref[bi, :, sl], v_ref[bi, :, sl], kk_ref[bi, :, sl], lw_ref[0, bi, :, sl],
                    bk_ref[0, bi, :, sl], kd_ref[0, bi, :, sl], s_ref[d, bi, g], d == 1, emit))
    while chains:
        chains = [c for c in chains if next(c, StopIteration) is not StopIteration]


WKV_BATCH = 4


def _wkv_scan(r, v, kk, lw, bk, kd, n_ctx):
    B, Ttot, C = r.shape
    L = WKV_CHUNK
    nc = Ttot // L
    ncc = n_ctx // L
    nl = nc - ncc
    n_groups = C // WKV_GROUP
    nb = WKV_BATCH

    cid = (lambda s: s, lambda s: jnp.where(s < ncc, ncc - 1 - s, nc + ncc - 1 - s))
    first_out = (0, nl - 1)
    in_specs, out_specs = [], []
    for d in range(2):
        shared = pl.BlockSpec((nb, L, C), lambda b, s, d=d: (b, cid[d](s), 0))
        per_dir = pl.BlockSpec((1, nb, L, C), lambda b, s, d=d: (d, b, cid[d](s), 0))
        in_specs += [shared, shared, shared, per_dir, per_dir, per_dir]
        out_specs.append(pl.BlockSpec(
            (nb, L, C), lambda b, s, d=d: (b, jnp.where(s < ncc, first_out[d], cid[d](s) - ncc), 0)))
    out = jax.ShapeDtypeStruct((B, nl * L, C), F32)
    return pl.pallas_call(
        functools.partial(_wkv_kernel, n_groups=n_groups, n_batch=nb),
        grid=(B // nb, nc),
        in_specs=in_specs,
        out_specs=out_specs,
        out_shape=[out, out],
        scratch_shapes=[pltpu.VMEM((2, nb, n_groups, L, WKV_GROUP), F32)],
        compiler_params=pltpu.CompilerParams(
            dimension_semantics=("arbitrary", "arbitrary"), vmem_limit_bytes=VMEM_LIMIT),
        name="wkv_scan",
    )(r, v, kk, lw, bk, kd, r, v, kk, lw, bk, kd)


def kernel(x, c, ctx, c_ctx, ada_w, ada_b, norm_mix, norm_ffn, w_in, shift_conv, q_lat_norm, w_uq, kv_lat_norm, w_ukv, q_norm, k_norm, w_o_mla, decay_w0, decay_w2, aicl_a0, aicl_a2, k_k, k_a, r_k, gn_w, gn_b, gate_g2, w_o_rwkv, w_out, router_w, router_bias, expert_w1, expert_w3, expert_w2, shared_w1, shared_w3, shared_w2):
    B, T, D = x.shape
    n_ctx = ctx.shape[1]
    i0 = 0
    p = dict(norm_mix=norm_mix[i0], norm_ffn=norm_ffn[i0], w_in=w_in[i0], shift_conv=shift_conv[i0],
             q_lat_norm=q_lat_norm[i0], w_uq=w_uq[i0], kv_lat_norm=kv_lat_norm[i0], w_ukv=w_ukv[i0],
             q_norm=q_norm[i0], k_norm=k_norm[i0], w_o_mla=w_o_mla[i0],
             decay_w0=decay_w0[i0], decay_w2=decay_w2[i0], aicl_a0=aicl_a0[i0], aicl_a2=aicl_a2[i0],
             k_k=k_k[i0], k_a=k_a[i0], r_k=r_k[i0], gn_w=gn_w[i0], gn_b=gn_b[i0], gate_g2=gate_g2[i0],
             w_o_rwkv=w_o_rwkv[i0], w_out=w_out[i0], router_w=router_w[i0], router_bias=router_bias[i0],
             expert_w1=expert_w1[i0], expert_w3=expert_w3[i0], expert_w2=expert_w2[i0],
             shared_w1=shared_w1[i0], shared_w3=shared_w3[i0], shared_w2=shared_w2[i0])

    mod = _ada_modulation(c, c_ctx, ada_w[i0], ada_b[i0])
    sh_a, sc_a, g_a, sh_m, sc_m, g_m = jnp.split(mod[:B], 6, axis=-1)
    csh_a, csc_a = jnp.split(mod[B], 6, axis=-1)[:2]
    sh2 = jnp.stack([jnp.broadcast_to(csh_a, (B, D)), sh_a], axis=1)
    sc2 = jnp.stack([jnp.broadcast_to(csc_a, (B, D)), sc_a], axis=1)

    proj = _in_proj(ctx, x, sh2, sc2, p['norm_mix'], _pack_w_in(p['w_in']))

    q, k, v = _mla_prep(proj, p, n_ctx, T)
    att = _attention(q, k, v)

    r, vv, kk, lw, bk, kd, bonus, gg = _rwkv_prep(proj, p, n_ctx)
    y_f, y_b = _wkv_scan(r, vv, kk, lw, bk, kd, n_ctx)

    x1 = _merge(att, y_f, y_b, bonus, gg, proj, x, g_a, p, n_ctx)
    return _moe(x1, sh_m, sc_m, g_m, p)


SUBLANES = 8


def _ada_kernel(c_ref, w_ref, b_ref, o_ref):
    cc = c_ref[...]
    w = w_ref[...]
    w_hi = w.astype(BF16)
    w_lo = (w - w_hi.astype(F32)).astype(BF16)
    o_ref[...] = _dot3(cc * jax.nn.sigmoid(cc), w_hi, w_lo) + b_ref[...]


def _ada_modulation(c, c_ctx, ada_w, ada_b):
    B, D = c.shape
    n_out = ada_w.shape[1]
    rows = -(-(B + 1) // SUBLANES) * SUBLANES
    c_all = jnp.concatenate([c, c_ctx[None, :], jnp.zeros((rows - B - 1, D), F32)], axis=0)
    out = pl.pallas_call(
        _ada_kernel,
        grid=(n_out // D,),
        in_specs=[pl.BlockSpec((rows, D), lambda j: (0, 0)),
                  pl.BlockSpec((D, D), lambda j: (0, j)),
                  pl.BlockSpec((1, D), lambda j: (0, j))],
        out_specs=pl.BlockSpec((rows, D), lambda j: (0, j)),
        out_shape=jax.ShapeDtypeStruct((rows, n_out), F32),
        compiler_params=pltpu.CompilerParams(dimension_semantics=("arbitrary",), vmem_limit_bytes=VMEM_LIMIT),
        name="ada_modulation",
    )(c_all, ada_w, ada_b.reshape(1, n_out))
    return out[:B + 1]


ROW_TILE = 256
PROJ_RW = 2048
PROJ_GATE = 2048
PROJ_MLA = 512
PROJ_W = PROJ_RW + PROJ_GATE + PROJ_MLA
MLA_IN = Q_LORA + KV_LORA + QK_ROPE
RW_SPLITS = (RW_DIM, RW_DIM, RW_DIM, DECAY_LORA, DECAY_LORA, AICL_LORA, AICL_LORA, GATE_LORA)
RW_IN = sum(RW_SPLITS)


def _pack_w_in(w_in):
    w_mla = w_in[:, :MLA_IN]
    w_rw = w_in[:, MLA_IN:MLA_IN + RW_IN]
    w_gate = w_in[:, MLA_IN + RW_IN:]
    pad = lambda w, n: jnp.pad(w, ((0, 0), (0, n - w.shape[1])))
    return jnp.concatenate([pad(w_rw, PROJ_RW), w_gate, pad(w_mla, PROJ_MLA)], axis=1).astype(BF16)


PROJ_PIECES = 3
PROJ_COL_CHUNK = PROJ_W // 3


def _in_proj_kernel(ctx_ref, *refs):
    x_refs, (sh_ref, sc_ref, gain_ref, w_ref, o_ref) = refs[:PROJ_PIECES], refs[PROJ_PIECES:]
    first = pl.program_id(1) == 0
    pieces = []
    for n, x_ref in enumerate(x_refs):
        is_ctx = first if n == 0 else False
        xt = jnp.where(is_ctx, ctx_ref[0], x_ref[0]) if n == 0 else x_ref[0]
        sh = jnp.where(is_ctx, sh_ref[0, 0:1], sh_ref[0, 1:2])
        sc = jnp.where(is_ctx, sc_ref[0, 0:1], sc_ref[0, 1:2])
        y = xt * lax.rsqrt(jnp.mean(xt * xt, axis=-1, keepdims=True) + EPS) * gain_ref[...]
        pieces.append((y * (1.0 + sc) + sh).astype(BF16))
    h = jnp.concatenate(pieces, axis=0)
    for c in range(PROJ_W // PROJ_COL_CHUNK):
        cols = slice(c * PROJ_COL_CHUNK, (c + 1) * PROJ_COL_CHUNK)
        o_ref[0, :, cols] = jnp.dot(h, w_ref[:, cols], preferred_element_type=F32).astype(o_ref.dtype)


def _in_proj(ctx, x, sh2, sc2, gain, w):
    B, T, D = x.shape
    n_ctx = ctx.shape[1]
    rows = PROJ_PIECES * ROW_TILE
    assert n_ctx == ROW_TILE and (n_ctx + T) % rows == 0
    nt = (n_ctx + T) // rows
    piece = lambda n: pl.BlockSpec(
        (1, ROW_TILE, D), lambda b, i, n=n: (b, jnp.maximum(PROJ_PIECES * i + n - 1, 0), 0))
    return pl.pallas_call(
        _in_proj_kernel,
        grid=(B, nt),
        in_specs=[pl.BlockSpec((1, ROW_TILE, D), lambda b, i: (b, 0, 0))]
                 + [piece(n) for n in range(PROJ_PIECES)]
                 + [pl.BlockSpec((1, 2, D), lambda b, i: (b, 0, 0)),
                    pl.BlockSpec((1, 2, D), lambda b, i: (b, 0, 0)),
                    pl.BlockSpec((1, D), lambda b, i: (0, 0)),
                    pl.BlockSpec((D, PROJ_W), lambda b, i: (0, 0))],
        out_specs=pl.BlockSpec((1, rows, PROJ_W), lambda b, i: (b, i, 0)),
        out_shape=jax.ShapeDtypeStruct((B, n_ctx + T, PROJ_W), BF16),
        compiler_params=pltpu.CompilerParams(
            dimension_semantics=("arbitrary", "arbitrary"), vmem_limit_bytes=VMEM_LIMIT),
        name="in_proj",
    )(ctx, *([x] * PROJ_PIECES), sh2, sc2, gain.reshape(1, D), w)


def _rms(x, gain):
    return x * lax.rsqrt(jnp.mean(x * x, axis=-1, keepdims=True) + EPS) * gain


def _rope_tables(n_tokens):
    rows = n_tokens // GRID_W
    row = jnp.repeat(jnp.arange(rows, dtype=F32), GRID_W)
    col = jnp.tile(jnp.arange(GRID_W, dtype=F32), rows)
    n_freq = QK_ROPE // 4
    inv_freq = ROPE_BASE ** (-jnp.arange(n_freq, dtype=F32) / n_freq)
    ang = jnp.concatenate([row[:, None] * inv_freq, col[:, None] * inv_freq], axis=-1)
    return jnp.cos(ang), jnp.sin(ang)


ROPE_HALF = QK_ROPE // 2
X1 = slice(QK_NOPE, QK_NOPE + ROPE_HALF)
X2 = slice(QK_NOPE + ROPE_HALF, QK_HEAD)


def _rot_cols(w):
    w3 = w.reshape(w.shape[0], MLA_HEADS, HEAD_PAD)
    rot = jnp.zeros_like(w3).at[:, :, X1].set(-w3[:, :, X2]).at[:, :, X2].set(w3[:, :, X1])
    return rot.reshape(w.shape)


def _swap_halves(g):
    return jnp.zeros_like(g).at[:, X1].set(g[:, X2]).at[:, X2].set(g[:, X1])


def _mla_weights(p):
    H = MLA_HEADS
    wq = jnp.pad(p['w_uq'].reshape(Q_LORA, H, QK_HEAD), ((0, 0), (0, 0), (0, HEAD_PAD - QK_HEAD)))
    wq = wq.reshape(Q_LORA, H * HEAD_PAD)
    wkv = p['w_ukv'].reshape(KV_LORA, H, QK_NOPE + V_HEAD)
    wk_lat = jnp.pad(wkv[:, :, :QK_NOPE], ((0, 0), (0, 0), (0, HEAD_PAD - QK_NOPE)))
    place = jnp.zeros((LANES, H, HEAD_PAD), F32).at[:QK_ROPE, :, QK_NOPE:QK_HEAD].set(
        jnp.broadcast_to(jnp.eye(QK_ROPE, dtype=F32)[:, None, :], (QK_ROPE, H, QK_ROPE)))
    wk = jnp.concatenate([wk_lat, place], axis=0).reshape(KV_LORA + LANES, H * HEAD_PAD)
    wv = wkv[:, :, QK_NOPE:].reshape(KV_LORA, H * V_HEAD)
    gq = jnp.pad(p['q_norm'], (0, HEAD_PAD - QK_HEAD)).reshape(1, HEAD_PAD)
    gk = jnp.pad(p['k_norm'], (0, HEAD_PAD - QK_HEAD)).reshape(1, HEAD_PAD)
    bf = lambda w: w.astype(BF16)
    return (bf(wq), bf(_rot_cols(wq)), bf(wk), bf(_rot_cols(wk)), bf(wv), gq, _swap_halves(gq), gk, _swap_halves(gk))


def _mla_tables(n_ctx, T):
    cos, sin = _rope_tables(T)
    c = jnp.ones((n_ctx + T, HEAD_PAD), F32).at[n_ctx:, X1].set(cos).at[n_ctx:, X2].set(cos)
    s = jnp.zeros((n_ctx + T, HEAD_PAD), F32).at[n_ctx:, X1].set(sin).at[n_ctx:, X2].set(sin)
    return c, s


def _mla_prep_kernel(x_ref, c_ref, s_ref, qlg_ref, kvg_ref, wq_ref, wqr_ref, wk_ref, wkr_ref, wv_ref,
                     gq_ref, gqp_ref, gk_ref, gkp_ref, q_ref, k_ref, v_ref):
    x = x_ref[0].astype(F32)
    ql = _rms(x[:, :Q_LORA], qlg_ref[...])
    kvl = _rms(x[:, Q_LORA:Q_LORA + KV_LORA], kvg_ref[...])
    k_in = jnp.concatenate([kvl, x[:, Q_LORA + KV_LORA:]], axis=1)
    cos, sin = c_ref[...], s_ref[...]

    def finish(raw, partner, g, g_swapped, scale, o_ref):
        gc, gs = g * cos, g_swapped * sin
        for h in range(MLA_HEADS):
            sl = slice(h * HEAD_PAD, (h + 1) * HEAD_PAD)
            rh = raw[:, sl]
            inv = lax.rsqrt(jnp.sum(rh * rh, axis=-1, keepdims=True) * (1.0 / QK_HEAD) + EPS) * scale
            o_ref[0, :, sl] = ((rh * gc + partner[:, sl] * gs) * inv).astype(o_ref.dtype)

    finish(_dot(ql, wq_ref[...]), _dot(ql, wqr_ref[...]), gq_ref[...], gqp_ref[...], ATTN_SCALE, q_ref)
    finish(_dot(k_in, wk_ref[...]), _dot(k_in, wkr_ref[...]), gk_ref[...], gkp_ref[...], 1.0, k_ref)
    v_ref[0] = _dot(kvl, wv_ref[...]).astype(v_ref.dtype)


def _mla_prep(proj, p, n_ctx, T):
    B, Tt, _ = proj.shape
    H = MLA_HEADS
    ctx_tiles = n_ctx // ROW_TILE
    mla_blk = (PROJ_RW + PROJ_GATE) // PROJ_MLA
    weights = _mla_weights(p)
    cos, sin = _mla_tables(n_ctx, T)
    const = lambda b, i: (0, 0)
    row = lambda b, i: (b, i, 0)
    tab = pl.BlockSpec((ROW_TILE, HEAD_PAD), lambda b, i: (i, 0))
    full = lambda a: pl.BlockSpec(a.shape, const)
    return pl.pallas_call(
        _mla_prep_kernel,
        grid=(B, Tt // ROW_TILE),
        in_specs=[pl.BlockSpec((1, ROW_TILE, PROJ_MLA), lambda b, i: (b, i, mla_blk)), tab, tab,
                  pl.BlockSpec((1, Q_LORA), const), pl.BlockSpec((1, KV_LORA), const)]
                 + [full(w) for w in weights],
        out_specs=[
            pl.BlockSpec((1, ROW_TILE, H * HEAD_PAD), lambda b, i: (b, jnp.maximum(i - ctx_tiles, 0), 0)),
            pl.BlockSpec((1, ROW_TILE, H * HEAD_PAD), row),
            pl.BlockSpec((1, ROW_TILE, H * V_HEAD), row),
        ],
        out_shape=[
            jax.ShapeDtypeStruct((B, T, H * HEAD_PAD), BF16),
            jax.ShapeDtypeStruct((B, Tt, H * HEAD_PAD), BF16),
            jax.ShapeDtypeStruct((B, Tt, H * V_HEAD), BF16),
        ],
        compiler_params=pltpu.CompilerParams(
            dimension_semantics=("arbitrary", "arbitrary"), vmem_limit_bytes=VMEM_LIMIT),
        name="mla_prep",
    )(proj, cos, sin, p['q_lat_norm'].reshape(1, Q_LORA), p['kv_lat_norm'].reshape(1, KV_LORA), *weights)


ATTN_Q_TILE = 2048
HEADS_PER_STEP = LANES // V_HEAD


ATTN_ROW_SPLIT = 16


def _attn_kernel(q_ref, k_ref, v_ref, o_ref):
    v2 = v_ref[0]
    rows = q_ref.shape[1] // ATTN_ROW_SPLIT
    work = [(hh, rs) for rs in range(ATTN_ROW_SPLIT) for hh in range(HEADS_PER_STEP)]

    def scores(hh, rs):
        sl = slice(hh * HEAD_PAD, (hh + 1) * HEAD_PAD)
        return _dot_t(q_ref[0, rs * rows:(rs + 1) * rows, sl], k_ref[0, :, sl])

    outs = {}
    s_next = scores(*work[0])
    for i, (hh, rs) in enumerate(work):
        s = s_next
        if i + 1 < len(work):
            s_next = scores(*work[i + 1])
        e = jnp.exp(s - jnp.max(s, axis=-1, keepdims=True))
        outs[hh, rs] = _dot(e, v2) / jnp.sum(e, axis=-1, keepdims=True)
    lane = lax.broadcasted_iota(jnp.int32, (rows, LANES), 1)
    for rs in range(ATTN_ROW_SPLIT):
        o_ref[0, rs * rows:(rs + 1) * rows] = jnp.where(lane < V_HEAD, outs[0, rs], outs[1, rs])


def _attention(q, k, v):
    B, T, _ = q.shape
    Kt = k.shape[1]
    assert T % ATTN_Q_TILE == 0 and ATTN_Q_TILE % (ATTN_ROW_SPLIT * SUBLANES) == 0
    hp = MLA_HEADS // HEADS_PER_STEP
    qw = HEADS_PER_STEP * HEAD_PAD
    return pl.pallas_call(
        _attn_kernel,
        grid=(B, hp, T // ATTN_Q_TILE),
        in_specs=[
            pl.BlockSpec((1, ATTN_Q_TILE, qw), lambda b, h, i: (b, i, h)),
            pl.BlockSpec((1, Kt, qw), lambda b, h, i: (b, 0, h)),
            pl.BlockSpec((1, Kt, LANES), lambda b, h, i: (b, 0, h)),
        ],
        out_specs=pl.BlockSpec((1, ATTN_Q_TILE, LANES), lambda b, h, i: (b, i, h)),
        out_shape=jax.ShapeDtypeStruct((B, T, MLA_HEADS * V_HEAD), F32),
        compiler_params=pltpu.CompilerParams(
            dimension_semantics=("arbitrary", "arbitrary", "arbitrary"), vmem_limit_bytes=VMEM_LIMIT),
        name="attention",
    )(q, k, v)


HALO = 16
LORA_W = LANES
GATE_W = PROJ_RW - 3 * RW_DIM - 2 * LORA_W


def _head_ones(width, head):
    i = np.arange(width) // head
    return jnp.asarray(i[:, None] == i[None, :], BF16)


def _head_sum(x, ones_bd):
    hi = x.astype(BF16)
    lo = (x - hi.astype(F32)).astype(BF16)
    return (jnp.dot(hi, ones_bd, preferred_element_type=F32)
            + jnp.dot(lo, ones_bd, preferred_element_type=F32))


def _dot3(a, b_hi, b_lo):
    hi = a.astype(BF16)
    lo = (a - hi.astype(F32)).astype(BF16)
    d = lambda u, w: jnp.dot(u, w, preferred_element_type=F32)
    return d(hi, b_hi) + (d(hi, b_lo) + d(lo, b_hi))


def _rwkv_prep_kernel(x_ref, prev_ref, next_ref, conv_ref, kkg_ref, ka_ref, rk_ref, w0_ref, a0_ref,
                      w2h_ref, w2l_ref, a2_ref, g2_ref, ones_ref,
                      r_ref, v_ref, kk_ref, lw_ref, bk_ref, kd_ref, bonus_ref, gg_ref, *, ctx_tiles, n_tiles):
    i = pl.program_id(1)
    x = x_ref[0].astype(F32)
    tm, W = x.shape
    C = RW_DIM
    first = (i == 0) | (i == ctx_tiles)
    last = (i == ctx_tiles - 1) | (i == n_tiles - 1)
    prev_row = jnp.where(first, 0.0, prev_ref[0, HALO - 1:HALO].astype(F32))
    next_row = jnp.where(last, 0.0, next_ref[0, 0:1].astype(F32))
    row = lax.broadcasted_iota(jnp.int32, (tm, W), 0)
    x_dn = jnp.where(row == 0, prev_row, pltpu.roll(x, 1, 0))
    x_up = jnp.where(row == tm - 1, next_row, pltpu.roll(x, tm - 1, 0))
    xc = x_dn * conv_ref[0:1] + x * conv_ref[1:2] + x_up * conv_ref[2:3]

    r, k, v = xc[:, :C], xc[:, C:2 * C], xc[:, 2 * C:3 * C]
    lora_w = jnp.tanh(xc[:, 3 * C:3 * C + LORA_W])
    lora_a = xc[:, 3 * C + LORA_W:3 * C + 2 * LORA_W]
    lg = xc[:, 3 * C + 2 * LORA_W:]
    ones_bd = ones_ref[...]
    kq = k * kkg_ref[...]
    kk = kq * lax.rsqrt(_head_sum(kq * kq, ones_bd) + 1e-12)
    r_ref[0], v_ref[0], kk_ref[0] = r, v, kk

    k_sum = jnp.zeros_like(k)
    for d in range(2):
        z = w0_ref[d:d + 1] + _dot3(lora_w, w2h_ref[d], w2l_ref[d])
        softplus_neg = jnp.maximum(-z, 0.0) + jnp.log(1.0 + jnp.exp(-jnp.abs(z)))
        lw_ref[d, 0] = -jnp.exp(-softplus_neg - 0.5)
        a = jax.nn.sigmoid(a0_ref[d:d + 1] + _dot(lora_a, a2_ref[d]))
        kd = k * (1.0 + (a - 1.0) * ka_ref[...])
        bk_ref[d, 0] = kk * a
        kd_ref[d, 0] = kd
        k_sum = k_sum + kd
    bonus_ref[0] = _head_sum(r * k_sum * rk_ref[...], ones_bd) * v
    gg_ref[0] = _dot(jax.nn.sigmoid(lg), g2_ref[...])


def _rwkv_prep(proj, p, n_ctx):
    B, Tt, _ = proj.shape
    C = RW_DIM
    nt = Tt // ROW_TILE
    hb = ROW_TILE // HALO
    pad_cols = lambda w: jnp.pad(w, ((0, 0), (0, PROJ_RW - w.shape[1])))
    w2 = jnp.stack([jnp.pad(p['decay_w2'][0], ((0, LORA_W - DECAY_LORA), (0, 0))),
                    jnp.pad(p['decay_w2'][1], ((DECAY_LORA, 0), (0, 0)))])
    w2h = w2.astype(BF16)
    w2l = (w2 - w2h.astype(F32)).astype(BF16)
    a2 = jnp.stack([jnp.pad(p['aicl_a2'][0], ((0, LORA_W - AICL_LORA), (0, 0))),
                    jnp.pad(p['aicl_a2'][1], ((AICL_LORA, 0), (0, 0)))]).astype(BF16)
    g2 = jnp.pad(p['gate_g2'], ((0, GATE_W - GATE_LORA), (0, 0))).astype(BF16)
    row = lambda b, i: (b, i, 0)
    drow = lambda b, i: (0, b, i, 0)
    const2 = lambda b, i: (0, 0)
    const3 = lambda b, i: (0, 0, 0)
    vec = pl.BlockSpec((1, C), const2)
    out_row = pl.BlockSpec((1, ROW_TILE, C), row)
    out_dir = pl.BlockSpec((2, 1, ROW_TILE, C), drow)
    sds = jax.ShapeDtypeStruct((B, Tt, C), F32)
    sds2 = jax.ShapeDtypeStruct((2, B, Tt, C), F32)
    ctx_tiles = n_ctx // ROW_TILE
    out_lat = pl.BlockSpec((1, ROW_TILE, C), lambda b, i: (b, jnp.maximum(i - ctx_tiles, 0), 0))
    sds_lat = jax.ShapeDtypeStruct((B, Tt - n_ctx, C), F32)
    return pl.pallas_call(
        functools.partial(_rwkv_prep_kernel, ctx_tiles=n_ctx // ROW_TILE, n_tiles=nt),
        grid=(B, nt),
        in_specs=[
            pl.BlockSpec((1, ROW_TILE, PROJ_RW), row),
            pl.BlockSpec((1, HALO, PROJ_RW), lambda b, i: (b, jnp.maximum(i * hb - 1, 0), 0)),
            pl.BlockSpec((1, HALO, PROJ_RW), lambda b, i: (b, jnp.minimum((i + 1) * hb, nt * hb - 1), 0)),
            pl.BlockSpec((3, PROJ_RW), const2),
            vec, vec, vec,
            pl.BlockSpec((2, C), const2),
            pl.BlockSpec((2, C), const2),
            pl.BlockSpec((2, LORA_W, C), const3),
            pl.BlockSpec((2, LORA_W, C), const3),
            pl.BlockSpec((2, LORA_W, C), const3),
            pl.BlockSpec((GATE_W, C), const2),
            pl.BlockSpec((C, C), const2),
        ],
        out_specs=[out_row, out_row, out_row, out_dir, out_dir, out_dir, out_lat, out_lat],
        out_shape=[sds, sds, sds, sds2, sds2, sds2, sds_lat, sds_lat],
        compiler_params=pltpu.CompilerParams(
            dimension_semantics=("arbitrary", "arbitrary"), vmem_limit_bytes=VMEM_LIMIT),
        name="rwkv_prep",
    )(proj, proj, proj, pad_cols(p['shift_conv']), p['k_k'].reshape(1, C), p['k_a'].reshape(1, C),
      p['r_k'].reshape(1, C), p['decay_w0'], p['aicl_a0'], w2h, w2l, a2, g2, _head_ones(C, RW_HEAD))


MERGE_PIECES = 2


def _merge_kernel(att_ref, yf_ref, yb_ref, bonus_ref, gg_ref, *refs):
    gate_refs, (x_ref, ga_ref, gnw_ref, gnb_ref, ones_ref, wm_ref, wr_ref, wo_ref, o_ref) = (
        refs[:MERGE_PIECES], refs[MERGE_PIECES:])
    D = x_ref.shape[-1]
    ones_bd = ones_ref[...]
    y = yf_ref[0] + yb_ref[0]
    yc = y - _head_sum(y, ones_bd) * (1.0 / RW_HEAD)
    var = _head_sum(yc * yc, ones_bd) * (1.0 / RW_HEAD)
    y_n = yc * lax.rsqrt(var + GN_EPS) * gnw_ref[...] + gnb_ref[...]
    rw = (y_n + bonus_ref[0]) * gg_ref[0]
    g = jax.nn.sigmoid(jnp.concatenate([r[0] for r in gate_refs], axis=0).astype(F32))
    mix = g[:, :D] * _dot(att_ref[0], wm_ref[...]) + g[:, D:] * _dot(rw, wr_ref[...])
    o_ref[0] = x_ref[0] + ga_ref[0] * _dot(mix, wo_ref[...])


def _merge(att, y_f, y_b, bonus, gg, proj, x, g_a, p, n_ctx):
    B, T, D = x.shape
    C = RW_DIM
    rows = MERGE_PIECES * ROW_TILE
    ctx_tiles = n_ctx // ROW_TILE
    gate_blk = PROJ_RW // PROJ_GATE
    row = lambda b, i: (b, i, 0)
    const = lambda b, i: (0, 0)
    gate_piece = lambda n: pl.BlockSpec(
        (1, ROW_TILE, PROJ_GATE), lambda b, i, n=n: (b, MERGE_PIECES * i + n + ctx_tiles, gate_blk))
    return pl.pallas_call(
        _merge_kernel,
        grid=(B, T // rows),
        in_specs=[
            pl.BlockSpec((1, rows, att.shape[-1]), row),
            pl.BlockSpec((1, rows, C), row),
            pl.BlockSpec((1, rows, C), row),
            pl.BlockSpec((1, rows, C), row),
            pl.BlockSpec((1, rows, C), row),
        ] + [gate_piece(n) for n in range(MERGE_PIECES)] + [
            pl.BlockSpec((1, rows, D), row),
            pl.BlockSpec((1, 1, D), lambda b, i: (b, 0, 0)),
            pl.BlockSpec((1, C), const),
            pl.BlockSpec((1, C), const),
            pl.BlockSpec((C, C), const),
            pl.BlockSpec(p['w_o_mla'].shape, const),
            pl.BlockSpec(p['w_o_rwkv'].shape, const),
            pl.BlockSpec(p['w_out'].shape, const),
        ],
        out_specs=pl.BlockSpec((1, rows, D), row),
        out_shape=jax.ShapeDtypeStruct((B, T, D), F32),
        compiler_params=pltpu.CompilerParams(
            dimension_semantics=("arbitrary", "arbitrary"), vmem_limit_bytes=VMEM_LIMIT),
        name="merge",
    )(att, y_f, y_b, bonus, gg, *([proj] * MERGE_PIECES), x, g_a[:, None, :], p['gn_w'].reshape(1, C),
      p['gn_b'].reshape(1, C), _head_ones(C, RW_HEAD), p['w_o_mla'].astype(BF16), p['w_o_rwkv'].astype(BF16),
      p['w_out'].astype(BF16))


PACK_CHUNKS = 4
TILE_ROWS = 8
MOE_BLOCK = 512
ISSUE_UNROLL = 4
DMA_PRIORITIES = 2


ROUTER_TILE = 1024
NEG_INF = float("-inf")
HI_MASK = 0xFFFF0000


def _pack_pair(lo, hi):
    lo_b = lax.bitcast_convert_type(lo.astype(BF16).astype(F32), jnp.uint32)
    hi_b = lax.bitcast_convert_type(hi.astype(BF16).astype(F32), jnp.uint32)
    return (lo_b >> 16) | (hi_b & jnp.uint32(HI_MASK))


def _unpack_pair(w):
    lo = lax.bitcast_convert_type(w << 16, F32)
    hi = lax.bitcast_convert_type(w & jnp.uint32(HI_MASK), F32)
    return lo.astype(BF16), hi.astype(BF16)


def _row_max(x):
    return jnp.max(x, axis=-1, keepdims=True)


def _first_index_of(x, value, lane_f):
    return jnp.min(jnp.where(x == value, lane_f, float(x.shape[-1])), axis=-1, keepdims=True)


def _router_kernel(x_ref, gain_ref, sh_ref, sc_ref, wh_ref, wm_ref, wl_ref, bias_ref,
                   u_ref, ut_ref, ids_ref, gates_ref, ranks_ref, counts_ref, carry_ref):
    @pl.when((pl.program_id(0) == 0) & (pl.program_id(1) == 0))
    def _():
        carry_ref[...] = jnp.zeros_like(carry_ref)

    x = x_ref[0]
    tm, D = x.shape
    E = bias_ref.shape[-1]
    u = x * lax.rsqrt(jnp.mean(x * x, axis=-1, keepdims=True) + EPS) * gain_ref[...]
    u = u * (1.0 + sc_ref[0]) + sh_ref[0]
    packed = _pack_pair(u[:, :D // 2], u[:, D // 2:])
    u_ref[0] = packed
    for c in range(PACK_CHUNKS):
        ut_ref[:, c, :] = packed[:, c * LANES:(c + 1) * LANES]

    uh, um, ul = _split3(u)
    wh, wm, wl = wh_ref[...], wm_ref[...], wl_ref[...]
    d = lambda a, b: jnp.dot(a, b, preferred_element_type=F32)
    logits = d(uh, wh) + (d(uh, wm) + d(um, wh)) + (d(uh, wl) + d(um, wm) + d(ul, wh))
    scores = jax.nn.sigmoid(logits)
    sel = scores + bias_ref[...]

    lane_i = lax.broadcasted_iota(jnp.int32, (tm, E), 1)
    lane_f = lane_i.astype(F32)
    out_f = lax.broadcasted_iota(jnp.int32, (tm, LANES), 1).astype(F32)
    per_group = E // N_GROUPS
    grp_f = jnp.floor(lane_f * (1.0 / per_group))

    gs = jnp.full((tm, LANES), NEG_INF, F32)
    for g in range(N_GROUPS):
        sg = jnp.where(lane_i >= g * per_group, jnp.where(lane_i < (g + 1) * per_group, sel, NEG_INF), NEG_INF)
        m1 = _row_max(sg)
        i1 = _first_index_of(sg, m1, lane_f)
        m2 = _row_max(jnp.where(lane_f == i1, NEG_INF, sg))
        gs = jnp.where(out_f == g, m1 + m2, gs)

    allow = jnp.zeros((tm, E), F32)
    for _ in range(TOPK_GROUPS):
        m = _row_max(gs)
        i = _first_index_of(gs, m, out_f)
        gs = jnp.where(out_f == i, NEG_INF, gs)
        allow = jnp.where(grp_f == i, 1.0, allow)
    selm = jnp.where(allow > 0.0, sel, NEG_INF)

    ids = jnp.zeros((tm, LANES), F32)
    gts = jnp.zeros((tm, LANES), F32)
    member = jnp.zeros((tm, E), F32)
    idx_cols = []
    gsum = jnp.zeros((tm, 1), F32)
    for k in range(TOP_K):
        m = _row_max(selm)
        i = _first_index_of(selm, m, lane_f)
        hit = lane_f == i
        gk = jnp.sum(jnp.where(hit, scores, 0.0), axis=-1, keepdims=True)
        selm = jnp.where(hit, NEG_INF, selm)
        member = jnp.where(hit, 1.0, member)
        ids = jnp.where(out_f == k, i, ids)
        gts = jnp.where(out_f == k, gk, gts)
        idx_cols.append(i)
        gsum = gsum + gk
    gts = gts / gsum * ROUTED_SCALE

    r2 = lax.broadcasted_iota(jnp.int32, (tm, tm), 0)
    c2 = lax.broadcasted_iota(jnp.int32, (tm, tm), 1)
    before = jnp.where(r2 > c2, 1.0, 0.0).astype(BF16)
    mem_b = member.astype(BF16)
    carry = carry_ref[...]
    pos = carry + jnp.dot(before, mem_b, preferred_element_type=F32)
    rk = jnp.zeros((tm, LANES), F32)
    for k in range(TOP_K):
        rk = jnp.where(out_f == k, jnp.sum(jnp.where(lane_f == idx_cols[k], pos, 0.0), axis=-1, keepdims=True), rk)
    colsum = jnp.dot(jnp.ones((8, tm), BF16), mem_b, preferred_element_type=F32)[0:1]
    carry_ref[...] = carry + colsum
    counts_ref[...] = carry + colsum
    ids_ref[0] = ids.astype(jnp.int32)
    ranks_ref[0] = rk.astype(jnp.int32)
    gates_ref[0] = gts


def _router(x1, sh_m, sc_m, p):
    B, T, D = x1.shape
    E = N_EXPERTS
    wh, wm, wl = _split3(p['router_w'])
    row = lambda b, i: (b, i, 0)
    const = lambda b, i: (0, 0)
    vec = lambda b, i: (b, 0, 0)
    lane_out = lambda dt: jax.ShapeDtypeStruct((B, T, LANES), dt)
    return pl.pallas_call(
        _router_kernel,
        grid=(B, T // ROUTER_TILE),
        in_specs=[
            pl.BlockSpec((1, ROUTER_TILE, D), row),
            pl.BlockSpec((1, D), const),
            pl.BlockSpec((1, 1, D), vec),
            pl.BlockSpec((1, 1, D), vec),
            pl.BlockSpec((D, E), const),
            pl.BlockSpec((D, E), const),
            pl.BlockSpec((D, E), const),
            pl.BlockSpec((1, E), const),
        ],
        out_specs=[
            pl.BlockSpec((1, ROUTER_TILE, D // 2), row),
            pl.BlockSpec((ROUTER_TILE, PACK_CHUNKS, LANES), lambda b, i: (b * (T // ROUTER_TILE) + i, 0, 0)),
            pl.BlockSpec((1, ROUTER_TILE, LANES), row),
            pl.BlockSpec((1, ROUTER_TILE, LANES), row),
            pl.BlockSpec((1, ROUTER_TILE, LANES), row),
            pl.BlockSpec((1, E), const),
        ],
        out_shape=[
            jax.ShapeDtypeStruct((B, T, D // 2), jnp.uint32),
            jax.ShapeDtypeStruct((B * T, PACK_CHUNKS, LANES), jnp.uint32),
            lane_out(jnp.int32), lane_out(F32), lane_out(jnp.int32),
            jax.ShapeDtypeStruct((1, E), F32),
        ],
        scratch_shapes=[pltpu.VMEM((1, E), F32)],
        compiler_params=pltpu.CompilerParams(
            dimension_semantics=("arbitrary", "arbitrary"), vmem_limit_bytes=VMEM_LIMIT),
        name="router",
    )(x1, p['norm_ffn'].reshape(1, D), sh_m[:, None, :], sc_m[:, None, :], wh, wm, wl,
      p['router_bias'].reshape(1, E))


def _slot_kernel(ids_ref, ranks_ref, base_ref, o_ref):
    ids = ids_ref[...].astype(F32)
    tm = ids.shape[0]
    E = base_ref.shape[-1]
    lane_e = lax.broadcasted_iota(jnp.int32, (tm, E), 1).astype(F32)
    out_lane = lax.broadcasted_iota(jnp.int32, (tm, LANES), 1)
    first = jnp.zeros((tm, LANES), F32)
    for k in range(TOP_K):
        fk = jnp.sum(jnp.where(lane_e == ids[:, k:k + 1], base_ref[...], 0.0), axis=-1, keepdims=True)
        first = jnp.where(out_lane == k, fk, first)
    o_ref[...] = first.astype(jnp.int32) + ranks_ref[...]


def _slots(ids, ranks, base):
    n_tok = ids.shape[0]
    E = base.shape[-1]
    row = pl.BlockSpec((ROUTER_TILE, LANES), lambda i: (i, 0))
    return pl.pallas_call(
        _slot_kernel,
        grid=(n_tok // ROUTER_TILE,),
        in_specs=[row, row, pl.BlockSpec((1, E), lambda i: (0, 0))],
        out_specs=row,
        out_shape=jax.ShapeDtypeStruct((n_tok, LANES), jnp.int32),
        compiler_params=pltpu.CompilerParams(dimension_semantics=("arbitrary",)),
        name="slots",
    )(ids, ranks, base)


def _dispatch_kernel(pad_ref, dest_ref, u_ref, slots_ref, zero_ref, sem):
    tm = u_ref.shape[0]
    n_experts = pad_ref.shape[0]

    @pl.when(pl.program_id(0) == 0)
    def _():
        zero_ref[...] = jnp.zeros_like(zero_ref)

        def fill(e, carry):
            pltpu.make_async_copy(zero_ref, slots_ref.at[pl.ds(pad_ref[e], MOE_BLOCK)], sem).start()
            return carry

        lax.fori_loop(0, n_experts, fill, 0)

        def filled(e, carry):
            pltpu.make_async_copy(zero_ref, slots_ref.at[pl.ds(0, MOE_BLOCK)], sem).wait()
            return carry

        lax.fori_loop(0, n_experts, filled, 0)

    def issue(r, carry):
        for k in range(TOP_K):
            pltpu.make_async_copy(u_ref.at[r], slots_ref.at[dest_ref[0, 0, r * TOP_K + k]], sem).start(
                priority=k % DMA_PRIORITIES)
        return carry

    lax.fori_loop(0, tm, issue, 0, unroll=ISSUE_UNROLL)
    for k in range(TOP_K):
        pltpu.make_async_copy(u_ref, slots_ref.at[pl.ds(0, tm)], sem).wait()


def _dispatch(pad_start, dest, u_tiles, n_slots):
    n_tok = u_tiles.shape[0]
    tile = u_tiles.shape[1:]
    nt = n_tok // ROW_TILE
    grid_spec = pltpu.PrefetchScalarGridSpec(
        num_scalar_prefetch=1,
        grid=(nt,),
        in_specs=[
            pl.BlockSpec((1, 1, ROW_TILE * TOP_K), lambda i, pad: (i, 0, 0), memory_space=pltpu.SMEM),
            pl.BlockSpec((ROW_TILE,) + tile, lambda i, pad: (i, 0, 0)),
        ],
        out_specs=pl.BlockSpec(memory_space=pl.ANY),
        scratch_shapes=[pltpu.VMEM((MOE_BLOCK,) + tile, jnp.uint32), pltpu.SemaphoreType.DMA(())],
    )
    return pl.pallas_call(
        _dispatch_kernel,
        grid_spec=grid_spec,
        out_shape=jax.ShapeDtypeStruct((n_slots + MOE_BLOCK,) + tile, jnp.uint32),
        compiler_params=pltpu.CompilerParams(
            dimension_semantics=("arbitrary",), vmem_limit_bytes=VMEM_LIMIT),
        name="dispatch",
    )(pad_start, dest, u_tiles)


WEIGHT_SLOTS = 2


def _expert_weight_copies(hbm_refs, stage_refs, expert, slot, sems):
    return [pltpu.make_async_copy(w.at[expert], s.at[slot], sems.at[slot, i])
            for i, (w, s) in enumerate(zip(hbm_refs, stage_refs))]


def _moe_ffn_kernel(be_ref, first_ref, slot_ref, next_ref, nu_ref, x_ref, w1_ref, w3_ref, w2_ref, o_ref,
                    s1_ref, s3_ref, s2_ref, w13_ref, w2b_ref, sems):
    j = pl.program_id(0)
    F = w1_ref.shape[-1]
    half = PACK_CHUNKS * LANES
    copies = functools.partial(_expert_weight_copies, (w1_ref, w3_ref, w2_ref), (s1_ref, s3_ref, s2_ref), sems=sems)

    @pl.when(j == 0)
    def _():
        for c in copies(be_ref[0], 0):
            c.start()

    @pl.when(first_ref[j] == 1)
    def _():
        slot = slot_ref[j]
        for c in copies(be_ref[j], slot):
            c.wait()

        @pl.when(next_ref[j] >= 0)
        def _():
            for c in copies(next_ref[j], 1 - slot):
                c.start()

        w13_ref[:, :F] = s1_ref[slot].astype(BF16)
        w13_ref[:, F:] = s3_ref[slot].astype(BF16)
        w2b_ref[...] = s2_ref[slot].astype(BF16)

    @pl.when(j < nu_ref[0])
    def _():
        lo, hi = _unpack_pair(jnp.concatenate([x_ref[:, c, :] for c in range(PACK_CHUNKS)], axis=1))
        h = (jnp.dot(lo, w13_ref[:half], preferred_element_type=F32)
             + jnp.dot(hi, w13_ref[half:], preferred_element_type=F32))
        h1, h3 = h[:, :F], h[:, F:]
        y = _dot(h1 * jax.nn.sigmoid(h1) * h3, w2b_ref[...])
        o_ref[...] = pltpu.einshape("s(cl)->(sc)l", y, c=TILE_ROWS)

    @pl.when(j >= nu_ref[0])
    def _():
        o_ref[...] = jnp.zeros_like(o_ref)


def _moe_ffn(block_e, n_used, x_slots, w1, w3, w2):
    E, D, F = w1.shape
    nblk = block_e.shape[0]
    slots = nblk * MOE_BLOCK
    idx = jnp.arange(nblk)
    first = jnp.concatenate([jnp.ones((1,), bool), block_e[1:] != block_e[:-1]])
    slot = ((jnp.cumsum(first) - 1) % WEIGHT_SLOTS).astype(jnp.int32)
    later_first = lax.cummin(jnp.where(first, idx, nblk)[::-1])[::-1]
    next_first = jnp.concatenate([later_first[1:], jnp.full((1,), nblk)])
    next_e = jnp.where(next_first < nblk, block_e[jnp.minimum(next_first, nblk - 1)], -1).astype(jnp.int32)
    blk = lambda j, be, first, slot, nxt, nu: (jnp.minimum(j, nu[0] - 1), 0, 0)
    grid_spec = pltpu.PrefetchScalarGridSpec(
        num_scalar_prefetch=5,
        grid=(nblk,),
        in_specs=[
            pl.BlockSpec((MOE_BLOCK, PACK_CHUNKS, LANES), blk),
            pl.BlockSpec(memory_space=pl.ANY),
            pl.BlockSpec(memory_space=pl.ANY),
            pl.BlockSpec(memory_space=pl.ANY),
        ],
        out_specs=pl.BlockSpec((MOE_BLOCK * TILE_ROWS, LANES), lambda j, *_: (j, 0)),
        scratch_shapes=[pltpu.VMEM((WEIGHT_SLOTS, D, F), F32), pltpu.VMEM((WEIGHT_SLOTS, D, F), F32),
                        pltpu.VMEM((WEIGHT_SLOTS, F, D), F32),
                        pltpu.VMEM((D, 2 * F), BF16), pltpu.VMEM((F, D), BF16),
                        pltpu.SemaphoreType.DMA((WEIGHT_SLOTS, 3))],
    )
    return pl.pallas_call(
        _moe_ffn_kernel,
        grid_spec=grid_spec,
        out_shape=jax.ShapeDtypeStruct((slots * TILE_ROWS, LANES), F32),
        compiler_params=pltpu.CompilerParams(
            dimension_semantics=("arbitrary",), vmem_limit_bytes=VMEM_LIMIT),
        name="moe_ffn",
    )(block_e, first.astype(jnp.int32), slot, next_e, n_used, x_slots, w1, w3, w2)


COMBINE_ROWS = 32


def _combine_kernel(dest_ref, gates_ref, u_ref, x_ref, gm_ref, w1_ref, w3_ref, w2_ref, ys_ref, o_ref, buf_ref, sem):
    tm, half = u_ref.shape

    def tile(i):
        return pl.ds(pl.multiple_of(i * TILE_ROWS, TILE_ROWS), TILE_ROWS)

    def issue(r, carry):
        for k in range(TOP_K):
            pltpu.make_async_copy(ys_ref.at[tile(dest_ref[0, 0, r * TOP_K + k])], buf_ref.at[k, tile(r)], sem).start(
                priority=k % DMA_PRIORITIES)
        return carry

    lax.fori_loop(0, tm, issue, 0, unroll=ISSUE_UNROLL)

    ulo, uhi = _unpack_pair(u_ref[...])
    both = lambda w_ref: (jnp.dot(ulo, w_ref[:half], preferred_element_type=F32)
                          + jnp.dot(uhi, w_ref[half:], preferred_element_type=F32))
    h1, h3 = both(w1_ref), both(w3_ref)
    o_ref[...] = x_ref[...] + gm_ref[0] * _dot(h1 * jax.nn.sigmoid(h1) * h3, w2_ref[...])

    for k in range(TOP_K):
        pltpu.make_async_copy(ys_ref.at[pl.ds(0, tm * TILE_ROWS)], buf_ref.at[k], sem).wait()

    gm = gm_ref[0]
    for rb in range(tm // COMBINE_ROWS):
        rows = slice(rb * COMBINE_ROWS, (rb + 1) * COMBINE_ROWS)
        gates = gates_ref[rows, :]
        for c in range(TILE_ROWS):
            acc = jnp.zeros((COMBINE_ROWS, LANES), F32)
            for k in range(TOP_K):
                acc = acc + gates[:, k:k + 1] * buf_ref[
                    k, pl.ds(rb * COMBINE_ROWS * TILE_ROWS + c, COMBINE_ROWS, stride=TILE_ROWS), :]
            cols = slice(c * LANES, (c + 1) * LANES)
            o_ref[rows, cols] = o_ref[rows, cols] + gm[:, cols] * acc


def _combine(dest, gates, u_rows, x1, g_m, y_slots, p):
    B, T, D = x1.shape
    n_tok = B * T
    W = u_rows.shape[1]
    tiles_per_batch = T // ROW_TILE
    row = lambda i: (i, 0)
    const = lambda i: (0, 0)
    out = pl.pallas_call(
        _combine_kernel,
        grid=(n_tok // ROW_TILE,),
        in_specs=[
            pl.BlockSpec((1, 1, ROW_TILE * TOP_K), lambda i: (i, 0, 0), memory_space=pltpu.SMEM),
            pl.BlockSpec((ROW_TILE, LANES), row),
            pl.BlockSpec((ROW_TILE, W), row),
            pl.BlockSpec((ROW_TILE, D), row),
            pl.BlockSpec((1, 1, D), lambda i: (i // tiles_per_batch, 0, 0)),
            pl.BlockSpec(p['shared_w1'].shape, const),
            pl.BlockSpec(p['shared_w3'].shape, const),
            pl.BlockSpec(p['shared_w2'].shape, const),
            pl.BlockSpec(memory_space=pl.ANY),
        ],
        out_specs=pl.BlockSpec((ROW_TILE, D), row),
        out_shape=jax.ShapeDtypeStruct((n_tok, D), F32),
        scratch_shapes=[pltpu.VMEM((TOP_K, ROW_TILE * TILE_ROWS, LANES), F32), pltpu.SemaphoreType.DMA(())],
        compiler_params=pltpu.CompilerParams(
            dimension_semantics=("arbitrary",), vmem_limit_bytes=VMEM_LIMIT),
        name="combine",
    )(dest, gates.reshape(n_tok, LANES), u_rows, x1.reshape(n_tok, D), g_m[:, None, :],
      p['shared_w1'].astype(BF16), p['shared_w3'].astype(BF16), p['shared_w2'].astype(BF16), y_slots)
    return out.reshape(B, T, D)


def _moe(x1, sh_m, sc_m, g_m, p):
    B, T, D = x1.shape
    n_tok = B * T
    u_packed, u_tiles, ids, gates, ranks, counts = _router(x1, sh_m, sc_m, p)

    counts = counts[0].astype(jnp.int32)
    padded = (counts + MOE_BLOCK - 1) // MOE_BLOCK * MOE_BLOCK
    padded_end = jnp.cumsum(padded)
    base = padded_end - padded
    n_blocks = n_tok * TOP_K // MOE_BLOCK + N_EXPERTS
    n_used = (padded_end[-1] // MOE_BLOCK).astype(jnp.int32)
    blk = jnp.minimum(jnp.arange(n_blocks), n_used - 1) * MOE_BLOCK
    block_e = jnp.minimum(jnp.sum(padded_end[None, :] <= blk[:, None], axis=1), N_EXPERTS - 1).astype(jnp.int32)
    dest = _slots(ids.reshape(n_tok, LANES), ranks.reshape(n_tok, LANES), base.astype(F32).reshape(1, N_EXPERTS))
    dest = dest[:, :TOP_K].reshape(n_tok // ROW_TILE, 1, ROW_TILE * TOP_K)

    u_rows = u_packed.reshape(n_tok, D // 2)
    x_slots = _dispatch((base + counts).astype(jnp.int32), dest, u_tiles, n_blocks * MOE_BLOCK)
    y_slots = _moe_ffn(block_e, n_used.reshape(1), x_slots, p['expert_w1'], p['expert_w3'], p['expert_w2'])
    return _combine(dest, gates, u_rows, x1, g_m, y_slots, p)
```

```python
import functools

import jax
import jax.numpy as jnp
import numpy as np
from jax import lax
from jax.experimental import pallas as pl
from jax.experimental.pallas import tpu as pltpu

F32 = jnp.float32
BF16 = jnp.bfloat16

GRID_W = 64
EPS = 1e-6
MLA_HEADS = 8
QK_NOPE = 64
QK_ROPE = 32
QK_HEAD = QK_NOPE + QK_ROPE
V_HEAD = 64
Q_LORA = 256
KV_LORA = 128
ROPE_BASE = 10000.0
ATTN_SCALE = QK_HEAD ** -0.5
RW_HEADS = 8
RW_HEAD = 64
RW_DIM = RW_HEADS * RW_HEAD
DECAY_LORA = 64
AICL_LORA = 64
GATE_LORA = 160
GN_EPS = 64e-5
N_EXPERTS = 256
TOP_K = 8
N_GROUPS = 8
TOPK_GROUPS = 4
ROUTED_SCALE = 2.5

LANES = 128
VMEM_LIMIT = 56 * 1024 * 1024

WKV_CHUNK = 64
WKV_GROUP = LANES
HEADS_PER_GROUP = WKV_GROUP // RW_HEAD
HEAD_PAD = LANES


def _dot(a, b):
    return jnp.dot(a.astype(BF16), b.astype(BF16), preferred_element_type=F32)


def _dot_t(a, b):
    return lax.dot_general(a.astype(BF16), b.astype(BF16), (((1,), (1,)), ((), ())),
                           preferred_element_type=F32)


def _split3(x):
    h = x.astype(BF16)
    r1 = x - h.astype(F32)
    m = r1.astype(BF16)
    lo = (r1 - m.astype(F32)).astype(BF16)
    return h, m, lo


def _dot_hi(a_bf16_exact, x):
    h, m, lo = _split3(x)
    d = lambda y: jnp.dot(a_bf16_exact, y, preferred_element_type=F32)
    return d(h) + d(m) + d(lo)


def _wkv_chunk(r, v, kk, lw, bk, kd, s0, reverse, emit):
    L, G = r.shape
    row = lax.broadcasted_iota(jnp.int32, (L, G), 0)
    lane = lax.broadcasted_iota(jnp.int32, (L, G), 1)
    diff = (lane % L - row) if reverse else (row - lane % L)
    strict = diff > 0
    incl = diff >= 0
    r2 = lax.broadcasted_iota(jnp.int32, (L, L), 0)
    c2 = lax.broadcasted_iota(jnp.int32, (L, L), 1)
    tri = jnp.where(((c2 - r2) if reverse else (r2 - c2)) >= 0, 1.0, 0.0).astype(BF16)
    tri_ones = jnp.concatenate([tri, jnp.ones((L, L), BF16)], axis=0)
    lane_head = lane // RW_HEAD
    rowb = lax.broadcasted_iota(jnp.int32, (G, G), 0) // RW_HEAD
    colb = lax.broadcasted_iota(jnp.int32, (G, G), 1) // RW_HEAD

    def stack(x):
        return jnp.concatenate(
            [jnp.where(lane_head == h, x, 0.0) for h in range(HEADS_PER_GROUP)], axis=0).astype(BF16)

    sums = _dot_hi(tri_ones, lw)
    cum_in = sums[:L]
    g_in = jnp.exp(cum_in)
    g_inv = jnp.exp(-cum_in)
    g_ex = jnp.exp(cum_in - lw)
    g_tot = jnp.exp(sums[L:])
    a_h = -kk * g_ex
    b_h = bk * g_inv
    k_h = kd * g_inv
    r_h = r * g_in
    yield

    gram = _dot_t(jnp.concatenate([a_h, r_h], axis=0),
                  jnp.concatenate([stack(b_h), stack(k_h)], axis=0))
    ab = jnp.where(strict, gram[:L, :G], 0.0)
    ak = jnp.where(strict, gram[:L, G:], 0.0)
    rb = jnp.where(incl, gram[L:, :G], 0.0)
    rk = jnp.where(incl, gram[L:, G:], 0.0)
    yield

    tm = jnp.where(diff == 0, 1.0, 0.0) + ab
    p = ab
    v_st = stack(v)
    akv = _dot(ak, v_st)
    for _ in range(int(np.log2(L)) - 1):
        p = _dot(p, stack(p))
        yield
        tm = tm + _dot(tm, stack(p))
        yield

    wu = _dot(tm, jnp.concatenate([stack(a_h), stack(akv)], axis=1))
    w_t, u_t = wu[:, :G], wu[:, G:]
    yield
    rbwu = _dot(rb, jnp.concatenate([stack(w_t), stack(u_t)], axis=1))
    r_t = r_h + rbwu[:, :G]
    y_t = rbwu[:, G:] + _dot(rk, v_st)
    yield

    s0_st = stack(s0)
    y = _dot_t(r_t, s0_st) + y_t
    u = _dot_t(w_t, s0_st) + u_t
    yield
    uv = jnp.concatenate([u, v], axis=0)
    bkc = jnp.concatenate([b_h, k_h], axis=0)
    upd = lax.dot_general(uv.astype(BF16), bkc.astype(BF16), (((0,), (0,)), ((), ())),
                          preferred_element_type=F32)
    upd = jnp.where(rowb == colb, upd, 0.0)
    upd_d = upd[0:L]
    for h in range(1, HEADS_PER_GROUP):
        upd_d = upd_d + upd[h * L:(h + 1) * L]
    emit(y, (s0 + upd_d) * g_tot)


def _wkv_kernel(*refs, n_groups, n_batch):
    ins, (yf_ref, yb_ref, s_ref) = refs[:12], refs[12:]
    G = WKV_GROUP

    @pl.when(pl.program_id(1) == 0)
    def _():
        s_ref[...] = jnp.zeros_like(s_ref)

    chains = []
    for d, y_ref in enumerate((yf_ref, yb_ref)):
        r_ref, v_ref, kk_ref, lw_ref, bk_ref, kd_ref = ins[6 * d:6 * d + 6]
        for bi in range(n_batch):
            for g in range(n_groups):
                sl = slice(g * G, (g + 1) * G)

                def emit(y, s_new, y_ref=y_ref, d=d, bi=bi, g=g, sl=sl):
                    y_ref[bi, :, sl] = y
                    s_ref[d, bi, g] = s_new

                chains.append(_wkv_chunk(
                    r_ref[bi, :, sl], v_ref[bi, :, sl], kk_ref[bi, :, sl], lw_ref[0, bi, :, sl],
                    bk_ref[0, bi, :, sl], kd_ref[0, bi, :, sl], s_ref[d, bi, g], d == 1, emit))
    while chains:
        chains = [c for c in chains if next(c, StopIteration) is not StopIteration]


WKV_BATCH = 4


def _wkv_scan(r, v, kk, lw, bk, kd, n_ctx):
    B, Ttot, C = r.shape
    L = WKV_CHUNK
    nc = Ttot // L
    ncc = n_ctx // L
    nl = nc - ncc
    n_groups = C // WKV_GROUP
    nb = WKV_BATCH
    assert B % nb == 0 and Ttot % L == 0 and n_ctx % L == 0

    cid = (lambda s: s, lambda s: jnp.where(s < ncc, ncc - 1 - s, nc + ncc - 1 - s))
    first_out = (0, nl - 1)
    in_specs, out_specs = [], []
    for d in range(2):
        shared = pl.BlockSpec((nb, L, C), lambda b, s, d=d: (b, cid[d](s), 0))
        per_dir = pl.BlockSpec((1, nb, L, C), lambda b, s, d=d: (d, b, cid[d](s), 0))
        in_specs += [shared, shared, shared, per_dir, per_dir, per_dir]
        out_specs.append(pl.BlockSpec(
            (nb, L, C), lambda b, s, d=d: (b, jnp.where(s < ncc, first_out[d], cid[d](s) - ncc), 0)))
    out = jax.ShapeDtypeStruct((B, nl * L, C), F32)
    return pl.pallas_call(
        functools.partial(_wkv_kernel, n_groups=n_groups, n_batch=nb),
        grid=(B // nb, nc),
        in_specs=in_specs,
        out_specs=out_specs,
        out_shape=[out, out],
        scratch_shapes=[pltpu.VMEM((2, nb, n_groups, L, WKV_GROUP), F32)],
        compiler_params=pltpu.CompilerParams(
            dimension_semantics=("arbitrary", "arbitrary"), vmem_limit_bytes=VMEM_LIMIT),
        name="wkv_scan",
    )(r, v, kk, lw, bk, kd, r, v, kk, lw, bk, kd)


def kernel(x, c, ctx, c_ctx, ada_w, ada_b, norm_mix, norm_ffn, w_in, shift_conv, q_lat_norm, w_uq, kv_lat_norm, w_ukv, q_norm, k_norm, w_o_mla, decay_w0, decay_w2, aicl_a0, aicl_a2, k_k, k_a, r_k, gn_w, gn_b, gate_g2, w_o_rwkv, w_out, router_w, router_bias, expert_w1, expert_w3, expert_w2, shared_w1, shared_w3, shared_w2):
    B, T, D = x.shape
    n_ctx = ctx.shape[1]
    i0 = 0
    p = dict(norm_mix=norm_mix[i0], norm_ffn=norm_ffn[i0], w_in=w_in[i0], shift_conv=shift_conv[i0],
             q_lat_norm=q_lat_norm[i0], w_uq=w_uq[i0], kv_lat_norm=kv_lat_norm[i0], w_ukv=w_ukv[i0],
             q_norm=q_norm[i0], k_norm=k_norm[i0], w_o_mla=w_o_mla[i0],
             decay_w0=decay_w0[i0], decay_w2=decay_w2[i0], aicl_a0=aicl_a0[i0], aicl_a2=aicl_a2[i0],
             k_k=k_k[i0], k_a=k_a[i0], r_k=r_k[i0], gn_w=gn_w[i0], gn_b=gn_b[i0], gate_g2=gate_g2[i0],
             w_o_rwkv=w_o_rwkv[i0], w_out=w_out[i0], router_w=router_w[i0], router_bias=router_bias[i0],
             expert_w1=expert_w1[i0], expert_w3=expert_w3[i0], expert_w2=expert_w2[i0],
             shared_w1=shared_w1[i0], shared_w3=shared_w3[i0], shared_w2=shared_w2[i0])

    mod = _ada_modulation(c, c_ctx, ada_w[i0], ada_b[i0])
    sh_a, sc_a, g_a, sh_m, sc_m, g_m = jnp.split(mod[:B], 6, axis=-1)
    csh_a, csc_a = jnp.split(mod[B], 6, axis=-1)[:2]
    sh2 = jnp.stack([jnp.broadcast_to(csh_a, (B, D)), sh_a], axis=1)
    sc2 = jnp.stack([jnp.broadcast_to(csc_a, (B, D)), sc_a], axis=1)

    proj = _in_proj(ctx, x, sh2, sc2, p['norm_mix'], _pack_w_in(p['w_in']))

    q, k, v = _mla_prep(proj, p, n_ctx, T)
    att = _attention(q, k, v)

    r, vv, kk, lw, bk, kd, bonus, gg = _rwkv_prep(proj, p, n_ctx)
    y_f, y_b = _wkv_scan(r, vv, kk, lw, bk, kd, n_ctx)

    x1 = _merge(att, y_f, y_b, bonus, gg, proj, x, g_a, p, n_ctx)
    return _moe(x1, sh_m, sc_m, g_m, p)


SUBLANES = 8


def _ada_kernel(c_ref, w_ref, b_ref, o_ref):
    cc = c_ref[...]
    w = w_ref[...]
    w_hi = w.astype(BF16)
    w_lo = (w - w_hi.astype(F32)).astype(BF16)
    o_ref[...] = _dot3(cc * jax.nn.sigmoid(cc), w_hi, w_lo) + b_ref[...]


def _ada_modulation(c, c_ctx, ada_w, ada_b):
    B, D = c.shape
    n_out = ada_w.shape[1]
    rows = -(-(B + 1) // SUBLANES) * SUBLANES
    c_all = jnp.concatenate([c, c_ctx[None, :], jnp.zeros((rows - B - 1, D), F32)], axis=0)
    out = pl.pallas_call(
        _ada_kernel,
        grid=(n_out // D,),
        in_specs=[pl.BlockSpec((rows, D), lambda j: (0, 0)),
                  pl.BlockSpec((D, D), lambda j: (0, j)),
                  pl.BlockSpec((1, D), lambda j: (0, j))],
        out_specs=pl.BlockSpec((rows, D), lambda j: (0, j)),
        out_shape=jax.ShapeDtypeStruct((rows, n_out), F32),
        compiler_params=pltpu.CompilerParams(dimension_semantics=("arbitrary",), vmem_limit_bytes=VMEM_LIMIT),
        name="ada_modulation",
    )(c_all, ada_w, ada_b.reshape(1, n_out))
    return out[:B + 1]


ROW_TILE = 256
PROJ_RW = 2048
PROJ_GATE = 2048
PROJ_MLA = 512
PROJ_W = PROJ_RW + PROJ_GATE + PROJ_MLA
MLA_IN = Q_LORA + KV_LORA + QK_ROPE
RW_SPLITS = (RW_DIM, RW_DIM, RW_DIM, DECAY_LORA, DECAY_LORA, AICL_LORA, AICL_LORA, GATE_LORA)
RW_IN = sum(RW_SPLITS)


def _pack_w_in(w_in):
    w_mla = w_in[:, :MLA_IN]
    w_rw = w_in[:, MLA_IN:MLA_IN + RW_IN]
    w_gate = w_in[:, MLA_IN + RW_IN:]
    pad = lambda w, n: jnp.pad(w, ((0, 0), (0, n - w.shape[1])))
    return jnp.concatenate([pad(w_rw, PROJ_RW), w_gate, pad(w_mla, PROJ_MLA)], axis=1).astype(BF16)


PROJ_PIECES = 3
PROJ_COL_CHUNK = PROJ_W // 3


def _in_proj_kernel(ctx_ref, *refs):
    x_refs, (sh_ref, sc_ref, gain_ref, w_ref, o_ref) = refs[:PROJ_PIECES], refs[PROJ_PIECES:]
    first = pl.program_id(1) == 0
    pieces = []
    for n, x_ref in enumerate(x_refs):
        is_ctx = first if n == 0 else False
        xt = jnp.where(is_ctx, ctx_ref[0], x_ref[0]) if n == 0 else x_ref[0]
        sh = jnp.where(is_ctx, sh_ref[0, 0:1], sh_ref[0, 1:2])
        sc = jnp.where(is_ctx, sc_ref[0, 0:1], sc_ref[0, 1:2])
        y = xt * lax.rsqrt(jnp.mean(xt * xt, axis=-1, keepdims=True) + EPS) * gain_ref[...]
        pieces.append((y * (1.0 + sc) + sh).astype(BF16))
    h = jnp.concatenate(pieces, axis=0)
    for c in range(PROJ_W // PROJ_COL_CHUNK):
        cols = slice(c * PROJ_COL_CHUNK, (c + 1) * PROJ_COL_CHUNK)
        o_ref[0, :, cols] = jnp.dot(h, w_ref[:, cols], preferred_element_type=F32).astype(o_ref.dtype)


def _in_proj(ctx, x, sh2, sc2, gain, w):
    B, T, D = x.shape
    n_ctx = ctx.shape[1]
    rows = PROJ_PIECES * ROW_TILE
    assert n_ctx == ROW_TILE and (n_ctx + T) % rows == 0
    nt = (n_ctx + T) // rows
    piece = lambda n: pl.BlockSpec(
        (1, ROW_TILE, D), lambda b, i, n=n: (b, jnp.maximum(PROJ_PIECES * i + n - 1, 0), 0))
    return pl.pallas_call(
        _in_proj_kernel,
        grid=(B, nt),
        in_specs=[pl.BlockSpec((1, ROW_TILE, D), lambda b, i: (b, 0, 0))]
                 + [piece(n) for n in range(PROJ_PIECES)]
                 + [pl.BlockSpec((1, 2, D), lambda b, i: (b, 0, 0)),
                    pl.BlockSpec((1, 2, D), lambda b, i: (b, 0, 0)),
                    pl.BlockSpec((1, D), lambda b, i: (0, 0)),
                    pl.BlockSpec((D, PROJ_W), lambda b, i: (0, 0))],
        out_specs=pl.BlockSpec((1, rows, PROJ_W), lambda b, i: (b, i, 0)),
        out_shape=jax.ShapeDtypeStruct((B, n_ctx + T, PROJ_W), BF16),
        compiler_params=pltpu.CompilerParams(
            dimension_semantics=("arbitrary", "arbitrary"), vmem_limit_bytes=VMEM_LIMIT),
        name="in_proj",
    )(ctx, *([x] * PROJ_PIECES), sh2, sc2, gain.reshape(1, D), w)


def _rms(x, gain):
    return x * lax.rsqrt(jnp.mean(x * x, axis=-1, keepdims=True) + EPS) * gain


def _rope_tables(n_tokens):
    rows = n_tokens // GRID_W
    row = jnp.repeat(jnp.arange(rows, dtype=F32), GRID_W)
    col = jnp.tile(jnp.arange(GRID_W, dtype=F32), rows)
    n_freq = QK_ROPE // 4
    inv_freq = ROPE_BASE ** (-jnp.arange(n_freq, dtype=F32) / n_freq)
    ang = jnp.concatenate([row[:, None] * inv_freq, col[:, None] * inv_freq], axis=-1)
    return jnp.cos(ang), jnp.sin(ang)


ROPE_HALF = QK_ROPE // 2
X1 = slice(QK_NOPE, QK_NOPE + ROPE_HALF)
X2 = slice(QK_NOPE + ROPE_HALF, QK_HEAD)


def _rot_cols(w):
    w3 = w.reshape(w.shape[0], MLA_HEADS, HEAD_PAD)
    rot = jnp.zeros_like(w3).at[:, :, X1].set(-w3[:, :, X2]).at[:, :, X2].set(w3[:, :, X1])
    return rot.reshape(w.shape)


def _swap_halves(g):
    return jnp.zeros_like(g).at[:, X1].set(g[:, X2]).at[:, X2].set(g[:, X1])


def _mla_weights(p):
    H = MLA_HEADS
    wq = jnp.pad(p['w_uq'].reshape(Q_LORA, H, QK_HEAD), ((0, 0), (0, 0), (0, HEAD_PAD - QK_HEAD)))
    wq = wq.reshape(Q_LORA, H * HEAD_PAD)
    wkv = p['w_ukv'].reshape(KV_LORA, H, QK_NOPE + V_HEAD)
    wk_lat = jnp.pad(wkv[:, :, :QK_NOPE], ((0, 0), (0, 0), (0, HEAD_PAD - QK_NOPE)))
    place = jnp.zeros((LANES, H, HEAD_PAD), F32).at[:QK_ROPE, :, QK_NOPE:QK_HEAD].set(
        jnp.broadcast_to(jnp.eye(QK_ROPE, dtype=F32)[:, None, :], (QK_ROPE, H, QK_ROPE)))
    wk = jnp.concatenate([wk_lat, place], axis=0).reshape(KV_LORA + LANES, H * HEAD_PAD)
    wv = wkv[:, :, QK_NOPE:].reshape(KV_LORA, H * V_HEAD)
    gq = jnp.pad(p['q_norm'], (0, HEAD_PAD - QK_HEAD)).reshape(1, HEAD_PAD)
    gk = jnp.pad(p['k_norm'], (0, HEAD_PAD - QK_HEAD)).reshape(1, HEAD_PAD)
    bf = lambda w: w.astype(BF16)
    return (bf(wq), bf(_rot_cols(wq)), bf(wk), bf(_rot_cols(wk)), bf(wv), gq, _swap_halves(gq), gk, _swap_halves(gk))


def _mla_tables(n_ctx, T):
    cos, sin = _rope_tables(T)
    c = jnp.ones((n_ctx + T, HEAD_PAD), F32).at[n_ctx:, X1].set(cos).at[n_ctx:, X2].set(cos)
    s = jnp.zeros((n_ctx + T, HEAD_PAD), F32).at[n_ctx:, X1].set(sin).at[n_ctx:, X2].set(sin)
    return c, s


def _mla_prep_kernel(x_ref, c_ref, s_ref, qlg_ref, kvg_ref, wq_ref, wqr_ref, wk_ref, wkr_ref, wv_ref,
                     gq_ref, gqp_ref, gk_ref, gkp_ref, q_ref, k_ref, v_ref):
    x = x_ref[0].astype(F32)
    ql = _rms(x[:, :Q_LORA], qlg_ref[...])
    kvl = _rms(x[:, Q_LORA:Q_LORA + KV_LORA], kvg_ref[...])
    k_in = jnp.concatenate([kvl, x[:, Q_LORA + KV_LORA:]], axis=1)
    cos, sin = c_ref[...], s_ref[...]

    def finish(raw, partner, g, g_swapped, scale, o_ref):
        gc, gs = g * cos, g_swapped * sin
        for h in range(MLA_HEADS):
            sl = slice(h * HEAD_PAD, (h + 1) * HEAD_PAD)
            rh = raw[:, sl]
            inv = lax.rsqrt(jnp.sum(rh * rh, axis=-1, keepdims=True) * (1.0 / QK_HEAD) + EPS) * scale
            o_ref[0, :, sl] = ((rh * gc + partner[:, sl] * gs) * inv).astype(o_ref.dtype)

    finish(_dot(ql, wq_ref[...]), _dot(ql, wqr_ref[...]), gq_ref[...], gqp_ref[...], ATTN_SCALE, q_ref)
    finish(_dot(k_in, wk_ref[...]), _dot(k_in, wkr_ref[...]), gk_ref[...], gkp_ref[...], 1.0, k_ref)
    v_ref[0] = _dot(kvl, wv_ref[...]).astype(v_ref.dtype)


def _mla_prep(proj, p, n_ctx, T):
    B, Tt, _ = proj.shape
    H = MLA_HEADS
    ctx_tiles = n_ctx // ROW_TILE
    mla_blk = (PROJ_RW + PROJ_GATE) // PROJ_MLA
    weights = _mla_weights(p)
    cos, sin = _mla_tables(n_ctx, T)
    const = lambda b, i: (0, 0)
    row = lambda b, i: (b, i, 0)
    tab = pl.BlockSpec((ROW_TILE, HEAD_PAD), lambda b, i: (i, 0))
    full = lambda a: pl.BlockSpec(a.shape, const)
    return pl.pallas_call(
        _mla_prep_kernel,
        grid=(B, Tt // ROW_TILE),
        in_specs=[pl.BlockSpec((1, ROW_TILE, PROJ_MLA), lambda b, i: (b, i, mla_blk)), tab, tab,
                  pl.BlockSpec((1, Q_LORA), const), pl.BlockSpec((1, KV_LORA), const)]
                 + [full(w) for w in weights],
        out_specs=[
            pl.BlockSpec((1, ROW_TILE, H * HEAD_PAD), lambda b, i: (b, jnp.maximum(i - ctx_tiles, 0), 0)),
            pl.BlockSpec((1, ROW_TILE, H * HEAD_PAD), row),
            pl.BlockSpec((1, ROW_TILE, H * V_HEAD), row),
        ],
        out_shape=[
            jax.ShapeDtypeStruct((B, T, H * HEAD_PAD), BF16),
            jax.ShapeDtypeStruct((B, Tt, H * HEAD_PAD), BF16),
            jax.ShapeDtypeStruct((B, Tt, H * V_HEAD), BF16),
        ],
        compiler_params=pltpu.CompilerParams(
            dimension_semantics=("arbitrary", "arbitrary"), vmem_limit_bytes=VMEM_LIMIT),
        name="mla_prep",
    )(proj, cos, sin, p['q_lat_norm'].reshape(1, Q_LORA), p['kv_lat_norm'].reshape(1, KV_LORA), *weights)


ATTN_Q_TILE = 2048
HEADS_PER_STEP = LANES // V_HEAD


ATTN_ROW_SPLIT = 16


def _attn_kernel(q_ref, k_ref, v_ref, o_ref):
    v2 = v_ref[0]
    rows = q_ref.shape[1] // ATTN_ROW_SPLIT
    work = [(hh, rs) for rs in range(ATTN_ROW_SPLIT) for hh in range(HEADS_PER_STEP)]

    def scores(hh, rs):
        sl = slice(hh * HEAD_PAD, (hh + 1) * HEAD_PAD)
        return _dot_t(q_ref[0, rs * rows:(rs + 1) * rows, sl], k_ref[0, :, sl])

    outs = {}
    s_next = scores(*work[0])
    for i, (hh, rs) in enumerate(work):
        s = s_next
        if i + 1 < len(work):
            s_next = scores(*work[i + 1])
        e = jnp.exp(s - jnp.max(s, axis=-1, keepdims=True))
        outs[hh, rs] = _dot(e, v2) / jnp.sum(e, axis=-1, keepdims=True)
    lane = lax.broadcasted_iota(jnp.int32, (rows, LANES), 1)
    for rs in range(ATTN_ROW_SPLIT):
        o_ref[0, rs * rows:(rs + 1) * rows] = jnp.where(lane < V_HEAD, outs[0, rs], outs[1, rs])


def _attention(q, k, v):
    B, T, _ = q.shape
    Kt = k.shape[1]
    assert T % ATTN_Q_TILE == 0 and ATTN_Q_TILE % (ATTN_ROW_SPLIT * SUBLANES) == 0
    hp = MLA_HEADS // HEADS_PER_STEP
    qw = HEADS_PER_STEP * HEAD_PAD
    return pl.pallas_call(
        _attn_kernel,
        grid=(B, hp, T // ATTN_Q_TILE),
        in_specs=[
            pl.BlockSpec((1, ATTN_Q_TILE, qw), lambda b, h, i: (b, i, h)),
            pl.BlockSpec((1, Kt, qw), lambda b, h, i: (b, 0, h)),
            pl.BlockSpec((1, Kt, LANES), lambda b, h, i: (b, 0, h)),
        ],
        out_specs=pl.BlockSpec((1, ATTN_Q_TILE, LANES), lambda b, h, i: (b, i, h)),
        out_shape=jax.ShapeDtypeStruct((B, T, MLA_HEADS * V_HEAD), F32),
        compiler_params=pltpu.CompilerParams(
            dimension_semantics=("arbitrary", "arbitrary", "arbitrary"), vmem_limit_bytes=VMEM_LIMIT),
        name="attention",
    )(q, k, v)


HALO = 16
LORA_W = LANES
GATE_W = PROJ_RW - 3 * RW_DIM - 2 * LORA_W


def _head_ones(width, head):
    i = np.arange(width) // head
    return jnp.asarray(i[:, None] == i[None, :], BF16)


def _head_sum(x, ones_bd):
    hi = x.astype(BF16)
    lo = (x - hi.astype(F32)).astype(BF16)
    return (jnp.dot(hi, ones_bd, preferred_element_type=F32)
            + jnp.dot(lo, ones_bd, preferred_element_type=F32))


def _dot3(a, b_hi, b_lo):
    hi = a.astype(BF16)
    lo = (a - hi.astype(F32)).astype(BF16)
    d = lambda u, w: jnp.dot(u, w, preferred_element_type=F32)
    return d(hi, b_hi) + (d(hi, b_lo) + d(lo, b_hi))


def _rwkv_prep_kernel(x_ref, prev_ref, next_ref, conv_ref, kkg_ref, ka_ref, rk_ref, w0_ref, a0_ref,
                      w2h_ref, w2l_ref, a2_ref, g2_ref, ones_ref,
                      r_ref, v_ref, kk_ref, lw_ref, bk_ref, kd_ref, bonus_ref, gg_ref, *, ctx_tiles, n_tiles):
    i = pl.program_id(1)
    x = x_ref[0].astype(F32)
    tm, W = x.shape
    C = RW_DIM
    first = (i == 0) | (i == ctx_tiles)
    last = (i == ctx_tiles - 1) | (i == n_tiles - 1)
    prev_row = jnp.where(first, 0.0, prev_ref[0, HALO - 1:HALO].astype(F32))
    next_row = jnp.where(last, 0.0, next_ref[0, 0:1].astype(F32))
    row = lax.broadcasted_iota(jnp.int32, (tm, W), 0)
    x_dn = jnp.where(row == 0, prev_row, pltpu.roll(x, 1, 0))
    x_up = jnp.where(row == tm - 1, next_row, pltpu.roll(x, tm - 1, 0))
    xc = x_dn * conv_ref[0:1] + x * conv_ref[1:2] + x_up * conv_ref[2:3]

    r, k, v = xc[:, :C], xc[:, C:2 * C], xc[:, 2 * C:3 * C]
    lora_w = jnp.tanh(xc[:, 3 * C:3 * C + LORA_W])
    lora_a = xc[:, 3 * C + LORA_W:3 * C + 2 * LORA_W]
    lg = xc[:, 3 * C + 2 * LORA_W:]
    ones_bd = ones_ref[...]
    kq = k * kkg_ref[...]
    kk = kq * lax.rsqrt(_head_sum(kq * kq, ones_bd) + 1e-12)
    r_ref[0], v_ref[0], kk_ref[0] = r, v, kk

    k_sum = jnp.zeros_like(k)
    for d in range(2):
        z = w0_ref[d:d + 1] + _dot3(lora_w, w2h_ref[d], w2l_ref[d])
        softplus_neg = jnp.maximum(-z, 0.0) + jnp.log(1.0 + jnp.exp(-jnp.abs(z)))
        lw_ref[d, 0] = -jnp.exp(-softplus_neg - 0.5)
        a = jax.nn.sigmoid(a0_ref[d:d + 1] + _dot(lora_a, a2_ref[d]))
        kd = k * (1.0 + (a - 1.0) * ka_ref[...])
        bk_ref[d, 0] = kk * a
        kd_ref[d, 0] = kd
        k_sum = k_sum + kd
    bonus_ref[0] = _head_sum(r * k_sum * rk_ref[...], ones_bd) * v
    gg_ref[0] = _dot(jax.nn.sigmoid(lg), g2_ref[...])


def _rwkv_prep(proj, p, n_ctx):
    B, Tt, _ = proj.shape
    C = RW_DIM
    nt = Tt // ROW_TILE
    hb = ROW_TILE // HALO
    pad_cols = lambda w: jnp.pad(w, ((0, 0), (0, PROJ_RW - w.shape[1])))
    w2 = jnp.stack([jnp.pad(p['decay_w2'][0], ((0, LORA_W - DECAY_LORA), (0, 0))),
                    jnp.pad(p['decay_w2'][1], ((DECAY_LORA, 0), (0, 0)))])
    w2h = w2.astype(BF16)
    w2l = (w2 - w2h.astype(F32)).astype(BF16)
    a2 = jnp.stack([jnp.pad(p['aicl_a2'][0], ((0, LORA_W - AICL_LORA), (0, 0))),
                    jnp.pad(p['aicl_a2'][1], ((AICL_LORA, 0), (0, 0)))]).astype(BF16)
    g2 = jnp.pad(p['gate_g2'], ((0, GATE_W - GATE_LORA), (0, 0))).astype(BF16)
    row = lambda b, i: (b, i, 0)
    drow = lambda b, i: (0, b, i, 0)
    const2 = lambda b, i: (0, 0)
    const3 = lambda b, i: (0, 0, 0)
    vec = pl.BlockSpec((1, C), const2)
    out_row = pl.BlockSpec((1, ROW_TILE, C), row)
    out_dir = pl.BlockSpec((2, 1, ROW_TILE, C), drow)
    sds = jax.ShapeDtypeStruct((B, Tt, C), F32)
    sds2 = jax.ShapeDtypeStruct((2, B, Tt, C), F32)
    ctx_tiles = n_ctx // ROW_TILE
    out_lat = pl.BlockSpec((1, ROW_TILE, C), lambda b, i: (b, jnp.maximum(i - ctx_tiles, 0), 0))
    sds_lat = jax.ShapeDtypeStruct((B, Tt - n_ctx, C), F32)
    return pl.pallas_call(
        functools.partial(_rwkv_prep_kernel, ctx_tiles=n_ctx // ROW_TILE, n_tiles=nt),
        grid=(B, nt),
        in_specs=[
            pl.BlockSpec((1, ROW_TILE, PROJ_RW), row),
            pl.BlockSpec((1, HALO, PROJ_RW), lambda b, i: (b, jnp.maximum(i * hb - 1, 0), 0)),
            pl.BlockSpec((1, HALO, PROJ_RW), lambda b, i: (b, jnp.minimum((i + 1) * hb, nt * hb - 1), 0)),
            pl.BlockSpec((3, PROJ_RW), const2),
            vec, vec, vec,
            pl.BlockSpec((2, C), const2),
            pl.BlockSpec((2, C), const2),
            pl.BlockSpec((2, LORA_W, C), const3),
            pl.BlockSpec((2, LORA_W, C), const3),
            pl.BlockSpec((2, LORA_W, C), const3),
            pl.BlockSpec((GATE_W, C), const2),
            pl.BlockSpec((C, C), const2),
        ],
        out_specs=[out_row, out_row, out_row, out_dir, out_dir, out_dir, out_lat, out_lat],
        out_shape=[sds, sds, sds, sds2, sds2, sds2, sds_lat, sds_lat],
        compiler_params=pltpu.CompilerParams(
            dimension_semantics=("arbitrary", "arbitrary"), vmem_limit_bytes=VMEM_LIMIT),
        name="rwkv_prep",
    )(proj, proj, proj, pad_cols(p['shift_conv']), p['k_k'].reshape(1, C), p['k_a'].reshape(1, C),
      p['r_k'].reshape(1, C), p['decay_w0'], p['aicl_a0'], w2h, w2l, a2, g2, _head_ones(C, RW_HEAD))


MERGE_PIECES = 2


def _merge_kernel(att_ref, yf_ref, yb_ref, bonus_ref, gg_ref, *refs):
    gate_refs, (x_ref, ga_ref, gnw_ref, gnb_ref, ones_ref, wm_ref, wr_ref, wo_ref, o_ref) = (
        refs[:MERGE_PIECES], refs[MERGE_PIECES:])
    D = x_ref.shape[-1]
    ones_bd = ones_ref[...]
    y = yf_ref[0] + yb_ref[0]
    yc = y - _head_sum(y, ones_bd) * (1.0 / RW_HEAD)
    var = _head_sum(yc * yc, ones_bd) * (1.0 / RW_HEAD)
    y_n = yc * lax.rsqrt(var + GN_EPS) * gnw_ref[...] + gnb_ref[...]
    rw = (y_n + bonus_ref[0]) * gg_ref[0]
    g = jax.nn.sigmoid(jnp.concatenate([r[0] for r in gate_refs], axis=0).astype(F32))
    mix = g[:, :D] * _dot(att_ref[0], wm_ref[...]) + g[:, D:] * _dot(rw, wr_ref[...])
    o_ref[0] = x_ref[0] + ga_ref[0] * _dot(mix, wo_ref[...])


def _merge(att, y_f, y_b, bonus, gg, proj, x, g_a, p, n_ctx):
    B, T, D = x.shape
    C = RW_DIM
    rows = MERGE_PIECES * ROW_TILE
    ctx_tiles = n_ctx // ROW_TILE
    gate_blk = PROJ_RW // PROJ_GATE
    row = lambda b, i: (b, i, 0)
    const = lambda b, i: (0, 0)
    gate_piece = lambda n: pl.BlockSpec(
        (1, ROW_TILE, PROJ_GATE), lambda b, i, n=n: (b, MERGE_PIECES * i + n + ctx_tiles, gate_blk))
    return pl.pallas_call(
        _merge_kernel,
        grid=(B, T // rows),
        in_specs=[
            pl.BlockSpec((1, rows, att.shape[-1]), row),
            pl.BlockSpec((1, rows, C), row),
            pl.BlockSpec((1, rows, C), row),
            pl.BlockSpec((1, rows, C), row),
            pl.BlockSpec((1, rows, C), row),
        ] + [gate_piece(n) for n in range(MERGE_PIECES)] + [
            pl.BlockSpec((1, rows, D), row),
            pl.BlockSpec((1, 1, D), lambda b, i: (b, 0, 0)),
            pl.BlockSpec((1, C), const),
            pl.BlockSpec((1, C), const),
            pl.BlockSpec((C, C), const),
            pl.BlockSpec(p['w_o_mla'].shape, const),
            pl.BlockSpec(p['w_o_rwkv'].shape, const),
            pl.BlockSpec(p['w_out'].shape, const),
        ],
        out_specs=pl.BlockSpec((1, rows, D), row),
        out_shape=jax.ShapeDtypeStruct((B, T, D), F32),
        compiler_params=pltpu.CompilerParams(
            dimension_semantics=("arbitrary", "arbitrary"), vmem_limit_bytes=VMEM_LIMIT),
        name="merge",
    )(att, y_f, y_b, bonus, gg, *([proj] * MERGE_PIECES), x, g_a[:, None, :], p['gn_w'].reshape(1, C),
      p['gn_b'].reshape(1, C), _head_ones(C, RW_HEAD), p['w_o_mla'].astype(BF16), p['w_o_rwkv'].astype(BF16),
      p['w_out'].astype(BF16))


PACK_CHUNKS = 4
TILE_ROWS = 8
MOE_BLOCK = 512


ROUTER_TILE = 1024
NEG_INF = float("-inf")
HI_MASK = 0xFFFF0000


def _pack_pair(lo, hi):
    lo_b = lax.bitcast_convert_type(lo.astype(BF16).astype(F32), jnp.uint32)
    hi_b = lax.bitcast_convert_type(hi.astype(BF16).astype(F32), jnp.uint32)
    return (lo_b >> 16) | (hi_b & jnp.uint32(HI_MASK))


def _unpack_pair(w):
    lo = lax.bitcast_convert_type(w << 16, F32)
    hi = lax.bitcast_convert_type(w & jnp.uint32(HI_MASK), F32)
    return lo.astype(BF16), hi.astype(BF16)


def _row_max(x):
    return jnp.max(x, axis=-1, keepdims=True)


def _first_index_of(x, value, lane_f):
    return jnp.min(jnp.where(x == value, lane_f, float(x.shape[-1])), axis=-1, keepdims=True)


def _router_kernel(x_ref, gain_ref, sh_ref, sc_ref, wh_ref, wm_ref, wl_ref, bias_ref,
                   u_ref, ut_ref, ids_ref, gates_ref, ranks_ref, counts_ref, carry_ref):
    @pl.when((pl.program_id(0) == 0) & (pl.program_id(1) == 0))
    def _():
        carry_ref[...] = jnp.zeros_like(carry_ref)

    x = x_ref[0]
    tm, D = x.shape
    E = bias_ref.shape[-1]
    u = x * lax.rsqrt(jnp.mean(x * x, axis=-1, keepdims=True) + EPS) * gain_ref[...]
    u = u * (1.0 + sc_ref[0]) + sh_ref[0]
    packed = _pack_pair(u[:, :D // 2], u[:, D // 2:])
    u_ref[0] = packed
    for c in range(PACK_CHUNKS):
        ut_ref[:, c, :] = packed[:, c * LANES:(c + 1) * LANES]

    uh, um, ul = _split3(u)
    wh, wm, wl = wh_ref[...], wm_ref[...], wl_ref[...]
    d = lambda a, b: jnp.dot(a, b, preferred_element_type=F32)
    logits = d(uh, wh) + (d(uh, wm) + d(um, wh)) + (d(uh, wl) + d(um, wm) + d(ul, wh))
    scores = jax.nn.sigmoid(logits)
    sel = scores + bias_ref[...]

    lane_i = lax.broadcasted_iota(jnp.int32, (tm, E), 1)
    lane_f = lane_i.astype(F32)
    out_f = lax.broadcasted_iota(jnp.int32, (tm, LANES), 1).astype(F32)
    per_group = E // N_GROUPS
    grp_f = jnp.floor(lane_f * (1.0 / per_group))

    gs = jnp.full((tm, LANES), NEG_INF, F32)
    for g in range(N_GROUPS):
        sg = jnp.where(lane_i >= g * per_group, jnp.where(lane_i < (g + 1) * per_group, sel, NEG_INF), NEG_INF)
        m1 = _row_max(sg)
        i1 = _first_index_of(sg, m1, lane_f)
        m2 = _row_max(jnp.where(lane_f == i1, NEG_INF, sg))
        gs = jnp.where(out_f == g, m1 + m2, gs)

    allow = jnp.zeros((tm, E), F32)
    for _ in range(TOPK_GROUPS):
        m = _row_max(gs)
        i = _first_index_of(gs, m, out_f)
        gs = jnp.where(out_f == i, NEG_INF, gs)
        allow = jnp.where(grp_f == i, 1.0, allow)
    selm = jnp.where(allow > 0.0, sel, NEG_INF)

    ids = jnp.zeros((tm, LANES), F32)
    gts = jnp.zeros((tm, LANES), F32)
    member = jnp.zeros((tm, E), F32)
    idx_cols = []
    gsum = jnp.zeros((tm, 1), F32)
    for k in range(TOP_K):
        m = _row_max(selm)
        i = _first_index_of(selm, m, lane_f)
        hit = lane_f == i
        gk = jnp.sum(jnp.where(hit, scores, 0.0), axis=-1, keepdims=True)
        selm = jnp.where(hit, NEG_INF, selm)
        member = jnp.where(hit, 1.0, member)
        ids = jnp.where(out_f == k, i, ids)
        gts = jnp.where(out_f == k, gk, gts)
        idx_cols.append(i)
        gsum = gsum + gk
    gts = gts / gsum * ROUTED_SCALE

    r2 = lax.broadcasted_iota(jnp.int32, (tm, tm), 0)
    c2 = lax.broadcasted_iota(jnp.int32, (tm, tm), 1)
    before = jnp.where(r2 > c2, 1.0, 0.0).astype(BF16)
    mem_b = member.astype(BF16)
    carry = carry_ref[...]
    pos = carry + jnp.dot(before, mem_b, preferred_element_type=F32)
    rk = jnp.zeros((tm, LANES), F32)
    for k in range(TOP_K):
        rk = jnp.where(out_f == k, jnp.sum(jnp.where(lane_f == idx_cols[k], pos, 0.0), axis=-1, keepdims=True), rk)
    colsum = jnp.dot(jnp.ones((8, tm), BF16), mem_b, preferred_element_type=F32)[0:1]
    carry_ref[...] = carry + colsum
    counts_ref[...] = carry + colsum
    ids_ref[0] = ids.astype(jnp.int32)
    ranks_ref[0] = rk.astype(jnp.int32)
    gates_ref[0] = gts


def _router(x1, sh_m, sc_m, p):
    B, T, D = x1.shape
    E = N_EXPERTS
    wh, wm, wl = _split3(p['router_w'])
    row = lambda b, i: (b, i, 0)
    const = lambda b, i: (0, 0)
    vec = lambda b, i: (b, 0, 0)
    lane_out = lambda dt: jax.ShapeDtypeStruct((B, T, LANES), dt)
    return pl.pallas_call(
        _router_kernel,
        grid=(B, T // ROUTER_TILE),
        in_specs=[
            pl.BlockSpec((1, ROUTER_TILE, D), row),
            pl.BlockSpec((1, D), const),
            pl.BlockSpec((1, 1, D), vec),
            pl.BlockSpec((1, 1, D), vec),
            pl.BlockSpec((D, E), const),
            pl.BlockSpec((D, E), const),
            pl.BlockSpec((D, E), const),
            pl.BlockSpec((1, E), const),
        ],
        out_specs=[
            pl.BlockSpec((1, ROUTER_TILE, D // 2), row),
            pl.BlockSpec((ROUTER_TILE, PACK_CHUNKS, LANES), lambda b, i: (b * (T // ROUTER_TILE) + i, 0, 0)),
            pl.BlockSpec((1, ROUTER_TILE, LANES), row),
            pl.BlockSpec((1, ROUTER_TILE, LANES), row),
            pl.BlockSpec((1, ROUTER_TILE, LANES), row),
            pl.BlockSpec((1, E), const),
        ],
        out_shape=[
            jax.ShapeDtypeStruct((B, T, D // 2), jnp.uint32),
            jax.ShapeDtypeStruct((B * T, PACK_CHUNKS, LANES), jnp.uint32),
            lane_out(jnp.int32), lane_out(F32), lane_out(jnp.int32),
            jax.ShapeDtypeStruct((1, E), F32),
        ],
        scratch_shapes=[pltpu.VMEM((1, E), F32)],
        compiler_params=pltpu.CompilerParams(
            dimension_semantics=("arbitrary", "arbitrary"), vmem_limit_bytes=VMEM_LIMIT),
        name="router",
    )(x1, p['norm_ffn'].reshape(1, D), sh_m[:, None, :], sc_m[:, None, :], wh, wm, wl,
      p['router_bias'].reshape(1, E))


def _slot_kernel(ids_ref, ranks_ref, base_ref, o_ref):
    ids = ids_ref[...].astype(F32)
    tm = ids.shape[0]
    E = base_ref.shape[-1]
    lane_e = lax.broadcasted_iota(jnp.int32, (tm, E), 1).astype(F32)
    out_lane = lax.broadcasted_iota(jnp.int32, (tm, LANES), 1)
    first = jnp.zeros((tm, LANES), F32)
    for k in range(TOP_K):
        fk = jnp.sum(jnp.where(lane_e == ids[:, k:k + 1], base_ref[...], 0.0), axis=-1, keepdims=True)
        first = jnp.where(out_lane == k, fk, first)
    o_ref[...] = first.astype(jnp.int32) + ranks_ref[...]


def _slots(ids, ranks, base):
    n_tok = ids.shape[0]
    E = base.shape[-1]
    row = pl.BlockSpec((ROUTER_TILE, LANES), lambda i: (i, 0))
    return pl.pallas_call(
        _slot_kernel,
        grid=(n_tok // ROUTER_TILE,),
        in_specs=[row, row, pl.BlockSpec((1, E), lambda i: (0, 0))],
        out_specs=row,
        out_shape=jax.ShapeDtypeStruct((n_tok, LANES), jnp.int32),
        compiler_params=pltpu.CompilerParams(dimension_semantics=("arbitrary",)),
        name="slots",
    )(ids, ranks, base)


DMA_PRIORITIES = 2


def _dispatch_kernel(pad_ref, dest_ref, u_ref, slots_ref, zero_ref, sem):
    tm = u_ref.shape[0]
    n_experts = pad_ref.shape[0]

    @pl.when(pl.program_id(0) == 0)
    def _():
        zero_ref[...] = jnp.zeros_like(zero_ref)

        def fill(e, carry):
            pltpu.make_async_copy(zero_ref, slots_ref.at[pl.ds(pad_ref[e], MOE_BLOCK)], sem).start()
            return carry

        lax.fori_loop(0, n_experts, fill, 0)

        def filled(e, carry):
            pltpu.make_async_copy(zero_ref, slots_ref.at[pl.ds(0, MOE_BLOCK)], sem).wait()
            return carry

        lax.fori_loop(0, n_experts, filled, 0)

    def issue(r, carry):
        for k in range(TOP_K):
            pltpu.make_async_copy(u_ref.at[r], slots_ref.at[dest_ref[0, 0, r * TOP_K + k]], sem).start(
                priority=k % DMA_PRIORITIES)
        return carry

    lax.fori_loop(0, tm, issue, 0)
    for k in range(TOP_K):
        pltpu.make_async_copy(u_ref, slots_ref.at[pl.ds(0, tm)], sem).wait()


def _dispatch(pad_start, dest, u_tiles, n_slots):
    n_tok = u_tiles.shape[0]
    tile = u_tiles.shape[1:]
    nt = n_tok // ROW_TILE
    grid_spec = pltpu.PrefetchScalarGridSpec(
        num_scalar_prefetch=1,
        grid=(nt,),
        in_specs=[
            pl.BlockSpec((1, 1, ROW_TILE * TOP_K), lambda i, pad: (i, 0, 0), memory_space=pltpu.SMEM),
            pl.BlockSpec((ROW_TILE,) + tile, lambda i, pad: (i, 0, 0)),
        ],
        out_specs=pl.BlockSpec(memory_space=pl.ANY),
        scratch_shapes=[pltpu.VMEM((MOE_BLOCK,) + tile, jnp.uint32), pltpu.SemaphoreType.DMA(())],
    )
    return pl.pallas_call(
        _dispatch_kernel,
        grid_spec=grid_spec,
        out_shape=jax.ShapeDtypeStruct((n_slots + MOE_BLOCK,) + tile, jnp.uint32),
        compiler_params=pltpu.CompilerParams(
            dimension_semantics=("arbitrary",), vmem_limit_bytes=VMEM_LIMIT),
        name="dispatch",
    )(pad_start, dest, u_tiles)


WEIGHT_SLOTS = 2


def _expert_weight_copies(hbm_refs, stage_refs, expert, slot, sems):
    return [pltpu.make_async_copy(w.at[expert], s.at[slot], sems.at[slot, i])
            for i, (w, s) in enumerate(zip(hbm_refs, stage_refs))]


def _moe_ffn_kernel(be_ref, first_ref, slot_ref, next_ref, nu_ref, x_ref, w1_ref, w3_ref, w2_ref, o_ref,
                    s1_ref, s3_ref, s2_ref, w13_ref, w2b_ref, sems):
    j = pl.program_id(0)
    F = w1_ref.shape[-1]
    half = PACK_CHUNKS * LANES
    copies = functools.partial(_expert_weight_copies, (w1_ref, w3_ref, w2_ref), (s1_ref, s3_ref, s2_ref), sems=sems)

    @pl.when(j == 0)
    def _():
        for c in copies(be_ref[0], 0):
            c.start()

    @pl.when(first_ref[j] == 1)
    def _():
        slot = slot_ref[j]
        for c in copies(be_ref[j], slot):
            c.wait()

        @pl.when(next_ref[j] >= 0)
        def _():
            for c in copies(next_ref[j], 1 - slot):
                c.start()

        w13_ref[:, :F] = s1_ref[slot].astype(BF16)
        w13_ref[:, F:] = s3_ref[slot].astype(BF16)
        w2b_ref[...] = s2_ref[slot].astype(BF16)

    @pl.when(j < nu_ref[0])
    def _():
        lo, hi = _unpack_pair(jnp.concatenate([x_ref[:, c, :] for c in range(PACK_CHUNKS)], axis=1))
        h = (jnp.dot(lo, w13_ref[:half], preferred_element_type=F32)
             + jnp.dot(hi, w13_ref[half:], preferred_element_type=F32))
        h1, h3 = h[:, :F], h[:, F:]
        y = _dot(h1 * jax.nn.sigmoid(h1) * h3, w2b_ref[...])
        o_ref[...] = pltpu.einshape("s(cl)->(sc)l", y, c=TILE_ROWS)

    @pl.when(j >= nu_ref[0])
    def _():
        o_ref[...] = jnp.zeros_like(o_ref)


def _moe_ffn(block_e, n_used, x_slots, w1, w3, w2):
    E, D, F = w1.shape
    nblk = block_e.shape[0]
    slots = nblk * MOE_BLOCK
    idx = jnp.arange(nblk)
    first = jnp.concatenate([jnp.ones((1,), bool), block_e[1:] != block_e[:-1]])
    slot = ((jnp.cumsum(first) - 1) % WEIGHT_SLOTS).astype(jnp.int32)
    later_first = lax.cummin(jnp.where(first, idx, nblk)[::-1])[::-1]
    next_first = jnp.concatenate([later_first[1:], jnp.full((1,), nblk)])
    next_e = jnp.where(next_first < nblk, block_e[jnp.minimum(next_first, nblk - 1)], -1).astype(jnp.int32)
    blk = lambda j, be, first, slot, nxt, nu: (jnp.minimum(j, nu[0] - 1), 0, 0)
    grid_spec = pltpu.PrefetchScalarGridSpec(
        num_scalar_prefetch=5,
        grid=(nblk,),
        in_specs=[
            pl.BlockSpec((MOE_BLOCK, PACK_CHUNKS, LANES), blk),
            pl.BlockSpec(memory_space=pl.ANY),
            pl.BlockSpec(memory_space=pl.ANY),
            pl.BlockSpec(memory_space=pl.ANY),
        ],
        out_specs=pl.BlockSpec((MOE_BLOCK * TILE_ROWS, LANES), lambda j, *_: (j, 0)),
        scratch_shapes=[pltpu.VMEM((WEIGHT_SLOTS, D, F), F32), pltpu.VMEM((WEIGHT_SLOTS, D, F), F32),
                        pltpu.VMEM((WEIGHT_SLOTS, F, D), F32),
                        pltpu.VMEM((D, 2 * F), BF16), pltpu.VMEM((F, D), BF16),
                        pltpu.SemaphoreType.DMA((WEIGHT_SLOTS, 3))],
    )
    return pl.pallas_call(
        _moe_ffn_kernel,
        grid_spec=grid_spec,
        out_shape=jax.ShapeDtypeStruct((slots * TILE_ROWS, LANES), F32),
        compiler_params=pltpu.CompilerParams(
            dimension_semantics=("arbitrary",), vmem_limit_bytes=VMEM_LIMIT),
        name="moe_ffn",
    )(block_e, first.astype(jnp.int32), slot, next_e, n_used, x_slots, w1, w3, w2)


COMBINE_ROWS = 32


GATHER_SLOTS = 2


def _combine_kernel(dest_ref, next_dest_ref, gates_ref, u_ref, x_ref, gm_ref, w1_ref, w3_ref, w2_ref, ys_ref, o_ref,
                    buf_ref, sems):
    i = pl.program_id(0)
    tm, half = u_ref.shape

    def tile(j):
        return pl.ds(pl.multiple_of(j * TILE_ROWS, TILE_ROWS), TILE_ROWS)

    def gather(d_ref, slot):
        def issue(r, carry):
            for k in range(TOP_K):
                pltpu.make_async_copy(ys_ref.at[tile(d_ref[0, 0, r * TOP_K + k])], buf_ref.at[slot, k, tile(r)],
                                      sems.at[slot]).start(priority=k % DMA_PRIORITIES)
            return carry

        lax.fori_loop(0, tm, issue, 0)

    def accumulate(slot):
        for k in range(TOP_K):
            pltpu.make_async_copy(ys_ref.at[pl.ds(0, tm * TILE_ROWS)], buf_ref.at[slot, k], sems.at[slot]).wait()
        gm = gm_ref[0]
        for rb in range(tm // COMBINE_ROWS):
            rows = slice(rb * COMBINE_ROWS, (rb + 1) * COMBINE_ROWS)
            gates = gates_ref[rows, :]
            for c in range(TILE_ROWS):
                acc = jnp.zeros((COMBINE_ROWS, LANES), F32)
                for k in range(TOP_K):
                    acc = acc + gates[:, k:k + 1] * buf_ref[
                        slot, k, pl.ds(rb * COMBINE_ROWS * TILE_ROWS + c, COMBINE_ROWS, stride=TILE_ROWS), :]
                cols = slice(c * LANES, (c + 1) * LANES)
                o_ref[rows, cols] = o_ref[rows, cols] + gm[:, cols] * acc

    @pl.when(i == 0)
    def _():
        gather(dest_ref, 0)

    ulo, uhi = _unpack_pair(u_ref[...])
    both = lambda w_ref: (jnp.dot(ulo, w_ref[:half], preferred_element_type=F32)
                          + jnp.dot(uhi, w_ref[half:], preferred_element_type=F32))
    h1, h3 = both(w1_ref), both(w3_ref)
    o_ref[...] = x_ref[...] + gm_ref[0] * _dot(h1 * jax.nn.sigmoid(h1) * h3, w2_ref[...])

    for slot in range(GATHER_SLOTS):
        @pl.when(i % GATHER_SLOTS == slot)
        def _(slot=slot):
            @pl.when(i + 1 < pl.num_programs(0))
            def _():
                gather(next_dest_ref, (slot + 1) % GATHER_SLOTS)

            accumulate(slot)


def _combine(dest, gates, u_rows, x1, g_m, y_slots, p):
    B, T, D = x1.shape
    n_tok = B * T
    W = u_rows.shape[1]
    tiles_per_batch = T // ROW_TILE
    row = lambda i: (i, 0)
    const = lambda i: (0, 0)
    nt = n_tok // ROW_TILE
    out = pl.pallas_call(
        _combine_kernel,
        grid=(nt,),
        in_specs=[
            pl.BlockSpec((1, 1, ROW_TILE * TOP_K), lambda i: (i, 0, 0), memory_space=pltpu.SMEM),
            pl.BlockSpec((1, 1, ROW_TILE * TOP_K), lambda i: (jnp.minimum(i + 1, nt - 1), 0, 0),
                         memory_space=pltpu.SMEM),
            pl.BlockSpec((ROW_TILE, LANES), row),
            pl.BlockSpec((ROW_TILE, W), row),
            pl.BlockSpec((ROW_TILE, D), row),
            pl.BlockSpec((1, 1, D), lambda i: (i // tiles_per_batch, 0, 0)),
            pl.BlockSpec(p['shared_w1'].shape, const),
            pl.BlockSpec(p['shared_w3'].shape, const),
            pl.BlockSpec(p['shared_w2'].shape, const),
            pl.BlockSpec(memory_space=pl.ANY),
        ],
        out_specs=pl.BlockSpec((ROW_TILE, D), row),
        out_shape=jax.ShapeDtypeStruct((n_tok, D), F32),
        scratch_shapes=[pltpu.VMEM((GATHER_SLOTS, TOP_K, ROW_TILE * TILE_ROWS, LANES), F32),
                        pltpu.SemaphoreType.DMA((GATHER_SLOTS,))],
        compiler_params=pltpu.CompilerParams(
            dimension_semantics=("arbitrary",), vmem_limit_bytes=VMEM_LIMIT),
        name="combine",
    )(dest, dest, gates.reshape(n_tok, LANES), u_rows, x1.reshape(n_tok, D), g_m[:, None, :],
      p['shared_w1'].astype(BF16), p['shared_w3'].astype(BF16), p['shared_w2'].astype(BF16), y_slots)
    return out.reshape(B, T, D)


def _moe(x1, sh_m, sc_m, g_m, p):
    B, T, D = x1.shape
    n_tok = B * T
    u_packed, u_tiles, ids, gates, ranks, counts = _router(x1, sh_m, sc_m, p)

    counts = counts[0].astype(jnp.int32)
    padded = (counts + MOE_BLOCK - 1) // MOE_BLOCK * MOE_BLOCK
    padded_end = jnp.cumsum(padded)
    base = padded_end - padded
    n_blocks = n_tok * TOP_K // MOE_BLOCK + N_EXPERTS
    n_used = (padded_end[-1] // MOE_BLOCK).astype(jnp.int32)
    blk = jnp.minimum(jnp.arange(n_blocks), n_used - 1) * MOE_BLOCK
    block_e = jnp.minimum(jnp.sum(padded_end[None, :] <= blk[:, None], axis=1), N_EXPERTS - 1).astype(jnp.int32)
    dest = _slots(ids.reshape(n_tok, LANES), ranks.reshape(n_tok, LANES), base.astype(F32).reshape(1, N_EXPERTS))
    dest = dest[:, :TOP_K].reshape(n_tok // ROW_TILE, 1, ROW_TILE * TOP_K)

    u_rows = u_packed.reshape(n_tok, D // 2)
    x_slots = _dispatch((base + counts).astype(jnp.int32), dest, u_tiles, n_blocks * MOE_BLOCK)
    y_slots = _moe_ffn(block_e, n_used.reshape(1), x_slots, p['expert_w1'], p['expert_w3'], p['expert_w2'])
    return _combine(dest, gates, u_rows, x1, g_m, y_slots, p)
```
